```python
import jax, jax.numpy as jnp
from jax import lax
import numpy as np

D_MODEL = 1024
BATCH = 2
SEQ = 8192
DEPTH = 1

ATT_HEADS = 8
ATT_KV_HEADS = 2
HEAD_DIM = 64
ATT_WIDTH = ATT_HEADS * HEAD_DIM
KV_WIDTH = ATT_KV_HEADS * HEAD_DIM
WINDOW = 128
ATT_BLOCK = 128
ROPE_DIM = HEAD_DIM // 4
ROPE_THETA = 500000.0
SSD_HEADS = 8
SSD_HEAD_DIM = 64
SSD_WIDTH = SSD_HEADS * SSD_HEAD_DIM
SSD_GROUPS = 2
SSD_STATE = 128
CONV_K = 4
CHUNK = 128
XBC_WIDTH = SSD_WIDTH + 2 * SSD_GROUPS * SSD_STATE
MIX_WIDTH = ATT_WIDTH + SSD_WIDTH
IN_WIDTH = ATT_WIDTH + 2 * KV_WIDTH + SSD_WIDTH + XBC_WIDTH + SSD_HEADS
N_GROUPS = 4
EXPERTS_PER_GROUP = 8
N_EXPERTS = N_GROUPS * EXPERTS_PER_GROUP
TOP_K = 2
D_EXPERT = 256
EPS = 1e-6

kernel_name = "hymba_ssd_swa_sink_hmoe_adaln_layer"


def rms_norm(x, w):
    xf = x.astype(jnp.float32)
    y = xf * lax.rsqrt(jnp.mean(xf * xf, axis=-1, keepdims=True) + EPS)
    return (y * w.astype(jnp.float32)).astype(x.dtype)


def partial_rope(x, positions):
    half = ROPE_DIM // 2
    inv_freq = jnp.power(ROPE_THETA, -jnp.arange(half, dtype=jnp.float32) * 2.0 / ROPE_DIM)
    ang = positions.astype(jnp.float32)[..., None] * inv_freq
    cos = jnp.cos(ang)[:, :, None, :]
    sin = jnp.sin(ang)[:, :, None, :]
    xr = x[..., :ROPE_DIM].astype(jnp.float32)
    x1, x2 = xr[..., :half], xr[..., half:]
    rot = jnp.concatenate([x1 * cos - x2 * sin, x2 * cos + x1 * sin], axis=-1).astype(x.dtype)
    return jnp.concatenate([rot, x[..., ROPE_DIM:]], axis=-1)


def sliding_window_attention(q, k, v, sinks):
    b, s = q.shape[:2]
    nb = s // ATT_BLOCK
    grp = ATT_HEADS // ATT_KV_HEADS
    qb = q.reshape(b, nb, ATT_BLOCK, ATT_KV_HEADS, grp, HEAD_DIM)

    def band(t):
        tb = t.reshape(b, nb, ATT_BLOCK, ATT_KV_HEADS, HEAD_DIM)
        prev = jnp.pad(tb[:, :-1], ((0, 0), (1, 0), (0, 0), (0, 0), (0, 0)))
        return jnp.concatenate([prev, tb], axis=2)

    kb, vb = band(k), band(v)
    scores = jnp.einsum('bnqkgd,bnskd->bnkgqs', qb, kb).astype(jnp.float32) * (HEAD_DIM ** -0.5)
    qi = jnp.arange(ATT_BLOCK)[:, None]
    si = jnp.arange(2 * ATT_BLOCK)[None, :]
    diff = qi + ATT_BLOCK - si
    band_mask = (diff >= 0) & (diff < WINDOW)
    blk = jnp.arange(nb)[:, None, None]
    valid = band_mask[None] & ((blk > 0) | (si[None] >= ATT_BLOCK))
    scores = jnp.where(valid[None, :, None, None], scores, -jnp.inf)
    sink = sinks.astype(jnp.float32).reshape(ATT_KV_HEADS, grp)[None, None, :, :, None, None]
    m = jnp.maximum(jnp.max(scores, axis=-1, keepdims=True), sink)
    p = jnp.exp(scores - m)
    probs = (p / (jnp.sum(p, axis=-1, keepdims=True) + jnp.exp(sink - m))).astype(v.dtype)
    out = jnp.einsum('bnkgqs,bnskd->bnqkgd', probs, vb)
    return out.reshape(b, s, ATT_WIDTH)


def causal_depthwise_conv(x, w, bias):
    ch = x.shape[-1]
    y = lax.conv_general_dilated(x, w[:, None, :], window_strides=(1,),
                                 padding=[(CONV_K - 1, 0)],
                                 dimension_numbers=('NWC', 'WIO', 'NWC'),
                                 feature_group_count=ch)
    return y + bias


def ssd_scan(x, dt, A, B, C):
    b, s, h, p = x.shape
    n = B.shape[-1]
    nc = s // CHUNK
    hpg = h // SSD_GROUPS
    f32 = jnp.float32
    xdt = (x.astype(f32) * dt[..., None]).reshape(b, nc, CHUNK, h, p)
    a = (dt * A).reshape(b, nc, CHUNK, h)
    a_cum = jnp.cumsum(a, axis=2)
    Bc = B.astype(f32).reshape(b, nc, CHUNK, SSD_GROUPS, n)
    Cc = C.astype(f32).reshape(b, nc, CHUNK, SSD_GROUPS, n)
    seg = a_cum[:, :, :, None, :] - a_cum[:, :, None, :, :]
    causal = jnp.tril(jnp.ones((CHUNK, CHUNK), dtype=bool))
    decay = jnp.exp(jnp.where(causal[None, None, :, :, None], seg, -jnp.inf))
    cb = jnp.repeat(jnp.einsum('bclgn,bcsgn->bclsg', Cc, Bc), hpg, axis=-1)
    y_diag = jnp.einsum('bclsh,bcshp->bclhp', cb * decay, xdt)
    B_h = jnp.repeat(Bc, hpg, axis=3)
    decay_to_end = jnp.exp(a_cum[:, :, -1:, :] - a_cum)
    states = jnp.einsum('bclhn,bclh,bclhp->bchpn', B_h, decay_to_end, xdt)
    chunk_decay = jnp.exp(a_cum[:, :, -1, :])

    def step(carry, inp):
        st, dec = inp
        return carry * dec[:, :, None, None] + st, carry

    init = jnp.zeros((b, h, p, n), f32)
    _, prev = lax.scan(step, init, (jnp.moveaxis(states, 1, 0), jnp.moveaxis(chunk_decay, 1, 0)))
    prev = jnp.moveaxis(prev, 0, 1)
    C_h = jnp.repeat(Cc, hpg, axis=3)
    y_off = jnp.einsum('bclhn,bchpn,bclh->bclhp', C_h, prev, jnp.exp(a_cum))
    return (y_diag + y_off).reshape(b, s, h, p).astype(x.dtype)


def hybrid_mixer(h, positions, w_in, conv_w, conv_b, dt_bias, a_log, d_skip, ssd_norm_w,
                 q_norm_w, k_norm_w, sinks, w_out):
    b, s, _ = h.shape
    proj = h @ w_in
    cuts = [ATT_WIDTH, ATT_WIDTH + KV_WIDTH, ATT_WIDTH + 2 * KV_WIDTH,
            ATT_WIDTH + 2 * KV_WIDTH + SSD_WIDTH,
            ATT_WIDTH + 2 * KV_WIDTH + SSD_WIDTH + XBC_WIDTH]
    q, k, v, z, xbc, dt = jnp.split(proj, cuts, axis=-1)
    q = partial_rope(rms_norm(q.reshape(b, s, ATT_HEADS, HEAD_DIM), q_norm_w), positions)
    k = partial_rope(rms_norm(k.reshape(b, s, ATT_KV_HEADS, HEAD_DIM), k_norm_w), positions)
    v = v.reshape(b, s, ATT_KV_HEADS, HEAD_DIM)
    att = sliding_window_attention(q, k, v, sinks)
    xbc = jax.nn.silu(causal_depthwise_conv(xbc, conv_w, conv_b))
    xs, Bm, Cm = jnp.split(xbc, [SSD_WIDTH, SSD_WIDTH + SSD_GROUPS * SSD_STATE], axis=-1)
    xs = xs.reshape(b, s, SSD_HEADS, SSD_HEAD_DIM)
    Bm = Bm.reshape(b, s, SSD_GROUPS, SSD_STATE)
    Cm = Cm.reshape(b, s, SSD_GROUPS, SSD_STATE)
    dt = jax.nn.softplus(dt.astype(jnp.float32) + dt_bias.astype(jnp.float32))
    A = -jnp.exp(a_log.astype(jnp.float32))
    y = ssd_scan(xs, dt, A, Bm, Cm) + d_skip[:, None] * xs
    y = y.reshape(b, s, SSD_WIDTH) * jax.nn.silu(z)
    y = rms_norm(y.reshape(b, s, SSD_GROUPS, SSD_WIDTH // SSD_GROUPS),
                 ssd_norm_w.reshape(SSD_GROUPS, SSD_WIDTH // SSD_GROUPS)).reshape(b, s, SSD_WIDTH)
    return jnp.concatenate([att, y], axis=-1) @ w_out


def hierarchical_moe(h, w_group, b_group, w_expert, b_expert, w_gate, w_up, w_down):
    b, s, d = h.shape
    t = h.reshape(b * s, d)
    g_probs = jax.nn.softmax((t @ w_group + b_group).astype(jnp.float32), axis=-1)
    g_p, g_idx = lax.top_k(g_probs, 1)
    e_logits = (t @ w_expert + b_expert).astype(jnp.float32).reshape(-1, N_GROUPS, EXPERTS_PER_GROUP)
    e_sel = jnp.take_along_axis(e_logits, g_idx[:, :, None], axis=1)[:, 0]
    e_p, e_idx = lax.top_k(jax.nn.softmax(e_sel, axis=-1), TOP_K)
    e_p = e_p / jnp.sum(e_p, axis=-1, keepdims=True)
    group_w = jax.nn.one_hot(g_idx[:, 0], N_GROUPS, dtype=jnp.float32) * g_p
    exp_w = jnp.einsum('tk,tke->te', e_p, jax.nn.one_hot(e_idx, EXPERTS_PER_GROUP, dtype=jnp.float32))
    combine = (group_w[:, :, None] * exp_w[:, None, :]).reshape(-1, N_EXPERTS).astype(h.dtype)
    hg = jnp.einsum('td,edf->tef', t, w_gate)
    hu = jnp.einsum('td,edf->tef', t, w_up)
    act = jax.nn.silu(hg) * hu * combine[:, :, None]
    y = jnp.einsum('tef,efd->td', act, w_down)
    return y.reshape(b, s, d)


def setup_inputs(seed: int = 0) -> dict:
    key = jax.random.key(seed)
    ks = jax.random.split(key, 26)
    f32 = jnp.float32
    nrm = lambda k, shape, scale: jax.random.normal(k, shape, f32) * scale
    dt0 = jnp.exp(jax.random.uniform(ks[9], (SSD_HEADS,), f32, np.log(1e-3), np.log(1e-1)))
    return {
        "x": nrm(ks[0], (BATCH, SEQ, D_MODEL), 1.0),
        "c": nrm(ks[1], (BATCH, D_MODEL), 1.0),
        "positions": jnp.broadcast_to(jnp.arange(SEQ, dtype=jnp.int32)[None, :], (BATCH, SEQ)),
        "norm1_w": 1.0 + nrm(ks[2], (D_MODEL,), 0.02),
        "norm2_w": 1.0 + nrm(ks[3], (D_MODEL,), 0.02),
        "w_ada": nrm(ks[4], (D_MODEL, 6 * D_MODEL), 0.5 * D_MODEL ** -0.5),
        "b_ada": nrm(ks[5], (6 * D_MODEL,), 0.02),
        "w_in": nrm(ks[6], (D_MODEL, IN_WIDTH), D_MODEL ** -0.5),
        "conv_w": nrm(ks[7], (CONV_K, XBC_WIDTH), CONV_K ** -0.5),
        "conv_b": nrm(ks[8], (XBC_WIDTH,), 0.02),
        "dt_bias": dt0 + jnp.log(-jnp.expm1(-dt0)),
        "a_log": jnp.log(jax.random.uniform(ks[10], (SSD_HEADS,), f32, 1.0, 16.0)),
        "d_skip": 1.0 + nrm(ks[11], (SSD_HEADS,), 0.1),
        "ssd_norm_w": 1.0 + nrm(ks[12], (SSD_WIDTH,), 0.02),
        "q_norm_w": 1.0 + nrm(ks[13], (HEAD_DIM,), 0.02),
        "k_norm_w": 1.0 + nrm(ks[14], (HEAD_DIM,), 0.02),
        "sinks": nrm(ks[15], (ATT_HEADS,), 0.5),
        "w_out": nrm(ks[16], (MIX_WIDTH, D_MODEL), MIX_WIDTH ** -0.5),
        "w_group": nrm(ks[17], (D_MODEL, N_GROUPS), D_MODEL ** -0.5),
        "b_group": nrm(ks[18], (N_GROUPS,), 0.01),
        "w_expert": nrm(ks[19], (D_MODEL, N_EXPERTS), D_MODEL ** -0.5),
        "b_expert": nrm(ks[20], (N_EXPERTS,), 0.01),
        "w_gate": nrm(ks[21], (N_EXPERTS, D_MODEL, D_EXPERT), D_MODEL ** -0.5),
        "w_up": nrm(ks[22], (N_EXPERTS, D_MODEL, D_EXPERT), D_MODEL ** -0.5),
        "w_down": nrm(ks[23], (N_EXPERTS, D_EXPERT, D_MODEL), D_EXPERT ** -0.5),
    }


def reference(x, c, positions, norm1_w, norm2_w, w_ada, b_ada, w_in, conv_w, conv_b,
              dt_bias, a_log, d_skip, ssd_norm_w, q_norm_w, k_norm_w, sinks, w_out,
              w_group, b_group, w_expert, b_expert, w_gate, w_up, w_down):
    mod = jax.nn.silu(c) @ w_ada + b_ada
    shift1, scale1, gate1, shift2, scale2, gate2 = [m[:, None, :] for m in jnp.split(mod, 6, axis=-1)]
    for _ in range(DEPTH):
        h = rms_norm(x, norm1_w) * (1.0 + scale1) + shift1
        x = x + gate1 * hybrid_mixer(h, positions, w_in, conv_w, conv_b, dt_bias, a_log, d_skip,
                                     ssd_norm_w, q_norm_w, k_norm_w, sinks, w_out)
        h = rms_norm(x, norm2_w) * (1.0 + scale2) + shift2
        x = x + gate2 * hierarchical_moe(h, w_group, b_group, w_expert, b_expert, w_gate, w_up, w_down)
    return x
```

```python
import jax
import jax.numpy as jnp
from jax import lax
from jax.experimental import pallas as pl
from jax.experimental.pallas import tpu as pltpu

F32 = jnp.float32
BF16 = jnp.bfloat16
I32 = jnp.int32

D_MODEL = 1024
BATCH = 2
SEQ = 8192
TOKENS = BATCH * SEQ
ATT_HEADS = 8
ATT_KV_HEADS = 2
HEAD_DIM = 64
ATT_WIDTH = ATT_HEADS * HEAD_DIM
KV_WIDTH = ATT_KV_HEADS * HEAD_DIM
ATT_BLOCK = 128
ROPE_DIM = HEAD_DIM // 4
ROPE_THETA = 500000.0
SSD_HEADS = 8
SSD_HEAD_DIM = 64
SSD_WIDTH = SSD_HEADS * SSD_HEAD_DIM
SSD_GROUPS = 2
SSD_STATE = 128
CONV_K = 4
CHUNK = 128
XBC_WIDTH = SSD_WIDTH + 2 * SSD_GROUPS * SSD_STATE
IN_WIDTH = ATT_WIDTH + 2 * KV_WIDTH + SSD_WIDTH + XBC_WIDTH + SSD_HEADS
N_GROUPS = 4
EXPERTS_PER_GROUP = 8
N_EXPERTS = N_GROUPS * EXPERTS_PER_GROUP
TOP_K = 2
D_EXPERT = 256
EPS = 1e-6

LANES = 128
QKV_WIDTH = ATT_WIDTH + 2 * KV_WIDTH
IN_PAD = QKV_WIDTH + SSD_WIDTH + XBC_WIDTH + LANES
NEG_BIG = -1e30

VMEM_LIMIT = 48 * 1024 * 1024


def _cparams(sem):
    return pltpu.CompilerParams(dimension_semantics=sem, vmem_limit_bytes=VMEM_LIMIT)


def _split_bf16(x):
    hi = x.astype(BF16)
    lo = (x - hi.astype(F32)).astype(BF16)
    return hi, lo


ADA_TN = 768


def _ada_kernel(ct_ref, w_ref, b_ref, o_ref):
    ct = ct_ref[...]
    s = ct * jax.nn.sigmoid(ct)
    w = w_ref[...]
    rows = [jnp.sum(s[:, b:b + 1] * w, axis=0, keepdims=True) for b in range(BATCH)]
    o_ref[...] = jnp.concatenate(rows, axis=0) + b_ref[...]


def _ada_mod(c, w_ada, b_ada):
    n = w_ada.shape[1]
    return pl.pallas_call(
        _ada_kernel,
        grid=(n // ADA_TN,),
        in_specs=[pl.BlockSpec((D_MODEL, BATCH), lambda j: (0, 0)),
                  pl.BlockSpec((D_MODEL, ADA_TN), lambda j: (0, j)),
                  pl.BlockSpec((1, ADA_TN), lambda j: (0, j))],
        out_specs=pl.BlockSpec((BATCH, ADA_TN), lambda j: (0, j)),
        out_shape=jax.ShapeDtypeStruct((BATCH, n), F32),
        compiler_params=_cparams(("arbitrary",)),
        name="ada_mod",
    )(c.T, w_ada, b_ada.reshape(1, n))


INPROJ_TM = 256
_INPROJ_CHUNK = 256


def _inproj_kernel(x_ref, nw_ref, sc_ref, sh_ref, w_ref, qkv_ref, z_ref, xbc_ref, dt_ref):
    x = x_ref[...]
    y = x * lax.rsqrt(jnp.mean(x * x, axis=-1, keepdims=True) + EPS)
    h = (y * nw_ref[...]) * (1.0 + sc_ref[0]) + sh_ref[0]
    hb = h.astype(BF16)

    def proj(c0, c1):
        return jnp.dot(hb, w_ref[:, c0:c1], preferred_element_type=F32)

    for c0 in range(0, QKV_WIDTH, _INPROJ_CHUNK):
        qkv_ref[:, c0:c0 + _INPROJ_CHUNK] = proj(c0, c0 + _INPROJ_CHUNK).astype(BF16)
    base = QKV_WIDTH
    for c0 in range(0, SSD_WIDTH, _INPROJ_CHUNK):
        z_ref[:, c0:c0 + _INPROJ_CHUNK] = proj(base + c0, base + c0 + _INPROJ_CHUNK).astype(BF16)
    base += SSD_WIDTH
    for c0 in range(0, XBC_WIDTH, _INPROJ_CHUNK):
        xbc_ref[:, c0:c0 + _INPROJ_CHUNK] = proj(base + c0, base + c0 + _INPROJ_CHUNK).astype(BF16)
    base += XBC_WIDTH
    dt_ref[...] = proj(base, base + LANES)


def _in_proj(x2d, norm_w, mod3, w_in_pad):
    tm = INPROJ_TM
    steps_per_batch = SEQ // tm
    return pl.pallas_call(
        _inproj_kernel,
        grid=(TOKENS // tm,),
        in_specs=[pl.BlockSpec((tm, D_MODEL), lambda i: (i, 0)),
                  pl.BlockSpec((1, D_MODEL), lambda i: (0, 0)),
                  pl.BlockSpec((1, 1, D_MODEL), lambda i: ((i // steps_per_batch) * 6 + 1, 0, 0)),
                  pl.BlockSpec((1, 1, D_MODEL), lambda i: ((i // steps_per_batch) * 6 + 0, 0, 0)),
                  pl.BlockSpec((D_MODEL, IN_PAD), lambda i: (0, 0))],
        out_specs=[pl.BlockSpec((tm, QKV_WIDTH), lambda i: (i, 0)),
                   pl.BlockSpec((tm, SSD_WIDTH), lambda i: (i, 0)),
                   pl.BlockSpec((tm, XBC_WIDTH), lambda i: (i, 0)),
                   pl.BlockSpec((tm, LANES), lambda i: (i, 0))],
        out_shape=[jax.ShapeDtypeStruct((TOKENS, QKV_WIDTH), BF16),
                   jax.ShapeDtypeStruct((TOKENS, SSD_WIDTH), BF16),
                   jax.ShapeDtypeStruct((TOKENS, XBC_WIDTH), BF16),
                   jax.ShapeDtypeStruct((TOKENS, LANES), F32)],
        compiler_params=_cparams(("arbitrary",)),
        name="in_proj",
    )(x2d, norm_w.reshape(1, D_MODEL), mod3, mod3, w_in_pad)


def _seg_meansq(xf, ones_bd):
    hi, lo = _split_bf16(xf * xf)
    tot = (jnp.dot(hi, ones_bd, preferred_element_type=F32)
           + jnp.dot(lo, ones_bd, preferred_element_type=F32))
    return tot * (1.0 / HEAD_DIM)


def _norm_rope(x_bf, w_row, ones_bd, cosf, s1, s2):
    xf = x_bf.astype(F32)
    width = xf.shape[1]
    xn = xf * lax.rsqrt(_seg_meansq(xf, ones_bd) + EPS) * w_row
    half = ROPE_DIM // 2
    up = pltpu.roll(xn, width - half, axis=1)
    down = pltpu.roll(xn, half, axis=1)
    return xn * cosf + up * s1 + down * s2


def _attn_kernel(sink_ref, pos_ref, q_ref, kv_ref, qw_ref, kw_ref, freq_ref, m1_ref, m2_ref,
                 ones_ref, o_ref, kprev_ref, vprev_ref):
    j = pl.program_id(1)
    blk = ATT_BLOCK

    @pl.when(j == 0)
    def _():
        kprev_ref[...] = jnp.zeros_like(kprev_ref)
        vprev_ref[...] = jnp.zeros_like(vprev_ref)

    ang = pos_ref[...].astype(F32) * freq_ref[...]
    cos1 = jnp.cos(ang)
    sin1 = jnp.sin(ang)
    s1_1 = -sin1 * m1_ref[...]
    s2_1 = sin1 * m2_ref[...]
    reps = ATT_WIDTH // LANES
    cosq = jnp.concatenate([cos1] * reps, axis=1)
    s1q = jnp.concatenate([s1_1] * reps, axis=1)
    s2q = jnp.concatenate([s2_1] * reps, axis=1)

    q = _norm_rope(q_ref[...], qw_ref[...], ones_ref[...], cosq, s1q, s2q)
    qb = (q * (HEAD_DIM ** -0.5)).astype(BF16)
    kv = kv_ref[...]
    kn = _norm_rope(kv[:, 0:KV_WIDTH], kw_ref[...], ones_ref[0:KV_WIDTH, 0:KV_WIDTH],
                    cos1, s1_1, s2_1)
    vn = kv[:, KV_WIDTH:2 * KV_WIDTH].astype(F32)

    kcat = jnp.concatenate([kprev_ref[...], kn], axis=0)
    vcat = jnp.concatenate([vprev_ref[...], vn], axis=0)
    kprev_ref[...] = kn
    vprev_ref[...] = vn

    lane = lax.broadcasted_iota(I32, (2 * blk, LANES), 1)
    lo_half = lane < HEAD_DIM
    zero = jnp.zeros_like(kcat)

    def variants(t, g):
        own = jnp.where(lo_half if g == 0 else ~lo_half, t, zero)
        other = pltpu.roll(own, HEAD_DIM, axis=1)
        lo, hi = (own, other) if g == 0 else (other, own)
        return lo.astype(BF16), hi.astype(BF16)

    row = lax.broadcasted_iota(I32, (2 * blk, 2 * blk), 0)
    col = lax.broadcasted_iota(I32, (2 * blk, 2 * blk), 1)
    qi = jnp.where(row >= blk, row - blk, row)
    no_prev = jnp.where(j > 0, 0, 2 * blk)
    valid = ((col < blk) & (col > qi + no_prev)) | ((col >= blk) & ((col - blk) <= qi))
    bias = jnp.where(valid, 0.0, NEG_BIG).astype(F32)
    first = lax.broadcasted_iota(I32, (2 * blk, 1), 0) < blk

    def probs(qcat, kvar, head_a, head_b):
        s = lax.dot_general(qcat, kvar, (((1,), (1,)), ((), ())), preferred_element_type=F32) + bias
        sink = jnp.where(first, sink_ref[head_a], sink_ref[head_b])
        m = jnp.maximum(jnp.max(s, axis=-1, keepdims=True), sink)
        p = jnp.exp(s - m)
        denom = jnp.sum(p, axis=-1, keepdims=True) + jnp.exp(sink - m)
        return (p * (1.0 / denom)).astype(BF16)

    for g in range(ATT_KV_HEADS):
        k_lo, k_hi = variants(kcat, g)
        v_lo, v_hi = variants(vcat, g)
        c0 = g * 2 * LANES
        qcat = jnp.concatenate([qb[:, c0:c0 + LANES], qb[:, c0 + LANES:c0 + 2 * LANES]], axis=0)
        h0 = 4 * g
        p_lo = probs(qcat, k_lo, h0 + 0, h0 + 2)
        p_hi = probs(qcat, k_hi, h0 + 1, h0 + 3)
        out = (jnp.dot(p_lo, v_lo, preferred_element_type=F32)
               + jnp.dot(p_hi, v_hi, preferred_element_type=F32))
        o_ref[:, c0:c0 + LANES] = out[0:blk].astype(BF16)
        o_ref[:, c0 + LANES:c0 + 2 * LANES] = out[blk:2 * blk].astype(BF16)


def _rope_tables():
    half = ROPE_DIM // 2
    lane = jnp.arange(LANES)
    d = lane % HEAD_DIM
    inv_freq = jnp.power(ROPE_THETA, -(d % half).astype(F32) * 2.0 / ROPE_DIM)
    freq = jnp.where(d < ROPE_DIM, inv_freq, 0.0).astype(F32).reshape(1, LANES)
    m1 = (d < half).astype(F32).reshape(1, LANES)
    m2 = ((d >= half) & (d < ROPE_DIM)).astype(F32).reshape(1, LANES)
    seg = jnp.arange(ATT_WIDTH) // HEAD_DIM
    ones_bd = (seg[:, None] == seg[None, :]).astype(BF16)
    return freq, m1, m2, ones_bd


def _attention(qkv, positions, q_norm_w, k_norm_w, sinks):
    nb = SEQ // ATT_BLOCK
    freq, m1, m2, ones_bd = _rope_tables()
    qw = jnp.tile(q_norm_w.astype(F32), ATT_HEADS).reshape(1, ATT_WIDTH)
    kw = jnp.tile(k_norm_w.astype(F32), ATT_KV_HEADS).reshape(1, KV_WIDTH)
    pos = positions.reshape(TOKENS, 1).astype(I32)
    const = lambda shape: pl.BlockSpec(shape, lambda b, j, s: (0, 0))
    grid_spec = pltpu.PrefetchScalarGridSpec(
        num_scalar_prefetch=1,
        grid=(BATCH, nb),
        in_specs=[pl.BlockSpec((ATT_BLOCK, 1), lambda b, j, s: (b * nb + j, 0)),
                  pl.BlockSpec((ATT_BLOCK, ATT_WIDTH), lambda b, j, s: (b * nb + j, 0)),
                  pl.BlockSpec((ATT_BLOCK, 2 * KV_WIDTH), lambda b, j, s: (b * nb + j, 2)),
                  const((1, ATT_WIDTH)), const((1, KV_WIDTH)),
                  const((1, LANES)), const((1, LANES)), const((1, LANES)),
                  const((ATT_WIDTH, ATT_WIDTH))],
        out_specs=pl.BlockSpec((ATT_BLOCK, ATT_WIDTH), lambda b, j, s: (b * nb + j, 0)),
        scratch_shapes=[pltpu.VMEM((ATT_BLOCK, KV_WIDTH), F32),
                        pltpu.VMEM((ATT_BLOCK, KV_WIDTH), F32)],
    )
    return pl.pallas_call(
        _attn_kernel,
        grid_spec=grid_spec,
        out_shape=jax.ShapeDtypeStruct((TOKENS, ATT_WIDTH), BF16),
        compiler_params=_cparams(("arbitrary", "arbitrary")),
        name="attention",
    )(sinks.astype(F32), pos, qkv, qkv, qw, kw, freq, m1, m2, ones_bd)


def _softplus(x):
    return jnp.maximum(x, 0.0) + jnp.log1p(jnp.exp(-jnp.abs(x)))


def _silu(x):
    return x * jax.nn.sigmoid(x)


def _ssd_kernel(xbc_ref, z_ref, dt_ref, dtt_ref, cw_ref, cb_ref, dtb_row_ref, dtb_col_ref,
                alog_row_ref, alog_col_ref, dskip_ref, nw_ref, tril_ref, triu_ref,
                o_ref, conv_ref, state_ref):
    c = pl.program_id(1)
    L = CHUNK
    tail = 8

    @pl.when(c == 0)
    def _():
        conv_ref[0:tail, :] = jnp.zeros((tail, XBC_WIDTH), F32)
        state_ref[...] = jnp.zeros_like(state_ref)

    xb = xbc_ref[...].astype(F32)
    conv_ref[tail:tail + L, :] = xb
    acc = cb_ref[...] + cw_ref[CONV_K - 1:CONV_K, :] * xb
    for k in range(CONV_K - 1):
        off = tail - (CONV_K - 1) + k
        acc = acc + cw_ref[k:k + 1, :] * conv_ref[off:off + L, :]
    conv_ref[0:tail, :] = xb[L - tail:L, :]
    u = _silu(acc)
    xs = u[:, 0:SSD_WIDTH]
    bmat = u[:, SSD_WIDTH:SSD_WIDTH + SSD_GROUPS * SSD_STATE]
    cmat = u[:, SSD_WIDTH + SSD_GROUPS * SSD_STATE:XBC_WIDTH]

    dt = _softplus(dt_ref[...] + dtb_row_ref[...])
    a = dt * (-jnp.exp(alog_row_ref[...]))
    a_hi, a_lo = _split_bf16(a)
    a_cum = (jnp.dot(tril_ref[...], a_hi, preferred_element_type=F32)
             + jnp.dot(tril_ref[...], a_lo, preferred_element_type=F32))
    dt_t = _softplus(dtt_ref[...] + dtb_col_ref[...])
    a_t = dt_t * (-jnp.exp(alog_col_ref[...]))
    at_hi, at_lo = _split_bf16(a_t)
    a_cum_t = (jnp.dot(at_hi, triu_ref[...], preferred_element_type=F32)
               + jnp.dot(at_lo, triu_ref[...], preferred_element_type=F32))
    a_end_t = a_cum_t[:, L - 1:L]
    wst_t = jnp.exp(a_end_t - a_cum_t) * dt_t
    cdec_t = jnp.exp(a_end_t)

    row = lax.broadcasted_iota(I32, (L, L), 0)
    col = lax.broadcasted_iota(I32, (L, L), 1)
    causal = col <= row
    lane = lax.broadcasted_iota(I32, (L, LANES), 1)
    lo_half = lane < SSD_HEAD_DIM

    xs_b = xs.astype(BF16)
    heads_per_group = SSD_HEADS // SSD_GROUPS
    gated = []
    for g in range(SSD_GROUPS):
        b_g = bmat[:, g * SSD_STATE:(g + 1) * SSD_STATE]
        c_g = cmat[:, g * SSD_STATE:(g + 1) * SSD_STATE]
        cb = lax.dot_general(c_g.astype(BF16), b_g.astype(BF16), (((1,), (1,)), ((), ())),
                             preferred_element_type=F32)
        b_gt = b_g.T
        for t in range(heads_per_group // 2):
            tile = g * (heads_per_group // 2) + t
            c0 = tile * LANES
            xs_tile = xs_b[:, c0:c0 + LANES]
            st_tile = state_ref[:, c0:c0 + LANES]
            st_b = st_tile.astype(BF16)
            y_tile = jnp.zeros((L, LANES), F32)
            new_tile = jnp.zeros((SSD_STATE, LANES), F32)
            for e in range(2):
                h = 2 * tile + e
                keep = lo_half if e == 0 else ~lo_half
                colb = jnp.broadcast_to(a_cum[:, h:h + 1], (L, L))
                rowb = a_cum_t[h:h + 1, :]
                decay = jnp.exp(jnp.where(causal, colb - rowb, NEG_BIG))
                w_in = (cb * decay) * dt_t[h:h + 1, :]
                w_off = c_g * jnp.exp(colb)
                lhs = jnp.concatenate([w_in, w_off], axis=1).astype(BF16)
                rhs = jnp.concatenate([jnp.where(keep, xs_tile, jnp.zeros_like(xs_tile)),
                                       jnp.where(keep, st_b, jnp.zeros_like(st_b))], axis=0)
                y_tile = y_tile + jnp.dot(lhs, rhs, preferred_element_type=F32)
                m_h = (b_gt * wst_t[h:h + 1, :]).astype(BF16)
                new_tile = new_tile + jnp.dot(m_h, jnp.where(keep, xs_tile, jnp.zeros_like(xs_tile)),
                                              preferred_element_type=F32)
            cd = jnp.where(lo_half[0:1, :], cdec_t[2 * tile:2 * tile + 1, :],
                           cdec_t[2 * tile + 1:2 * tile + 2, :])
            state_ref[:, c0:c0 + LANES] = st_tile * cd + new_tile
            y_full = y_tile + dskip_ref[:, c0:c0 + LANES] * xs[:, c0:c0 + LANES]
            gated.append(y_full * _silu(z_ref[:, c0:c0 + LANES].astype(F32)))

    gw = SSD_WIDTH // SSD_GROUPS
    tiles_per_group = gw // LANES
    for g in range(SSD_GROUPS):
        yg = jnp.concatenate(gated[g * tiles_per_group:(g + 1) * tiles_per_group], axis=1)
        ms = jnp.mean(yg * yg, axis=-1, keepdims=True)
        o_ref[:, g * gw:(g + 1) * gw] = ((yg * lax.rsqrt(ms + EPS)) * nw_ref[:, g * gw:(g + 1) * gw]).astype(o_ref.dtype)


def _ssd(xbc, z, dt, conv_w, conv_b, dt_bias, a_log, d_skip, ssd_norm_w):
    nc = SEQ // CHUNK
    L = CHUNK
    dt_t = dt[:, 0:SSD_HEADS].T
    pad_row = lambda v: jnp.pad(v.astype(F32), (0, LANES - SSD_HEADS)).reshape(1, LANES)
    col8 = lambda v: v.astype(F32).reshape(SSD_HEADS, 1)
    idx = jnp.arange(L)
    tril = (idx[None, :] <= idx[:, None]).astype(BF16)
    triu = (idx[:, None] <= idx[None, :]).astype(BF16)
    dskip = jnp.repeat(d_skip.astype(F32), SSD_HEAD_DIM).reshape(1, SSD_WIDTH)
    const = lambda shape: pl.BlockSpec(shape, lambda b, c: (0, 0))
    tok = lambda width: pl.BlockSpec((L, width), lambda b, c: (b * nc + c, 0))
    return pl.pallas_call(
        _ssd_kernel,
        grid=(BATCH, nc),
        in_specs=[tok(XBC_WIDTH), tok(SSD_WIDTH), tok(LANES),
                  pl.BlockSpec((SSD_HEADS, L), lambda b, c: (0, b * nc + c)),
                  const((CONV_K, XBC_WIDTH)), const((1, XBC_WIDTH)),
                  const((1, LANES)), const((SSD_HEADS, 1)), const((1, LANES)), const((SSD_HEADS, 1)),
                  const((1, SSD_WIDTH)), const((1, SSD_WIDTH)), const((L, L)), const((L, L))],
        out_specs=tok(SSD_WIDTH),
        out_shape=jax.ShapeDtypeStruct((TOKENS, SSD_WIDTH), BF16),
        scratch_shapes=[pltpu.VMEM((8 + L, XBC_WIDTH), F32),
                        pltpu.VMEM((SSD_STATE, SSD_WIDTH), F32)],
        compiler_params=_cparams(("arbitrary", "arbitrary")),
        name="ssd",
    )(xbc, z, dt, dt_t, conv_w.astype(F32), conv_b.astype(F32).reshape(1, XBC_WIDTH),
      pad_row(dt_bias), col8(dt_bias), pad_row(a_log), col8(a_log), dskip,
      ssd_norm_w.astype(F32).reshape(1, SSD_WIDTH), tril, triu)


HALF_D = D_MODEL // 2


def _pack_rows(xf):
    r = xf.astype(BF16).astype(F32)
    lo = lax.shift_right_logical(lax.bitcast_convert_type(r[:, 0:HALF_D], jnp.uint32), jnp.uint32(16))
    hi = lax.bitcast_convert_type(r[:, HALF_D:D_MODEL], jnp.uint32) & jnp.uint32(0xFFFF0000)
    return hi | lo


def _unpack_rows(w):
    lo = lax.bitcast_convert_type(lax.shift_left(w, jnp.uint32(16)), F32)
    hi = lax.bitcast_convert_type(w & jnp.uint32(0xFFFF0000), F32)
    return jnp.concatenate([lo, hi], axis=1)


OUT_TM = 256
ROUTE_W = 8
ROUTER_COLS = N_GROUPS + N_EXPERTS


def _lane_pick(values, lane, index):
    return jnp.sum(jnp.where(lane == index, values, 0.0), axis=-1, keepdims=True)


def _first_argmax(vals, lane):
    m = jnp.max(vals, axis=-1, keepdims=True)
    idx = jnp.min(jnp.where(vals == m, lane, float(LANES)), axis=-1, keepdims=True)
    return m, idx


def _out_router_kernel(att_ref, y_ref, x_ref, g1_ref, wo_ref, nw_ref, sc_ref, sh_ref, wr_ref, br_ref,
                       ltri_ref, x1_ref, h2_ref, route_ref, cnt_ref, wr_hi_ref, wr_lo_ref, carry_ref):
    i = pl.program_id(0)

    @pl.when(i == 0)
    def _():
        hi, lo = _split_bf16(wr_ref[...])
        wr_hi_ref[...] = hi
        wr_lo_ref[...] = lo
        carry_ref[...] = jnp.zeros_like(carry_ref)

    mixer = (jnp.dot(att_ref[...], wo_ref[0:ATT_WIDTH, :], preferred_element_type=F32)
             + jnp.dot(y_ref[...], wo_ref[ATT_WIDTH:ATT_WIDTH + SSD_WIDTH, :], preferred_element_type=F32))
    x1 = x_ref[...] + g1_ref[0] * mixer
    x1_ref[...] = x1
    yn = x1 * lax.rsqrt(jnp.mean(x1 * x1, axis=-1, keepdims=True) + EPS)
    h2 = (yn * nw_ref[...]) * (1.0 + sc_ref[0]) + sh_ref[0]
    h2_ref[...] = _pack_rows(h2)

    h_hi, h_lo = _split_bf16(h2)
    logits = (jnp.dot(h_hi, wr_hi_ref[...], preferred_element_type=F32)
              + jnp.dot(h_hi, wr_lo_ref[...], preferred_element_type=F32)
              + jnp.dot(h_lo, wr_hi_ref[...], preferred_element_type=F32)) + br_ref[...]
    tm = logits.shape[0]
    lane = lax.broadcasted_iota(I32, (tm, LANES), 1).astype(F32)

    gl = jnp.where(lane < N_GROUPS, logits, NEG_BIG)
    gmax, gidx = _first_argmax(gl, lane)
    g_p = 1.0 / jnp.sum(jnp.exp(gl - gmax), axis=-1, keepdims=True)

    lo_lane = N_GROUPS + EXPERTS_PER_GROUP * gidx
    el = jnp.where((lane >= lo_lane) & (lane < lo_lane + EXPERTS_PER_GROUP), logits, NEG_BIG)
    m1, i1 = _first_argmax(el, lane)
    m2, i2 = _first_argmax(jnp.where(lane == i1, NEG_BIG, el), lane)
    r = jnp.exp(m2 - m1)
    p1 = 1.0 / (1.0 + r)
    p2 = r / (1.0 + r)
    e0 = i1 - N_GROUPS
    e1 = i2 - N_GROUPS

    onehot = ((lane == e0) | (lane == e1)).astype(F32)
    before = jnp.dot(ltri_ref[...], onehot.astype(BF16), preferred_element_type=F32) + carry_ref[...]
    rank0 = _lane_pick(before, lane, e0)
    rank1 = _lane_pick(before, lane, e1)
    carry_ref[...] = carry_ref[...] + jnp.sum(onehot, axis=0, keepdims=True)
    cnt_ref[...] = jnp.broadcast_to(carry_ref[...], cnt_ref.shape)

    cols = [e0.astype(F32), e1.astype(F32), g_p * p1, g_p * p2, rank0, rank1]
    rec = jnp.zeros((tm, LANES), F32)
    for k, v in enumerate(cols):
        rec = jnp.where(lane == k, v, rec)
    route_ref[...] = rec[:, 0:ROUTE_W]


def _out_router(att, y, x2d, mod3, w_out_b, norm_w, w_router, b_router):
    tm = OUT_TM
    steps_per_batch = SEQ // tm
    idx = jnp.arange(tm)
    ltri = (idx[None, :] < idx[:, None]).astype(BF16)
    const = lambda shape: pl.BlockSpec(shape, lambda i: (0, 0))
    tok = lambda width: pl.BlockSpec((tm, width), lambda i: (i, 0))
    modspec = lambda k: pl.BlockSpec((1, 1, D_MODEL), lambda i: ((i // steps_per_batch) * 6 + k, 0, 0))
    return pl.pallas_call(
        _out_router_kernel,
        grid=(TOKENS // tm,),
        in_specs=[tok(ATT_WIDTH), tok(SSD_WIDTH), tok(D_MODEL), modspec(2),
                  const((D_MODEL, D_MODEL)), const((1, D_MODEL)), modspec(4), modspec(3),
                  const((D_MODEL, LANES)), const((1, LANES)), const((tm, tm))],
        out_specs=[tok(D_MODEL), tok(HALF_D), tok(ROUTE_W), const((8, LANES))],
        out_shape=[jax.ShapeDtypeStruct((TOKENS, D_MODEL), F32),
                   jax.ShapeDtypeStruct((TOKENS, HALF_D), jnp.uint32),
                   jax.ShapeDtypeStruct((TOKENS, ROUTE_W), F32),
                   jax.ShapeDtypeStruct((8, LANES), F32)],
        scratch_shapes=[pltpu.VMEM((D_MODEL, LANES), BF16), pltpu.VMEM((D_MODEL, LANES), BF16),
                        pltpu.VMEM((1, LANES), F32)],
        compiler_params=_cparams(("arbitrary",)),
        name="out_router",
    )(att, y, x2d, mod3, w_out_b, norm_w.reshape(1, D_MODEL), mod3, mod3, w_router, b_router, ltri)


POS_TM = 2048


def _pos_kernel(route_ref, off_ref, pos_ref):
    rec = route_ref[...]
    tm = rec.shape[0]
    lane = lax.broadcasted_iota(I32, (tm, LANES), 1)
    off = off_ref[...]
    out = jnp.zeros((tm, ROUTE_W), F32)
    lane8 = lax.broadcasted_iota(I32, (tm, ROUTE_W), 1)
    for k in range(TOP_K):
        e = rec[:, k:k + 1].astype(I32)
        p = _lane_pick(jnp.broadcast_to(off, (tm, LANES)), lane, e) + rec[:, 4 + k:5 + k]
        out = jnp.where(lane8 == k, p, out)
    pos_ref[...] = out.astype(I32)


def _positions(route, offsets_row):
    return pl.pallas_call(
        _pos_kernel,
        grid=(TOKENS // POS_TM,),
        in_specs=[pl.BlockSpec((POS_TM, ROUTE_W), lambda i: (i, 0)),
                  pl.BlockSpec((1, LANES), lambda i: (0, 0))],
        out_specs=pl.BlockSpec((POS_TM, ROUTE_W), lambda i: (i, 0)),
        out_shape=jax.ShapeDtypeStruct((TOKENS, ROUTE_W), I32),
        compiler_params=_cparams(("arbitrary",)),
        name="positions",
    )(route, offsets_row)


MOE_TM = 256
N_TILES = (TOKENS * TOP_K) // MOE_TM + N_EXPERTS
N_ROWS = N_TILES * MOE_TM
DISP_TM = 256


def _dispatch_kernel(seg_end_ref, cnt_ref, pos_ref, h2_ref, xs_ref, zero_ref, sem, zsem):
    i = pl.program_id(0)

    @pl.when(i == 0)
    def _():
        zero_ref[...] = jnp.zeros_like(zero_ref)

        def zcopy(start):
            return pltpu.make_async_copy(zero_ref, xs_ref.at[pl.ds(pl.multiple_of(start, MOE_TM), MOE_TM)], zsem)

        def zstart(e, carry):
            @pl.when(cnt_ref[e] > 0)
            def _():
                zcopy(seg_end_ref[e] - MOE_TM).start()
            return carry

        def zwait(e, carry):
            @pl.when(cnt_ref[e] > 0)
            def _():
                zcopy(seg_end_ref[e] - MOE_TM).wait()
            return carry

        def tstart(t, carry):
            zcopy(t * MOE_TM).start()
            return carry

        def twait(t, carry):
            zcopy(t * MOE_TM).wait()
            return carry

        first_unused = seg_end_ref[N_EXPERTS - 1] // MOE_TM
        lax.fori_loop(0, N_EXPERTS, zstart, 0)
        lax.fori_loop(first_unused, N_TILES, tstart, 0)
        lax.fori_loop(0, N_EXPERTS, zwait, 0)
        lax.fori_loop(first_unused, N_TILES, twait, 0)

    def row_copy(r, k):
        p = pos_ref[0, 0, TOP_K * r + k]
        return pltpu.make_async_copy(h2_ref.at[pl.ds(r, 1)], xs_ref.at[pl.ds(p, 1)], sem)

    def start(r, carry):
        for k in range(TOP_K):
            row_copy(r, k).start()
        return carry

    def wait(r, carry):
        for k in range(TOP_K):
            row_copy(r, k).wait()
        return carry

    lax.fori_loop(0, DISP_TM, start, 0)
    lax.fori_loop(0, DISP_TM, wait, 0)


def _dispatch(seg_end, counts, pos3, h2p):
    grid_spec = pltpu.PrefetchScalarGridSpec(
        num_scalar_prefetch=2,
        grid=(TOKENS // DISP_TM,),
        in_specs=[pl.BlockSpec((1, 1, TOP_K * DISP_TM), lambda i, se, cn: (i, 0, 0),
                               memory_space=pltpu.SMEM),
                  pl.BlockSpec((DISP_TM, HALF_D), lambda i, se, cn: (i, 0))],
        out_specs=pl.BlockSpec(memory_space=pl.ANY),
        scratch_shapes=[pltpu.VMEM((MOE_TM, HALF_D), jnp.uint32),
                        pltpu.SemaphoreType.DMA, pltpu.SemaphoreType.DMA],
    )
    return pl.pallas_call(
        _dispatch_kernel,
        grid_spec=grid_spec,
        out_shape=jax.ShapeDtypeStruct((N_ROWS, HALF_D), jnp.uint32),
        compiler_params=_cparams(("arbitrary",)),
        name="dispatch",
    )(seg_end, counts, pos3, h2p)


def _experts_kernel(te_ref, nu_ref, xs_ref, wg_ref, wu_ref, wd_ref, ys_ref):
    i = pl.program_id(0)

    @pl.when(i < nu_ref[0])
    def _():
        x = _unpack_rows(xs_ref[...]).astype(BF16)
        hg = jnp.dot(x, wg_ref[0].astype(BF16), preferred_element_type=F32)
        hu = jnp.dot(x, wu_ref[0].astype(BF16), preferred_element_type=F32)
        act = (_silu(hg) * hu).astype(BF16)
        ys_ref[...] = _pack_rows(jnp.dot(act, wd_ref[0].astype(BF16), preferred_element_type=F32))

    @pl.when(i >= nu_ref[0])
    def _():
        ys_ref[...] = jnp.zeros_like(ys_ref)


def _experts(tile_expert, n_used, xs, w_gate, w_up, w_down):
    row_tile = lambda i, te, nu: (jnp.minimum(i, nu[0] - 1), 0)
    grid_spec = pltpu.PrefetchScalarGridSpec(
        num_scalar_prefetch=2,
        grid=(N_TILES,),
        in_specs=[pl.BlockSpec((MOE_TM, HALF_D), row_tile),
                  pl.BlockSpec((1, D_MODEL, D_EXPERT), lambda i, te, nu: (te[i], 0, 0)),
                  pl.BlockSpec((1, D_MODEL, D_EXPERT), lambda i, te, nu: (te[i], 0, 0)),
                  pl.BlockSpec((1, D_EXPERT, D_MODEL), lambda i, te, nu: (te[i], 0, 0))],
        out_specs=pl.BlockSpec((MOE_TM, HALF_D), lambda i, te, nu: (i, 0)),
    )
    return pl.pallas_call(
        _experts_kernel,
        grid_spec=grid_spec,
        out_shape=jax.ShapeDtypeStruct((N_ROWS, HALF_D), jnp.uint32),
        compiler_params=_cparams(("arbitrary",)),
        name="experts",
    )(tile_expert, n_used, xs, w_gate, w_up, w_down)


def _combine_kernel(pos_ref, route_ref, x1_ref, g2_ref, ys_ref, o_ref, gbuf_ref, sem):
    def row_copy(r, k):
        p = pos_ref[0, 0, TOP_K * r + k]
        return pltpu.make_async_copy(ys_ref.at[pl.ds(p, 1)], gbuf_ref.at[k, pl.ds(r, 1)], sem)

    def start(r, carry):
        for k in range(TOP_K):
            row_copy(r, k).start()
        return carry

    def wait(r, carry):
        for k in range(TOP_K):
            row_copy(r, k).wait()
        return carry

    lax.fori_loop(0, DISP_TM, start, 0)
    lax.fori_loop(0, DISP_TM, wait, 0)
    rec = route_ref[...]
    moe = rec[:, 2:3] * _unpack_rows(gbuf_ref[0]) + rec[:, 3:4] * _unpack_rows(gbuf_ref[1])
    o_ref[...] = x1_ref[...] + g2_ref[0] * moe


def _combine(pos3, route, x1, mod3, ys):
    tm = DISP_TM
    steps_per_batch = SEQ // tm
    return pl.pallas_call(
        _combine_kernel,
        grid=(TOKENS // tm,),
        in_specs=[pl.BlockSpec((1, 1, TOP_K * tm), lambda i: (i, 0, 0), memory_space=pltpu.SMEM),
                  pl.BlockSpec((tm, ROUTE_W), lambda i: (i, 0)),
                  pl.BlockSpec((tm, D_MODEL), lambda i: (i, 0)),
                  pl.BlockSpec((1, 1, D_MODEL), lambda i: ((i // steps_per_batch) * 6 + 5, 0, 0)),
                  pl.BlockSpec(memory_space=pl.ANY)],
        out_specs=pl.BlockSpec((tm, D_MODEL), lambda i: (i, 0)),
        out_shape=jax.ShapeDtypeStruct((TOKENS, D_MODEL), F32),
        scratch_shapes=[pltpu.VMEM((TOP_K, tm, HALF_D), jnp.uint32), pltpu.SemaphoreType.DMA],
        compiler_params=_cparams(("arbitrary",)),
        name="combine",
    )(pos3, route, x1, mod3, ys)


def kernel(x, c, positions, norm1_w, norm2_w, w_ada, b_ada, w_in, conv_w, conv_b, dt_bias, a_log,
           d_skip, ssd_norm_w, q_norm_w, k_norm_w, sinks, w_out, w_group, b_group, w_expert, b_expert,
           w_gate, w_up, w_down):
    assert x.shape == (BATCH, SEQ, D_MODEL) and w_in.shape == (D_MODEL, IN_WIDTH)
    x2d = x.reshape(TOKENS, D_MODEL)
    mod = _ada_mod(c, w_ada, b_ada)
    mod3 = mod.reshape(BATCH * 6, 1, D_MODEL)

    w_in_pad = jnp.pad(w_in, ((0, 0), (0, IN_PAD - IN_WIDTH))).astype(BF16)
    qkv, z, xbc, dt = _in_proj(x2d, norm1_w, mod3, w_in_pad)
    att = _attention(qkv, positions, q_norm_w, k_norm_w, sinks)
    y = _ssd(xbc, z, dt, conv_w, conv_b, dt_bias, a_log, d_skip, ssd_norm_w)

    w_router = jnp.pad(jnp.concatenate([w_group, w_expert], axis=1).astype(F32),
                       ((0, 0), (0, LANES - ROUTER_COLS)))
    b_router = jnp.pad(jnp.concatenate([b_group, b_expert]).astype(F32),
                       (0, LANES - ROUTER_COLS)).reshape(1, LANES)
    x1, h2p, route, cnt = _out_router(att, y, x2d, mod3, w_out.astype(BF16), norm2_w, w_router, b_router)

    counts = cnt[0, 0:N_EXPERTS].astype(I32)
    padded = ((counts + MOE_TM - 1) // MOE_TM) * MOE_TM
    seg_end = jnp.cumsum(padded)
    seg_start = seg_end - padded
    n_used = (seg_end[-1] // MOE_TM).reshape(1)
    tile_start = jnp.arange(N_TILES, dtype=I32) * MOE_TM
    tile_expert = jnp.searchsorted(seg_end, jnp.minimum(tile_start, seg_end[-1] - 1), side="right").astype(I32)
    offsets_row = jnp.pad(seg_start.astype(F32), (0, LANES - N_EXPERTS)).reshape(1, LANES)

    pos = _positions(route, offsets_row)
    pos3 = pos[:, 0:TOP_K].reshape(TOKENS // DISP_TM, 1, TOP_K * DISP_TM)
    xs = _dispatch(seg_end.astype(I32), counts, pos3, h2p)
    ys = _experts(tile_expert, n_used.astype(I32), xs, w_gate, w_up, w_down)
    out = _combine(pos3, route, x1, mod3, ys)
    return out.reshape(BATCH, SEQ, D_MODEL)
```

```python
import jax
import jax.numpy as jnp
from jax import lax
from jax.experimental import pallas as pl
from jax.experimental.pallas import tpu as pltpu

F32 = jnp.float32
BF16 = jnp.bfloat16
I32 = jnp.int32

D_MODEL = 1024
BATCH = 2
SEQ = 8192
TOKENS = BATCH * SEQ
ATT_HEADS = 8
ATT_KV_HEADS = 2
HEAD_DIM = 64
ATT_WIDTH = ATT_HEADS * HEAD_DIM
KV_WIDTH = ATT_KV_HEADS * HEAD_DIM
ATT_BLOCK = 128
ROPE_DIM = HEAD_DIM // 4
ROPE_THETA = 500000.0
SSD_HEADS = 8
SSD_HEAD_DIM = 64
SSD_WIDTH = SSD_HEADS * SSD_HEAD_DIM
SSD_GROUPS = 2
SSD_STATE = 128
CONV_K = 4
CHUNK = 128
XBC_WIDTH = SSD_WIDTH + 2 * SSD_GROUPS * SSD_STATE
IN_WIDTH = ATT_WIDTH + 2 * KV_WIDTH + SSD_WIDTH + XBC_WIDTH + SSD_HEADS
N_GROUPS = 4
EXPERTS_PER_GROUP = 8
N_EXPERTS = N_GROUPS * EXPERTS_PER_GROUP
TOP_K = 2
D_EXPERT = 256
EPS = 1e-6

LANES = 128
QKV_WIDTH = ATT_WIDTH + 2 * KV_WIDTH
IN_PAD = QKV_WIDTH + SSD_WIDTH + XBC_WIDTH + LANES
NEG_BIG = -1e30

VMEM_LIMIT = 48 * 1024 * 1024


def _cparams(sem):
    return pltpu.CompilerParams(dimension_semantics=sem, vmem_limit_bytes=VMEM_LIMIT)


def _split_bf16(x):
    hi = x.astype(BF16)
    lo = (x - hi.astype(F32)).astype(BF16)
    return hi, lo


ADA_TN = 768


def _ada_kernel(ct_ref, w_ref, b_ref, o_ref):
    ct = ct_ref[...]
    s = ct * jax.nn.sigmoid(ct)
    w = w_ref[...]
    rows = [jnp.sum(s[:, b:b + 1] * w, axis=0, keepdims=True) for b in range(BATCH)]
    o_ref[...] = jnp.concatenate(rows, axis=0) + b_ref[...]


def _ada_mod(c, w_ada, b_ada):
    n = w_ada.shape[1]
    return pl.pallas_call(
        _ada_kernel,
        grid=(n // ADA_TN,),
        in_specs=[pl.BlockSpec((D_MODEL, BATCH), lambda j: (0, 0)),
                  pl.BlockSpec((D_MODEL, ADA_TN), lambda j: (0, j)),
                  pl.BlockSpec((1, ADA_TN), lambda j: (0, j))],
        out_specs=pl.BlockSpec((BATCH, ADA_TN), lambda j: (0, j)),
        out_shape=jax.ShapeDtypeStruct((BATCH, n), F32),
        compiler_params=_cparams(("arbitrary",)),
        name="ada_mod",
    )(c.T, w_ada, b_ada.reshape(1, n))


INPROJ_TM = 256
_INPROJ_CHUNK = 256


def _inproj_kernel(x_ref, nw_ref, sc_ref, sh_ref, w_ref, qkv_ref, z_ref, xbc_ref, dt_ref):
    x = x_ref[...]
    y = x * lax.rsqrt(jnp.mean(x * x, axis=-1, keepdims=True) + EPS)
    h = (y * nw_ref[...]) * (1.0 + sc_ref[0]) + sh_ref[0]
    hb = h.astype(BF16)

    def proj(c0, c1):
        return jnp.dot(hb, w_ref[:, c0:c1], preferred_element_type=F32)

    for c0 in range(0, QKV_WIDTH, _INPROJ_CHUNK):
        qkv_ref[:, c0:c0 + _INPROJ_CHUNK] = proj(c0, c0 + _INPROJ_CHUNK).astype(BF16)
    base = QKV_WIDTH
    for c0 in range(0, SSD_WIDTH, _INPROJ_CHUNK):
        z_ref[:, c0:c0 + _INPROJ_CHUNK] = proj(base + c0, base + c0 + _INPROJ_CHUNK).astype(BF16)
    base += SSD_WIDTH
    for c0 in range(0, XBC_WIDTH, _INPROJ_CHUNK):
        xbc_ref[:, c0:c0 + _INPROJ_CHUNK] = proj(base + c0, base + c0 + _INPROJ_CHUNK).astype(BF16)
    base += XBC_WIDTH
    dt_ref[...] = proj(base, base + LANES)


def _in_proj(x2d, norm_w, mod3, w_in_pad):
    tm = INPROJ_TM
    steps_per_batch = SEQ // tm
    return pl.pallas_call(
        _inproj_kernel,
        grid=(TOKENS // tm,),
        in_specs=[pl.BlockSpec((tm, D_MODEL), lambda i: (i, 0)),
                  pl.BlockSpec((1, D_MODEL), lambda i: (0, 0)),
                  pl.BlockSpec((1, 1, D_MODEL), lambda i: ((i // steps_per_batch) * 6 + 1, 0, 0)),
                  pl.BlockSpec((1, 1, D_MODEL), lambda i: ((i // steps_per_batch) * 6 + 0, 0, 0)),
                  pl.BlockSpec((D_MODEL, IN_PAD), lambda i: (0, 0))],
        out_specs=[pl.BlockSpec((tm, QKV_WIDTH), lambda i: (i, 0)),
                   pl.BlockSpec((tm, SSD_WIDTH), lambda i: (i, 0)),
                   pl.BlockSpec((tm, XBC_WIDTH), lambda i: (i, 0)),
                   pl.BlockSpec((tm, LANES), lambda i: (i, 0))],
        out_shape=[jax.ShapeDtypeStruct((TOKENS, QKV_WIDTH), BF16),
                   jax.ShapeDtypeStruct((TOKENS, SSD_WIDTH), BF16),
                   jax.ShapeDtypeStruct((TOKENS, XBC_WIDTH), BF16),
                   jax.ShapeDtypeStruct((TOKENS, LANES), F32)],
        compiler_params=_cparams(("arbitrary",)),
        name="in_proj",
    )(x2d, norm_w.reshape(1, D_MODEL), mod3, mod3, w_in_pad)


def _seg_meansq(xf, ones_bd):
    hi, lo = _split_bf16(xf * xf)
    tot = (jnp.dot(hi, ones_bd, preferred_element_type=F32)
           + jnp.dot(lo, ones_bd, preferred_element_type=F32))
    return tot * (1.0 / HEAD_DIM)


def _norm_rope(x_bf, w_row, ones_bd, cosf, s1, s2):
    xf = x_bf.astype(F32)
    width = xf.shape[1]
    xn = xf * lax.rsqrt(_seg_meansq(xf, ones_bd) + EPS) * w_row
    half = ROPE_DIM // 2
    up = pltpu.roll(xn, width - half, axis=1)
    down = pltpu.roll(xn, half, axis=1)
    return xn * cosf + up * s1 + down * s2


def _attn_kernel(sink_ref, pos_ref, q_ref, kv_ref, qw_ref, kw_ref, freq_ref, m1_ref, m2_ref,
                 ones_ref, o_ref, kprev_ref, vprev_ref):
    j = pl.program_id(1)
    blk = ATT_BLOCK

    @pl.when(j == 0)
    def _():
        kprev_ref[...] = jnp.zeros_like(kprev_ref)
        vprev_ref[...] = jnp.zeros_like(vprev_ref)

    ang = pos_ref[...].astype(F32) * freq_ref[...]
    cos1 = jnp.cos(ang)
    sin1 = jnp.sin(ang)
    s1_1 = -sin1 * m1_ref[...]
    s2_1 = sin1 * m2_ref[...]
    reps = ATT_WIDTH // LANES
    cosq = jnp.concatenate([cos1] * reps, axis=1)
    s1q = jnp.concatenate([s1_1] * reps, axis=1)
    s2q = jnp.concatenate([s2_1] * reps, axis=1)

    q = _norm_rope(q_ref[...], qw_ref[...], ones_ref[...], cosq, s1q, s2q)
    qb = (q * (HEAD_DIM ** -0.5)).astype(BF16)
    kv = kv_ref[...]
    kn = _norm_rope(kv[:, 0:KV_WIDTH], kw_ref[...], ones_ref[0:KV_WIDTH, 0:KV_WIDTH],
                    cos1, s1_1, s2_1)
    vn = kv[:, KV_WIDTH:2 * KV_WIDTH].astype(F32)

    kcat = jnp.concatenate([kprev_ref[...], kn], axis=0)
    vcat = jnp.concatenate([vprev_ref[...], vn], axis=0)
    kprev_ref[...] = kn
    vprev_ref[...] = vn

    lane = lax.broadcasted_iota(I32, (2 * blk, LANES), 1)
    lo_half = lane < HEAD_DIM
    zero = jnp.zeros_like(kcat)

    def variants(t, g):
        own = jnp.where(lo_half if g == 0 else ~lo_half, t, zero)
        other = pltpu.roll(own, HEAD_DIM, axis=1)
        lo, hi = (own, other) if g == 0 else (other, own)
        return lo.astype(BF16), hi.astype(BF16)

    row = lax.broadcasted_iota(I32, (2 * blk, 2 * blk), 0)
    col = lax.broadcasted_iota(I32, (2 * blk, 2 * blk), 1)
    qi = jnp.where(row >= blk, row - blk, row)
    no_prev = jnp.where(j > 0, 0, 2 * blk)
    valid = ((col < blk) & (col > qi + no_prev)) | ((col >= blk) & ((col - blk) <= qi))
    bias = jnp.where(valid, 0.0, NEG_BIG).astype(F32)
    first = lax.broadcasted_iota(I32, (2 * blk, 1), 0) < blk

    def probs(qcat, kvar, head_a, head_b):
        s = lax.dot_general(qcat, kvar, (((1,), (1,)), ((), ())), preferred_element_type=F32) + bias
        sink = jnp.where(first, sink_ref[head_a], sink_ref[head_b])
        m = jnp.maximum(jnp.max(s, axis=-1, keepdims=True), sink)
        p = jnp.exp(s - m)
        denom = jnp.sum(p, axis=-1, keepdims=True) + jnp.exp(sink - m)
        return (p * (1.0 / denom)).astype(BF16)

    for g in range(ATT_KV_HEADS):
        k_lo, k_hi = variants(kcat, g)
        v_lo, v_hi = variants(vcat, g)
        c0 = g * 2 * LANES
        qcat = jnp.concatenate([qb[:, c0:c0 + LANES], qb[:, c0 + LANES:c0 + 2 * LANES]], axis=0)
        h0 = 4 * g
        p_lo = probs(qcat, k_lo, h0 + 0, h0 + 2)
        p_hi = probs(qcat, k_hi, h0 + 1, h0 + 3)
        out = (jnp.dot(p_lo, v_lo, preferred_element_type=F32)
               + jnp.dot(p_hi, v_hi, preferred_element_type=F32))
        o_ref[:, c0:c0 + LANES] = out[0:blk].astype(BF16)
        o_ref[:, c0 + LANES:c0 + 2 * LANES] = out[blk:2 * blk].astype(BF16)


def _rope_tables():
    half = ROPE_DIM // 2
    lane = jnp.arange(LANES)
    d = lane % HEAD_DIM
    inv_freq = jnp.power(ROPE_THETA, -(d % half).astype(F32) * 2.0 / ROPE_DIM)
    freq = jnp.where(d < ROPE_DIM, inv_freq, 0.0).astype(F32).reshape(1, LANES)
    m1 = (d < half).astype(F32).reshape(1, LANES)
    m2 = ((d >= half) & (d < ROPE_DIM)).astype(F32).reshape(1, LANES)
    seg = jnp.arange(ATT_WIDTH) // HEAD_DIM
    ones_bd = (seg[:, None] == seg[None, :]).astype(BF16)
    return freq, m1, m2, ones_bd


def _attention(qkv, positions, q_norm_w, k_norm_w, sinks):
    nb = SEQ // ATT_BLOCK
    freq, m1, m2, ones_bd = _rope_tables()
    qw = jnp.tile(q_norm_w.astype(F32), ATT_HEADS).reshape(1, ATT_WIDTH)
    kw = jnp.tile(k_norm_w.astype(F32), ATT_KV_HEADS).reshape(1, KV_WIDTH)
    pos = positions.reshape(TOKENS, 1).astype(I32)
    const = lambda shape: pl.BlockSpec(shape, lambda b, j, s: (0, 0))
    grid_spec = pltpu.PrefetchScalarGridSpec(
        num_scalar_prefetch=1,
        grid=(BATCH, nb),
        in_specs=[pl.BlockSpec((ATT_BLOCK, 1), lambda b, j, s: (b * nb + j, 0)),
                  pl.BlockSpec((ATT_BLOCK, ATT_WIDTH), lambda b, j, s: (b * nb + j, 0)),
                  pl.BlockSpec((ATT_BLOCK, 2 * KV_WIDTH), lambda b, j, s: (b * nb + j, 2)),
                  const((1, ATT_WIDTH)), const((1, KV_WIDTH)),
                  const((1, LANES)), const((1, LANES)), const((1, LANES)),
                  const((ATT_WIDTH, ATT_WIDTH))],
        out_specs=pl.BlockSpec((ATT_BLOCK, ATT_WIDTH), lambda b, j, s: (b * nb + j, 0)),
        scratch_shapes=[pltpu.VMEM((ATT_BLOCK, KV_WIDTH), F32),
                        pltpu.VMEM((ATT_BLOCK, KV_WIDTH), F32)],
    )
    return pl.pallas_call(
        _attn_kernel,
        grid_spec=grid_spec,
        out_shape=jax.ShapeDtypeStruct((TOKENS, ATT_WIDTH), BF16),
        compiler_params=_cparams(("arbitrary", "arbitrary")),
        name="attention",
    )(sinks.astype(F32), pos, qkv, qkv, qw, kw, freq, m1, m2, ones_bd)


def _softplus(x):
    return jnp.maximum(x, 0.0) + jnp.log1p(jnp.exp(-jnp.abs(x)))


def _silu(x):
    return x * jax.nn.sigmoid(x)


def _ssd_kernel(xbc_ref, z_ref, dt_ref, dtt_ref, cw_ref, cb_ref, dtb_row_ref, dtb_col_ref,
                alog_row_ref, alog_col_ref, dskip_ref, nw_ref, tril_ref, triu_ref,
                o_ref, conv_ref, state_ref):
    c = pl.program_id(1)
    L = CHUNK
    tail = 8

    @pl.when(c == 0)
    def _():
        conv_ref[0:tail, :] = jnp.zeros((tail, XBC_WIDTH), F32)
        state_ref[...] = jnp.zeros_like(state_ref)

    xb = xbc_ref[...].astype(F32)
    conv_ref[tail:tail + L, :] = xb
    acc = cb_ref[...] + cw_ref[CONV_K - 1:CONV_K, :] * xb
    for k in range(CONV_K - 1):
        off = tail - (CONV_K - 1) + k
        acc = acc + cw_ref[k:k + 1, :] * conv_ref[off:off + L, :]
    conv_ref[0:tail, :] = xb[L - tail:L, :]
    u = _silu(acc)
    xs = u[:, 0:SSD_WIDTH]
    bmat = u[:, SSD_WIDTH:SSD_WIDTH + SSD_GROUPS * SSD_STATE]
    cmat = u[:, SSD_WIDTH + SSD_GROUPS * SSD_STATE:XBC_WIDTH]

    dt = _softplus(dt_ref[...] + dtb_row_ref[...])
    a = dt * (-jnp.exp(alog_row_ref[...]))
    a_hi, a_lo = _split_bf16(a)
    a_cum = (jnp.dot(tril_ref[...], a_hi, preferred_element_type=F32)
             + jnp.dot(tril_ref[...], a_lo, preferred_element_type=F32))
    dt_t = _softplus(dtt_ref[...] + dtb_col_ref[...])
    a_t = dt_t * (-jnp.exp(alog_col_ref[...]))
    at_hi, at_lo = _split_bf16(a_t)
    a_cum_t = (jnp.dot(at_hi, triu_ref[...], preferred_element_type=F32)
               + jnp.dot(at_lo, triu_ref[...], preferred_element_type=F32))
    a_end_t = a_cum_t[:, L - 1:L]
    wst_t = jnp.exp(a_end_t - a_cum_t) * dt_t
    cdec_t = jnp.exp(a_end_t)

    row = lax.broadcasted_iota(I32, (L, L), 0)
    col = lax.broadcasted_iota(I32, (L, L), 1)
    causal = col <= row
    lane = lax.broadcasted_iota(I32, (L, LANES), 1)
    lo_half = lane < SSD_HEAD_DIM

    xs_b = xs.astype(BF16)
    heads_per_group = SSD_HEADS // SSD_GROUPS
    gated = []
    for g in range(SSD_GROUPS):
        b_g = bmat[:, g * SSD_STATE:(g + 1) * SSD_STATE]
        c_g = cmat[:, g * SSD_STATE:(g + 1) * SSD_STATE]
        cb = lax.dot_general(c_g.astype(BF16), b_g.astype(BF16), (((1,), (1,)), ((), ())),
                             preferred_element_type=F32)
        b_gt = b_g.T
        for t in range(heads_per_group // 2):
            tile = g * (heads_per_group // 2) + t
            c0 = tile * LANES
            xs_tile = xs_b[:, c0:c0 + LANES]
            st_tile = state_ref[:, c0:c0 + LANES]
            st_b = st_tile.astype(BF16)
            y_tile = jnp.zeros((L, LANES), F32)
            new_tile = jnp.zeros((SSD_STATE, LANES), F32)
            for e in range(2):
                h = 2 * tile + e
                keep = lo_half if e == 0 else ~lo_half
                colb = jnp.broadcast_to(a_cum[:, h:h + 1], (L, L))
                rowb = a_cum_t[h:h + 1, :]
                decay = jnp.exp(jnp.where(causal, colb - rowb, NEG_BIG))
                w_in = (cb * decay) * dt_t[h:h + 1, :]
                w_off = c_g * jnp.exp(colb)
                lhs = jnp.concatenate([w_in, w_off], axis=1).astype(BF16)
                rhs = jnp.concatenate([jnp.where(keep, xs_tile, jnp.zeros_like(xs_tile)),
                                       jnp.where(keep, st_b, jnp.zeros_like(st_b))], axis=0)
                y_tile = y_tile + jnp.dot(lhs, rhs, preferred_element_type=F32)
                m_h = (b_gt * wst_t[h:h + 1, :]).astype(BF16)
                new_tile = new_tile + jnp.dot(m_h, jnp.where(keep, xs_tile, jnp.zeros_like(xs_tile)),
                                              preferred_element_type=F32)
            cd = jnp.where(lo_half[0:1, :], cdec_t[2 * tile:2 * tile + 1, :],
                           cdec_t[2 * tile + 1:2 * tile + 2, :])
            state_ref[:, c0:c0 + LANES] = st_tile * cd + new_tile
            y_full = y_tile + dskip_ref[:, c0:c0 + LANES] * xs[:, c0:c0 + LANES]
            gated.append(y_full * _silu(z_ref[:, c0:c0 + LANES].astype(F32)))

    gw = SSD_WIDTH // SSD_GROUPS
    tiles_per_group = gw // LANES
    for g in range(SSD_GROUPS):
        yg = jnp.concatenate(gated[g * tiles_per_group:(g + 1) * tiles_per_group], axis=1)
        ms = jnp.mean(yg * yg, axis=-1, keepdims=True)
        o_ref[:, g * gw:(g + 1) * gw] = ((yg * lax.rsqrt(ms + EPS)) * nw_ref[:, g * gw:(g + 1) * gw]).astype(o_ref.dtype)


def _ssd(xbc, z, dt, conv_w, conv_b, dt_bias, a_log, d_skip, ssd_norm_w):
    nc = SEQ // CHUNK
    L = CHUNK
    dt_t = dt[:, 0:SSD_HEADS].T
    pad_row = lambda v: jnp.pad(v.astype(F32), (0, LANES - SSD_HEADS)).reshape(1, LANES)
    col8 = lambda v: v.astype(F32).reshape(SSD_HEADS, 1)
    idx = jnp.arange(L)
    tril = (idx[None, :] <= idx[:, None]).astype(BF16)
    triu = (idx[:, None] <= idx[None, :]).astype(BF16)
    dskip = jnp.repeat(d_skip.astype(F32), SSD_HEAD_DIM).reshape(1, SSD_WIDTH)
    const = lambda shape: pl.BlockSpec(shape, lambda b, c: (0, 0))
    tok = lambda width: pl.BlockSpec((L, width), lambda b, c: (b * nc + c, 0))
    return pl.pallas_call(
        _ssd_kernel,
        grid=(BATCH, nc),
        in_specs=[tok(XBC_WIDTH), tok(SSD_WIDTH), tok(LANES),
                  pl.BlockSpec((SSD_HEADS, L), lambda b, c: (0, b * nc + c)),
                  const((CONV_K, XBC_WIDTH)), const((1, XBC_WIDTH)),
                  const((1, LANES)), const((SSD_HEADS, 1)), const((1, LANES)), const((SSD_HEADS, 1)),
                  const((1, SSD_WIDTH)), const((1, SSD_WIDTH)), const((L, L)), const((L, L))],
        out_specs=tok(SSD_WIDTH),
        out_shape=jax.ShapeDtypeStruct((TOKENS, SSD_WIDTH), BF16),
        scratch_shapes=[pltpu.VMEM((8 + L, XBC_WIDTH), F32),
                        pltpu.VMEM((SSD_STATE, SSD_WIDTH), F32)],
        compiler_params=_cparams(("arbitrary", "arbitrary")),
        name="ssd",
    )(xbc, z, dt, dt_t, conv_w.astype(F32), conv_b.astype(F32).reshape(1, XBC_WIDTH),
      pad_row(dt_bias), col8(dt_bias), pad_row(a_log), col8(a_log), dskip,
      ssd_norm_w.astype(F32).reshape(1, SSD_WIDTH), tril, triu)


OUT_TM = 256
ROUTE_W = 8
ROUTER_COLS = N_GROUPS + N_EXPERTS
RUN_ALIGN = 16
RUN_SHIFT = 4
LOCAL_ROWS = 1024
assert RUN_ALIGN == 1 << RUN_SHIFT and LOCAL_ROWS >= TOP_K * OUT_TM + N_EXPERTS * (RUN_ALIGN - 1)


def _lane_pick(values, lane, index):
    return jnp.sum(jnp.where(lane == index, values, 0.0), axis=-1, keepdims=True)


def _first_argmax(vals, lane):
    m = jnp.max(vals, axis=-1, keepdims=True)
    idx = jnp.min(jnp.where(vals == m, lane, float(LANES)), axis=-1, keepdims=True)
    return m, idx


def _out_router_kernel(att_ref, y_ref, x_ref, g1_ref, wo_ref, nw_ref, sc_ref, sh_ref, wr_ref, br_ref,
                       ltri_ref, sut_ref, x1_ref, h2_ref, route_ref, routet_ref, tcnt_ref, wr_split_ref):
    i = pl.program_id(0)

    @pl.when(i == 0)
    def _():
        hi, lo = _split_bf16(wr_ref[...])
        wr_split_ref[:, 0:LANES] = hi
        wr_split_ref[:, LANES:2 * LANES] = lo

    mixer = (jnp.dot(att_ref[...], wo_ref[0:ATT_WIDTH, :], preferred_element_type=F32)
             + jnp.dot(y_ref[...], wo_ref[ATT_WIDTH:ATT_WIDTH + SSD_WIDTH, :], preferred_element_type=F32))
    x1 = x_ref[...] + g1_ref[0] * mixer
    x1_ref[...] = x1
    yn = x1 * lax.rsqrt(jnp.mean(x1 * x1, axis=-1, keepdims=True) + EPS)
    h2 = (yn * nw_ref[...]) * (1.0 + sc_ref[0]) + sh_ref[0]
    h2_ref[...] = h2.astype(BF16)

    h_hi, h_lo = _split_bf16(h2)
    both = jnp.dot(h_hi, wr_split_ref[...], preferred_element_type=F32)
    logits = (both[:, 0:LANES] + both[:, LANES:2 * LANES]
              + jnp.dot(h_lo, wr_split_ref[:, 0:LANES], preferred_element_type=F32)) + br_ref[...]
    tm = logits.shape[0]
    lane = lax.broadcasted_iota(I32, (tm, LANES), 1).astype(F32)

    gl = jnp.where(lane < N_GROUPS, logits, NEG_BIG)
    gmax, gidx = _first_argmax(gl, lane)
    g_p = 1.0 / jnp.sum(jnp.exp(gl - gmax), axis=-1, keepdims=True)

    lo_lane = N_GROUPS + EXPERTS_PER_GROUP * gidx
    el = jnp.where((lane >= lo_lane) & (lane < lo_lane + EXPERTS_PER_GROUP), logits, NEG_BIG)
    m1, i1 = _first_argmax(el, lane)
    m2, i2 = _first_argmax(jnp.where(lane == i1, NEG_BIG, el), lane)
    r = jnp.exp(m2 - m1)
    p1 = 1.0 / (1.0 + r)
    p2 = r / (1.0 + r)
    e0 = i1 - N_GROUPS
    e1 = i2 - N_GROUPS

    onehot = ((lane == e0) | (lane == e1)).astype(F32)
    tile_cnt = jnp.sum(onehot, axis=0, keepdims=True)
    run_len = jnp.floor((tile_cnt + (RUN_ALIGN - 1)) * (1.0 / RUN_ALIGN)) * RUN_ALIGN
    run_start = jnp.dot(jnp.broadcast_to(run_len, (8, LANES)).astype(BF16), sut_ref[...],
                        preferred_element_type=F32)[0:1, :]
    before = jnp.dot(ltri_ref[...], onehot.astype(BF16), preferred_element_type=F32) + run_start
    slot0 = _lane_pick(before, lane, e0)
    slot1 = _lane_pick(before, lane, e1)
    tcnt_ref[0] = tile_cnt

    rec = jnp.zeros((tm, LANES), F32)
    for k, v in enumerate([slot0, slot1, g_p * p1, g_p * p2, e0, e1]):
        rec = jnp.where(lane == k, v, rec)
    route_ref[...] = rec[:, 0:ROUTE_W]
    routet_ref[...] = rec.T[0:ROUTE_W, :]


def _out_router(att, y, x2d, mod3, w_out_b, norm_w, w_router, b_router):
    tm = OUT_TM
    n_steps = TOKENS // tm
    steps_per_batch = SEQ // tm
    idx = jnp.arange(tm)
    ltri = (idx[None, :] < idx[:, None]).astype(BF16)
    lidx = jnp.arange(LANES)
    sut = (lidx[:, None] < lidx[None, :]).astype(BF16)
    const = lambda shape: pl.BlockSpec(shape, lambda i: (0, 0))
    tok = lambda width: pl.BlockSpec((tm, width), lambda i: (i, 0))
    modspec = lambda k: pl.BlockSpec((1, 1, D_MODEL), lambda i: ((i // steps_per_batch) * 6 + k, 0, 0))
    return pl.pallas_call(
        _out_router_kernel,
        grid=(n_steps,),
        in_specs=[tok(ATT_WIDTH), tok(SSD_WIDTH), tok(D_MODEL), modspec(2),
                  const((D_MODEL, D_MODEL)), const((1, D_MODEL)), modspec(4), modspec(3),
                  const((D_MODEL, LANES)), const((1, LANES)), const((tm, tm)), const((LANES, LANES))],
        out_specs=[tok(D_MODEL), tok(D_MODEL), tok(ROUTE_W),
                   pl.BlockSpec((ROUTE_W, tm), lambda i: (i, 0)),
                   pl.BlockSpec((1, 1, LANES), lambda i: (i, 0, 0))],
        out_shape=[jax.ShapeDtypeStruct((TOKENS, D_MODEL), F32),
                   jax.ShapeDtypeStruct((TOKENS, D_MODEL), BF16),
                   jax.ShapeDtypeStruct((TOKENS, ROUTE_W), F32),
                   jax.ShapeDtypeStruct((n_steps * ROUTE_W, tm), F32),
                   jax.ShapeDtypeStruct((n_steps, 1, LANES), F32)],
        scratch_shapes=[pltpu.VMEM((D_MODEL, 2 * LANES), BF16)],
        compiler_params=_cparams(("arbitrary",)),
        name="out_router",
    )(att, y, x2d, mod3, w_out_b, norm_w.reshape(1, D_MODEL), mod3, mod3, w_router, b_router, ltri, sut)


MOE_TM = 256
MAX_SORTED_ROWS = TOKENS * TOP_K + (TOKENS // OUT_TM) * N_EXPERTS * (RUN_ALIGN - 1)
N_TILES = MAX_SORTED_ROWS // MOE_TM + N_EXPERTS
N_ROWS = N_TILES * MOE_TM


def _run_loops(tc_ref, do_ref, make_copy):
    def sweep(action):
        def per_expert(e, local):
            pieces = lax.shift_right_logical(tc_ref[0, 0, e] + (RUN_ALIGN - 1), RUN_SHIFT)
            dst = do_ref[0, 0, e]

            def per_piece(p, carry):
                action(make_copy(pl.multiple_of(local + RUN_ALIGN * p, RUN_ALIGN),
                                 pl.multiple_of(dst + RUN_ALIGN * p, RUN_ALIGN)))
                return carry

            lax.fori_loop(0, pieces, per_piece, 0)
            return local + RUN_ALIGN * pieces

        lax.fori_loop(0, N_EXPERTS, per_expert, 0)

    sweep(lambda cp: cp.start())
    sweep(lambda cp: cp.wait())


def _dispatch_kernel(seg_end_ref, cnt_ref, tc_ref, do_ref, routet_ref, h2_ref, xs_ref,
                     sbuf_ref, zero_ref, sem, zsem):
    i = pl.program_id(0)

    @pl.when(i == 0)
    def _():
        zero_ref[...] = jnp.zeros_like(zero_ref)

        def zcopy(start):
            return pltpu.make_async_copy(zero_ref, xs_ref.at[pl.ds(pl.multiple_of(start, MOE_TM), MOE_TM)], zsem)

        def zstart(e, carry):
            @pl.when(cnt_ref[e] > 0)
            def _():
                zcopy(seg_end_ref[e] - MOE_TM).start()
            return carry

        def zwait(e, carry):
            @pl.when(cnt_ref[e] > 0)
            def _():
                zcopy(seg_end_ref[e] - MOE_TM).wait()
            return carry

        def tstart(t, carry):
            zcopy(t * MOE_TM).start()
            return carry

        def twait(t, carry):
            zcopy(t * MOE_TM).wait()
            return carry

        first_unused = seg_end_ref[N_EXPERTS - 1] // MOE_TM
        lax.fori_loop(0, N_EXPERTS, zstart, 0)
        lax.fori_loop(first_unused, N_TILES, tstart, 0)
        lax.fori_loop(0, N_EXPERTS, zwait, 0)
        lax.fori_loop(first_unused, N_TILES, twait, 0)

    slot = lax.broadcasted_iota(I32, (LOCAL_ROWS, OUT_TM), 0).astype(F32)
    perm = jnp.where((slot == routet_ref[0:1, :]) | (slot == routet_ref[1:2, :]), 1.0, 0.0).astype(BF16)
    sbuf_ref[...] = jnp.dot(perm, h2_ref[...], preferred_element_type=F32).astype(BF16)

    def piece(local, sorted_row):
        return pltpu.make_async_copy(sbuf_ref.at[pl.ds(local, RUN_ALIGN)],
                                     xs_ref.at[pl.ds(sorted_row, RUN_ALIGN)], sem)

    _run_loops(tc_ref, do_ref, piece)


def _smem_tile_spec(index_map):
    return pl.BlockSpec((1, 1, N_EXPERTS), index_map, memory_space=pltpu.SMEM)


def _dispatch(seg_end, counts, tc3, do3, route_t, h2p):
    grid_spec = pltpu.PrefetchScalarGridSpec(
        num_scalar_prefetch=2,
        grid=(TOKENS // OUT_TM,),
        in_specs=[_smem_tile_spec(lambda i, se, cn: (i, 0, 0)),
                  _smem_tile_spec(lambda i, se, cn: (i, 0, 0)),
                  pl.BlockSpec((ROUTE_W, OUT_TM), lambda i, se, cn: (i, 0)),
                  pl.BlockSpec((OUT_TM, D_MODEL), lambda i, se, cn: (i, 0))],
        out_specs=pl.BlockSpec(memory_space=pl.ANY),
        scratch_shapes=[pltpu.VMEM((LOCAL_ROWS, D_MODEL), BF16),
                        pltpu.VMEM((MOE_TM, D_MODEL), BF16),
                        pltpu.SemaphoreType.DMA, pltpu.SemaphoreType.DMA],
    )
    return pl.pallas_call(
        _dispatch_kernel,
        grid_spec=grid_spec,
        out_shape=jax.ShapeDtypeStruct((N_ROWS, D_MODEL), BF16),
        compiler_params=_cparams(("arbitrary",)),
        name="dispatch",
    )(seg_end, counts, tc3, do3, route_t, h2p)


def _experts_kernel(te_ref, nu_ref, xs_ref, wg_ref, wu_ref, wd_ref, ys_ref, wgu_b_ref, wd_b_ref):
    i = pl.program_id(0)
    used = i < nu_ref[0]

    @pl.when(used & ((i == 0) | (te_ref[i] != te_ref[jnp.maximum(i - 1, 0)])))
    def _():
        wgu_b_ref[:, 0:D_EXPERT] = wg_ref[0].astype(BF16)
        wgu_b_ref[:, D_EXPERT:2 * D_EXPERT] = wu_ref[0].astype(BF16)
        wd_b_ref[...] = wd_ref[0].astype(BF16)

    @pl.when(used)
    def _():
        h = jnp.dot(xs_ref[...], wgu_b_ref[...], preferred_element_type=F32)
        act = (_silu(h[:, 0:D_EXPERT]) * h[:, D_EXPERT:2 * D_EXPERT]).astype(BF16)
        ys_ref[...] = jnp.dot(act, wd_b_ref[...], preferred_element_type=F32).astype(BF16)

    @pl.when(jnp.logical_not(used))
    def _():
        ys_ref[...] = jnp.zeros_like(ys_ref)


def _experts(tile_expert, n_used, xs, w_gate, w_up, w_down):
    row_tile = lambda i, te, nu: (jnp.minimum(i, nu[0] - 1), 0)
    grid_spec = pltpu.PrefetchScalarGridSpec(
        num_scalar_prefetch=2,
        grid=(N_TILES,),
        in_specs=[pl.BlockSpec((MOE_TM, D_MODEL), row_tile),
                  pl.BlockSpec((1, D_MODEL, D_EXPERT), lambda i, te, nu: (te[i], 0, 0)),
                  pl.BlockSpec((1, D_MODEL, D_EXPERT), lambda i, te, nu: (te[i], 0, 0)),
                  pl.BlockSpec((1, D_EXPERT, D_MODEL), lambda i, te, nu: (te[i], 0, 0))],
        out_specs=pl.BlockSpec((MOE_TM, D_MODEL), lambda i, te, nu: (i, 0)),
        scratch_shapes=[pltpu.VMEM((D_MODEL, 2 * D_EXPERT), BF16), pltpu.VMEM((D_EXPERT, D_MODEL), BF16)],
    )
    return pl.pallas_call(
        _experts_kernel,
        grid_spec=grid_spec,
        out_shape=jax.ShapeDtypeStruct((N_ROWS, D_MODEL), BF16),
        compiler_params=_cparams(("arbitrary",)),
        name="experts",
    )(tile_expert, n_used, xs, w_gate, w_up, w_down)


def _combine_kernel(tc_ref, do_ref, route_ref, x1_ref, g2_ref, ys_ref, o_ref, gbuf_ref, sem):
    i = pl.program_id(0)

    @pl.when(i == 0)
    def _():
        gbuf_ref[...] = jnp.zeros_like(gbuf_ref)

    def piece(local, sorted_row):
        return pltpu.make_async_copy(ys_ref.at[pl.ds(sorted_row, RUN_ALIGN)],
                                     gbuf_ref.at[pl.ds(local, RUN_ALIGN)], sem)

    _run_loops(tc_ref, do_ref, piece)

    rec = route_ref[...]
    slot = lax.broadcasted_iota(I32, (OUT_TM, LOCAL_ROWS), 1).astype(F32)
    weights = (jnp.where(slot == rec[:, 0:1], rec[:, 2:3], 0.0)
               + jnp.where(slot == rec[:, 1:2], rec[:, 3:4], 0.0)).astype(BF16)
    moe = jnp.dot(weights, gbuf_ref[...], preferred_element_type=F32)
    o_ref[...] = x1_ref[...] + g2_ref[0] * moe


def _combine(tc3, do3, route, x1, mod3, ys):
    tm = OUT_TM
    steps_per_batch = SEQ // tm
    return pl.pallas_call(
        _combine_kernel,
        grid=(TOKENS // tm,),
        in_specs=[_smem_tile_spec(lambda i: (i, 0, 0)),
                  _smem_tile_spec(lambda i: (i, 0, 0)),
                  pl.BlockSpec((tm, ROUTE_W), lambda i: (i, 0)),
                  pl.BlockSpec((tm, D_MODEL), lambda i: (i, 0)),
                  pl.BlockSpec((1, 1, D_MODEL), lambda i: ((i // steps_per_batch) * 6 + 5, 0, 0)),
                  pl.BlockSpec(memory_space=pl.ANY)],
        out_specs=pl.BlockSpec((tm, D_MODEL), lambda i: (i, 0)),
        out_shape=jax.ShapeDtypeStruct((TOKENS, D_MODEL), F32),
        scratch_shapes=[pltpu.VMEM((LOCAL_ROWS, D_MODEL), BF16), pltpu.SemaphoreType.DMA],
        compiler_params=_cparams(("arbitrary",)),
        name="combine",
    )(tc3, do3, route, x1, mod3, ys)


def kernel(x, c, positions, norm1_w, norm2_w, w_ada, b_ada, w_in, conv_w, conv_b, dt_bias, a_log,
           d_skip, ssd_norm_w, q_norm_w, k_norm_w, sinks, w_out, w_group, b_group, w_expert, b_expert,
           w_gate, w_up, w_down):
    assert x.shape == (BATCH, SEQ, D_MODEL) and w_in.shape == (D_MODEL, IN_WIDTH)
    x2d = x.reshape(TOKENS, D_MODEL)
    mod = _ada_mod(c, w_ada, b_ada)
    mod3 = mod.reshape(BATCH * 6, 1, D_MODEL)

    w_in_pad = jnp.pad(w_in, ((0, 0), (0, IN_PAD - IN_WIDTH))).astype(BF16)
    qkv, z, xbc, dt = _in_proj(x2d, norm1_w, mod3, w_in_pad)
    att = _attention(qkv, positions, q_norm_w, k_norm_w, sinks)
    y = _ssd(xbc, z, dt, conv_w, conv_b, dt_bias, a_log, d_skip, ssd_norm_w)

    w_router = jnp.pad(jnp.concatenate([w_group, w_expert], axis=1).astype(F32),
                       ((0, 0), (0, LANES - ROUTER_COLS)))
    b_router = jnp.pad(jnp.concatenate([b_group, b_expert]).astype(F32),
                       (0, LANES - ROUTER_COLS)).reshape(1, LANES)
    x1, h2p, route, route_t, tcnt = _out_router(att, y, x2d, mod3, w_out.astype(BF16), norm2_w,
                                                w_router, b_router)

    tc = tcnt[:, 0, 0:N_EXPERTS].astype(I32)
    run_rows = ((tc + RUN_ALIGN - 1) // RUN_ALIGN) * RUN_ALIGN
    counts = jnp.sum(run_rows, axis=0)
    padded = ((counts + MOE_TM - 1) // MOE_TM) * MOE_TM
    seg_end = jnp.cumsum(padded)
    seg_start = seg_end - padded
    run_dst = seg_start[None, :] + jnp.cumsum(run_rows, axis=0) - run_rows
    n_used = (seg_end[-1] // MOE_TM).reshape(1)
    last_row = jnp.minimum(jnp.arange(N_TILES, dtype=I32) * MOE_TM, seg_end[-1] - 1)
    tile_expert = jnp.sum((seg_end[None, :] <= last_row[:, None]).astype(I32), axis=1)
    tc3 = tc.reshape(-1, 1, N_EXPERTS)
    do3 = run_dst.astype(I32).reshape(-1, 1, N_EXPERTS)

    xs = _dispatch(seg_end.astype(I32), counts, tc3, do3, route_t, h2p)
    ys = _experts(tile_expert, n_used.astype(I32), xs, w_gate, w_up, w_down)
    out = _combine(tc3, do3, route, x1, mod3, ys)
    return out.reshape(BATCH, SEQ, D_MODEL)
```

```python
import jax
import jax.numpy as jnp
from jax import lax
from jax.experimental import pallas as pl
from jax.experimental.pallas import tpu as pltpu

F32 = jnp.float32
BF16 = jnp.bfloat16
I32 = jnp.int32

D_MODEL = 1024
BATCH = 2
SEQ = 8192
TOKENS = BATCH * SEQ
ATT_HEADS = 8
ATT_KV_HEADS = 2
HEAD_DIM = 64
ATT_WIDTH = ATT_HEADS * HEAD_DIM
KV_WIDTH = ATT_KV_HEADS * HEAD_DIM
ATT_BLOCK = 128
ROPE_DIM = HEAD_DIM // 4
ROPE_THETA = 500000.0
SSD_HEADS = 8
SSD_HEAD_DIM = 64
SSD_WIDTH = SSD_HEADS * SSD_HEAD_DIM
SSD_GROUPS = 2
SSD_STATE = 128
CONV_K = 4
CHUNK = 128
XBC_WIDTH = SSD_WIDTH + 2 * SSD_GROUPS * SSD_STATE
IN_WIDTH = ATT_WIDTH + 2 * KV_WIDTH + SSD_WIDTH + XBC_WIDTH + SSD_HEADS
N_GROUPS = 4
EXPERTS_PER_GROUP = 8
N_EXPERTS = N_GROUPS * EXPERTS_PER_GROUP
TOP_K = 2
D_EXPERT = 256
EPS = 1e-6

LANES = 128
QKV_WIDTH = ATT_WIDTH + 2 * KV_WIDTH
IN_PAD = QKV_WIDTH + SSD_WIDTH + XBC_WIDTH + LANES
NEG_BIG = -1e30

VMEM_LIMIT = 48 * 1024 * 1024


def _cparams(sem):
    return pltpu.CompilerParams(dimension_semantics=sem, vmem_limit_bytes=VMEM_LIMIT)


def _split_bf16(x):
    hi = x.astype(BF16)
    lo = (x - hi.astype(F32)).astype(BF16)
    return hi, lo


ADA_TN = 768


def _ada_kernel(ct_ref, w_ref, b_ref, o_ref):
    ct = ct_ref[...]
    s = ct * jax.nn.sigmoid(ct)
    w = w_ref[...]
    rows = [jnp.sum(s[:, b:b + 1] * w, axis=0, keepdims=True) for b in range(BATCH)]
    o_ref[...] = jnp.concatenate(rows, axis=0) + b_ref[...]


def _ada_mod(c, w_ada, b_ada):
    n = w_ada.shape[1]
    return pl.pallas_call(
        _ada_kernel,
        grid=(n // ADA_TN,),
        in_specs=[pl.BlockSpec((D_MODEL, BATCH), lambda j: (0, 0)),
                  pl.BlockSpec((D_MODEL, ADA_TN), lambda j: (0, j)),
                  pl.BlockSpec((1, ADA_TN), lambda j: (0, j))],
        out_specs=pl.BlockSpec((BATCH, ADA_TN), lambda j: (0, j)),
        out_shape=jax.ShapeDtypeStruct((BATCH, n), F32),
        compiler_params=_cparams(("arbitrary",)),
        name="ada_mod",
    )(c.T, w_ada, b_ada.reshape(1, n))


INPROJ_TM = 256
_INPROJ_CHUNK = 256


def _inproj_kernel(x_ref, nw_ref, sc_ref, sh_ref, w_ref, qkv_ref, z_ref, xbc_ref, dt_ref):
    x = x_ref[...]
    y = x * lax.rsqrt(jnp.mean(x * x, axis=-1, keepdims=True) + EPS)
    h = (y * nw_ref[...]) * (1.0 + sc_ref[0]) + sh_ref[0]
    hb = h.astype(BF16)

    def proj(c0, c1):
        return jnp.dot(hb, w_ref[:, c0:c1], preferred_element_type=F32)

    for c0 in range(0, QKV_WIDTH, _INPROJ_CHUNK):
        qkv_ref[:, c0:c0 + _INPROJ_CHUNK] = proj(c0, c0 + _INPROJ_CHUNK).astype(BF16)
    base = QKV_WIDTH
    for c0 in range(0, SSD_WIDTH, _INPROJ_CHUNK):
        z_ref[:, c0:c0 + _INPROJ_CHUNK] = proj(base + c0, base + c0 + _INPROJ_CHUNK).astype(BF16)
    base += SSD_WIDTH
    for c0 in range(0, XBC_WIDTH, _INPROJ_CHUNK):
        xbc_ref[:, c0:c0 + _INPROJ_CHUNK] = proj(base + c0, base + c0 + _INPROJ_CHUNK).astype(BF16)
    base += XBC_WIDTH
    dt_ref[...] = proj(base, base + LANES)


def _in_proj(x2d, norm_w, mod3, w_in_pad):
    tm = INPROJ_TM
    steps_per_batch = SEQ // tm
    return pl.pallas_call(
        _inproj_kernel,
        grid=(TOKENS // tm,),
        in_specs=[pl.BlockSpec((tm, D_MODEL), lambda i: (i, 0)),
                  pl.BlockSpec((1, D_MODEL), lambda i: (0, 0)),
                  pl.BlockSpec((1, 1, D_MODEL), lambda i: ((i // steps_per_batch) * 6 + 1, 0, 0)),
                  pl.BlockSpec((1, 1, D_MODEL), lambda i: ((i // steps_per_batch) * 6 + 0, 0, 0)),
                  pl.BlockSpec((D_MODEL, IN_PAD), lambda i: (0, 0))],
        out_specs=[pl.BlockSpec((tm, QKV_WIDTH), lambda i: (i, 0)),
                   pl.BlockSpec((tm, SSD_WIDTH), lambda i: (i, 0)),
                   pl.BlockSpec((tm, XBC_WIDTH), lambda i: (i, 0)),
                   pl.BlockSpec((tm, LANES), lambda i: (i, 0))],
        out_shape=[jax.ShapeDtypeStruct((TOKENS, QKV_WIDTH), BF16),
                   jax.ShapeDtypeStruct((TOKENS, SSD_WIDTH), BF16),
                   jax.ShapeDtypeStruct((TOKENS, XBC_WIDTH), BF16),
                   jax.ShapeDtypeStruct((TOKENS, LANES), F32)],
        compiler_params=_cparams(("arbitrary",)),
        name="in_proj",
    )(x2d, norm_w.reshape(1, D_MODEL), mod3, mod3, w_in_pad)


def _seg_meansq(xf, ones_bd):
    hi, lo = _split_bf16(xf * xf)
    tot = (jnp.dot(hi, ones_bd, preferred_element_type=F32)
           + jnp.dot(lo, ones_bd, preferred_element_type=F32))
    return tot * (1.0 / HEAD_DIM)


def _norm_rope(x_bf, w_row, ones_bd, cosf, s1, s2):
    xf = x_bf.astype(F32)
    width = xf.shape[1]
    xn = xf * lax.rsqrt(_seg_meansq(xf, ones_bd) + EPS) * w_row
    half = ROPE_DIM // 2
    up = pltpu.roll(xn, width - half, axis=1)
    down = pltpu.roll(xn, half, axis=1)
    return xn * cosf + up * s1 + down * s2


def _attn_kernel(sink_ref, pos_ref, q_ref, kv_ref, qw_ref, kw_ref, freq_ref, m1_ref, m2_ref,
                 ones_ref, o_ref, kprev_ref, vprev_ref):
    j = pl.program_id(1)
    blk = ATT_BLOCK

    @pl.when(j == 0)
    def _():
        kprev_ref[...] = jnp.zeros_like(kprev_ref)
        vprev_ref[...] = jnp.zeros_like(vprev_ref)

    ang = pos_ref[...].astype(F32) * freq_ref[...]
    cos1 = jnp.cos(ang)
    sin1 = jnp.sin(ang)
    s1_1 = -sin1 * m1_ref[...]
    s2_1 = sin1 * m2_ref[...]
    reps = ATT_WIDTH // LANES
    cosq = jnp.concatenate([cos1] * reps, axis=1)
    s1q = jnp.concatenate([s1_1] * reps, axis=1)
    s2q = jnp.concatenate([s2_1] * reps, axis=1)

    q = _norm_rope(q_ref[...], qw_ref[...], ones_ref[...], cosq, s1q, s2q)
    qb = (q * (HEAD_DIM ** -0.5)).astype(BF16)
    kv = kv_ref[...]
    kn = _norm_rope(kv[:, 0:KV_WIDTH], kw_ref[...], ones_ref[0:KV_WIDTH, 0:KV_WIDTH],
                    cos1, s1_1, s2_1)
    vn = kv[:, KV_WIDTH:2 * KV_WIDTH].astype(F32)

    kcat = jnp.concatenate([kprev_ref[...], kn], axis=0)
    vcat = jnp.concatenate([vprev_ref[...], vn], axis=0)
    kprev_ref[...] = kn
    vprev_ref[...] = vn

    lane = lax.broadcasted_iota(I32, (2 * blk, LANES), 1)
    lo_half = lane < HEAD_DIM
    zero = jnp.zeros_like(kcat)

    def variants(t, g):
        own = jnp.where(lo_half if g == 0 else ~lo_half, t, zero)
        other = pltpu.roll(own, HEAD_DIM, axis=1)
        lo, hi = (own, other) if g == 0 else (other, own)
        return lo.astype(BF16), hi.astype(BF16)

    row = lax.broadcasted_iota(I32, (2 * blk, 2 * blk), 0)
    col = lax.broadcasted_iota(I32, (2 * blk, 2 * blk), 1)
    qi = jnp.where(row >= blk, row - blk, row)
    no_prev = jnp.where(j > 0, 0, 2 * blk)
    valid = ((col < blk) & (col > qi + no_prev)) | ((col >= blk) & ((col - blk) <= qi))
    bias = jnp.where(valid, 0.0, NEG_BIG).astype(F32)
    first = lax.broadcasted_iota(I32, (2 * blk, 1), 0) < blk

    def probs(qcat, kvar, head_a, head_b):
        s = lax.dot_general(qcat, kvar, (((1,), (1,)), ((), ())), preferred_element_type=F32) + bias
        sink = jnp.where(first, sink_ref[head_a], sink_ref[head_b])
        m = jnp.maximum(jnp.max(s, axis=-1, keepdims=True), sink)
        p = jnp.exp(s - m)
        denom = jnp.sum(p, axis=-1, keepdims=True) + jnp.exp(sink - m)
        return (p * (1.0 / denom)).astype(BF16)

    for g in range(ATT_KV_HEADS):
        k_lo, k_hi = variants(kcat, g)
        v_lo, v_hi = variants(vcat, g)
        c0 = g * 2 * LANES
        qcat = jnp.concatenate([qb[:, c0:c0 + LANES], qb[:, c0 + LANES:c0 + 2 * LANES]], axis=0)
        h0 = 4 * g
        p_lo = probs(qcat, k_lo, h0 + 0, h0 + 2)
        p_hi = probs(qcat, k_hi, h0 + 1, h0 + 3)
        out = (jnp.dot(p_lo, v_lo, preferred_element_type=F32)
               + jnp.dot(p_hi, v_hi, preferred_element_type=F32))
        o_ref[:, c0:c0 + LANES] = out[0:blk].astype(BF16)
        o_ref[:, c0 + LANES:c0 + 2 * LANES] = out[blk:2 * blk].astype(BF16)


def _rope_tables():
    half = ROPE_DIM // 2
    lane = jnp.arange(LANES)
    d = lane % HEAD_DIM
    inv_freq = jnp.power(ROPE_THETA, -(d % half).astype(F32) * 2.0 / ROPE_DIM)
    freq = jnp.where(d < ROPE_DIM, inv_freq, 0.0).astype(F32).reshape(1, LANES)
    m1 = (d < half).astype(F32).reshape(1, LANES)
    m2 = ((d >= half) & (d < ROPE_DIM)).astype(F32).reshape(1, LANES)
    seg = jnp.arange(ATT_WIDTH) // HEAD_DIM
    ones_bd = (seg[:, None] == seg[None, :]).astype(BF16)
    return freq, m1, m2, ones_bd


def _attention(qkv, positions, q_norm_w, k_norm_w, sinks):
    nb = SEQ // ATT_BLOCK
    freq, m1, m2, ones_bd = _rope_tables()
    qw = jnp.tile(q_norm_w.astype(F32), ATT_HEADS).reshape(1, ATT_WIDTH)
    kw = jnp.tile(k_norm_w.astype(F32), ATT_KV_HEADS).reshape(1, KV_WIDTH)
    pos = positions.reshape(TOKENS, 1).astype(I32)
    const = lambda shape: pl.BlockSpec(shape, lambda b, j, s: (0, 0))
    grid_spec = pltpu.PrefetchScalarGridSpec(
        num_scalar_prefetch=1,
        grid=(BATCH, nb),
        in_specs=[pl.BlockSpec((ATT_BLOCK, 1), lambda b, j, s: (b * nb + j, 0)),
                  pl.BlockSpec((ATT_BLOCK, ATT_WIDTH), lambda b, j, s: (b * nb + j, 0)),
                  pl.BlockSpec((ATT_BLOCK, 2 * KV_WIDTH), lambda b, j, s: (b * nb + j, 2)),
                  const((1, ATT_WIDTH)), const((1, KV_WIDTH)),
                  const((1, LANES)), const((1, LANES)), const((1, LANES)),
                  const((ATT_WIDTH, ATT_WIDTH))],
        out_specs=pl.BlockSpec((ATT_BLOCK, ATT_WIDTH), lambda b, j, s: (b * nb + j, 0)),
        scratch_shapes=[pltpu.VMEM((ATT_BLOCK, KV_WIDTH), F32),
                        pltpu.VMEM((ATT_BLOCK, KV_WIDTH), F32)],
    )
    return pl.pallas_call(
        _attn_kernel,
        grid_spec=grid_spec,
        out_shape=jax.ShapeDtypeStruct((TOKENS, ATT_WIDTH), BF16),
        compiler_params=_cparams(("arbitrary", "arbitrary")),
        name="attention",
    )(sinks.astype(F32), pos, qkv, qkv, qw, kw, freq, m1, m2, ones_bd)


def _softplus(x):
    return jnp.maximum(x, 0.0) + jnp.log1p(jnp.exp(-jnp.abs(x)))


def _silu(x):
    return x * jax.nn.sigmoid(x)


def _ssd_kernel(xbc_ref, z_ref, dt_ref, dtt_ref, cw_ref, cb_ref, dtb_row_ref, dtb_col_ref,
                alog_row_ref, alog_col_ref, dskip_ref, nw_ref, tril_ref, triu_ref,
                o_ref, conv_ref, state_ref):
    c = pl.program_id(1)
    L = CHUNK
    tail = 8

    @pl.when(c == 0)
    def _():
        conv_ref[0:tail, :] = jnp.zeros((tail, XBC_WIDTH), F32)
        state_ref[...] = jnp.zeros_like(state_ref)

    xb = xbc_ref[...].astype(F32)
    conv_ref[tail:tail + L, :] = xb
    acc = cb_ref[...] + cw_ref[CONV_K - 1:CONV_K, :] * xb
    for k in range(CONV_K - 1):
        off = tail - (CONV_K - 1) + k
        acc = acc + cw_ref[k:k + 1, :] * conv_ref[off:off + L, :]
    conv_ref[0:tail, :] = xb[L - tail:L, :]
    u = _silu(acc)
    xs = u[:, 0:SSD_WIDTH]
    bmat = u[:, SSD_WIDTH:SSD_WIDTH + SSD_GROUPS * SSD_STATE]
    cmat = u[:, SSD_WIDTH + SSD_GROUPS * SSD_STATE:XBC_WIDTH]

    dt = _softplus(dt_ref[...] + dtb_row_ref[...])
    a = dt * (-jnp.exp(alog_row_ref[...]))
    a_hi, a_lo = _split_bf16(a)
    a_cum = (jnp.dot(tril_ref[...], a_hi, preferred_element_type=F32)
             + jnp.dot(tril_ref[...], a_lo, preferred_element_type=F32))
    dt_t = _softplus(dtt_ref[...] + dtb_col_ref[...])
    a_t = dt_t * (-jnp.exp(alog_col_ref[...]))
    at_hi, at_lo = _split_bf16(a_t)
    a_cum_t = (jnp.dot(at_hi, triu_ref[...], preferred_element_type=F32)
               + jnp.dot(at_lo, triu_ref[...], preferred_element_type=F32))
    a_end_t = a_cum_t[:, L - 1:L]
    wst_t = jnp.exp(a_end_t - a_cum_t) * dt_t
    cdec_t = jnp.exp(a_end_t)

    row = lax.broadcasted_iota(I32, (L, L), 0)
    col = lax.broadcasted_iota(I32, (L, L), 1)
    causal = col <= row
    lane = lax.broadcasted_iota(I32, (L, LANES), 1)
    lo_half = lane < SSD_HEAD_DIM

    xs_b = xs.astype(BF16)
    heads_per_group = SSD_HEADS // SSD_GROUPS
    gated = []
    for g in range(SSD_GROUPS):
        b_g = bmat[:, g * SSD_STATE:(g + 1) * SSD_STATE]
        c_g = cmat[:, g * SSD_STATE:(g + 1) * SSD_STATE]
        cb = lax.dot_general(c_g.astype(BF16), b_g.astype(BF16), (((1,), (1,)), ((), ())),
                             preferred_element_type=F32)
        b_gt = b_g.T
        for t in range(heads_per_group // 2):
            tile = g * (heads_per_group // 2) + t
            c0 = tile * LANES
            xs_tile = xs_b[:, c0:c0 + LANES]
            st_tile = state_ref[:, c0:c0 + LANES]
            st_b = st_tile.astype(BF16)
            y_tile = jnp.zeros((L, LANES), F32)
            new_tile = jnp.zeros((SSD_STATE, LANES), F32)
            for e in range(2):
                h = 2 * tile + e
                keep = lo_half if e == 0 else ~lo_half
                colb = jnp.broadcast_to(a_cum[:, h:h + 1], (L, L))
                rowb = a_cum_t[h:h + 1, :]
                decay = jnp.exp(jnp.where(causal, colb - rowb, NEG_BIG))
                w_in = (cb * decay) * dt_t[h:h + 1, :]
                w_off = c_g * jnp.exp(colb)
                lhs = jnp.concatenate([w_in, w_off], axis=1).astype(BF16)
                rhs = jnp.concatenate([jnp.where(keep, xs_tile, jnp.zeros_like(xs_tile)),
                                       jnp.where(keep, st_b, jnp.zeros_like(st_b))], axis=0)
                y_tile = y_tile + jnp.dot(lhs, rhs, preferred_element_type=F32)
                m_h = (b_gt * wst_t[h:h + 1, :]).astype(BF16)
                new_tile = new_tile + jnp.dot(m_h, jnp.where(keep, xs_tile, jnp.zeros_like(xs_tile)),
                                              preferred_element_type=F32)
            cd = jnp.where(lo_half[0:1, :], cdec_t[2 * tile:2 * tile + 1, :],
                           cdec_t[2 * tile + 1:2 * tile + 2, :])
            state_ref[:, c0:c0 + LANES] = st_tile * cd + new_tile
            y_full = y_tile + dskip_ref[:, c0:c0 + LANES] * xs[:, c0:c0 + LANES]
            gated.append(y_full * _silu(z_ref[:, c0:c0 + LANES].astype(F32)))

    gw = SSD_WIDTH // SSD_GROUPS
    tiles_per_group = gw // LANES
    for g in range(SSD_GROUPS):
        yg = jnp.concatenate(gated[g * tiles_per_group:(g + 1) * tiles_per_group], axis=1)
        ms = jnp.mean(yg * yg, axis=-1, keepdims=True)
        o_ref[:, g * gw:(g + 1) * gw] = ((yg * lax.rsqrt(ms + EPS)) * nw_ref[:, g * gw:(g + 1) * gw]).astype(o_ref.dtype)


def _ssd(xbc, z, dt, conv_w, conv_b, dt_bias, a_log, d_skip, ssd_norm_w):
    nc = SEQ // CHUNK
    L = CHUNK
    dt_t = dt[:, 0:SSD_HEADS].T
    pad_row = lambda v: jnp.pad(v.astype(F32), (0, LANES - SSD_HEADS)).reshape(1, LANES)
    col8 = lambda v: v.astype(F32).reshape(SSD_HEADS, 1)
    idx = jnp.arange(L)
    tril = (idx[None, :] <= idx[:, None]).astype(BF16)
    triu = (idx[:, None] <= idx[None, :]).astype(BF16)
    dskip = jnp.repeat(d_skip.astype(F32), SSD_HEAD_DIM).reshape(1, SSD_WIDTH)
    const = lambda shape: pl.BlockSpec(shape, lambda b, c: (0, 0))
    tok = lambda width: pl.BlockSpec((L, width), lambda b, c: (b * nc + c, 0))
    return pl.pallas_call(
        _ssd_kernel,
        grid=(BATCH, nc),
        in_specs=[tok(XBC_WIDTH), tok(SSD_WIDTH), tok(LANES),
                  pl.BlockSpec((SSD_HEADS, L), lambda b, c: (0, b * nc + c)),
                  const((CONV_K, XBC_WIDTH)), const((1, XBC_WIDTH)),
                  const((1, LANES)), const((SSD_HEADS, 1)), const((1, LANES)), const((SSD_HEADS, 1)),
                  const((1, SSD_WIDTH)), const((1, SSD_WIDTH)), const((L, L)), const((L, L))],
        out_specs=tok(SSD_WIDTH),
        out_shape=jax.ShapeDtypeStruct((TOKENS, SSD_WIDTH), BF16),
        scratch_shapes=[pltpu.VMEM((8 + L, XBC_WIDTH), F32),
                        pltpu.VMEM((SSD_STATE, SSD_WIDTH), F32)],
        compiler_params=_cparams(("arbitrary", "arbitrary")),
        name="ssd",
    )(xbc, z, dt, dt_t, conv_w.astype(F32), conv_b.astype(F32).reshape(1, XBC_WIDTH),
      pad_row(dt_bias), col8(dt_bias), pad_row(a_log), col8(a_log), dskip,
      ssd_norm_w.astype(F32).reshape(1, SSD_WIDTH), tril, triu)


OUT_TM = 256
ROUTE_W = 8
ROUTER_COLS = N_GROUPS + N_EXPERTS
RUN_ALIGN = 16
RUN_SHIFT = 4
LOCAL_ROWS = 1024
assert RUN_ALIGN == 1 << RUN_SHIFT and LOCAL_ROWS >= TOP_K * OUT_TM + N_EXPERTS * (RUN_ALIGN - 1)


def _lane_pick(values, lane, index):
    return jnp.sum(jnp.where(lane == index, values, 0.0), axis=-1, keepdims=True)


def _first_argmax(vals, lane):
    m = jnp.max(vals, axis=-1, keepdims=True)
    idx = jnp.min(jnp.where(vals == m, lane, float(LANES)), axis=-1, keepdims=True)
    return m, idx


def _out_router_kernel(att_ref, y_ref, x_ref, g1_ref, wo_ref, nw_ref, sc_ref, sh_ref, wr_ref, br_ref,
                       ltri_ref, sut_ref, x1_ref, h2_ref, route_ref, routet_ref, tcnt_ref, wr_split_ref):
    i = pl.program_id(0)

    @pl.when(i == 0)
    def _():
        hi, lo = _split_bf16(wr_ref[...])
        wr_split_ref[:, 0:LANES] = hi
        wr_split_ref[:, LANES:2 * LANES] = lo

    mixer = (jnp.dot(att_ref[...], wo_ref[0:ATT_WIDTH, :], preferred_element_type=F32)
             + jnp.dot(y_ref[...], wo_ref[ATT_WIDTH:ATT_WIDTH + SSD_WIDTH, :], preferred_element_type=F32))
    x1 = x_ref[...] + g1_ref[0] * mixer
    x1_ref[...] = x1
    yn = x1 * lax.rsqrt(jnp.mean(x1 * x1, axis=-1, keepdims=True) + EPS)
    h2 = (yn * nw_ref[...]) * (1.0 + sc_ref[0]) + sh_ref[0]
    h2_ref[...] = h2.astype(BF16)

    h_hi, h_lo = _split_bf16(h2)
    both = jnp.dot(h_hi, wr_split_ref[...], preferred_element_type=F32)
    logits = (both[:, 0:LANES] + both[:, LANES:2 * LANES]
              + jnp.dot(h_lo, wr_split_ref[:, 0:LANES], preferred_element_type=F32)) + br_ref[...]
    tm = logits.shape[0]
    lane = lax.broadcasted_iota(I32, (tm, LANES), 1).astype(F32)

    gl = jnp.where(lane < N_GROUPS, logits, NEG_BIG)
    gmax, gidx = _first_argmax(gl, lane)
    g_p = 1.0 / jnp.sum(jnp.exp(gl - gmax), axis=-1, keepdims=True)

    lo_lane = N_GROUPS + EXPERTS_PER_GROUP * gidx
    el = jnp.where((lane >= lo_lane) & (lane < lo_lane + EXPERTS_PER_GROUP), logits, NEG_BIG)
    m1, i1 = _first_argmax(el, lane)
    m2, i2 = _first_argmax(jnp.where(lane == i1, NEG_BIG, el), lane)
    r = jnp.exp(m2 - m1)
    p1 = 1.0 / (1.0 + r)
    p2 = r / (1.0 + r)
    e0 = i1 - N_GROUPS
    e1 = i2 - N_GROUPS

    onehot = ((lane == e0) | (lane == e1)).astype(F32)
    tile_cnt = jnp.sum(onehot, axis=0, keepdims=True)
    run_len = jnp.floor((tile_cnt + (RUN_ALIGN - 1)) * (1.0 / RUN_ALIGN)) * RUN_ALIGN
    run_start = jnp.dot(jnp.broadcast_to(run_len, (8, LANES)).astype(BF16), sut_ref[...],
                        preferred_element_type=F32)[0:1, :]
    before = jnp.dot(ltri_ref[...], onehot.astype(BF16), preferred_element_type=F32) + run_start
    slot0 = _lane_pick(before, lane, e0)
    slot1 = _lane_pick(before, lane, e1)
    tcnt_ref[0] = tile_cnt

    rec = jnp.zeros((tm, LANES), F32)
    for k, v in enumerate([slot0, slot1, g_p * p1, g_p * p2, e0, e1]):
        rec = jnp.where(lane == k, v, rec)
    route_ref[...] = rec[:, 0:ROUTE_W]
    routet_ref[...] = rec.T[0:ROUTE_W, :]


def _out_router(att, y, x2d, mod3, w_out_b, norm_w, w_router, b_router):
    tm = OUT_TM
    n_steps = TOKENS // tm
    steps_per_batch = SEQ // tm
    idx = jnp.arange(tm)
    ltri = (idx[None, :] < idx[:, None]).astype(BF16)
    lidx = jnp.arange(LANES)
    sut = (lidx[:, None] < lidx[None, :]).astype(BF16)
    const = lambda shape: pl.BlockSpec(shape, lambda i: (0, 0))
    tok = lambda width: pl.BlockSpec((tm, width), lambda i: (i, 0))
    modspec = lambda k: pl.BlockSpec((1, 1, D_MODEL), lambda i: ((i // steps_per_batch) * 6 + k, 0, 0))
    return pl.pallas_call(
        _out_router_kernel,
        grid=(n_steps,),
        in_specs=[tok(ATT_WIDTH), tok(SSD_WIDTH), tok(D_MODEL), modspec(2),
                  const((D_MODEL, D_MODEL)), const((1, D_MODEL)), modspec(4), modspec(3),
                  const((D_MODEL, LANES)), const((1, LANES)), const((tm, tm)), const((LANES, LANES))],
        out_specs=[tok(D_MODEL), tok(D_MODEL), tok(ROUTE_W),
                   pl.BlockSpec((ROUTE_W, tm), lambda i: (i, 0)),
                   pl.BlockSpec((1, 1, LANES), lambda i: (i, 0, 0))],
        out_shape=[jax.ShapeDtypeStruct((TOKENS, D_MODEL), F32),
                   jax.ShapeDtypeStruct((TOKENS, D_MODEL), BF16),
                   jax.ShapeDtypeStruct((TOKENS, ROUTE_W), F32),
                   jax.ShapeDtypeStruct((n_steps * ROUTE_W, tm), F32),
                   jax.ShapeDtypeStruct((n_steps, 1, LANES), F32)],
        scratch_shapes=[pltpu.VMEM((D_MODEL, 2 * LANES), BF16)],
        compiler_params=_cparams(("arbitrary",)),
        name="out_router",
    )(att, y, x2d, mod3, w_out_b, norm_w.reshape(1, D_MODEL), mod3, mod3, w_router, b_router, ltri, sut)


MOE_TM = 512
ZERO_ROWS = 256
N_TOKEN_TILES = TOKENS // OUT_TM
MAX_PIECES = LOCAL_ROWS // RUN_ALIGN
MAX_SORTED_ROWS = TOKENS * TOP_K + N_TOKEN_TILES * N_EXPERTS * (RUN_ALIGN - 1)
N_TILES = MAX_SORTED_ROWS // MOE_TM + N_EXPERTS
N_ROWS = N_TILES * MOE_TM
assert MOE_TM % ZERO_ROWS == 0


def _for_pieces(count, action):
    def body(j, carry):
        action(j)
        return carry

    lax.fori_loop(0, count, body, 0)


def _dispatch_kernel(seg_end_ref, cnt_ref, np_ref, dst_ref, routet_ref, h2_ref, xs_ref,
                     sbuf_ref, zero_ref, sems, zsem):
    i = pl.program_id(0)
    last = pl.num_programs(0) - 1
    buf = lax.rem(i, 2)

    @pl.when(i == 0)
    def _():
        zero_ref[...] = jnp.zeros_like(zero_ref)
        blocks_per_tile = MOE_TM // ZERO_ROWS

        def zcopy(block):
            start = pl.multiple_of(block * ZERO_ROWS, ZERO_ROWS)
            return pltpu.make_async_copy(zero_ref, xs_ref.at[pl.ds(start, ZERO_ROWS)], zsem)

        def tails(action):
            def body(e, carry):
                @pl.when(cnt_ref[e] > 0)
                def _():
                    for b in range(blocks_per_tile):
                        action(zcopy(seg_end_ref[e] // ZERO_ROWS - blocks_per_tile + b))
                return carry

            lax.fori_loop(0, N_EXPERTS, body, 0)

        def unused(action):
            def body(block, carry):
                action(zcopy(block))
                return carry

            lax.fori_loop(seg_end_ref[N_EXPERTS - 1] // ZERO_ROWS, N_ROWS // ZERO_ROWS, body, 0)

        tails(lambda cp: cp.start())
        unused(lambda cp: cp.start())
        tails(lambda cp: cp.wait())
        unused(lambda cp: cp.wait())

    slot = lax.broadcasted_iota(I32, (LOCAL_ROWS, OUT_TM), 0).astype(F32)
    perm = jnp.where((slot == routet_ref[0:1, :]) | (slot == routet_ref[1:2, :]), 1.0, 0.0).astype(BF16)
    sbuf_ref[buf] = jnp.dot(perm, h2_ref[...], preferred_element_type=F32).astype(BF16)

    def piece(b, local_piece, sorted_row):
        local = pl.multiple_of(local_piece * RUN_ALIGN, RUN_ALIGN)
        return pltpu.make_async_copy(sbuf_ref.at[b, pl.ds(local, RUN_ALIGN)],
                                     xs_ref.at[pl.ds(pl.multiple_of(sorted_row, RUN_ALIGN), RUN_ALIGN)],
                                     sems.at[b])

    _for_pieces(np_ref[i], lambda j: piece(buf, j, dst_ref[0, 0, j]).start())

    @pl.when(i > 0)
    def _():
        _for_pieces(np_ref[jnp.maximum(i - 1, 0)], lambda j: piece(1 - buf, j, 0).wait())

    @pl.when(i == last)
    def _():
        _for_pieces(np_ref[i], lambda j: piece(buf, j, 0).wait())


def _piece_spec(index_map):
    return pl.BlockSpec((1, 1, MAX_PIECES), index_map, memory_space=pltpu.SMEM)


def _dispatch(seg_end, counts, n_pieces, piece_dst, route_t, h2):
    grid_spec = pltpu.PrefetchScalarGridSpec(
        num_scalar_prefetch=3,
        grid=(N_TOKEN_TILES,),
        in_specs=[_piece_spec(lambda i, se, cn, npc: (i, 0, 0)),
                  pl.BlockSpec((ROUTE_W, OUT_TM), lambda i, se, cn, npc: (i, 0)),
                  pl.BlockSpec((OUT_TM, D_MODEL), lambda i, se, cn, npc: (i, 0))],
        out_specs=pl.BlockSpec(memory_space=pl.ANY),
        scratch_shapes=[pltpu.VMEM((2, LOCAL_ROWS, D_MODEL), BF16),
                        pltpu.VMEM((ZERO_ROWS, D_MODEL), BF16),
                        pltpu.SemaphoreType.DMA((2,)), pltpu.SemaphoreType.DMA],
    )
    return pl.pallas_call(
        _dispatch_kernel,
        grid_spec=grid_spec,
        out_shape=jax.ShapeDtypeStruct((N_ROWS, D_MODEL), BF16),
        compiler_params=_cparams(("arbitrary",)),
        name="dispatch",
    )(seg_end, counts, n_pieces, piece_dst, route_t, h2)


def _experts_kernel(te_ref, nu_ref, xs_ref, wg_ref, wu_ref, wd_ref, ys_ref, wgu_b_ref, wd_b_ref):
    i = pl.program_id(0)
    used = i < nu_ref[0]

    @pl.when(used & ((i == 0) | (te_ref[i] != te_ref[jnp.maximum(i - 1, 0)])))
    def _():
        wgu_b_ref[:, 0:D_EXPERT] = wg_ref[0].astype(BF16)
        wgu_b_ref[:, D_EXPERT:2 * D_EXPERT] = wu_ref[0].astype(BF16)
        wd_b_ref[...] = wd_ref[0].astype(BF16)

    @pl.when(used)
    def _():
        h = jnp.dot(xs_ref[...], wgu_b_ref[...], preferred_element_type=F32)
        act = (_silu(h[:, 0:D_EXPERT]) * h[:, D_EXPERT:2 * D_EXPERT]).astype(BF16)
        ys_ref[...] = jnp.dot(act, wd_b_ref[...], preferred_element_type=F32).astype(BF16)

    @pl.when(jnp.logical_not(used))
    def _():
        ys_ref[...] = jnp.zeros_like(ys_ref)


def _experts(tile_expert, n_used, xs, w_gate, w_up, w_down):
    row_tile = lambda i, te, nu: (jnp.minimum(i, nu[0] - 1), 0)
    grid_spec = pltpu.PrefetchScalarGridSpec(
        num_scalar_prefetch=2,
        grid=(N_TILES,),
        in_specs=[pl.BlockSpec((MOE_TM, D_MODEL), row_tile),
                  pl.BlockSpec((1, D_MODEL, D_EXPERT), lambda i, te, nu: (te[i], 0, 0)),
                  pl.BlockSpec((1, D_MODEL, D_EXPERT), lambda i, te, nu: (te[i], 0, 0)),
                  pl.BlockSpec((1, D_EXPERT, D_MODEL), lambda i, te, nu: (te[i], 0, 0))],
        out_specs=pl.BlockSpec((MOE_TM, D_MODEL), lambda i, te, nu: (i, 0)),
        scratch_shapes=[pltpu.VMEM((D_MODEL, 2 * D_EXPERT), BF16), pltpu.VMEM((D_EXPERT, D_MODEL), BF16)],
    )
    return pl.pallas_call(
        _experts_kernel,
        grid_spec=grid_spec,
        out_shape=jax.ShapeDtypeStruct((N_ROWS, D_MODEL), BF16),
        compiler_params=_cparams(("arbitrary",)),
        name="experts",
    )(tile_expert, n_used, xs, w_gate, w_up, w_down)


def _combine_kernel(np_ref, dst_ref, dst_next_ref, route_ref, x1_ref, g2_ref, ys_ref, o_ref, gbuf_ref, sems):
    i = pl.program_id(0)
    last = pl.num_programs(0) - 1
    buf = lax.rem(i, 2)

    def piece(b, local_piece, sorted_row):
        local = pl.multiple_of(local_piece * RUN_ALIGN, RUN_ALIGN)
        return pltpu.make_async_copy(ys_ref.at[pl.ds(pl.multiple_of(sorted_row, RUN_ALIGN), RUN_ALIGN)],
                                     gbuf_ref.at[b, pl.ds(local, RUN_ALIGN)], sems.at[b])

    @pl.when(i == 0)
    def _():
        gbuf_ref[...] = jnp.zeros_like(gbuf_ref)
        _for_pieces(np_ref[0], lambda j: piece(0, j, dst_ref[0, 0, j]).start())

    @pl.when(i < last)
    def _():
        _for_pieces(np_ref[jnp.minimum(i + 1, last)], lambda j: piece(1 - buf, j, dst_next_ref[0, 0, j]).start())

    rec = route_ref[...]
    slot = lax.broadcasted_iota(I32, (OUT_TM, LOCAL_ROWS), 1).astype(F32)
    weights = (jnp.where(slot == rec[:, 0:1], rec[:, 2:3], 0.0)
               + jnp.where(slot == rec[:, 1:2], rec[:, 3:4], 0.0)).astype(BF16)
    _for_pieces(np_ref[i], lambda j: piece(buf, j, 0).wait())
    moe = jnp.dot(weights, gbuf_ref[buf], preferred_element_type=F32)
    o_ref[...] = x1_ref[...] + g2_ref[0] * moe


def _combine(n_pieces, piece_dst, route, x1, mod3, ys):
    tm = OUT_TM
    steps_per_batch = SEQ // tm
    grid_spec = pltpu.PrefetchScalarGridSpec(
        num_scalar_prefetch=1,
        grid=(N_TOKEN_TILES,),
        in_specs=[_piece_spec(lambda i, npc: (i, 0, 0)),
                  _piece_spec(lambda i, npc: (jnp.minimum(i + 1, N_TOKEN_TILES - 1), 0, 0)),
                  pl.BlockSpec((tm, ROUTE_W), lambda i, npc: (i, 0)),
                  pl.BlockSpec((tm, D_MODEL), lambda i, npc: (i, 0)),
                  pl.BlockSpec((1, 1, D_MODEL), lambda i, npc: ((i // steps_per_batch) * 6 + 5, 0, 0)),
                  pl.BlockSpec(memory_space=pl.ANY)],
        out_specs=pl.BlockSpec((tm, D_MODEL), lambda i, npc: (i, 0)),
        scratch_shapes=[pltpu.VMEM((2, LOCAL_ROWS, D_MODEL), BF16), pltpu.SemaphoreType.DMA((2,))],
    )
    return pl.pallas_call(
        _combine_kernel,
        grid_spec=grid_spec,
        out_shape=jax.ShapeDtypeStruct((TOKENS, D_MODEL), F32),
        compiler_params=_cparams(("arbitrary",)),
        name="combine",
    )(n_pieces, piece_dst, piece_dst, route, x1, mod3, ys)


def kernel(x, c, positions, norm1_w, norm2_w, w_ada, b_ada, w_in, conv_w, conv_b, dt_bias, a_log,
           d_skip, ssd_norm_w, q_norm_w, k_norm_w, sinks, w_out, w_group, b_group, w_expert, b_expert,
           w_gate, w_up, w_down):
    assert x.shape == (BATCH, SEQ, D_MODEL) and w_in.shape == (D_MODEL, IN_WIDTH)
    x2d = x.reshape(TOKENS, D_MODEL)
    mod = _ada_mod(c, w_ada, b_ada)
    mod3 = mod.reshape(BATCH * 6, 1, D_MODEL)

    w_in_pad = jnp.pad(w_in, ((0, 0), (0, IN_PAD - IN_WIDTH))).astype(BF16)
    qkv, z, xbc, dt = _in_proj(x2d, norm1_w, mod3, w_in_pad)
    att = _attention(qkv, positions, q_norm_w, k_norm_w, sinks)
    y = _ssd(xbc, z, dt, conv_w, conv_b, dt_bias, a_log, d_skip, ssd_norm_w)

    w_router = jnp.pad(jnp.concatenate([w_group, w_expert], axis=1).astype(F32),
                       ((0, 0), (0, LANES - ROUTER_COLS)))
    b_router = jnp.pad(jnp.concatenate([b_group, b_expert]).astype(F32),
                       (0, LANES - ROUTER_COLS)).reshape(1, LANES)
    x1, h2, route, route_t, tcnt = _out_router(att, y, x2d, mod3, w_out.astype(BF16), norm2_w,
                                                w_router, b_router)

    tc = tcnt[:, 0, 0:N_EXPERTS].astype(I32)
    run_rows = ((tc + RUN_ALIGN - 1) // RUN_ALIGN) * RUN_ALIGN
    counts = jnp.sum(run_rows, axis=0)
    padded = ((counts + MOE_TM - 1) // MOE_TM) * MOE_TM
    seg_end = jnp.cumsum(padded)
    seg_start = seg_end - padded
    run_dst = seg_start[None, :] + jnp.cumsum(run_rows, axis=0) - run_rows
    n_used = (seg_end[-1] // MOE_TM).reshape(1)
    last_row = jnp.minimum(jnp.arange(N_TILES, dtype=I32) * MOE_TM, seg_end[-1] - 1)
    tile_expert = jnp.sum((seg_end[None, :] <= last_row[:, None]).astype(I32), axis=1)

    run_pieces = run_rows // RUN_ALIGN
    piece_end = jnp.cumsum(run_pieces, axis=1)
    n_pieces = piece_end[:, -1]
    j = jnp.arange(MAX_PIECES, dtype=I32)
    piece_expert = jnp.sum((piece_end[:, None, :] <= j[None, :, None]).astype(I32), axis=2)
    in_run = (piece_expert[:, :, None] == jnp.arange(N_EXPERTS, dtype=I32)[None, None, :]).astype(I32)
    run_base = run_dst - RUN_ALIGN * (piece_end - run_pieces)
    piece_dst = jnp.sum(in_run * run_base[:, None, :], axis=2) + RUN_ALIGN * j[None, :]
    piece_dst = piece_dst.astype(I32).reshape(N_TOKEN_TILES, 1, MAX_PIECES)

    xs = _dispatch(seg_end.astype(I32), counts, n_pieces.astype(I32), piece_dst, route_t, h2)
    ys = _experts(tile_expert, n_used.astype(I32), xs, w_gate, w_up, w_down)
    out = _combine(n_pieces.astype(I32), piece_dst, route, x1, mod3, ys)
    return out.reshape(BATCH, SEQ, D_MODEL)
```

```python
import jax
import jax.numpy as jnp
from jax import lax
from jax.experimental import pallas as pl
from jax.experimental.pallas import tpu as pltpu

F32 = jnp.float32
BF16 = jnp.bfloat16
I32 = jnp.int32

D_MODEL = 1024
BATCH = 2
SEQ = 8192
TOKENS = BATCH * SEQ
ATT_HEADS = 8
ATT_KV_HEADS = 2
HEAD_DIM = 64
ATT_WIDTH = ATT_HEADS * HEAD_DIM
KV_WIDTH = ATT_KV_HEADS * HEAD_DIM
ATT_BLOCK = 128
ROPE_DIM = HEAD_DIM // 4
ROPE_THETA = 500000.0
SSD_HEADS = 8
SSD_HEAD_DIM = 64
SSD_WIDTH = SSD_HEADS * SSD_HEAD_DIM
SSD_GROUPS = 2
SSD_STATE = 128
CONV_K = 4
CHUNK = 128
XBC_WIDTH = SSD_WIDTH + 2 * SSD_GROUPS * SSD_STATE
IN_WIDTH = ATT_WIDTH + 2 * KV_WIDTH + SSD_WIDTH + XBC_WIDTH + SSD_HEADS
N_GROUPS = 4
EXPERTS_PER_GROUP = 8
N_EXPERTS = N_GROUPS * EXPERTS_PER_GROUP
TOP_K = 2
D_EXPERT = 256
EPS = 1e-6

LANES = 128
QKV_WIDTH = ATT_WIDTH + 2 * KV_WIDTH
IN_PAD = QKV_WIDTH + SSD_WIDTH + XBC_WIDTH + LANES
NEG_BIG = -1e30

VMEM_LIMIT = 48 * 1024 * 1024


def _cparams(sem):
    return pltpu.CompilerParams(dimension_semantics=sem, vmem_limit_bytes=VMEM_LIMIT)


def _split_bf16(x):
    hi = x.astype(BF16)
    lo = (x - hi.astype(F32)).astype(BF16)
    return hi, lo


ADA_TN = 768


def _ada_kernel(ct_ref, w_ref, b_ref, o_ref):
    ct = ct_ref[...]
    s = ct * jax.nn.sigmoid(ct)
    w = w_ref[...]
    rows = [jnp.sum(s[:, b:b + 1] * w, axis=0, keepdims=True) for b in range(BATCH)]
    o_ref[...] = jnp.concatenate(rows, axis=0) + b_ref[...]


def _ada_mod(c, w_ada, b_ada):
    n = w_ada.shape[1]
    return pl.pallas_call(
        _ada_kernel,
        grid=(n // ADA_TN,),
        in_specs=[pl.BlockSpec((D_MODEL, BATCH), lambda j: (0, 0)),
                  pl.BlockSpec((D_MODEL, ADA_TN), lambda j: (0, j)),
                  pl.BlockSpec((1, ADA_TN), lambda j: (0, j))],
        out_specs=pl.BlockSpec((BATCH, ADA_TN), lambda j: (0, j)),
        out_shape=jax.ShapeDtypeStruct((BATCH, n), F32),
        compiler_params=_cparams(("arbitrary",)),
        name="ada_mod",
    )(c.T, w_ada, b_ada.reshape(1, n))


INPROJ_TM = 512
_INPROJ_CHUNK = 256


def _inproj_kernel(x_ref, nw_ref, sc_ref, sh_ref, w_ref, qkv_ref, z_ref, xbc_ref, dt_ref):
    x = x_ref[...]
    y = x * lax.rsqrt(jnp.mean(x * x, axis=-1, keepdims=True) + EPS)
    h = (y * nw_ref[...]) * (1.0 + sc_ref[0]) + sh_ref[0]
    hb = h.astype(BF16)

    def proj(c0, c1):
        return jnp.dot(hb, w_ref[:, c0:c1], preferred_element_type=F32)

    for c0 in range(0, QKV_WIDTH, _INPROJ_CHUNK):
        qkv_ref[:, c0:c0 + _INPROJ_CHUNK] = proj(c0, c0 + _INPROJ_CHUNK).astype(BF16)
    base = QKV_WIDTH
    for c0 in range(0, SSD_WIDTH, _INPROJ_CHUNK):
        z_ref[:, c0:c0 + _INPROJ_CHUNK] = proj(base + c0, base + c0 + _INPROJ_CHUNK).astype(BF16)
    base += SSD_WIDTH
    for c0 in range(0, XBC_WIDTH, _INPROJ_CHUNK):
        xbc_ref[:, c0:c0 + _INPROJ_CHUNK] = proj(base + c0, base + c0 + _INPROJ_CHUNK).astype(BF16)
    base += XBC_WIDTH
    dt_ref[...] = proj(base, base + LANES)


def _in_proj(x2d, norm_w, mod3, w_in_pad):
    tm = INPROJ_TM
    steps_per_batch = SEQ // tm
    return pl.pallas_call(
        _inproj_kernel,
        grid=(TOKENS // tm,),
        in_specs=[pl.BlockSpec((tm, D_MODEL), lambda i: (i, 0)),
                  pl.BlockSpec((1, D_MODEL), lambda i: (0, 0)),
                  pl.BlockSpec((1, 1, D_MODEL), lambda i: ((i // steps_per_batch) * 6 + 1, 0, 0)),
                  pl.BlockSpec((1, 1, D_MODEL), lambda i: ((i // steps_per_batch) * 6 + 0, 0, 0)),
                  pl.BlockSpec((D_MODEL, IN_PAD), lambda i: (0, 0))],
        out_specs=[pl.BlockSpec((tm, QKV_WIDTH), lambda i: (i, 0)),
                   pl.BlockSpec((tm, SSD_WIDTH), lambda i: (i, 0)),
                   pl.BlockSpec((tm, XBC_WIDTH), lambda i: (i, 0)),
                   pl.BlockSpec((tm, LANES), lambda i: (i, 0))],
        out_shape=[jax.ShapeDtypeStruct((TOKENS, QKV_WIDTH), BF16),
                   jax.ShapeDtypeStruct((TOKENS, SSD_WIDTH), BF16),
                   jax.ShapeDtypeStruct((TOKENS, XBC_WIDTH), BF16),
                   jax.ShapeDtypeStruct((TOKENS, LANES), F32)],
        compiler_params=_cparams(("arbitrary",)),
        name="in_proj",
    )(x2d, norm_w.reshape(1, D_MODEL), mod3, mod3, w_in_pad)


ATT_SUB = 2


def _rope_kernel(pos_ref, freq_ref, cos_ref, sin_ref, nsin_ref):
    ang = pos_ref[...].astype(F32) * freq_ref[...]
    sin = jnp.sin(ang)
    cos_ref[...] = jnp.cos(ang)
    sin_ref[...] = sin
    nsin_ref[...] = -sin


def _rope_tables(positions):
    half = ROPE_DIM // 2
    per_row = LANES // half
    rows = TOKENS // per_row
    pos_rep = jnp.repeat(positions.reshape(TOKENS).astype(I32), half).reshape(rows, LANES)
    inv_freq = jnp.power(ROPE_THETA, -jnp.arange(half, dtype=F32) * 2.0 / ROPE_DIM)
    freq = jnp.tile(inv_freq, per_row).reshape(1, LANES)
    full = pl.BlockSpec((rows, LANES), lambda i: (0, 0))
    out = jax.ShapeDtypeStruct((rows, LANES), F32)
    tables = pl.pallas_call(
        _rope_kernel,
        grid=(1,),
        in_specs=[full, pl.BlockSpec((1, LANES), lambda i: (0, 0))],
        out_specs=[full, full, full],
        out_shape=[out, out, out],
        compiler_params=_cparams(("arbitrary",)),
        name="rope_tables",
    )(pos_rep, freq)
    return [t.reshape(TOKENS, half) for t in tables]


def _seg_meansq(xf, ones128):
    rows, width = xf.shape
    nt = width // LANES
    parts = _split_bf16(xf * xf)
    stacked = jnp.concatenate([p[:, t * LANES:(t + 1) * LANES] for p in parts for t in range(nt)], axis=0)
    tot = jnp.dot(stacked, ones128, preferred_element_type=F32)
    tiles = [tot[t * rows:(t + 1) * rows] + tot[(nt + t) * rows:(nt + t + 1) * rows] for t in range(nt)]
    return jnp.concatenate(tiles, axis=1) * (1.0 / HEAD_DIM)


def _norm_rope(x_bf, w_row, ones_bd, cosf, s1, s2):
    xf = x_bf.astype(F32)
    width = xf.shape[1]
    xn = xf * lax.rsqrt(_seg_meansq(xf, ones_bd) + EPS) * w_row
    half = ROPE_DIM // 2
    up = pltpu.roll(xn, width - half, axis=1)
    down = pltpu.roll(xn, half, axis=1)
    return xn * cosf + up * s1 + down * s2


def _attn_kernel(sink_ref, q_ref, kv_ref, cos_ref, s1_ref, s2_ref, qw_ref, kw_ref,
                 ones_ref, o_ref, kprev_ref, vprev_ref):
    j = pl.program_id(1)
    blk = ATT_BLOCK

    @pl.when(j == 0)
    def _():
        kprev_ref[...] = jnp.zeros_like(kprev_ref)
        vprev_ref[...] = jnp.zeros_like(vprev_ref)

    cos1 = cos_ref[...]
    s1_1 = s1_ref[...]
    s2_1 = s2_ref[...]
    reps = ATT_WIDTH // LANES
    cosq = jnp.concatenate([cos1] * reps, axis=1)
    s1q = jnp.concatenate([s1_1] * reps, axis=1)
    s2q = jnp.concatenate([s2_1] * reps, axis=1)

    q = _norm_rope(q_ref[...], qw_ref[...], ones_ref[...], cosq, s1q, s2q)
    qf = q * (HEAD_DIM ** -0.5)
    kv = kv_ref[...]
    kn = _norm_rope(kv[:, 0:KV_WIDTH], kw_ref[...], ones_ref[...], cos1, s1_1, s2_1)
    vn = kv[:, KV_WIDTH:2 * KV_WIDTH].astype(F32)

    kall = jnp.concatenate([kprev_ref[...], kn], axis=0)
    vall = jnp.concatenate([vprev_ref[...], vn], axis=0)
    kprev_ref[...] = kn[(ATT_SUB - 1) * blk:ATT_SUB * blk]
    vprev_ref[...] = vn[(ATT_SUB - 1) * blk:ATT_SUB * blk]

    lo_all = lax.broadcasted_iota(I32, kall.shape, 1) < HEAD_DIM
    ones_all = jnp.ones(kall.shape, BF16)

    row = lax.broadcasted_iota(I32, (2 * blk, blk), 0)
    col = lax.broadcasted_iota(I32, (2 * blk, blk), 1)
    from_prev = col > (row & (blk - 1))
    second_tile = lax.broadcasted_iota(I32, (2 * blk, 1), 0) >= blk
    zero_p = jnp.zeros((2 * blk, blk), F32)

    k_par, v_par = [], []
    for g in range(ATT_KV_HEADS):
        keep = lo_all if g == 0 else ~lo_all
        k_own = jnp.where(keep, kall, 0.0)
        v_own = jnp.where(keep, vall, 0.0)
        k_oth = pltpu.roll(k_own, HEAD_DIM, axis=1)
        v_oth = pltpu.roll(v_own, HEAD_DIM, axis=1)
        k_lo, k_hi = (k_own, k_oth) if g == 0 else (k_oth, k_own)
        v_lo, v_hi = (v_own, v_oth) if g == 0 else (v_oth, v_own)
        k_par.append((k_lo.astype(BF16), k_hi.astype(BF16)))
        v_par.append((jnp.concatenate([v_lo.astype(BF16), ones_all], axis=1),
                      jnp.concatenate([v_hi.astype(BF16), ones_all], axis=1)))

    problems = [(g, sub) for g in range(ATT_KV_HEADS) for sub in range(ATT_SUB)]
    scores = []
    for g, sub in problems:
        r0, c0 = sub * blk, g * 2 * LANES
        qcat = jnp.concatenate([qf[r0:r0 + blk, c0:c0 + LANES],
                                qf[r0:r0 + blk, c0 + LANES:c0 + 2 * LANES]], axis=0).astype(BF16)
        kw = jnp.concatenate([k_par[g][0][r0:r0 + 2 * blk], k_par[g][1][r0:r0 + 2 * blk]], axis=0)
        scores.append(lax.dot_general(qcat, kw, (((1,), (1,)), ((), ())),
                                      preferred_element_type=F32))

    weights, rescale = [], []
    for (g, sub), s_all in zip(problems, scores):
        for par in range(2):
            s = s_all[:, par * 2 * blk:(par + 1) * 2 * blk]
            s_prev = s[:, 0:blk]
            if sub == 0:
                s_prev = s_prev + jnp.where(j > 0, 0.0, NEG_BIG)
            s = jnp.where(from_prev, s_prev, s[:, blk:2 * blk])
            h_first = ATT_HEADS // ATT_KV_HEADS * g + par
            sink = jnp.where(second_tile, sink_ref[h_first + 2], sink_ref[h_first])
            m = jnp.maximum(jnp.max(s, axis=-1, keepdims=True), sink)
            p = jnp.exp(s - m)
            weights.append(jnp.concatenate([jnp.where(from_prev, p, zero_p), jnp.where(from_prev, zero_p, p)],
                                           axis=1).astype(BF16))
            rescale.append(jnp.exp(sink - m))

    outs = []
    for idx, (g, sub) in enumerate(problems):
        for par in range(2):
            outs.append(jnp.dot(weights[2 * idx + par], v_par[g][par][sub * blk:(sub + 2) * blk],
                                preferred_element_type=F32))

    for idx, (g, sub) in enumerate(problems):
        r0, c0 = sub * blk, g * 2 * LANES
        pair = None
        for par in range(2):
            o = outs[2 * idx + par]
            part = o[:, 0:LANES] * (1.0 / (o[:, LANES:2 * LANES] + rescale[2 * idx + par]))
            pair = part if pair is None else pair + part
        o_ref[r0:r0 + blk, c0:c0 + LANES] = pair[0:blk].astype(BF16)
        o_ref[r0:r0 + blk, c0 + LANES:c0 + 2 * LANES] = pair[blk:2 * blk].astype(BF16)


def _attention(qkv, positions, q_norm_w, k_norm_w, sinks):
    half = ROPE_DIM // 2
    cos8, sin8, nsin8 = _rope_tables(positions)
    one = jnp.ones((TOKENS, HEAD_DIM - ROPE_DIM), F32)
    zero = lambda n: jnp.zeros((TOKENS, n), F32)
    per_head = lambda parts: jnp.tile(jnp.concatenate(parts, axis=1), (1, LANES // HEAD_DIM))
    cosf = per_head([cos8, cos8, one])
    s1 = per_head([nsin8, zero(HEAD_DIM - half)])
    s2 = per_head([zero(half), sin8, zero(HEAD_DIM - ROPE_DIM)])
    qw = jnp.tile(q_norm_w.astype(F32), ATT_HEADS).reshape(1, ATT_WIDTH)
    kw = jnp.tile(k_norm_w.astype(F32), ATT_KV_HEADS).reshape(1, KV_WIDTH)
    seg = jnp.arange(LANES) // HEAD_DIM
    ones128 = (seg[:, None] == seg[None, :]).astype(BF16)
    const = lambda shape: pl.BlockSpec(shape, lambda b, j, s: (0, 0))
    rows = ATT_SUB * ATT_BLOCK
    nb = SEQ // rows
    tok = lambda width, cb: pl.BlockSpec((rows, width), lambda b, j, s: (b * nb + j, cb))
    grid_spec = pltpu.PrefetchScalarGridSpec(
        num_scalar_prefetch=1,
        grid=(BATCH, nb),
        in_specs=[tok(ATT_WIDTH, 0), tok(2 * KV_WIDTH, 2), tok(LANES, 0), tok(LANES, 0), tok(LANES, 0),
                  const((1, ATT_WIDTH)), const((1, KV_WIDTH)), const((LANES, LANES))],
        out_specs=tok(ATT_WIDTH, 0),
        scratch_shapes=[pltpu.VMEM((ATT_BLOCK, KV_WIDTH), F32),
                        pltpu.VMEM((ATT_BLOCK, KV_WIDTH), F32)],
    )
    return pl.pallas_call(
        _attn_kernel,
        grid_spec=grid_spec,
        out_shape=jax.ShapeDtypeStruct((TOKENS, ATT_WIDTH), BF16),
        compiler_params=_cparams(("arbitrary", "arbitrary")),
        name="attention",
    )(sinks.astype(F32), qkv, qkv, cosf, s1, s2, qw, kw, ones128)


def _softplus(x):
    return jnp.maximum(x, 0.0) + jnp.log1p(jnp.exp(-jnp.abs(x)))


def _silu(x):
    return x * jax.nn.sigmoid(x)


def _ssd_kernel(xbc_ref, z_ref, dt_ref, dtt_ref, cw_ref, cb_ref, dtb_row_ref, dtb_col_ref,
                alog_row_ref, alog_col_ref, dskip_ref, nw_ref, tril_ref, triu_ref,
                o_ref, conv_ref, state_ref):
    c = pl.program_id(1)
    L = CHUNK
    tail = 8

    @pl.when(c == 0)
    def _():
        conv_ref[0:tail, :] = jnp.zeros((tail, XBC_WIDTH), F32)
        state_ref[...] = jnp.zeros_like(state_ref)

    xb = xbc_ref[...].astype(F32)
    conv_ref[tail:tail + L, :] = xb
    acc = cb_ref[...] + cw_ref[CONV_K - 1:CONV_K, :] * xb
    for k in range(CONV_K - 1):
        off = tail - (CONV_K - 1) + k
        acc = acc + cw_ref[k:k + 1, :] * conv_ref[off:off + L, :]
    conv_ref[0:tail, :] = xb[L - tail:L, :]
    u = _silu(acc)
    xs = u[:, 0:SSD_WIDTH]
    bmat = u[:, SSD_WIDTH:SSD_WIDTH + SSD_GROUPS * SSD_STATE]
    cmat = u[:, SSD_WIDTH + SSD_GROUPS * SSD_STATE:XBC_WIDTH]

    dt = _softplus(dt_ref[...] + dtb_row_ref[...])
    a = dt * (-jnp.exp(alog_row_ref[...]))
    a_hi, a_lo = _split_bf16(a)
    a_cum = (jnp.dot(tril_ref[...], a_hi, preferred_element_type=F32)
             + jnp.dot(tril_ref[...], a_lo, preferred_element_type=F32))
    dt_t = _softplus(dtt_ref[...] + dtb_col_ref[...])
    a_t = dt_t * (-jnp.exp(alog_col_ref[...]))
    at_hi, at_lo = _split_bf16(a_t)
    a_cum_t = (jnp.dot(at_hi, triu_ref[...], preferred_element_type=F32)
               + jnp.dot(at_lo, triu_ref[...], preferred_element_type=F32))
    a_end_t = a_cum_t[:, L - 1:L]
    wst_t = jnp.exp(a_end_t - a_cum_t) * dt_t
    cdec_t = jnp.exp(a_end_t)

    row = lax.broadcasted_iota(I32, (L, L), 0)
    col = lax.broadcasted_iota(I32, (L, L), 1)
    causal = col <= row
    lane = lax.broadcasted_iota(I32, (L, LANES), 1)
    lo_half = lane < SSD_HEAD_DIM

    xs_b = xs.astype(BF16)
    heads_per_group = SSD_HEADS // SSD_GROUPS
    gated = []
    for g in range(SSD_GROUPS):
        b_g = bmat[:, g * SSD_STATE:(g + 1) * SSD_STATE]
        c_g = cmat[:, g * SSD_STATE:(g + 1) * SSD_STATE]
        cb = lax.dot_general(c_g.astype(BF16), b_g.astype(BF16), (((1,), (1,)), ((), ())),
                             preferred_element_type=F32)
        b_gt = b_g.T
        for t in range(heads_per_group // 2):
            tile = g * (heads_per_group // 2) + t
            c0 = tile * LANES
            xs_tile = xs_b[:, c0:c0 + LANES]
            st_tile = state_ref[:, c0:c0 + LANES]
            st_b = st_tile.astype(BF16)
            y_tile = jnp.zeros((L, LANES), F32)
            new_tile = jnp.zeros((SSD_STATE, LANES), F32)
            for e in range(2):
                h = 2 * tile + e
                keep = lo_half if e == 0 else ~lo_half
                colb = jnp.broadcast_to(a_cum[:, h:h + 1], (L, L))
                rowb = a_cum_t[h:h + 1, :]
                decay = jnp.exp(jnp.where(causal, colb - rowb, NEG_BIG))
                w_in = (cb * decay) * dt_t[h:h + 1, :]
                w_off = c_g * jnp.exp(colb)
                lhs = jnp.concatenate([w_in, w_off], axis=1).astype(BF16)
                rhs = jnp.concatenate([jnp.where(keep, xs_tile, jnp.zeros_like(xs_tile)),
                                       jnp.where(keep, st_b, jnp.zeros_like(st_b))], axis=0)
                y_tile = y_tile + jnp.dot(lhs, rhs, preferred_element_type=F32)
                m_h = (b_gt * wst_t[h:h + 1, :]).astype(BF16)
                new_tile = new_tile + jnp.dot(m_h, jnp.where(keep, xs_tile, jnp.zeros_like(xs_tile)),
                                              preferred_element_type=F32)
            cd = jnp.where(lo_half[0:1, :], cdec_t[2 * tile:2 * tile + 1, :],
                           cdec_t[2 * tile + 1:2 * tile + 2, :])
            state_ref[:, c0:c0 + LANES] = st_tile * cd + new_tile
            y_full = y_tile + dskip_ref[:, c0:c0 + LANES] * xs[:, c0:c0 + LANES]
            gated.append(y_full * _silu(z_ref[:, c0:c0 + LANES].astype(F32)))

    gw = SSD_WIDTH // SSD_GROUPS
    tiles_per_group = gw // LANES
    for g in range(SSD_GROUPS):
        yg = jnp.concatenate(gated[g * tiles_per_group:(g + 1) * tiles_per_group], axis=1)
        ms = jnp.mean(yg * yg, axis=-1, keepdims=True)
        o_ref[:, g * gw:(g + 1) * gw] = ((yg * lax.rsqrt(ms + EPS)) * nw_ref[:, g * gw:(g + 1) * gw]).astype(o_ref.dtype)


def _ssd(xbc, z, dt, conv_w, conv_b, dt_bias, a_log, d_skip, ssd_norm_w):
    nc = SEQ // CHUNK
    L = CHUNK
    dt_t = dt[:, 0:SSD_HEADS].T
    pad_row = lambda v: jnp.pad(v.astype(F32), (0, LANES - SSD_HEADS)).reshape(1, LANES)
    col8 = lambda v: v.astype(F32).reshape(SSD_HEADS, 1)
    idx = jnp.arange(L)
    tril = (idx[None, :] <= idx[:, None]).astype(BF16)
    triu = (idx[:, None] <= idx[None, :]).astype(BF16)
    dskip = jnp.repeat(d_skip.astype(F32), SSD_HEAD_DIM).reshape(1, SSD_WIDTH)
    const = lambda shape: pl.BlockSpec(shape, lambda b, c: (0, 0))
    tok = lambda width: pl.BlockSpec((L, width), lambda b, c: (b * nc + c, 0))
    return pl.pallas_call(
        _ssd_kernel,
        grid=(BATCH, nc),
        in_specs=[tok(XBC_WIDTH), tok(SSD_WIDTH), tok(LANES),
                  pl.BlockSpec((SSD_HEADS, L), lambda b, c: (0, b * nc + c)),
                  const((CONV_K, XBC_WIDTH)), const((1, XBC_WIDTH)),
                  const((1, LANES)), const((SSD_HEADS, 1)), const((1, LANES)), const((SSD_HEADS, 1)),
                  const((1, SSD_WIDTH)), const((1, SSD_WIDTH)), const((L, L)), const((L, L))],
        out_specs=tok(SSD_WIDTH),
        out_shape=jax.ShapeDtypeStruct((TOKENS, SSD_WIDTH), BF16),
        scratch_shapes=[pltpu.VMEM((8 + L, XBC_WIDTH), F32),
                        pltpu.VMEM((SSD_STATE, SSD_WIDTH), F32)],
        compiler_params=_cparams(("arbitrary", "arbitrary")),
        name="ssd",
    )(xbc, z, dt, dt_t, conv_w.astype(F32), conv_b.astype(F32).reshape(1, XBC_WIDTH),
      pad_row(dt_bias), col8(dt_bias), pad_row(a_log), col8(a_log), dskip,
      ssd_norm_w.astype(F32).reshape(1, SSD_WIDTH), tril, triu)


OUT_TM = 256
ROUTE_W = 8
ROUTER_COLS = N_GROUPS + N_EXPERTS
RUN_ALIGN = 16
RUN_SHIFT = 4
LOCAL_ROWS = 1024
assert RUN_ALIGN == 1 << RUN_SHIFT and LOCAL_ROWS >= TOP_K * OUT_TM + N_EXPERTS * (RUN_ALIGN - 1)


def _lane_pick(values, lane, index):
    return jnp.sum(jnp.where(lane == index, values, 0.0), axis=-1, keepdims=True)


def _first_argmax(vals, lane):
    m = jnp.max(vals, axis=-1, keepdims=True)
    idx = jnp.min(jnp.where(vals == m, lane, float(LANES)), axis=-1, keepdims=True)
    return m, idx


def _out_router_kernel(att_ref, y_ref, x_ref, g1_ref, wo_ref, nw_ref, sc_ref, sh_ref, wr_ref, br_ref,
                       ltri_ref, sut_ref, x1_ref, h2_ref, route_ref, routet_ref, tcnt_ref, wr_split_ref):
    i = pl.program_id(0)

    @pl.when(i == 0)
    def _():
        hi, lo = _split_bf16(wr_ref[...])
        wr_split_ref[:, 0:LANES] = hi
        wr_split_ref[:, LANES:2 * LANES] = lo

    mixer = (jnp.dot(att_ref[...], wo_ref[0:ATT_WIDTH, :], preferred_element_type=F32)
             + jnp.dot(y_ref[...], wo_ref[ATT_WIDTH:ATT_WIDTH + SSD_WIDTH, :], preferred_element_type=F32))
    x1 = x_ref[...] + g1_ref[0] * mixer
    x1_ref[...] = x1
    yn = x1 * lax.rsqrt(jnp.mean(x1 * x1, axis=-1, keepdims=True) + EPS)
    h2 = (yn * nw_ref[...]) * (1.0 + sc_ref[0]) + sh_ref[0]
    h2_ref[...] = h2.astype(BF16)

    h_hi, h_lo = _split_bf16(h2)
    both = jnp.dot(h_hi, wr_split_ref[...], preferred_element_type=F32)
    logits = (both[:, 0:LANES] + both[:, LANES:2 * LANES]
              + jnp.dot(h_lo, wr_split_ref[:, 0:LANES], preferred_element_type=F32)) + br_ref[...]
    tm = logits.shape[0]
    lane = lax.broadcasted_iota(I32, (tm, LANES), 1).astype(F32)

    gl = jnp.where(lane < N_GROUPS, logits, NEG_BIG)
    gmax, gidx = _first_argmax(gl, lane)
    g_p = 1.0 / jnp.sum(jnp.exp(gl - gmax), axis=-1, keepdims=True)

    lo_lane = N_GROUPS + EXPERTS_PER_GROUP * gidx
    el = jnp.where((lane >= lo_lane) & (lane < lo_lane + EXPERTS_PER_GROUP), logits, NEG_BIG)
    m1, i1 = _first_argmax(el, lane)
    m2, i2 = _first_argmax(jnp.where(lane == i1, NEG_BIG, el), lane)
    r = jnp.exp(m2 - m1)
    p1 = 1.0 / (1.0 + r)
    p2 = r / (1.0 + r)
    e0 = i1 - N_GROUPS
    e1 = i2 - N_GROUPS

    onehot = ((lane == e0) | (lane == e1)).astype(F32)
    tile_cnt = jnp.sum(onehot, axis=0, keepdims=True)
    run_len = jnp.floor((tile_cnt + (RUN_ALIGN - 1)) * (1.0 / RUN_ALIGN)) * RUN_ALIGN
    run_start = jnp.dot(jnp.broadcast_to(run_len, (8, LANES)).astype(BF16), sut_ref[...],
                        preferred_element_type=F32)[0:1, :]
    before = jnp.dot(ltri_ref[...], onehot.astype(BF16), preferred_element_type=F32) + run_start
    slot0 = _lane_pick(before, lane, e0)
    slot1 = _lane_pick(before, lane, e1)
    tcnt_ref[0] = tile_cnt

    rec = jnp.zeros((tm, LANES), F32)
    for k, v in enumerate([slot0, slot1, g_p * p1, g_p * p2, e0, e1]):
        rec = jnp.where(lane == k, v, rec)
    route_ref[...] = rec[:, 0:ROUTE_W]
    routet_ref[...] = rec.T[0:ROUTE_W, :]


def _out_router(att, y, x2d, mod3, w_out_b, norm_w, w_router, b_router):
    tm = OUT_TM
    n_steps = TOKENS // tm
    steps_per_batch = SEQ // tm
    idx = jnp.arange(tm)
    ltri = (idx[None, :] < idx[:, None]).astype(BF16)
    lidx = jnp.arange(LANES)
    sut = (lidx[:, None] < lidx[None, :]).astype(BF16)
    const = lambda shape: pl.BlockSpec(shape, lambda i: (0, 0))
    tok = lambda width: pl.BlockSpec((tm, width), lambda i: (i, 0))
    modspec = lambda k: pl.BlockSpec((1, 1, D_MODEL), lambda i: ((i // steps_per_batch) * 6 + k, 0, 0))
    return pl.pallas_call(
        _out_router_kernel,
        grid=(n_steps,),
        in_specs=[tok(ATT_WIDTH), tok(SSD_WIDTH), tok(D_MODEL), modspec(2),
                  const((D_MODEL, D_MODEL)), const((1, D_MODEL)), modspec(4), modspec(3),
                  const((D_MODEL, LANES)), const((1, LANES)), const((tm, tm)), const((LANES, LANES))],
        out_specs=[tok(D_MODEL), tok(D_MODEL), tok(ROUTE_W),
                   pl.BlockSpec((ROUTE_W, tm), lambda i: (i, 0)),
                   pl.BlockSpec((1, 1, LANES), lambda i: (i, 0, 0))],
        out_shape=[jax.ShapeDtypeStruct((TOKENS, D_MODEL), F32),
                   jax.ShapeDtypeStruct((TOKENS, D_MODEL), BF16),
                   jax.ShapeDtypeStruct((TOKENS, ROUTE_W), F32),
                   jax.ShapeDtypeStruct((n_steps * ROUTE_W, tm), F32),
                   jax.ShapeDtypeStruct((n_steps, 1, LANES), F32)],
        scratch_shapes=[pltpu.VMEM((D_MODEL, 2 * LANES), BF16)],
        compiler_params=_cparams(("arbitrary",)),
        name="out_router",
    )(att, y, x2d, mod3, w_out_b, norm_w.reshape(1, D_MODEL), mod3, mod3, w_router, b_router, ltri, sut)


MOE_TM = 512
ZERO_ROWS = 256
N_TOKEN_TILES = TOKENS // OUT_TM
MAX_PIECES = LOCAL_ROWS // RUN_ALIGN
MAX_SORTED_ROWS = TOKENS * TOP_K + N_TOKEN_TILES * N_EXPERTS * (RUN_ALIGN - 1)
N_TILES = MAX_SORTED_ROWS // MOE_TM + N_EXPERTS
N_ROWS = N_TILES * MOE_TM
assert MOE_TM % ZERO_ROWS == 0


def _for_pieces(count, action):
    def body(j, carry):
        action(j)
        return carry

    lax.fori_loop(0, count, body, 0)


def _dispatch_kernel(seg_end_ref, cnt_ref, np_ref, dst_ref, routet_ref, h2_ref, xs_ref,
                     sbuf_ref, zero_ref, sems, zsem):
    i = pl.program_id(0)
    last = pl.num_programs(0) - 1
    buf = lax.rem(i, 2)

    @pl.when(i == 0)
    def _():
        zero_ref[...] = jnp.zeros_like(zero_ref)
        blocks_per_tile = MOE_TM // ZERO_ROWS

        def zcopy(block):
            start = pl.multiple_of(block * ZERO_ROWS, ZERO_ROWS)
            return pltpu.make_async_copy(zero_ref, xs_ref.at[pl.ds(start, ZERO_ROWS)], zsem)

        def tails(action):
            def body(e, carry):
                @pl.when(cnt_ref[e] > 0)
                def _():
                    for b in range(blocks_per_tile):
                        action(zcopy(seg_end_ref[e] // ZERO_ROWS - blocks_per_tile + b))
                return carry

            lax.fori_loop(0, N_EXPERTS, body, 0)

        def unused(action):
            def body(block, carry):
                action(zcopy(block))
                return carry

            lax.fori_loop(seg_end_ref[N_EXPERTS - 1] // ZERO_ROWS, N_ROWS // ZERO_ROWS, body, 0)

        tails(lambda cp: cp.start())
        unused(lambda cp: cp.start())
        tails(lambda cp: cp.wait())
        unused(lambda cp: cp.wait())

    slot = lax.broadcasted_iota(I32, (LOCAL_ROWS, OUT_TM), 0).astype(F32)
    perm = jnp.where((slot == routet_ref[0:1, :]) | (slot == routet_ref[1:2, :]), 1.0, 0.0).astype(BF16)
    sbuf_ref[buf] = jnp.dot(perm, h2_ref[...], preferred_element_type=F32).astype(BF16)

    def piece(b, local_piece, sorted_row):
        local = pl.multiple_of(local_piece * RUN_ALIGN, RUN_ALIGN)
        return pltpu.make_async_copy(sbuf_ref.at[b, pl.ds(local, RUN_ALIGN)],
                                     xs_ref.at[pl.ds(pl.multiple_of(sorted_row, RUN_ALIGN), RUN_ALIGN)],
                                     sems.at[b])

    _for_pieces(np_ref[i], lambda j: piece(buf, j, dst_ref[0, 0, j]).start())

    @pl.when(i > 0)
    def _():
        _for_pieces(np_ref[jnp.maximum(i - 1, 0)], lambda j: piece(1 - buf, j, 0).wait())

    @pl.when(i == last)
    def _():
        _for_pieces(np_ref[i], lambda j: piece(buf, j, 0).wait())


def _piece_spec(index_map):
    return pl.BlockSpec((1, 1, MAX_PIECES), index_map, memory_space=pltpu.SMEM)


def _dispatch(seg_end, counts, n_pieces, piece_dst, route_t, h2):
    grid_spec = pltpu.PrefetchScalarGridSpec(
        num_scalar_prefetch=3,
        grid=(N_TOKEN_TILES,),
        in_specs=[_piece_spec(lambda i, se, cn, npc: (i, 0, 0)),
                  pl.BlockSpec((ROUTE_W, OUT_TM), lambda i, se, cn, npc: (i, 0)),
                  pl.BlockSpec((OUT_TM, D_MODEL), lambda i, se, cn, npc: (i, 0))],
        out_specs=pl.BlockSpec(memory_space=pl.ANY),
        scratch_shapes=[pltpu.VMEM((2, LOCAL_ROWS, D_MODEL), BF16),
                        pltpu.VMEM((ZERO_ROWS, D_MODEL), BF16),
                        pltpu.SemaphoreType.DMA((2,)), pltpu.SemaphoreType.DMA],
    )
    return pl.pallas_call(
        _dispatch_kernel,
        grid_spec=grid_spec,
        out_shape=jax.ShapeDtypeStruct((N_ROWS, D_MODEL), BF16),
        compiler_params=_cparams(("arbitrary",)),
        name="dispatch",
    )(seg_end, counts, n_pieces, piece_dst, route_t, h2)


def _experts_kernel(te_ref, nu_ref, xs_ref, wg_ref, wu_ref, wd_ref, ys_ref, wgu_b_ref, wd_b_ref):
    i = pl.program_id(0)
    used = i < nu_ref[0]

    @pl.when(used & ((i == 0) | (te_ref[i] != te_ref[jnp.maximum(i - 1, 0)])))
    def _():
        wgu_b_ref[:, 0:D_EXPERT] = wg_ref[0].astype(BF16)
        wgu_b_ref[:, D_EXPERT:2 * D_EXPERT] = wu_ref[0].astype(BF16)
        wd_b_ref[...] = wd_ref[0].astype(BF16)

    @pl.when(used)
    def _():
        h = jnp.dot(xs_ref[...], wgu_b_ref[...], preferred_element_type=F32)
        act = (_silu(h[:, 0:D_EXPERT]) * h[:, D_EXPERT:2 * D_EXPERT]).astype(BF16)
        ys_ref[...] = jnp.dot(act, wd_b_ref[...], preferred_element_type=F32).astype(BF16)

    @pl.when(jnp.logical_not(used))
    def _():
        ys_ref[...] = jnp.zeros_like(ys_ref)


def _experts(tile_expert, n_used, xs, w_gate, w_up, w_down):
    row_tile = lambda i, te, nu: (jnp.minimum(i, nu[0] - 1), 0)
    grid_spec = pltpu.PrefetchScalarGridSpec(
        num_scalar_prefetch=2,
        grid=(N_TILES,),
        in_specs=[pl.BlockSpec((MOE_TM, D_MODEL), row_tile),
                  pl.BlockSpec((1, D_MODEL, D_EXPERT), lambda i, te, nu: (te[i], 0, 0)),
                  pl.BlockSpec((1, D_MODEL, D_EXPERT), lambda i, te, nu: (te[i], 0, 0)),
                  pl.BlockSpec((1, D_EXPERT, D_MODEL), lambda i, te, nu: (te[i], 0, 0))],
        out_specs=pl.BlockSpec((MOE_TM, D_MODEL), lambda i, te, nu: (i, 0)),
        scratch_shapes=[pltpu.VMEM((D_MODEL, 2 * D_EXPERT), BF16), pltpu.VMEM((D_EXPERT, D_MODEL), BF16)],
    )
    return pl.pallas_call(
        _experts_kernel,
        grid_spec=grid_spec,
        out_shape=jax.ShapeDtypeStruct((N_ROWS, D_MODEL), BF16),
        compiler_params=_cparams(("arbitrary",)),
        name="experts",
    )(tile_expert, n_used, xs, w_gate, w_up, w_down)


def _combine_kernel(np_ref, dst_ref, dst_next_ref, route_ref, x1_ref, g2_ref, ys_ref, o_ref, gbuf_ref, sems):
    i = pl.program_id(0)
    last = pl.num_programs(0) - 1
    buf = lax.rem(i, 2)

    def piece(b, local_piece, sorted_row):
        local = pl.multiple_of(local_piece * RUN_ALIGN, RUN_ALIGN)
        return pltpu.make_async_copy(ys_ref.at[pl.ds(pl.multiple_of(sorted_row, RUN_ALIGN), RUN_ALIGN)],
                                     gbuf_ref.at[b, pl.ds(local, RUN_ALIGN)], sems.at[b])

    @pl.when(i == 0)
    def _():
        gbuf_ref[...] = jnp.zeros_like(gbuf_ref)
        _for_pieces(np_ref[0], lambda j: piece(0, j, dst_ref[0, 0, j]).start())

    @pl.when(i < last)
    def _():
        _for_pieces(np_ref[jnp.minimum(i + 1, last)], lambda j: piece(1 - buf, j, dst_next_ref[0, 0, j]).start())

    rec = route_ref[...]
    slot = lax.broadcasted_iota(I32, (OUT_TM, LOCAL_ROWS), 1).astype(F32)
    weights = (jnp.where(slot == rec[:, 0:1], rec[:, 2:3], 0.0)
               + jnp.where(slot == rec[:, 1:2], rec[:, 3:4], 0.0)).astype(BF16)
    _for_pieces(np_ref[i], lambda j: piece(buf, j, 0).wait())
    moe = jnp.dot(weights, gbuf_ref[buf], preferred_element_type=F32)
    o_ref[...] = x1_ref[...] + g2_ref[0] * moe


def _combine(n_pieces, piece_dst, route, x1, mod3, ys):
    tm = OUT_TM
    steps_per_batch = SEQ // tm
    grid_spec = pltpu.PrefetchScalarGridSpec(
        num_scalar_prefetch=1,
        grid=(N_TOKEN_TILES,),
        in_specs=[_piece_spec(lambda i, npc: (i, 0, 0)),
                  _piece_spec(lambda i, npc: (jnp.minimum(i + 1, N_TOKEN_TILES - 1), 0, 0)),
                  pl.BlockSpec((tm, ROUTE_W), lambda i, npc: (i, 0)),
                  pl.BlockSpec((tm, D_MODEL), lambda i, npc: (i, 0)),
                  pl.BlockSpec((1, 1, D_MODEL), lambda i, npc: ((i // steps_per_batch) * 6 + 5, 0, 0)),
                  pl.BlockSpec(memory_space=pl.ANY)],
        out_specs=pl.BlockSpec((tm, D_MODEL), lambda i, npc: (i, 0)),
        scratch_shapes=[pltpu.VMEM((2, LOCAL_ROWS, D_MODEL), BF16), pltpu.SemaphoreType.DMA((2,))],
    )
    return pl.pallas_call(
        _combine_kernel,
        grid_spec=grid_spec,
        out_shape=jax.ShapeDtypeStruct((TOKENS, D_MODEL), F32),
        compiler_params=_cparams(("arbitrary",)),
        name="combine",
    )(n_pieces, piece_dst, piece_dst, route, x1, mod3, ys)


def kernel(x, c, positions, norm1_w, norm2_w, w_ada, b_ada, w_in, conv_w, conv_b, dt_bias, a_log,
           d_skip, ssd_norm_w, q_norm_w, k_norm_w, sinks, w_out, w_group, b_group, w_expert, b_expert,
           w_gate, w_up, w_down):
    assert x.shape == (BATCH, SEQ, D_MODEL) and w_in.shape == (D_MODEL, IN_WIDTH)
    x2d = x.reshape(TOKENS, D_MODEL)
    mod = _ada_mod(c, w_ada, b_ada)
    mod3 = mod.reshape(BATCH * 6, 1, D_MODEL)

    w_in_pad = jnp.pad(w_in, ((0, 0), (0, IN_PAD - IN_WIDTH))).astype(BF16)
    qkv, z, xbc, dt = _in_proj(x2d, norm1_w, mod3, w_in_pad)
    att = _attention(qkv, positions, q_norm_w, k_norm_w, sinks)
    y = _ssd(xbc, z, dt, conv_w, conv_b, dt_bias, a_log, d_skip, ssd_norm_w)

    w_router = jnp.pad(jnp.concatenate([w_group, w_expert], axis=1).astype(F32),
                       ((0, 0), (0, LANES - ROUTER_COLS)))
    b_router = jnp.pad(jnp.concatenate([b_group, b_expert]).astype(F32),
                       (0, LANES - ROUTER_COLS)).reshape(1, LANES)
    x1, h2, route, route_t, tcnt = _out_router(att, y, x2d, mod3, w_out.astype(BF16), norm2_w,
                                                w_router, b_router)

    tc = tcnt[:, 0, 0:N_EXPERTS].astype(I32)
    run_rows = ((tc + RUN_ALIGN - 1) // RUN_ALIGN) * RUN_ALIGN
    counts = jnp.sum(run_rows, axis=0)
    padded = ((counts + MOE_TM - 1) // MOE_TM) * MOE_TM
    seg_end = jnp.cumsum(padded)
    seg_start = seg_end - padded
    run_dst = seg_start[None, :] + jnp.cumsum(run_rows, axis=0) - run_rows
    n_used = (seg_end[-1] // MOE_TM).reshape(1)
    last_row = jnp.minimum(jnp.arange(N_TILES, dtype=I32) * MOE_TM, seg_end[-1] - 1)
    tile_expert = jnp.sum((seg_end[None, :] <= last_row[:, None]).astype(I32), axis=1)

    run_pieces = run_rows // RUN_ALIGN
    piece_end = jnp.cumsum(run_pieces, axis=1)
    n_pieces = piece_end[:, -1]
    j = jnp.arange(MAX_PIECES, dtype=I32)
    piece_expert = jnp.sum((piece_end[:, None, :] <= j[None, :, None]).astype(I32), axis=2)
    in_run = (piece_expert[:, :, None] == jnp.arange(N_EXPERTS, dtype=I32)[None, None, :]).astype(I32)
    run_base = run_dst - RUN_ALIGN * (piece_end - run_pieces)
    piece_dst = jnp.sum(in_run * run_base[:, None, :], axis=2) + RUN_ALIGN * j[None, :]
    piece_dst = piece_dst.astype(I32).reshape(N_TOKEN_TILES, 1, MAX_PIECES)

    xs = _dispatch(seg_end.astype(I32), counts, n_pieces.astype(I32), piece_dst, route_t, h2)
    ys = _experts(tile_expert, n_used.astype(I32), xs, w_gate, w_up, w_down)
    out = _combine(n_pieces.astype(I32), piece_dst, route, x1, mod3, ys)
    return out.reshape(BATCH, SEQ, D_MODEL)
```

```python
import jax
import jax.numpy as jnp
from jax import lax
from jax.experimental import pallas as pl
from jax.experimental.pallas import tpu as pltpu

F32 = jnp.float32
BF16 = jnp.bfloat16
I32 = jnp.int32

D_MODEL = 1024
BATCH = 2
SEQ = 8192
TOKENS = BATCH * SEQ
ATT_HEADS = 8
ATT_KV_HEADS = 2
HEAD_DIM = 64
ATT_WIDTH = ATT_HEADS * HEAD_DIM
KV_WIDTH = ATT_KV_HEADS * HEAD_DIM
ATT_BLOCK = 128
ROPE_DIM = HEAD_DIM // 4
ROPE_THETA = 500000.0
SSD_HEADS = 8
SSD_HEAD_DIM = 64
SSD_WIDTH = SSD_HEADS * SSD_HEAD_DIM
SSD_GROUPS = 2
SSD_STATE = 128
CONV_K = 4
CHUNK = 128
XBC_WIDTH = SSD_WIDTH + 2 * SSD_GROUPS * SSD_STATE
IN_WIDTH = ATT_WIDTH + 2 * KV_WIDTH + SSD_WIDTH + XBC_WIDTH + SSD_HEADS
N_GROUPS = 4
EXPERTS_PER_GROUP = 8
N_EXPERTS = N_GROUPS * EXPERTS_PER_GROUP
TOP_K = 2
D_EXPERT = 256
EPS = 1e-6

LANES = 128
QKV_WIDTH = ATT_WIDTH + 2 * KV_WIDTH
IN_PAD = QKV_WIDTH + SSD_WIDTH + XBC_WIDTH + LANES
NEG_BIG = -1e30

VMEM_LIMIT = 48 * 1024 * 1024


def _cparams(sem):
    return pltpu.CompilerParams(dimension_semantics=sem, vmem_limit_bytes=VMEM_LIMIT)


def _split_bf16(x):
    hi = x.astype(BF16)
    lo = (x - hi.astype(F32)).astype(BF16)
    return hi, lo


ADA_TN = 768


def _ada_kernel(ct_ref, w_ref, b_ref, o_ref):
    ct = ct_ref[...]
    s = ct * jax.nn.sigmoid(ct)
    w = w_ref[...]
    rows = [jnp.sum(s[:, b:b + 1] * w, axis=0, keepdims=True) for b in range(BATCH)]
    o_ref[...] = jnp.concatenate(rows, axis=0) + b_ref[...]


def _ada_mod(c, w_ada, b_ada):
    n = w_ada.shape[1]
    return pl.pallas_call(
        _ada_kernel,
        grid=(n // ADA_TN,),
        in_specs=[pl.BlockSpec((D_MODEL, BATCH), lambda j: (0, 0)),
                  pl.BlockSpec((D_MODEL, ADA_TN), lambda j: (0, j)),
                  pl.BlockSpec((1, ADA_TN), lambda j: (0, j))],
        out_specs=pl.BlockSpec((BATCH, ADA_TN), lambda j: (0, j)),
        out_shape=jax.ShapeDtypeStruct((BATCH, n), F32),
        compiler_params=_cparams(("arbitrary",)),
        name="ada_mod",
    )(c.T, w_ada, b_ada.reshape(1, n))


INPROJ_TM = 512
_INPROJ_CHUNK = 256


def _inproj_kernel(x_ref, nw_ref, sc_ref, sh_ref, w_ref, qkv_ref, z_ref, xbc_ref, dt_ref):
    x = x_ref[...]
    y = x * lax.rsqrt(jnp.mean(x * x, axis=-1, keepdims=True) + EPS)
    h = (y * nw_ref[...]) * (1.0 + sc_ref[0]) + sh_ref[0]
    hb = h.astype(BF16)

    def proj(c0, c1):
        return jnp.dot(hb, w_ref[:, c0:c1], preferred_element_type=F32)

    for c0 in range(0, QKV_WIDTH, _INPROJ_CHUNK):
        qkv_ref[:, c0:c0 + _INPROJ_CHUNK] = proj(c0, c0 + _INPROJ_CHUNK).astype(BF16)
    base = QKV_WIDTH
    for c0 in range(0, SSD_WIDTH, _INPROJ_CHUNK):
        z_ref[:, c0:c0 + _INPROJ_CHUNK] = proj(base + c0, base + c0 + _INPROJ_CHUNK).astype(BF16)
    base += SSD_WIDTH
    for c0 in range(0, XBC_WIDTH, _INPROJ_CHUNK):
        xbc_ref[:, c0:c0 + _INPROJ_CHUNK] = proj(base + c0, base + c0 + _INPROJ_CHUNK).astype(BF16)
    base += XBC_WIDTH
    dt_ref[...] = proj(base, base + LANES)


def _in_proj(x2d, norm_w, mod3, w_in_pad):
    tm = INPROJ_TM
    steps_per_batch = SEQ // tm
    return pl.pallas_call(
        _inproj_kernel,
        grid=(TOKENS // tm,),
        in_specs=[pl.BlockSpec((tm, D_MODEL), lambda i: (i, 0)),
                  pl.BlockSpec((1, D_MODEL), lambda i: (0, 0)),
                  pl.BlockSpec((1, 1, D_MODEL), lambda i: ((i // steps_per_batch) * 6 + 1, 0, 0)),
                  pl.BlockSpec((1, 1, D_MODEL), lambda i: ((i // steps_per_batch) * 6 + 0, 0, 0)),
                  pl.BlockSpec((D_MODEL, IN_PAD), lambda i: (0, 0))],
        out_specs=[pl.BlockSpec((tm, QKV_WIDTH), lambda i: (i, 0)),
                   pl.BlockSpec((tm, SSD_WIDTH), lambda i: (i, 0)),
                   pl.BlockSpec((tm, XBC_WIDTH), lambda i: (i, 0)),
                   pl.BlockSpec((tm, LANES), lambda i: (i, 0))],
        out_shape=[jax.ShapeDtypeStruct((TOKENS, QKV_WIDTH), BF16),
                   jax.ShapeDtypeStruct((TOKENS, SSD_WIDTH), BF16),
                   jax.ShapeDtypeStruct((TOKENS, XBC_WIDTH), BF16),
                   jax.ShapeDtypeStruct((TOKENS, LANES), F32)],
        compiler_params=_cparams(("arbitrary",)),
        name="in_proj",
    )(x2d, norm_w.reshape(1, D_MODEL), mod3, mod3, w_in_pad)


ATT_SUB = 2


ROPE_TM = 2048
_ROPE_HALF = ROPE_DIM // 2
_TOK_PER_ROW = LANES // _ROPE_HALF


def _exact_dot(x, onehot_b):
    hi, lo = _split_bf16(x)
    return (jnp.dot(hi, onehot_b, preferred_element_type=F32)
            + jnp.dot(lo, onehot_b, preferred_element_type=F32))


def _rope_kernel(pos_ref, freq_ref, sel_ref, own_ref, gcos_ref, gs1_ref, gs2_ref, ident_ref,
                 cos_ref, s1_ref, s2_ref):
    ang = pos_ref[...].astype(F32) * freq_ref[...]
    cos_p, sin_p = jnp.cos(ang), jnp.sin(ang)
    hi_c, lo_c = _split_bf16(cos_p)
    hi_s, lo_s = _split_bf16(sin_p)
    sel = sel_ref[...]
    rows_c = jnp.dot(sel, hi_c, preferred_element_type=F32) + jnp.dot(sel, lo_c, preferred_element_type=F32)
    rows_s = jnp.dot(sel, hi_s, preferred_element_type=F32) + jnp.dot(sel, lo_s, preferred_element_type=F32)
    own = own_ref[...]
    cos_ref[...] = _exact_dot(rows_c * own, gcos_ref[...]) + ident_ref[...]
    s1_ref[...] = _exact_dot(rows_s * own, gs1_ref[...])
    s2_ref[...] = _exact_dot(rows_s * own, gs2_ref[...])


def _rope_tables(positions):
    half, per_row = _ROPE_HALF, _TOK_PER_ROW
    rows = ROPE_TM // per_row
    pos_rep = jnp.repeat(positions.reshape(TOKENS).astype(I32), half).reshape(TOKENS // per_row, LANES)
    inv_freq = jnp.power(ROPE_THETA, -jnp.arange(half, dtype=F32) * 2.0 / ROPE_DIM)
    freq = jnp.tile(inv_freq, per_row).reshape(1, LANES)
    tok = jnp.arange(ROPE_TM)
    lane = jnp.arange(LANES)
    sel = (tok[:, None] // per_row == jnp.arange(rows)[None, :]).astype(BF16)
    own = (lane[None, :] // half == tok[:, None] % per_row).astype(F32)
    d = lane % HEAD_DIM
    src_f = lane % half
    hits = lambda lo, hi: ((src_f[:, None] == d[None, :] % half) & (d[None, :] >= lo) & (d[None, :] < hi))
    gcos = hits(0, ROPE_DIM).astype(BF16)
    gs1 = -hits(0, half).astype(BF16)
    gs2 = hits(half, ROPE_DIM).astype(BF16)
    ident = (d >= ROPE_DIM).astype(F32).reshape(1, LANES)
    const = lambda shape: pl.BlockSpec(shape, lambda i: (0, 0))
    out_spec = pl.BlockSpec((ROPE_TM, LANES), lambda i: (i, 0))
    out = jax.ShapeDtypeStruct((TOKENS, LANES), F32)
    return pl.pallas_call(
        _rope_kernel,
        grid=(TOKENS // ROPE_TM,),
        in_specs=[pl.BlockSpec((rows, LANES), lambda i: (i, 0)), const((1, LANES)),
                  const((ROPE_TM, rows)), const((ROPE_TM, LANES)),
                  const((LANES, LANES)), const((LANES, LANES)), const((LANES, LANES)), const((1, LANES))],
        out_specs=[out_spec, out_spec, out_spec],
        out_shape=[out, out, out],
        compiler_params=_cparams(("arbitrary",)),
        name="rope_tables",
    )(pos_rep, freq, sel, own, gcos, gs1, gs2, ident)


def _seg_meansq(xf, ones128):
    rows, width = xf.shape
    nt = width // LANES
    parts = _split_bf16(xf * xf)
    stacked = jnp.concatenate([p[:, t * LANES:(t + 1) * LANES] for p in parts for t in range(nt)], axis=0)
    tot = jnp.dot(stacked, ones128, preferred_element_type=F32)
    tiles = [tot[t * rows:(t + 1) * rows] + tot[(nt + t) * rows:(nt + t + 1) * rows] for t in range(nt)]
    return jnp.concatenate(tiles, axis=1) * (1.0 / HEAD_DIM)


def _norm_rope(x_bf, w_row, ones_bd, cosf, s1, s2):
    xf = x_bf.astype(F32)
    width = xf.shape[1]
    xn = xf * lax.rsqrt(_seg_meansq(xf, ones_bd) + EPS) * w_row
    half = ROPE_DIM // 2
    up = pltpu.roll(xn, width - half, axis=1)
    down = pltpu.roll(xn, half, axis=1)
    return xn * cosf + up * s1 + down * s2


def _attn_kernel(sink_ref, q_ref, kv_ref, cos_ref, s1_ref, s2_ref, qw_ref, kw_ref,
                 ones_ref, o_ref, kprev_ref, vprev_ref):
    j = pl.program_id(1)
    blk = ATT_BLOCK

    @pl.when(j == 0)
    def _():
        kprev_ref[...] = jnp.zeros_like(kprev_ref)
        vprev_ref[...] = jnp.zeros_like(vprev_ref)

    cos1 = cos_ref[...]
    s1_1 = s1_ref[...]
    s2_1 = s2_ref[...]
    reps = ATT_WIDTH // LANES
    cosq = jnp.concatenate([cos1] * reps, axis=1)
    s1q = jnp.concatenate([s1_1] * reps, axis=1)
    s2q = jnp.concatenate([s2_1] * reps, axis=1)

    q = _norm_rope(q_ref[...], qw_ref[...], ones_ref[...], cosq, s1q, s2q)
    qf = q * (HEAD_DIM ** -0.5)
    kv = kv_ref[...]
    kn = _norm_rope(kv[:, 0:KV_WIDTH], kw_ref[...], ones_ref[...], cos1, s1_1, s2_1)
    vn = kv[:, KV_WIDTH:2 * KV_WIDTH].astype(F32)

    kall = jnp.concatenate([kprev_ref[...], kn], axis=0)
    vall = jnp.concatenate([vprev_ref[...], vn], axis=0)
    kprev_ref[...] = kn[(ATT_SUB - 1) * blk:ATT_SUB * blk]
    vprev_ref[...] = vn[(ATT_SUB - 1) * blk:ATT_SUB * blk]

    lo_all = lax.broadcasted_iota(I32, kall.shape, 1) < HEAD_DIM
    ones_all = jnp.ones(kall.shape, BF16)

    row = lax.broadcasted_iota(I32, (2 * blk, blk), 0)
    col = lax.broadcasted_iota(I32, (2 * blk, blk), 1)
    from_prev = col > (row & (blk - 1))
    second_tile = lax.broadcasted_iota(I32, (2 * blk, 1), 0) >= blk
    zero_p = jnp.zeros((2 * blk, blk), F32)

    k_par, v_par = [], []
    for g in range(ATT_KV_HEADS):
        keep = lo_all if g == 0 else ~lo_all
        k_own = jnp.where(keep, kall, 0.0)
        v_own = jnp.where(keep, vall, 0.0)
        k_oth = pltpu.roll(k_own, HEAD_DIM, axis=1)
        v_oth = pltpu.roll(v_own, HEAD_DIM, axis=1)
        k_lo, k_hi = (k_own, k_oth) if g == 0 else (k_oth, k_own)
        v_lo, v_hi = (v_own, v_oth) if g == 0 else (v_oth, v_own)
        k_par.append((k_lo.astype(BF16), k_hi.astype(BF16)))
        v_par.append((jnp.concatenate([v_lo.astype(BF16), ones_all], axis=1),
                      jnp.concatenate([v_hi.astype(BF16), ones_all], axis=1)))

    problems = [(g, sub) for g in range(ATT_KV_HEADS) for sub in range(ATT_SUB)]
    scores = []
    for g, sub in problems:
        r0, c0 = sub * blk, g * 2 * LANES
        qcat = jnp.concatenate([qf[r0:r0 + blk, c0:c0 + LANES],
                                qf[r0:r0 + blk, c0 + LANES:c0 + 2 * LANES]], axis=0).astype(BF16)
        kw = jnp.concatenate([k_par[g][0][r0:r0 + 2 * blk], k_par[g][1][r0:r0 + 2 * blk]], axis=0)
        scores.append(lax.dot_general(qcat, kw, (((1,), (1,)), ((), ())),
                                      preferred_element_type=F32))

    weights, rescale = [], []
    for (g, sub), s_all in zip(problems, scores):
        for par in range(2):
            s = s_all[:, par * 2 * blk:(par + 1) * 2 * blk]
            s_prev = s[:, 0:blk]
            if sub == 0:
                s_prev = s_prev + jnp.where(j > 0, 0.0, NEG_BIG)
            s = jnp.where(from_prev, s_prev, s[:, blk:2 * blk])
            h_first = ATT_HEADS // ATT_KV_HEADS * g + par
            sink = jnp.where(second_tile, sink_ref[h_first + 2], sink_ref[h_first])
            m = jnp.maximum(jnp.max(s, axis=-1, keepdims=True), sink)
            p = jnp.exp(s - m)
            weights.append(jnp.concatenate([jnp.where(from_prev, p, zero_p), jnp.where(from_prev, zero_p, p)],
                                           axis=1).astype(BF16))
            rescale.append(jnp.exp(sink - m))

    outs = []
    for idx, (g, sub) in enumerate(problems):
        for par in range(2):
            outs.append(jnp.dot(weights[2 * idx + par], v_par[g][par][sub * blk:(sub + 2) * blk],
                                preferred_element_type=F32))

    for idx, (g, sub) in enumerate(problems):
        r0, c0 = sub * blk, g * 2 * LANES
        pair = None
        for par in range(2):
            o = outs[2 * idx + par]
            part = o[:, 0:LANES] * (1.0 / (o[:, LANES:2 * LANES] + rescale[2 * idx + par]))
            pair = part if pair is None else pair + part
        o_ref[r0:r0 + blk, c0:c0 + LANES] = pair[0:blk].astype(BF16)
        o_ref[r0:r0 + blk, c0 + LANES:c0 + 2 * LANES] = pair[blk:2 * blk].astype(BF16)


def _attention(qkv, positions, q_norm_w, k_norm_w, sinks):
    cosf, s1, s2 = _rope_tables(positions)
    qw = jnp.tile(q_norm_w.astype(F32), ATT_HEADS).reshape(1, ATT_WIDTH)
    kw = jnp.tile(k_norm_w.astype(F32), ATT_KV_HEADS).reshape(1, KV_WIDTH)
    seg = jnp.arange(LANES) // HEAD_DIM
    ones128 = (seg[:, None] == seg[None, :]).astype(BF16)
    const = lambda shape: pl.BlockSpec(shape, lambda b, j, s: (0, 0))
    rows = ATT_SUB * ATT_BLOCK
    nb = SEQ // rows
    tok = lambda width, cb: pl.BlockSpec((rows, width), lambda b, j, s: (b * nb + j, cb))
    grid_spec = pltpu.PrefetchScalarGridSpec(
        num_scalar_prefetch=1,
        grid=(BATCH, nb),
        in_specs=[tok(ATT_WIDTH, 0), tok(2 * KV_WIDTH, 2), tok(LANES, 0), tok(LANES, 0), tok(LANES, 0),
                  const((1, ATT_WIDTH)), const((1, KV_WIDTH)), const((LANES, LANES))],
        out_specs=tok(ATT_WIDTH, 0),
        scratch_shapes=[pltpu.VMEM((ATT_BLOCK, KV_WIDTH), F32),
                        pltpu.VMEM((ATT_BLOCK, KV_WIDTH), F32)],
    )
    return pl.pallas_call(
        _attn_kernel,
        grid_spec=grid_spec,
        out_shape=jax.ShapeDtypeStruct((TOKENS, ATT_WIDTH), BF16),
        compiler_params=_cparams(("arbitrary", "arbitrary")),
        name="attention",
    )(sinks.astype(F32), qkv, qkv, cosf, s1, s2, qw, kw, ones128)


def _softplus(x):
    return jnp.maximum(x, 0.0) + jnp.log1p(jnp.exp(-jnp.abs(x)))


def _silu(x):
    return x * jax.nn.sigmoid(x)


def _ssd_kernel(xbc_ref, z_ref, dt_ref, dtt_ref, cw_ref, cb_ref, dtb_row_ref, dtb_col_ref,
                alog_row_ref, alog_col_ref, dskip_ref, nw_ref, tril_ref, triu_ref,
                o_ref, conv_ref, state_ref):
    c = pl.program_id(1)
    L = CHUNK
    tail = 8

    @pl.when(c == 0)
    def _():
        conv_ref[0:tail, :] = jnp.zeros((tail, XBC_WIDTH), F32)
        state_ref[...] = jnp.zeros_like(state_ref)

    xb = xbc_ref[...].astype(F32)
    conv_ref[tail:tail + L, :] = xb
    acc = cb_ref[...] + cw_ref[CONV_K - 1:CONV_K, :] * xb
    for k in range(CONV_K - 1):
        off = tail - (CONV_K - 1) + k
        acc = acc + cw_ref[k:k + 1, :] * conv_ref[off:off + L, :]
    conv_ref[0:tail, :] = xb[L - tail:L, :]
    u = _silu(acc)
    xs = u[:, 0:SSD_WIDTH]
    bmat = u[:, SSD_WIDTH:SSD_WIDTH + SSD_GROUPS * SSD_STATE]
    cmat = u[:, SSD_WIDTH + SSD_GROUPS * SSD_STATE:XBC_WIDTH]

    dt = _softplus(dt_ref[...] + dtb_row_ref[...])
    a = dt * (-jnp.exp(alog_row_ref[...]))
    a_hi, a_lo = _split_bf16(a)
    a_cum = (jnp.dot(tril_ref[...], a_hi, preferred_element_type=F32)
             + jnp.dot(tril_ref[...], a_lo, preferred_element_type=F32))
    dt_t = _softplus(dtt_ref[...] + dtb_col_ref[...])
    a_t = dt_t * (-jnp.exp(alog_col_ref[...]))
    at_hi, at_lo = _split_bf16(a_t)
    a_cum_t = (jnp.dot(at_hi, triu_ref[...], preferred_element_type=F32)
               + jnp.dot(at_lo, triu_ref[...], preferred_element_type=F32))
    a_end_t = a_cum_t[:, L - 1:L]
    wst_t = jnp.exp(a_end_t - a_cum_t) * dt_t
    cdec_t = jnp.exp(a_end_t)

    row = lax.broadcasted_iota(I32, (L, L), 0)
    col = lax.broadcasted_iota(I32, (L, L), 1)
    causal = col <= row
    lane = lax.broadcasted_iota(I32, (L, LANES), 1)
    lo_half = lane < SSD_HEAD_DIM

    xs_b = xs.astype(BF16)
    heads_per_group = SSD_HEADS // SSD_GROUPS
    gated = []
    for g in range(SSD_GROUPS):
        b_g = bmat[:, g * SSD_STATE:(g + 1) * SSD_STATE]
        c_g = cmat[:, g * SSD_STATE:(g + 1) * SSD_STATE]
        cb = lax.dot_general(c_g.astype(BF16), b_g.astype(BF16), (((1,), (1,)), ((), ())),
                             preferred_element_type=F32)
        b_gt = b_g.T
        for t in range(heads_per_group // 2):
            tile = g * (heads_per_group // 2) + t
            c0 = tile * LANES
            xs_tile = xs_b[:, c0:c0 + LANES]
            st_tile = state_ref[:, c0:c0 + LANES]
            st_b = st_tile.astype(BF16)
            y_tile = jnp.zeros((L, LANES), F32)
            new_tile = jnp.zeros((SSD_STATE, LANES), F32)
            for e in range(2):
                h = 2 * tile + e
                keep = lo_half if e == 0 else ~lo_half
                colb = jnp.broadcast_to(a_cum[:, h:h + 1], (L, L))
                rowb = a_cum_t[h:h + 1, :]
                decay = jnp.exp(jnp.where(causal, colb - rowb, NEG_BIG))
                w_in = (cb * decay) * dt_t[h:h + 1, :]
                w_off = c_g * jnp.exp(colb)
                lhs = jnp.concatenate([w_in, w_off], axis=1).astype(BF16)
                rhs = jnp.concatenate([jnp.where(keep, xs_tile, jnp.zeros_like(xs_tile)),
                                       jnp.where(keep, st_b, jnp.zeros_like(st_b))], axis=0)
                y_tile = y_tile + jnp.dot(lhs, rhs, preferred_element_type=F32)
                m_h = (b_gt * wst_t[h:h + 1, :]).astype(BF16)
                new_tile = new_tile + jnp.dot(m_h, jnp.where(keep, xs_tile, jnp.zeros_like(xs_tile)),
                                              preferred_element_type=F32)
            cd = jnp.where(lo_half[0:1, :], cdec_t[2 * tile:2 * tile + 1, :],
                           cdec_t[2 * tile + 1:2 * tile + 2, :])
            state_ref[:, c0:c0 + LANES] = st_tile * cd + new_tile
            y_full = y_tile + dskip_ref[:, c0:c0 + LANES] * xs[:, c0:c0 + LANES]
            gated.append(y_full * _silu(z_ref[:, c0:c0 + LANES].astype(F32)))

    gw = SSD_WIDTH // SSD_GROUPS
    tiles_per_group = gw // LANES
    for g in range(SSD_GROUPS):
        yg = jnp.concatenate(gated[g * tiles_per_group:(g + 1) * tiles_per_group], axis=1)
        ms = jnp.mean(yg * yg, axis=-1, keepdims=True)
        o_ref[:, g * gw:(g + 1) * gw] = ((yg * lax.rsqrt(ms + EPS)) * nw_ref[:, g * gw:(g + 1) * gw]).astype(o_ref.dtype)


def _ssd(xbc, z, dt, conv_w, conv_b, dt_bias, a_log, d_skip, ssd_norm_w):
    nc = SEQ // CHUNK
    L = CHUNK
    dt_t = dt[:, 0:SSD_HEADS].T
    pad_row = lambda v: jnp.pad(v.astype(F32), (0, LANES - SSD_HEADS)).reshape(1, LANES)
    col8 = lambda v: v.astype(F32).reshape(SSD_HEADS, 1)
    idx = jnp.arange(L)
    tril = (idx[None, :] <= idx[:, None]).astype(BF16)
    triu = (idx[:, None] <= idx[None, :]).astype(BF16)
    dskip = jnp.repeat(d_skip.astype(F32), SSD_HEAD_DIM).reshape(1, SSD_WIDTH)
    const = lambda shape: pl.BlockSpec(shape, lambda b, c: (0, 0))
    tok = lambda width: pl.BlockSpec((L, width), lambda b, c: (b * nc + c, 0))
    return pl.pallas_call(
        _ssd_kernel,
        grid=(BATCH, nc),
        in_specs=[tok(XBC_WIDTH), tok(SSD_WIDTH), tok(LANES),
                  pl.BlockSpec((SSD_HEADS, L), lambda b, c: (0, b * nc + c)),
                  const((CONV_K, XBC_WIDTH)), const((1, XBC_WIDTH)),
                  const((1, LANES)), const((SSD_HEADS, 1)), const((1, LANES)), const((SSD_HEADS, 1)),
                  const((1, SSD_WIDTH)), const((1, SSD_WIDTH)), const((L, L)), const((L, L))],
        out_specs=tok(SSD_WIDTH),
        out_shape=jax.ShapeDtypeStruct((TOKENS, SSD_WIDTH), BF16),
        scratch_shapes=[pltpu.VMEM((8 + L, XBC_WIDTH), F32),
                        pltpu.VMEM((SSD_STATE, SSD_WIDTH), F32)],
        compiler_params=_cparams(("arbitrary", "arbitrary")),
        name="ssd",
    )(xbc, z, dt, dt_t, conv_w.astype(F32), conv_b.astype(F32).reshape(1, XBC_WIDTH),
      pad_row(dt_bias), col8(dt_bias), pad_row(a_log), col8(a_log), dskip,
      ssd_norm_w.astype(F32).reshape(1, SSD_WIDTH), tril, triu)


OUT_TM = 256
ROUTE_W = 8
ROUTER_COLS = N_GROUPS + N_EXPERTS
RUN_ALIGN = 16
RUN_SHIFT = 4
LOCAL_ROWS = 1024
assert RUN_ALIGN == 1 << RUN_SHIFT and LOCAL_ROWS >= TOP_K * OUT_TM + N_EXPERTS * (RUN_ALIGN - 1)


def _lane_pick(values, lane, index):
    return jnp.sum(jnp.where(lane == index, values, 0.0), axis=-1, keepdims=True)


def _first_argmax(vals, lane):
    m = jnp.max(vals, axis=-1, keepdims=True)
    idx = jnp.min(jnp.where(vals == m, lane, float(LANES)), axis=-1, keepdims=True)
    return m, idx


def _out_router_kernel(att_ref, y_ref, x_ref, g1_ref, wo_ref, nw_ref, sc_ref, sh_ref, wr_ref, br_ref,
                       ltri_ref, sut_ref, x1_ref, h2_ref, route_ref, routet_ref, tcnt_ref,
                       wr_split_ref, logits_ref):
    i = pl.program_id(0)

    @pl.when(i == 0)
    def _():
        hi, lo = _split_bf16(wr_ref[...])
        wr_split_ref[:, 0:LANES] = hi
        wr_split_ref[:, LANES:2 * LANES] = lo
        logits_ref[...] = jnp.zeros_like(logits_ref)

    logits = logits_ref[...]

    mixer = (jnp.dot(att_ref[...], wo_ref[0:ATT_WIDTH, :], preferred_element_type=F32)
             + jnp.dot(y_ref[...], wo_ref[ATT_WIDTH:ATT_WIDTH + SSD_WIDTH, :], preferred_element_type=F32))
    x1 = x_ref[...] + g1_ref[0] * mixer
    x1_ref[...] = x1
    yn = x1 * lax.rsqrt(jnp.mean(x1 * x1, axis=-1, keepdims=True) + EPS)
    h2 = (yn * nw_ref[...]) * (1.0 + sc_ref[0]) + sh_ref[0]
    h2_ref[...] = h2.astype(BF16)

    h_hi, h_lo = _split_bf16(h2)
    both = jnp.dot(h_hi, wr_split_ref[...], preferred_element_type=F32)
    logits_ref[...] = (both[:, 0:LANES] + both[:, LANES:2 * LANES]
                       + jnp.dot(h_lo, wr_split_ref[:, 0:LANES], preferred_element_type=F32)) + br_ref[...]

    tm = logits.shape[0]
    lane = lax.broadcasted_iota(I32, (tm, LANES), 1).astype(F32)

    gl = jnp.where(lane < N_GROUPS, logits, NEG_BIG)
    gmax, gidx = _first_argmax(gl, lane)
    g_p = 1.0 / jnp.sum(jnp.exp(gl - gmax), axis=-1, keepdims=True)

    lo_lane = N_GROUPS + EXPERTS_PER_GROUP * gidx
    el = jnp.where((lane >= lo_lane) & (lane < lo_lane + EXPERTS_PER_GROUP), logits, NEG_BIG)
    m1, i1 = _first_argmax(el, lane)
    m2, i2 = _first_argmax(jnp.where(lane == i1, NEG_BIG, el), lane)
    r = jnp.exp(m2 - m1)
    p1 = 1.0 / (1.0 + r)
    p2 = r / (1.0 + r)
    e0 = i1 - N_GROUPS
    e1 = i2 - N_GROUPS

    onehot = ((lane == e0) | (lane == e1)).astype(F32)
    tile_cnt = jnp.sum(onehot, axis=0, keepdims=True)
    run_len = jnp.floor((tile_cnt + (RUN_ALIGN - 1)) * (1.0 / RUN_ALIGN)) * RUN_ALIGN
    run_start = jnp.dot(jnp.broadcast_to(run_len, (8, LANES)).astype(BF16), sut_ref[...],
                        preferred_element_type=F32)[0:1, :]
    before = jnp.dot(ltri_ref[...], onehot.astype(BF16), preferred_element_type=F32) + run_start
    slot0 = _lane_pick(before, lane, e0)
    slot1 = _lane_pick(before, lane, e1)
    tcnt_ref[0] = tile_cnt

    rec = jnp.zeros((tm, LANES), F32)
    for k, v in enumerate([slot0, slot1, g_p * p1, g_p * p2, e0, e1]):
        rec = jnp.where(lane == k, v, rec)
    route_ref[...] = rec[:, 0:ROUTE_W]
    routet_ref[...] = rec.T[0:ROUTE_W, :]


def _out_router(att, y, x2d, mod3, w_out_b, norm_w, w_router, b_router):
    tm = OUT_TM
    n_steps = TOKENS // tm
    steps_per_batch = SEQ // tm
    idx = jnp.arange(tm)
    ltri = (idx[None, :] < idx[:, None]).astype(BF16)
    lidx = jnp.arange(LANES)
    sut = (lidx[:, None] < lidx[None, :]).astype(BF16)
    const = lambda shape: pl.BlockSpec(shape, lambda i: (0, 0))
    cur = lambda i: jnp.minimum(i, n_steps - 1)
    prev = lambda i: jnp.maximum(i - 1, 0)
    tok = lambda width: pl.BlockSpec((tm, width), lambda i: (cur(i), 0))
    modspec = lambda k: pl.BlockSpec((1, 1, D_MODEL), lambda i: ((cur(i) // steps_per_batch) * 6 + k, 0, 0))
    return pl.pallas_call(
        _out_router_kernel,
        grid=(n_steps + 1,),
        in_specs=[tok(ATT_WIDTH), tok(SSD_WIDTH), tok(D_MODEL), modspec(2),
                  const((D_MODEL, D_MODEL)), const((1, D_MODEL)), modspec(4), modspec(3),
                  const((D_MODEL, LANES)), const((1, LANES)), const((tm, tm)), const((LANES, LANES))],
        out_specs=[tok(D_MODEL), tok(D_MODEL),
                   pl.BlockSpec((tm, ROUTE_W), lambda i: (prev(i), 0)),
                   pl.BlockSpec((ROUTE_W, tm), lambda i: (prev(i), 0)),
                   pl.BlockSpec((1, 1, LANES), lambda i: (prev(i), 0, 0))],
        out_shape=[jax.ShapeDtypeStruct((TOKENS, D_MODEL), F32),
                   jax.ShapeDtypeStruct((TOKENS, D_MODEL), BF16),
                   jax.ShapeDtypeStruct((TOKENS, ROUTE_W), F32),
                   jax.ShapeDtypeStruct((n_steps * ROUTE_W, tm), F32),
                   jax.ShapeDtypeStruct((n_steps, 1, LANES), F32)],
        scratch_shapes=[pltpu.VMEM((D_MODEL, 2 * LANES), BF16), pltpu.VMEM((tm, LANES), F32)],
        compiler_params=_cparams(("arbitrary",)),
        name="out_router",
    )(att, y, x2d, mod3, w_out_b, norm_w.reshape(1, D_MODEL), mod3, mod3, w_router, b_router, ltri, sut)


MOE_TM = 512
ZERO_ROWS = 256
N_TOKEN_TILES = TOKENS // OUT_TM
MAX_PIECES = LOCAL_ROWS // RUN_ALIGN
MAX_SORTED_ROWS = TOKENS * TOP_K + N_TOKEN_TILES * N_EXPERTS * (RUN_ALIGN - 1)
N_TILES = MAX_SORTED_ROWS // MOE_TM + N_EXPERTS
N_ROWS = N_TILES * MOE_TM
assert MOE_TM % ZERO_ROWS == 0


def _for_pieces(count, action):
    def body(j, carry):
        action(j)
        return carry

    lax.fori_loop(0, count, body, 0)


def _dispatch_kernel(seg_end_ref, cnt_ref, np_ref, dst_ref, routet_ref, h2_ref, xs_ref,
                     sbuf_ref, zero_ref, sems, zsem):
    i = pl.program_id(0)
    last = pl.num_programs(0) - 1
    buf = lax.rem(i, 2)

    @pl.when(i == 0)
    def _():
        zero_ref[...] = jnp.zeros_like(zero_ref)
        blocks_per_tile = MOE_TM // ZERO_ROWS

        def zcopy(block):
            start = pl.multiple_of(block * ZERO_ROWS, ZERO_ROWS)
            return pltpu.make_async_copy(zero_ref, xs_ref.at[pl.ds(start, ZERO_ROWS)], zsem)

        def tails(action):
            def body(e, carry):
                @pl.when(cnt_ref[e] > 0)
                def _():
                    for b in range(blocks_per_tile):
                        action(zcopy(seg_end_ref[e] // ZERO_ROWS - blocks_per_tile + b))
                return carry

            lax.fori_loop(0, N_EXPERTS, body, 0)

        def unused(action):
            def body(block, carry):
                action(zcopy(block))
                return carry

            lax.fori_loop(seg_end_ref[N_EXPERTS - 1] // ZERO_ROWS, N_ROWS // ZERO_ROWS, body, 0)

        tails(lambda cp: cp.start())
        unused(lambda cp: cp.start())
        tails(lambda cp: cp.wait())
        unused(lambda cp: cp.wait())

    slot = lax.broadcasted_iota(I32, (LOCAL_ROWS, OUT_TM), 0).astype(F32)
    perm = jnp.where((slot == routet_ref[0:1, :]) | (slot == routet_ref[1:2, :]), 1.0, 0.0).astype(BF16)
    sbuf_ref[buf] = jnp.dot(perm, h2_ref[...], preferred_element_type=F32).astype(BF16)

    def piece(b, local_piece, sorted_row):
        local = pl.multiple_of(local_piece * RUN_ALIGN, RUN_ALIGN)
        return pltpu.make_async_copy(sbuf_ref.at[b, pl.ds(local, RUN_ALIGN)],
                                     xs_ref.at[pl.ds(pl.multiple_of(sorted_row, RUN_ALIGN), RUN_ALIGN)],
                                     sems.at[b])

    _for_pieces(np_ref[i], lambda j: piece(buf, j, dst_ref[0, 0, j]).start())

    @pl.when(i > 0)
    def _():
        _for_pieces(np_ref[jnp.maximum(i - 1, 0)], lambda j: piece(1 - buf, j, 0).wait())

    @pl.when(i == last)
    def _():
        _for_pieces(np_ref[i], lambda j: piece(buf, j, 0).wait())


def _piece_spec(index_map):
    return pl.BlockSpec((1, 1, MAX_PIECES), index_map, memory_space=pltpu.SMEM)


def _dispatch(seg_end, counts, n_pieces, piece_dst, route_t, h2):
    grid_spec = pltpu.PrefetchScalarGridSpec(
        num_scalar_prefetch=3,
        grid=(N_TOKEN_TILES,),
        in_specs=[_piece_spec(lambda i, se, cn, npc: (i, 0, 0)),
                  pl.BlockSpec((ROUTE_W, OUT_TM), lambda i, se, cn, npc: (i, 0)),
                  pl.BlockSpec((OUT_TM, D_MODEL), lambda i, se, cn, npc: (i, 0))],
        out_specs=pl.BlockSpec(memory_space=pl.ANY),
        scratch_shapes=[pltpu.VMEM((2, LOCAL_ROWS, D_MODEL), BF16),
                        pltpu.VMEM((ZERO_ROWS, D_MODEL), BF16),
                        pltpu.SemaphoreType.DMA((2,)), pltpu.SemaphoreType.DMA],
    )
    return pl.pallas_call(
        _dispatch_kernel,
        grid_spec=grid_spec,
        out_shape=jax.ShapeDtypeStruct((N_ROWS, D_MODEL), BF16),
        compiler_params=_cparams(("arbitrary",)),
        name="dispatch",
    )(seg_end, counts, n_pieces, piece_dst, route_t, h2)


def _experts_kernel(te_ref, nu_ref, xs_ref, wg_ref, wu_ref, wd_ref, ys_ref, wgu_b_ref, wd_b_ref):
    i = pl.program_id(0)
    used = i < nu_ref[0]

    @pl.when(used & ((i == 0) | (te_ref[i] != te_ref[jnp.maximum(i - 1, 0)])))
    def _():
        wgu_b_ref[:, 0:D_EXPERT] = wg_ref[0].astype(BF16)
        wgu_b_ref[:, D_EXPERT:2 * D_EXPERT] = wu_ref[0].astype(BF16)
        wd_b_ref[...] = wd_ref[0].astype(BF16)

    @pl.when(used)
    def _():
        h = jnp.dot(xs_ref[...], wgu_b_ref[...], preferred_element_type=F32)
        act = (_silu(h[:, 0:D_EXPERT]) * h[:, D_EXPERT:2 * D_EXPERT]).astype(BF16)
        ys_ref[...] = jnp.dot(act, wd_b_ref[...], preferred_element_type=F32).astype(BF16)

    @pl.when(jnp.logical_not(used))
    def _():
        ys_ref[...] = jnp.zeros_like(ys_ref)


def _experts(tile_expert, n_used, xs, w_gate, w_up, w_down):
    row_tile = lambda i, te, nu: (jnp.minimum(i, nu[0] - 1), 0)
    grid_spec = pltpu.PrefetchScalarGridSpec(
        num_scalar_prefetch=2,
        grid=(N_TILES,),
        in_specs=[pl.BlockSpec((MOE_TM, D_MODEL), row_tile),
                  pl.BlockSpec((1, D_MODEL, D_EXPERT), lambda i, te, nu: (te[i], 0, 0)),
                  pl.BlockSpec((1, D_MODEL, D_EXPERT), lambda i, te, nu: (te[i], 0, 0)),
                  pl.BlockSpec((1, D_EXPERT, D_MODEL), lambda i, te, nu: (te[i], 0, 0))],
        out_specs=pl.BlockSpec((MOE_TM, D_MODEL), lambda i, te, nu: (i, 0)),
        scratch_shapes=[pltpu.VMEM((D_MODEL, 2 * D_EXPERT), BF16), pltpu.VMEM((D_EXPERT, D_MODEL), BF16)],
    )
    return pl.pallas_call(
        _experts_kernel,
        grid_spec=grid_spec,
        out_shape=jax.ShapeDtypeStruct((N_ROWS, D_MODEL), BF16),
        compiler_params=_cparams(("arbitrary",)),
        name="experts",
    )(tile_expert, n_used, xs, w_gate, w_up, w_down)


def _combine_kernel(np_ref, dst_ref, dst_next_ref, route_ref, x1_ref, g2_ref, ys_ref, o_ref, gbuf_ref, sems):
    i = pl.program_id(0)
    last = pl.num_programs(0) - 1
    buf = lax.rem(i, 2)

    def piece(b, local_piece, sorted_row):
        local = pl.multiple_of(local_piece * RUN_ALIGN, RUN_ALIGN)
        return pltpu.make_async_copy(ys_ref.at[pl.ds(pl.multiple_of(sorted_row, RUN_ALIGN), RUN_ALIGN)],
                                     gbuf_ref.at[b, pl.ds(local, RUN_ALIGN)], sems.at[b])

    @pl.when(i == 0)
    def _():
        gbuf_ref[...] = jnp.zeros_like(gbuf_ref)
        _for_pieces(np_ref[0], lambda j: piece(0, j, dst_ref[0, 0, j]).start())

    @pl.when(i < last)
    def _():
        _for_pieces(np_ref[jnp.minimum(i + 1, last)], lambda j: piece(1 - buf, j, dst_next_ref[0, 0, j]).start())

    rec = route_ref[...]
    slot = lax.broadcasted_iota(I32, (OUT_TM, LOCAL_ROWS), 1).astype(F32)
    weights = (jnp.where(slot == rec[:, 0:1], rec[:, 2:3], 0.0)
               + jnp.where(slot == rec[:, 1:2], rec[:, 3:4], 0.0)).astype(BF16)
    _for_pieces(np_ref[i], lambda j: piece(buf, j, 0).wait())
    moe = jnp.dot(weights, gbuf_ref[buf], preferred_element_type=F32)
    o_ref[...] = x1_ref[...] + g2_ref[0] * moe


def _combine(n_pieces, piece_dst, route, x1, mod3, ys):
    tm = OUT_TM
    steps_per_batch = SEQ // tm
    grid_spec = pltpu.PrefetchScalarGridSpec(
        num_scalar_prefetch=1,
        grid=(N_TOKEN_TILES,),
        in_specs=[_piece_spec(lambda i, npc: (i, 0, 0)),
                  _piece_spec(lambda i, npc: (jnp.minimum(i + 1, N_TOKEN_TILES - 1), 0, 0)),
                  pl.BlockSpec((tm, ROUTE_W), lambda i, npc: (i, 0)),
                  pl.BlockSpec((tm, D_MODEL), lambda i, npc: (i, 0)),
                  pl.BlockSpec((1, 1, D_MODEL), lambda i, npc: ((i // steps_per_batch) * 6 + 5, 0, 0)),
                  pl.BlockSpec(memory_space=pl.ANY)],
        out_specs=pl.BlockSpec((tm, D_MODEL), lambda i, npc: (i, 0)),
        scratch_shapes=[pltpu.VMEM((2, LOCAL_ROWS, D_MODEL), BF16), pltpu.SemaphoreType.DMA((2,))],
    )
    return pl.pallas_call(
        _combine_kernel,
        grid_spec=grid_spec,
        out_shape=jax.ShapeDtypeStruct((TOKENS, D_MODEL), F32),
        compiler_params=_cparams(("arbitrary",)),
        name="combine",
    )(n_pieces, piece_dst, piece_dst, route, x1, mod3, ys)


def kernel(x, c, positions, norm1_w, norm2_w, w_ada, b_ada, w_in, conv_w, conv_b, dt_bias, a_log,
           d_skip, ssd_norm_w, q_norm_w, k_norm_w, sinks, w_out, w_group, b_group, w_expert, b_expert,
           w_gate, w_up, w_down):
    assert x.shape == (BATCH, SEQ, D_MODEL) and w_in.shape == (D_MODEL, IN_WIDTH)
    x2d = x.reshape(TOKENS, D_MODEL)
    mod = _ada_mod(c, w_ada, b_ada)
    mod3 = mod.reshape(BATCH * 6, 1, D_MODEL)

    w_in_pad = jnp.pad(w_in, ((0, 0), (0, IN_PAD - IN_WIDTH))).astype(BF16)
    qkv, z, xbc, dt = _in_proj(x2d, norm1_w, mod3, w_in_pad)
    att = _attention(qkv, positions, q_norm_w, k_norm_w, sinks)
    y = _ssd(xbc, z, dt, conv_w, conv_b, dt_bias, a_log, d_skip, ssd_norm_w)

    w_router = jnp.pad(jnp.concatenate([w_group, w_expert], axis=1).astype(F32),
                       ((0, 0), (0, LANES - ROUTER_COLS)))
    b_router = jnp.pad(jnp.concatenate([b_group, b_expert]).astype(F32),
                       (0, LANES - ROUTER_COLS)).reshape(1, LANES)
    x1, h2, route, route_t, tcnt = _out_router(att, y, x2d, mod3, w_out.astype(BF16), norm2_w,
                                                w_router, b_router)

    tc = tcnt[:, 0, 0:N_EXPERTS].astype(I32)
    run_rows = ((tc + RUN_ALIGN - 1) // RUN_ALIGN) * RUN_ALIGN
    counts = jnp.sum(run_rows, axis=0)
    padded = ((counts + MOE_TM - 1) // MOE_TM) * MOE_TM
    seg_end = jnp.cumsum(padded)
    seg_start = seg_end - padded
    run_dst = seg_start[None, :] + jnp.cumsum(run_rows, axis=0) - run_rows
    n_used = (seg_end[-1] // MOE_TM).reshape(1)
    last_row = jnp.minimum(jnp.arange(N_TILES, dtype=I32) * MOE_TM, seg_end[-1] - 1)
    tile_expert = jnp.sum((seg_end[None, :] <= last_row[:, None]).astype(I32), axis=1)

    run_pieces = run_rows // RUN_ALIGN
    piece_end = jnp.cumsum(run_pieces, axis=1)
    n_pieces = piece_end[:, -1]
    j = jnp.arange(MAX_PIECES, dtype=I32)
    piece_expert = jnp.sum((piece_end[:, None, :] <= j[None, :, None]).astype(I32), axis=2)
    in_run = (piece_expert[:, :, None] == jnp.arange(N_EXPERTS, dtype=I32)[None, None, :]).astype(I32)
    run_base = run_dst - RUN_ALIGN * (piece_end - run_pieces)
    piece_dst = jnp.sum(in_run * run_base[:, None, :], axis=2) + RUN_ALIGN * j[None, :]
    piece_dst = piece_dst.astype(I32).reshape(N_TOKEN_TILES, 1, MAX_PIECES)

    xs = _dispatch(seg_end.astype(I32), counts, n_pieces.astype(I32), piece_dst, route_t, h2)
    ys = _experts(tile_expert, n_used.astype(I32), xs, w_gate, w_up, w_down)
    out = _combine(n_pieces.astype(I32), piece_dst, route, x1, mod3, ys)
    return out.reshape(BATCH, SEQ, D_MODEL)
```

```python
import jax
import jax.numpy as jnp
from jax import lax
from jax.experimental import pallas as pl
from jax.experimental.pallas import tpu as pltpu

F32 = jnp.float32
BF16 = jnp.bfloat16
I32 = jnp.int32

D_MODEL = 1024
BATCH = 2
SEQ = 8192
TOKENS = BATCH * SEQ
ATT_HEADS = 8
ATT_KV_HEADS = 2
HEAD_DIM = 64
ATT_WIDTH = ATT_HEADS * HEAD_DIM
KV_WIDTH = ATT_KV_HEADS * HEAD_DIM
ATT_BLOCK = 128
ROPE_DIM = HEAD_DIM // 4
ROPE_THETA = 500000.0
SSD_HEADS = 8
SSD_HEAD_DIM = 64
SSD_WIDTH = SSD_HEADS * SSD_HEAD_DIM
SSD_GROUPS = 2
SSD_STATE = 128
CONV_K = 4
CHUNK = 128
XBC_WIDTH = SSD_WIDTH + 2 * SSD_GROUPS * SSD_STATE
IN_WIDTH = ATT_WIDTH + 2 * KV_WIDTH + SSD_WIDTH + XBC_WIDTH + SSD_HEADS
N_GROUPS = 4
EXPERTS_PER_GROUP = 8
N_EXPERTS = N_GROUPS * EXPERTS_PER_GROUP
TOP_K = 2
D_EXPERT = 256
EPS = 1e-6

LANES = 128
QKV_WIDTH = ATT_WIDTH + 2 * KV_WIDTH
IN_PAD = QKV_WIDTH + SSD_WIDTH + XBC_WIDTH + LANES
NEG_BIG = -1e30

VMEM_LIMIT = 48 * 1024 * 1024


def _cparams(sem):
    return pltpu.CompilerParams(dimension_semantics=sem, vmem_limit_bytes=VMEM_LIMIT)


def _split_bf16(x):
    hi = x.astype(BF16)
    lo = (x - hi.astype(F32)).astype(BF16)
    return hi, lo


ADA_TN = 768


def _ada_kernel(ct_ref, w_ref, b_ref, o_ref):
    ct = ct_ref[...]
    s = ct * jax.nn.sigmoid(ct)
    w = w_ref[...]
    rows = [jnp.sum(s[:, b:b + 1] * w, axis=0, keepdims=True) for b in range(BATCH)]
    o_ref[...] = jnp.concatenate(rows, axis=0) + b_ref[...]


def _ada_mod(c, w_ada, b_ada):
    n = w_ada.shape[1]
    return pl.pallas_call(
        _ada_kernel,
        grid=(n // ADA_TN,),
        in_specs=[pl.BlockSpec((D_MODEL, BATCH), lambda j: (0, 0)),
                  pl.BlockSpec((D_MODEL, ADA_TN), lambda j: (0, j)),
                  pl.BlockSpec((1, ADA_TN), lambda j: (0, j))],
        out_specs=pl.BlockSpec((BATCH, ADA_TN), lambda j: (0, j)),
        out_shape=jax.ShapeDtypeStruct((BATCH, n), F32),
        compiler_params=_cparams(("arbitrary",)),
        name="ada_mod",
    )(c.T, w_ada, b_ada.reshape(1, n))


INPROJ_TM = 512
_INPROJ_CHUNK = 256


def _inproj_kernel(x_ref, nw_ref, sc_ref, sh_ref, w_ref, qkv_ref, z_ref, xbc_ref, dt_ref):
    x = x_ref[...]
    y = x * lax.rsqrt(jnp.mean(x * x, axis=-1, keepdims=True) + EPS)
    h = (y * nw_ref[...]) * (1.0 + sc_ref[0]) + sh_ref[0]
    hb = h.astype(BF16)

    def proj(c0, c1):
        return jnp.dot(hb, w_ref[:, c0:c1], preferred_element_type=F32)

    for c0 in range(0, QKV_WIDTH, _INPROJ_CHUNK):
        qkv_ref[:, c0:c0 + _INPROJ_CHUNK] = proj(c0, c0 + _INPROJ_CHUNK).astype(BF16)
    base = QKV_WIDTH
    for c0 in range(0, SSD_WIDTH, _INPROJ_CHUNK):
        z_ref[:, c0:c0 + _INPROJ_CHUNK] = proj(base + c0, base + c0 + _INPROJ_CHUNK).astype(BF16)
    base += SSD_WIDTH
    for c0 in range(0, XBC_WIDTH, _INPROJ_CHUNK):
        xbc_ref[:, c0:c0 + _INPROJ_CHUNK] = proj(base + c0, base + c0 + _INPROJ_CHUNK).astype(BF16)
    base += XBC_WIDTH
    dt_ref[...] = proj(base, base + LANES)


def _in_proj(x2d, norm_w, mod3, w_in_pad):
    tm = INPROJ_TM
    steps_per_batch = SEQ // tm
    return pl.pallas_call(
        _inproj_kernel,
        grid=(TOKENS // tm,),
        in_specs=[pl.BlockSpec((tm, D_MODEL), lambda i: (i, 0)),
                  pl.BlockSpec((1, D_MODEL), lambda i: (0, 0)),
                  pl.BlockSpec((1, 1, D_MODEL), lambda i: ((i // steps_per_batch) * 6 + 1, 0, 0)),
                  pl.BlockSpec((1, 1, D_MODEL), lambda i: ((i // steps_per_batch) * 6 + 0, 0, 0)),
                  pl.BlockSpec((D_MODEL, IN_PAD), lambda i: (0, 0))],
        out_specs=[pl.BlockSpec((tm, QKV_WIDTH), lambda i: (i, 0)),
                   pl.BlockSpec((tm, SSD_WIDTH), lambda i: (i, 0)),
                   pl.BlockSpec((tm, XBC_WIDTH), lambda i: (i, 0)),
                   pl.BlockSpec((tm, LANES), lambda i: (i, 0))],
        out_shape=[jax.ShapeDtypeStruct((TOKENS, QKV_WIDTH), BF16),
                   jax.ShapeDtypeStruct((TOKENS, SSD_WIDTH), BF16),
                   jax.ShapeDtypeStruct((TOKENS, XBC_WIDTH), BF16),
                   jax.ShapeDtypeStruct((TOKENS, LANES), F32)],
        compiler_params=_cparams(("arbitrary",)),
        name="in_proj",
    )(x2d, norm_w.reshape(1, D_MODEL), mod3, mod3, w_in_pad)


ATT_SUB = 2


ROPE_TM = 2048
_ROPE_HALF = ROPE_DIM // 2
_TOK_PER_ROW = LANES // _ROPE_HALF


def _exact_dot(x, onehot_b):
    hi, lo = _split_bf16(x)
    return (jnp.dot(hi, onehot_b, preferred_element_type=F32)
            + jnp.dot(lo, onehot_b, preferred_element_type=F32))


def _rope_kernel(pos_ref, freq_ref, sel_ref, own_ref, gcos_ref, gs1_ref, gs2_ref, ident_ref,
                 cos_ref, s1_ref, s2_ref):
    ang = pos_ref[...].astype(F32) * freq_ref[...]
    cos_p, sin_p = jnp.cos(ang), jnp.sin(ang)
    hi_c, lo_c = _split_bf16(cos_p)
    hi_s, lo_s = _split_bf16(sin_p)
    sel = sel_ref[...]
    rows_c = jnp.dot(sel, hi_c, preferred_element_type=F32) + jnp.dot(sel, lo_c, preferred_element_type=F32)
    rows_s = jnp.dot(sel, hi_s, preferred_element_type=F32) + jnp.dot(sel, lo_s, preferred_element_type=F32)
    own = own_ref[...]
    cos_ref[...] = _exact_dot(rows_c * own, gcos_ref[...]) + ident_ref[...]
    s1_ref[...] = _exact_dot(rows_s * own, gs1_ref[...])
    s2_ref[...] = _exact_dot(rows_s * own, gs2_ref[...])


def _rope_tables(positions):
    half, per_row = _ROPE_HALF, _TOK_PER_ROW
    rows = ROPE_TM // per_row
    pos_rep = jnp.repeat(positions.reshape(TOKENS).astype(I32), half).reshape(TOKENS // per_row, LANES)
    inv_freq = jnp.power(ROPE_THETA, -jnp.arange(half, dtype=F32) * 2.0 / ROPE_DIM)
    freq = jnp.tile(inv_freq, per_row).reshape(1, LANES)
    tok = jnp.arange(ROPE_TM)
    lane = jnp.arange(LANES)
    sel = (tok[:, None] // per_row == jnp.arange(rows)[None, :]).astype(BF16)
    own = (lane[None, :] // half == tok[:, None] % per_row).astype(F32)
    d = lane % HEAD_DIM
    src_f = lane % half
    hits = lambda lo, hi: ((src_f[:, None] == d[None, :] % half) & (d[None, :] >= lo) & (d[None, :] < hi))
    gcos = hits(0, ROPE_DIM).astype(BF16)
    gs1 = -hits(0, half).astype(BF16)
    gs2 = hits(half, ROPE_DIM).astype(BF16)
    ident = (d >= ROPE_DIM).astype(F32).reshape(1, LANES)
    const = lambda shape: pl.BlockSpec(shape, lambda i: (0, 0))
    out_spec = pl.BlockSpec((ROPE_TM, LANES), lambda i: (i, 0))
    out = jax.ShapeDtypeStruct((TOKENS, LANES), F32)
    return pl.pallas_call(
        _rope_kernel,
        grid=(TOKENS // ROPE_TM,),
        in_specs=[pl.BlockSpec((rows, LANES), lambda i: (i, 0)), const((1, LANES)),
                  const((ROPE_TM, rows)), const((ROPE_TM, LANES)),
                  const((LANES, LANES)), const((LANES, LANES)), const((LANES, LANES)), const((1, LANES))],
        out_specs=[out_spec, out_spec, out_spec],
        out_shape=[out, out, out],
        compiler_params=_cparams(("arbitrary",)),
        name="rope_tables",
    )(pos_rep, freq, sel, own, gcos, gs1, gs2, ident)


def _seg_meansq(xf, ones128):
    rows, width = xf.shape
    nt = width // LANES
    parts = _split_bf16(xf * xf)
    stacked = jnp.concatenate([p[:, t * LANES:(t + 1) * LANES] for p in parts for t in range(nt)], axis=0)
    tot = jnp.dot(stacked, ones128, preferred_element_type=F32)
    tiles = [tot[t * rows:(t + 1) * rows] + tot[(nt + t) * rows:(nt + t + 1) * rows] for t in range(nt)]
    return jnp.concatenate(tiles, axis=1) * (1.0 / HEAD_DIM)


def _norm_rope(x_bf, w_row, ones_bd, cosf, s1, s2):
    xf = x_bf.astype(F32)
    width = xf.shape[1]
    xn = xf * lax.rsqrt(_seg_meansq(xf, ones_bd) + EPS) * w_row
    half = ROPE_DIM // 2
    up = pltpu.roll(xn, width - half, axis=1)
    down = pltpu.roll(xn, half, axis=1)
    return xn * cosf + up * s1 + down * s2


def _attn_kernel(sink_ref, q_ref, kv_ref, cos_ref, s1_ref, s2_ref, qw_ref, kw_ref,
                 ones_ref, o_ref, kprev_ref, vprev_ref):
    j = pl.program_id(1)
    blk = ATT_BLOCK

    @pl.when(j == 0)
    def _():
        kprev_ref[...] = jnp.zeros_like(kprev_ref)
        vprev_ref[...] = jnp.zeros_like(vprev_ref)

    cos1 = cos_ref[...]
    s1_1 = s1_ref[...]
    s2_1 = s2_ref[...]
    reps = ATT_WIDTH // LANES
    cosq = jnp.concatenate([cos1] * reps, axis=1)
    s1q = jnp.concatenate([s1_1] * reps, axis=1)
    s2q = jnp.concatenate([s2_1] * reps, axis=1)

    q = _norm_rope(q_ref[...], qw_ref[...], ones_ref[...], cosq, s1q, s2q)
    qf = q * (HEAD_DIM ** -0.5)
    kv = kv_ref[...]
    kn = _norm_rope(kv[:, 0:KV_WIDTH], kw_ref[...], ones_ref[...], cos1, s1_1, s2_1)
    vn = kv[:, KV_WIDTH:2 * KV_WIDTH].astype(F32)

    kall = jnp.concatenate([kprev_ref[...], kn], axis=0)
    vall = jnp.concatenate([vprev_ref[...], vn], axis=0)
    kprev_ref[...] = kn[(ATT_SUB - 1) * blk:ATT_SUB * blk]
    vprev_ref[...] = vn[(ATT_SUB - 1) * blk:ATT_SUB * blk]

    lo_all = lax.broadcasted_iota(I32, kall.shape, 1) < HEAD_DIM
    ones_all = jnp.ones(kall.shape, BF16)

    row = lax.broadcasted_iota(I32, (2 * blk, blk), 0)
    col = lax.broadcasted_iota(I32, (2 * blk, blk), 1)
    from_prev = col > (row & (blk - 1))
    second_tile = lax.broadcasted_iota(I32, (2 * blk, 1), 0) >= blk
    zero_p = jnp.zeros((2 * blk, blk), F32)

    k_par, v_par = [], []
    for g in range(ATT_KV_HEADS):
        keep = lo_all if g == 0 else ~lo_all
        k_own = jnp.where(keep, kall, 0.0)
        v_own = jnp.where(keep, vall, 0.0)
        k_oth = pltpu.roll(k_own, HEAD_DIM, axis=1)
        v_oth = pltpu.roll(v_own, HEAD_DIM, axis=1)
        k_lo, k_hi = (k_own, k_oth) if g == 0 else (k_oth, k_own)
        v_lo, v_hi = (v_own, v_oth) if g == 0 else (v_oth, v_own)
        k_par.append((k_lo.astype(BF16), k_hi.astype(BF16)))
        v_par.append((jnp.concatenate([v_lo.astype(BF16), ones_all], axis=1),
                      jnp.concatenate([v_hi.astype(BF16), ones_all], axis=1)))

    problems = [(g, sub) for g in range(ATT_KV_HEADS) for sub in range(ATT_SUB)]
    scores = []
    for g, sub in problems:
        r0, c0 = sub * blk, g * 2 * LANES
        qcat = jnp.concatenate([qf[r0:r0 + blk, c0:c0 + LANES],
                                qf[r0:r0 + blk, c0 + LANES:c0 + 2 * LANES]], axis=0).astype(BF16)
        kw = jnp.concatenate([k_par[g][0][r0:r0 + 2 * blk], k_par[g][1][r0:r0 + 2 * blk]], axis=0)
        scores.append(lax.dot_general(qcat, kw, (((1,), (1,)), ((), ())),
                                      preferred_element_type=F32))

    weights, rescale = [], []
    for (g, sub), s_all in zip(problems, scores):
        for par in range(2):
            s = s_all[:, par * 2 * blk:(par + 1) * 2 * blk]
            s_prev = s[:, 0:blk]
            if sub == 0:
                s_prev = s_prev + jnp.where(j > 0, 0.0, NEG_BIG)
            s = jnp.where(from_prev, s_prev, s[:, blk:2 * blk])
            h_first = ATT_HEADS // ATT_KV_HEADS * g + par
            sink = jnp.where(second_tile, sink_ref[h_first + 2], sink_ref[h_first])
            m = jnp.maximum(jnp.max(s, axis=-1, keepdims=True), sink)
            p = jnp.exp(s - m)
            weights.append(jnp.concatenate([jnp.where(from_prev, p, zero_p), jnp.where(from_prev, zero_p, p)],
                                           axis=1).astype(BF16))
            rescale.append(jnp.exp(sink - m))

    outs = []
    for idx, (g, sub) in enumerate(problems):
        for par in range(2):
            outs.append(jnp.dot(weights[2 * idx + par], v_par[g][par][sub * blk:(sub + 2) * blk],
                                preferred_element_type=F32))

    for idx, (g, sub) in enumerate(problems):
        r0, c0 = sub * blk, g * 2 * LANES
        pair = None
        for par in range(2):
            o = outs[2 * idx + par]
            part = o[:, 0:LANES] * (1.0 / (o[:, LANES:2 * LANES] + rescale[2 * idx + par]))
            pair = part if pair is None else pair + part
        o_ref[r0:r0 + blk, c0:c0 + LANES] = pair[0:blk].astype(BF16)
        o_ref[r0:r0 + blk, c0 + LANES:c0 + 2 * LANES] = pair[blk:2 * blk].astype(BF16)


def _attention(qkv, positions, q_norm_w, k_norm_w, sinks):
    cosf, s1, s2 = _rope_tables(positions)
    qw = jnp.tile(q_norm_w.astype(F32), ATT_HEADS).reshape(1, ATT_WIDTH)
    kw = jnp.tile(k_norm_w.astype(F32), ATT_KV_HEADS).reshape(1, KV_WIDTH)
    seg = jnp.arange(LANES) // HEAD_DIM
    ones128 = (seg[:, None] == seg[None, :]).astype(BF16)
    const = lambda shape: pl.BlockSpec(shape, lambda b, j, s: (0, 0))
    rows = ATT_SUB * ATT_BLOCK
    nb = SEQ // rows
    tok = lambda width, cb: pl.BlockSpec((rows, width), lambda b, j, s: (b * nb + j, cb))
    grid_spec = pltpu.PrefetchScalarGridSpec(
        num_scalar_prefetch=1,
        grid=(BATCH, nb),
        in_specs=[tok(ATT_WIDTH, 0), tok(2 * KV_WIDTH, 2), tok(LANES, 0), tok(LANES, 0), tok(LANES, 0),
                  const((1, ATT_WIDTH)), const((1, KV_WIDTH)), const((LANES, LANES))],
        out_specs=tok(ATT_WIDTH, 0),
        scratch_shapes=[pltpu.VMEM((ATT_BLOCK, KV_WIDTH), F32),
                        pltpu.VMEM((ATT_BLOCK, KV_WIDTH), F32)],
    )
    return pl.pallas_call(
        _attn_kernel,
        grid_spec=grid_spec,
        out_shape=jax.ShapeDtypeStruct((TOKENS, ATT_WIDTH), BF16),
        compiler_params=_cparams(("arbitrary", "arbitrary")),
        name="attention",
    )(sinks.astype(F32), qkv, qkv, cosf, s1, s2, qw, kw, ones128)


def _softplus(x):
    return jnp.maximum(x, 0.0) + jnp.log1p(jnp.exp(-jnp.abs(x)))


def _silu(x):
    return x * jax.nn.sigmoid(x)


def _ssd_kernel(xbc_ref, z_ref, dt_ref, dtt_ref, cw_ref, cb_ref, dtb_row_ref, dtb_col_ref,
                alog_row_ref, alog_col_ref, dskip_ref, nw_ref, tril_ref, triu_ref,
                o_ref, conv_ref, state_ref):
    c = pl.program_id(1)
    L = CHUNK
    tail = 8

    @pl.when(c == 0)
    def _():
        conv_ref[0:tail, :] = jnp.zeros((tail, XBC_WIDTH), F32)
        state_ref[...] = jnp.zeros_like(state_ref)

    xb = xbc_ref[...].astype(F32)
    conv_ref[tail:tail + L, :] = xb
    acc = cb_ref[...] + cw_ref[CONV_K - 1:CONV_K, :] * xb
    for k in range(CONV_K - 1):
        off = tail - (CONV_K - 1) + k
        acc = acc + cw_ref[k:k + 1, :] * conv_ref[off:off + L, :]
    conv_ref[0:tail, :] = xb[L - tail:L, :]
    u = _silu(acc)
    xs = u[:, 0:SSD_WIDTH]
    bmat = u[:, SSD_WIDTH:SSD_WIDTH + SSD_GROUPS * SSD_STATE]
    cmat = u[:, SSD_WIDTH + SSD_GROUPS * SSD_STATE:XBC_WIDTH]

    dt = _softplus(dt_ref[...] + dtb_row_ref[...])
    a = dt * (-jnp.exp(alog_row_ref[...]))
    a_hi, a_lo = _split_bf16(a)
    a_cum = (jnp.dot(tril_ref[...], a_hi, preferred_element_type=F32)
             + jnp.dot(tril_ref[...], a_lo, preferred_element_type=F32))
    dt_t = _softplus(dtt_ref[...] + dtb_col_ref[...])
    a_t = dt_t * (-jnp.exp(alog_col_ref[...]))
    at_hi, at_lo = _split_bf16(a_t)
    a_cum_t = (jnp.dot(at_hi, triu_ref[...], preferred_element_type=F32)
               + jnp.dot(at_lo, triu_ref[...], preferred_element_type=F32))
    a_end_t = a_cum_t[:, L - 1:L]
    wst_t = jnp.exp(a_end_t - a_cum_t) * dt_t
    cdec_t = jnp.exp(a_end_t)

    row = lax.broadcasted_iota(I32, (L, L), 0)
    col = lax.broadcasted_iota(I32, (L, L), 1)
    causal = col <= row
    lane = lax.broadcasted_iota(I32, (L, LANES), 1)
    lo_half = lane < SSD_HEAD_DIM

    xs_b = xs.astype(BF16)
    heads_per_group = SSD_HEADS // SSD_GROUPS
    gated = []
    for g in range(SSD_GROUPS):
        b_g = bmat[:, g * SSD_STATE:(g + 1) * SSD_STATE]
        c_g = cmat[:, g * SSD_STATE:(g + 1) * SSD_STATE]
        cb = lax.dot_general(c_g.astype(BF16), b_g.astype(BF16), (((1,), (1,)), ((), ())),
                             preferred_element_type=F32)
        b_gt = b_g.T
        for t in range(heads_per_group // 2):
            tile = g * (heads_per_group // 2) + t
            c0 = tile * LANES
            xs_tile = xs_b[:, c0:c0 + LANES]
            st_tile = state_ref[:, c0:c0 + LANES]
            st_b = st_tile.astype(BF16)
            y_tile = jnp.zeros((L, LANES), F32)
            new_tile = jnp.zeros((SSD_STATE, LANES), F32)
            for e in range(2):
                h = 2 * tile + e
                keep = lo_half if e == 0 else ~lo_half
                colb = jnp.broadcast_to(a_cum[:, h:h + 1], (L, L))
                rowb = a_cum_t[h:h + 1, :]
                decay = jnp.exp(jnp.where(causal, colb - rowb, NEG_BIG))
                w_in = (cb * decay) * dt_t[h:h + 1, :]
                w_off = c_g * jnp.exp(colb)
                lhs = jnp.concatenate([w_in, w_off], axis=1).astype(BF16)
                rhs = jnp.concatenate([jnp.where(keep, xs_tile, jnp.zeros_like(xs_tile)),
                                       jnp.where(keep, st_b, jnp.zeros_like(st_b))], axis=0)
                y_tile = y_tile + jnp.dot(lhs, rhs, preferred_element_type=F32)
                m_h = (b_gt * wst_t[h:h + 1, :]).astype(BF16)
                new_tile = new_tile + jnp.dot(m_h, jnp.where(keep, xs_tile, jnp.zeros_like(xs_tile)),
                                              preferred_element_type=F32)
            cd = jnp.where(lo_half[0:1, :], cdec_t[2 * tile:2 * tile + 1, :],
                           cdec_t[2 * tile + 1:2 * tile + 2, :])
            state_ref[:, c0:c0 + LANES] = st_tile * cd + new_tile
            y_full = y_tile + dskip_ref[:, c0:c0 + LANES] * xs[:, c0:c0 + LANES]
            gated.append(y_full * _silu(z_ref[:, c0:c0 + LANES].astype(F32)))

    gw = SSD_WIDTH // SSD_GROUPS
    tiles_per_group = gw // LANES
    for g in range(SSD_GROUPS):
        yg = jnp.concatenate(gated[g * tiles_per_group:(g + 1) * tiles_per_group], axis=1)
        ms = jnp.mean(yg * yg, axis=-1, keepdims=True)
        o_ref[:, g * gw:(g + 1) * gw] = ((yg * lax.rsqrt(ms + EPS)) * nw_ref[:, g * gw:(g + 1) * gw]).astype(o_ref.dtype)


def _ssd(xbc, z, dt, conv_w, conv_b, dt_bias, a_log, d_skip, ssd_norm_w):
    nc = SEQ // CHUNK
    L = CHUNK
    dt_t = dt[:, 0:SSD_HEADS].T
    pad_row = lambda v: jnp.pad(v.astype(F32), (0, LANES - SSD_HEADS)).reshape(1, LANES)
    col8 = lambda v: v.astype(F32).reshape(SSD_HEADS, 1)
    idx = jnp.arange(L)
    tril = (idx[None, :] <= idx[:, None]).astype(BF16)
    triu = (idx[:, None] <= idx[None, :]).astype(BF16)
    dskip = jnp.repeat(d_skip.astype(F32), SSD_HEAD_DIM).reshape(1, SSD_WIDTH)
    const = lambda shape: pl.BlockSpec(shape, lambda b, c: (0, 0))
    tok = lambda width: pl.BlockSpec((L, width), lambda b, c: (b * nc + c, 0))
    return pl.pallas_call(
        _ssd_kernel,
        grid=(BATCH, nc),
        in_specs=[tok(XBC_WIDTH), tok(SSD_WIDTH), tok(LANES),
                  pl.BlockSpec((SSD_HEADS, L), lambda b, c: (0, b * nc + c)),
                  const((CONV_K, XBC_WIDTH)), const((1, XBC_WIDTH)),
                  const((1, LANES)), const((SSD_HEADS, 1)), const((1, LANES)), const((SSD_HEADS, 1)),
                  const((1, SSD_WIDTH)), const((1, SSD_WIDTH)), const((L, L)), const((L, L))],
        out_specs=tok(SSD_WIDTH),
        out_shape=jax.ShapeDtypeStruct((TOKENS, SSD_WIDTH), BF16),
        scratch_shapes=[pltpu.VMEM((8 + L, XBC_WIDTH), F32),
                        pltpu.VMEM((SSD_STATE, SSD_WIDTH), F32)],
        compiler_params=_cparams(("arbitrary", "arbitrary")),
        name="ssd",
    )(xbc, z, dt, dt_t, conv_w.astype(F32), conv_b.astype(F32).reshape(1, XBC_WIDTH),
      pad_row(dt_bias), col8(dt_bias), pad_row(a_log), col8(a_log), dskip,
      ssd_norm_w.astype(F32).reshape(1, SSD_WIDTH), tril, triu)


OUT_TM = 256
ROUTE_W = 8
ROUTER_COLS = N_GROUPS + N_EXPERTS
RUN_ALIGN = 16
RUN_SHIFT = 4
LOCAL_ROWS = 1024
assert RUN_ALIGN == 1 << RUN_SHIFT and LOCAL_ROWS >= TOP_K * OUT_TM + N_EXPERTS * (RUN_ALIGN - 1)


def _lane_pick(values, lane, index):
    return jnp.sum(jnp.where(lane == index, values, 0.0), axis=-1, keepdims=True)


def _first_argmax(vals, lane):
    m = jnp.max(vals, axis=-1, keepdims=True)
    idx = jnp.min(jnp.where(vals == m, lane, float(LANES)), axis=-1, keepdims=True)
    return m, idx


def _out_router_kernel(att_ref, y_ref, x_ref, g1_ref, wo_ref, nw_ref, sc_ref, sh_ref, wr_ref, br_ref,
                       ltri_ref, sut_ref, x1_ref, h2_ref, route_ref, routet_ref, tcnt_ref,
                       wr_split_ref, logits_ref):
    i = pl.program_id(0)

    @pl.when(i == 0)
    def _():
        hi, lo = _split_bf16(wr_ref[...])
        wr_split_ref[:, 0:LANES] = hi
        wr_split_ref[:, LANES:2 * LANES] = lo
        logits_ref[...] = jnp.zeros_like(logits_ref)

    logits = logits_ref[...]

    mixer = (jnp.dot(att_ref[...], wo_ref[0:ATT_WIDTH, :], preferred_element_type=F32)
             + jnp.dot(y_ref[...], wo_ref[ATT_WIDTH:ATT_WIDTH + SSD_WIDTH, :], preferred_element_type=F32))
    x1 = x_ref[...] + g1_ref[0] * mixer
    x1_ref[...] = x1
    yn = x1 * lax.rsqrt(jnp.mean(x1 * x1, axis=-1, keepdims=True) + EPS)
    h2 = (yn * nw_ref[...]) * (1.0 + sc_ref[0]) + sh_ref[0]
    h2_ref[...] = h2.astype(BF16)

    h_hi, h_lo = _split_bf16(h2)
    both = jnp.dot(h_hi, wr_split_ref[...], preferred_element_type=F32)
    logits_ref[...] = (both[:, 0:LANES] + both[:, LANES:2 * LANES]
                       + jnp.dot(h_lo, wr_split_ref[:, 0:LANES], preferred_element_type=F32)) + br_ref[...]

    tm = logits.shape[0]
    lane = lax.broadcasted_iota(I32, (tm, LANES), 1).astype(F32)

    gl = jnp.where(lane < N_GROUPS, logits, NEG_BIG)
    gmax, gidx = _first_argmax(gl, lane)
    g_p = 1.0 / jnp.sum(jnp.exp(gl - gmax), axis=-1, keepdims=True)

    lo_lane = N_GROUPS + EXPERTS_PER_GROUP * gidx
    el = jnp.where((lane >= lo_lane) & (lane < lo_lane + EXPERTS_PER_GROUP), logits, NEG_BIG)
    m1, i1 = _first_argmax(el, lane)
    m2, i2 = _first_argmax(jnp.where(lane == i1, NEG_BIG, el), lane)
    r = jnp.exp(m2 - m1)
    p1 = 1.0 / (1.0 + r)
    p2 = r / (1.0 + r)
    e0 = i1 - N_GROUPS
    e1 = i2 - N_GROUPS

    onehot = ((lane == e0) | (lane == e1)).astype(F32)
    tile_cnt = jnp.sum(onehot, axis=0, keepdims=True)
    run_len = jnp.floor((tile_cnt + (RUN_ALIGN - 1)) * (1.0 / RUN_ALIGN)) * RUN_ALIGN
    run_start = jnp.dot(jnp.broadcast_to(run_len, (8, LANES)).astype(BF16), sut_ref[...],
                        preferred_element_type=F32)[0:1, :]
    before = jnp.dot(ltri_ref[...], onehot.astype(BF16), preferred_element_type=F32) + run_start
    slot0 = _lane_pick(before, lane, e0)
    slot1 = _lane_pick(before, lane, e1)
    tcnt_ref[0] = tile_cnt

    rec = jnp.zeros((tm, LANES), F32)
    for k, v in enumerate([slot0, slot1, g_p * p1, g_p * p2, e0, e1]):
        rec = jnp.where(lane == k, v, rec)
    route_ref[...] = rec[:, 0:ROUTE_W]
    routet_ref[...] = rec.T[0:ROUTE_W, :]


def _out_router(att, y, x2d, mod3, w_out_b, norm_w, w_router, b_router):
    tm = OUT_TM
    n_steps = TOKENS // tm
    steps_per_batch = SEQ // tm
    idx = jnp.arange(tm)
    ltri = (idx[None, :] < idx[:, None]).astype(BF16)
    lidx = jnp.arange(LANES)
    sut = (lidx[:, None] < lidx[None, :]).astype(BF16)
    const = lambda shape: pl.BlockSpec(shape, lambda i: (0, 0))
    cur = lambda i: jnp.minimum(i, n_steps - 1)
    prev = lambda i: jnp.maximum(i - 1, 0)
    tok = lambda width: pl.BlockSpec((tm, width), lambda i: (cur(i), 0))
    modspec = lambda k: pl.BlockSpec((1, 1, D_MODEL), lambda i: ((cur(i) // steps_per_batch) * 6 + k, 0, 0))
    return pl.pallas_call(
        _out_router_kernel,
        grid=(n_steps + 1,),
        in_specs=[tok(ATT_WIDTH), tok(SSD_WIDTH), tok(D_MODEL), modspec(2),
                  const((D_MODEL, D_MODEL)), const((1, D_MODEL)), modspec(4), modspec(3),
                  const((D_MODEL, LANES)), const((1, LANES)), const((tm, tm)), const((LANES, LANES))],
        out_specs=[tok(D_MODEL), tok(D_MODEL),
                   pl.BlockSpec((tm, ROUTE_W), lambda i: (prev(i), 0)),
                   pl.BlockSpec((ROUTE_W, tm), lambda i: (prev(i), 0)),
                   pl.BlockSpec((1, 1, LANES), lambda i: (prev(i), 0, 0))],
        out_shape=[jax.ShapeDtypeStruct((TOKENS, D_MODEL), F32),
                   jax.ShapeDtypeStruct((TOKENS, D_MODEL), BF16),
                   jax.ShapeDtypeStruct((TOKENS, ROUTE_W), F32),
                   jax.ShapeDtypeStruct((n_steps * ROUTE_W, tm), F32),
                   jax.ShapeDtypeStruct((n_steps, 1, LANES), F32)],
        scratch_shapes=[pltpu.VMEM((D_MODEL, 2 * LANES), BF16), pltpu.VMEM((tm, LANES), F32)],
        compiler_params=_cparams(("arbitrary",)),
        name="out_router",
    )(att, y, x2d, mod3, w_out_b, norm_w.reshape(1, D_MODEL), mod3, mod3, w_router, b_router, ltri, sut)


MOE_TM = 512
ZERO_ROWS = 256
N_TOKEN_TILES = TOKENS // OUT_TM
MAX_PIECES = LOCAL_ROWS // RUN_ALIGN
MAX_SORTED_ROWS = TOKENS * TOP_K + N_TOKEN_TILES * N_EXPERTS * (RUN_ALIGN - 1)
N_TILES = MAX_SORTED_ROWS // MOE_TM + N_EXPERTS
N_ROWS = N_TILES * MOE_TM
assert MOE_TM % ZERO_ROWS == 0


def _for_pieces(count, action):
    def body(j, carry):
        action(j)
        return carry

    lax.fori_loop(0, count, body, 0)


def _dispatch_kernel(seg_end_ref, used_end_ref, np_ref, dst_ref, routet_ref, h2_ref, xs_ref,
                     sbuf_ref, zero_ref, sems, zsem):
    i = pl.program_id(0)
    last = pl.num_programs(0) - 1
    buf = lax.rem(i, 2)

    def zero_fills(action):
        def tail_copy(row):
            return pltpu.make_async_copy(zero_ref.at[pl.ds(0, RUN_ALIGN)],
                                         xs_ref.at[pl.ds(pl.multiple_of(row, RUN_ALIGN), RUN_ALIGN)], zsem)

        def block_copy(block):
            start = pl.multiple_of(block * ZERO_ROWS, ZERO_ROWS)
            return pltpu.make_async_copy(zero_ref, xs_ref.at[pl.ds(start, ZERO_ROWS)], zsem)

        def tails(e, carry):
            def body(r, c):
                action(tail_copy(r * RUN_ALIGN))
                return c

            lax.fori_loop(used_end_ref[e] // RUN_ALIGN, seg_end_ref[e] // RUN_ALIGN, body, 0)
            return carry

        def blocks(block, carry):
            action(block_copy(block))
            return carry

        lax.fori_loop(0, N_EXPERTS, tails, 0)
        lax.fori_loop(seg_end_ref[N_EXPERTS - 1] // ZERO_ROWS, N_ROWS // ZERO_ROWS, blocks, 0)

    @pl.when(i == 0)
    def _():
        zero_ref[...] = jnp.zeros_like(zero_ref)
        zero_fills(lambda cp: cp.start())

    slot = lax.broadcasted_iota(I32, (LOCAL_ROWS, OUT_TM), 0).astype(F32)
    perm = jnp.where((slot == routet_ref[0:1, :]) | (slot == routet_ref[1:2, :]), 1.0, 0.0).astype(BF16)
    sbuf_ref[buf] = jnp.dot(perm, h2_ref[...], preferred_element_type=F32).astype(BF16)

    def piece(b, local_piece, sorted_row):
        local = pl.multiple_of(local_piece * RUN_ALIGN, RUN_ALIGN)
        return pltpu.make_async_copy(sbuf_ref.at[b, pl.ds(local, RUN_ALIGN)],
                                     xs_ref.at[pl.ds(pl.multiple_of(sorted_row, RUN_ALIGN), RUN_ALIGN)],
                                     sems.at[b])

    _for_pieces(np_ref[i], lambda j: piece(buf, j, dst_ref[0, 0, j]).start())

    @pl.when(i > 0)
    def _():
        _for_pieces(np_ref[jnp.maximum(i - 1, 0)], lambda j: piece(1 - buf, j, 0).wait())

    @pl.when(i == last)
    def _():
        _for_pieces(np_ref[i], lambda j: piece(buf, j, 0).wait())
        zero_fills(lambda cp: cp.wait())


def _piece_spec(index_map):
    return pl.BlockSpec((1, 1, MAX_PIECES), index_map, memory_space=pltpu.SMEM)


def _dispatch(seg_end, counts, n_pieces, piece_dst, route_t, h2):
    grid_spec = pltpu.PrefetchScalarGridSpec(
        num_scalar_prefetch=3,
        grid=(N_TOKEN_TILES,),
        in_specs=[_piece_spec(lambda i, se, cn, npc: (i, 0, 0)),
                  pl.BlockSpec((ROUTE_W, OUT_TM), lambda i, se, cn, npc: (i, 0)),
                  pl.BlockSpec((OUT_TM, D_MODEL), lambda i, se, cn, npc: (i, 0))],
        out_specs=pl.BlockSpec(memory_space=pl.ANY),
        scratch_shapes=[pltpu.VMEM((2, LOCAL_ROWS, D_MODEL), BF16),
                        pltpu.VMEM((ZERO_ROWS, D_MODEL), BF16),
                        pltpu.SemaphoreType.DMA((2,)), pltpu.SemaphoreType.DMA],
    )
    return pl.pallas_call(
        _dispatch_kernel,
        grid_spec=grid_spec,
        out_shape=jax.ShapeDtypeStruct((N_ROWS, D_MODEL), BF16),
        compiler_params=_cparams(("arbitrary",)),
        name="dispatch",
    )(seg_end, counts, n_pieces, piece_dst, route_t, h2)


def _experts_kernel(te_ref, seg_ref, nxt_ref, nu_ref, xs_ref, wg_hbm, wu_hbm, wd_hbm, ys_ref,
                    wg_buf, wu_buf, wd_buf, wgu_b_ref, wd_b_ref, wsem):
    i = pl.program_id(0)
    used = i < nu_ref[0]
    slot = lax.rem(seg_ref[i], 2)

    def weight_copies(expert, s):
        return [pltpu.make_async_copy(wg_hbm.at[expert], wg_buf.at[s], wsem.at[s]),
                pltpu.make_async_copy(wu_hbm.at[expert], wu_buf.at[s], wsem.at[s]),
                pltpu.make_async_copy(wd_hbm.at[expert], wd_buf.at[s], wsem.at[s])]

    @pl.when(i == 0)
    def _():
        for cp in weight_copies(te_ref[0], 0):
            cp.start()

    @pl.when(used & ((i == 0) | (te_ref[i] != te_ref[jnp.maximum(i - 1, 0)])))
    def _():
        for cp in weight_copies(te_ref[i], slot):
            cp.wait()

        @pl.when(nxt_ref[i] >= 0)
        def _():
            for cp in weight_copies(nxt_ref[i], 1 - slot):
                cp.start()

        wgu_b_ref[:, 0:D_EXPERT] = wg_buf[slot].astype(BF16)
        wgu_b_ref[:, D_EXPERT:2 * D_EXPERT] = wu_buf[slot].astype(BF16)
        wd_b_ref[...] = wd_buf[slot].astype(BF16)

    @pl.when(used)
    def _():
        h = jnp.dot(xs_ref[...], wgu_b_ref[...], preferred_element_type=F32)
        act = (_silu(h[:, 0:D_EXPERT]) * h[:, D_EXPERT:2 * D_EXPERT]).astype(BF16)
        ys_ref[...] = jnp.dot(act, wd_b_ref[...], preferred_element_type=F32).astype(BF16)


def _experts(tile_expert, tile_segment, next_expert, n_used, xs, w_gate, w_up, w_down):
    row_tile = lambda i, te, sg, nx, nu: (jnp.minimum(i, nu[0] - 1), 0)
    n_prefetch = 4
    grid_spec = pltpu.PrefetchScalarGridSpec(
        num_scalar_prefetch=n_prefetch,
        grid=(N_TILES,),
        in_specs=[pl.BlockSpec((MOE_TM, D_MODEL), row_tile),
                  pl.BlockSpec(memory_space=pl.ANY), pl.BlockSpec(memory_space=pl.ANY),
                  pl.BlockSpec(memory_space=pl.ANY)],
        out_specs=pl.BlockSpec((MOE_TM, D_MODEL), row_tile),
        scratch_shapes=[pltpu.VMEM((2, D_MODEL, D_EXPERT), F32), pltpu.VMEM((2, D_MODEL, D_EXPERT), F32),
                        pltpu.VMEM((2, D_EXPERT, D_MODEL), F32),
                        pltpu.VMEM((D_MODEL, 2 * D_EXPERT), BF16), pltpu.VMEM((D_EXPERT, D_MODEL), BF16),
                        pltpu.SemaphoreType.DMA((2,))],
    )
    return pl.pallas_call(
        _experts_kernel,
        grid_spec=grid_spec,
        out_shape=jax.ShapeDtypeStruct((N_ROWS, D_MODEL), BF16),
        input_output_aliases={n_prefetch: 0},
        compiler_params=_cparams(("arbitrary",)),
        name="experts",
    )(tile_expert, tile_segment, next_expert, n_used, xs, w_gate, w_up, w_down)


def _combine_kernel(np_ref, dst_ref, dst_next_ref, route_ref, x1_ref, g2_ref, ys_ref, o_ref, gbuf_ref, sems):
    i = pl.program_id(0)
    last = pl.num_programs(0) - 1
    buf = lax.rem(i, 2)

    def piece(b, local_piece, sorted_row):
        local = pl.multiple_of(local_piece * RUN_ALIGN, RUN_ALIGN)
        return pltpu.make_async_copy(ys_ref.at[pl.ds(pl.multiple_of(sorted_row, RUN_ALIGN), RUN_ALIGN)],
                                     gbuf_ref.at[b, pl.ds(local, RUN_ALIGN)], sems.at[b])

    @pl.when(i == 0)
    def _():
        gbuf_ref[...] = jnp.zeros_like(gbuf_ref)
        _for_pieces(np_ref[0], lambda j: piece(0, j, dst_ref[0, 0, j]).start())

    @pl.when(i < last)
    def _():
        _for_pieces(np_ref[jnp.minimum(i + 1, last)], lambda j: piece(1 - buf, j, dst_next_ref[0, 0, j]).start())

    rec = route_ref[...]
    slot = lax.broadcasted_iota(I32, (OUT_TM, LOCAL_ROWS), 1).astype(F32)
    weights = (jnp.where(slot == rec[:, 0:1], rec[:, 2:3], 0.0)
               + jnp.where(slot == rec[:, 1:2], rec[:, 3:4], 0.0)).astype(BF16)
    _for_pieces(np_ref[i], lambda j: piece(buf, j, 0).wait())
    moe = jnp.dot(weights, gbuf_ref[buf], preferred_element_type=F32)
    o_ref[...] = x1_ref[...] + g2_ref[0] * moe


def _combine(n_pieces, piece_dst, route, x1, mod3, ys):
    tm = OUT_TM
    steps_per_batch = SEQ // tm
    grid_spec = pltpu.PrefetchScalarGridSpec(
        num_scalar_prefetch=1,
        grid=(N_TOKEN_TILES,),
        in_specs=[_piece_spec(lambda i, npc: (i, 0, 0)),
                  _piece_spec(lambda i, npc: (jnp.minimum(i + 1, N_TOKEN_TILES - 1), 0, 0)),
                  pl.BlockSpec((tm, ROUTE_W), lambda i, npc: (i, 0)),
                  pl.BlockSpec((tm, D_MODEL), lambda i, npc: (i, 0)),
                  pl.BlockSpec((1, 1, D_MODEL), lambda i, npc: ((i // steps_per_batch) * 6 + 5, 0, 0)),
                  pl.BlockSpec(memory_space=pl.ANY)],
        out_specs=pl.BlockSpec((tm, D_MODEL), lambda i, npc: (i, 0)),
        scratch_shapes=[pltpu.VMEM((2, LOCAL_ROWS, D_MODEL), BF16), pltpu.SemaphoreType.DMA((2,))],
    )
    return pl.pallas_call(
        _combine_kernel,
        grid_spec=grid_spec,
        out_shape=jax.ShapeDtypeStruct((TOKENS, D_MODEL), F32),
        compiler_params=_cparams(("arbitrary",)),
        name="combine",
    )(n_pieces, piece_dst, piece_dst, route, x1, mod3, ys)


def kernel(x, c, positions, norm1_w, norm2_w, w_ada, b_ada, w_in, conv_w, conv_b, dt_bias, a_log,
           d_skip, ssd_norm_w, q_norm_w, k_norm_w, sinks, w_out, w_group, b_group, w_expert, b_expert,
           w_gate, w_up, w_down):
    assert x.shape == (BATCH, SEQ, D_MODEL) and w_in.shape == (D_MODEL, IN_WIDTH)
    x2d = x.reshape(TOKENS, D_MODEL)
    mod = _ada_mod(c, w_ada, b_ada)
    mod3 = mod.reshape(BATCH * 6, 1, D_MODEL)

    w_in_pad = jnp.pad(w_in, ((0, 0), (0, IN_PAD - IN_WIDTH))).astype(BF16)
    qkv, z, xbc, dt = _in_proj(x2d, norm1_w, mod3, w_in_pad)
    att = _attention(qkv, positions, q_norm_w, k_norm_w, sinks)
    y = _ssd(xbc, z, dt, conv_w, conv_b, dt_bias, a_log, d_skip, ssd_norm_w)

    w_router = jnp.pad(jnp.concatenate([w_group, w_expert], axis=1).astype(F32),
                       ((0, 0), (0, LANES - ROUTER_COLS)))
    b_router = jnp.pad(jnp.concatenate([b_group, b_expert]).astype(F32),
                       (0, LANES - ROUTER_COLS)).reshape(1, LANES)
    x1, h2, route, route_t, tcnt = _out_router(att, y, x2d, mod3, w_out.astype(BF16), norm2_w,
                                                w_router, b_router)

    tc = tcnt[:, 0, 0:N_EXPERTS].astype(I32)
    run_rows = ((tc + RUN_ALIGN - 1) // RUN_ALIGN) * RUN_ALIGN
    counts = jnp.sum(run_rows, axis=0)
    padded = ((counts + MOE_TM - 1) // MOE_TM) * MOE_TM
    seg_end = jnp.cumsum(padded)
    seg_start = seg_end - padded
    run_dst = seg_start[None, :] + jnp.cumsum(run_rows, axis=0) - run_rows
    n_used = (seg_end[-1] // MOE_TM).reshape(1)
    last_row = jnp.minimum(jnp.arange(N_TILES, dtype=I32) * MOE_TM, seg_end[-1] - 1)
    tile_expert = jnp.sum((seg_end[None, :] <= last_row[:, None]).astype(I32), axis=1)

    run_pieces = run_rows // RUN_ALIGN
    piece_end = jnp.cumsum(run_pieces, axis=1)
    n_pieces = piece_end[:, -1]
    j = jnp.arange(MAX_PIECES, dtype=I32)
    piece_expert = jnp.sum((piece_end[:, None, :] <= j[None, :, None]).astype(I32), axis=2)
    in_run = (piece_expert[:, :, None] == jnp.arange(N_EXPERTS, dtype=I32)[None, None, :]).astype(I32)
    run_base = run_dst - RUN_ALIGN * (piece_end - run_pieces)
    piece_dst = jnp.sum(in_run * run_base[:, None, :], axis=2) + RUN_ALIGN * j[None, :]
    piece_dst = piece_dst.astype(I32).reshape(N_TOKEN_TILES, 1, MAX_PIECES)

    experts = jnp.arange(N_EXPERTS, dtype=I32)
    nonempty = padded > 0
    seg_rank = jnp.cumsum(nonempty.astype(I32)) - 1
    later = nonempty[None, :] & (experts[None, :] > experts[:, None])
    next_of = jnp.min(jnp.where(later, experts[None, :], N_EXPERTS), axis=1)
    next_of = jnp.where(next_of == N_EXPERTS, -1, next_of)
    tile_is = (tile_expert[:, None] == experts[None, :]).astype(I32)
    tile_segment = jnp.sum(tile_is * seg_rank[None, :], axis=1)
    next_expert = jnp.sum(tile_is * next_of[None, :], axis=1)

    xs = _dispatch(seg_end.astype(I32), (seg_start + counts).astype(I32), n_pieces.astype(I32), piece_dst,
                   route_t, h2)
    ys = _experts(tile_expert, tile_segment.astype(I32), next_expert.astype(I32), n_used.astype(I32),
                  xs, w_gate, w_up, w_down)
    out = _combine(n_pieces.astype(I32), piece_dst, route, x1, mod3, ys)
    return out.reshape(BATCH, SEQ, D_MODEL)
```

```python
import jax
import jax.numpy as jnp
from jax import lax
from jax.experimental import pallas as pl
from jax.experimental.pallas import tpu as pltpu

F32 = jnp.float32
BF16 = jnp.bfloat16
I32 = jnp.int32

D_MODEL = 1024
BATCH = 2
SEQ = 8192
TOKENS = BATCH * SEQ
ATT_HEADS = 8
ATT_KV_HEADS = 2
HEAD_DIM = 64
ATT_WIDTH = ATT_HEADS * HEAD_DIM
KV_WIDTH = ATT_KV_HEADS * HEAD_DIM
ATT_BLOCK = 128
ROPE_DIM = HEAD_DIM // 4
ROPE_THETA = 500000.0
SSD_HEADS = 8
SSD_HEAD_DIM = 64
SSD_WIDTH = SSD_HEADS * SSD_HEAD_DIM
SSD_GROUPS = 2
SSD_STATE = 128
CONV_K = 4
CHUNK = 128
XBC_WIDTH = SSD_WIDTH + 2 * SSD_GROUPS * SSD_STATE
IN_WIDTH = ATT_WIDTH + 2 * KV_WIDTH + SSD_WIDTH + XBC_WIDTH + SSD_HEADS
N_GROUPS = 4
EXPERTS_PER_GROUP = 8
N_EXPERTS = N_GROUPS * EXPERTS_PER_GROUP
TOP_K = 2
D_EXPERT = 256
EPS = 1e-6

LANES = 128
QKV_WIDTH = ATT_WIDTH + 2 * KV_WIDTH
IN_PAD = QKV_WIDTH + SSD_WIDTH + XBC_WIDTH + LANES
NEG_BIG = -1e30

VMEM_LIMIT = 48 * 1024 * 1024


def _cparams(sem):
    return pltpu.CompilerParams(dimension_semantics=sem, vmem_limit_bytes=VMEM_LIMIT)


def _split_bf16(x):
    hi = x.astype(BF16)
    lo = (x - hi.astype(F32)).astype(BF16)
    return hi, lo


ADA_TN = 768


def _ada_kernel(ct_ref, w_ref, b_ref, o_ref):
    ct = ct_ref[...]
    s = ct * jax.nn.sigmoid(ct)
    w = w_ref[...]
    rows = [jnp.sum(s[:, b:b + 1] * w, axis=0, keepdims=True) for b in range(BATCH)]
    o_ref[...] = jnp.concatenate(rows, axis=0) + b_ref[...]


def _ada_mod(c, w_ada, b_ada):
    n = w_ada.shape[1]
    return pl.pallas_call(
        _ada_kernel,
        grid=(n // ADA_TN,),
        in_specs=[pl.BlockSpec((D_MODEL, BATCH), lambda j: (0, 0)),
                  pl.BlockSpec((D_MODEL, ADA_TN), lambda j: (0, j)),
                  pl.BlockSpec((1, ADA_TN), lambda j: (0, j))],
        out_specs=pl.BlockSpec((BATCH, ADA_TN), lambda j: (0, j)),
        out_shape=jax.ShapeDtypeStruct((BATCH, n), F32),
        compiler_params=_cparams(("arbitrary",)),
        name="ada_mod",
    )(c.T, w_ada, b_ada.reshape(1, n))


INPROJ_TM = 512
_INPROJ_CHUNK = 256


def _inproj_kernel(x_ref, nw_ref, sc_ref, sh_ref, wf_ref, wdt_ref, qkv_ref, z_ref, xbc_ref, dt_ref, dtt_ref,
                   w_ref):
    @pl.when(pl.program_id(0) == 0)
    def _():
        for c0 in range(0, IN_PAD - LANES, _INPROJ_CHUNK):
            w_ref[:, c0:c0 + _INPROJ_CHUNK] = wf_ref[:, c0:c0 + _INPROJ_CHUNK].astype(BF16)
        w_ref[:, IN_PAD - LANES:IN_PAD] = wdt_ref[...].astype(BF16)

    x = x_ref[...]
    y = x * lax.rsqrt(jnp.mean(x * x, axis=-1, keepdims=True) + EPS)
    h = (y * nw_ref[...]) * (1.0 + sc_ref[0]) + sh_ref[0]
    hb = h.astype(BF16)

    def proj(c0, c1):
        return jnp.dot(hb, w_ref[:, c0:c1], preferred_element_type=F32)

    for c0 in range(0, QKV_WIDTH, _INPROJ_CHUNK):
        qkv_ref[:, c0:c0 + _INPROJ_CHUNK] = proj(c0, c0 + _INPROJ_CHUNK).astype(BF16)
    base = QKV_WIDTH
    for c0 in range(0, SSD_WIDTH, _INPROJ_CHUNK):
        z_ref[:, c0:c0 + _INPROJ_CHUNK] = proj(base + c0, base + c0 + _INPROJ_CHUNK).astype(BF16)
    base += SSD_WIDTH
    for c0 in range(0, XBC_WIDTH, _INPROJ_CHUNK):
        xbc_ref[:, c0:c0 + _INPROJ_CHUNK] = proj(base + c0, base + c0 + _INPROJ_CHUNK).astype(BF16)
    base += XBC_WIDTH
    dt = proj(base, base + LANES)
    dt_ref[...] = dt
    dtt_ref[...] = dt.T[0:SSD_HEADS, :]


def _in_proj(x2d, norm_w, mod3, w_in):
    tm = INPROJ_TM
    steps_per_batch = SEQ // tm
    w_dt = jnp.pad(w_in[:, IN_WIDTH - SSD_HEADS:IN_WIDTH].astype(F32), ((0, 0), (0, LANES - SSD_HEADS)))
    return pl.pallas_call(
        _inproj_kernel,
        grid=(TOKENS // tm,),
        in_specs=[pl.BlockSpec((tm, D_MODEL), lambda i: (i, 0)),
                  pl.BlockSpec((1, D_MODEL), lambda i: (0, 0)),
                  pl.BlockSpec((1, 1, D_MODEL), lambda i: ((i // steps_per_batch) * 6 + 1, 0, 0)),
                  pl.BlockSpec((1, 1, D_MODEL), lambda i: ((i // steps_per_batch) * 6 + 0, 0, 0)),
                  pl.BlockSpec((D_MODEL, IN_WIDTH), lambda i: (0, 0), pipeline_mode=pl.Buffered(1)),
                  pl.BlockSpec((D_MODEL, LANES), lambda i: (0, 0))],
        out_specs=[pl.BlockSpec((tm, QKV_WIDTH), lambda i: (i, 0)),
                   pl.BlockSpec((tm, SSD_WIDTH), lambda i: (i, 0)),
                   pl.BlockSpec((tm, XBC_WIDTH), lambda i: (i, 0)),
                   pl.BlockSpec((tm, LANES), lambda i: (i, 0)),
                   pl.BlockSpec((SSD_HEADS, tm), lambda i: (0, i))],
        out_shape=[jax.ShapeDtypeStruct((TOKENS, QKV_WIDTH), BF16),
                   jax.ShapeDtypeStruct((TOKENS, SSD_WIDTH), BF16),
                   jax.ShapeDtypeStruct((TOKENS, XBC_WIDTH), BF16),
                   jax.ShapeDtypeStruct((TOKENS, LANES), F32),
                   jax.ShapeDtypeStruct((SSD_HEADS, TOKENS), F32)],
        scratch_shapes=[pltpu.VMEM((D_MODEL, IN_PAD), BF16)],
        compiler_params=_cparams(("arbitrary",)),
        name="in_proj",
    )(x2d, norm_w.reshape(1, D_MODEL), mod3, mod3, w_in.astype(F32), w_dt)


ATT_SUB = 2


ROPE_TM = 2048
_ROPE_HALF = ROPE_DIM // 2
_TOK_PER_ROW = LANES // _ROPE_HALF


def _exact_dot(x, onehot_b):
    hi, lo = _split_bf16(x)
    return (jnp.dot(hi, onehot_b, preferred_element_type=F32)
            + jnp.dot(lo, onehot_b, preferred_element_type=F32))


def _rope_kernel(pos_ref, freq_ref, sel_ref, own_ref, gcos_ref, gs1_ref, gs2_ref, ident_ref,
                 cos_ref, s1_ref, s2_ref):
    ang = pos_ref[...].astype(F32) * freq_ref[...]
    cos_p, sin_p = jnp.cos(ang), jnp.sin(ang)
    hi_c, lo_c = _split_bf16(cos_p)
    hi_s, lo_s = _split_bf16(sin_p)
    sel = sel_ref[...]
    rows_c = jnp.dot(sel, hi_c, preferred_element_type=F32) + jnp.dot(sel, lo_c, preferred_element_type=F32)
    rows_s = jnp.dot(sel, hi_s, preferred_element_type=F32) + jnp.dot(sel, lo_s, preferred_element_type=F32)
    own = own_ref[...]
    cos_ref[...] = _exact_dot(rows_c * own, gcos_ref[...]) + ident_ref[...]
    s1_ref[...] = _exact_dot(rows_s * own, gs1_ref[...])
    s2_ref[...] = _exact_dot(rows_s * own, gs2_ref[...])


def _rope_tables(positions):
    half, per_row = _ROPE_HALF, _TOK_PER_ROW
    rows = ROPE_TM // per_row
    pos_rep = jnp.repeat(positions.reshape(TOKENS).astype(I32), half).reshape(TOKENS // per_row, LANES)
    inv_freq = jnp.power(ROPE_THETA, -jnp.arange(half, dtype=F32) * 2.0 / ROPE_DIM)
    freq = jnp.tile(inv_freq, per_row).reshape(1, LANES)
    tok = jnp.arange(ROPE_TM)
    lane = jnp.arange(LANES)
    sel = (tok[:, None] // per_row == jnp.arange(rows)[None, :]).astype(BF16)
    own = (lane[None, :] // half == tok[:, None] % per_row).astype(F32)
    d = lane % HEAD_DIM
    src_f = lane % half
    hits = lambda lo, hi: ((src_f[:, None] == d[None, :] % half) & (d[None, :] >= lo) & (d[None, :] < hi))
    gcos = hits(0, ROPE_DIM).astype(BF16)
    gs1 = -hits(0, half).astype(BF16)
    gs2 = hits(half, ROPE_DIM).astype(BF16)
    ident = (d >= ROPE_DIM).astype(F32).reshape(1, LANES)
    const = lambda shape: pl.BlockSpec(shape, lambda i: (0, 0))
    out_spec = pl.BlockSpec((ROPE_TM, LANES), lambda i: (i, 0))
    out = jax.ShapeDtypeStruct((TOKENS, LANES), F32)
    return pl.pallas_call(
        _rope_kernel,
        grid=(TOKENS // ROPE_TM,),
        in_specs=[pl.BlockSpec((rows, LANES), lambda i: (i, 0)), const((1, LANES)),
                  const((ROPE_TM, rows)), const((ROPE_TM, LANES)),
                  const((LANES, LANES)), const((LANES, LANES)), const((LANES, LANES)), const((1, LANES))],
        out_specs=[out_spec, out_spec, out_spec],
        out_shape=[out, out, out],
        compiler_params=_cparams(("arbitrary",)),
        name="rope_tables",
    )(pos_rep, freq, sel, own, gcos, gs1, gs2, ident)


def _seg_meansq(xf, ones128):
    rows, width = xf.shape
    nt = width // LANES
    parts = _split_bf16(xf * xf)
    stacked = jnp.concatenate([p[:, t * LANES:(t + 1) * LANES] for p in parts for t in range(nt)], axis=0)
    tot = jnp.dot(stacked, ones128, preferred_element_type=F32)
    tiles = [tot[t * rows:(t + 1) * rows] + tot[(nt + t) * rows:(nt + t + 1) * rows] for t in range(nt)]
    return jnp.concatenate(tiles, axis=1) * (1.0 / HEAD_DIM)


def _norm_rope(x_bf, w_row, ones_bd, cosf, s1, s2):
    xf = x_bf.astype(F32)
    width = xf.shape[1]
    xn = xf * lax.rsqrt(_seg_meansq(xf, ones_bd) + EPS) * w_row
    half = ROPE_DIM // 2
    up = pltpu.roll(xn, width - half, axis=1)
    down = pltpu.roll(xn, half, axis=1)
    return xn * cosf + up * s1 + down * s2


def _attn_kernel(sink_ref, q_ref, kv_ref, cos_ref, s1_ref, s2_ref, qw_ref, kw_ref,
                 ones_ref, o_ref, kprev_ref, vprev_ref):
    j = pl.program_id(1)
    blk = ATT_BLOCK

    @pl.when(j == 0)
    def _():
        kprev_ref[...] = jnp.zeros_like(kprev_ref)
        vprev_ref[...] = jnp.zeros_like(vprev_ref)

    cos1 = cos_ref[...]
    s1_1 = s1_ref[...]
    s2_1 = s2_ref[...]
    reps = ATT_WIDTH // LANES
    cosq = jnp.concatenate([cos1] * reps, axis=1)
    s1q = jnp.concatenate([s1_1] * reps, axis=1)
    s2q = jnp.concatenate([s2_1] * reps, axis=1)

    q = _norm_rope(q_ref[...], qw_ref[...], ones_ref[...], cosq, s1q, s2q)
    qf = q * (HEAD_DIM ** -0.5)
    kv = kv_ref[...]
    kn = _norm_rope(kv[:, 0:KV_WIDTH], kw_ref[...], ones_ref[...], cos1, s1_1, s2_1)
    vn = kv[:, KV_WIDTH:2 * KV_WIDTH].astype(F32)

    kall = jnp.concatenate([kprev_ref[...], kn], axis=0)
    vall = jnp.concatenate([vprev_ref[...], vn], axis=0)
    kprev_ref[...] = kn[(ATT_SUB - 1) * blk:ATT_SUB * blk]
    vprev_ref[...] = vn[(ATT_SUB - 1) * blk:ATT_SUB * blk]

    lo_all = lax.broadcasted_iota(I32, kall.shape, 1) < HEAD_DIM
    ones_all = jnp.ones(kall.shape, BF16)

    row = lax.broadcasted_iota(I32, (2 * blk, blk), 0)
    col = lax.broadcasted_iota(I32, (2 * blk, blk), 1)
    from_prev = col > (row & (blk - 1))
    second_tile = lax.broadcasted_iota(I32, (2 * blk, 1), 0) >= blk
    zero_p = jnp.zeros((2 * blk, blk), F32)

    k_par, v_par = [], []
    for g in range(ATT_KV_HEADS):
        keep = lo_all if g == 0 else ~lo_all
        k_own = jnp.where(keep, kall, 0.0)
        v_own = jnp.where(keep, vall, 0.0)
        k_oth = pltpu.roll(k_own, HEAD_DIM, axis=1)
        v_oth = pltpu.roll(v_own, HEAD_DIM, axis=1)
        k_lo, k_hi = (k_own, k_oth) if g == 0 else (k_oth, k_own)
        v_lo, v_hi = (v_own, v_oth) if g == 0 else (v_oth, v_own)
        k_par.append((k_lo.astype(BF16), k_hi.astype(BF16)))
        v_par.append((jnp.concatenate([v_lo.astype(BF16), ones_all], axis=1),
                      jnp.concatenate([v_hi.astype(BF16), ones_all], axis=1)))

    problems = [(g, sub) for g in range(ATT_KV_HEADS) for sub in range(ATT_SUB)]
    scores = []
    for g, sub in problems:
        r0, c0 = sub * blk, g * 2 * LANES
        qcat = jnp.concatenate([qf[r0:r0 + blk, c0:c0 + LANES],
                                qf[r0:r0 + blk, c0 + LANES:c0 + 2 * LANES]], axis=0).astype(BF16)
        kw = jnp.concatenate([k_par[g][0][r0:r0 + 2 * blk], k_par[g][1][r0:r0 + 2 * blk]], axis=0)
        scores.append(lax.dot_general(qcat, kw, (((1,), (1,)), ((), ())),
                                      preferred_element_type=F32))

    weights, rescale = [], []
    for (g, sub), s_all in zip(problems, scores):
        for par in range(2):
            s = s_all[:, par * 2 * blk:(par + 1) * 2 * blk]
            s_prev = s[:, 0:blk]
            if sub == 0:
                s_prev = s_prev + jnp.where(j > 0, 0.0, NEG_BIG)
            s = jnp.where(from_prev, s_prev, s[:, blk:2 * blk])
            h_first = ATT_HEADS // ATT_KV_HEADS * g + par
            sink = jnp.where(second_tile, sink_ref[h_first + 2], sink_ref[h_first])
            m = jnp.maximum(jnp.max(s, axis=-1, keepdims=True), sink)
            p = jnp.exp(s - m)
            weights.append(jnp.concatenate([jnp.where(from_prev, p, zero_p), jnp.where(from_prev, zero_p, p)],
                                           axis=1).astype(BF16))
            rescale.append(jnp.exp(sink - m))

    outs = []
    for idx, (g, sub) in enumerate(problems):
        for par in range(2):
            outs.append(jnp.dot(weights[2 * idx + par], v_par[g][par][sub * blk:(sub + 2) * blk],
                                preferred_element_type=F32))

    for idx, (g, sub) in enumerate(problems):
        r0, c0 = sub * blk, g * 2 * LANES
        pair = None
        for par in range(2):
            o = outs[2 * idx + par]
            part = o[:, 0:LANES] * (1.0 / (o[:, LANES:2 * LANES] + rescale[2 * idx + par]))
            pair = part if pair is None else pair + part
        o_ref[r0:r0 + blk, c0:c0 + LANES] = pair[0:blk].astype(BF16)
        o_ref[r0:r0 + blk, c0 + LANES:c0 + 2 * LANES] = pair[blk:2 * blk].astype(BF16)


def _attention(qkv, positions, q_norm_w, k_norm_w, sinks):
    cosf, s1, s2 = _rope_tables(positions)
    qw = jnp.tile(q_norm_w.astype(F32), ATT_HEADS).reshape(1, ATT_WIDTH)
    kw = jnp.tile(k_norm_w.astype(F32), ATT_KV_HEADS).reshape(1, KV_WIDTH)
    seg = jnp.arange(LANES) // HEAD_DIM
    ones128 = (seg[:, None] == seg[None, :]).astype(BF16)
    const = lambda shape: pl.BlockSpec(shape, lambda b, j, s: (0, 0))
    rows = ATT_SUB * ATT_BLOCK
    nb = SEQ // rows
    tok = lambda width, cb: pl.BlockSpec((rows, width), lambda b, j, s: (b * nb + j, cb))
    grid_spec = pltpu.PrefetchScalarGridSpec(
        num_scalar_prefetch=1,
        grid=(BATCH, nb),
        in_specs=[tok(ATT_WIDTH, 0), tok(2 * KV_WIDTH, 2), tok(LANES, 0), tok(LANES, 0), tok(LANES, 0),
                  const((1, ATT_WIDTH)), const((1, KV_WIDTH)), const((LANES, LANES))],
        out_specs=tok(ATT_WIDTH, 0),
        scratch_shapes=[pltpu.VMEM((ATT_BLOCK, KV_WIDTH), F32),
                        pltpu.VMEM((ATT_BLOCK, KV_WIDTH), F32)],
    )
    return pl.pallas_call(
        _attn_kernel,
        grid_spec=grid_spec,
        out_shape=jax.ShapeDtypeStruct((TOKENS, ATT_WIDTH), BF16),
        compiler_params=_cparams(("arbitrary", "arbitrary")),
        name="attention",
    )(sinks.astype(F32), qkv, qkv, cosf, s1, s2, qw, kw, ones128)


def _softplus(x):
    return jnp.maximum(x, 0.0) + jnp.log1p(jnp.exp(-jnp.abs(x)))


def _silu(x):
    return x * jax.nn.sigmoid(x)


def _ssd_kernel(xbc_ref, z_ref, dt_ref, dtt_ref, cw_ref, cb_ref, dtb_row_ref, dtb_col_ref,
                alog_row_ref, alog_col_ref, dskip_ref, nw_ref, tril_ref, triu_ref,
                o_ref, conv_ref, state_ref):
    c = pl.program_id(1)
    L = CHUNK
    tail = 8

    @pl.when(c == 0)
    def _():
        conv_ref[0:tail, :] = jnp.zeros((tail, XBC_WIDTH), F32)
        state_ref[...] = jnp.zeros_like(state_ref)

    xb = xbc_ref[...].astype(F32)
    conv_ref[tail:tail + L, :] = xb
    acc = cb_ref[...] + cw_ref[CONV_K - 1:CONV_K, :] * xb
    for k in range(CONV_K - 1):
        off = tail - (CONV_K - 1) + k
        acc = acc + cw_ref[k:k + 1, :] * conv_ref[off:off + L, :]
    conv_ref[0:tail, :] = xb[L - tail:L, :]
    u = _silu(acc)
    xs = u[:, 0:SSD_WIDTH]
    bmat = u[:, SSD_WIDTH:SSD_WIDTH + SSD_GROUPS * SSD_STATE]
    cmat = u[:, SSD_WIDTH + SSD_GROUPS * SSD_STATE:XBC_WIDTH]

    dt = _softplus(dt_ref[...] + dtb_row_ref[...])
    a = dt * (-jnp.exp(alog_row_ref[...]))
    a_hi, a_lo = _split_bf16(a)
    a_cum = (jnp.dot(tril_ref[...], a_hi, preferred_element_type=F32)
             + jnp.dot(tril_ref[...], a_lo, preferred_element_type=F32))
    dt_t = _softplus(dtt_ref[...] + dtb_col_ref[...])
    a_t = dt_t * (-jnp.exp(alog_col_ref[...]))
    at_hi, at_lo = _split_bf16(a_t)
    a_cum_t = (jnp.dot(at_hi, triu_ref[...], preferred_element_type=F32)
               + jnp.dot(at_lo, triu_ref[...], preferred_element_type=F32))
    a_end_t = a_cum_t[:, L - 1:L]
    wst_t = jnp.exp(a_end_t - a_cum_t) * dt_t
    cdec_t = jnp.exp(a_end_t)

    row = lax.broadcasted_iota(I32, (L, L), 0)
    col = lax.broadcasted_iota(I32, (L, L), 1)
    causal = col <= row
    lane = lax.broadcasted_iota(I32, (L, LANES), 1)
    lo_half = lane < SSD_HEAD_DIM

    xs_b = xs.astype(BF16)
    heads_per_group = SSD_HEADS // SSD_GROUPS
    gated = []
    for g in range(SSD_GROUPS):
        b_g = bmat[:, g * SSD_STATE:(g + 1) * SSD_STATE]
        c_g = cmat[:, g * SSD_STATE:(g + 1) * SSD_STATE]
        cb = lax.dot_general(c_g.astype(BF16), b_g.astype(BF16), (((1,), (1,)), ((), ())),
                             preferred_element_type=F32)
        b_gt = b_g.T
        for t in range(heads_per_group // 2):
            tile = g * (heads_per_group // 2) + t
            c0 = tile * LANES
            xs_tile = xs_b[:, c0:c0 + LANES]
            st_tile = state_ref[:, c0:c0 + LANES]
            st_b = st_tile.astype(BF16)
            y_tile = jnp.zeros((L, LANES), F32)
            new_tile = jnp.zeros((SSD_STATE, LANES), F32)
            for e in range(2):
                h = 2 * tile + e
                keep = lo_half if e == 0 else ~lo_half
                colb = jnp.broadcast_to(a_cum[:, h:h + 1], (L, L))
                rowb = a_cum_t[h:h + 1, :]
                decay = jnp.exp(jnp.where(causal, colb - rowb, NEG_BIG))
                w_in = (cb * decay) * dt_t[h:h + 1, :]
                w_off = c_g * jnp.exp(colb)
                lhs = jnp.concatenate([w_in, w_off], axis=1).astype(BF16)
                rhs = jnp.concatenate([jnp.where(keep, xs_tile, jnp.zeros_like(xs_tile)),
                                       jnp.where(keep, st_b, jnp.zeros_like(st_b))], axis=0)
                y_tile = y_tile + jnp.dot(lhs, rhs, preferred_element_type=F32)
                m_h = (b_gt * wst_t[h:h + 1, :]).astype(BF16)
                new_tile = new_tile + jnp.dot(m_h, jnp.where(keep, xs_tile, jnp.zeros_like(xs_tile)),
                                              preferred_element_type=F32)
            cd = jnp.where(lo_half[0:1, :], cdec_t[2 * tile:2 * tile + 1, :],
                           cdec_t[2 * tile + 1:2 * tile + 2, :])
            state_ref[:, c0:c0 + LANES] = st_tile * cd + new_tile
            y_full = y_tile + dskip_ref[:, c0:c0 + LANES] * xs[:, c0:c0 + LANES]
            gated.append(y_full * _silu(z_ref[:, c0:c0 + LANES].astype(F32)))

    gw = SSD_WIDTH // SSD_GROUPS
    tiles_per_group = gw // LANES
    for g in range(SSD_GROUPS):
        yg = jnp.concatenate(gated[g * tiles_per_group:(g + 1) * tiles_per_group], axis=1)
        ms = jnp.mean(yg * yg, axis=-1, keepdims=True)
        o_ref[:, g * gw:(g + 1) * gw] = ((yg * lax.rsqrt(ms + EPS)) * nw_ref[:, g * gw:(g + 1) * gw]).astype(o_ref.dtype)


def _ssd(xbc, z, dt, dt_t, conv_w, conv_b, dt_bias, a_log, d_skip, ssd_norm_w):
    nc = SEQ // CHUNK
    L = CHUNK
    pad_row = lambda v: jnp.pad(v.astype(F32), (0, LANES - SSD_HEADS)).reshape(1, LANES)
    col8 = lambda v: v.astype(F32).reshape(SSD_HEADS, 1)
    idx = jnp.arange(L)
    tril = (idx[None, :] <= idx[:, None]).astype(BF16)
    triu = (idx[:, None] <= idx[None, :]).astype(BF16)
    dskip = jnp.repeat(d_skip.astype(F32), SSD_HEAD_DIM).reshape(1, SSD_WIDTH)
    const = lambda shape: pl.BlockSpec(shape, lambda b, c: (0, 0))
    tok = lambda width: pl.BlockSpec((L, width), lambda b, c: (b * nc + c, 0))
    return pl.pallas_call(
        _ssd_kernel,
        grid=(BATCH, nc),
        in_specs=[tok(XBC_WIDTH), tok(SSD_WIDTH), tok(LANES),
                  pl.BlockSpec((SSD_HEADS, L), lambda b, c: (0, b * nc + c)),
                  const((CONV_K, XBC_WIDTH)), const((1, XBC_WIDTH)),
                  const((1, LANES)), const((SSD_HEADS, 1)), const((1, LANES)), const((SSD_HEADS, 1)),
                  const((1, SSD_WIDTH)), const((1, SSD_WIDTH)), const((L, L)), const((L, L))],
        out_specs=tok(SSD_WIDTH),
        out_shape=jax.ShapeDtypeStruct((TOKENS, SSD_WIDTH), BF16),
        scratch_shapes=[pltpu.VMEM((8 + L, XBC_WIDTH), F32),
                        pltpu.VMEM((SSD_STATE, SSD_WIDTH), F32)],
        compiler_params=_cparams(("arbitrary", "arbitrary")),
        name="ssd",
    )(xbc, z, dt, dt_t, conv_w.astype(F32), conv_b.astype(F32).reshape(1, XBC_WIDTH),
      pad_row(dt_bias), col8(dt_bias), pad_row(a_log), col8(a_log), dskip,
      ssd_norm_w.astype(F32).reshape(1, SSD_WIDTH), tril, triu)


OUT_TM = 256
ROUTE_W = 8
ROUTER_COLS = N_GROUPS + N_EXPERTS
RUN_ALIGN = 16
RUN_SHIFT = 4
LOCAL_ROWS = 1024
assert RUN_ALIGN == 1 << RUN_SHIFT and LOCAL_ROWS >= TOP_K * OUT_TM + N_EXPERTS * (RUN_ALIGN - 1)


def _lane_pick(values, lane, index):
    return jnp.sum(jnp.where(lane == index, values, 0.0), axis=-1, keepdims=True)


def _first_argmax(vals, lane):
    m = jnp.max(vals, axis=-1, keepdims=True)
    idx = jnp.min(jnp.where(vals == m, lane, float(LANES)), axis=-1, keepdims=True)
    return m, idx


def _out_router_kernel(att_ref, y_ref, x_ref, g1_ref, wof_ref, nw_ref, sc_ref, sh_ref, wr_ref, br_ref,
                       ltri_ref, sut_ref, x1_ref, h2_ref, route_ref, routet_ref, tcnt_ref,
                       wr_split_ref, logits_ref, wo_ref):
    i = pl.program_id(0)

    @pl.when(i == 0)
    def _():
        hi, lo = _split_bf16(wr_ref[...])
        wr_split_ref[:, 0:LANES] = hi
        wr_split_ref[:, LANES:2 * LANES] = lo
        logits_ref[...] = jnp.zeros_like(logits_ref)
        for r0 in range(0, D_MODEL, 256):
            wo_ref[r0:r0 + 256, :] = wof_ref[r0:r0 + 256, :].astype(BF16)

    logits = logits_ref[...]

    mixer = (jnp.dot(att_ref[...], wo_ref[0:ATT_WIDTH, :], preferred_element_type=F32)
             + jnp.dot(y_ref[...], wo_ref[ATT_WIDTH:ATT_WIDTH + SSD_WIDTH, :], preferred_element_type=F32))
    x1 = x_ref[...] + g1_ref[0] * mixer
    x1_ref[...] = x1
    yn = x1 * lax.rsqrt(jnp.mean(x1 * x1, axis=-1, keepdims=True) + EPS)
    h2 = (yn * nw_ref[...]) * (1.0 + sc_ref[0]) + sh_ref[0]
    h2_ref[...] = h2.astype(BF16)

    h_hi, h_lo = _split_bf16(h2)
    both = jnp.dot(h_hi, wr_split_ref[...], preferred_element_type=F32)
    logits_ref[...] = (both[:, 0:LANES] + both[:, LANES:2 * LANES]
                       + jnp.dot(h_lo, wr_split_ref[:, 0:LANES], preferred_element_type=F32)) + br_ref[...]

    tm = logits.shape[0]
    lane = lax.broadcasted_iota(I32, (tm, LANES), 1).astype(F32)

    gl = jnp.where(lane < N_GROUPS, logits, NEG_BIG)
    gmax, gidx = _first_argmax(gl, lane)
    g_p = 1.0 / jnp.sum(jnp.exp(gl - gmax), axis=-1, keepdims=True)

    lo_lane = N_GROUPS + EXPERTS_PER_GROUP * gidx
    el = jnp.where((lane >= lo_lane) & (lane < lo_lane + EXPERTS_PER_GROUP), logits, NEG_BIG)
    m1, i1 = _first_argmax(el, lane)
    m2, i2 = _first_argmax(jnp.where(lane == i1, NEG_BIG, el), lane)
    r = jnp.exp(m2 - m1)
    p1 = 1.0 / (1.0 + r)
    p2 = r / (1.0 + r)
    e0 = i1 - N_GROUPS
    e1 = i2 - N_GROUPS

    onehot = ((lane == e0) | (lane == e1)).astype(F32)
    tile_cnt = jnp.sum(onehot, axis=0, keepdims=True)
    run_len = jnp.floor((tile_cnt + (RUN_ALIGN - 1)) * (1.0 / RUN_ALIGN)) * RUN_ALIGN
    run_start = jnp.dot(jnp.broadcast_to(run_len, (8, LANES)).astype(BF16), sut_ref[...],
                        preferred_element_type=F32)[0:1, :]
    before = jnp.dot(ltri_ref[...], onehot.astype(BF16), preferred_element_type=F32) + run_start
    slot0 = _lane_pick(before, lane, e0)
    slot1 = _lane_pick(before, lane, e1)
    tcnt_ref[0] = tile_cnt

    rec = jnp.zeros((tm, LANES), F32)
    for k, v in enumerate([slot0, slot1, g_p * p1, g_p * p2, e0, e1]):
        rec = jnp.where(lane == k, v, rec)
    route_ref[...] = rec[:, 0:ROUTE_W]
    routet_ref[...] = rec.T[0:ROUTE_W, :]


def _out_router(att, y, x2d, mod3, w_out_b, norm_w, w_router, b_router):
    tm = OUT_TM
    n_steps = TOKENS // tm
    steps_per_batch = SEQ // tm
    idx = jnp.arange(tm)
    ltri = (idx[None, :] < idx[:, None]).astype(BF16)
    lidx = jnp.arange(LANES)
    sut = (lidx[:, None] < lidx[None, :]).astype(BF16)
    const = lambda shape: pl.BlockSpec(shape, lambda i: (0, 0))
    cur = lambda i: jnp.minimum(i, n_steps - 1)
    prev = lambda i: jnp.maximum(i - 1, 0)
    tok = lambda width: pl.BlockSpec((tm, width), lambda i: (cur(i), 0))
    modspec = lambda k: pl.BlockSpec((1, 1, D_MODEL), lambda i: ((cur(i) // steps_per_batch) * 6 + k, 0, 0))
    return pl.pallas_call(
        _out_router_kernel,
        grid=(n_steps + 1,),
        in_specs=[tok(ATT_WIDTH), tok(SSD_WIDTH), tok(D_MODEL), modspec(2),
                  const((D_MODEL, D_MODEL)), const((1, D_MODEL)), modspec(4), modspec(3),
                  const((D_MODEL, LANES)), const((1, LANES)), const((tm, tm)), const((LANES, LANES))],
        out_specs=[tok(D_MODEL), tok(D_MODEL),
                   pl.BlockSpec((tm, ROUTE_W), lambda i: (prev(i), 0)),
                   pl.BlockSpec((ROUTE_W, tm), lambda i: (prev(i), 0)),
                   pl.BlockSpec((1, 1, LANES), lambda i: (prev(i), 0, 0))],
        out_shape=[jax.ShapeDtypeStruct((TOKENS, D_MODEL), F32),
                   jax.ShapeDtypeStruct((TOKENS, D_MODEL), BF16),
                   jax.ShapeDtypeStruct((TOKENS, ROUTE_W), F32),
                   jax.ShapeDtypeStruct((n_steps * ROUTE_W, tm), F32),
                   jax.ShapeDtypeStruct((n_steps, 1, LANES), F32)],
        scratch_shapes=[pltpu.VMEM((D_MODEL, 2 * LANES), BF16), pltpu.VMEM((tm, LANES), F32),
                        pltpu.VMEM((D_MODEL, D_MODEL), BF16)],
        compiler_params=_cparams(("arbitrary",)),
        name="out_router",
    )(att, y, x2d, mod3, w_out_b, norm_w.reshape(1, D_MODEL), mod3, mod3, w_router, b_router, ltri, sut)


MOE_TM = 512
ZERO_ROWS = 256
N_TOKEN_TILES = TOKENS // OUT_TM
MAX_PIECES = LOCAL_ROWS // RUN_ALIGN
MAX_SORTED_ROWS = TOKENS * TOP_K + N_TOKEN_TILES * N_EXPERTS * (RUN_ALIGN - 1)
N_TILES = MAX_SORTED_ROWS // MOE_TM + N_EXPERTS
N_ROWS = N_TILES * MOE_TM
assert MOE_TM % ZERO_ROWS == 0


def _for_pieces(count, action):
    def body(j, carry):
        action(j)
        return carry

    lax.fori_loop(0, count, body, 0)


def _dispatch_kernel(seg_end_ref, used_end_ref, np_ref, dst_ref, routet_ref, h2_ref, xs_ref,
                     sbuf_ref, zero_ref, sems, zsem):
    i = pl.program_id(0)
    last = pl.num_programs(0) - 1
    buf = lax.rem(i, 2)

    def zero_fills(action):
        def tail_copy(row):
            return pltpu.make_async_copy(zero_ref.at[pl.ds(0, RUN_ALIGN)],
                                         xs_ref.at[pl.ds(pl.multiple_of(row, RUN_ALIGN), RUN_ALIGN)], zsem)

        def block_copy(block):
            start = pl.multiple_of(block * ZERO_ROWS, ZERO_ROWS)
            return pltpu.make_async_copy(zero_ref, xs_ref.at[pl.ds(start, ZERO_ROWS)], zsem)

        def tails(e, carry):
            def body(r, c):
                action(tail_copy(r * RUN_ALIGN))
                return c

            lax.fori_loop(used_end_ref[e] // RUN_ALIGN, seg_end_ref[e] // RUN_ALIGN, body, 0)
            return carry

        def blocks(block, carry):
            action(block_copy(block))
            return carry

        lax.fori_loop(0, N_EXPERTS, tails, 0)
        lax.fori_loop(seg_end_ref[N_EXPERTS - 1] // ZERO_ROWS, N_ROWS // ZERO_ROWS, blocks, 0)

    @pl.when(i == 0)
    def _():
        zero_ref[...] = jnp.zeros_like(zero_ref)
        zero_fills(lambda cp: cp.start())

    slot = lax.broadcasted_iota(I32, (LOCAL_ROWS, OUT_TM), 0).astype(F32)
    perm = jnp.where((slot == routet_ref[0:1, :]) | (slot == routet_ref[1:2, :]), 1.0, 0.0).astype(BF16)
    sbuf_ref[buf] = jnp.dot(perm, h2_ref[...], preferred_element_type=F32).astype(BF16)

    def piece(b, local_piece, sorted_row):
        local = pl.multiple_of(local_piece * RUN_ALIGN, RUN_ALIGN)
        return pltpu.make_async_copy(sbuf_ref.at[b, pl.ds(local, RUN_ALIGN)],
                                     xs_ref.at[pl.ds(pl.multiple_of(sorted_row, RUN_ALIGN), RUN_ALIGN)],
                                     sems.at[b])

    _for_pieces(np_ref[i], lambda j: piece(buf, j, dst_ref[0, 0, j]).start())

    @pl.when(i > 0)
    def _():
        _for_pieces(np_ref[jnp.maximum(i - 1, 0)], lambda j: piece(1 - buf, j, 0).wait())

    @pl.when(i == last)
    def _():
        _for_pieces(np_ref[i], lambda j: piece(buf, j, 0).wait())
        zero_fills(lambda cp: cp.wait())


def _piece_spec(index_map):
    return pl.BlockSpec((1, 1, MAX_PIECES), index_map, memory_space=pltpu.SMEM)


def _dispatch(seg_end, counts, n_pieces, piece_dst, route_t, h2):
    grid_spec = pltpu.PrefetchScalarGridSpec(
        num_scalar_prefetch=3,
        grid=(N_TOKEN_TILES,),
        in_specs=[_piece_spec(lambda i, se, cn, npc: (i, 0, 0)),
                  pl.BlockSpec((ROUTE_W, OUT_TM), lambda i, se, cn, npc: (i, 0)),
                  pl.BlockSpec((OUT_TM, D_MODEL), lambda i, se, cn, npc: (i, 0))],
        out_specs=pl.BlockSpec(memory_space=pl.ANY),
        scratch_shapes=[pltpu.VMEM((2, LOCAL_ROWS, D_MODEL), BF16),
                        pltpu.VMEM((ZERO_ROWS, D_MODEL), BF16),
                        pltpu.SemaphoreType.DMA((2,)), pltpu.SemaphoreType.DMA],
    )
    return pl.pallas_call(
        _dispatch_kernel,
        grid_spec=grid_spec,
        out_shape=jax.ShapeDtypeStruct((N_ROWS, D_MODEL), BF16),
        compiler_params=_cparams(("arbitrary",)),
        name="dispatch",
    )(seg_end, counts, n_pieces, piece_dst, route_t, h2)


def _experts_kernel(te_ref, seg_ref, nxt_ref, nu_ref, xs_ref, wg_hbm, wu_hbm, wd_hbm, ys_ref,
                    wg_buf, wu_buf, wd_buf, wgu_b_ref, wd_b_ref, wsem):
    i = pl.program_id(0)
    used = i < nu_ref[0]
    slot = lax.rem(seg_ref[i], 2)

    def weight_copies(expert, s):
        return [pltpu.make_async_copy(wg_hbm.at[expert], wg_buf.at[s], wsem.at[s]),
                pltpu.make_async_copy(wu_hbm.at[expert], wu_buf.at[s], wsem.at[s]),
                pltpu.make_async_copy(wd_hbm.at[expert], wd_buf.at[s], wsem.at[s])]

    @pl.when(i == 0)
    def _():
        for cp in weight_copies(te_ref[0], 0):
            cp.start()

    @pl.when(used & ((i == 0) | (te_ref[i] != te_ref[jnp.maximum(i - 1, 0)])))
    def _():
        for cp in weight_copies(te_ref[i], slot):
            cp.wait()

        @pl.when(nxt_ref[i] >= 0)
        def _():
            for cp in weight_copies(nxt_ref[i], 1 - slot):
                cp.start()

        wgu_b_ref[:, 0:D_EXPERT] = wg_buf[slot].astype(BF16)
        wgu_b_ref[:, D_EXPERT:2 * D_EXPERT] = wu_buf[slot].astype(BF16)
        wd_b_ref[...] = wd_buf[slot].astype(BF16)

    @pl.when(used)
    def _():
        h = jnp.dot(xs_ref[...], wgu_b_ref[...], preferred_element_type=F32)
        act = (_silu(h[:, 0:D_EXPERT]) * h[:, D_EXPERT:2 * D_EXPERT]).astype(BF16)
        ys_ref[...] = jnp.dot(act, wd_b_ref[...], preferred_element_type=F32).astype(BF16)


def _experts(tile_expert, tile_segment, next_expert, n_used, xs, w_gate, w_up, w_down):
    row_tile = lambda i, te, sg, nx, nu: (jnp.minimum(i, nu[0] - 1), 0)
    n_prefetch = 4
    grid_spec = pltpu.PrefetchScalarGridSpec(
        num_scalar_prefetch=n_prefetch,
        grid=(N_TILES,),
        in_specs=[pl.BlockSpec((MOE_TM, D_MODEL), row_tile),
                  pl.BlockSpec(memory_space=pl.ANY), pl.BlockSpec(memory_space=pl.ANY),
                  pl.BlockSpec(memory_space=pl.ANY)],
        out_specs=pl.BlockSpec((MOE_TM, D_MODEL), row_tile),
        scratch_shapes=[pltpu.VMEM((2, D_MODEL, D_EXPERT), F32), pltpu.VMEM((2, D_MODEL, D_EXPERT), F32),
                        pltpu.VMEM((2, D_EXPERT, D_MODEL), F32),
                        pltpu.VMEM((D_MODEL, 2 * D_EXPERT), BF16), pltpu.VMEM((D_EXPERT, D_MODEL), BF16),
                        pltpu.SemaphoreType.DMA((2,))],
    )
    return pl.pallas_call(
        _experts_kernel,
        grid_spec=grid_spec,
        out_shape=jax.ShapeDtypeStruct((N_ROWS, D_MODEL), BF16),
        input_output_aliases={n_prefetch: 0},
        compiler_params=_cparams(("arbitrary",)),
        name="experts",
    )(tile_expert, tile_segment, next_expert, n_used, xs, w_gate, w_up, w_down)


def _combine_kernel(np_ref, dst_ref, dst_next_ref, route_ref, x1_ref, g2_ref, ys_ref, o_ref, gbuf_ref, sems):
    i = pl.program_id(0)
    last = pl.num_programs(0) - 1
    buf = lax.rem(i, 2)

    def piece(b, local_piece, sorted_row):
        local = pl.multiple_of(local_piece * RUN_ALIGN, RUN_ALIGN)
        return pltpu.make_async_copy(ys_ref.at[pl.ds(pl.multiple_of(sorted_row, RUN_ALIGN), RUN_ALIGN)],
                                     gbuf_ref.at[b, pl.ds(local, RUN_ALIGN)], sems.at[b])

    @pl.when(i == 0)
    def _():
        gbuf_ref[...] = jnp.zeros_like(gbuf_ref)
        _for_pieces(np_ref[0], lambda j: piece(0, j, dst_ref[0, 0, j]).start())

    @pl.when(i < last)
    def _():
        _for_pieces(np_ref[jnp.minimum(i + 1, last)], lambda j: piece(1 - buf, j, dst_next_ref[0, 0, j]).start())

    rec = route_ref[...]
    slot = lax.broadcasted_iota(I32, (OUT_TM, LOCAL_ROWS), 1).astype(F32)
    weights = (jnp.where(slot == rec[:, 0:1], rec[:, 2:3], 0.0)
               + jnp.where(slot == rec[:, 1:2], rec[:, 3:4], 0.0)).astype(BF16)
    _for_pieces(np_ref[i], lambda j: piece(buf, j, 0).wait())
    moe = jnp.dot(weights, gbuf_ref[buf], preferred_element_type=F32)
    o_ref[...] = x1_ref[...] + g2_ref[0] * moe


def _combine(n_pieces, piece_dst, route, x1, mod3, ys):
    tm = OUT_TM
    steps_per_batch = SEQ // tm
    grid_spec = pltpu.PrefetchScalarGridSpec(
        num_scalar_prefetch=1,
        grid=(N_TOKEN_TILES,),
        in_specs=[_piece_spec(lambda i, npc: (i, 0, 0)),
                  _piece_spec(lambda i, npc: (jnp.minimum(i + 1, N_TOKEN_TILES - 1), 0, 0)),
                  pl.BlockSpec((tm, ROUTE_W), lambda i, npc: (i, 0)),
                  pl.BlockSpec((tm, D_MODEL), lambda i, npc: (i, 0)),
                  pl.BlockSpec((1, 1, D_MODEL), lambda i, npc: ((i // steps_per_batch) * 6 + 5, 0, 0)),
                  pl.BlockSpec(memory_space=pl.ANY)],
        out_specs=pl.BlockSpec((tm, D_MODEL), lambda i, npc: (i, 0)),
        scratch_shapes=[pltpu.VMEM((2, LOCAL_ROWS, D_MODEL), BF16), pltpu.SemaphoreType.DMA((2,))],
    )
    return pl.pallas_call(
        _combine_kernel,
        grid_spec=grid_spec,
        out_shape=jax.ShapeDtypeStruct((TOKENS, D_MODEL), F32),
        compiler_params=_cparams(("arbitrary",)),
        name="combine",
    )(n_pieces, piece_dst, piece_dst, route, x1, mod3, ys)


def kernel(x, c, positions, norm1_w, norm2_w, w_ada, b_ada, w_in, conv_w, conv_b, dt_bias, a_log,
           d_skip, ssd_norm_w, q_norm_w, k_norm_w, sinks, w_out, w_group, b_group, w_expert, b_expert,
           w_gate, w_up, w_down):
    assert x.shape == (BATCH, SEQ, D_MODEL) and w_in.shape == (D_MODEL, IN_WIDTH)
    x2d = x.reshape(TOKENS, D_MODEL)
    mod = _ada_mod(c, w_ada, b_ada)
    mod3 = mod.reshape(BATCH * 6, 1, D_MODEL)

    qkv, z, xbc, dt, dt_t = _in_proj(x2d, norm1_w, mod3, w_in)
    att = _attention(qkv, positions, q_norm_w, k_norm_w, sinks)
    y = _ssd(xbc, z, dt, dt_t, conv_w, conv_b, dt_bias, a_log, d_skip, ssd_norm_w)

    w_router = jnp.pad(jnp.concatenate([w_group, w_expert], axis=1).astype(F32),
                       ((0, 0), (0, LANES - ROUTER_COLS)))
    b_router = jnp.pad(jnp.concatenate([b_group, b_expert]).astype(F32),
                       (0, LANES - ROUTER_COLS)).reshape(1, LANES)
    x1, h2, route, route_t, tcnt = _out_router(att, y, x2d, mod3, w_out.astype(F32), norm2_w,
                                                w_router, b_router)

    tc = tcnt[:, 0, 0:N_EXPERTS].astype(I32)
    run_rows = ((tc + RUN_ALIGN - 1) // RUN_ALIGN) * RUN_ALIGN
    counts = jnp.sum(run_rows, axis=0)
    padded = ((counts + MOE_TM - 1) // MOE_TM) * MOE_TM
    seg_end = jnp.cumsum(padded)
    seg_start = seg_end - padded
    run_dst = seg_start[None, :] + jnp.cumsum(run_rows, axis=0) - run_rows
    n_used = (seg_end[-1] // MOE_TM).reshape(1)
    last_row = jnp.minimum(jnp.arange(N_TILES, dtype=I32) * MOE_TM, seg_end[-1] - 1)
    tile_expert = jnp.sum((seg_end[None, :] <= last_row[:, None]).astype(I32), axis=1)

    run_pieces = run_rows // RUN_ALIGN
    piece_end = jnp.cumsum(run_pieces, axis=1)
    n_pieces = piece_end[:, -1]
    j = jnp.arange(MAX_PIECES, dtype=I32)
    piece_expert = jnp.sum((piece_end[:, None, :] <= j[None, :, None]).astype(I32), axis=2)
    in_run = (piece_expert[:, :, None] == jnp.arange(N_EXPERTS, dtype=I32)[None, None, :]).astype(I32)
    run_base = run_dst - RUN_ALIGN * (piece_end - run_pieces)
    piece_dst = jnp.sum(in_run * run_base[:, None, :], axis=2) + RUN_ALIGN * j[None, :]
    piece_dst = piece_dst.astype(I32).reshape(N_TOKEN_TILES, 1, MAX_PIECES)

    experts = jnp.arange(N_EXPERTS, dtype=I32)
    nonempty = padded > 0
    seg_rank = jnp.cumsum(nonempty.astype(I32)) - 1
    later = nonempty[None, :] & (experts[None, :] > experts[:, None])
    next_of = jnp.min(jnp.where(later, experts[None, :], N_EXPERTS), axis=1)
    next_of = jnp.where(next_of == N_EXPERTS, -1, next_of)
    tile_is = (tile_expert[:, None] == experts[None, :]).astype(I32)
    tile_segment = jnp.sum(tile_is * seg_rank[None, :], axis=1)
    next_expert = jnp.sum(tile_is * next_of[None, :], axis=1)

    xs = _dispatch(seg_end.astype(I32), (seg_start + counts).astype(I32), n_pieces.astype(I32), piece_dst,
                   route_t, h2)
    ys = _experts(tile_expert, tile_segment.astype(I32), next_expert.astype(I32), n_used.astype(I32),
                  xs, w_gate, w_up, w_down)
    out = _combine(n_pieces.astype(I32), piece_dst, route, x1, mod3, ys)
    return out.reshape(BATCH, SEQ, D_MODEL)
```

```python
import jax
import jax.numpy as jnp
from jax import lax
from jax.experimental import pallas as pl
from jax.experimental.pallas import tpu as pltpu

F32 = jnp.float32
BF16 = jnp.bfloat16
I32 = jnp.int32

D_MODEL = 1024
BATCH = 2
SEQ = 8192
TOKENS = BATCH * SEQ
ATT_HEADS = 8
ATT_KV_HEADS = 2
HEAD_DIM = 64
ATT_WIDTH = ATT_HEADS * HEAD_DIM
KV_WIDTH = ATT_KV_HEADS * HEAD_DIM
ATT_BLOCK = 128
ROPE_DIM = HEAD_DIM // 4
ROPE_THETA = 500000.0
SSD_HEADS = 8
SSD_HEAD_DIM = 64
SSD_WIDTH = SSD_HEADS * SSD_HEAD_DIM
SSD_GROUPS = 2
SSD_STATE = 128
CONV_K = 4
CHUNK = 128
XBC_WIDTH = SSD_WIDTH + 2 * SSD_GROUPS * SSD_STATE
IN_WIDTH = ATT_WIDTH + 2 * KV_WIDTH + SSD_WIDTH + XBC_WIDTH + SSD_HEADS
N_GROUPS = 4
EXPERTS_PER_GROUP = 8
N_EXPERTS = N_GROUPS * EXPERTS_PER_GROUP
TOP_K = 2
D_EXPERT = 256
EPS = 1e-6

LANES = 128
QKV_WIDTH = ATT_WIDTH + 2 * KV_WIDTH
IN_PAD = QKV_WIDTH + SSD_WIDTH + XBC_WIDTH + LANES
NEG_BIG = -1e30

VMEM_LIMIT = 48 * 1024 * 1024


def _cparams(sem):
    return pltpu.CompilerParams(dimension_semantics=sem, vmem_limit_bytes=VMEM_LIMIT)


def _split_bf16(x):
    hi = x.astype(BF16)
    lo = (x - hi.astype(F32)).astype(BF16)
    return hi, lo


ADA_TN = 768


def _ada_kernel(ct_ref, w_ref, b_ref, o_ref):
    ct = ct_ref[...]
    s = ct * jax.nn.sigmoid(ct)
    w = w_ref[...]
    rows = [jnp.sum(s[:, b:b + 1] * w, axis=0, keepdims=True) for b in range(BATCH)]
    o_ref[...] = jnp.concatenate(rows, axis=0) + b_ref[...]


def _ada_mod(c, w_ada, b_ada):
    n = w_ada.shape[1]
    return pl.pallas_call(
        _ada_kernel,
        grid=(n // ADA_TN,),
        in_specs=[pl.BlockSpec((D_MODEL, BATCH), lambda j: (0, 0)),
                  pl.BlockSpec((D_MODEL, ADA_TN), lambda j: (0, j)),
                  pl.BlockSpec((1, ADA_TN), lambda j: (0, j))],
        out_specs=pl.BlockSpec((BATCH, ADA_TN), lambda j: (0, j)),
        out_shape=jax.ShapeDtypeStruct((BATCH, n), F32),
        compiler_params=_cparams(("arbitrary",)),
        name="ada_mod",
    )(c.T, w_ada, b_ada.reshape(1, n))


INPROJ_TM = 512
_INPROJ_CHUNK = 256


def _inproj_kernel(x_ref, nw_ref, sc_ref, sh_ref, wf_ref, wdt_ref, qkv_ref, z_ref, xbc_ref, dt_ref, dtt_ref,
                   w_ref):
    @pl.when(pl.program_id(0) == 0)
    def _():
        for c0 in range(0, IN_PAD - LANES, _INPROJ_CHUNK):
            w_ref[:, c0:c0 + _INPROJ_CHUNK] = wf_ref[:, c0:c0 + _INPROJ_CHUNK].astype(BF16)
        w_ref[:, IN_PAD - LANES:IN_PAD] = wdt_ref[...].astype(BF16)

    x = x_ref[...]
    y = x * lax.rsqrt(jnp.mean(x * x, axis=-1, keepdims=True) + EPS)
    h = (y * nw_ref[...]) * (1.0 + sc_ref[0]) + sh_ref[0]
    hb = h.astype(BF16)

    def proj(c0, c1):
        return jnp.dot(hb, w_ref[:, c0:c1], preferred_element_type=F32)

    for c0 in range(0, QKV_WIDTH, _INPROJ_CHUNK):
        qkv_ref[:, c0:c0 + _INPROJ_CHUNK] = proj(c0, c0 + _INPROJ_CHUNK).astype(BF16)
    base = QKV_WIDTH
    for c0 in range(0, SSD_WIDTH, _INPROJ_CHUNK):
        z_ref[:, c0:c0 + _INPROJ_CHUNK] = proj(base + c0, base + c0 + _INPROJ_CHUNK).astype(BF16)
    base += SSD_WIDTH
    for c0 in range(0, XBC_WIDTH, _INPROJ_CHUNK):
        xbc_ref[:, c0:c0 + _INPROJ_CHUNK] = proj(base + c0, base + c0 + _INPROJ_CHUNK).astype(BF16)
    base += XBC_WIDTH
    dt = proj(base, base + LANES)
    dt_ref[...] = dt
    dtt_ref[...] = dt.T[0:SSD_HEADS, :]


def _in_proj(x2d, norm_w, mod3, w_in):
    tm = INPROJ_TM
    steps_per_batch = SEQ // tm
    w_dt = jnp.pad(w_in[:, IN_WIDTH - SSD_HEADS:IN_WIDTH].astype(F32), ((0, 0), (0, LANES - SSD_HEADS)))
    return pl.pallas_call(
        _inproj_kernel,
        grid=(TOKENS // tm,),
        in_specs=[pl.BlockSpec((tm, D_MODEL), lambda i: (i, 0)),
                  pl.BlockSpec((1, D_MODEL), lambda i: (0, 0)),
                  pl.BlockSpec((1, 1, D_MODEL), lambda i: ((i // steps_per_batch) * 6 + 1, 0, 0)),
                  pl.BlockSpec((1, 1, D_MODEL), lambda i: ((i // steps_per_batch) * 6 + 0, 0, 0)),
                  pl.BlockSpec((D_MODEL, IN_WIDTH), lambda i: (0, 0), pipeline_mode=pl.Buffered(1)),
                  pl.BlockSpec((D_MODEL, LANES), lambda i: (0, 0))],
        out_specs=[pl.BlockSpec((tm, QKV_WIDTH), lambda i: (i, 0)),
                   pl.BlockSpec((tm, SSD_WIDTH), lambda i: (i, 0)),
                   pl.BlockSpec((tm, XBC_WIDTH), lambda i: (i, 0)),
                   pl.BlockSpec((tm, LANES), lambda i: (i, 0)),
                   pl.BlockSpec((SSD_HEADS, tm), lambda i: (0, i))],
        out_shape=[jax.ShapeDtypeStruct((TOKENS, QKV_WIDTH), BF16),
                   jax.ShapeDtypeStruct((TOKENS, SSD_WIDTH), BF16),
                   jax.ShapeDtypeStruct((TOKENS, XBC_WIDTH), BF16),
                   jax.ShapeDtypeStruct((TOKENS, LANES), F32),
                   jax.ShapeDtypeStruct((SSD_HEADS, TOKENS), F32)],
        scratch_shapes=[pltpu.VMEM((D_MODEL, IN_PAD), BF16)],
        compiler_params=_cparams(("arbitrary",)),
        name="in_proj",
    )(x2d, norm_w.reshape(1, D_MODEL), mod3, mod3, w_in.astype(F32), w_dt)


ATT_SUB = 2


ROPE_TM = 2048
_ROPE_HALF = ROPE_DIM // 2
_TOK_PER_ROW = LANES // _ROPE_HALF


def _exact_dot(x, onehot_b):
    hi, lo = _split_bf16(x)
    return (jnp.dot(hi, onehot_b, preferred_element_type=F32)
            + jnp.dot(lo, onehot_b, preferred_element_type=F32))


def _rope_kernel(pos_ref, freq_ref, sel_ref, own_ref, gcos_ref, gs1_ref, gs2_ref, ident_ref,
                 cos_ref, s1_ref, s2_ref):
    ang = pos_ref[...].astype(F32) * freq_ref[...]
    cos_p, sin_p = jnp.cos(ang), jnp.sin(ang)
    hi_c, lo_c = _split_bf16(cos_p)
    hi_s, lo_s = _split_bf16(sin_p)
    sel = sel_ref[...]
    rows_c = jnp.dot(sel, hi_c, preferred_element_type=F32) + jnp.dot(sel, lo_c, preferred_element_type=F32)
    rows_s = jnp.dot(sel, hi_s, preferred_element_type=F32) + jnp.dot(sel, lo_s, preferred_element_type=F32)
    own = own_ref[...]
    cos_ref[...] = _exact_dot(rows_c * own, gcos_ref[...]) + ident_ref[...]
    s1_ref[...] = _exact_dot(rows_s * own, gs1_ref[...])
    s2_ref[...] = _exact_dot(rows_s * own, gs2_ref[...])


def _rope_tables(positions):
    half, per_row = _ROPE_HALF, _TOK_PER_ROW
    rows = ROPE_TM // per_row
    pos_rep = jnp.repeat(positions.reshape(TOKENS).astype(I32), half).reshape(TOKENS // per_row, LANES)
    inv_freq = jnp.power(ROPE_THETA, -jnp.arange(half, dtype=F32) * 2.0 / ROPE_DIM)
    freq = jnp.tile(inv_freq, per_row).reshape(1, LANES)
    tok = jnp.arange(ROPE_TM)
    lane = jnp.arange(LANES)
    sel = (tok[:, None] // per_row == jnp.arange(rows)[None, :]).astype(BF16)
    own = (lane[None, :] // half == tok[:, None] % per_row).astype(F32)
    d = lane % HEAD_DIM
    src_f = lane % half
    hits = lambda lo, hi: ((src_f[:, None] == d[None, :] % half) & (d[None, :] >= lo) & (d[None, :] < hi))
    gcos = hits(0, ROPE_DIM).astype(BF16)
    gs1 = -hits(0, half).astype(BF16)
    gs2 = hits(half, ROPE_DIM).astype(BF16)
    ident = (d >= ROPE_DIM).astype(F32).reshape(1, LANES)
    const = lambda shape: pl.BlockSpec(shape, lambda i: (0, 0))
    out_spec = pl.BlockSpec((ROPE_TM, LANES), lambda i: (i, 0))
    out = jax.ShapeDtypeStruct((TOKENS, LANES), F32)
    return pl.pallas_call(
        _rope_kernel,
        grid=(TOKENS // ROPE_TM,),
        in_specs=[pl.BlockSpec((rows, LANES), lambda i: (i, 0)), const((1, LANES)),
                  const((ROPE_TM, rows)), const((ROPE_TM, LANES)),
                  const((LANES, LANES)), const((LANES, LANES)), const((LANES, LANES)), const((1, LANES))],
        out_specs=[out_spec, out_spec, out_spec],
        out_shape=[out, out, out],
        compiler_params=_cparams(("arbitrary",)),
        name="rope_tables",
    )(pos_rep, freq, sel, own, gcos, gs1, gs2, ident)


def _seg_meansq(xf, ones128):
    rows, width = xf.shape
    nt = width // LANES
    parts = _split_bf16(xf * xf)
    stacked = jnp.concatenate([p[:, t * LANES:(t + 1) * LANES] for p in parts for t in range(nt)], axis=0)
    tot = jnp.dot(stacked, ones128, preferred_element_type=F32)
    tiles = [tot[t * rows:(t + 1) * rows] + tot[(nt + t) * rows:(nt + t + 1) * rows] for t in range(nt)]
    return jnp.concatenate(tiles, axis=1) * (1.0 / HEAD_DIM)


def _norm_rope(x_bf, w_row, ones_bd, cosf, s1, s2):
    xf = x_bf.astype(F32)
    width = xf.shape[1]
    xn = xf * lax.rsqrt(_seg_meansq(xf, ones_bd) + EPS) * w_row
    half = ROPE_DIM // 2
    up = pltpu.roll(xn, width - half, axis=1)
    down = pltpu.roll(xn, half, axis=1)
    return xn * cosf + up * s1 + down * s2


def _attn_kernel(sink_ref, q_ref, kv_ref, cos_ref, s1_ref, s2_ref, qw_ref, kw_ref,
                 ones_ref, o_ref, kprev_ref, vprev_ref):
    j = pl.program_id(1)
    blk = ATT_BLOCK

    @pl.when(j == 0)
    def _():
        kprev_ref[...] = jnp.zeros_like(kprev_ref)
        vprev_ref[...] = jnp.zeros_like(vprev_ref)

    cos1 = cos_ref[...]
    s1_1 = s1_ref[...]
    s2_1 = s2_ref[...]
    reps = ATT_WIDTH // LANES
    cosq = jnp.concatenate([cos1] * reps, axis=1)
    s1q = jnp.concatenate([s1_1] * reps, axis=1)
    s2q = jnp.concatenate([s2_1] * reps, axis=1)

    q = _norm_rope(q_ref[...], qw_ref[...], ones_ref[...], cosq, s1q, s2q)
    qf = q * (HEAD_DIM ** -0.5)
    kv = kv_ref[...]
    kn = _norm_rope(kv[:, 0:KV_WIDTH], kw_ref[...], ones_ref[...], cos1, s1_1, s2_1)
    vn = kv[:, KV_WIDTH:2 * KV_WIDTH].astype(F32)

    kall = jnp.concatenate([kprev_ref[...], kn], axis=0)
    vall = jnp.concatenate([vprev_ref[...], vn], axis=0)
    kprev_ref[...] = kn[(ATT_SUB - 1) * blk:ATT_SUB * blk]
    vprev_ref[...] = vn[(ATT_SUB - 1) * blk:ATT_SUB * blk]

    lo_all = lax.broadcasted_iota(I32, kall.shape, 1) < HEAD_DIM
    ones_all = jnp.ones(kall.shape, BF16)

    row = lax.broadcasted_iota(I32, (2 * blk, blk), 0)
    col = lax.broadcasted_iota(I32, (2 * blk, blk), 1)
    from_prev = col > (row & (blk - 1))
    second_tile = lax.broadcasted_iota(I32, (2 * blk, 1), 0) >= blk
    zero_p = jnp.zeros((2 * blk, blk), F32)

    k_par, v_par = [], []
    for g in range(ATT_KV_HEADS):
        keep = lo_all if g == 0 else ~lo_all
        k_own = jnp.where(keep, kall, 0.0)
        v_own = jnp.where(keep, vall, 0.0)
        k_oth = pltpu.roll(k_own, HEAD_DIM, axis=1)
        v_oth = pltpu.roll(v_own, HEAD_DIM, axis=1)
        k_lo, k_hi = (k_own, k_oth) if g == 0 else (k_oth, k_own)
        v_lo, v_hi = (v_own, v_oth) if g == 0 else (v_oth, v_own)
        k_par.append((k_lo.astype(BF16), k_hi.astype(BF16)))
        v_par.append((jnp.concatenate([v_lo.astype(BF16), ones_all], axis=1),
                      jnp.concatenate([v_hi.astype(BF16), ones_all], axis=1)))

    problems = [(g, sub) for g in range(ATT_KV_HEADS) for sub in range(ATT_SUB)]
    scores = []
    for g, sub in problems:
        r0, c0 = sub * blk, g * 2 * LANES
        qcat = jnp.concatenate([qf[r0:r0 + blk, c0:c0 + LANES],
                                qf[r0:r0 + blk, c0 + LANES:c0 + 2 * LANES]], axis=0).astype(BF16)
        kw = jnp.concatenate([k_par[g][0][r0:r0 + 2 * blk], k_par[g][1][r0:r0 + 2 * blk]], axis=0)
        scores.append(lax.dot_general(qcat, kw, (((1,), (1,)), ((), ())),
                                      preferred_element_type=F32))

    weights, rescale = [], []
    for (g, sub), s_all in zip(problems, scores):
        for par in range(2):
            s = s_all[:, par * 2 * blk:(par + 1) * 2 * blk]
            s_prev = s[:, 0:blk]
            if sub == 0:
                s_prev = s_prev + jnp.where(j > 0, 0.0, NEG_BIG)
            s = jnp.where(from_prev, s_prev, s[:, blk:2 * blk])
            h_first = ATT_HEADS // ATT_KV_HEADS * g + par
            sink = jnp.where(second_tile, sink_ref[h_first + 2], sink_ref[h_first])
            m = jnp.maximum(jnp.max(s, axis=-1, keepdims=True), sink)
            p = jnp.exp(s - m)
            weights.append(jnp.concatenate([jnp.where(from_prev, p, zero_p), jnp.where(from_prev, zero_p, p)],
                                           axis=1).astype(BF16))
            rescale.append(jnp.exp(sink - m))

    outs = []
    for idx, (g, sub) in enumerate(problems):
        for par in range(2):
            outs.append(jnp.dot(weights[2 * idx + par], v_par[g][par][sub * blk:(sub + 2) * blk],
                                preferred_element_type=F32))

    for idx, (g, sub) in enumerate(problems):
        r0, c0 = sub * blk, g * 2 * LANES
        pair = None
        for par in range(2):
            o = outs[2 * idx + par]
            part = o[:, 0:LANES] * (1.0 / (o[:, LANES:2 * LANES] + rescale[2 * idx + par]))
            pair = part if pair is None else pair + part
        o_ref[r0:r0 + blk, c0:c0 + LANES] = pair[0:blk].astype(BF16)
        o_ref[r0:r0 + blk, c0 + LANES:c0 + 2 * LANES] = pair[blk:2 * blk].astype(BF16)


def _attention(qkv, positions, q_norm_w, k_norm_w, sinks):
    cosf, s1, s2 = _rope_tables(positions)
    qw = jnp.tile(q_norm_w.astype(F32), ATT_HEADS).reshape(1, ATT_WIDTH)
    kw = jnp.tile(k_norm_w.astype(F32), ATT_KV_HEADS).reshape(1, KV_WIDTH)
    seg = jnp.arange(LANES) // HEAD_DIM
    ones128 = (seg[:, None] == seg[None, :]).astype(BF16)
    const = lambda shape: pl.BlockSpec(shape, lambda b, j, s: (0, 0))
    rows = ATT_SUB * ATT_BLOCK
    nb = SEQ // rows
    tok = lambda width, cb: pl.BlockSpec((rows, width), lambda b, j, s: (b * nb + j, cb))
    grid_spec = pltpu.PrefetchScalarGridSpec(
        num_scalar_prefetch=1,
        grid=(BATCH, nb),
        in_specs=[tok(ATT_WIDTH, 0), tok(2 * KV_WIDTH, 2), tok(LANES, 0), tok(LANES, 0), tok(LANES, 0),
                  const((1, ATT_WIDTH)), const((1, KV_WIDTH)), const((LANES, LANES))],
        out_specs=tok(ATT_WIDTH, 0),
        scratch_shapes=[pltpu.VMEM((ATT_BLOCK, KV_WIDTH), F32),
                        pltpu.VMEM((ATT_BLOCK, KV_WIDTH), F32)],
    )
    return pl.pallas_call(
        _attn_kernel,
        grid_spec=grid_spec,
        out_shape=jax.ShapeDtypeStruct((TOKENS, ATT_WIDTH), BF16),
        compiler_params=_cparams(("arbitrary", "arbitrary")),
        name="attention",
    )(sinks.astype(F32), qkv, qkv, cosf, s1, s2, qw, kw, ones128)


def _softplus(x):
    return jnp.maximum(x, 0.0) + jnp.log1p(jnp.exp(-jnp.abs(x)))


def _silu(x):
    return x * jax.nn.sigmoid(x)


def _ssd_kernel(xbc_ref, z_ref, dt_ref, dtt_ref, cw_ref, cb_ref, dtb_row_ref, dtb_col_ref,
                alog_row_ref, alog_col_ref, dskip_ref, nw_ref, tril_ref, triu_ref,
                o_ref, conv_ref, state_ref):
    c = pl.program_id(1)
    L = CHUNK
    tail = 8

    @pl.when(c == 0)
    def _():
        conv_ref[0:tail, :] = jnp.zeros((tail, XBC_WIDTH), F32)
        state_ref[...] = jnp.zeros_like(state_ref)

    xb = xbc_ref[...].astype(F32)
    conv_ref[tail:tail + L, :] = xb
    acc = cb_ref[...] + cw_ref[CONV_K - 1:CONV_K, :] * xb
    for k in range(CONV_K - 1):
        off = tail - (CONV_K - 1) + k
        acc = acc + cw_ref[k:k + 1, :] * conv_ref[off:off + L, :]
    conv_ref[0:tail, :] = xb[L - tail:L, :]
    u = _silu(acc)
    xs = u[:, 0:SSD_WIDTH]
    bmat = u[:, SSD_WIDTH:SSD_WIDTH + SSD_GROUPS * SSD_STATE]
    cmat = u[:, SSD_WIDTH + SSD_GROUPS * SSD_STATE:XBC_WIDTH]

    dt = _softplus(dt_ref[...] + dtb_row_ref[...])
    a = dt * (-jnp.exp(alog_row_ref[...]))
    a_hi, a_lo = _split_bf16(a)
    a_cum = (jnp.dot(tril_ref[...], a_hi, preferred_element_type=F32)
             + jnp.dot(tril_ref[...], a_lo, preferred_element_type=F32))
    dt_t = _softplus(dtt_ref[...] + dtb_col_ref[...])
    a_t = dt_t * (-jnp.exp(alog_col_ref[...]))
    at_hi, at_lo = _split_bf16(a_t)
    a_cum_t = (jnp.dot(at_hi, triu_ref[...], preferred_element_type=F32)
               + jnp.dot(at_lo, triu_ref[...], preferred_element_type=F32))
    a_end_t = a_cum_t[:, L - 1:L]
    wst_t = jnp.exp(a_end_t - a_cum_t) * dt_t
    cdec_t = jnp.exp(a_end_t)

    row = lax.broadcasted_iota(I32, (L, L), 0)
    col = lax.broadcasted_iota(I32, (L, L), 1)
    causal = col <= row
    lane = lax.broadcasted_iota(I32, (L, LANES), 1)
    lo_half = lane < SSD_HEAD_DIM

    xs_b = xs.astype(BF16)
    heads_per_group = SSD_HEADS // SSD_GROUPS
    gated = []
    for g in range(SSD_GROUPS):
        b_g = bmat[:, g * SSD_STATE:(g + 1) * SSD_STATE]
        c_g = cmat[:, g * SSD_STATE:(g + 1) * SSD_STATE]
        cb = lax.dot_general(c_g.astype(BF16), b_g.astype(BF16), (((1,), (1,)), ((), ())),
                             preferred_element_type=F32)
        b_gt = b_g.T
        for t in range(heads_per_group // 2):
            tile = g * (heads_per_group // 2) + t
            c0 = tile * LANES
            xs_tile = xs_b[:, c0:c0 + LANES]
            st_tile = state_ref[:, c0:c0 + LANES]
            st_b = st_tile.astype(BF16)
            y_tile = jnp.zeros((L, LANES), F32)
            new_tile = jnp.zeros((SSD_STATE, LANES), F32)
            for e in range(2):
                h = 2 * tile + e
                keep = lo_half if e == 0 else ~lo_half
                colb = jnp.broadcast_to(a_cum[:, h:h + 1], (L, L))
                rowb = a_cum_t[h:h + 1, :]
                decay = jnp.exp(jnp.where(causal, colb - rowb, NEG_BIG))
                w_in = (cb * decay) * dt_t[h:h + 1, :]
                w_off = c_g * jnp.exp(colb)
                lhs = jnp.concatenate([w_in, w_off], axis=1).astype(BF16)
                rhs = jnp.concatenate([jnp.where(keep, xs_tile, jnp.zeros_like(xs_tile)),
                                       jnp.where(keep, st_b, jnp.zeros_like(st_b))], axis=0)
                y_tile = y_tile + jnp.dot(lhs, rhs, preferred_element_type=F32)
                m_h = (b_gt * wst_t[h:h + 1, :]).astype(BF16)
                new_tile = new_tile + jnp.dot(m_h, jnp.where(keep, xs_tile, jnp.zeros_like(xs_tile)),
                                              preferred_element_type=F32)
            cd = jnp.where(lo_half[0:1, :], cdec_t[2 * tile:2 * tile + 1, :],
                           cdec_t[2 * tile + 1:2 * tile + 2, :])
            state_ref[:, c0:c0 + LANES] = st_tile * cd + new_tile
            y_full = y_tile + dskip_ref[:, c0:c0 + LANES] * xs[:, c0:c0 + LANES]
            gated.append(y_full * _silu(z_ref[:, c0:c0 + LANES].astype(F32)))

    gw = SSD_WIDTH // SSD_GROUPS
    tiles_per_group = gw // LANES
    for g in range(SSD_GROUPS):
        yg = jnp.concatenate(gated[g * tiles_per_group:(g + 1) * tiles_per_group], axis=1)
        ms = jnp.mean(yg * yg, axis=-1, keepdims=True)
        o_ref[:, g * gw:(g + 1) * gw] = ((yg * lax.rsqrt(ms + EPS)) * nw_ref[:, g * gw:(g + 1) * gw]).astype(o_ref.dtype)


def _ssd(xbc, z, dt, dt_t, conv_w, conv_b, dt_bias, a_log, d_skip, ssd_norm_w):
    nc = SEQ // CHUNK
    L = CHUNK
    pad_row = lambda v: jnp.pad(v.astype(F32), (0, LANES - SSD_HEADS)).reshape(1, LANES)
    col8 = lambda v: v.astype(F32).reshape(SSD_HEADS, 1)
    idx = jnp.arange(L)
    tril = (idx[None, :] <= idx[:, None]).astype(BF16)
    triu = (idx[:, None] <= idx[None, :]).astype(BF16)
    dskip = jnp.repeat(d_skip.astype(F32), SSD_HEAD_DIM).reshape(1, SSD_WIDTH)
    const = lambda shape: pl.BlockSpec(shape, lambda b, c: (0, 0))
    tok = lambda width: pl.BlockSpec((L, width), lambda b, c: (b * nc + c, 0))
    return pl.pallas_call(
        _ssd_kernel,
        grid=(BATCH, nc),
        in_specs=[tok(XBC_WIDTH), tok(SSD_WIDTH), tok(LANES),
                  pl.BlockSpec((SSD_HEADS, L), lambda b, c: (0, b * nc + c)),
                  const((CONV_K, XBC_WIDTH)), const((1, XBC_WIDTH)),
                  const((1, LANES)), const((SSD_HEADS, 1)), const((1, LANES)), const((SSD_HEADS, 1)),
                  const((1, SSD_WIDTH)), const((1, SSD_WIDTH)), const((L, L)), const((L, L))],
        out_specs=tok(SSD_WIDTH),
        out_shape=jax.ShapeDtypeStruct((TOKENS, SSD_WIDTH), BF16),
        scratch_shapes=[pltpu.VMEM((8 + L, XBC_WIDTH), F32),
                        pltpu.VMEM((SSD_STATE, SSD_WIDTH), F32)],
        compiler_params=_cparams(("arbitrary", "arbitrary")),
        name="ssd",
    )(xbc, z, dt, dt_t, conv_w.astype(F32), conv_b.astype(F32).reshape(1, XBC_WIDTH),
      pad_row(dt_bias), col8(dt_bias), pad_row(a_log), col8(a_log), dskip,
      ssd_norm_w.astype(F32).reshape(1, SSD_WIDTH), tril, triu)


OUT_TM = 256
ROUTE_W = 8
ROUTER_COLS = N_GROUPS + N_EXPERTS
RUN_ALIGN = 16
RUN_SHIFT = 4
LOCAL_ROWS = 1024
assert RUN_ALIGN == 1 << RUN_SHIFT and LOCAL_ROWS >= TOP_K * OUT_TM + N_EXPERTS * (RUN_ALIGN - 1)


def _lane_pick(values, lane, index):
    return jnp.sum(jnp.where(lane == index, values, 0.0), axis=-1, keepdims=True)


def _first_argmax(vals, lane):
    m = jnp.max(vals, axis=-1, keepdims=True)
    idx = jnp.min(jnp.where(vals == m, lane, float(LANES)), axis=-1, keepdims=True)
    return m, idx


def _out_router_kernel(att_ref, y_ref, x_ref, g1_ref, wof_ref, nw_ref, sc_ref, sh_ref, wr_ref, br_ref,
                       ltri_ref, sut_ref, x1_ref, h2_ref, route_ref, routet_ref, tcnt_ref,
                       wr_split_ref, logits_ref, wo_ref):
    i = pl.program_id(0)

    @pl.when(i == 0)
    def _():
        hi, lo = _split_bf16(wr_ref[...])
        wr_split_ref[:, 0:LANES] = hi
        wr_split_ref[:, LANES:2 * LANES] = lo
        logits_ref[...] = jnp.zeros_like(logits_ref)
        for r0 in range(0, D_MODEL, 256):
            wo_ref[r0:r0 + 256, :] = wof_ref[r0:r0 + 256, :].astype(BF16)

    logits = logits_ref[...]

    mixer = (jnp.dot(att_ref[...], wo_ref[0:ATT_WIDTH, :], preferred_element_type=F32)
             + jnp.dot(y_ref[...], wo_ref[ATT_WIDTH:ATT_WIDTH + SSD_WIDTH, :], preferred_element_type=F32))
    x1 = x_ref[...] + g1_ref[0] * mixer
    x1_ref[...] = x1
    yn = x1 * lax.rsqrt(jnp.mean(x1 * x1, axis=-1, keepdims=True) + EPS)
    h2 = (yn * nw_ref[...]) * (1.0 + sc_ref[0]) + sh_ref[0]
    h2_ref[...] = h2.astype(BF16)

    h_hi, h_lo = _split_bf16(h2)
    both = jnp.dot(h_hi, wr_split_ref[...], preferred_element_type=F32)
    logits_ref[...] = (both[:, 0:LANES] + both[:, LANES:2 * LANES]
                       + jnp.dot(h_lo, wr_split_ref[:, 0:LANES], preferred_element_type=F32)) + br_ref[...]

    tm = logits.shape[0]
    lane = lax.broadcasted_iota(I32, (tm, LANES), 1).astype(F32)

    gl = jnp.where(lane < N_GROUPS, logits, NEG_BIG)
    gmax, gidx = _first_argmax(gl, lane)
    g_p = 1.0 / jnp.sum(jnp.exp(gl - gmax), axis=-1, keepdims=True)

    lo_lane = N_GROUPS + EXPERTS_PER_GROUP * gidx
    el = jnp.where((lane >= lo_lane) & (lane < lo_lane + EXPERTS_PER_GROUP), logits, NEG_BIG)
    m1, i1 = _first_argmax(el, lane)
    m2, i2 = _first_argmax(jnp.where(lane == i1, NEG_BIG, el), lane)
    r = jnp.exp(m2 - m1)
    p1 = 1.0 / (1.0 + r)
    p2 = r / (1.0 + r)
    e0 = i1 - N_GROUPS
    e1 = i2 - N_GROUPS

    onehot = ((lane == e0) | (lane == e1)).astype(F32)
    tile_cnt = jnp.sum(onehot, axis=0, keepdims=True)
    run_len = jnp.floor((tile_cnt + (RUN_ALIGN - 1)) * (1.0 / RUN_ALIGN)) * RUN_ALIGN
    run_start = jnp.dot(jnp.broadcast_to(run_len, (8, LANES)).astype(BF16), sut_ref[...],
                        preferred_element_type=F32)[0:1, :]
    before = jnp.dot(ltri_ref[...], onehot.astype(BF16), preferred_element_type=F32) + run_start
    slot0 = _lane_pick(before, lane, e0)
    slot1 = _lane_pick(before, lane, e1)
    tcnt_ref[0] = tile_cnt

    rec = jnp.zeros((tm, LANES), F32)
    for k, v in enumerate([slot0, slot1, g_p * p1, g_p * p2, e0, e1]):
        rec = jnp.where(lane == k, v, rec)
    route_ref[...] = rec[:, 0:ROUTE_W]
    routet_ref[...] = rec.T[0:ROUTE_W, :]


def _out_router(att, y, x2d, mod3, w_out_b, norm_w, w_router, b_router):
    tm = OUT_TM
    n_steps = TOKENS // tm
    steps_per_batch = SEQ // tm
    idx = jnp.arange(tm)
    ltri = (idx[None, :] < idx[:, None]).astype(BF16)
    lidx = jnp.arange(LANES)
    sut = (lidx[:, None] < lidx[None, :]).astype(BF16)
    const = lambda shape: pl.BlockSpec(shape, lambda i: (0, 0))
    cur = lambda i: jnp.minimum(i, n_steps - 1)
    prev = lambda i: jnp.maximum(i - 1, 0)
    tok = lambda width: pl.BlockSpec((tm, width), lambda i: (cur(i), 0))
    modspec = lambda k: pl.BlockSpec((1, 1, D_MODEL), lambda i: ((cur(i) // steps_per_batch) * 6 + k, 0, 0))
    return pl.pallas_call(
        _out_router_kernel,
        grid=(n_steps + 1,),
        in_specs=[tok(ATT_WIDTH), tok(SSD_WIDTH), tok(D_MODEL), modspec(2),
                  const((D_MODEL, D_MODEL)), const((1, D_MODEL)), modspec(4), modspec(3),
                  const((D_MODEL, LANES)), const((1, LANES)), const((tm, tm)), const((LANES, LANES))],
        out_specs=[tok(D_MODEL), tok(D_MODEL),
                   pl.BlockSpec((tm, ROUTE_W), lambda i: (prev(i), 0)),
                   pl.BlockSpec((ROUTE_W, tm), lambda i: (prev(i), 0)),
                   pl.BlockSpec((1, 1, LANES), lambda i: (prev(i), 0, 0))],
        out_shape=[jax.ShapeDtypeStruct((TOKENS, D_MODEL), F32),
                   jax.ShapeDtypeStruct((TOKENS, D_MODEL), BF16),
                   jax.ShapeDtypeStruct((TOKENS, ROUTE_W), F32),
                   jax.ShapeDtypeStruct((n_steps * ROUTE_W, tm), F32),
                   jax.ShapeDtypeStruct((n_steps, 1, LANES), F32)],
        scratch_shapes=[pltpu.VMEM((D_MODEL, 2 * LANES), BF16), pltpu.VMEM((tm, LANES), F32),
                        pltpu.VMEM((D_MODEL, D_MODEL), BF16)],
        compiler_params=_cparams(("arbitrary",)),
        name="out_router",
    )(att, y, x2d, mod3, w_out_b, norm_w.reshape(1, D_MODEL), mod3, mod3, w_router, b_router, ltri, sut)


MOE_TM = 512
ZERO_ROWS = 256
N_TOKEN_TILES = TOKENS // OUT_TM
MAX_PIECES = LOCAL_ROWS // RUN_ALIGN
MAX_SORTED_ROWS = TOKENS * TOP_K + N_TOKEN_TILES * N_EXPERTS * (RUN_ALIGN - 1)
N_TILES = MAX_SORTED_ROWS // MOE_TM + N_EXPERTS
N_ROWS = N_TILES * MOE_TM
assert MOE_TM % ZERO_ROWS == 0


SINK_ROWS = 2 * LOCAL_ROWS
XS_ROWS = N_ROWS + SINK_ROWS


def _dispatch_kernel(seg_end_ref, used_end_ref, dst_ref, routet_ref, h2_ref, xs_ref,
                     sbuf_ref, zero_ref, sems, zsem):
    i = pl.program_id(0)
    last = pl.num_programs(0) - 1
    buf = lax.rem(i, 2)

    def whole_buffer(b):
        return pltpu.make_async_copy(sbuf_ref.at[b], xs_ref.at[pl.ds(0, LOCAL_ROWS)], sems.at[b])

    def zero_fills(action):
        def tail_copy(row):
            return pltpu.make_async_copy(zero_ref.at[pl.ds(0, RUN_ALIGN)],
                                         xs_ref.at[pl.ds(pl.multiple_of(row, RUN_ALIGN), RUN_ALIGN)], zsem)

        def block_copy(block):
            start = pl.multiple_of(block * ZERO_ROWS, ZERO_ROWS)
            return pltpu.make_async_copy(zero_ref, xs_ref.at[pl.ds(start, ZERO_ROWS)], zsem)

        def tails(e, carry):
            def body(r, c):
                action(tail_copy(r * RUN_ALIGN))
                return c

            lax.fori_loop(used_end_ref[e] // RUN_ALIGN, seg_end_ref[e] // RUN_ALIGN, body, 0)
            return carry

        def blocks(block, carry):
            action(block_copy(block))
            return carry

        lax.fori_loop(0, N_EXPERTS, tails, 0)
        lax.fori_loop(seg_end_ref[N_EXPERTS - 1] // ZERO_ROWS, N_ROWS // ZERO_ROWS, blocks, 0)

    @pl.when(i == 0)
    def _():
        zero_ref[...] = jnp.zeros_like(zero_ref)
        sbuf_ref[1] = jnp.zeros((LOCAL_ROWS, D_MODEL), BF16)
        zero_fills(lambda cp: cp.start())
        sink = [pltpu.make_async_copy(zero_ref, xs_ref.at[pl.ds(N_ROWS + b * ZERO_ROWS, ZERO_ROWS)], sems.at[0])
                for b in range(SINK_ROWS // ZERO_ROWS)]
        for cp in sink:
            cp.start()
        for cp in sink:
            cp.wait()

    @pl.when(i > 0)
    def _():
        whole_buffer(buf).wait()

    slot = lax.broadcasted_iota(I32, (LOCAL_ROWS, OUT_TM), 0).astype(F32)
    perm = jnp.where((slot == routet_ref[0:1, :]) | (slot == routet_ref[1:2, :]), 1.0, 0.0).astype(BF16)
    sbuf_ref[buf] = jnp.dot(perm, h2_ref[...], preferred_element_type=F32).astype(BF16)

    for j in range(MAX_PIECES):
        pltpu.make_async_copy(
            sbuf_ref.at[1 - buf, pl.ds(j * RUN_ALIGN, RUN_ALIGN)],
            xs_ref.at[pl.ds(pl.multiple_of(dst_ref[0, 0, j], RUN_ALIGN), RUN_ALIGN)],
            sems.at[1 - buf]).start()

    @pl.when(i == last)
    def _():
        whole_buffer(1 - buf).wait()
        zero_fills(lambda cp: cp.wait())


def _piece_spec(index_map):
    return pl.BlockSpec((1, 1, MAX_PIECES), index_map, memory_space=pltpu.SMEM)


def _dispatch(seg_end, used_end, send_dst, route_t, h2):
    cur = lambda i: jnp.minimum(i, N_TOKEN_TILES - 1)
    grid_spec = pltpu.PrefetchScalarGridSpec(
        num_scalar_prefetch=2,
        grid=(N_TOKEN_TILES + 1,),
        in_specs=[_piece_spec(lambda i, se, ue: (i, 0, 0)),
                  pl.BlockSpec((ROUTE_W, OUT_TM), lambda i, se, ue: (cur(i), 0)),
                  pl.BlockSpec((OUT_TM, D_MODEL), lambda i, se, ue: (cur(i), 0))],
        out_specs=pl.BlockSpec(memory_space=pl.ANY),
        scratch_shapes=[pltpu.VMEM((2, LOCAL_ROWS, D_MODEL), BF16),
                        pltpu.VMEM((ZERO_ROWS, D_MODEL), BF16),
                        pltpu.SemaphoreType.DMA((2,)), pltpu.SemaphoreType.DMA],
    )
    return pl.pallas_call(
        _dispatch_kernel,
        grid_spec=grid_spec,
        out_shape=jax.ShapeDtypeStruct((XS_ROWS, D_MODEL), BF16),
        compiler_params=_cparams(("arbitrary",)),
        name="dispatch",
    )(seg_end, used_end, send_dst, route_t, h2)


def _experts_kernel(te_ref, seg_ref, nxt_ref, nu_ref, xs_ref, wg_hbm, wu_hbm, wd_hbm, ys_ref,
                    wg_buf, wu_buf, wd_buf, wgu_b_ref, wd_b_ref, wsem):
    i = pl.program_id(0)
    used = i < nu_ref[0]
    slot = lax.rem(seg_ref[i], 2)

    def weight_copies(expert, s):
        return [pltpu.make_async_copy(wg_hbm.at[expert], wg_buf.at[s], wsem.at[s]),
                pltpu.make_async_copy(wu_hbm.at[expert], wu_buf.at[s], wsem.at[s]),
                pltpu.make_async_copy(wd_hbm.at[expert], wd_buf.at[s], wsem.at[s])]

    @pl.when(i == 0)
    def _():
        for cp in weight_copies(te_ref[0], 0):
            cp.start()

    @pl.when(used & ((i == 0) | (te_ref[i] != te_ref[jnp.maximum(i - 1, 0)])))
    def _():
        for cp in weight_copies(te_ref[i], slot):
            cp.wait()

        @pl.when(nxt_ref[i] >= 0)
        def _():
            for cp in weight_copies(nxt_ref[i], 1 - slot):
                cp.start()

        wgu_b_ref[:, 0:D_EXPERT] = wg_buf[slot].astype(BF16)
        wgu_b_ref[:, D_EXPERT:2 * D_EXPERT] = wu_buf[slot].astype(BF16)
        wd_b_ref[...] = wd_buf[slot].astype(BF16)

    @pl.when(used)
    def _():
        h = jnp.dot(xs_ref[...], wgu_b_ref[...], preferred_element_type=F32)
        act = (_silu(h[:, 0:D_EXPERT]) * h[:, D_EXPERT:2 * D_EXPERT]).astype(BF16)
        ys_ref[...] = jnp.dot(act, wd_b_ref[...], preferred_element_type=F32).astype(BF16)


def _experts(tile_expert, tile_segment, next_expert, n_used, xs, w_gate, w_up, w_down):
    row_tile = lambda i, te, sg, nx, nu: (jnp.minimum(i, nu[0] - 1), 0)
    n_prefetch = 4
    grid_spec = pltpu.PrefetchScalarGridSpec(
        num_scalar_prefetch=n_prefetch,
        grid=(N_TILES,),
        in_specs=[pl.BlockSpec((MOE_TM, D_MODEL), row_tile),
                  pl.BlockSpec(memory_space=pl.ANY), pl.BlockSpec(memory_space=pl.ANY),
                  pl.BlockSpec(memory_space=pl.ANY)],
        out_specs=pl.BlockSpec((MOE_TM, D_MODEL), row_tile),
        scratch_shapes=[pltpu.VMEM((2, D_MODEL, D_EXPERT), F32), pltpu.VMEM((2, D_MODEL, D_EXPERT), F32),
                        pltpu.VMEM((2, D_EXPERT, D_MODEL), F32),
                        pltpu.VMEM((D_MODEL, 2 * D_EXPERT), BF16), pltpu.VMEM((D_EXPERT, D_MODEL), BF16),
                        pltpu.SemaphoreType.DMA((2,))],
    )
    return pl.pallas_call(
        _experts_kernel,
        grid_spec=grid_spec,
        out_shape=jax.ShapeDtypeStruct((XS_ROWS, D_MODEL), BF16),
        input_output_aliases={n_prefetch: 0},
        compiler_params=_cparams(("arbitrary",)),
        name="experts",
    )(tile_expert, tile_segment, next_expert, n_used, xs, w_gate, w_up, w_down)


def _combine_kernel(src_ref, src_next_ref, route_ref, x1_ref, g2_ref, ys_ref, o_ref, gbuf_ref, sems):
    i = pl.program_id(0)
    last = pl.num_programs(0) - 1
    buf = lax.rem(i, 2)

    def fetch(table_ref, b):
        for j in range(MAX_PIECES):
            pltpu.make_async_copy(
                ys_ref.at[pl.ds(pl.multiple_of(table_ref[0, 0, j], RUN_ALIGN), RUN_ALIGN)],
                gbuf_ref.at[b, pl.ds(j * RUN_ALIGN, RUN_ALIGN)], sems.at[b]).start()

    def whole_buffer(b):
        return pltpu.make_async_copy(ys_ref.at[pl.ds(0, LOCAL_ROWS)], gbuf_ref.at[b], sems.at[b])

    @pl.when(i == 0)
    def _():
        fetch(src_ref, 0)

    fetch(src_next_ref, 1 - buf)

    rec = route_ref[...]
    slot = lax.broadcasted_iota(I32, (OUT_TM, LOCAL_ROWS), 1).astype(F32)
    weights = (jnp.where(slot == rec[:, 0:1], rec[:, 2:3], 0.0)
               + jnp.where(slot == rec[:, 1:2], rec[:, 3:4], 0.0)).astype(BF16)
    whole_buffer(buf).wait()
    moe = jnp.dot(weights, gbuf_ref[buf], preferred_element_type=F32)
    o_ref[...] = x1_ref[...] + g2_ref[0] * moe

    @pl.when(i == last)
    def _():
        whole_buffer(1 - buf).wait()


def _combine(fetch_src, route, x1, mod3, ys):
    tm = OUT_TM
    steps_per_batch = SEQ // tm
    return pl.pallas_call(
        _combine_kernel,
        grid=(N_TOKEN_TILES,),
        in_specs=[_piece_spec(lambda i: (i, 0, 0)),
                  _piece_spec(lambda i: (jnp.minimum(i + 1, N_TOKEN_TILES - 1), 0, 0)),
                  pl.BlockSpec((tm, ROUTE_W), lambda i: (i, 0)),
                  pl.BlockSpec((tm, D_MODEL), lambda i: (i, 0)),
                  pl.BlockSpec((1, 1, D_MODEL), lambda i: ((i // steps_per_batch) * 6 + 5, 0, 0)),
                  pl.BlockSpec(memory_space=pl.ANY)],
        out_specs=pl.BlockSpec((tm, D_MODEL), lambda i: (i, 0)),
        out_shape=jax.ShapeDtypeStruct((TOKENS, D_MODEL), F32),
        scratch_shapes=[pltpu.VMEM((2, LOCAL_ROWS, D_MODEL), BF16), pltpu.SemaphoreType.DMA((2,))],
        compiler_params=_cparams(("arbitrary",)),
        name="combine",
    )(fetch_src, fetch_src, route, x1, mod3, ys)


def kernel(x, c, positions, norm1_w, norm2_w, w_ada, b_ada, w_in, conv_w, conv_b, dt_bias, a_log,
           d_skip, ssd_norm_w, q_norm_w, k_norm_w, sinks, w_out, w_group, b_group, w_expert, b_expert,
           w_gate, w_up, w_down):
    assert x.shape == (BATCH, SEQ, D_MODEL) and w_in.shape == (D_MODEL, IN_WIDTH)
    x2d = x.reshape(TOKENS, D_MODEL)
    mod = _ada_mod(c, w_ada, b_ada)
    mod3 = mod.reshape(BATCH * 6, 1, D_MODEL)

    qkv, z, xbc, dt, dt_t = _in_proj(x2d, norm1_w, mod3, w_in)
    att = _attention(qkv, positions, q_norm_w, k_norm_w, sinks)
    y = _ssd(xbc, z, dt, dt_t, conv_w, conv_b, dt_bias, a_log, d_skip, ssd_norm_w)

    w_router = jnp.pad(jnp.concatenate([w_group, w_expert], axis=1).astype(F32),
                       ((0, 0), (0, LANES - ROUTER_COLS)))
    b_router = jnp.pad(jnp.concatenate([b_group, b_expert]).astype(F32),
                       (0, LANES - ROUTER_COLS)).reshape(1, LANES)
    x1, h2, route, route_t, tcnt = _out_router(att, y, x2d, mod3, w_out.astype(F32), norm2_w,
                                                w_router, b_router)

    tc = tcnt[:, 0, 0:N_EXPERTS].astype(I32)
    run_rows = ((tc + RUN_ALIGN - 1) // RUN_ALIGN) * RUN_ALIGN
    counts = jnp.sum(run_rows, axis=0)
    padded = ((counts + MOE_TM - 1) // MOE_TM) * MOE_TM
    seg_end = jnp.cumsum(padded)
    seg_start = seg_end - padded
    run_dst = seg_start[None, :] + jnp.cumsum(run_rows, axis=0) - run_rows
    n_used = (seg_end[-1] // MOE_TM).reshape(1)
    last_row = jnp.minimum(jnp.arange(N_TILES, dtype=I32) * MOE_TM, seg_end[-1] - 1)
    tile_expert = jnp.sum((seg_end[None, :] <= last_row[:, None]).astype(I32), axis=1)

    run_pieces = run_rows // RUN_ALIGN
    piece_end = jnp.cumsum(run_pieces, axis=1)
    n_pieces = piece_end[:, -1]
    j = jnp.arange(MAX_PIECES, dtype=I32)
    piece_expert = jnp.sum((piece_end[:, None, :] <= j[None, :, None]).astype(I32), axis=2)
    in_run = (piece_expert[:, :, None] == jnp.arange(N_EXPERTS, dtype=I32)[None, None, :]).astype(I32)
    run_base = run_dst - RUN_ALIGN * (piece_end - run_pieces)
    piece_row = jnp.sum(in_run * run_base[:, None, :], axis=2) + RUN_ALIGN * j[None, :]
    in_use = j[None, :] < n_pieces[:, None]
    tile_par = (jnp.arange(N_TOKEN_TILES, dtype=I32) % 2)[:, None]
    sink_row = N_ROWS + tile_par * LOCAL_ROWS + RUN_ALIGN * j[None, :]
    send_dst = jnp.concatenate([N_ROWS + LOCAL_ROWS + RUN_ALIGN * j[None, :],
                                jnp.where(in_use, piece_row, sink_row)], axis=0)
    send_dst = send_dst.astype(I32).reshape(N_TOKEN_TILES + 1, 1, MAX_PIECES)
    fetch_src = jnp.where(in_use, piece_row, 0).astype(I32).reshape(N_TOKEN_TILES, 1, MAX_PIECES)

    experts = jnp.arange(N_EXPERTS, dtype=I32)
    nonempty = padded > 0
    seg_rank = jnp.cumsum(nonempty.astype(I32)) - 1
    later = nonempty[None, :] & (experts[None, :] > experts[:, None])
    next_of = jnp.min(jnp.where(later, experts[None, :], N_EXPERTS), axis=1)
    next_of = jnp.where(next_of == N_EXPERTS, -1, next_of)
    tile_is = (tile_expert[:, None] == experts[None, :]).astype(I32)
    tile_segment = jnp.sum(tile_is * seg_rank[None, :], axis=1)
    next_expert = jnp.sum(tile_is * next_of[None, :], axis=1)

    xs = _dispatch(seg_end.astype(I32), (seg_start + counts).astype(I32), send_dst, route_t, h2)
    ys = _experts(tile_expert, tile_segment.astype(I32), next_expert.astype(I32), n_used.astype(I32),
                  xs, w_gate, w_up, w_down)
    out = _combine(fetch_src, route, x1, mod3, ys)
    return out.reshape(BATCH, SEQ, D_MODEL)
```

```python
import jax
import jax.numpy as jnp
from jax import lax
from jax.experimental import pallas as pl
from jax.experimental.pallas import tpu as pltpu

F32 = jnp.float32
BF16 = jnp.bfloat16
I32 = jnp.int32

D_MODEL = 1024
BATCH = 2
SEQ = 8192
TOKENS = BATCH * SEQ
ATT_HEADS = 8
ATT_KV_HEADS = 2
HEAD_DIM = 64
ATT_WIDTH = ATT_HEADS * HEAD_DIM
KV_WIDTH = ATT_KV_HEADS * HEAD_DIM
ATT_BLOCK = 128
ROPE_DIM = HEAD_DIM // 4
ROPE_THETA = 500000.0
SSD_HEADS = 8
SSD_HEAD_DIM = 64
SSD_WIDTH = SSD_HEADS * SSD_HEAD_DIM
SSD_GROUPS = 2
SSD_STATE = 128
CONV_K = 4
CHUNK = 128
XBC_WIDTH = SSD_WIDTH + 2 * SSD_GROUPS * SSD_STATE
IN_WIDTH = ATT_WIDTH + 2 * KV_WIDTH + SSD_WIDTH + XBC_WIDTH + SSD_HEADS
N_GROUPS = 4
EXPERTS_PER_GROUP = 8
N_EXPERTS = N_GROUPS * EXPERTS_PER_GROUP
TOP_K = 2
D_EXPERT = 256
EPS = 1e-6

LANES = 128
QKV_WIDTH = ATT_WIDTH + 2 * KV_WIDTH
IN_PAD = QKV_WIDTH + SSD_WIDTH + XBC_WIDTH + LANES
NEG_BIG = -1e30

VMEM_LIMIT = 48 * 1024 * 1024


def _cparams(sem):
    return pltpu.CompilerParams(dimension_semantics=sem, vmem_limit_bytes=VMEM_LIMIT)


def _split_bf16(x):
    hi = x.astype(BF16)
    lo = (x - hi.astype(F32)).astype(BF16)
    return hi, lo


ADA_TN = 768


def _ada_kernel(ct_ref, w_ref, b_ref, o_ref):
    ct = ct_ref[...]
    s = ct * jax.nn.sigmoid(ct)
    w = w_ref[...]
    rows = [jnp.sum(s[:, b:b + 1] * w, axis=0, keepdims=True) for b in range(BATCH)]
    o_ref[...] = jnp.concatenate(rows, axis=0) + b_ref[...]


def _ada_mod(c, w_ada, b_ada):
    n = w_ada.shape[1]
    return pl.pallas_call(
        _ada_kernel,
        grid=(n // ADA_TN,),
        in_specs=[pl.BlockSpec((D_MODEL, BATCH), lambda j: (0, 0)),
                  pl.BlockSpec((D_MODEL, ADA_TN), lambda j: (0, j)),
                  pl.BlockSpec((1, ADA_TN), lambda j: (0, j))],
        out_specs=pl.BlockSpec((BATCH, ADA_TN), lambda j: (0, j)),
        out_shape=jax.ShapeDtypeStruct((BATCH, n), F32),
        compiler_params=_cparams(("arbitrary",)),
        name="ada_mod",
    )(c.T, w_ada, b_ada.reshape(1, n))


INPROJ_TM = 512
_INPROJ_CHUNK = 256


def _inproj_kernel(x_ref, nw_ref, sc_ref, sh_ref, wf_ref, wdt_ref, qkv_ref, z_ref, xbc_ref, dt_ref, dtt_ref,
                   w_ref):
    @pl.when(pl.program_id(0) == 0)
    def _():
        for c0 in range(0, IN_PAD - LANES, _INPROJ_CHUNK):
            w_ref[:, c0:c0 + _INPROJ_CHUNK] = wf_ref[:, c0:c0 + _INPROJ_CHUNK].astype(BF16)
        w_ref[:, IN_PAD - LANES:IN_PAD] = wdt_ref[...].astype(BF16)

    x = x_ref[...]
    y = x * lax.rsqrt(jnp.mean(x * x, axis=-1, keepdims=True) + EPS)
    h = (y * nw_ref[...]) * (1.0 + sc_ref[0]) + sh_ref[0]
    hb = h.astype(BF16)

    def proj(c0, c1):
        return jnp.dot(hb, w_ref[:, c0:c1], preferred_element_type=F32)

    for c0 in range(0, QKV_WIDTH, _INPROJ_CHUNK):
        qkv_ref[:, c0:c0 + _INPROJ_CHUNK] = proj(c0, c0 + _INPROJ_CHUNK).astype(BF16)
    base = QKV_WIDTH
    for c0 in range(0, SSD_WIDTH, _INPROJ_CHUNK):
        z_ref[:, c0:c0 + _INPROJ_CHUNK] = proj(base + c0, base + c0 + _INPROJ_CHUNK).astype(BF16)
    base += SSD_WIDTH
    for c0 in range(0, XBC_WIDTH, _INPROJ_CHUNK):
        xbc_ref[:, c0:c0 + _INPROJ_CHUNK] = proj(base + c0, base + c0 + _INPROJ_CHUNK).astype(BF16)
    base += XBC_WIDTH
    dt = proj(base, base + LANES)
    dt_ref[...] = dt
    dtt_ref[...] = dt.T[0:SSD_HEADS, :]


def _in_proj(x2d, norm_w, mod3, w_in):
    tm = INPROJ_TM
    steps_per_batch = SEQ // tm
    w_dt = jnp.pad(w_in[:, IN_WIDTH - SSD_HEADS:IN_WIDTH].astype(F32), ((0, 0), (0, LANES - SSD_HEADS)))
    return pl.pallas_call(
        _inproj_kernel,
        grid=(TOKENS // tm,),
        in_specs=[pl.BlockSpec((tm, D_MODEL), lambda i: (i, 0)),
                  pl.BlockSpec((1, D_MODEL), lambda i: (0, 0)),
                  pl.BlockSpec((1, 1, D_MODEL), lambda i: ((i // steps_per_batch) * 6 + 1, 0, 0)),
                  pl.BlockSpec((1, 1, D_MODEL), lambda i: ((i // steps_per_batch) * 6 + 0, 0, 0)),
                  pl.BlockSpec((D_MODEL, IN_WIDTH), lambda i: (0, 0), pipeline_mode=pl.Buffered(1)),
                  pl.BlockSpec((D_MODEL, LANES), lambda i: (0, 0))],
        out_specs=[pl.BlockSpec((tm, QKV_WIDTH), lambda i: (i, 0)),
                   pl.BlockSpec((tm, SSD_WIDTH), lambda i: (i, 0)),
                   pl.BlockSpec((tm, XBC_WIDTH), lambda i: (i, 0)),
                   pl.BlockSpec((tm, LANES), lambda i: (i, 0)),
                   pl.BlockSpec((SSD_HEADS, tm), lambda i: (0, i))],
        out_shape=[jax.ShapeDtypeStruct((TOKENS, QKV_WIDTH), BF16),
                   jax.ShapeDtypeStruct((TOKENS, SSD_WIDTH), BF16),
                   jax.ShapeDtypeStruct((TOKENS, XBC_WIDTH), BF16),
                   jax.ShapeDtypeStruct((TOKENS, LANES), F32),
                   jax.ShapeDtypeStruct((SSD_HEADS, TOKENS), F32)],
        scratch_shapes=[pltpu.VMEM((D_MODEL, IN_PAD), BF16)],
        compiler_params=_cparams(("arbitrary",)),
        name="in_proj",
    )(x2d, norm_w.reshape(1, D_MODEL), mod3, mod3, w_in.astype(F32), w_dt)


ATT_SUB = 2


ROPE_TM = 2048
_ROPE_HALF = ROPE_DIM // 2
_TOK_PER_ROW = LANES // _ROPE_HALF


def _exact_dot(x, onehot_b):
    hi, lo = _split_bf16(x)
    return (jnp.dot(hi, onehot_b, preferred_element_type=F32)
            + jnp.dot(lo, onehot_b, preferred_element_type=F32))


def _rope_kernel(pos_ref, freq_ref, sel_ref, own_ref, gcos_ref, gs1_ref, gs2_ref, ident_ref,
                 cos_ref, s1_ref, s2_ref):
    ang = pos_ref[...].astype(F32) * freq_ref[...]
    cos_p, sin_p = jnp.cos(ang), jnp.sin(ang)
    hi_c, lo_c = _split_bf16(cos_p)
    hi_s, lo_s = _split_bf16(sin_p)
    sel = sel_ref[...]
    rows_c = jnp.dot(sel, hi_c, preferred_element_type=F32) + jnp.dot(sel, lo_c, preferred_element_type=F32)
    rows_s = jnp.dot(sel, hi_s, preferred_element_type=F32) + jnp.dot(sel, lo_s, preferred_element_type=F32)
    own = own_ref[...]
    cos_ref[...] = _exact_dot(rows_c * own, gcos_ref[...]) + ident_ref[...]
    s1_ref[...] = _exact_dot(rows_s * own, gs1_ref[...])
    s2_ref[...] = _exact_dot(rows_s * own, gs2_ref[...])


def _rope_tables(positions):
    half, per_row = _ROPE_HALF, _TOK_PER_ROW
    rows = ROPE_TM // per_row
    pos_rep = jnp.repeat(positions.reshape(TOKENS).astype(I32), half).reshape(TOKENS // per_row, LANES)
    inv_freq = jnp.power(ROPE_THETA, -jnp.arange(half, dtype=F32) * 2.0 / ROPE_DIM)
    freq = jnp.tile(inv_freq, per_row).reshape(1, LANES)
    tok = jnp.arange(ROPE_TM)
    lane = jnp.arange(LANES)
    sel = (tok[:, None] // per_row == jnp.arange(rows)[None, :]).astype(BF16)
    own = (lane[None, :] // half == tok[:, None] % per_row).astype(F32)
    d = lane % HEAD_DIM
    src_f = lane % half
    hits = lambda lo, hi: ((src_f[:, None] == d[None, :] % half) & (d[None, :] >= lo) & (d[None, :] < hi))
    gcos = hits(0, ROPE_DIM).astype(BF16)
    gs1 = -hits(0, half).astype(BF16)
    gs2 = hits(half, ROPE_DIM).astype(BF16)
    ident = (d >= ROPE_DIM).astype(F32).reshape(1, LANES)
    const = lambda shape: pl.BlockSpec(shape, lambda i: (0, 0))
    out_spec = pl.BlockSpec((ROPE_TM, LANES), lambda i: (i, 0))
    out = jax.ShapeDtypeStruct((TOKENS, LANES), F32)
    return pl.pallas_call(
        _rope_kernel,
        grid=(TOKENS // ROPE_TM,),
        in_specs=[pl.BlockSpec((rows, LANES), lambda i: (i, 0)), const((1, LANES)),
                  const((ROPE_TM, rows)), const((ROPE_TM, LANES)),
                  const((LANES, LANES)), const((LANES, LANES)), const((LANES, LANES)), const((1, LANES))],
        out_specs=[out_spec, out_spec, out_spec],
        out_shape=[out, out, out],
        compiler_params=_cparams(("arbitrary",)),
        name="rope_tables",
    )(pos_rep, freq, sel, own, gcos, gs1, gs2, ident)


def _seg_meansq(xf, ones128):
    rows, width = xf.shape
    nt = width // LANES
    parts = _split_bf16(xf * xf)
    stacked = jnp.concatenate([p[:, t * LANES:(t + 1) * LANES] for p in parts for t in range(nt)], axis=0)
    tot = jnp.dot(stacked, ones128, preferred_element_type=F32)
    tiles = [tot[t * rows:(t + 1) * rows] + tot[(nt + t) * rows:(nt + t + 1) * rows] for t in range(nt)]
    return jnp.concatenate(tiles, axis=1) * (1.0 / HEAD_DIM)


def _norm_rope(x_bf, w_row, ones_bd, cosf, s1, s2):
    xf = x_bf.astype(F32)
    width = xf.shape[1]
    xn = xf * lax.rsqrt(_seg_meansq(xf, ones_bd) + EPS) * w_row
    half = ROPE_DIM // 2
    up = pltpu.roll(xn, width - half, axis=1)
    down = pltpu.roll(xn, half, axis=1)
    return xn * cosf + up * s1 + down * s2


def _attn_kernel(sink_ref, q_ref, kv_ref, cos_ref, s1_ref, s2_ref, qw_ref, kw_ref,
                 ones_ref, o_ref, kprev_ref, vprev_ref):
    j = pl.program_id(1)
    blk = ATT_BLOCK

    @pl.when(j == 0)
    def _():
        kprev_ref[...] = jnp.zeros_like(kprev_ref)
        vprev_ref[...] = jnp.zeros_like(vprev_ref)

    cos1 = cos_ref[...]
    s1_1 = s1_ref[...]
    s2_1 = s2_ref[...]
    reps = ATT_WIDTH // LANES
    cosq = jnp.concatenate([cos1] * reps, axis=1)
    s1q = jnp.concatenate([s1_1] * reps, axis=1)
    s2q = jnp.concatenate([s2_1] * reps, axis=1)

    q = _norm_rope(q_ref[...], qw_ref[...], ones_ref[...], cosq, s1q, s2q)
    qf = q * (HEAD_DIM ** -0.5)
    kv = kv_ref[...]
    kn = _norm_rope(kv[:, 0:KV_WIDTH], kw_ref[...], ones_ref[...], cos1, s1_1, s2_1)
    vn = kv[:, KV_WIDTH:2 * KV_WIDTH].astype(F32)

    kall = jnp.concatenate([kprev_ref[...], kn], axis=0)
    vall = jnp.concatenate([vprev_ref[...], vn], axis=0)
    kprev_ref[...] = kn[(ATT_SUB - 1) * blk:ATT_SUB * blk]
    vprev_ref[...] = vn[(ATT_SUB - 1) * blk:ATT_SUB * blk]

    lo_all = lax.broadcasted_iota(I32, kall.shape, 1) < HEAD_DIM
    ones_all = jnp.ones(kall.shape, BF16)

    row = lax.broadcasted_iota(I32, (2 * blk, blk), 0)
    col = lax.broadcasted_iota(I32, (2 * blk, blk), 1)
    from_prev = col > (row & (blk - 1))
    second_tile = lax.broadcasted_iota(I32, (2 * blk, 1), 0) >= blk
    zero_p = jnp.zeros((2 * blk, blk), F32)

    k_par, v_par = [], []
    for g in range(ATT_KV_HEADS):
        keep = lo_all if g == 0 else ~lo_all
        k_own = jnp.where(keep, kall, 0.0)
        v_own = jnp.where(keep, vall, 0.0)
        k_oth = pltpu.roll(k_own, HEAD_DIM, axis=1)
        v_oth = pltpu.roll(v_own, HEAD_DIM, axis=1)
        k_lo, k_hi = (k_own, k_oth) if g == 0 else (k_oth, k_own)
        v_lo, v_hi = (v_own, v_oth) if g == 0 else (v_oth, v_own)
        k_par.append((k_lo.astype(BF16), k_hi.astype(BF16)))
        v_par.append((jnp.concatenate([v_lo.astype(BF16), ones_all], axis=1),
                      jnp.concatenate([v_hi.astype(BF16), ones_all], axis=1)))

    problems = [(g, sub) for g in range(ATT_KV_HEADS) for sub in range(ATT_SUB)]
    scores = []
    for g, sub in problems:
        r0, c0 = sub * blk, g * 2 * LANES
        qcat = jnp.concatenate([qf[r0:r0 + blk, c0:c0 + LANES],
                                qf[r0:r0 + blk, c0 + LANES:c0 + 2 * LANES]], axis=0).astype(BF16)
        kw = jnp.concatenate([k_par[g][0][r0:r0 + 2 * blk], k_par[g][1][r0:r0 + 2 * blk]], axis=0)
        scores.append(lax.dot_general(qcat, kw, (((1,), (1,)), ((), ())),
                                      preferred_element_type=F32))

    weights, rescale = [], []
    for (g, sub), s_all in zip(problems, scores):
        for par in range(2):
            s = s_all[:, par * 2 * blk:(par + 1) * 2 * blk]
            s_prev = s[:, 0:blk]
            if sub == 0:
                s_prev = s_prev + jnp.where(j > 0, 0.0, NEG_BIG)
            s = jnp.where(from_prev, s_prev, s[:, blk:2 * blk])
            h_first = ATT_HEADS // ATT_KV_HEADS * g + par
            sink = jnp.where(second_tile, sink_ref[h_first + 2], sink_ref[h_first])
            m = jnp.maximum(jnp.max(s, axis=-1, keepdims=True), sink)
            p = jnp.exp(s - m)
            weights.append(jnp.concatenate([jnp.where(from_prev, p, zero_p), jnp.where(from_prev, zero_p, p)],
                                           axis=1).astype(BF16))
            rescale.append(jnp.exp(sink - m))

    outs = []
    for idx, (g, sub) in enumerate(problems):
        for par in range(2):
            outs.append(jnp.dot(weights[2 * idx + par], v_par[g][par][sub * blk:(sub + 2) * blk],
                                preferred_element_type=F32))

    for idx, (g, sub) in enumerate(problems):
        r0, c0 = sub * blk, g * 2 * LANES
        pair = None
        for par in range(2):
            o = outs[2 * idx + par]
            part = o[:, 0:LANES] * (1.0 / (o[:, LANES:2 * LANES] + rescale[2 * idx + par]))
            pair = part if pair is None else pair + part
        o_ref[r0:r0 + blk, c0:c0 + LANES] = pair[0:blk].astype(BF16)
        o_ref[r0:r0 + blk, c0 + LANES:c0 + 2 * LANES] = pair[blk:2 * blk].astype(BF16)


def _attention(qkv, positions, q_norm_w, k_norm_w, sinks):
    cosf, s1, s2 = _rope_tables(positions)
    qw = jnp.tile(q_norm_w.astype(F32), ATT_HEADS).reshape(1, ATT_WIDTH)
    kw = jnp.tile(k_norm_w.astype(F32), ATT_KV_HEADS).reshape(1, KV_WIDTH)
    seg = jnp.arange(LANES) // HEAD_DIM
    ones128 = (seg[:, None] == seg[None, :]).astype(BF16)
    const = lambda shape: pl.BlockSpec(shape, lambda b, j, s: (0, 0))
    rows = ATT_SUB * ATT_BLOCK
    nb = SEQ // rows
    tok = lambda width, cb: pl.BlockSpec((rows, width), lambda b, j, s: (b * nb + j, cb))
    grid_spec = pltpu.PrefetchScalarGridSpec(
        num_scalar_prefetch=1,
        grid=(BATCH, nb),
        in_specs=[tok(ATT_WIDTH, 0), tok(2 * KV_WIDTH, 2), tok(LANES, 0), tok(LANES, 0), tok(LANES, 0),
                  const((1, ATT_WIDTH)), const((1, KV_WIDTH)), const((LANES, LANES))],
        out_specs=tok(ATT_WIDTH, 0),
        scratch_shapes=[pltpu.VMEM((ATT_BLOCK, KV_WIDTH), F32),
                        pltpu.VMEM((ATT_BLOCK, KV_WIDTH), F32)],
    )
    return pl.pallas_call(
        _attn_kernel,
        grid_spec=grid_spec,
        out_shape=jax.ShapeDtypeStruct((TOKENS, ATT_WIDTH), BF16),
        compiler_params=_cparams(("arbitrary", "arbitrary")),
        name="attention",
    )(sinks.astype(F32), qkv, qkv, cosf, s1, s2, qw, kw, ones128)


def _softplus(x):
    return jnp.maximum(x, 0.0) + jnp.log1p(jnp.exp(-jnp.abs(x)))


def _silu(x):
    return x * jax.nn.sigmoid(x)


def _ssd_kernel(xbc_ref, z_ref, dt_ref, dtt_ref, cw_ref, cb_ref, dtb_row_ref, dtb_col_ref,
                alog_row_ref, alog_col_ref, dskip_ref, nw_ref, tril_ref, triu_ref,
                o_ref, conv_ref, state_ref):
    c = pl.program_id(1)
    L = CHUNK
    tail = 8

    @pl.when(c == 0)
    def _():
        conv_ref[0:tail, :] = jnp.zeros((tail, XBC_WIDTH), F32)
        state_ref[...] = jnp.zeros_like(state_ref)

    xb = xbc_ref[...].astype(F32)
    conv_ref[tail:tail + L, :] = xb
    acc = cb_ref[...] + cw_ref[CONV_K - 1:CONV_K, :] * xb
    for k in range(CONV_K - 1):
        off = tail - (CONV_K - 1) + k
        acc = acc + cw_ref[k:k + 1, :] * conv_ref[off:off + L, :]
    conv_ref[0:tail, :] = xb[L - tail:L, :]
    u = _silu(acc)
    xs = u[:, 0:SSD_WIDTH]
    bmat = u[:, SSD_WIDTH:SSD_WIDTH + SSD_GROUPS * SSD_STATE]
    cmat = u[:, SSD_WIDTH + SSD_GROUPS * SSD_STATE:XBC_WIDTH]

    dt = _softplus(dt_ref[...] + dtb_row_ref[...])
    a = dt * (-jnp.exp(alog_row_ref[...]))
    a_hi, a_lo = _split_bf16(a)
    a_cum = (jnp.dot(tril_ref[...], a_hi, preferred_element_type=F32)
             + jnp.dot(tril_ref[...], a_lo, preferred_element_type=F32))
    dt_t = _softplus(dtt_ref[...] + dtb_col_ref[...])
    a_t = dt_t * (-jnp.exp(alog_col_ref[...]))
    at_hi, at_lo = _split_bf16(a_t)
    a_cum_t = (jnp.dot(at_hi, triu_ref[...], preferred_element_type=F32)
               + jnp.dot(at_lo, triu_ref[...], preferred_element_type=F32))
    a_end_t = a_cum_t[:, L - 1:L]
    wst_t = jnp.exp(a_end_t - a_cum_t) * dt_t
    cdec_t = jnp.exp(a_end_t)

    row = lax.broadcasted_iota(I32, (L, L), 0)
    col = lax.broadcasted_iota(I32, (L, L), 1)
    causal = col <= row
    lane = lax.broadcasted_iota(I32, (L, LANES), 1)
    lo_half = lane < SSD_HEAD_DIM

    xs_b = xs.astype(BF16)
    heads_per_group = SSD_HEADS // SSD_GROUPS
    gated = []
    for g in range(SSD_GROUPS):
        b_g = bmat[:, g * SSD_STATE:(g + 1) * SSD_STATE]
        c_g = cmat[:, g * SSD_STATE:(g + 1) * SSD_STATE]
        cb = lax.dot_general(c_g.astype(BF16), b_g.astype(BF16), (((1,), (1,)), ((), ())),
                             preferred_element_type=F32)
        b_gt = b_g.T
        for t in range(heads_per_group // 2):
            tile = g * (heads_per_group // 2) + t
            c0 = tile * LANES
            xs_tile = xs_b[:, c0:c0 + LANES]
            st_tile = state_ref[:, c0:c0 + LANES]
            st_b = st_tile.astype(BF16)
            y_tile = jnp.zeros((L, LANES), F32)
            new_tile = jnp.zeros((SSD_STATE, LANES), F32)
            for e in range(2):
                h = 2 * tile + e
                keep = lo_half if e == 0 else ~lo_half
                colb = jnp.broadcast_to(a_cum[:, h:h + 1], (L, L))
                rowb = a_cum_t[h:h + 1, :]
                decay = jnp.exp(jnp.where(causal, colb - rowb, NEG_BIG))
                w_in = (cb * decay) * dt_t[h:h + 1, :]
                w_off = c_g * jnp.exp(colb)
                lhs = jnp.concatenate([w_in, w_off], axis=1).astype(BF16)
                rhs = jnp.concatenate([jnp.where(keep, xs_tile, jnp.zeros_like(xs_tile)),
                                       jnp.where(keep, st_b, jnp.zeros_like(st_b))], axis=0)
                y_tile = y_tile + jnp.dot(lhs, rhs, preferred_element_type=F32)
                m_h = (b_gt * wst_t[h:h + 1, :]).astype(BF16)
                new_tile = new_tile + jnp.dot(m_h, jnp.where(keep, xs_tile, jnp.zeros_like(xs_tile)),
                                              preferred_element_type=F32)
            cd = jnp.where(lo_half[0:1, :], cdec_t[2 * tile:2 * tile + 1, :],
                           cdec_t[2 * tile + 1:2 * tile + 2, :])
            state_ref[:, c0:c0 + LANES] = st_tile * cd + new_tile
            y_full = y_tile + dskip_ref[:, c0:c0 + LANES] * xs[:, c0:c0 + LANES]
            gated.append(y_full * _silu(z_ref[:, c0:c0 + LANES].astype(F32)))

    gw = SSD_WIDTH // SSD_GROUPS
    tiles_per_group = gw // LANES
    for g in range(SSD_GROUPS):
        yg = jnp.concatenate(gated[g * tiles_per_group:(g + 1) * tiles_per_group], axis=1)
        ms = jnp.mean(yg * yg, axis=-1, keepdims=True)
        o_ref[:, g * gw:(g + 1) * gw] = ((yg * lax.rsqrt(ms + EPS)) * nw_ref[:, g * gw:(g + 1) * gw]).astype(o_ref.dtype)


def _ssd(xbc, z, dt, dt_t, conv_w, conv_b, dt_bias, a_log, d_skip, ssd_norm_w):
    nc = SEQ // CHUNK
    L = CHUNK
    pad_row = lambda v: jnp.pad(v.astype(F32), (0, LANES - SSD_HEADS)).reshape(1, LANES)
    col8 = lambda v: v.astype(F32).reshape(SSD_HEADS, 1)
    idx = jnp.arange(L)
    tril = (idx[None, :] <= idx[:, None]).astype(BF16)
    triu = (idx[:, None] <= idx[None, :]).astype(BF16)
    dskip = jnp.repeat(d_skip.astype(F32), SSD_HEAD_DIM).reshape(1, SSD_WIDTH)
    const = lambda shape: pl.BlockSpec(shape, lambda b, c: (0, 0))
    tok = lambda width: pl.BlockSpec((L, width), lambda b, c: (b * nc + c, 0))
    return pl.pallas_call(
        _ssd_kernel,
        grid=(BATCH, nc),
        in_specs=[tok(XBC_WIDTH), tok(SSD_WIDTH), tok(LANES),
                  pl.BlockSpec((SSD_HEADS, L), lambda b, c: (0, b * nc + c)),
                  const((CONV_K, XBC_WIDTH)), const((1, XBC_WIDTH)),
                  const((1, LANES)), const((SSD_HEADS, 1)), const((1, LANES)), const((SSD_HEADS, 1)),
                  const((1, SSD_WIDTH)), const((1, SSD_WIDTH)), const((L, L)), const((L, L))],
        out_specs=tok(SSD_WIDTH),
        out_shape=jax.ShapeDtypeStruct((TOKENS, SSD_WIDTH), BF16),
        scratch_shapes=[pltpu.VMEM((8 + L, XBC_WIDTH), F32),
                        pltpu.VMEM((SSD_STATE, SSD_WIDTH), F32)],
        compiler_params=_cparams(("arbitrary", "arbitrary")),
        name="ssd",
    )(xbc, z, dt, dt_t, conv_w.astype(F32), conv_b.astype(F32).reshape(1, XBC_WIDTH),
      pad_row(dt_bias), col8(dt_bias), pad_row(a_log), col8(a_log), dskip,
      ssd_norm_w.astype(F32).reshape(1, SSD_WIDTH), tril, triu)


OUT_TM = 256
ROUTE_W = 8
ROUTER_COLS = N_GROUPS + N_EXPERTS
RUN_ALIGN = 16
RUN_SHIFT = 4
LOCAL_ROWS = 1024
assert RUN_ALIGN == 1 << RUN_SHIFT and LOCAL_ROWS >= TOP_K * OUT_TM + N_EXPERTS * (RUN_ALIGN - 1)


def _lane_pick(values, lane, index):
    return jnp.sum(jnp.where(lane == index, values, 0.0), axis=-1, keepdims=True)


def _first_argmax(vals, lane):
    m = jnp.max(vals, axis=-1, keepdims=True)
    idx = jnp.min(jnp.where(vals == m, lane, float(LANES)), axis=-1, keepdims=True)
    return m, idx


def _out_router_kernel(att_ref, y_ref, x_ref, g1_ref, wof_ref, nw_ref, sc_ref, sh_ref, wr_ref, br_ref,
                       ltri_ref, sut_ref, x1_ref, h2_ref, route_ref, routet_ref, tcnt_ref,
                       wr_split_ref, logits_ref, wo_ref):
    i = pl.program_id(0)

    @pl.when(i == 0)
    def _():
        hi, lo = _split_bf16(wr_ref[...])
        wr_split_ref[:, 0:LANES] = hi
        wr_split_ref[:, LANES:2 * LANES] = lo
        logits_ref[...] = jnp.zeros_like(logits_ref)
        for r0 in range(0, D_MODEL, 256):
            wo_ref[r0:r0 + 256, :] = wof_ref[r0:r0 + 256, :].astype(BF16)

    logits = logits_ref[...]

    mixer = (jnp.dot(att_ref[...], wo_ref[0:ATT_WIDTH, :], preferred_element_type=F32)
             + jnp.dot(y_ref[...], wo_ref[ATT_WIDTH:ATT_WIDTH + SSD_WIDTH, :], preferred_element_type=F32))
    x1 = x_ref[...] + g1_ref[0] * mixer
    x1_ref[...] = x1
    yn = x1 * lax.rsqrt(jnp.mean(x1 * x1, axis=-1, keepdims=True) + EPS)
    h2 = (yn * nw_ref[...]) * (1.0 + sc_ref[0]) + sh_ref[0]
    h2_ref[...] = h2.astype(BF16)

    h_hi, h_lo = _split_bf16(h2)
    both = jnp.dot(h_hi, wr_split_ref[...], preferred_element_type=F32)
    logits_ref[...] = (both[:, 0:LANES] + both[:, LANES:2 * LANES]
                       + jnp.dot(h_lo, wr_split_ref[:, 0:LANES], preferred_element_type=F32)) + br_ref[...]

    tm = logits.shape[0]
    lane = lax.broadcasted_iota(I32, (tm, LANES), 1).astype(F32)

    gl = jnp.where(lane < N_GROUPS, logits, NEG_BIG)
    gmax, gidx = _first_argmax(gl, lane)
    g_p = 1.0 / jnp.sum(jnp.exp(gl - gmax), axis=-1, keepdims=True)

    lo_lane = N_GROUPS + EXPERTS_PER_GROUP * gidx
    el = jnp.where((lane >= lo_lane) & (lane < lo_lane + EXPERTS_PER_GROUP), logits, NEG_BIG)
    m1, i1 = _first_argmax(el, lane)
    m2, i2 = _first_argmax(jnp.where(lane == i1, NEG_BIG, el), lane)
    r = jnp.exp(m2 - m1)
    p1 = 1.0 / (1.0 + r)
    p2 = r / (1.0 + r)
    e0 = i1 - N_GROUPS
    e1 = i2 - N_GROUPS

    onehot = ((lane == e0) | (lane == e1)).astype(F32)
    tile_cnt = jnp.sum(onehot, axis=0, keepdims=True)
    run_len = jnp.floor((tile_cnt + (RUN_ALIGN - 1)) * (1.0 / RUN_ALIGN)) * RUN_ALIGN
    run_start = jnp.dot(jnp.broadcast_to(run_len, (8, LANES)).astype(BF16), sut_ref[...],
                        preferred_element_type=F32)[0:1, :]
    before = jnp.dot(ltri_ref[...], onehot.astype(BF16), preferred_element_type=F32) + run_start
    slot0 = _lane_pick(before, lane, e0)
    slot1 = _lane_pick(before, lane, e1)
    tcnt_ref[0] = tile_cnt

    rec = jnp.zeros((tm, LANES), F32)
    for k, v in enumerate([slot0, slot1, g_p * p1, g_p * p2, e0, e1]):
        rec = jnp.where(lane == k, v, rec)
    route_ref[...] = rec[:, 0:ROUTE_W]
    routet_ref[...] = rec.T[0:ROUTE_W, :]


def _out_router(att, y, x2d, mod3, w_out_b, norm_w, w_router, b_router):
    tm = OUT_TM
    n_steps = TOKENS // tm
    steps_per_batch = SEQ // tm
    idx = jnp.arange(tm)
    ltri = (idx[None, :] < idx[:, None]).astype(BF16)
    lidx = jnp.arange(LANES)
    sut = (lidx[:, None] < lidx[None, :]).astype(BF16)
    const = lambda shape: pl.BlockSpec(shape, lambda i: (0, 0))
    cur = lambda i: jnp.minimum(i, n_steps - 1)
    prev = lambda i: jnp.maximum(i - 1, 0)
    tok = lambda width: pl.BlockSpec((tm, width), lambda i: (cur(i), 0))
    modspec = lambda k: pl.BlockSpec((1, 1, D_MODEL), lambda i: ((cur(i) // steps_per_batch) * 6 + k, 0, 0))
    return pl.pallas_call(
        _out_router_kernel,
        grid=(n_steps + 1,),
        in_specs=[tok(ATT_WIDTH), tok(SSD_WIDTH), tok(D_MODEL), modspec(2),
                  const((D_MODEL, D_MODEL)), const((1, D_MODEL)), modspec(4), modspec(3),
                  const((D_MODEL, LANES)), const((1, LANES)), const((tm, tm)), const((LANES, LANES))],
        out_specs=[tok(D_MODEL), tok(D_MODEL),
                   pl.BlockSpec((tm, ROUTE_W), lambda i: (prev(i), 0)),
                   pl.BlockSpec((ROUTE_W, tm), lambda i: (prev(i), 0)),
                   pl.BlockSpec((1, 1, LANES), lambda i: (prev(i), 0, 0))],
        out_shape=[jax.ShapeDtypeStruct((TOKENS, D_MODEL), F32),
                   jax.ShapeDtypeStruct((TOKENS, D_MODEL), BF16),
                   jax.ShapeDtypeStruct((TOKENS, ROUTE_W), F32),
                   jax.ShapeDtypeStruct((n_steps * ROUTE_W, tm), F32),
                   jax.ShapeDtypeStruct((n_steps, 1, LANES), F32)],
        scratch_shapes=[pltpu.VMEM((D_MODEL, 2 * LANES), BF16), pltpu.VMEM((tm, LANES), F32),
                        pltpu.VMEM((D_MODEL, D_MODEL), BF16)],
        compiler_params=_cparams(("arbitrary",)),
        name="out_router",
    )(att, y, x2d, mod3, w_out_b, norm_w.reshape(1, D_MODEL), mod3, mod3, w_router, b_router, ltri, sut)


MOE_TM = 512
ZERO_ROWS = 256
N_TOKEN_TILES = TOKENS // OUT_TM
MAX_SORTED_ROWS = TOKENS * TOP_K + N_TOKEN_TILES * N_EXPERTS * (RUN_ALIGN - 1)
N_TILES = MAX_SORTED_ROWS // MOE_TM + N_EXPERTS
N_ROWS = N_TILES * MOE_TM
assert MOE_TM % ZERO_ROWS == 0


BIG_PIECE = 2 * RUN_ALIGN
PIECE_SLOTS = LOCAL_ROWS // BIG_PIECE
assert PIECE_SLOTS >= N_EXPERTS and 4 * PIECE_SLOTS == LANES


def _run_copies(table_ref, n_big, n_small, make_copy, action):
    def big(q, carry):
        action(make_copy(table_ref[0, 0, q], table_ref[0, 0, PIECE_SLOTS + q], BIG_PIECE))
        return carry

    def small(q, carry):
        action(make_copy(table_ref[0, 0, 2 * PIECE_SLOTS + q], table_ref[0, 0, 3 * PIECE_SLOTS + q], RUN_ALIGN))
        return carry

    lax.fori_loop(0, n_big, big, 0)
    lax.fori_loop(0, n_small, small, 0)


def _dispatch_kernel(seg_end_ref, used_end_ref, nb_ref, ns_ref, tab_ref, routet_ref, h2_ref, xs_ref,
                     sbuf_ref, zero_ref, sems, zsem):
    i = pl.program_id(0)
    last = pl.num_programs(0) - 1
    buf = lax.rem(i, 2)

    def zero_fills(action):
        def tail_copy(row):
            return pltpu.make_async_copy(zero_ref.at[pl.ds(0, RUN_ALIGN)],
                                         xs_ref.at[pl.ds(pl.multiple_of(row, RUN_ALIGN), RUN_ALIGN)], zsem)

        def block_copy(block):
            start = pl.multiple_of(block * ZERO_ROWS, ZERO_ROWS)
            return pltpu.make_async_copy(zero_ref, xs_ref.at[pl.ds(start, ZERO_ROWS)], zsem)

        def tails(e, carry):
            def body(r, c):
                action(tail_copy(r * RUN_ALIGN))
                return c

            lax.fori_loop(used_end_ref[e] // RUN_ALIGN, seg_end_ref[e] // RUN_ALIGN, body, 0)
            return carry

        def blocks(block, carry):
            action(block_copy(block))
            return carry

        lax.fori_loop(0, N_EXPERTS, tails, 0)
        lax.fori_loop(seg_end_ref[N_EXPERTS - 1] // ZERO_ROWS, N_ROWS // ZERO_ROWS, blocks, 0)

    @pl.when(i == 0)
    def _():
        zero_ref[...] = jnp.zeros_like(zero_ref)
        zero_fills(lambda cp: cp.start())

    slot = lax.broadcasted_iota(I32, (LOCAL_ROWS, OUT_TM), 0).astype(F32)
    perm = jnp.where((slot == routet_ref[0:1, :]) | (slot == routet_ref[1:2, :]), 1.0, 0.0).astype(BF16)
    sbuf_ref[buf] = jnp.dot(perm, h2_ref[...], preferred_element_type=F32).astype(BF16)

    def piece(b):
        def make(local, sorted_row, rows):
            return pltpu.make_async_copy(
                sbuf_ref.at[b, pl.ds(pl.multiple_of(local, RUN_ALIGN), rows)],
                xs_ref.at[pl.ds(pl.multiple_of(sorted_row, RUN_ALIGN), rows)], sems.at[b])
        return make

    _run_copies(tab_ref, nb_ref[i], ns_ref[i], piece(buf), lambda cp: cp.start())
    prev = jnp.maximum(i - 1, 0)

    @pl.when(i > 0)
    def _():
        _run_copies(tab_ref, nb_ref[prev], ns_ref[prev], lambda lo, so, rows: piece(1 - buf)(0, 0, rows),
                    lambda cp: cp.wait())

    @pl.when(i == last)
    def _():
        _run_copies(tab_ref, nb_ref[i], ns_ref[i], lambda lo, so, rows: piece(buf)(0, 0, rows),
                    lambda cp: cp.wait())
        zero_fills(lambda cp: cp.wait())


def _piece_spec(index_map):
    return pl.BlockSpec((1, 1, LANES), index_map, memory_space=pltpu.SMEM)


def _dispatch(seg_end, used_end, n_big, n_small, piece_table, route_t, h2):
    grid_spec = pltpu.PrefetchScalarGridSpec(
        num_scalar_prefetch=4,
        grid=(N_TOKEN_TILES,),
        in_specs=[_piece_spec(lambda i, se, ue, nb, ns: (i, 0, 0)),
                  pl.BlockSpec((ROUTE_W, OUT_TM), lambda i, se, ue, nb, ns: (i, 0)),
                  pl.BlockSpec((OUT_TM, D_MODEL), lambda i, se, ue, nb, ns: (i, 0))],
        out_specs=pl.BlockSpec(memory_space=pl.ANY),
        scratch_shapes=[pltpu.VMEM((2, LOCAL_ROWS, D_MODEL), BF16),
                        pltpu.VMEM((ZERO_ROWS, D_MODEL), BF16),
                        pltpu.SemaphoreType.DMA((2,)), pltpu.SemaphoreType.DMA],
    )
    return pl.pallas_call(
        _dispatch_kernel,
        grid_spec=grid_spec,
        out_shape=jax.ShapeDtypeStruct((N_ROWS, D_MODEL), BF16),
        compiler_params=_cparams(("arbitrary",)),
        name="dispatch",
    )(seg_end, used_end, n_big, n_small, piece_table, route_t, h2)


def _experts_kernel(te_ref, seg_ref, nxt_ref, nu_ref, xs_ref, wg_hbm, wu_hbm, wd_hbm, ys_ref,
                    wg_buf, wu_buf, wd_buf, wgu_b_ref, wd_b_ref, wsem):
    i = pl.program_id(0)
    used = i < nu_ref[0]
    slot = lax.rem(seg_ref[i], 2)

    def weight_copies(expert, s):
        return [pltpu.make_async_copy(wg_hbm.at[expert], wg_buf.at[s], wsem.at[s]),
                pltpu.make_async_copy(wu_hbm.at[expert], wu_buf.at[s], wsem.at[s]),
                pltpu.make_async_copy(wd_hbm.at[expert], wd_buf.at[s], wsem.at[s])]

    @pl.when(i == 0)
    def _():
        for cp in weight_copies(te_ref[0], 0):
            cp.start()

    @pl.when(used & ((i == 0) | (te_ref[i] != te_ref[jnp.maximum(i - 1, 0)])))
    def _():
        for cp in weight_copies(te_ref[i], slot):
            cp.wait()

        @pl.when(nxt_ref[i] >= 0)
        def _():
            for cp in weight_copies(nxt_ref[i], 1 - slot):
                cp.start()

        wgu_b_ref[:, 0:D_EXPERT] = wg_buf[slot].astype(BF16)
        wgu_b_ref[:, D_EXPERT:2 * D_EXPERT] = wu_buf[slot].astype(BF16)
        wd_b_ref[...] = wd_buf[slot].astype(BF16)

    @pl.when(used)
    def _():
        h = jnp.dot(xs_ref[...], wgu_b_ref[...], preferred_element_type=F32)
        act = (_silu(h[:, 0:D_EXPERT]) * h[:, D_EXPERT:2 * D_EXPERT]).astype(BF16)
        ys_ref[...] = jnp.dot(act, wd_b_ref[...], preferred_element_type=F32).astype(BF16)


def _experts(tile_expert, tile_segment, next_expert, n_used, xs, w_gate, w_up, w_down):
    row_tile = lambda i, te, sg, nx, nu: (jnp.minimum(i, nu[0] - 1), 0)
    n_prefetch = 4
    grid_spec = pltpu.PrefetchScalarGridSpec(
        num_scalar_prefetch=n_prefetch,
        grid=(N_TILES,),
        in_specs=[pl.BlockSpec((MOE_TM, D_MODEL), row_tile),
                  pl.BlockSpec(memory_space=pl.ANY), pl.BlockSpec(memory_space=pl.ANY),
                  pl.BlockSpec(memory_space=pl.ANY)],
        out_specs=pl.BlockSpec((MOE_TM, D_MODEL), row_tile),
        scratch_shapes=[pltpu.VMEM((2, D_MODEL, D_EXPERT), F32), pltpu.VMEM((2, D_MODEL, D_EXPERT), F32),
                        pltpu.VMEM((2, D_EXPERT, D_MODEL), F32),
                        pltpu.VMEM((D_MODEL, 2 * D_EXPERT), BF16), pltpu.VMEM((D_EXPERT, D_MODEL), BF16),
                        pltpu.SemaphoreType.DMA((2,))],
    )
    return pl.pallas_call(
        _experts_kernel,
        grid_spec=grid_spec,
        out_shape=jax.ShapeDtypeStruct((N_ROWS, D_MODEL), BF16),
        input_output_aliases={n_prefetch: 0},
        compiler_params=_cparams(("arbitrary",)),
        name="experts",
    )(tile_expert, tile_segment, next_expert, n_used, xs, w_gate, w_up, w_down)


def _combine_kernel(nb_ref, ns_ref, tab_ref, tab_next_ref, route_ref, x1_ref, g2_ref, ys_ref, o_ref,
                    gbuf_ref, sems):
    i = pl.program_id(0)
    last = pl.num_programs(0) - 1
    buf = lax.rem(i, 2)

    def piece(b):
        def make(local, sorted_row, rows):
            return pltpu.make_async_copy(
                ys_ref.at[pl.ds(pl.multiple_of(sorted_row, RUN_ALIGN), rows)],
                gbuf_ref.at[b, pl.ds(pl.multiple_of(local, RUN_ALIGN), rows)], sems.at[b])
        return make

    @pl.when(i == 0)
    def _():
        gbuf_ref[...] = jnp.zeros_like(gbuf_ref)
        _run_copies(tab_ref, nb_ref[0], ns_ref[0], piece(0), lambda cp: cp.start())

    nxt = jnp.minimum(i + 1, last)

    @pl.when(i < last)
    def _():
        _run_copies(tab_next_ref, nb_ref[nxt], ns_ref[nxt], piece(1 - buf), lambda cp: cp.start())

    rec = route_ref[...]
    slot = lax.broadcasted_iota(I32, (OUT_TM, LOCAL_ROWS), 1).astype(F32)
    weights = (jnp.where(slot == rec[:, 0:1], rec[:, 2:3], 0.0)
               + jnp.where(slot == rec[:, 1:2], rec[:, 3:4], 0.0)).astype(BF16)
    _run_copies(tab_ref, nb_ref[i], ns_ref[i], lambda lo, so, rows: piece(buf)(0, 0, rows), lambda cp: cp.wait())
    moe = jnp.dot(weights, gbuf_ref[buf], preferred_element_type=F32)
    o_ref[...] = x1_ref[...] + g2_ref[0] * moe


def _combine(n_big, n_small, piece_table, route, x1, mod3, ys):
    tm = OUT_TM
    steps_per_batch = SEQ // tm
    grid_spec = pltpu.PrefetchScalarGridSpec(
        num_scalar_prefetch=2,
        grid=(N_TOKEN_TILES,),
        in_specs=[_piece_spec(lambda i, nb, ns: (i, 0, 0)),
                  _piece_spec(lambda i, nb, ns: (jnp.minimum(i + 1, N_TOKEN_TILES - 1), 0, 0)),
                  pl.BlockSpec((tm, ROUTE_W), lambda i, nb, ns: (i, 0)),
                  pl.BlockSpec((tm, D_MODEL), lambda i, nb, ns: (i, 0)),
                  pl.BlockSpec((1, 1, D_MODEL), lambda i, nb, ns: ((i // steps_per_batch) * 6 + 5, 0, 0)),
                  pl.BlockSpec(memory_space=pl.ANY)],
        out_specs=pl.BlockSpec((tm, D_MODEL), lambda i, nb, ns: (i, 0)),
        scratch_shapes=[pltpu.VMEM((2, LOCAL_ROWS, D_MODEL), BF16), pltpu.SemaphoreType.DMA((2,))],
    )
    return pl.pallas_call(
        _combine_kernel,
        grid_spec=grid_spec,
        out_shape=jax.ShapeDtypeStruct((TOKENS, D_MODEL), F32),
        compiler_params=_cparams(("arbitrary",)),
        name="combine",
    )(n_big, n_small, piece_table, piece_table, route, x1, mod3, ys)


def kernel(x, c, positions, norm1_w, norm2_w, w_ada, b_ada, w_in, conv_w, conv_b, dt_bias, a_log,
           d_skip, ssd_norm_w, q_norm_w, k_norm_w, sinks, w_out, w_group, b_group, w_expert, b_expert,
           w_gate, w_up, w_down):
    assert x.shape == (BATCH, SEQ, D_MODEL) and w_in.shape == (D_MODEL, IN_WIDTH)
    x2d = x.reshape(TOKENS, D_MODEL)
    mod = _ada_mod(c, w_ada, b_ada)
    mod3 = mod.reshape(BATCH * 6, 1, D_MODEL)

    qkv, z, xbc, dt, dt_t = _in_proj(x2d, norm1_w, mod3, w_in)
    att = _attention(qkv, positions, q_norm_w, k_norm_w, sinks)
    y = _ssd(xbc, z, dt, dt_t, conv_w, conv_b, dt_bias, a_log, d_skip, ssd_norm_w)

    w_router = jnp.pad(jnp.concatenate([w_group, w_expert], axis=1).astype(F32),
                       ((0, 0), (0, LANES - ROUTER_COLS)))
    b_router = jnp.pad(jnp.concatenate([b_group, b_expert]).astype(F32),
                       (0, LANES - ROUTER_COLS)).reshape(1, LANES)
    x1, h2, route, route_t, tcnt = _out_router(att, y, x2d, mod3, w_out.astype(F32), norm2_w,
                                                w_router, b_router)

    tc = tcnt[:, 0, 0:N_EXPERTS].astype(I32)
    run_rows = ((tc + RUN_ALIGN - 1) // RUN_ALIGN) * RUN_ALIGN
    counts = jnp.sum(run_rows, axis=0)
    padded = ((counts + MOE_TM - 1) // MOE_TM) * MOE_TM
    seg_end = jnp.cumsum(padded)
    seg_start = seg_end - padded
    run_dst = seg_start[None, :] + jnp.cumsum(run_rows, axis=0) - run_rows
    n_used = (seg_end[-1] // MOE_TM).reshape(1)
    last_row = jnp.minimum(jnp.arange(N_TILES, dtype=I32) * MOE_TM, seg_end[-1] - 1)
    tile_expert = jnp.sum((seg_end[None, :] <= last_row[:, None]).astype(I32), axis=1)

    run_local = jnp.cumsum(run_rows, axis=1) - run_rows
    n_big_run = run_rows // BIG_PIECE
    n_small_run = (run_rows // RUN_ALIGN) % 2
    q = jnp.arange(PIECE_SLOTS, dtype=I32)
    experts = jnp.arange(N_EXPERTS, dtype=I32)

    def flat(per_run, local0, dst0, stride):
        end = jnp.cumsum(per_run, axis=1)
        run_of = jnp.sum((end[:, None, :] <= q[None, :, None]).astype(I32), axis=2)
        pick = (run_of[:, :, None] == experts[None, None, :]).astype(I32)
        k = q[None, :] - jnp.sum(pick * (end - per_run)[:, None, :], axis=2)
        local = jnp.sum(pick * local0[:, None, :], axis=2) + stride * k
        dst = jnp.sum(pick * dst0[:, None, :], axis=2) + stride * k
        return end[:, -1], local, dst

    n_big, big_local, big_dst = flat(n_big_run, run_local, run_dst, BIG_PIECE)
    n_small, small_local, small_dst = flat(n_small_run, run_local + BIG_PIECE * n_big_run,
                                           run_dst + BIG_PIECE * n_big_run, 0)
    piece_table = jnp.concatenate([big_local, big_dst, small_local, small_dst], axis=1)
    piece_table = piece_table.astype(I32).reshape(N_TOKEN_TILES, 1, LANES)

    nonempty = padded > 0
    seg_rank = jnp.cumsum(nonempty.astype(I32)) - 1
    later = nonempty[None, :] & (experts[None, :] > experts[:, None])
    next_of = jnp.min(jnp.where(later, experts[None, :], N_EXPERTS), axis=1)
    next_of = jnp.where(next_of == N_EXPERTS, -1, next_of)
    tile_is = (tile_expert[:, None] == experts[None, :]).astype(I32)
    tile_segment = jnp.sum(tile_is * seg_rank[None, :], axis=1)
    next_expert = jnp.sum(tile_is * next_of[None, :], axis=1)

    n_big, n_small = n_big.astype(I32), n_small.astype(I32)
    xs = _dispatch(seg_end.astype(I32), (seg_start + counts).astype(I32), n_big, n_small, piece_table,
                   route_t, h2)
    ys = _experts(tile_expert, tile_segment.astype(I32), next_expert.astype(I32), n_used.astype(I32),
                  xs, w_gate, w_up, w_down)
    out = _combine(n_big, n_small, piece_table, route, x1, mod3, ys)
    return out.reshape(BATCH, SEQ, D_MODEL)
```

```python
import jax
import jax.numpy as jnp
from jax import lax
from jax.experimental import pallas as pl
from jax.experimental.pallas import tpu as pltpu

F32 = jnp.float32
BF16 = jnp.bfloat16
I32 = jnp.int32

D_MODEL = 1024
BATCH = 2
SEQ = 8192
TOKENS = BATCH * SEQ
ATT_HEADS = 8
ATT_KV_HEADS = 2
HEAD_DIM = 64
ATT_WIDTH = ATT_HEADS * HEAD_DIM
KV_WIDTH = ATT_KV_HEADS * HEAD_DIM
ATT_BLOCK = 128
ROPE_DIM = HEAD_DIM // 4
ROPE_THETA = 500000.0
SSD_HEADS = 8
SSD_HEAD_DIM = 64
SSD_WIDTH = SSD_HEADS * SSD_HEAD_DIM
SSD_GROUPS = 2
SSD_STATE = 128
CONV_K = 4
CHUNK = 128
XBC_WIDTH = SSD_WIDTH + 2 * SSD_GROUPS * SSD_STATE
IN_WIDTH = ATT_WIDTH + 2 * KV_WIDTH + SSD_WIDTH + XBC_WIDTH + SSD_HEADS
N_GROUPS = 4
EXPERTS_PER_GROUP = 8
N_EXPERTS = N_GROUPS * EXPERTS_PER_GROUP
TOP_K = 2
D_EXPERT = 256
EPS = 1e-6

LANES = 128
QKV_WIDTH = ATT_WIDTH + 2 * KV_WIDTH
IN_PAD = QKV_WIDTH + SSD_WIDTH + XBC_WIDTH + LANES
NEG_BIG = -1e30

VMEM_LIMIT = 48 * 1024 * 1024


def _cparams(sem):
    return pltpu.CompilerParams(dimension_semantics=sem, vmem_limit_bytes=VMEM_LIMIT)


def _split_bf16(x):
    hi = x.astype(BF16)
    lo = (x - hi.astype(F32)).astype(BF16)
    return hi, lo


ADA_TN = 768


def _ada_kernel(ct_ref, w_ref, b_ref, o_ref):
    ct = ct_ref[...]
    s = ct * jax.nn.sigmoid(ct)
    w = w_ref[...]
    rows = [jnp.sum(s[:, b:b + 1] * w, axis=0, keepdims=True) for b in range(BATCH)]
    o_ref[...] = jnp.concatenate(rows, axis=0) + b_ref[...]


def _ada_mod(c, w_ada, b_ada):
    n = w_ada.shape[1]
    return pl.pallas_call(
        _ada_kernel,
        grid=(n // ADA_TN,),
        in_specs=[pl.BlockSpec((D_MODEL, BATCH), lambda j: (0, 0)),
                  pl.BlockSpec((D_MODEL, ADA_TN), lambda j: (0, j)),
                  pl.BlockSpec((1, ADA_TN), lambda j: (0, j))],
        out_specs=pl.BlockSpec((BATCH, ADA_TN), lambda j: (0, j)),
        out_shape=jax.ShapeDtypeStruct((BATCH, n), F32),
        compiler_params=_cparams(("arbitrary",)),
        name="ada_mod",
    )(c.T, w_ada, b_ada.reshape(1, n))


INPROJ_TM = 512
_INPROJ_CHUNK = 256


def _inproj_kernel(x_ref, nw_ref, sc_ref, sh_ref, wf_ref, wdt_ref, qkv_ref, z_ref, xbc_ref, dt_ref, dtt_ref,
                   w_ref):
    @pl.when(pl.program_id(0) == 0)
    def _():
        for c0 in range(0, IN_PAD - LANES, _INPROJ_CHUNK):
            w_ref[:, c0:c0 + _INPROJ_CHUNK] = wf_ref[:, c0:c0 + _INPROJ_CHUNK].astype(BF16)
        w_ref[:, IN_PAD - LANES:IN_PAD] = wdt_ref[...].astype(BF16)

    x = x_ref[...]
    y = x * lax.rsqrt(jnp.mean(x * x, axis=-1, keepdims=True) + EPS)
    h = (y * nw_ref[...]) * (1.0 + sc_ref[0]) + sh_ref[0]
    hb = h.astype(BF16)

    def proj(c0, c1):
        return jnp.dot(hb, w_ref[:, c0:c1], preferred_element_type=F32)

    for c0 in range(0, QKV_WIDTH, _INPROJ_CHUNK):
        qkv_ref[:, c0:c0 + _INPROJ_CHUNK] = proj(c0, c0 + _INPROJ_CHUNK).astype(BF16)
    base = QKV_WIDTH
    for c0 in range(0, SSD_WIDTH, _INPROJ_CHUNK):
        z_ref[:, c0:c0 + _INPROJ_CHUNK] = proj(base + c0, base + c0 + _INPROJ_CHUNK).astype(BF16)
    base += SSD_WIDTH
    for c0 in range(0, XBC_WIDTH, _INPROJ_CHUNK):
        xbc_ref[:, c0:c0 + _INPROJ_CHUNK] = proj(base + c0, base + c0 + _INPROJ_CHUNK).astype(BF16)
    base += XBC_WIDTH
    dt = proj(base, base + LANES)
    dt_ref[...] = dt
    dtt_ref[...] = dt.T[0:SSD_HEADS, :]


def _in_proj(x2d, norm_w, mod3, w_in):
    tm = INPROJ_TM
    steps_per_batch = SEQ // tm
    w_dt = jnp.pad(w_in[:, IN_WIDTH - SSD_HEADS:IN_WIDTH].astype(F32), ((0, 0), (0, LANES - SSD_HEADS)))
    return pl.pallas_call(
        _inproj_kernel,
        grid=(TOKENS // tm,),
        in_specs=[pl.BlockSpec((tm, D_MODEL), lambda i: (i, 0)),
                  pl.BlockSpec((1, D_MODEL), lambda i: (0, 0)),
                  pl.BlockSpec((1, 1, D_MODEL), lambda i: ((i // steps_per_batch) * 6 + 1, 0, 0)),
                  pl.BlockSpec((1, 1, D_MODEL), lambda i: ((i // steps_per_batch) * 6 + 0, 0, 0)),
                  pl.BlockSpec((D_MODEL, IN_WIDTH), lambda i: (0, 0), pipeline_mode=pl.Buffered(1)),
                  pl.BlockSpec((D_MODEL, LANES), lambda i: (0, 0))],
        out_specs=[pl.BlockSpec((tm, QKV_WIDTH), lambda i: (i, 0)),
                   pl.BlockSpec((tm, SSD_WIDTH), lambda i: (i, 0)),
                   pl.BlockSpec((tm, XBC_WIDTH), lambda i: (i, 0)),
                   pl.BlockSpec((tm, LANES), lambda i: (i, 0)),
                   pl.BlockSpec((SSD_HEADS, tm), lambda i: (0, i))],
        out_shape=[jax.ShapeDtypeStruct((TOKENS, QKV_WIDTH), BF16),
                   jax.ShapeDtypeStruct((TOKENS, SSD_WIDTH), BF16),
                   jax.ShapeDtypeStruct((TOKENS, XBC_WIDTH), BF16),
                   jax.ShapeDtypeStruct((TOKENS, LANES), F32),
                   jax.ShapeDtypeStruct((SSD_HEADS, TOKENS), F32)],
        scratch_shapes=[pltpu.VMEM((D_MODEL, IN_PAD), BF16)],
        compiler_params=_cparams(("arbitrary",)),
        name="in_proj",
    )(x2d, norm_w.reshape(1, D_MODEL), mod3, mod3, w_in.astype(F32), w_dt)


ATT_SUB = 2


ROPE_TM = 2048
_ROPE_HALF = ROPE_DIM // 2
_TOK_PER_ROW = LANES // _ROPE_HALF


def _exact_dot(x, onehot_b):
    hi, lo = _split_bf16(x)
    return (jnp.dot(hi, onehot_b, preferred_element_type=F32)
            + jnp.dot(lo, onehot_b, preferred_element_type=F32))


def _rope_kernel(pos_ref, freq_ref, sel_ref, own_ref, gcos_ref, gs1_ref, gs2_ref, ident_ref,
                 cos_ref, s1_ref, s2_ref):
    ang = pos_ref[...].astype(F32) * freq_ref[...]
    cos_p, sin_p = jnp.cos(ang), jnp.sin(ang)
    hi_c, lo_c = _split_bf16(cos_p)
    hi_s, lo_s = _split_bf16(sin_p)
    sel = sel_ref[...]
    rows_c = jnp.dot(sel, hi_c, preferred_element_type=F32) + jnp.dot(sel, lo_c, preferred_element_type=F32)
    rows_s = jnp.dot(sel, hi_s, preferred_element_type=F32) + jnp.dot(sel, lo_s, preferred_element_type=F32)
    own = own_ref[...]
    cos_ref[...] = _exact_dot(rows_c * own, gcos_ref[...]) + ident_ref[...]
    s1_ref[...] = _exact_dot(rows_s * own, gs1_ref[...])
    s2_ref[...] = _exact_dot(rows_s * own, gs2_ref[...])


def _rope_tables(positions):
    half, per_row = _ROPE_HALF, _TOK_PER_ROW
    rows = ROPE_TM // per_row
    pos_rep = jnp.repeat(positions.reshape(TOKENS).astype(I32), half).reshape(TOKENS // per_row, LANES)
    inv_freq = jnp.power(ROPE_THETA, -jnp.arange(half, dtype=F32) * 2.0 / ROPE_DIM)
    freq = jnp.tile(inv_freq, per_row).reshape(1, LANES)
    tok = jnp.arange(ROPE_TM)
    lane = jnp.arange(LANES)
    sel = (tok[:, None] // per_row == jnp.arange(rows)[None, :]).astype(BF16)
    own = (lane[None, :] // half == tok[:, None] % per_row).astype(F32)
    d = lane % HEAD_DIM
    src_f = lane % half
    hits = lambda lo, hi: ((src_f[:, None] == d[None, :] % half) & (d[None, :] >= lo) & (d[None, :] < hi))
    gcos = hits(0, ROPE_DIM).astype(BF16)
    gs1 = -hits(0, half).astype(BF16)
    gs2 = hits(half, ROPE_DIM).astype(BF16)
    ident = (d >= ROPE_DIM).astype(F32).reshape(1, LANES)
    const = lambda shape: pl.BlockSpec(shape, lambda i: (0, 0))
    out_spec = pl.BlockSpec((ROPE_TM, LANES), lambda i: (i, 0))
    out = jax.ShapeDtypeStruct((TOKENS, LANES), F32)
    return pl.pallas_call(
        _rope_kernel,
        grid=(TOKENS // ROPE_TM,),
        in_specs=[pl.BlockSpec((rows, LANES), lambda i: (i, 0)), const((1, LANES)),
                  const((ROPE_TM, rows)), const((ROPE_TM, LANES)),
                  const((LANES, LANES)), const((LANES, LANES)), const((LANES, LANES)), const((1, LANES))],
        out_specs=[out_spec, out_spec, out_spec],
        out_shape=[out, out, out],
        compiler_params=_cparams(("arbitrary",)),
        name="rope_tables",
    )(pos_rep, freq, sel, own, gcos, gs1, gs2, ident)


def _seg_meansq(xf, ones128):
    rows, width = xf.shape
    nt = width // LANES
    parts = _split_bf16(xf * xf)
    stacked = jnp.concatenate([p[:, t * LANES:(t + 1) * LANES] for p in parts for t in range(nt)], axis=0)
    tot = jnp.dot(stacked, ones128, preferred_element_type=F32)
    tiles = [tot[t * rows:(t + 1) * rows] + tot[(nt + t) * rows:(nt + t + 1) * rows] for t in range(nt)]
    return jnp.concatenate(tiles, axis=1) * (1.0 / HEAD_DIM)


def _norm_rope(x_bf, w_row, ones_bd, cosf, s1, s2):
    xf = x_bf.astype(F32)
    width = xf.shape[1]
    xn = xf * lax.rsqrt(_seg_meansq(xf, ones_bd) + EPS) * w_row
    half = ROPE_DIM // 2
    up = pltpu.roll(xn, width - half, axis=1)
    down = pltpu.roll(xn, half, axis=1)
    return xn * cosf + up * s1 + down * s2


def _attn_kernel(sink_ref, q_ref, kv_ref, cos_ref, s1_ref, s2_ref, qw_ref, kw_ref,
                 ones_ref, o_ref, kprev_ref, vprev_ref):
    j = pl.program_id(1)
    blk = ATT_BLOCK

    @pl.when(j == 0)
    def _():
        kprev_ref[...] = jnp.zeros_like(kprev_ref)
        vprev_ref[...] = jnp.zeros_like(vprev_ref)

    cos1 = cos_ref[...]
    s1_1 = s1_ref[...]
    s2_1 = s2_ref[...]
    reps = ATT_WIDTH // LANES
    cosq = jnp.concatenate([cos1] * reps, axis=1)
    s1q = jnp.concatenate([s1_1] * reps, axis=1)
    s2q = jnp.concatenate([s2_1] * reps, axis=1)

    q = _norm_rope(q_ref[...], qw_ref[...], ones_ref[...], cosq, s1q, s2q)
    qf = q * (HEAD_DIM ** -0.5)
    kv = kv_ref[...]
    kn = _norm_rope(kv[:, 0:KV_WIDTH], kw_ref[...], ones_ref[...], cos1, s1_1, s2_1)
    vn = kv[:, KV_WIDTH:2 * KV_WIDTH].astype(F32)

    kall = jnp.concatenate([kprev_ref[...], kn], axis=0)
    vall = jnp.concatenate([vprev_ref[...], vn], axis=0)
    kprev_ref[...] = kn[(ATT_SUB - 1) * blk:ATT_SUB * blk]
    vprev_ref[...] = vn[(ATT_SUB - 1) * blk:ATT_SUB * blk]

    lo_all = lax.broadcasted_iota(I32, kall.shape, 1) < HEAD_DIM
    ones_all = jnp.ones(kall.shape, BF16)

    row = lax.broadcasted_iota(I32, (2 * blk, blk), 0)
    col = lax.broadcasted_iota(I32, (2 * blk, blk), 1)
    from_prev = col > (row & (blk - 1))
    second_tile = lax.broadcasted_iota(I32, (2 * blk, 1), 0) >= blk
    zero_p = jnp.zeros((2 * blk, blk), F32)

    k_par, v_par = [], []
    for g in range(ATT_KV_HEADS):
        keep = lo_all if g == 0 else ~lo_all
        k_own = jnp.where(keep, kall, 0.0)
        v_own = jnp.where(keep, vall, 0.0)
        k_oth = pltpu.roll(k_own, HEAD_DIM, axis=1)
        v_oth = pltpu.roll(v_own, HEAD_DIM, axis=1)
        k_lo, k_hi = (k_own, k_oth) if g == 0 else (k_oth, k_own)
        v_lo, v_hi = (v_own, v_oth) if g == 0 else (v_oth, v_own)
        k_par.append((k_lo.astype(BF16), k_hi.astype(BF16)))
        v_par.append((jnp.concatenate([v_lo.astype(BF16), ones_all], axis=1),
                      jnp.concatenate([v_hi.astype(BF16), ones_all], axis=1)))

    problems = [(g, sub) for g in range(ATT_KV_HEADS) for sub in range(ATT_SUB)]
    scores = []
    for g, sub in problems:
        r0, c0 = sub * blk, g * 2 * LANES
        qcat = jnp.concatenate([qf[r0:r0 + blk, c0:c0 + LANES],
                                qf[r0:r0 + blk, c0 + LANES:c0 + 2 * LANES]], axis=0).astype(BF16)
        kw = jnp.concatenate([k_par[g][0][r0:r0 + 2 * blk], k_par[g][1][r0:r0 + 2 * blk]], axis=0)
        scores.append(lax.dot_general(qcat, kw, (((1,), (1,)), ((), ())),
                                      preferred_element_type=F32))

    weights, rescale = [], []
    for (g, sub), s_all in zip(problems, scores):
        for par in range(2):
            s = s_all[:, par * 2 * blk:(par + 1) * 2 * blk]
            s_prev = s[:, 0:blk]
            if sub == 0:
                s_prev = s_prev + jnp.where(j > 0, 0.0, NEG_BIG)
            s = jnp.where(from_prev, s_prev, s[:, blk:2 * blk])
            h_first = ATT_HEADS // ATT_KV_HEADS * g + par
            sink = jnp.where(second_tile, sink_ref[h_first + 2], sink_ref[h_first])
            m = jnp.maximum(jnp.max(s, axis=-1, keepdims=True), sink)
            p = jnp.exp(s - m)
            weights.append(jnp.concatenate([jnp.where(from_prev, p, zero_p), jnp.where(from_prev, zero_p, p)],
                                           axis=1).astype(BF16))
            rescale.append(jnp.exp(sink - m))

    outs = []
    for idx, (g, sub) in enumerate(problems):
        for par in range(2):
            outs.append(jnp.dot(weights[2 * idx + par], v_par[g][par][sub * blk:(sub + 2) * blk],
                                preferred_element_type=F32))

    for idx, (g, sub) in enumerate(problems):
        r0, c0 = sub * blk, g * 2 * LANES
        pair = None
        for par in range(2):
            o = outs[2 * idx + par]
            part = o[:, 0:LANES] * (1.0 / (o[:, LANES:2 * LANES] + rescale[2 * idx + par]))
            pair = part if pair is None else pair + part
        o_ref[r0:r0 + blk, c0:c0 + LANES] = pair[0:blk].astype(BF16)
        o_ref[r0:r0 + blk, c0 + LANES:c0 + 2 * LANES] = pair[blk:2 * blk].astype(BF16)


def _attention(qkv, positions, q_norm_w, k_norm_w, sinks):
    cosf, s1, s2 = _rope_tables(positions)
    qw = jnp.tile(q_norm_w.astype(F32), ATT_HEADS).reshape(1, ATT_WIDTH)
    kw = jnp.tile(k_norm_w.astype(F32), ATT_KV_HEADS).reshape(1, KV_WIDTH)
    seg = jnp.arange(LANES) // HEAD_DIM
    ones128 = (seg[:, None] == seg[None, :]).astype(BF16)
    const = lambda shape: pl.BlockSpec(shape, lambda b, j, s: (0, 0))
    rows = ATT_SUB * ATT_BLOCK
    nb = SEQ // rows
    tok = lambda width, cb: pl.BlockSpec((rows, width), lambda b, j, s: (b * nb + j, cb))
    grid_spec = pltpu.PrefetchScalarGridSpec(
        num_scalar_prefetch=1,
        grid=(BATCH, nb),
        in_specs=[tok(ATT_WIDTH, 0), tok(2 * KV_WIDTH, 2), tok(LANES, 0), tok(LANES, 0), tok(LANES, 0),
                  const((1, ATT_WIDTH)), const((1, KV_WIDTH)), const((LANES, LANES))],
        out_specs=tok(ATT_WIDTH, 0),
        scratch_shapes=[pltpu.VMEM((ATT_BLOCK, KV_WIDTH), F32),
                        pltpu.VMEM((ATT_BLOCK, KV_WIDTH), F32)],
    )
    return pl.pallas_call(
        _attn_kernel,
        grid_spec=grid_spec,
        out_shape=jax.ShapeDtypeStruct((TOKENS, ATT_WIDTH), BF16),
        compiler_params=_cparams(("arbitrary", "arbitrary")),
        name="attention",
    )(sinks.astype(F32), qkv, qkv, cosf, s1, s2, qw, kw, ones128)


def _softplus(x):
    return jnp.maximum(x, 0.0) + jnp.log1p(jnp.exp(-jnp.abs(x)))


def _silu(x):
    return x * jax.nn.sigmoid(x)


def _ssd_kernel(xbc_ref, z_ref, dt_ref, dtt_ref, cw_ref, cb_ref, dtb_row_ref, dtb_col_ref,
                alog_row_ref, alog_col_ref, dskip_ref, nw_ref, tril_ref, triu_ref,
                o_ref, conv_ref, state_ref):
    c = pl.program_id(1)
    L = CHUNK
    tail = 8

    @pl.when(c == 0)
    def _():
        conv_ref[0:tail, :] = jnp.zeros((tail, XBC_WIDTH), F32)
        state_ref[...] = jnp.zeros_like(state_ref)

    xb = xbc_ref[...].astype(F32)
    conv_ref[tail:tail + L, :] = xb
    acc = cb_ref[...] + cw_ref[CONV_K - 1:CONV_K, :] * xb
    for k in range(CONV_K - 1):
        off = tail - (CONV_K - 1) + k
        acc = acc + cw_ref[k:k + 1, :] * conv_ref[off:off + L, :]
    conv_ref[0:tail, :] = xb[L - tail:L, :]
    u = _silu(acc)
    xs = u[:, 0:SSD_WIDTH]
    bmat = u[:, SSD_WIDTH:SSD_WIDTH + SSD_GROUPS * SSD_STATE]
    cmat = u[:, SSD_WIDTH + SSD_GROUPS * SSD_STATE:XBC_WIDTH]

    dt = _softplus(dt_ref[...] + dtb_row_ref[...])
    a = dt * (-jnp.exp(alog_row_ref[...]))
    a_hi, a_lo = _split_bf16(a)
    a_cum = (jnp.dot(tril_ref[...], a_hi, preferred_element_type=F32)
             + jnp.dot(tril_ref[...], a_lo, preferred_element_type=F32))
    dt_t = _softplus(dtt_ref[...] + dtb_col_ref[...])
    a_t = dt_t * (-jnp.exp(alog_col_ref[...]))
    at_hi, at_lo = _split_bf16(a_t)
    a_cum_t = (jnp.dot(at_hi, triu_ref[...], preferred_element_type=F32)
               + jnp.dot(at_lo, triu_ref[...], preferred_element_type=F32))
    a_end_t = a_cum_t[:, L - 1:L]
    wst_t = jnp.exp(a_end_t - a_cum_t) * dt_t
    cdec_t = jnp.exp(a_end_t)

    row = lax.broadcasted_iota(I32, (L, L), 0)
    col = lax.broadcasted_iota(I32, (L, L), 1)
    causal = col <= row
    lane = lax.broadcasted_iota(I32, (L, LANES), 1)
    lo_half = lane < SSD_HEAD_DIM

    xs_b = xs.astype(BF16)
    heads_per_group = SSD_HEADS // SSD_GROUPS
    gated = []
    for g in range(SSD_GROUPS):
        b_g = bmat[:, g * SSD_STATE:(g + 1) * SSD_STATE]
        c_g = cmat[:, g * SSD_STATE:(g + 1) * SSD_STATE]
        cb = lax.dot_general(c_g.astype(BF16), b_g.astype(BF16), (((1,), (1,)), ((), ())),
                             preferred_element_type=F32)
        b_gt = b_g.T
        for t in range(heads_per_group // 2):
            tile = g * (heads_per_group // 2) + t
            c0 = tile * LANES
            xs_tile = xs_b[:, c0:c0 + LANES]
            st_tile = state_ref[:, c0:c0 + LANES]
            st_b = st_tile.astype(BF16)
            y_tile = jnp.zeros((L, LANES), F32)
            new_tile = jnp.zeros((SSD_STATE, LANES), F32)
            for e in range(2):
                h = 2 * tile + e
                keep = lo_half if e == 0 else ~lo_half
                colb = jnp.broadcast_to(a_cum[:, h:h + 1], (L, L))
                rowb = a_cum_t[h:h + 1, :]
                decay = jnp.exp(jnp.where(causal, colb - rowb, NEG_BIG))
                w_in = (cb * decay) * dt_t[h:h + 1, :]
                w_off = c_g * jnp.exp(colb)
                lhs = jnp.concatenate([w_in, w_off], axis=1).astype(BF16)
                rhs = jnp.concatenate([jnp.where(keep, xs_tile, jnp.zeros_like(xs_tile)),
                                       jnp.where(keep, st_b, jnp.zeros_like(st_b))], axis=0)
                y_tile = y_tile + jnp.dot(lhs, rhs, preferred_element_type=F32)
                m_h = (b_gt * wst_t[h:h + 1, :]).astype(BF16)
                new_tile = new_tile + jnp.dot(m_h, jnp.where(keep, xs_tile, jnp.zeros_like(xs_tile)),
                                              preferred_element_type=F32)
            cd = jnp.where(lo_half[0:1, :], cdec_t[2 * tile:2 * tile + 1, :],
                           cdec_t[2 * tile + 1:2 * tile + 2, :])
            state_ref[:, c0:c0 + LANES] = st_tile * cd + new_tile
            y_full = y_tile + dskip_ref[:, c0:c0 + LANES] * xs[:, c0:c0 + LANES]
            gated.append(y_full * _silu(z_ref[:, c0:c0 + LANES].astype(F32)))

    gw = SSD_WIDTH // SSD_GROUPS
    tiles_per_group = gw // LANES
    for g in range(SSD_GROUPS):
        yg = jnp.concatenate(gated[g * tiles_per_group:(g + 1) * tiles_per_group], axis=1)
        ms = jnp.mean(yg * yg, axis=-1, keepdims=True)
        o_ref[:, g * gw:(g + 1) * gw] = ((yg * lax.rsqrt(ms + EPS)) * nw_ref[:, g * gw:(g + 1) * gw]).astype(o_ref.dtype)


def _ssd(xbc, z, dt, dt_t, conv_w, conv_b, dt_bias, a_log, d_skip, ssd_norm_w):
    nc = SEQ // CHUNK
    L = CHUNK
    pad_row = lambda v: jnp.pad(v.astype(F32), (0, LANES - SSD_HEADS)).reshape(1, LANES)
    col8 = lambda v: v.astype(F32).reshape(SSD_HEADS, 1)
    idx = jnp.arange(L)
    tril = (idx[None, :] <= idx[:, None]).astype(BF16)
    triu = (idx[:, None] <= idx[None, :]).astype(BF16)
    dskip = jnp.repeat(d_skip.astype(F32), SSD_HEAD_DIM).reshape(1, SSD_WIDTH)
    const = lambda shape: pl.BlockSpec(shape, lambda b, c: (0, 0))
    tok = lambda width: pl.BlockSpec((L, width), lambda b, c: (b * nc + c, 0))
    return pl.pallas_call(
        _ssd_kernel,
        grid=(BATCH, nc),
        in_specs=[tok(XBC_WIDTH), tok(SSD_WIDTH), tok(LANES),
                  pl.BlockSpec((SSD_HEADS, L), lambda b, c: (0, b * nc + c)),
                  const((CONV_K, XBC_WIDTH)), const((1, XBC_WIDTH)),
                  const((1, LANES)), const((SSD_HEADS, 1)), const((1, LANES)), const((SSD_HEADS, 1)),
                  const((1, SSD_WIDTH)), const((1, SSD_WIDTH)), const((L, L)), const((L, L))],
        out_specs=tok(SSD_WIDTH),
        out_shape=jax.ShapeDtypeStruct((TOKENS, SSD_WIDTH), BF16),
        scratch_shapes=[pltpu.VMEM((8 + L, XBC_WIDTH), F32),
                        pltpu.VMEM((SSD_STATE, SSD_WIDTH), F32)],
        compiler_params=_cparams(("arbitrary", "arbitrary")),
        name="ssd",
    )(xbc, z, dt, dt_t, conv_w.astype(F32), conv_b.astype(F32).reshape(1, XBC_WIDTH),
      pad_row(dt_bias), col8(dt_bias), pad_row(a_log), col8(a_log), dskip,
      ssd_norm_w.astype(F32).reshape(1, SSD_WIDTH), tril, triu)


OUT_TM = 512
ROUTE_W = 8
ROUTER_COLS = N_GROUPS + N_EXPERTS
RUN_ALIGN = 16
RUN_SHIFT = 4
LOCAL_ROWS = 1536
assert RUN_ALIGN == 1 << RUN_SHIFT and LOCAL_ROWS >= TOP_K * OUT_TM + N_EXPERTS * (RUN_ALIGN - 1)


def _lane_pick(values, lane, index):
    return jnp.sum(jnp.where(lane == index, values, 0.0), axis=-1, keepdims=True)


def _first_argmax(vals, lane):
    m = jnp.max(vals, axis=-1, keepdims=True)
    idx = jnp.min(jnp.where(vals == m, lane, float(LANES)), axis=-1, keepdims=True)
    return m, idx


def _out_router_kernel(att_ref, y_ref, x_ref, g1_ref, wof_ref, nw_ref, sc_ref, sh_ref, wr_ref, br_ref,
                       ltri_ref, sut_ref, x1_ref, h2_ref, route_ref, routet_ref, tcnt_ref,
                       wr_split_ref, logits_ref, wo_ref):
    i = pl.program_id(0)

    @pl.when(i == 0)
    def _():
        hi, lo = _split_bf16(wr_ref[...])
        wr_split_ref[:, 0:LANES] = hi
        wr_split_ref[:, LANES:2 * LANES] = lo
        logits_ref[...] = jnp.zeros_like(logits_ref)
        for r0 in range(0, D_MODEL, 256):
            wo_ref[r0:r0 + 256, :] = wof_ref[r0:r0 + 256, :].astype(BF16)

    logits = logits_ref[...]

    mixer = (jnp.dot(att_ref[...], wo_ref[0:ATT_WIDTH, :], preferred_element_type=F32)
             + jnp.dot(y_ref[...], wo_ref[ATT_WIDTH:ATT_WIDTH + SSD_WIDTH, :], preferred_element_type=F32))
    x1 = x_ref[...] + g1_ref[0] * mixer
    x1_ref[...] = x1
    yn = x1 * lax.rsqrt(jnp.mean(x1 * x1, axis=-1, keepdims=True) + EPS)
    h2 = (yn * nw_ref[...]) * (1.0 + sc_ref[0]) + sh_ref[0]
    h2_ref[...] = h2.astype(BF16)

    h_hi, h_lo = _split_bf16(h2)
    both = jnp.dot(h_hi, wr_split_ref[...], preferred_element_type=F32)
    logits_ref[...] = (both[:, 0:LANES] + both[:, LANES:2 * LANES]
                       + jnp.dot(h_lo, wr_split_ref[:, 0:LANES], preferred_element_type=F32)) + br_ref[...]

    tm = logits.shape[0]
    lane = lax.broadcasted_iota(I32, (tm, LANES), 1).astype(F32)

    gl = jnp.where(lane < N_GROUPS, logits, NEG_BIG)
    gmax, gidx = _first_argmax(gl, lane)
    g_p = 1.0 / jnp.sum(jnp.exp(gl - gmax), axis=-1, keepdims=True)

    lo_lane = N_GROUPS + EXPERTS_PER_GROUP * gidx
    el = jnp.where((lane >= lo_lane) & (lane < lo_lane + EXPERTS_PER_GROUP), logits, NEG_BIG)
    m1, i1 = _first_argmax(el, lane)
    m2, i2 = _first_argmax(jnp.where(lane == i1, NEG_BIG, el), lane)
    r = jnp.exp(m2 - m1)
    p1 = 1.0 / (1.0 + r)
    p2 = r / (1.0 + r)
    e0 = i1 - N_GROUPS
    e1 = i2 - N_GROUPS

    onehot = ((lane == e0) | (lane == e1)).astype(F32)
    tile_cnt = jnp.sum(onehot, axis=0, keepdims=True)
    run_len = jnp.floor((tile_cnt + (RUN_ALIGN - 1)) * (1.0 / RUN_ALIGN)) * RUN_ALIGN
    run_start = jnp.dot(jnp.broadcast_to(run_len, (8, LANES)).astype(BF16), sut_ref[...],
                        preferred_element_type=F32)[0:1, :]
    before = jnp.dot(ltri_ref[...], onehot.astype(BF16), preferred_element_type=F32) + run_start
    slot0 = _lane_pick(before, lane, e0)
    slot1 = _lane_pick(before, lane, e1)
    tcnt_ref[0] = tile_cnt

    rec = jnp.zeros((tm, LANES), F32)
    for k, v in enumerate([slot0, slot1, g_p * p1, g_p * p2, e0, e1]):
        rec = jnp.where(lane == k, v, rec)
    route_ref[...] = rec[:, 0:ROUTE_W]
    routet_ref[...] = rec.T[0:ROUTE_W, :]


def _out_router(att, y, x2d, mod3, w_out_b, norm_w, w_router, b_router):
    tm = OUT_TM
    n_steps = TOKENS // tm
    steps_per_batch = SEQ // tm
    idx = jnp.arange(tm)
    ltri = (idx[None, :] < idx[:, None]).astype(BF16)
    lidx = jnp.arange(LANES)
    sut = (lidx[:, None] < lidx[None, :]).astype(BF16)
    const = lambda shape: pl.BlockSpec(shape, lambda i: (0, 0))
    cur = lambda i: jnp.minimum(i, n_steps - 1)
    prev = lambda i: jnp.maximum(i - 1, 0)
    tok = lambda width: pl.BlockSpec((tm, width), lambda i: (cur(i), 0))
    modspec = lambda k: pl.BlockSpec((1, 1, D_MODEL), lambda i: ((cur(i) // steps_per_batch) * 6 + k, 0, 0))
    return pl.pallas_call(
        _out_router_kernel,
        grid=(n_steps + 1,),
        in_specs=[tok(ATT_WIDTH), tok(SSD_WIDTH), tok(D_MODEL), modspec(2),
                  const((D_MODEL, D_MODEL)), const((1, D_MODEL)), modspec(4), modspec(3),
                  const((D_MODEL, LANES)), const((1, LANES)), const((tm, tm)), const((LANES, LANES))],
        out_specs=[tok(D_MODEL), tok(D_MODEL),
                   pl.BlockSpec((tm, ROUTE_W), lambda i: (prev(i), 0)),
                   pl.BlockSpec((ROUTE_W, tm), lambda i: (prev(i), 0)),
                   pl.BlockSpec((1, 1, LANES), lambda i: (prev(i), 0, 0))],
        out_shape=[jax.ShapeDtypeStruct((TOKENS, D_MODEL), F32),
                   jax.ShapeDtypeStruct((TOKENS, D_MODEL), BF16),
                   jax.ShapeDtypeStruct((TOKENS, ROUTE_W), F32),
                   jax.ShapeDtypeStruct((n_steps * ROUTE_W, tm), F32),
                   jax.ShapeDtypeStruct((n_steps, 1, LANES), F32)],
        scratch_shapes=[pltpu.VMEM((D_MODEL, 2 * LANES), BF16), pltpu.VMEM((tm, LANES), F32),
                        pltpu.VMEM((D_MODEL, D_MODEL), BF16)],
        compiler_params=_cparams(("arbitrary",)),
        name="out_router",
    )(att, y, x2d, mod3, w_out_b, norm_w.reshape(1, D_MODEL), mod3, mod3, w_router, b_router, ltri, sut)


MOE_TM = 512
ZERO_ROWS = 256
N_TOKEN_TILES = TOKENS // OUT_TM
MAX_SORTED_ROWS = TOKENS * TOP_K + N_TOKEN_TILES * N_EXPERTS * (RUN_ALIGN - 1)
N_TILES = MAX_SORTED_ROWS // MOE_TM + N_EXPERTS
N_ROWS = N_TILES * MOE_TM
assert MOE_TM % ZERO_ROWS == 0


BIG_PIECE = 2 * RUN_ALIGN
PIECE_SLOTS = LOCAL_ROWS // BIG_PIECE
TABLE_W = 4 * PIECE_SLOTS
assert PIECE_SLOTS >= N_EXPERTS


def _run_copies(table_ref, n_big, n_small, make_copy, action):
    def big(q, carry):
        action(make_copy(table_ref[0, 0, q], table_ref[0, 0, PIECE_SLOTS + q], BIG_PIECE))
        return carry

    def small(q, carry):
        action(make_copy(table_ref[0, 0, 2 * PIECE_SLOTS + q], table_ref[0, 0, 3 * PIECE_SLOTS + q], RUN_ALIGN))
        return carry

    lax.fori_loop(0, n_big, big, 0)
    lax.fori_loop(0, n_small, small, 0)


def _dispatch_kernel(seg_end_ref, used_end_ref, nb_ref, ns_ref, tab_ref, routet_ref, h2_ref, xs_ref,
                     sbuf_ref, zero_ref, sems, zsem):
    i = pl.program_id(0)
    last = pl.num_programs(0) - 1
    buf = lax.rem(i, 2)

    def zero_fills(action):
        def tail_copy(row):
            return pltpu.make_async_copy(zero_ref.at[pl.ds(0, RUN_ALIGN)],
                                         xs_ref.at[pl.ds(pl.multiple_of(row, RUN_ALIGN), RUN_ALIGN)], zsem)

        def block_copy(block):
            start = pl.multiple_of(block * ZERO_ROWS, ZERO_ROWS)
            return pltpu.make_async_copy(zero_ref, xs_ref.at[pl.ds(start, ZERO_ROWS)], zsem)

        def tails(e, carry):
            def body(r, c):
                action(tail_copy(r * RUN_ALIGN))
                return c

            lax.fori_loop(used_end_ref[e] // RUN_ALIGN, seg_end_ref[e] // RUN_ALIGN, body, 0)
            return carry

        def blocks(block, carry):
            action(block_copy(block))
            return carry

        lax.fori_loop(0, N_EXPERTS, tails, 0)
        lax.fori_loop(seg_end_ref[N_EXPERTS - 1] // ZERO_ROWS, N_ROWS // ZERO_ROWS, blocks, 0)

    @pl.when(i == 0)
    def _():
        zero_ref[...] = jnp.zeros_like(zero_ref)
        zero_fills(lambda cp: cp.start())

    slot = lax.broadcasted_iota(I32, (LOCAL_ROWS, OUT_TM), 0).astype(F32)
    perm = jnp.where((slot == routet_ref[0:1, :]) | (slot == routet_ref[1:2, :]), 1.0, 0.0).astype(BF16)
    sbuf_ref[buf] = jnp.dot(perm, h2_ref[...], preferred_element_type=F32).astype(BF16)

    def piece(b):
        def make(local, sorted_row, rows):
            return pltpu.make_async_copy(
                sbuf_ref.at[b, pl.ds(pl.multiple_of(local, RUN_ALIGN), rows)],
                xs_ref.at[pl.ds(pl.multiple_of(sorted_row, RUN_ALIGN), rows)], sems.at[b])
        return make

    _run_copies(tab_ref, nb_ref[i], ns_ref[i], piece(buf), lambda cp: cp.start())
    prev = jnp.maximum(i - 1, 0)

    @pl.when(i > 0)
    def _():
        _run_copies(tab_ref, nb_ref[prev], ns_ref[prev], lambda lo, so, rows: piece(1 - buf)(0, 0, rows),
                    lambda cp: cp.wait())

    @pl.when(i == last)
    def _():
        _run_copies(tab_ref, nb_ref[i], ns_ref[i], lambda lo, so, rows: piece(buf)(0, 0, rows),
                    lambda cp: cp.wait())
        zero_fills(lambda cp: cp.wait())


def _piece_spec(index_map):
    return pl.BlockSpec((1, 1, TABLE_W), index_map, memory_space=pltpu.SMEM)


def _dispatch(seg_end, used_end, n_big, n_small, piece_table, route_t, h2):
    grid_spec = pltpu.PrefetchScalarGridSpec(
        num_scalar_prefetch=4,
        grid=(N_TOKEN_TILES,),
        in_specs=[_piece_spec(lambda i, se, ue, nb, ns: (i, 0, 0)),
                  pl.BlockSpec((ROUTE_W, OUT_TM), lambda i, se, ue, nb, ns: (i, 0)),
                  pl.BlockSpec((OUT_TM, D_MODEL), lambda i, se, ue, nb, ns: (i, 0))],
        out_specs=pl.BlockSpec(memory_space=pl.ANY),
        scratch_shapes=[pltpu.VMEM((2, LOCAL_ROWS, D_MODEL), BF16),
                        pltpu.VMEM((ZERO_ROWS, D_MODEL), BF16),
                        pltpu.SemaphoreType.DMA((2,)), pltpu.SemaphoreType.DMA],
    )
    return pl.pallas_call(
        _dispatch_kernel,
        grid_spec=grid_spec,
        out_shape=jax.ShapeDtypeStruct((N_ROWS, D_MODEL), BF16),
        compiler_params=_cparams(("arbitrary",)),
        name="dispatch",
    )(seg_end, used_end, n_big, n_small, piece_table, route_t, h2)


def _experts_kernel(te_ref, seg_ref, nxt_ref, nu_ref, xs_ref, wg_hbm, wu_hbm, wd_hbm, ys_ref,
                    wg_buf, wu_buf, wd_buf, wgu_b_ref, wd_b_ref, wsem):
    i = pl.program_id(0)
    used = i < nu_ref[0]
    slot = lax.rem(seg_ref[i], 2)

    def weight_copies(expert, s):
        return [pltpu.make_async_copy(wg_hbm.at[expert], wg_buf.at[s], wsem.at[s]),
                pltpu.make_async_copy(wu_hbm.at[expert], wu_buf.at[s], wsem.at[s]),
                pltpu.make_async_copy(wd_hbm.at[expert], wd_buf.at[s], wsem.at[s])]

    @pl.when(i == 0)
    def _():
        for cp in weight_copies(te_ref[0], 0):
            cp.start()

    @pl.when(used & ((i == 0) | (te_ref[i] != te_ref[jnp.maximum(i - 1, 0)])))
    def _():
        for cp in weight_copies(te_ref[i], slot):
            cp.wait()

        @pl.when(nxt_ref[i] >= 0)
        def _():
            for cp in weight_copies(nxt_ref[i], 1 - slot):
                cp.start()

        wgu_b_ref[:, 0:D_EXPERT] = wg_buf[slot].astype(BF16)
        wgu_b_ref[:, D_EXPERT:2 * D_EXPERT] = wu_buf[slot].astype(BF16)
        wd_b_ref[...] = wd_buf[slot].astype(BF16)

    @pl.when(used)
    def _():
        h = jnp.dot(xs_ref[...], wgu_b_ref[...], preferred_element_type=F32)
        act = (_silu(h[:, 0:D_EXPERT]) * h[:, D_EXPERT:2 * D_EXPERT]).astype(BF16)
        ys_ref[...] = jnp.dot(act, wd_b_ref[...], preferred_element_type=F32).astype(BF16)


def _experts(tile_expert, tile_segment, next_expert, n_used, xs, w_gate, w_up, w_down):
    row_tile = lambda i, te, sg, nx, nu: (jnp.minimum(i, nu[0] - 1), 0)
    n_prefetch = 4
    grid_spec = pltpu.PrefetchScalarGridSpec(
        num_scalar_prefetch=n_prefetch,
        grid=(N_TILES,),
        in_specs=[pl.BlockSpec((MOE_TM, D_MODEL), row_tile),
                  pl.BlockSpec(memory_space=pl.ANY), pl.BlockSpec(memory_space=pl.ANY),
                  pl.BlockSpec(memory_space=pl.ANY)],
        out_specs=pl.BlockSpec((MOE_TM, D_MODEL), row_tile),
        scratch_shapes=[pltpu.VMEM((2, D_MODEL, D_EXPERT), F32), pltpu.VMEM((2, D_MODEL, D_EXPERT), F32),
                        pltpu.VMEM((2, D_EXPERT, D_MODEL), F32),
                        pltpu.VMEM((D_MODEL, 2 * D_EXPERT), BF16), pltpu.VMEM((D_EXPERT, D_MODEL), BF16),
                        pltpu.SemaphoreType.DMA((2,))],
    )
    return pl.pallas_call(
        _experts_kernel,
        grid_spec=grid_spec,
        out_shape=jax.ShapeDtypeStruct((N_ROWS, D_MODEL), BF16),
        input_output_aliases={n_prefetch: 0},
        compiler_params=_cparams(("arbitrary",)),
        name="experts",
    )(tile_expert, tile_segment, next_expert, n_used, xs, w_gate, w_up, w_down)


def _combine_kernel(nb_ref, ns_ref, tab_ref, tab_next_ref, route_ref, x1_ref, g2_ref, ys_ref, o_ref,
                    gbuf_ref, sems):
    i = pl.program_id(0)
    last = pl.num_programs(0) - 1
    buf = lax.rem(i, 2)

    def piece(b):
        def make(local, sorted_row, rows):
            return pltpu.make_async_copy(
                ys_ref.at[pl.ds(pl.multiple_of(sorted_row, RUN_ALIGN), rows)],
                gbuf_ref.at[b, pl.ds(pl.multiple_of(local, RUN_ALIGN), rows)], sems.at[b])
        return make

    @pl.when(i == 0)
    def _():
        gbuf_ref[...] = jnp.zeros_like(gbuf_ref)
        _run_copies(tab_ref, nb_ref[0], ns_ref[0], piece(0), lambda cp: cp.start())

    nxt = jnp.minimum(i + 1, last)

    @pl.when(i < last)
    def _():
        _run_copies(tab_next_ref, nb_ref[nxt], ns_ref[nxt], piece(1 - buf), lambda cp: cp.start())

    rec = route_ref[...]
    slot = lax.broadcasted_iota(I32, (OUT_TM, LOCAL_ROWS), 1).astype(F32)
    weights = (jnp.where(slot == rec[:, 0:1], rec[:, 2:3], 0.0)
               + jnp.where(slot == rec[:, 1:2], rec[:, 3:4], 0.0)).astype(BF16)
    _run_copies(tab_ref, nb_ref[i], ns_ref[i], lambda lo, so, rows: piece(buf)(0, 0, rows), lambda cp: cp.wait())
    moe = jnp.dot(weights, gbuf_ref[buf], preferred_element_type=F32)
    o_ref[...] = x1_ref[...] + g2_ref[0] * moe


def _combine(n_big, n_small, piece_table, route, x1, mod3, ys):
    tm = OUT_TM
    steps_per_batch = SEQ // tm
    grid_spec = pltpu.PrefetchScalarGridSpec(
        num_scalar_prefetch=2,
        grid=(N_TOKEN_TILES,),
        in_specs=[_piece_spec(lambda i, nb, ns: (i, 0, 0)),
                  _piece_spec(lambda i, nb, ns: (jnp.minimum(i + 1, N_TOKEN_TILES - 1), 0, 0)),
                  pl.BlockSpec((tm, ROUTE_W), lambda i, nb, ns: (i, 0)),
                  pl.BlockSpec((tm, D_MODEL), lambda i, nb, ns: (i, 0)),
                  pl.BlockSpec((1, 1, D_MODEL), lambda i, nb, ns: ((i // steps_per_batch) * 6 + 5, 0, 0)),
                  pl.BlockSpec(memory_space=pl.ANY)],
        out_specs=pl.BlockSpec((tm, D_MODEL), lambda i, nb, ns: (i, 0)),
        scratch_shapes=[pltpu.VMEM((2, LOCAL_ROWS, D_MODEL), BF16), pltpu.SemaphoreType.DMA((2,))],
    )
    return pl.pallas_call(
        _combine_kernel,
        grid_spec=grid_spec,
        out_shape=jax.ShapeDtypeStruct((TOKENS, D_MODEL), F32),
        compiler_params=_cparams(("arbitrary",)),
        name="combine",
    )(n_big, n_small, piece_table, piece_table, route, x1, mod3, ys)


def kernel(x, c, positions, norm1_w, norm2_w, w_ada, b_ada, w_in, conv_w, conv_b, dt_bias, a_log,
           d_skip, ssd_norm_w, q_norm_w, k_norm_w, sinks, w_out, w_group, b_group, w_expert, b_expert,
           w_gate, w_up, w_down):
    assert x.shape == (BATCH, SEQ, D_MODEL) and w_in.shape == (D_MODEL, IN_WIDTH)
    x2d = x.reshape(TOKENS, D_MODEL)
    mod = _ada_mod(c, w_ada, b_ada)
    mod3 = mod.reshape(BATCH * 6, 1, D_MODEL)

    qkv, z, xbc, dt, dt_t = _in_proj(x2d, norm1_w, mod3, w_in)
    att = _attention(qkv, positions, q_norm_w, k_norm_w, sinks)
    y = _ssd(xbc, z, dt, dt_t, conv_w, conv_b, dt_bias, a_log, d_skip, ssd_norm_w)

    w_router = jnp.pad(jnp.concatenate([w_group, w_expert], axis=1).astype(F32),
                       ((0, 0), (0, LANES - ROUTER_COLS)))
    b_router = jnp.pad(jnp.concatenate([b_group, b_expert]).astype(F32),
                       (0, LANES - ROUTER_COLS)).reshape(1, LANES)
    x1, h2, route, route_t, tcnt = _out_router(att, y, x2d, mod3, w_out.astype(F32), norm2_w,
                                                w_router, b_router)

    tc = tcnt[:, 0, 0:N_EXPERTS].astype(I32)
    run_rows = ((tc + RUN_ALIGN - 1) // RUN_ALIGN) * RUN_ALIGN
    counts = jnp.sum(run_rows, axis=0)
    padded = ((counts + MOE_TM - 1) // MOE_TM) * MOE_TM
    seg_end = jnp.cumsum(padded)
    seg_start = seg_end - padded
    run_dst = seg_start[None, :] + jnp.cumsum(run_rows, axis=0) - run_rows
    n_used = (seg_end[-1] // MOE_TM).reshape(1)
    last_row = jnp.minimum(jnp.arange(N_TILES, dtype=I32) * MOE_TM, seg_end[-1] - 1)
    tile_expert = jnp.sum((seg_end[None, :] <= last_row[:, None]).astype(I32), axis=1)

    run_local = jnp.cumsum(run_rows, axis=1) - run_rows
    n_big_run = run_rows // BIG_PIECE
    n_small_run = (run_rows // RUN_ALIGN) % 2
    q = jnp.arange(PIECE_SLOTS, dtype=I32)
    experts = jnp.arange(N_EXPERTS, dtype=I32)

    def flat(per_run, local0, dst0, stride):
        end = jnp.cumsum(per_run, axis=1)
        run_of = jnp.sum((end[:, None, :] <= q[None, :, None]).astype(I32), axis=2)
        pick = (run_of[:, :, None] == experts[None, None, :]).astype(I32)
        k = q[None, :] - jnp.sum(pick * (end - per_run)[:, None, :], axis=2)
        local = jnp.sum(pick * local0[:, None, :], axis=2) + stride * k
        dst = jnp.sum(pick * dst0[:, None, :], axis=2) + stride * k
        return end[:, -1], local, dst

    n_big, big_local, big_dst = flat(n_big_run, run_local, run_dst, BIG_PIECE)
    n_small, small_local, small_dst = flat(n_small_run, run_local + BIG_PIECE * n_big_run,
                                           run_dst + BIG_PIECE * n_big_run, 0)
    piece_table = jnp.concatenate([big_local, big_dst, small_local, small_dst], axis=1)
    piece_table = piece_table.astype(I32).reshape(N_TOKEN_TILES, 1, TABLE_W)

    nonempty = padded > 0
    seg_rank = jnp.cumsum(nonempty.astype(I32)) - 1
    later = nonempty[None, :] & (experts[None, :] > experts[:, None])
    next_of = jnp.min(jnp.where(later, experts[None, :], N_EXPERTS), axis=1)
    next_of = jnp.where(next_of == N_EXPERTS, -1, next_of)
    tile_is = (tile_expert[:, None] == experts[None, :]).astype(I32)
    tile_segment = jnp.sum(tile_is * seg_rank[None, :], axis=1)
    next_expert = jnp.sum(tile_is * next_of[None, :], axis=1)

    n_big, n_small = n_big.astype(I32), n_small.astype(I32)
    xs = _dispatch(seg_end.astype(I32), (seg_start + counts).astype(I32), n_big, n_small, piece_table,
                   route_t, h2)
    ys = _experts(tile_expert, tile_segment.astype(I32), next_expert.astype(I32), n_used.astype(I32),
                  xs, w_gate, w_up, w_down)
    out = _combine(n_big, n_small, piece_table, route, x1, mod3, ys)
    return out.reshape(BATCH, SEQ, D_MODEL)
```

```python
import jax
import jax.numpy as jnp
from jax import lax
from jax.experimental import pallas as pl
from jax.experimental.pallas import tpu as pltpu

F32 = jnp.float32
BF16 = jnp.bfloat16
I32 = jnp.int32

D_MODEL = 1024
BATCH = 2
SEQ = 8192
TOKENS = BATCH * SEQ
ATT_HEADS = 8
ATT_KV_HEADS = 2
HEAD_DIM = 64
ATT_WIDTH = ATT_HEADS * HEAD_DIM
KV_WIDTH = ATT_KV_HEADS * HEAD_DIM
ATT_BLOCK = 128
ROPE_DIM = HEAD_DIM // 4
ROPE_THETA = 500000.0
SSD_HEADS = 8
SSD_HEAD_DIM = 64
SSD_WIDTH = SSD_HEADS * SSD_HEAD_DIM
SSD_GROUPS = 2
SSD_STATE = 128
CONV_K = 4
CHUNK = 128
XBC_WIDTH = SSD_WIDTH + 2 * SSD_GROUPS * SSD_STATE
IN_WIDTH = ATT_WIDTH + 2 * KV_WIDTH + SSD_WIDTH + XBC_WIDTH + SSD_HEADS
N_GROUPS = 4
EXPERTS_PER_GROUP = 8
N_EXPERTS = N_GROUPS * EXPERTS_PER_GROUP
TOP_K = 2
D_EXPERT = 256
EPS = 1e-6

LANES = 128
QKV_WIDTH = ATT_WIDTH + 2 * KV_WIDTH
IN_PAD = QKV_WIDTH + SSD_WIDTH + XBC_WIDTH + LANES
NEG_BIG = -1e30

VMEM_LIMIT = 48 * 1024 * 1024


def _cparams(sem):
    return pltpu.CompilerParams(dimension_semantics=sem, vmem_limit_bytes=VMEM_LIMIT)


def _split_bf16(x):
    hi = x.astype(BF16)
    lo = (x - hi.astype(F32)).astype(BF16)
    return hi, lo


ADA_TN = 768


def _ada_kernel(ct_ref, w_ref, b_ref, o_ref):
    ct = ct_ref[...]
    s = ct * jax.nn.sigmoid(ct)
    w = w_ref[...]
    rows = [jnp.sum(s[:, b:b + 1] * w, axis=0, keepdims=True) for b in range(BATCH)]
    o_ref[...] = jnp.concatenate(rows, axis=0) + b_ref[...]


def _ada_mod(c, w_ada, b_ada):
    n = w_ada.shape[1]
    return pl.pallas_call(
        _ada_kernel,
        grid=(n // ADA_TN,),
        in_specs=[pl.BlockSpec((D_MODEL, BATCH), lambda j: (0, 0)),
                  pl.BlockSpec((D_MODEL, ADA_TN), lambda j: (0, j)),
                  pl.BlockSpec((1, ADA_TN), lambda j: (0, j))],
        out_specs=pl.BlockSpec((BATCH, ADA_TN), lambda j: (0, j)),
        out_shape=jax.ShapeDtypeStruct((BATCH, n), F32),
        compiler_params=_cparams(("arbitrary",)),
        name="ada_mod",
    )(c.T, w_ada, b_ada.reshape(1, n))


INPROJ_TM = 512
_INPROJ_CHUNK = 256


def _inproj_kernel(x_ref, nw_ref, sc_ref, sh_ref, wf_ref, wdt_ref, qkv_ref, z_ref, xbc_ref, dt_ref, dtt_ref,
                   w_ref):
    @pl.when(pl.program_id(0) == 0)
    def _():
        for c0 in range(0, IN_PAD - LANES, _INPROJ_CHUNK):
            w_ref[:, c0:c0 + _INPROJ_CHUNK] = wf_ref[:, c0:c0 + _INPROJ_CHUNK].astype(BF16)
        w_ref[:, IN_PAD - LANES:IN_PAD] = wdt_ref[...].astype(BF16)

    x = x_ref[...]
    y = x * lax.rsqrt(jnp.mean(x * x, axis=-1, keepdims=True) + EPS)
    h = (y * nw_ref[...]) * (1.0 + sc_ref[0]) + sh_ref[0]
    hb = h.astype(BF16)

    def proj(c0, c1):
        return jnp.dot(hb, w_ref[:, c0:c1], preferred_element_type=F32)

    for c0 in range(0, QKV_WIDTH, _INPROJ_CHUNK):
        qkv_ref[:, c0:c0 + _INPROJ_CHUNK] = proj(c0, c0 + _INPROJ_CHUNK).astype(BF16)
    base = QKV_WIDTH
    for c0 in range(0, SSD_WIDTH, _INPROJ_CHUNK):
        z_ref[:, c0:c0 + _INPROJ_CHUNK] = proj(base + c0, base + c0 + _INPROJ_CHUNK).astype(BF16)
    base += SSD_WIDTH
    for c0 in range(0, XBC_WIDTH, _INPROJ_CHUNK):
        xbc_ref[:, c0:c0 + _INPROJ_CHUNK] = proj(base + c0, base + c0 + _INPROJ_CHUNK).astype(BF16)
    base += XBC_WIDTH
    dt = proj(base, base + LANES)
    dt_ref[...] = dt
    dtt_ref[...] = dt.T[0:SSD_HEADS, :]


def _in_proj(x2d, norm_w, mod3, w_in):
    tm = INPROJ_TM
    steps_per_batch = SEQ // tm
    w_dt = jnp.pad(w_in[:, IN_WIDTH - SSD_HEADS:IN_WIDTH].astype(F32), ((0, 0), (0, LANES - SSD_HEADS)))
    return pl.pallas_call(
        _inproj_kernel,
        grid=(TOKENS // tm,),
        in_specs=[pl.BlockSpec((tm, D_MODEL), lambda i: (i, 0)),
                  pl.BlockSpec((1, D_MODEL), lambda i: (0, 0)),
                  pl.BlockSpec((1, 1, D_MODEL), lambda i: ((i // steps_per_batch) * 6 + 1, 0, 0)),
                  pl.BlockSpec((1, 1, D_MODEL), lambda i: ((i // steps_per_batch) * 6 + 0, 0, 0)),
                  pl.BlockSpec((D_MODEL, IN_WIDTH), lambda i: (0, 0), pipeline_mode=pl.Buffered(1)),
                  pl.BlockSpec((D_MODEL, LANES), lambda i: (0, 0))],
        out_specs=[pl.BlockSpec((tm, QKV_WIDTH), lambda i: (i, 0)),
                   pl.BlockSpec((tm, SSD_WIDTH), lambda i: (i, 0)),
                   pl.BlockSpec((tm, XBC_WIDTH), lambda i: (i, 0)),
                   pl.BlockSpec((tm, LANES), lambda i: (i, 0)),
                   pl.BlockSpec((SSD_HEADS, tm), lambda i: (0, i))],
        out_shape=[jax.ShapeDtypeStruct((TOKENS, QKV_WIDTH), BF16),
                   jax.ShapeDtypeStruct((TOKENS, SSD_WIDTH), BF16),
                   jax.ShapeDtypeStruct((TOKENS, XBC_WIDTH), BF16),
                   jax.ShapeDtypeStruct((TOKENS, LANES), F32),
                   jax.ShapeDtypeStruct((SSD_HEADS, TOKENS), F32)],
        scratch_shapes=[pltpu.VMEM((D_MODEL, IN_PAD), BF16)],
        compiler_params=_cparams(("arbitrary",)),
        name="in_proj",
    )(x2d, norm_w.reshape(1, D_MODEL), mod3, mod3, w_in.astype(F32), w_dt)


ATT_SUB = 4


ROPE_TM = 2048
_ROPE_HALF = ROPE_DIM // 2
_TOK_PER_ROW = LANES // _ROPE_HALF


def _exact_dot(x, onehot_b):
    hi, lo = _split_bf16(x)
    return (jnp.dot(hi, onehot_b, preferred_element_type=F32)
            + jnp.dot(lo, onehot_b, preferred_element_type=F32))


def _rope_kernel(pos_ref, freq_ref, sel_ref, own_ref, gcos_ref, gs1_ref, gs2_ref, ident_ref,
                 cos_ref, s1_ref, s2_ref):
    ang = pos_ref[...].astype(F32) * freq_ref[...]
    cos_p, sin_p = jnp.cos(ang), jnp.sin(ang)
    hi_c, lo_c = _split_bf16(cos_p)
    hi_s, lo_s = _split_bf16(sin_p)
    sel = sel_ref[...]
    rows_c = jnp.dot(sel, hi_c, preferred_element_type=F32) + jnp.dot(sel, lo_c, preferred_element_type=F32)
    rows_s = jnp.dot(sel, hi_s, preferred_element_type=F32) + jnp.dot(sel, lo_s, preferred_element_type=F32)
    own = own_ref[...]
    cos_ref[...] = _exact_dot(rows_c * own, gcos_ref[...]) + ident_ref[...]
    s1_ref[...] = _exact_dot(rows_s * own, gs1_ref[...])
    s2_ref[...] = _exact_dot(rows_s * own, gs2_ref[...])


def _rope_tables(positions):
    half, per_row = _ROPE_HALF, _TOK_PER_ROW
    rows = ROPE_TM // per_row
    pos_rep = jnp.repeat(positions.reshape(TOKENS).astype(I32), half).reshape(TOKENS // per_row, LANES)
    inv_freq = jnp.power(ROPE_THETA, -jnp.arange(half, dtype=F32) * 2.0 / ROPE_DIM)
    freq = jnp.tile(inv_freq, per_row).reshape(1, LANES)
    tok = jnp.arange(ROPE_TM)
    lane = jnp.arange(LANES)
    sel = (tok[:, None] // per_row == jnp.arange(rows)[None, :]).astype(BF16)
    own = (lane[None, :] // half == tok[:, None] % per_row).astype(F32)
    d = lane % HEAD_DIM
    src_f = lane % half
    hits = lambda lo, hi: ((src_f[:, None] == d[None, :] % half) & (d[None, :] >= lo) & (d[None, :] < hi))
    gcos = hits(0, ROPE_DIM).astype(BF16)
    gs1 = -hits(0, half).astype(BF16)
    gs2 = hits(half, ROPE_DIM).astype(BF16)
    ident = (d >= ROPE_DIM).astype(F32).reshape(1, LANES)
    const = lambda shape: pl.BlockSpec(shape, lambda i: (0, 0))
    out_spec = pl.BlockSpec((ROPE_TM, LANES), lambda i: (i, 0))
    out = jax.ShapeDtypeStruct((TOKENS, LANES), F32)
    return pl.pallas_call(
        _rope_kernel,
        grid=(TOKENS // ROPE_TM,),
        in_specs=[pl.BlockSpec((rows, LANES), lambda i: (i, 0)), const((1, LANES)),
                  const((ROPE_TM, rows)), const((ROPE_TM, LANES)),
                  const((LANES, LANES)), const((LANES, LANES)), const((LANES, LANES)), const((1, LANES))],
        out_specs=[out_spec, out_spec, out_spec],
        out_shape=[out, out, out],
        compiler_params=_cparams(("arbitrary",)),
        name="rope_tables",
    )(pos_rep, freq, sel, own, gcos, gs1, gs2, ident)


def _seg_meansq(xf, ones128):
    rows, width = xf.shape
    nt = width // LANES
    parts = _split_bf16(xf * xf)
    stacked = jnp.concatenate([p[:, t * LANES:(t + 1) * LANES] for p in parts for t in range(nt)], axis=0)
    tot = jnp.dot(stacked, ones128, preferred_element_type=F32)
    tiles = [tot[t * rows:(t + 1) * rows] + tot[(nt + t) * rows:(nt + t + 1) * rows] for t in range(nt)]
    return jnp.concatenate(tiles, axis=1) * (1.0 / HEAD_DIM)


def _norm_rope(x_bf, w_row, ones_bd, cosf, s1, s2):
    xf = x_bf.astype(F32)
    width = xf.shape[1]
    xn = xf * lax.rsqrt(_seg_meansq(xf, ones_bd) + EPS) * w_row
    half = ROPE_DIM // 2
    up = pltpu.roll(xn, width - half, axis=1)
    down = pltpu.roll(xn, half, axis=1)
    return xn * cosf + up * s1 + down * s2


def _attn_kernel(sink_ref, q_ref, kv_ref, cos_ref, s1_ref, s2_ref, qw_ref, kw_ref,
                 ones_ref, o_ref, kprev_ref, vprev_ref):
    j = pl.program_id(1)
    blk = ATT_BLOCK

    @pl.when(j == 0)
    def _():
        kprev_ref[...] = jnp.zeros_like(kprev_ref)
        vprev_ref[...] = jnp.zeros_like(vprev_ref)

    cos1 = cos_ref[...]
    s1_1 = s1_ref[...]
    s2_1 = s2_ref[...]
    reps = ATT_WIDTH // LANES
    cosq = jnp.concatenate([cos1] * reps, axis=1)
    s1q = jnp.concatenate([s1_1] * reps, axis=1)
    s2q = jnp.concatenate([s2_1] * reps, axis=1)

    q = _norm_rope(q_ref[...], qw_ref[...], ones_ref[...], cosq, s1q, s2q)
    qf = q * (HEAD_DIM ** -0.5)
    kv = kv_ref[...]
    kn = _norm_rope(kv[:, 0:KV_WIDTH], kw_ref[...], ones_ref[...], cos1, s1_1, s2_1)
    vn = kv[:, KV_WIDTH:2 * KV_WIDTH].astype(F32)

    kall = jnp.concatenate([kprev_ref[...], kn], axis=0)
    vall = jnp.concatenate([vprev_ref[...], vn], axis=0)
    kprev_ref[...] = kn[(ATT_SUB - 1) * blk:ATT_SUB * blk]
    vprev_ref[...] = vn[(ATT_SUB - 1) * blk:ATT_SUB * blk]

    lo_all = lax.broadcasted_iota(I32, kall.shape, 1) < HEAD_DIM
    ones_all = jnp.ones(kall.shape, BF16)

    row = lax.broadcasted_iota(I32, (2 * blk, blk), 0)
    col = lax.broadcasted_iota(I32, (2 * blk, blk), 1)
    from_prev = col > (row & (blk - 1))
    second_tile = lax.broadcasted_iota(I32, (2 * blk, 1), 0) >= blk
    zero_p = jnp.zeros((2 * blk, blk), F32)

    k_par, v_par = [], []
    for g in range(ATT_KV_HEADS):
        keep = lo_all if g == 0 else ~lo_all
        k_own = jnp.where(keep, kall, 0.0)
        v_own = jnp.where(keep, vall, 0.0)
        k_oth = pltpu.roll(k_own, HEAD_DIM, axis=1)
        v_oth = pltpu.roll(v_own, HEAD_DIM, axis=1)
        k_lo, k_hi = (k_own, k_oth) if g == 0 else (k_oth, k_own)
        v_lo, v_hi = (v_own, v_oth) if g == 0 else (v_oth, v_own)
        k_par.append((k_lo.astype(BF16), k_hi.astype(BF16)))
        v_par.append((jnp.concatenate([v_lo.astype(BF16), ones_all], axis=1),
                      jnp.concatenate([v_hi.astype(BF16), ones_all], axis=1)))

    problems = [(g, sub) for g in range(ATT_KV_HEADS) for sub in range(ATT_SUB)]
    scores = []
    for g, sub in problems:
        r0, c0 = sub * blk, g * 2 * LANES
        qcat = jnp.concatenate([qf[r0:r0 + blk, c0:c0 + LANES],
                                qf[r0:r0 + blk, c0 + LANES:c0 + 2 * LANES]], axis=0).astype(BF16)
        kw = jnp.concatenate([k_par[g][0][r0:r0 + 2 * blk], k_par[g][1][r0:r0 + 2 * blk]], axis=0)
        scores.append(lax.dot_general(qcat, kw, (((1,), (1,)), ((), ())),
                                      preferred_element_type=F32))

    weights, rescale = [], []
    for (g, sub), s_all in zip(problems, scores):
        for par in range(2):
            s = s_all[:, par * 2 * blk:(par + 1) * 2 * blk]
            s_prev = s[:, 0:blk]
            if sub == 0:
                s_prev = s_prev + jnp.where(j > 0, 0.0, NEG_BIG)
            s = jnp.where(from_prev, s_prev, s[:, blk:2 * blk])
            h_first = ATT_HEADS // ATT_KV_HEADS * g + par
            sink = jnp.where(second_tile, sink_ref[h_first + 2], sink_ref[h_first])
            m = jnp.maximum(jnp.max(s, axis=-1, keepdims=True), sink)
            p = jnp.exp(s - m)
            weights.append(jnp.concatenate([jnp.where(from_prev, p, zero_p), jnp.where(from_prev, zero_p, p)],
                                           axis=1).astype(BF16))
            rescale.append(jnp.exp(sink - m))

    outs = []
    for idx, (g, sub) in enumerate(problems):
        for par in range(2):
            outs.append(jnp.dot(weights[2 * idx + par], v_par[g][par][sub * blk:(sub + 2) * blk],
                                preferred_element_type=F32))

    for idx, (g, sub) in enumerate(problems):
        r0, c0 = sub * blk, g * 2 * LANES
        pair = None
        for par in range(2):
            o = outs[2 * idx + par]
            part = o[:, 0:LANES] * (1.0 / (o[:, LANES:2 * LANES] + rescale[2 * idx + par]))
            pair = part if pair is None else pair + part
        o_ref[r0:r0 + blk, c0:c0 + LANES] = pair[0:blk].astype(BF16)
        o_ref[r0:r0 + blk, c0 + LANES:c0 + 2 * LANES] = pair[blk:2 * blk].astype(BF16)


def _attention(qkv, positions, q_norm_w, k_norm_w, sinks):
    cosf, s1, s2 = _rope_tables(positions)
    qw = jnp.tile(q_norm_w.astype(F32), ATT_HEADS).reshape(1, ATT_WIDTH)
    kw = jnp.tile(k_norm_w.astype(F32), ATT_KV_HEADS).reshape(1, KV_WIDTH)
    seg = jnp.arange(LANES) // HEAD_DIM
    ones128 = (seg[:, None] == seg[None, :]).astype(BF16)
    const = lambda shape: pl.BlockSpec(shape, lambda b, j, s: (0, 0))
    rows = ATT_SUB * ATT_BLOCK
    nb = SEQ // rows
    tok = lambda width, cb: pl.BlockSpec((rows, width), lambda b, j, s: (b * nb + j, cb))
    grid_spec = pltpu.PrefetchScalarGridSpec(
        num_scalar_prefetch=1,
        grid=(BATCH, nb),
        in_specs=[tok(ATT_WIDTH, 0), tok(2 * KV_WIDTH, 2), tok(LANES, 0), tok(LANES, 0), tok(LANES, 0),
                  const((1, ATT_WIDTH)), const((1, KV_WIDTH)), const((LANES, LANES))],
        out_specs=tok(ATT_WIDTH, 0),
        scratch_shapes=[pltpu.VMEM((ATT_BLOCK, KV_WIDTH), F32),
                        pltpu.VMEM((ATT_BLOCK, KV_WIDTH), F32)],
    )
    return pl.pallas_call(
        _attn_kernel,
        grid_spec=grid_spec,
        out_shape=jax.ShapeDtypeStruct((TOKENS, ATT_WIDTH), BF16),
        compiler_params=_cparams(("arbitrary", "arbitrary")),
        name="attention",
    )(sinks.astype(F32), qkv, qkv, cosf, s1, s2, qw, kw, ones128)


def _softplus(x):
    return jnp.maximum(x, 0.0) + jnp.log1p(jnp.exp(-jnp.abs(x)))


def _silu(x):
    h = 0.5 * x
    return h + h * jnp.tanh(h)


def _ssd_kernel(xbc_ref, z_ref, dt_ref, dtt_ref, cw_ref, cb_ref, dtb_row_ref, dtb_col_ref,
                alog_row_ref, alog_col_ref, dskip_ref, nw_ref, tril_ref, triu_ref,
                o_ref, conv_ref, state_ref):
    c = pl.program_id(1)
    L = CHUNK
    tail = 8

    @pl.when(c == 0)
    def _():
        conv_ref[0:tail, :] = jnp.zeros((tail, XBC_WIDTH), F32)
        state_ref[...] = jnp.zeros_like(state_ref)

    xb = xbc_ref[...].astype(F32)
    conv_ref[tail:tail + L, :] = xb
    acc = cb_ref[...] + cw_ref[CONV_K - 1:CONV_K, :] * xb
    for k in range(CONV_K - 1):
        off = tail - (CONV_K - 1) + k
        acc = acc + cw_ref[k:k + 1, :] * conv_ref[off:off + L, :]
    conv_ref[0:tail, :] = xb[L - tail:L, :]
    u = _silu(acc)
    xs = u[:, 0:SSD_WIDTH]
    bmat = u[:, SSD_WIDTH:SSD_WIDTH + SSD_GROUPS * SSD_STATE]
    cmat = u[:, SSD_WIDTH + SSD_GROUPS * SSD_STATE:XBC_WIDTH]

    dt = _softplus(dt_ref[...] + dtb_row_ref[...])
    a = dt * (-jnp.exp(alog_row_ref[...]))
    a_hi, a_lo = _split_bf16(a)
    a_cum = (jnp.dot(tril_ref[...], a_hi, preferred_element_type=F32)
             + jnp.dot(tril_ref[...], a_lo, preferred_element_type=F32))
    dt_t = _softplus(dtt_ref[...] + dtb_col_ref[...])
    a_t = dt_t * (-jnp.exp(alog_col_ref[...]))
    at_hi, at_lo = _split_bf16(a_t)
    a_cum_t = (jnp.dot(at_hi, triu_ref[...], preferred_element_type=F32)
               + jnp.dot(at_lo, triu_ref[...], preferred_element_type=F32))
    exp_a_cum = jnp.exp(a_cum)
    shifted_t = a_cum_t - jnp.log(dt_t)
    a_end_t = a_cum_t[:, L - 1:L]
    wst_t = jnp.exp(a_end_t - a_cum_t) * dt_t
    cdec_t = jnp.exp(a_end_t)

    row = lax.broadcasted_iota(I32, (L, L), 0)
    col = lax.broadcasted_iota(I32, (L, L), 1)
    causal = col <= row
    lane = lax.broadcasted_iota(I32, (L, LANES), 1)
    lo_half = lane < SSD_HEAD_DIM

    xs_b = xs.astype(BF16)
    heads_per_group = SSD_HEADS // SSD_GROUPS
    gated = []
    for g in range(SSD_GROUPS):
        b_g = bmat[:, g * SSD_STATE:(g + 1) * SSD_STATE]
        c_g = cmat[:, g * SSD_STATE:(g + 1) * SSD_STATE]
        cb = lax.dot_general(c_g.astype(BF16), b_g.astype(BF16), (((1,), (1,)), ((), ())),
                             preferred_element_type=F32)
        b_gt = b_g.T
        for t in range(heads_per_group // 2):
            tile = g * (heads_per_group // 2) + t
            c0 = tile * LANES
            xs_tile = xs_b[:, c0:c0 + LANES]
            st_tile = state_ref[:, c0:c0 + LANES]
            st_b = st_tile.astype(BF16)
            y_tile = jnp.zeros((L, LANES), F32)
            new_tile = jnp.zeros((SSD_STATE, LANES), F32)
            for e in range(2):
                h = 2 * tile + e
                keep = lo_half if e == 0 else ~lo_half
                colb = jnp.broadcast_to(a_cum[:, h:h + 1], (L, L))
                rowb = shifted_t[h:h + 1, :]
                w_in = cb * jnp.exp(jnp.where(causal, colb - rowb, NEG_BIG))
                w_off = c_g * jnp.broadcast_to(exp_a_cum[:, h:h + 1], (L, L))
                lhs = jnp.concatenate([w_in, w_off], axis=1).astype(BF16)
                rhs = jnp.concatenate([jnp.where(keep, xs_tile, jnp.zeros_like(xs_tile)),
                                       jnp.where(keep, st_b, jnp.zeros_like(st_b))], axis=0)
                y_tile = y_tile + jnp.dot(lhs, rhs, preferred_element_type=F32)
                m_h = (b_gt * wst_t[h:h + 1, :]).astype(BF16)
                new_tile = new_tile + jnp.dot(m_h, jnp.where(keep, xs_tile, jnp.zeros_like(xs_tile)),
                                              preferred_element_type=F32)
            cd = jnp.where(lo_half[0:1, :], cdec_t[2 * tile:2 * tile + 1, :],
                           cdec_t[2 * tile + 1:2 * tile + 2, :])
            state_ref[:, c0:c0 + LANES] = st_tile * cd + new_tile
            y_full = y_tile + dskip_ref[:, c0:c0 + LANES] * xs[:, c0:c0 + LANES]
            gated.append(y_full * _silu(z_ref[:, c0:c0 + LANES].astype(F32)))

    gw = SSD_WIDTH // SSD_GROUPS
    tiles_per_group = gw // LANES
    for g in range(SSD_GROUPS):
        yg = jnp.concatenate(gated[g * tiles_per_group:(g + 1) * tiles_per_group], axis=1)
        ms = jnp.mean(yg * yg, axis=-1, keepdims=True)
        o_ref[:, g * gw:(g + 1) * gw] = ((yg * lax.rsqrt(ms + EPS)) * nw_ref[:, g * gw:(g + 1) * gw]).astype(o_ref.dtype)


def _ssd(xbc, z, dt, dt_t, conv_w, conv_b, dt_bias, a_log, d_skip, ssd_norm_w):
    nc = SEQ // CHUNK
    L = CHUNK
    pad_row = lambda v: jnp.pad(v.astype(F32), (0, LANES - SSD_HEADS)).reshape(1, LANES)
    col8 = lambda v: v.astype(F32).reshape(SSD_HEADS, 1)
    idx = jnp.arange(L)
    tril = (idx[None, :] <= idx[:, None]).astype(BF16)
    triu = (idx[:, None] <= idx[None, :]).astype(BF16)
    dskip = jnp.repeat(d_skip.astype(F32), SSD_HEAD_DIM).reshape(1, SSD_WIDTH)
    const = lambda shape: pl.BlockSpec(shape, lambda b, c: (0, 0))
    tok = lambda width: pl.BlockSpec((L, width), lambda b, c: (b * nc + c, 0))
    return pl.pallas_call(
        _ssd_kernel,
        grid=(BATCH, nc),
        in_specs=[tok(XBC_WIDTH), tok(SSD_WIDTH), tok(LANES),
                  pl.BlockSpec((SSD_HEADS, L), lambda b, c: (0, b * nc + c)),
                  const((CONV_K, XBC_WIDTH)), const((1, XBC_WIDTH)),
                  const((1, LANES)), const((SSD_HEADS, 1)), const((1, LANES)), const((SSD_HEADS, 1)),
                  const((1, SSD_WIDTH)), const((1, SSD_WIDTH)), const((L, L)), const((L, L))],
        out_specs=tok(SSD_WIDTH),
        out_shape=jax.ShapeDtypeStruct((TOKENS, SSD_WIDTH), BF16),
        scratch_shapes=[pltpu.VMEM((8 + L, XBC_WIDTH), F32),
                        pltpu.VMEM((SSD_STATE, SSD_WIDTH), F32)],
        compiler_params=_cparams(("arbitrary", "arbitrary")),
        name="ssd",
    )(xbc, z, dt, dt_t, conv_w.astype(F32), conv_b.astype(F32).reshape(1, XBC_WIDTH),
      pad_row(dt_bias), col8(dt_bias), pad_row(a_log), col8(a_log), dskip,
      ssd_norm_w.astype(F32).reshape(1, SSD_WIDTH), tril, triu)


OUT_TM = 512
ROUTE_W = 8
ROUTER_COLS = N_GROUPS + N_EXPERTS
RUN_ALIGN = 16
RUN_SHIFT = 4
LOCAL_ROWS = 1536
assert RUN_ALIGN == 1 << RUN_SHIFT and LOCAL_ROWS >= TOP_K * OUT_TM + N_EXPERTS * (RUN_ALIGN - 1)


def _lane_pick(values, lane, index):
    return jnp.sum(jnp.where(lane == index, values, 0.0), axis=-1, keepdims=True)


def _first_argmax(vals, lane):
    m = jnp.max(vals, axis=-1, keepdims=True)
    idx = jnp.min(jnp.where(vals == m, lane, float(LANES)), axis=-1, keepdims=True)
    return m, idx


def _out_router_kernel(att_ref, y_ref, x_ref, g1_ref, wof_ref, nw_ref, sc_ref, sh_ref, wr_ref, br_ref,
                       ltri_ref, sut_ref, x1_ref, h2_ref, route_ref, routet_ref, tcnt_ref,
                       wr_split_ref, logits_ref, wo_ref):
    i = pl.program_id(0)

    @pl.when(i == 0)
    def _():
        hi, lo = _split_bf16(wr_ref[...])
        wr_split_ref[:, 0:LANES] = hi
        wr_split_ref[:, LANES:2 * LANES] = lo
        logits_ref[...] = jnp.zeros_like(logits_ref)
        for r0 in range(0, D_MODEL, 256):
            wo_ref[r0:r0 + 256, :] = wof_ref[r0:r0 + 256, :].astype(BF16)

    logits = logits_ref[...]

    mixer = (jnp.dot(att_ref[...], wo_ref[0:ATT_WIDTH, :], preferred_element_type=F32)
             + jnp.dot(y_ref[...], wo_ref[ATT_WIDTH:ATT_WIDTH + SSD_WIDTH, :], preferred_element_type=F32))
    x1 = x_ref[...] + g1_ref[0] * mixer
    x1_ref[...] = x1
    yn = x1 * lax.rsqrt(jnp.mean(x1 * x1, axis=-1, keepdims=True) + EPS)
    h2 = (yn * nw_ref[...]) * (1.0 + sc_ref[0]) + sh_ref[0]
    h2_ref[...] = h2.astype(BF16)

    h_hi, h_lo = _split_bf16(h2)
    both = jnp.dot(h_hi, wr_split_ref[...], preferred_element_type=F32)
    logits_ref[...] = (both[:, 0:LANES] + both[:, LANES:2 * LANES]
                       + jnp.dot(h_lo, wr_split_ref[:, 0:LANES], preferred_element_type=F32)) + br_ref[...]

    tm = logits.shape[0]
    lane = lax.broadcasted_iota(I32, (tm, LANES), 1).astype(F32)

    gl = jnp.where(lane < N_GROUPS, logits, NEG_BIG)
    gmax, gidx = _first_argmax(gl, lane)
    g_p = 1.0 / jnp.sum(jnp.exp(gl - gmax), axis=-1, keepdims=True)

    lo_lane = N_GROUPS + EXPERTS_PER_GROUP * gidx
    el = jnp.where((lane >= lo_lane) & (lane < lo_lane + EXPERTS_PER_GROUP), logits, NEG_BIG)
    m1, i1 = _first_argmax(el, lane)
    m2, i2 = _first_argmax(jnp.where(lane == i1, NEG_BIG, el), lane)
    r = jnp.exp(m2 - m1)
    p1 = 1.0 / (1.0 + r)
    p2 = r / (1.0 + r)
    e0 = i1 - N_GROUPS
    e1 = i2 - N_GROUPS

    onehot = ((lane == e0) | (lane == e1)).astype(F32)
    tile_cnt = jnp.sum(onehot, axis=0, keepdims=True)
    run_len = jnp.floor((tile_cnt + (RUN_ALIGN - 1)) * (1.0 / RUN_ALIGN)) * RUN_ALIGN
    run_start = jnp.dot(jnp.broadcast_to(run_len, (8, LANES)).astype(BF16), sut_ref[...],
                        preferred_element_type=F32)[0:1, :]
    before = jnp.dot(ltri_ref[...], onehot.astype(BF16), preferred_element_type=F32) + run_start
    slot0 = _lane_pick(before, lane, e0)
    slot1 = _lane_pick(before, lane, e1)
    tcnt_ref[0] = tile_cnt

    rec = jnp.zeros((tm, LANES), F32)
    for k, v in enumerate([slot0, slot1, g_p * p1, g_p * p2, e0, e1]):
        rec = jnp.where(lane == k, v, rec)
    route_ref[...] = rec[:, 0:ROUTE_W]
    routet_ref[...] = rec.T[0:ROUTE_W, :]


def _out_router(att, y, x2d, mod3, w_out_b, norm_w, w_router, b_router):
    tm = OUT_TM
    n_steps = TOKENS // tm
    steps_per_batch = SEQ // tm
    idx = jnp.arange(tm)
    ltri = (idx[None, :] < idx[:, None]).astype(BF16)
    lidx = jnp.arange(LANES)
    sut = (lidx[:, None] < lidx[None, :]).astype(BF16)
    const = lambda shape: pl.BlockSpec(shape, lambda i: (0, 0))
    cur = lambda i: jnp.minimum(i, n_steps - 1)
    prev = lambda i: jnp.maximum(i - 1, 0)
    tok = lambda width: pl.BlockSpec((tm, width), lambda i: (cur(i), 0))
    modspec = lambda k: pl.BlockSpec((1, 1, D_MODEL), lambda i: ((cur(i) // steps_per_batch) * 6 + k, 0, 0))
    return pl.pallas_call(
        _out_router_kernel,
        grid=(n_steps + 1,),
        in_specs=[tok(ATT_WIDTH), tok(SSD_WIDTH), tok(D_MODEL), modspec(2),
                  const((D_MODEL, D_MODEL)), const((1, D_MODEL)), modspec(4), modspec(3),
                  const((D_MODEL, LANES)), const((1, LANES)), const((tm, tm)), const((LANES, LANES))],
        out_specs=[tok(D_MODEL), tok(D_MODEL),
                   pl.BlockSpec((tm, ROUTE_W), lambda i: (prev(i), 0)),
                   pl.BlockSpec((ROUTE_W, tm), lambda i: (prev(i), 0)),
                   pl.BlockSpec((1, 1, LANES), lambda i: (prev(i), 0, 0))],
        out_shape=[jax.ShapeDtypeStruct((TOKENS, D_MODEL), F32),
                   jax.ShapeDtypeStruct((TOKENS, D_MODEL), BF16),
                   jax.ShapeDtypeStruct((TOKENS, ROUTE_W), F32),
                   jax.ShapeDtypeStruct((n_steps * ROUTE_W, tm), F32),
                   jax.ShapeDtypeStruct((n_steps, 1, LANES), F32)],
        scratch_shapes=[pltpu.VMEM((D_MODEL, 2 * LANES), BF16), pltpu.VMEM((tm, LANES), F32),
                        pltpu.VMEM((D_MODEL, D_MODEL), BF16)],
        compiler_params=_cparams(("arbitrary",)),
        name="out_router",
    )(att, y, x2d, mod3, w_out_b, norm_w.reshape(1, D_MODEL), mod3, mod3, w_router, b_router, ltri, sut)


MOE_TM = 512
ZERO_ROWS = 256
N_TOKEN_TILES = TOKENS // OUT_TM
MAX_SORTED_ROWS = TOKENS * TOP_K + N_TOKEN_TILES * N_EXPERTS * (RUN_ALIGN - 1)
N_TILES = MAX_SORTED_ROWS // MOE_TM + N_EXPERTS
N_ROWS = N_TILES * MOE_TM
assert MOE_TM % ZERO_ROWS == 0


BIG_PIECE = 2 * RUN_ALIGN
PIECE_SLOTS = LOCAL_ROWS // BIG_PIECE
TABLE_W = 4 * PIECE_SLOTS
assert PIECE_SLOTS >= N_EXPERTS


def _run_copies(table_ref, n_big, n_small, make_copy, action):
    def big(q, carry):
        action(make_copy(table_ref[0, 0, q], table_ref[0, 0, PIECE_SLOTS + q], BIG_PIECE))
        return carry

    def small(q, carry):
        action(make_copy(table_ref[0, 0, 2 * PIECE_SLOTS + q], table_ref[0, 0, 3 * PIECE_SLOTS + q], RUN_ALIGN))
        return carry

    lax.fori_loop(0, n_big, big, 0)
    lax.fori_loop(0, n_small, small, 0)


def _dispatch_kernel(seg_end_ref, used_end_ref, nb_ref, ns_ref, tab_ref, routet_ref, h2_ref, xs_ref,
                     sbuf_ref, zero_ref, sems, zsem):
    i = pl.program_id(0)
    last = pl.num_programs(0) - 1
    buf = lax.rem(i, 2)

    def zero_fills(action):
        def tail_copy(row):
            return pltpu.make_async_copy(zero_ref.at[pl.ds(0, RUN_ALIGN)],
                                         xs_ref.at[pl.ds(pl.multiple_of(row, RUN_ALIGN), RUN_ALIGN)], zsem)

        def block_copy(block):
            start = pl.multiple_of(block * ZERO_ROWS, ZERO_ROWS)
            return pltpu.make_async_copy(zero_ref, xs_ref.at[pl.ds(start, ZERO_ROWS)], zsem)

        def tails(e, carry):
            def body(r, c):
                action(tail_copy(r * RUN_ALIGN))
                return c

            lax.fori_loop(used_end_ref[e] // RUN_ALIGN, seg_end_ref[e] // RUN_ALIGN, body, 0)
            return carry

        def blocks(block, carry):
            action(block_copy(block))
            return carry

        lax.fori_loop(0, N_EXPERTS, tails, 0)
        lax.fori_loop(seg_end_ref[N_EXPERTS - 1] // ZERO_ROWS, N_ROWS // ZERO_ROWS, blocks, 0)

    @pl.when(i == 0)
    def _():
        zero_ref[...] = jnp.zeros_like(zero_ref)
        zero_fills(lambda cp: cp.start())

    slot = lax.broadcasted_iota(I32, (LOCAL_ROWS, OUT_TM), 0).astype(F32)
    perm = jnp.where((slot == routet_ref[0:1, :]) | (slot == routet_ref[1:2, :]), 1.0, 0.0).astype(BF16)
    sbuf_ref[buf] = jnp.dot(perm, h2_ref[...], preferred_element_type=F32).astype(BF16)

    def piece(b):
        def make(local, sorted_row, rows):
            return pltpu.make_async_copy(
                sbuf_ref.at[b, pl.ds(pl.multiple_of(local, RUN_ALIGN), rows)],
                xs_ref.at[pl.ds(pl.multiple_of(sorted_row, RUN_ALIGN), rows)], sems.at[b])
        return make

    _run_copies(tab_ref, nb_ref[i], ns_ref[i], piece(buf), lambda cp: cp.start())
    prev = jnp.maximum(i - 1, 0)

    @pl.when(i > 0)
    def _():
        _run_copies(tab_ref, nb_ref[prev], ns_ref[prev], lambda lo, so, rows: piece(1 - buf)(0, 0, rows),
                    lambda cp: cp.wait())

    @pl.when(i == last)
    def _():
        _run_copies(tab_ref, nb_ref[i], ns_ref[i], lambda lo, so, rows: piece(buf)(0, 0, rows),
                    lambda cp: cp.wait())
        zero_fills(lambda cp: cp.wait())


def _piece_spec(index_map):
    return pl.BlockSpec((1, 1, TABLE_W), index_map, memory_space=pltpu.SMEM)


def _dispatch(seg_end, used_end, n_big, n_small, piece_table, route_t, h2):
    grid_spec = pltpu.PrefetchScalarGridSpec(
        num_scalar_prefetch=4,
        grid=(N_TOKEN_TILES,),
        in_specs=[_piece_spec(lambda i, se, ue, nb, ns: (i, 0, 0)),
                  pl.BlockSpec((ROUTE_W, OUT_TM), lambda i, se, ue, nb, ns: (i, 0)),
                  pl.BlockSpec((OUT_TM, D_MODEL), lambda i, se, ue, nb, ns: (i, 0))],
        out_specs=pl.BlockSpec(memory_space=pl.ANY),
        scratch_shapes=[pltpu.VMEM((2, LOCAL_ROWS, D_MODEL), BF16),
                        pltpu.VMEM((ZERO_ROWS, D_MODEL), BF16),
                        pltpu.SemaphoreType.DMA((2,)), pltpu.SemaphoreType.DMA],
    )
    return pl.pallas_call(
        _dispatch_kernel,
        grid_spec=grid_spec,
        out_shape=jax.ShapeDtypeStruct((N_ROWS, D_MODEL), BF16),
        compiler_params=_cparams(("arbitrary",)),
        name="dispatch",
    )(seg_end, used_end, n_big, n_small, piece_table, route_t, h2)


def _experts_kernel(te_ref, seg_ref, nxt_ref, nu_ref, xs_ref, wg_hbm, wu_hbm, wd_hbm, ys_ref,
                    wg_buf, wu_buf, wd_buf, wgu_b_ref, wd_b_ref, wsem):
    i = pl.program_id(0)
    used = i < nu_ref[0]
    slot = lax.rem(seg_ref[i], 2)

    def weight_copies(expert, s):
        return [pltpu.make_async_copy(wg_hbm.at[expert], wg_buf.at[s], wsem.at[s]),
                pltpu.make_async_copy(wu_hbm.at[expert], wu_buf.at[s], wsem.at[s]),
                pltpu.make_async_copy(wd_hbm.at[expert], wd_buf.at[s], wsem.at[s])]

    @pl.when(i == 0)
    def _():
        for cp in weight_copies(te_ref[0], 0):
            cp.start()

    @pl.when(used & ((i == 0) | (te_ref[i] != te_ref[jnp.maximum(i - 1, 0)])))
    def _():
        for cp in weight_copies(te_ref[i], slot):
            cp.wait()

        @pl.when(nxt_ref[i] >= 0)
        def _():
            for cp in weight_copies(nxt_ref[i], 1 - slot):
                cp.start()

        wgu_b_ref[:, 0:D_EXPERT] = wg_buf[slot].astype(BF16)
        wgu_b_ref[:, D_EXPERT:2 * D_EXPERT] = wu_buf[slot].astype(BF16)
        wd_b_ref[...] = wd_buf[slot].astype(BF16)

    @pl.when(used)
    def _():
        h = jnp.dot(xs_ref[...], wgu_b_ref[...], preferred_element_type=F32)
        act = (_silu(h[:, 0:D_EXPERT]) * h[:, D_EXPERT:2 * D_EXPERT]).astype(BF16)
        ys_ref[...] = jnp.dot(act, wd_b_ref[...], preferred_element_type=F32).astype(BF16)


def _experts(tile_expert, tile_segment, next_expert, n_used, xs, w_gate, w_up, w_down):
    row_tile = lambda i, te, sg, nx, nu: (jnp.minimum(i, nu[0] - 1), 0)
    n_prefetch = 4
    grid_spec = pltpu.PrefetchScalarGridSpec(
        num_scalar_prefetch=n_prefetch,
        grid=(N_TILES,),
        in_specs=[pl.BlockSpec((MOE_TM, D_MODEL), row_tile),
                  pl.BlockSpec(memory_space=pl.ANY), pl.BlockSpec(memory_space=pl.ANY),
                  pl.BlockSpec(memory_space=pl.ANY)],
        out_specs=pl.BlockSpec((MOE_TM, D_MODEL), row_tile),
        scratch_shapes=[pltpu.VMEM((2, D_MODEL, D_EXPERT), F32), pltpu.VMEM((2, D_MODEL, D_EXPERT), F32),
                        pltpu.VMEM((2, D_EXPERT, D_MODEL), F32),
                        pltpu.VMEM((D_MODEL, 2 * D_EXPERT), BF16), pltpu.VMEM((D_EXPERT, D_MODEL), BF16),
                        pltpu.SemaphoreType.DMA((2,))],
    )
    return pl.pallas_call(
        _experts_kernel,
        grid_spec=grid_spec,
        out_shape=jax.ShapeDtypeStruct((N_ROWS, D_MODEL), BF16),
        input_output_aliases={n_prefetch: 0},
        compiler_params=_cparams(("arbitrary",)),
        name="experts",
    )(tile_expert, tile_segment, next_expert, n_used, xs, w_gate, w_up, w_down)


def _combine_kernel(nb_ref, ns_ref, tab_ref, tab_next_ref, route_ref, x1_ref, g2_ref, ys_ref, o_ref,
                    gbuf_ref, sems):
    i = pl.program_id(0)
    last = pl.num_programs(0) - 1
    buf = lax.rem(i, 2)

    def piece(b):
        def make(local, sorted_row, rows):
            return pltpu.make_async_copy(
                ys_ref.at[pl.ds(pl.multiple_of(sorted_row, RUN_ALIGN), rows)],
                gbuf_ref.at[b, pl.ds(pl.multiple_of(local, RUN_ALIGN), rows)], sems.at[b])
        return make

    @pl.when(i == 0)
    def _():
        gbuf_ref[...] = jnp.zeros_like(gbuf_ref)
        _run_copies(tab_ref, nb_ref[0], ns_ref[0], piece(0), lambda cp: cp.start())

    nxt = jnp.minimum(i + 1, last)

    @pl.when(i < last)
    def _():
        _run_copies(tab_next_ref, nb_ref[nxt], ns_ref[nxt], piece(1 - buf), lambda cp: cp.start())

    rec = route_ref[...]
    slot = lax.broadcasted_iota(I32, (OUT_TM, LOCAL_ROWS), 1).astype(F32)
    weights = (jnp.where(slot == rec[:, 0:1], rec[:, 2:3], 0.0)
               + jnp.where(slot == rec[:, 1:2], rec[:, 3:4], 0.0)).astype(BF16)
    _run_copies(tab_ref, nb_ref[i], ns_ref[i], lambda lo, so, rows: piece(buf)(0, 0, rows), lambda cp: cp.wait())
    moe = jnp.dot(weights, gbuf_ref[buf], preferred_element_type=F32)
    o_ref[...] = x1_ref[...] + g2_ref[0] * moe


def _combine(n_big, n_small, piece_table, route, x1, mod3, ys):
    tm = OUT_TM
    steps_per_batch = SEQ // tm
    grid_spec = pltpu.PrefetchScalarGridSpec(
        num_scalar_prefetch=2,
        grid=(N_TOKEN_TILES,),
        in_specs=[_piece_spec(lambda i, nb, ns: (i, 0, 0)),
                  _piece_spec(lambda i, nb, ns: (jnp.minimum(i + 1, N_TOKEN_TILES - 1), 0, 0)),
                  pl.BlockSpec((tm, ROUTE_W), lambda i, nb, ns: (i, 0)),
                  pl.BlockSpec((tm, D_MODEL), lambda i, nb, ns: (i, 0)),
                  pl.BlockSpec((1, 1, D_MODEL), lambda i, nb, ns: ((i // steps_per_batch) * 6 + 5, 0, 0)),
                  pl.BlockSpec(memory_space=pl.ANY)],
        out_specs=pl.BlockSpec((tm, D_MODEL), lambda i, nb, ns: (i, 0)),
        scratch_shapes=[pltpu.VMEM((2, LOCAL_ROWS, D_MODEL), BF16), pltpu.SemaphoreType.DMA((2,))],
    )
    return pl.pallas_call(
        _combine_kernel,
        grid_spec=grid_spec,
        out_shape=jax.ShapeDtypeStruct((TOKENS, D_MODEL), F32),
        compiler_params=_cparams(("arbitrary",)),
        name="combine",
    )(n_big, n_small, piece_table, piece_table, route, x1, mod3, ys)


def kernel(x, c, positions, norm1_w, norm2_w, w_ada, b_ada, w_in, conv_w, conv_b, dt_bias, a_log,
           d_skip, ssd_norm_w, q_norm_w, k_norm_w, sinks, w_out, w_group, b_group, w_expert, b_expert,
           w_gate, w_up, w_down):
    assert x.shape == (BATCH, SEQ, D_MODEL) and w_in.shape == (D_MODEL, IN_WIDTH)
    x2d = x.reshape(TOKENS, D_MODEL)
    mod = _ada_mod(c, w_ada, b_ada)
    mod3 = mod.reshape(BATCH * 6, 1, D_MODEL)

    qkv, z, xbc, dt, dt_t = _in_proj(x2d, norm1_w, mod3, w_in)
    att = _attention(qkv, positions, q_norm_w, k_norm_w, sinks)
    y = _ssd(xbc, z, dt, dt_t, conv_w, conv_b, dt_bias, a_log, d_skip, ssd_norm_w)

    w_router = jnp.pad(jnp.concatenate([w_group, w_expert], axis=1).astype(F32),
                       ((0, 0), (0, LANES - ROUTER_COLS)))
    b_router = jnp.pad(jnp.concatenate([b_group, b_expert]).astype(F32),
                       (0, LANES - ROUTER_COLS)).reshape(1, LANES)
    x1, h2, route, route_t, tcnt = _out_router(att, y, x2d, mod3, w_out.astype(F32), norm2_w,
                                                w_router, b_router)

    tc = tcnt[:, 0, 0:N_EXPERTS].astype(I32)
    run_rows = ((tc + RUN_ALIGN - 1) // RUN_ALIGN) * RUN_ALIGN
    counts = jnp.sum(run_rows, axis=0)
    padded = ((counts + MOE_TM - 1) // MOE_TM) * MOE_TM
    seg_end = jnp.cumsum(padded)
    seg_start = seg_end - padded
    run_dst = seg_start[None, :] + jnp.cumsum(run_rows, axis=0) - run_rows
    n_used = (seg_end[-1] // MOE_TM).reshape(1)
    last_row = jnp.minimum(jnp.arange(N_TILES, dtype=I32) * MOE_TM, seg_end[-1] - 1)
    tile_expert = jnp.sum((seg_end[None, :] <= last_row[:, None]).astype(I32), axis=1)

    run_local = jnp.cumsum(run_rows, axis=1) - run_rows
    n_big_run = run_rows // BIG_PIECE
    n_small_run = (run_rows // RUN_ALIGN) % 2
    q = jnp.arange(PIECE_SLOTS, dtype=I32)
    experts = jnp.arange(N_EXPERTS, dtype=I32)

    def flat(per_run, local0, dst0, stride):
        end = jnp.cumsum(per_run, axis=1)
        run_of = jnp.sum((end[:, None, :] <= q[None, :, None]).astype(I32), axis=2)
        pick = (run_of[:, :, None] == experts[None, None, :]).astype(I32)
        k = q[None, :] - jnp.sum(pick * (end - per_run)[:, None, :], axis=2)
        local = jnp.sum(pick * local0[:, None, :], axis=2) + stride * k
        dst = jnp.sum(pick * dst0[:, None, :], axis=2) + stride * k
        return end[:, -1], local, dst

    n_big, big_local, big_dst = flat(n_big_run, run_local, run_dst, BIG_PIECE)
    n_small, small_local, small_dst = flat(n_small_run, run_local + BIG_PIECE * n_big_run,
                                           run_dst + BIG_PIECE * n_big_run, 0)
    piece_table = jnp.concatenate([big_local, big_dst, small_local, small_dst], axis=1)
    piece_table = piece_table.astype(I32).reshape(N_TOKEN_TILES, 1, TABLE_W)

    nonempty = padded > 0
    seg_rank = jnp.cumsum(nonempty.astype(I32)) - 1
    later = nonempty[None, :] & (experts[None, :] > experts[:, None])
    next_of = jnp.min(jnp.where(later, experts[None, :], N_EXPERTS), axis=1)
    next_of = jnp.where(next_of == N_EXPERTS, -1, next_of)
    tile_is = (tile_expert[:, None] == experts[None, :]).astype(I32)
    tile_segment = jnp.sum(tile_is * seg_rank[None, :], axis=1)
    next_expert = jnp.sum(tile_is * next_of[None, :], axis=1)

    n_big, n_small = n_big.astype(I32), n_small.astype(I32)
    xs = _dispatch(seg_end.astype(I32), (seg_start + counts).astype(I32), n_big, n_small, piece_table,
                   route_t, h2)
    ys = _experts(tile_expert, tile_segment.astype(I32), next_expert.astype(I32), n_used.astype(I32),
                  xs, w_gate, w_up, w_down)
    out = _combine(n_big, n_small, piece_table, route, x1, mod3, ys)
    return out.reshape(BATCH, SEQ, D_MODEL)
```

```python
import jax
import jax.numpy as jnp
from jax import lax
from jax.experimental import pallas as pl
from jax.experimental.pallas import tpu as pltpu

F32 = jnp.float32
BF16 = jnp.bfloat16
I32 = jnp.int32

D_MODEL = 1024
BATCH = 2
SEQ = 8192
TOKENS = BATCH * SEQ
ATT_HEADS = 8
ATT_KV_HEADS = 2
HEAD_DIM = 64
ATT_WIDTH = ATT_HEADS * HEAD_DIM
KV_WIDTH = ATT_KV_HEADS * HEAD_DIM
ATT_BLOCK = 128
ROPE_DIM = HEAD_DIM // 4
ROPE_THETA = 500000.0
SSD_HEADS = 8
SSD_HEAD_DIM = 64
SSD_WIDTH = SSD_HEADS * SSD_HEAD_DIM
SSD_GROUPS = 2
SSD_STATE = 128
CONV_K = 4
CHUNK = 128
XBC_WIDTH = SSD_WIDTH + 2 * SSD_GROUPS * SSD_STATE
IN_WIDTH = ATT_WIDTH + 2 * KV_WIDTH + SSD_WIDTH + XBC_WIDTH + SSD_HEADS
N_GROUPS = 4
EXPERTS_PER_GROUP = 8
N_EXPERTS = N_GROUPS * EXPERTS_PER_GROUP
TOP_K = 2
D_EXPERT = 256
EPS = 1e-6

LANES = 128
QKV_WIDTH = ATT_WIDTH + 2 * KV_WIDTH
IN_PAD = QKV_WIDTH + SSD_WIDTH + XBC_WIDTH + LANES
NEG_BIG = -1e30

VMEM_LIMIT = 48 * 1024 * 1024


def _cparams(sem):
    return pltpu.CompilerParams(dimension_semantics=sem, vmem_limit_bytes=VMEM_LIMIT)


def _split_bf16(x):
    hi = x.astype(BF16)
    lo = (x - hi.astype(F32)).astype(BF16)
    return hi, lo


ADA_TN = 768


def _ada_kernel(ct_ref, w_ref, b_ref, o_ref):
    ct = ct_ref[...]
    s = ct * jax.nn.sigmoid(ct)
    w = w_ref[...]
    rows = [jnp.sum(s[:, b:b + 1] * w, axis=0, keepdims=True) for b in range(BATCH)]
    o_ref[...] = jnp.concatenate(rows, axis=0) + b_ref[...]


def _ada_mod(c, w_ada, b_ada):
    n = w_ada.shape[1]
    return pl.pallas_call(
        _ada_kernel,
        grid=(n // ADA_TN,),
        in_specs=[pl.BlockSpec((D_MODEL, BATCH), lambda j: (0, 0)),
                  pl.BlockSpec((D_MODEL, ADA_TN), lambda j: (0, j)),
                  pl.BlockSpec((1, ADA_TN), lambda j: (0, j))],
        out_specs=pl.BlockSpec((BATCH, ADA_TN), lambda j: (0, j)),
        out_shape=jax.ShapeDtypeStruct((BATCH, n), F32),
        compiler_params=_cparams(("arbitrary",)),
        name="ada_mod",
    )(c.T, w_ada, b_ada.reshape(1, n))


INPROJ_TM = 512
_INPROJ_CHUNK = 256


def _inproj_kernel(x_ref, nw_ref, sc_ref, sh_ref, wf_ref, wdt_ref, qkv_ref, z_ref, xbc_ref, dt_ref, dtt_ref,
                   w_ref):
    @pl.when(pl.program_id(0) == 0)
    def _():
        for c0 in range(0, IN_PAD - LANES, _INPROJ_CHUNK):
            w_ref[:, c0:c0 + _INPROJ_CHUNK] = wf_ref[:, c0:c0 + _INPROJ_CHUNK].astype(BF16)
        w_ref[:, IN_PAD - LANES:IN_PAD] = wdt_ref[...].astype(BF16)

    x = x_ref[...]
    y = x * lax.rsqrt(jnp.mean(x * x, axis=-1, keepdims=True) + EPS)
    h = (y * nw_ref[...]) * (1.0 + sc_ref[0]) + sh_ref[0]
    hb = h.astype(BF16)

    def proj(c0, c1):
        return jnp.dot(hb, w_ref[:, c0:c1], preferred_element_type=F32)

    for c0 in range(0, QKV_WIDTH, _INPROJ_CHUNK):
        qkv_ref[:, c0:c0 + _INPROJ_CHUNK] = proj(c0, c0 + _INPROJ_CHUNK).astype(BF16)
    base = QKV_WIDTH
    for c0 in range(0, SSD_WIDTH, _INPROJ_CHUNK):
        z_ref[:, c0:c0 + _INPROJ_CHUNK] = proj(base + c0, base + c0 + _INPROJ_CHUNK).astype(BF16)
    base += SSD_WIDTH
    for c0 in range(0, XBC_WIDTH, _INPROJ_CHUNK):
        xbc_ref[:, c0:c0 + _INPROJ_CHUNK] = proj(base + c0, base + c0 + _INPROJ_CHUNK).astype(BF16)
    base += XBC_WIDTH
    dt = proj(base, base + LANES)
    dt_ref[...] = dt
    dtt_ref[...] = dt.T[0:SSD_HEADS, :]


def _in_proj(x2d, norm_w, mod3, w_in):
    tm = INPROJ_TM
    steps_per_batch = SEQ // tm
    w_dt = jnp.pad(w_in[:, IN_WIDTH - SSD_HEADS:IN_WIDTH].astype(F32), ((0, 0), (0, LANES - SSD_HEADS)))
    return pl.pallas_call(
        _inproj_kernel,
        grid=(TOKENS // tm,),
        in_specs=[pl.BlockSpec((tm, D_MODEL), lambda i: (i, 0)),
                  pl.BlockSpec((1, D_MODEL), lambda i: (0, 0)),
                  pl.BlockSpec((1, 1, D_MODEL), lambda i: ((i // steps_per_batch) * 6 + 1, 0, 0)),
                  pl.BlockSpec((1, 1, D_MODEL), lambda i: ((i // steps_per_batch) * 6 + 0, 0, 0)),
                  pl.BlockSpec((D_MODEL, IN_WIDTH), lambda i: (0, 0), pipeline_mode=pl.Buffered(1)),
                  pl.BlockSpec((D_MODEL, LANES), lambda i: (0, 0))],
        out_specs=[pl.BlockSpec((tm, QKV_WIDTH), lambda i: (i, 0)),
                   pl.BlockSpec((tm, SSD_WIDTH), lambda i: (i, 0)),
                   pl.BlockSpec((tm, XBC_WIDTH), lambda i: (i, 0)),
                   pl.BlockSpec((tm, LANES), lambda i: (i, 0)),
                   pl.BlockSpec((SSD_HEADS, tm), lambda i: (0, i))],
        out_shape=[jax.ShapeDtypeStruct((TOKENS, QKV_WIDTH), BF16),
                   jax.ShapeDtypeStruct((TOKENS, SSD_WIDTH), BF16),
                   jax.ShapeDtypeStruct((TOKENS, XBC_WIDTH), BF16),
                   jax.ShapeDtypeStruct((TOKENS, LANES), F32),
                   jax.ShapeDtypeStruct((SSD_HEADS, TOKENS), F32)],
        scratch_shapes=[pltpu.VMEM((D_MODEL, IN_PAD), BF16)],
        compiler_params=_cparams(("arbitrary",)),
        name="in_proj",
    )(x2d, norm_w.reshape(1, D_MODEL), mod3, mod3, w_in.astype(F32), w_dt)


ATT_SUB = 8


ROPE_TM = 2048
_ROPE_HALF = ROPE_DIM // 2
_TOK_PER_ROW = LANES // _ROPE_HALF


def _exact_dot(x, onehot_b):
    hi, lo = _split_bf16(x)
    return (jnp.dot(hi, onehot_b, preferred_element_type=F32)
            + jnp.dot(lo, onehot_b, preferred_element_type=F32))


def _rope_kernel(pos_ref, freq_ref, sel_ref, own_ref, gcos_ref, gs1_ref, gs2_ref, ident_ref,
                 cos_ref, s1_ref, s2_ref):
    ang = pos_ref[...].astype(F32) * freq_ref[...]
    cos_p, sin_p = jnp.cos(ang), jnp.sin(ang)
    hi_c, lo_c = _split_bf16(cos_p)
    hi_s, lo_s = _split_bf16(sin_p)
    sel = sel_ref[...]
    rows_c = jnp.dot(sel, hi_c, preferred_element_type=F32) + jnp.dot(sel, lo_c, preferred_element_type=F32)
    rows_s = jnp.dot(sel, hi_s, preferred_element_type=F32) + jnp.dot(sel, lo_s, preferred_element_type=F32)
    own = own_ref[...]
    cos_ref[...] = _exact_dot(rows_c * own, gcos_ref[...]) + ident_ref[...]
    s1_ref[...] = _exact_dot(rows_s * own, gs1_ref[...])
    s2_ref[...] = _exact_dot(rows_s * own, gs2_ref[...])


def _rope_tables(positions):
    half, per_row = _ROPE_HALF, _TOK_PER_ROW
    rows = ROPE_TM // per_row
    pos_rep = jnp.repeat(positions.reshape(TOKENS).astype(I32), half).reshape(TOKENS // per_row, LANES)
    inv_freq = jnp.power(ROPE_THETA, -jnp.arange(half, dtype=F32) * 2.0 / ROPE_DIM)
    freq = jnp.tile(inv_freq, per_row).reshape(1, LANES)
    tok = jnp.arange(ROPE_TM)
    lane = jnp.arange(LANES)
    sel = (tok[:, None] // per_row == jnp.arange(rows)[None, :]).astype(BF16)
    own = (lane[None, :] // half == tok[:, None] % per_row).astype(F32)
    d = lane % HEAD_DIM
    src_f = lane % half
    hits = lambda lo, hi: ((src_f[:, None] == d[None, :] % half) & (d[None, :] >= lo) & (d[None, :] < hi))
    gcos = hits(0, ROPE_DIM).astype(BF16)
    gs1 = -hits(0, half).astype(BF16)
    gs2 = hits(half, ROPE_DIM).astype(BF16)
    ident = (d >= ROPE_DIM).astype(F32).reshape(1, LANES)
    const = lambda shape: pl.BlockSpec(shape, lambda i: (0, 0))
    out_spec = pl.BlockSpec((ROPE_TM, LANES), lambda i: (i, 0))
    out = jax.ShapeDtypeStruct((TOKENS, LANES), F32)
    return pl.pallas_call(
        _rope_kernel,
        grid=(TOKENS // ROPE_TM,),
        in_specs=[pl.BlockSpec((rows, LANES), lambda i: (i, 0)), const((1, LANES)),
                  const((ROPE_TM, rows)), const((ROPE_TM, LANES)),
                  const((LANES, LANES)), const((LANES, LANES)), const((LANES, LANES)), const((1, LANES))],
        out_specs=[out_spec, out_spec, out_spec],
        out_shape=[out, out, out],
        compiler_params=_cparams(("arbitrary",)),
        name="rope_tables",
    )(pos_rep, freq, sel, own, gcos, gs1, gs2, ident)


def _seg_meansq(xf, ones128):
    rows, width = xf.shape
    nt = width // LANES
    parts = _split_bf16(xf * xf)
    stacked = jnp.concatenate([p[:, t * LANES:(t + 1) * LANES] for p in parts for t in range(nt)], axis=0)
    tot = jnp.dot(stacked, ones128, preferred_element_type=F32)
    tiles = [tot[t * rows:(t + 1) * rows] + tot[(nt + t) * rows:(nt + t + 1) * rows] for t in range(nt)]
    return jnp.concatenate(tiles, axis=1) * (1.0 / HEAD_DIM)


def _norm_rope(x_bf, w_row, ones_bd, cosf, s1, s2):
    xf = x_bf.astype(F32)
    width = xf.shape[1]
    xn = xf * lax.rsqrt(_seg_meansq(xf, ones_bd) + EPS) * w_row
    half = ROPE_DIM // 2
    up = pltpu.roll(xn, width - half, axis=1)
    down = pltpu.roll(xn, half, axis=1)
    return xn * cosf + up * s1 + down * s2


def _attn_kernel(sink_ref, q_ref, kv_ref, cos_ref, s1_ref, s2_ref, qw_ref, kw_ref,
                 ones_ref, o_ref, kprev_ref, vprev_ref):
    j = pl.program_id(1)
    blk = ATT_BLOCK

    @pl.when(j == 0)
    def _():
        kprev_ref[...] = jnp.zeros_like(kprev_ref)
        vprev_ref[...] = jnp.zeros_like(vprev_ref)

    cos1 = cos_ref[...]
    s1_1 = s1_ref[...]
    s2_1 = s2_ref[...]
    reps = ATT_WIDTH // LANES
    cosq = jnp.concatenate([cos1] * reps, axis=1)
    s1q = jnp.concatenate([s1_1] * reps, axis=1)
    s2q = jnp.concatenate([s2_1] * reps, axis=1)

    q = _norm_rope(q_ref[...], qw_ref[...], ones_ref[...], cosq, s1q, s2q)
    qf = q * (HEAD_DIM ** -0.5)
    kv = kv_ref[...]
    kn = _norm_rope(kv[:, 0:KV_WIDTH], kw_ref[...], ones_ref[...], cos1, s1_1, s2_1)
    vn = kv[:, KV_WIDTH:2 * KV_WIDTH].astype(F32)

    kall = jnp.concatenate([kprev_ref[...], kn], axis=0)
    vall = jnp.concatenate([vprev_ref[...], vn], axis=0)
    kprev_ref[...] = kn[(ATT_SUB - 1) * blk:ATT_SUB * blk]
    vprev_ref[...] = vn[(ATT_SUB - 1) * blk:ATT_SUB * blk]

    lo_all = lax.broadcasted_iota(I32, kall.shape, 1) < HEAD_DIM
    ones_all = jnp.ones(kall.shape, BF16)

    row = lax.broadcasted_iota(I32, (2 * blk, blk), 0)
    col = lax.broadcasted_iota(I32, (2 * blk, blk), 1)
    from_prev = col > (row & (blk - 1))
    second_tile = lax.broadcasted_iota(I32, (2 * blk, 1), 0) >= blk
    zero_p = jnp.zeros((2 * blk, blk), F32)

    k_par, v_par = [], []
    for g in range(ATT_KV_HEADS):
        keep = lo_all if g == 0 else ~lo_all
        k_own = jnp.where(keep, kall, 0.0)
        v_own = jnp.where(keep, vall, 0.0)
        k_oth = pltpu.roll(k_own, HEAD_DIM, axis=1)
        v_oth = pltpu.roll(v_own, HEAD_DIM, axis=1)
        k_lo, k_hi = (k_own, k_oth) if g == 0 else (k_oth, k_own)
        v_lo, v_hi = (v_own, v_oth) if g == 0 else (v_oth, v_own)
        k_par.append((k_lo.astype(BF16), k_hi.astype(BF16)))
        v_par.append((jnp.concatenate([v_lo.astype(BF16), ones_all], axis=1),
                      jnp.concatenate([v_hi.astype(BF16), ones_all], axis=1)))

    problems = [(g, sub) for g in range(ATT_KV_HEADS) for sub in range(ATT_SUB)]
    scores = []
    for g, sub in problems:
        r0, c0 = sub * blk, g * 2 * LANES
        qcat = jnp.concatenate([qf[r0:r0 + blk, c0:c0 + LANES],
                                qf[r0:r0 + blk, c0 + LANES:c0 + 2 * LANES]], axis=0).astype(BF16)
        kw = jnp.concatenate([k_par[g][0][r0:r0 + 2 * blk], k_par[g][1][r0:r0 + 2 * blk]], axis=0)
        scores.append(lax.dot_general(qcat, kw, (((1,), (1,)), ((), ())),
                                      preferred_element_type=F32))

    weights, rescale = [], []
    for (g, sub), s_all in zip(problems, scores):
        for par in range(2):
            s = s_all[:, par * 2 * blk:(par + 1) * 2 * blk]
            s_prev = s[:, 0:blk]
            if sub == 0:
                s_prev = s_prev + jnp.where(j > 0, 0.0, NEG_BIG)
            s = jnp.where(from_prev, s_prev, s[:, blk:2 * blk])
            h_first = ATT_HEADS // ATT_KV_HEADS * g + par
            sink = jnp.where(second_tile, sink_ref[h_first + 2], sink_ref[h_first])
            m = jnp.maximum(jnp.max(s, axis=-1, keepdims=True), sink)
            p = jnp.exp(s - m)
            weights.append(jnp.concatenate([jnp.where(from_prev, p, zero_p), jnp.where(from_prev, zero_p, p)],
                                           axis=1).astype(BF16))
            rescale.append(jnp.exp(sink - m))

    outs = []
    for idx, (g, sub) in enumerate(problems):
        for par in range(2):
            outs.append(jnp.dot(weights[2 * idx + par], v_par[g][par][sub * blk:(sub + 2) * blk],
                                preferred_element_type=F32))

    for idx, (g, sub) in enumerate(problems):
        r0, c0 = sub * blk, g * 2 * LANES
        pair = None
        for par in range(2):
            o = outs[2 * idx + par]
            part = o[:, 0:LANES] * (1.0 / (o[:, LANES:2 * LANES] + rescale[2 * idx + par]))
            pair = part if pair is None else pair + part
        o_ref[r0:r0 + blk, c0:c0 + LANES] = pair[0:blk].astype(BF16)
        o_ref[r0:r0 + blk, c0 + LANES:c0 + 2 * LANES] = pair[blk:2 * blk].astype(BF16)


def _attention(qkv, positions, q_norm_w, k_norm_w, sinks):
    cosf, s1, s2 = _rope_tables(positions)
    qw = jnp.tile(q_norm_w.astype(F32), ATT_HEADS).reshape(1, ATT_WIDTH)
    kw = jnp.tile(k_norm_w.astype(F32), ATT_KV_HEADS).reshape(1, KV_WIDTH)
    seg = jnp.arange(LANES) // HEAD_DIM
    ones128 = (seg[:, None] == seg[None, :]).astype(BF16)
    const = lambda shape: pl.BlockSpec(shape, lambda b, j, s: (0, 0))
    rows = ATT_SUB * ATT_BLOCK
    nb = SEQ // rows
    tok = lambda width, cb: pl.BlockSpec((rows, width), lambda b, j, s: (b * nb + j, cb))
    grid_spec = pltpu.PrefetchScalarGridSpec(
        num_scalar_prefetch=1,
        grid=(BATCH, nb),
        in_specs=[tok(ATT_WIDTH, 0), tok(2 * KV_WIDTH, 2), tok(LANES, 0), tok(LANES, 0), tok(LANES, 0),
                  const((1, ATT_WIDTH)), const((1, KV_WIDTH)), const((LANES, LANES))],
        out_specs=tok(ATT_WIDTH, 0),
        scratch_shapes=[pltpu.VMEM((ATT_BLOCK, KV_WIDTH), F32),
                        pltpu.VMEM((ATT_BLOCK, KV_WIDTH), F32)],
    )
    return pl.pallas_call(
        _attn_kernel,
        grid_spec=grid_spec,
        out_shape=jax.ShapeDtypeStruct((TOKENS, ATT_WIDTH), BF16),
        compiler_params=_cparams(("arbitrary", "arbitrary")),
        name="attention",
    )(sinks.astype(F32), qkv, qkv, cosf, s1, s2, qw, kw, ones128)


SSD_SUB = 4


def _softplus(x):
    return jnp.maximum(x, 0.0) + jnp.log1p(jnp.exp(-jnp.abs(x)))


def _silu(x):
    h = 0.5 * x
    return h + h * jnp.tanh(h)


def _ssd_kernel(xbc_ref, z_ref, dt_ref, dtt_ref, cw_ref, cb_ref, dtb_row_ref, dtb_col_ref,
                alog_row_ref, alog_col_ref, dskip_ref, nw_ref, tril_ref, triu_ref,
                o_ref, conv_ref, state_ref):
    c = pl.program_id(1)
    L = CHUNK
    tail = 8

    @pl.when(c == 0)
    def _():
        conv_ref[0:tail, :] = jnp.zeros((tail, XBC_WIDTH), F32)
        state_ref[...] = jnp.zeros_like(state_ref)

    row = lax.broadcasted_iota(I32, (L, L), 0)
    col = lax.broadcasted_iota(I32, (L, L), 1)
    causal = col <= row
    lane = lax.broadcasted_iota(I32, (L, LANES), 1)
    lo_half = lane < SSD_HEAD_DIM

    prepared = [_ssd_prepare(s * L, xbc_ref, dt_ref, dtt_ref, cw_ref, cb_ref, dtb_row_ref, dtb_col_ref,
                             alog_row_ref, alog_col_ref, tril_ref, triu_ref, conv_ref)
                for s in range(SSD_SUB)]
    for s in range(SSD_SUB):
        _ssd_chunk(s * L, prepared[s], causal, lo_half, z_ref, dskip_ref, nw_ref, o_ref, state_ref)


def _ssd_prepare(r0, xbc_ref, dt_ref, dtt_ref, cw_ref, cb_ref, dtb_row_ref, dtb_col_ref,
                 alog_row_ref, alog_col_ref, tril_ref, triu_ref, conv_ref):
    L = CHUNK
    tail = 8
    xb = xbc_ref[r0:r0 + L, :].astype(F32)
    conv_ref[tail:tail + L, :] = xb
    acc = cb_ref[...] + cw_ref[CONV_K - 1:CONV_K, :] * xb
    for k in range(CONV_K - 1):
        off = tail - (CONV_K - 1) + k
        acc = acc + cw_ref[k:k + 1, :] * conv_ref[off:off + L, :]
    conv_ref[0:tail, :] = xb[L - tail:L, :]
    u = _silu(acc)
    xs = u[:, 0:SSD_WIDTH]
    bmat = u[:, SSD_WIDTH:SSD_WIDTH + SSD_GROUPS * SSD_STATE]
    cmat = u[:, SSD_WIDTH + SSD_GROUPS * SSD_STATE:XBC_WIDTH]

    dt = _softplus(dt_ref[r0:r0 + L, :] + dtb_row_ref[...])
    a = dt * (-jnp.exp(alog_row_ref[...]))
    a_hi, a_lo = _split_bf16(a)
    a_cum = (jnp.dot(tril_ref[...], a_hi, preferred_element_type=F32)
             + jnp.dot(tril_ref[...], a_lo, preferred_element_type=F32))
    dt_t = _softplus(dtt_ref[:, r0:r0 + L] + dtb_col_ref[...])
    a_t = dt_t * (-jnp.exp(alog_col_ref[...]))
    at_hi, at_lo = _split_bf16(a_t)
    a_cum_t = (jnp.dot(at_hi, triu_ref[...], preferred_element_type=F32)
               + jnp.dot(at_lo, triu_ref[...], preferred_element_type=F32))
    a_end_t = a_cum_t[:, L - 1:L]
    return dict(
        xs=xs, bmat=bmat, cmat=cmat, a_cum=a_cum, exp_a_cum=jnp.exp(a_cum),
        shifted_t=a_cum_t - jnp.log(dt_t),
        wst_t=jnp.exp(a_end_t - a_cum_t) * dt_t,
        cdec_t=jnp.exp(a_end_t))


def _ssd_chunk(r0, p, causal, lo_half, z_ref, dskip_ref, nw_ref, o_ref, state_ref):
    L = CHUNK
    xs, bmat, cmat, a_cum, exp_a_cum = p["xs"], p["bmat"], p["cmat"], p["a_cum"], p["exp_a_cum"]
    shifted_t, wst_t, cdec_t = p["shifted_t"], p["wst_t"], p["cdec_t"]
    xs_b = xs.astype(BF16)
    heads_per_group = SSD_HEADS // SSD_GROUPS
    gated = []
    for g in range(SSD_GROUPS):
        b_g = bmat[:, g * SSD_STATE:(g + 1) * SSD_STATE]
        c_g = cmat[:, g * SSD_STATE:(g + 1) * SSD_STATE]
        cb = lax.dot_general(c_g.astype(BF16), b_g.astype(BF16), (((1,), (1,)), ((), ())),
                             preferred_element_type=F32)
        b_gt = b_g.T
        for t in range(heads_per_group // 2):
            tile = g * (heads_per_group // 2) + t
            c0 = tile * LANES
            xs_tile = xs_b[:, c0:c0 + LANES]
            st_tile = state_ref[:, c0:c0 + LANES]
            st_b = st_tile.astype(BF16)
            y_tile = jnp.zeros((L, LANES), F32)
            new_tile = jnp.zeros((SSD_STATE, LANES), F32)
            for e in range(2):
                h = 2 * tile + e
                keep = lo_half if e == 0 else ~lo_half
                colb = jnp.broadcast_to(a_cum[:, h:h + 1], (L, L))
                rowb = shifted_t[h:h + 1, :]
                w_in = cb * jnp.exp(jnp.where(causal, colb - rowb, NEG_BIG))
                w_off = c_g * jnp.broadcast_to(exp_a_cum[:, h:h + 1], (L, L))
                lhs = jnp.concatenate([w_in, w_off], axis=1).astype(BF16)
                rhs = jnp.concatenate([jnp.where(keep, xs_tile, jnp.zeros_like(xs_tile)),
                                       jnp.where(keep, st_b, jnp.zeros_like(st_b))], axis=0)
                y_tile = y_tile + jnp.dot(lhs, rhs, preferred_element_type=F32)
                m_h = (b_gt * wst_t[h:h + 1, :]).astype(BF16)
                new_tile = new_tile + jnp.dot(m_h, jnp.where(keep, xs_tile, jnp.zeros_like(xs_tile)),
                                              preferred_element_type=F32)
            cd = jnp.where(lo_half[0:1, :], cdec_t[2 * tile:2 * tile + 1, :],
                           cdec_t[2 * tile + 1:2 * tile + 2, :])
            state_ref[:, c0:c0 + LANES] = st_tile * cd + new_tile
            y_full = y_tile + dskip_ref[:, c0:c0 + LANES] * xs[:, c0:c0 + LANES]
            gated.append(y_full * _silu(z_ref[r0:r0 + L, c0:c0 + LANES].astype(F32)))

    gw = SSD_WIDTH // SSD_GROUPS
    tiles_per_group = gw // LANES
    for g in range(SSD_GROUPS):
        yg = jnp.concatenate(gated[g * tiles_per_group:(g + 1) * tiles_per_group], axis=1)
        ms = jnp.mean(yg * yg, axis=-1, keepdims=True)
        o_ref[r0:r0 + L, g * gw:(g + 1) * gw] = (
            (yg * lax.rsqrt(ms + EPS)) * nw_ref[:, g * gw:(g + 1) * gw]).astype(o_ref.dtype)


def _ssd(xbc, z, dt, dt_t, conv_w, conv_b, dt_bias, a_log, d_skip, ssd_norm_w):
    L = SSD_SUB * CHUNK
    nc = SEQ // L
    pad_row = lambda v: jnp.pad(v.astype(F32), (0, LANES - SSD_HEADS)).reshape(1, LANES)
    col8 = lambda v: v.astype(F32).reshape(SSD_HEADS, 1)
    idx = jnp.arange(CHUNK)
    tril = (idx[None, :] <= idx[:, None]).astype(BF16)
    triu = (idx[:, None] <= idx[None, :]).astype(BF16)
    dskip = jnp.repeat(d_skip.astype(F32), SSD_HEAD_DIM).reshape(1, SSD_WIDTH)
    const = lambda shape: pl.BlockSpec(shape, lambda b, c: (0, 0))
    tok = lambda width: pl.BlockSpec((L, width), lambda b, c: (b * nc + c, 0))
    return pl.pallas_call(
        _ssd_kernel,
        grid=(BATCH, nc),
        in_specs=[tok(XBC_WIDTH), tok(SSD_WIDTH), tok(LANES),
                  pl.BlockSpec((SSD_HEADS, L), lambda b, c: (0, b * nc + c)),
                  const((CONV_K, XBC_WIDTH)), const((1, XBC_WIDTH)),
                  const((1, LANES)), const((SSD_HEADS, 1)), const((1, LANES)), const((SSD_HEADS, 1)),
                  const((1, SSD_WIDTH)), const((1, SSD_WIDTH)), const((CHUNK, CHUNK)), const((CHUNK, CHUNK))],
        out_specs=tok(SSD_WIDTH),
        out_shape=jax.ShapeDtypeStruct((TOKENS, SSD_WIDTH), BF16),
        scratch_shapes=[pltpu.VMEM((8 + CHUNK, XBC_WIDTH), F32),
                        pltpu.VMEM((SSD_STATE, SSD_WIDTH), F32)],
        compiler_params=_cparams(("arbitrary", "arbitrary")),
        name="ssd",
    )(xbc, z, dt, dt_t, conv_w.astype(F32), conv_b.astype(F32).reshape(1, XBC_WIDTH),
      pad_row(dt_bias), col8(dt_bias), pad_row(a_log), col8(a_log), dskip,
      ssd_norm_w.astype(F32).reshape(1, SSD_WIDTH), tril, triu)


OUT_TM = 512
ROUTE_W = 8
ROUTER_COLS = N_GROUPS + N_EXPERTS
RUN_ALIGN = 16
RUN_SHIFT = 4
LOCAL_ROWS = 1536
assert RUN_ALIGN == 1 << RUN_SHIFT and LOCAL_ROWS >= TOP_K * OUT_TM + N_EXPERTS * (RUN_ALIGN - 1)


def _lane_pick(values, lane, index):
    return jnp.sum(jnp.where(lane == index, values, 0.0), axis=-1, keepdims=True)


def _first_argmax(vals, lane):
    m = jnp.max(vals, axis=-1, keepdims=True)
    idx = jnp.min(jnp.where(vals == m, lane, float(LANES)), axis=-1, keepdims=True)
    return m, idx


def _out_router_kernel(att_ref, y_ref, x_ref, g1_ref, wof_ref, nw_ref, sc_ref, sh_ref, wr_ref, br_ref,
                       ltri_ref, sut_ref, x1_ref, h2_ref, route_ref, routet_ref, tcnt_ref,
                       wr_split_ref, logits_ref, wo_ref):
    i = pl.program_id(0)

    @pl.when(i == 0)
    def _():
        hi, lo = _split_bf16(wr_ref[...])
        wr_split_ref[:, 0:LANES] = hi
        wr_split_ref[:, LANES:2 * LANES] = lo
        logits_ref[...] = jnp.zeros_like(logits_ref)
        for r0 in range(0, D_MODEL, 256):
            wo_ref[r0:r0 + 256, :] = wof_ref[r0:r0 + 256, :].astype(BF16)

    logits = logits_ref[...]

    mixer = (jnp.dot(att_ref[...], wo_ref[0:ATT_WIDTH, :], preferred_element_type=F32)
             + jnp.dot(y_ref[...], wo_ref[ATT_WIDTH:ATT_WIDTH + SSD_WIDTH, :], preferred_element_type=F32))
    x1 = x_ref[...] + g1_ref[0] * mixer
    x1_ref[...] = x1
    yn = x1 * lax.rsqrt(jnp.mean(x1 * x1, axis=-1, keepdims=True) + EPS)
    h2 = (yn * nw_ref[...]) * (1.0 + sc_ref[0]) + sh_ref[0]
    h2_ref[...] = h2.astype(BF16)

    h_hi, h_lo = _split_bf16(h2)
    both = jnp.dot(h_hi, wr_split_ref[...], preferred_element_type=F32)
    logits_ref[...] = (both[:, 0:LANES] + both[:, LANES:2 * LANES]
                       + jnp.dot(h_lo, wr_split_ref[:, 0:LANES], preferred_element_type=F32)) + br_ref[...]

    tm = logits.shape[0]
    lane = lax.broadcasted_iota(I32, (tm, LANES), 1).astype(F32)

    gl = jnp.where(lane < N_GROUPS, logits, NEG_BIG)
    gmax, gidx = _first_argmax(gl, lane)
    g_p = 1.0 / jnp.sum(jnp.exp(gl - gmax), axis=-1, keepdims=True)

    lo_lane = N_GROUPS + EXPERTS_PER_GROUP * gidx
    el = jnp.where((lane >= lo_lane) & (lane < lo_lane + EXPERTS_PER_GROUP), logits, NEG_BIG)
    m1, i1 = _first_argmax(el, lane)
    m2, i2 = _first_argmax(jnp.where(lane == i1, NEG_BIG, el), lane)
    r = jnp.exp(m2 - m1)
    p1 = 1.0 / (1.0 + r)
    p2 = r / (1.0 + r)
    e0 = i1 - N_GROUPS
    e1 = i2 - N_GROUPS

    onehot = ((lane == e0) | (lane == e1)).astype(F32)
    tile_cnt = jnp.sum(onehot, axis=0, keepdims=True)
    run_len = jnp.floor((tile_cnt + (RUN_ALIGN - 1)) * (1.0 / RUN_ALIGN)) * RUN_ALIGN
    run_start = jnp.dot(jnp.broadcast_to(run_len, (8, LANES)).astype(BF16), sut_ref[...],
                        preferred_element_type=F32)[0:1, :]
    before = jnp.dot(ltri_ref[...], onehot.astype(BF16), preferred_element_type=F32) + run_start
    slot0 = _lane_pick(before, lane, e0)
    slot1 = _lane_pick(before, lane, e1)
    tcnt_ref[0] = tile_cnt

    rec = jnp.zeros((tm, LANES), F32)
    for k, v in enumerate([slot0, slot1, g_p * p1, g_p * p2, e0, e1]):
        rec = jnp.where(lane == k, v, rec)
    route_ref[...] = rec[:, 0:ROUTE_W]
    routet_ref[...] = rec.T[0:ROUTE_W, :]


def _out_router(att, y, x2d, mod3, w_out_b, norm_w, w_router, b_router):
    tm = OUT_TM
    n_steps = TOKENS // tm
    steps_per_batch = SEQ // tm
    idx = jnp.arange(tm)
    ltri = (idx[None, :] < idx[:, None]).astype(BF16)
    lidx = jnp.arange(LANES)
    sut = (lidx[:, None] < lidx[None, :]).astype(BF16)
    const = lambda shape: pl.BlockSpec(shape, lambda i: (0, 0))
    cur = lambda i: jnp.minimum(i, n_steps - 1)
    prev = lambda i: jnp.maximum(i - 1, 0)
    tok = lambda width: pl.BlockSpec((tm, width), lambda i: (cur(i), 0))
    modspec = lambda k: pl.BlockSpec((1, 1, D_MODEL), lambda i: ((cur(i) // steps_per_batch) * 6 + k, 0, 0))
    return pl.pallas_call(
        _out_router_kernel,
        grid=(n_steps + 1,),
        in_specs=[tok(ATT_WIDTH), tok(SSD_WIDTH), tok(D_MODEL), modspec(2),
                  const((D_MODEL, D_MODEL)), const((1, D_MODEL)), modspec(4), modspec(3),
                  const((D_MODEL, LANES)), const((1, LANES)), const((tm, tm)), const((LANES, LANES))],
        out_specs=[tok(D_MODEL), tok(D_MODEL),
                   pl.BlockSpec((tm, ROUTE_W), lambda i: (prev(i), 0)),
                   pl.BlockSpec((ROUTE_W, tm), lambda i: (prev(i), 0)),
                   pl.BlockSpec((1, 1, LANES), lambda i: (prev(i), 0, 0))],
        out_shape=[jax.ShapeDtypeStruct((TOKENS, D_MODEL), F32),
                   jax.ShapeDtypeStruct((TOKENS, D_MODEL), BF16),
                   jax.ShapeDtypeStruct((TOKENS, ROUTE_W), F32),
                   jax.ShapeDtypeStruct((n_steps * ROUTE_W, tm), F32),
                   jax.ShapeDtypeStruct((n_steps, 1, LANES), F32)],
        scratch_shapes=[pltpu.VMEM((D_MODEL, 2 * LANES), BF16), pltpu.VMEM((tm, LANES), F32),
                        pltpu.VMEM((D_MODEL, D_MODEL), BF16)],
        compiler_params=_cparams(("arbitrary",)),
        name="out_router",
    )(att, y, x2d, mod3, w_out_b, norm_w.reshape(1, D_MODEL), mod3, mod3, w_router, b_router, ltri, sut)


MOE_TM = 512
ZERO_ROWS = 256
N_TOKEN_TILES = TOKENS // OUT_TM
MAX_SORTED_ROWS = TOKENS * TOP_K + N_TOKEN_TILES * N_EXPERTS * (RUN_ALIGN - 1)
N_TILES = MAX_SORTED_ROWS // MOE_TM + N_EXPERTS
N_ROWS = N_TILES * MOE_TM
assert MOE_TM % ZERO_ROWS == 0


BIG_PIECE = 2 * RUN_ALIGN
PIECE_SLOTS = LOCAL_ROWS // BIG_PIECE
TABLE_W = 4 * PIECE_SLOTS
assert PIECE_SLOTS >= N_EXPERTS


def _run_copies(table_ref, n_big, n_small, make_copy, action):
    def big(q, carry):
        action(make_copy(table_ref[0, 0, q], table_ref[0, 0, PIECE_SLOTS + q], BIG_PIECE))
        return carry

    def small(q, carry):
        action(make_copy(table_ref[0, 0, 2 * PIECE_SLOTS + q], table_ref[0, 0, 3 * PIECE_SLOTS + q], RUN_ALIGN))
        return carry

    lax.fori_loop(0, n_big, big, 0)
    lax.fori_loop(0, n_small, small, 0)


def _dispatch_kernel(seg_end_ref, used_end_ref, nb_ref, ns_ref, tab_ref, routet_ref, h2_ref, xs_ref,
                     sbuf_ref, zero_ref, sems, zsem):
    i = pl.program_id(0)
    last = pl.num_programs(0) - 1
    buf = lax.rem(i, 2)

    def zero_fills(action):
        def tail_copy(row):
            return pltpu.make_async_copy(zero_ref.at[pl.ds(0, RUN_ALIGN)],
                                         xs_ref.at[pl.ds(pl.multiple_of(row, RUN_ALIGN), RUN_ALIGN)], zsem)

        def block_copy(block):
            start = pl.multiple_of(block * ZERO_ROWS, ZERO_ROWS)
            return pltpu.make_async_copy(zero_ref, xs_ref.at[pl.ds(start, ZERO_ROWS)], zsem)

        def tails(e, carry):
            def body(r, c):
                action(tail_copy(r * RUN_ALIGN))
                return c

            lax.fori_loop(used_end_ref[e] // RUN_ALIGN, seg_end_ref[e] // RUN_ALIGN, body, 0)
            return carry

        def blocks(block, carry):
            action(block_copy(block))
            return carry

        lax.fori_loop(0, N_EXPERTS, tails, 0)
        lax.fori_loop(seg_end_ref[N_EXPERTS - 1] // ZERO_ROWS, N_ROWS // ZERO_ROWS, blocks, 0)

    @pl.when(i == 0)
    def _():
        zero_ref[...] = jnp.zeros_like(zero_ref)
        zero_fills(lambda cp: cp.start())

    slot = lax.broadcasted_iota(I32, (LOCAL_ROWS, OUT_TM), 0).astype(F32)
    perm = jnp.where((slot == routet_ref[0:1, :]) | (slot == routet_ref[1:2, :]), 1.0, 0.0).astype(BF16)
    sbuf_ref[buf] = jnp.dot(perm, h2_ref[...], preferred_element_type=F32).astype(BF16)

    def piece(b):
        def make(local, sorted_row, rows):
            return pltpu.make_async_copy(
                sbuf_ref.at[b, pl.ds(pl.multiple_of(local, RUN_ALIGN), rows)],
                xs_ref.at[pl.ds(pl.multiple_of(sorted_row, RUN_ALIGN), rows)], sems.at[b])
        return make

    _run_copies(tab_ref, nb_ref[i], ns_ref[i], piece(buf), lambda cp: cp.start())
    prev = jnp.maximum(i - 1, 0)

    @pl.when(i > 0)
    def _():
        _run_copies(tab_ref, nb_ref[prev], ns_ref[prev], lambda lo, so, rows: piece(1 - buf)(0, 0, rows),
                    lambda cp: cp.wait())

    @pl.when(i == last)
    def _():
        _run_copies(tab_ref, nb_ref[i], ns_ref[i], lambda lo, so, rows: piece(buf)(0, 0, rows),
                    lambda cp: cp.wait())
        zero_fills(lambda cp: cp.wait())


def _piece_spec(index_map):
    return pl.BlockSpec((1, 1, TABLE_W), index_map, memory_space=pltpu.SMEM)


def _dispatch(seg_end, used_end, n_big, n_small, piece_table, route_t, h2):
    grid_spec = pltpu.PrefetchScalarGridSpec(
        num_scalar_prefetch=4,
        grid=(N_TOKEN_TILES,),
        in_specs=[_piece_spec(lambda i, se, ue, nb, ns: (i, 0, 0)),
                  pl.BlockSpec((ROUTE_W, OUT_TM), lambda i, se, ue, nb, ns: (i, 0)),
                  pl.BlockSpec((OUT_TM, D_MODEL), lambda i, se, ue, nb, ns: (i, 0))],
        out_specs=pl.BlockSpec(memory_space=pl.ANY),
        scratch_shapes=[pltpu.VMEM((2, LOCAL_ROWS, D_MODEL), BF16),
                        pltpu.VMEM((ZERO_ROWS, D_MODEL), BF16),
                        pltpu.SemaphoreType.DMA((2,)), pltpu.SemaphoreType.DMA],
    )
    return pl.pallas_call(
        _dispatch_kernel,
        grid_spec=grid_spec,
        out_shape=jax.ShapeDtypeStruct((N_ROWS, D_MODEL), BF16),
        compiler_params=_cparams(("arbitrary",)),
        name="dispatch",
    )(seg_end, used_end, n_big, n_small, piece_table, route_t, h2)


def _experts_kernel(te_ref, seg_ref, nxt_ref, nu_ref, xs_ref, wg_hbm, wu_hbm, wd_hbm, ys_ref,
                    wg_buf, wu_buf, wd_buf, wgu_b_ref, wd_b_ref, wsem):
    i = pl.program_id(0)
    used = i < nu_ref[0]
    slot = lax.rem(seg_ref[i], 2)

    def weight_copies(expert, s):
        return [pltpu.make_async_copy(wg_hbm.at[expert], wg_buf.at[s], wsem.at[s]),
                pltpu.make_async_copy(wu_hbm.at[expert], wu_buf.at[s], wsem.at[s]),
                pltpu.make_async_copy(wd_hbm.at[expert], wd_buf.at[s], wsem.at[s])]

    @pl.when(i == 0)
    def _():
        for cp in weight_copies(te_ref[0], 0):
            cp.start()

    @pl.when(used & ((i == 0) | (te_ref[i] != te_ref[jnp.maximum(i - 1, 0)])))
    def _():
        for cp in weight_copies(te_ref[i], slot):
            cp.wait()

        @pl.when(nxt_ref[i] >= 0)
        def _():
            for cp in weight_copies(nxt_ref[i], 1 - slot):
                cp.start()

        wgu_b_ref[:, 0:D_EXPERT] = wg_buf[slot].astype(BF16)
        wgu_b_ref[:, D_EXPERT:2 * D_EXPERT] = wu_buf[slot].astype(BF16)
        wd_b_ref[...] = wd_buf[slot].astype(BF16)

    @pl.when(used)
    def _():
        h = jnp.dot(xs_ref[...], wgu_b_ref[...], preferred_element_type=F32)
        act = (_silu(h[:, 0:D_EXPERT]) * h[:, D_EXPERT:2 * D_EXPERT]).astype(BF16)
        ys_ref[...] = jnp.dot(act, wd_b_ref[...], preferred_element_type=F32).astype(BF16)


def _experts(tile_expert, tile_segment, next_expert, n_used, xs, w_gate, w_up, w_down):
    row_tile = lambda i, te, sg, nx, nu: (jnp.minimum(i, nu[0] - 1), 0)
    n_prefetch = 4
    grid_spec = pltpu.PrefetchScalarGridSpec(
        num_scalar_prefetch=n_prefetch,
        grid=(N_TILES,),
        in_specs=[pl.BlockSpec((MOE_TM, D_MODEL), row_tile),
                  pl.BlockSpec(memory_space=pl.ANY), pl.BlockSpec(memory_space=pl.ANY),
                  pl.BlockSpec(memory_space=pl.ANY)],
        out_specs=pl.BlockSpec((MOE_TM, D_MODEL), row_tile),
        scratch_shapes=[pltpu.VMEM((2, D_MODEL, D_EXPERT), F32), pltpu.VMEM((2, D_MODEL, D_EXPERT), F32),
                        pltpu.VMEM((2, D_EXPERT, D_MODEL), F32),
                        pltpu.VMEM((D_MODEL, 2 * D_EXPERT), BF16), pltpu.VMEM((D_EXPERT, D_MODEL), BF16),
                        pltpu.SemaphoreType.DMA((2,))],
    )
    return pl.pallas_call(
        _experts_kernel,
        grid_spec=grid_spec,
        out_shape=jax.ShapeDtypeStruct((N_ROWS, D_MODEL), BF16),
        input_output_aliases={n_prefetch: 0},
        compiler_params=_cparams(("arbitrary",)),
        name="experts",
    )(tile_expert, tile_segment, next_expert, n_used, xs, w_gate, w_up, w_down)


def _combine_kernel(nb_ref, ns_ref, tab_ref, tab_next_ref, route_ref, x1_ref, g2_ref, ys_ref, o_ref,
                    gbuf_ref, sems):
    i = pl.program_id(0)
    last = pl.num_programs(0) - 1
    buf = lax.rem(i, 2)

    def piece(b):
        def make(local, sorted_row, rows):
            return pltpu.make_async_copy(
                ys_ref.at[pl.ds(pl.multiple_of(sorted_row, RUN_ALIGN), rows)],
                gbuf_ref.at[b, pl.ds(pl.multiple_of(local, RUN_ALIGN), rows)], sems.at[b])
        return make

    @pl.when(i == 0)
    def _():
        gbuf_ref[...] = jnp.zeros_like(gbuf_ref)
        _run_copies(tab_ref, nb_ref[0], ns_ref[0], piece(0), lambda cp: cp.start())

    nxt = jnp.minimum(i + 1, last)

    @pl.when(i < last)
    def _():
        _run_copies(tab_next_ref, nb_ref[nxt], ns_ref[nxt], piece(1 - buf), lambda cp: cp.start())

    rec = route_ref[...]
    slot = lax.broadcasted_iota(I32, (OUT_TM, LOCAL_ROWS), 1).astype(F32)
    weights = (jnp.where(slot == rec[:, 0:1], rec[:, 2:3], 0.0)
               + jnp.where(slot == rec[:, 1:2], rec[:, 3:4], 0.0)).astype(BF16)
    _run_copies(tab_ref, nb_ref[i], ns_ref[i], lambda lo, so, rows: piece(buf)(0, 0, rows), lambda cp: cp.wait())
    moe = jnp.dot(weights, gbuf_ref[buf], preferred_element_type=F32)
    o_ref[...] = x1_ref[...] + g2_ref[0] * moe


def _combine(n_big, n_small, piece_table, route, x1, mod3, ys):
    tm = OUT_TM
    steps_per_batch = SEQ // tm
    grid_spec = pltpu.PrefetchScalarGridSpec(
        num_scalar_prefetch=2,
        grid=(N_TOKEN_TILES,),
        in_specs=[_piece_spec(lambda i, nb, ns: (i, 0, 0)),
                  _piece_spec(lambda i, nb, ns: (jnp.minimum(i + 1, N_TOKEN_TILES - 1), 0, 0)),
                  pl.BlockSpec((tm, ROUTE_W), lambda i, nb, ns: (i, 0)),
                  pl.BlockSpec((tm, D_MODEL), lambda i, nb, ns: (i, 0)),
                  pl.BlockSpec((1, 1, D_MODEL), lambda i, nb, ns: ((i // steps_per_batch) * 6 + 5, 0, 0)),
                  pl.BlockSpec(memory_space=pl.ANY)],
        out_specs=pl.BlockSpec((tm, D_MODEL), lambda i, nb, ns: (i, 0)),
        scratch_shapes=[pltpu.VMEM((2, LOCAL_ROWS, D_MODEL), BF16), pltpu.SemaphoreType.DMA((2,))],
    )
    return pl.pallas_call(
        _combine_kernel,
        grid_spec=grid_spec,
        out_shape=jax.ShapeDtypeStruct((TOKENS, D_MODEL), F32),
        compiler_params=_cparams(("arbitrary",)),
        name="combine",
    )(n_big, n_small, piece_table, piece_table, route, x1, mod3, ys)


def kernel(x, c, positions, norm1_w, norm2_w, w_ada, b_ada, w_in, conv_w, conv_b, dt_bias, a_log,
           d_skip, ssd_norm_w, q_norm_w, k_norm_w, sinks, w_out, w_group, b_group, w_expert, b_expert,
           w_gate, w_up, w_down):
    assert x.shape == (BATCH, SEQ, D_MODEL) and w_in.shape == (D_MODEL, IN_WIDTH)
    x2d = x.reshape(TOKENS, D_MODEL)
    mod = _ada_mod(c, w_ada, b_ada)
    mod3 = mod.reshape(BATCH * 6, 1, D_MODEL)

    qkv, z, xbc, dt, dt_t = _in_proj(x2d, norm1_w, mod3, w_in)
    att = _attention(qkv, positions, q_norm_w, k_norm_w, sinks)
    y = _ssd(xbc, z, dt, dt_t, conv_w, conv_b, dt_bias, a_log, d_skip, ssd_norm_w)

    w_router = jnp.pad(jnp.concatenate([w_group, w_expert], axis=1).astype(F32),
                       ((0, 0), (0, LANES - ROUTER_COLS)))
    b_router = jnp.pad(jnp.concatenate([b_group, b_expert]).astype(F32),
                       (0, LANES - ROUTER_COLS)).reshape(1, LANES)
    x1, h2, route, route_t, tcnt = _out_router(att, y, x2d, mod3, w_out.astype(F32), norm2_w,
                                                w_router, b_router)

    tc = tcnt[:, 0, 0:N_EXPERTS].astype(I32)
    run_rows = ((tc + RUN_ALIGN - 1) // RUN_ALIGN) * RUN_ALIGN
    counts = jnp.sum(run_rows, axis=0)
    padded = ((counts + MOE_TM - 1) // MOE_TM) * MOE_TM
    seg_end = jnp.cumsum(padded)
    seg_start = seg_end - padded
    run_dst = seg_start[None, :] + jnp.cumsum(run_rows, axis=0) - run_rows
    n_used = (seg_end[-1] // MOE_TM).reshape(1)
    last_row = jnp.minimum(jnp.arange(N_TILES, dtype=I32) * MOE_TM, seg_end[-1] - 1)
    tile_expert = jnp.sum((seg_end[None, :] <= last_row[:, None]).astype(I32), axis=1)

    run_local = jnp.cumsum(run_rows, axis=1) - run_rows
    n_big_run = run_rows // BIG_PIECE
    n_small_run = (run_rows // RUN_ALIGN) % 2
    q = jnp.arange(PIECE_SLOTS, dtype=I32)
    experts = jnp.arange(N_EXPERTS, dtype=I32)

    def flat(per_run, local0, dst0, stride):
        end = jnp.cumsum(per_run, axis=1)
        run_of = jnp.sum((end[:, None, :] <= q[None, :, None]).astype(I32), axis=2)
        pick = (run_of[:, :, None] == experts[None, None, :]).astype(I32)
        k = q[None, :] - jnp.sum(pick * (end - per_run)[:, None, :], axis=2)
        local = jnp.sum(pick * local0[:, None, :], axis=2) + stride * k
        dst = jnp.sum(pick * dst0[:, None, :], axis=2) + stride * k
        return end[:, -1], local, dst

    n_big, big_local, big_dst = flat(n_big_run, run_local, run_dst, BIG_PIECE)
    n_small, small_local, small_dst = flat(n_small_run, run_local + BIG_PIECE * n_big_run,
                                           run_dst + BIG_PIECE * n_big_run, 0)
    piece_table = jnp.concatenate([big_local, big_dst, small_local, small_dst], axis=1)
    piece_table = piece_table.astype(I32).reshape(N_TOKEN_TILES, 1, TABLE_W)

    nonempty = padded > 0
    seg_rank = jnp.cumsum(nonempty.astype(I32)) - 1
    later = nonempty[None, :] & (experts[None, :] > experts[:, None])
    next_of = jnp.min(jnp.where(later, experts[None, :], N_EXPERTS), axis=1)
    next_of = jnp.where(next_of == N_EXPERTS, -1, next_of)
    tile_is = (tile_expert[:, None] == experts[None, :]).astype(I32)
    tile_segment = jnp.sum(tile_is * seg_rank[None, :], axis=1)
    next_expert = jnp.sum(tile_is * next_of[None, :], axis=1)

    n_big, n_small = n_big.astype(I32), n_small.astype(I32)
    xs = _dispatch(seg_end.astype(I32), (seg_start + counts).astype(I32), n_big, n_small, piece_table,
                   route_t, h2)
    ys = _experts(tile_expert, tile_segment.astype(I32), next_expert.astype(I32), n_used.astype(I32),
                  xs, w_gate, w_up, w_down)
    out = _combine(n_big, n_small, piece_table, route, x1, mod3, ys)
    return out.reshape(BATCH, SEQ, D_MODEL)
```

```python
import jax
import jax.numpy as jnp
from jax import lax
from jax.experimental import pallas as pl
from jax.experimental.pallas import tpu as pltpu

F32 = jnp.float32
BF16 = jnp.bfloat16
I32 = jnp.int32

D_MODEL = 1024
BATCH = 2
SEQ = 8192
TOKENS = BATCH * SEQ
ATT_HEADS = 8
ATT_KV_HEADS = 2
HEAD_DIM = 64
ATT_WIDTH = ATT_HEADS * HEAD_DIM
KV_WIDTH = ATT_KV_HEADS * HEAD_DIM
ATT_BLOCK = 128
ROPE_DIM = HEAD_DIM // 4
ROPE_THETA = 500000.0
SSD_HEADS = 8
SSD_HEAD_DIM = 64
SSD_WIDTH = SSD_HEADS * SSD_HEAD_DIM
SSD_GROUPS = 2
SSD_STATE = 128
CONV_K = 4
CHUNK = 128
XBC_WIDTH = SSD_WIDTH + 2 * SSD_GROUPS * SSD_STATE
IN_WIDTH = ATT_WIDTH + 2 * KV_WIDTH + SSD_WIDTH + XBC_WIDTH + SSD_HEADS
N_GROUPS = 4
EXPERTS_PER_GROUP = 8
N_EXPERTS = N_GROUPS * EXPERTS_PER_GROUP
TOP_K = 2
D_EXPERT = 256
EPS = 1e-6

LANES = 128
QKV_WIDTH = ATT_WIDTH + 2 * KV_WIDTH
IN_PAD = QKV_WIDTH + SSD_WIDTH + XBC_WIDTH + LANES
NEG_BIG = -1e30

VMEM_LIMIT = 48 * 1024 * 1024


def _cparams(sem):
    return pltpu.CompilerParams(dimension_semantics=sem, vmem_limit_bytes=VMEM_LIMIT)


def _split_bf16(x):
    hi = x.astype(BF16)
    lo = (x - hi.astype(F32)).astype(BF16)
    return hi, lo


ADA_TN = 768


def _ada_kernel(ct_ref, w_ref, b_ref, o_ref):
    ct = ct_ref[...]
    s = ct * jax.nn.sigmoid(ct)
    w = w_ref[...]
    rows = [jnp.sum(s[:, b:b + 1] * w, axis=0, keepdims=True) for b in range(BATCH)]
    o_ref[...] = jnp.concatenate(rows, axis=0) + b_ref[...]


def _ada_mod(c, w_ada, b_ada):
    n = w_ada.shape[1]
    return pl.pallas_call(
        _ada_kernel,
        grid=(n // ADA_TN,),
        in_specs=[pl.BlockSpec((D_MODEL, BATCH), lambda j: (0, 0)),
                  pl.BlockSpec((D_MODEL, ADA_TN), lambda j: (0, j)),
                  pl.BlockSpec((1, ADA_TN), lambda j: (0, j))],
        out_specs=pl.BlockSpec((BATCH, ADA_TN), lambda j: (0, j)),
        out_shape=jax.ShapeDtypeStruct((BATCH, n), F32),
        compiler_params=_cparams(("arbitrary",)),
        name="ada_mod",
    )(c.T, w_ada, b_ada.reshape(1, n))


INPROJ_TM = 512
_INPROJ_CHUNK = 256


def _inproj_kernel(x_ref, nw_ref, sc_ref, sh_ref, wf_ref, wdt_ref, qkv_ref, z_ref, xbc_ref, dt_ref, dtt_ref,
                   w_ref):
    @pl.when(pl.program_id(0) == 0)
    def _():
        for c0 in range(0, IN_PAD - LANES, _INPROJ_CHUNK):
            w_ref[:, c0:c0 + _INPROJ_CHUNK] = wf_ref[:, c0:c0 + _INPROJ_CHUNK].astype(BF16)
        w_ref[:, IN_PAD - LANES:IN_PAD] = wdt_ref[...].astype(BF16)

    x = x_ref[...]
    y = x * lax.rsqrt(jnp.mean(x * x, axis=-1, keepdims=True) + EPS)
    h = (y * nw_ref[...]) * (1.0 + sc_ref[0]) + sh_ref[0]
    hb = h.astype(BF16)

    def proj(c0, c1):
        return jnp.dot(hb, w_ref[:, c0:c1], preferred_element_type=F32)

    for c0 in range(0, QKV_WIDTH, _INPROJ_CHUNK):
        qkv_ref[:, c0:c0 + _INPROJ_CHUNK] = proj(c0, c0 + _INPROJ_CHUNK).astype(BF16)
    base = QKV_WIDTH
    for c0 in range(0, SSD_WIDTH, _INPROJ_CHUNK):
        z_ref[:, c0:c0 + _INPROJ_CHUNK] = proj(base + c0, base + c0 + _INPROJ_CHUNK).astype(BF16)
    base += SSD_WIDTH
    for c0 in range(0, XBC_WIDTH, _INPROJ_CHUNK):
        xbc_ref[:, c0:c0 + _INPROJ_CHUNK] = proj(base + c0, base + c0 + _INPROJ_CHUNK).astype(BF16)
    base += XBC_WIDTH
    dt = proj(base, base + LANES)
    dt_ref[...] = dt
    dtt_ref[...] = dt.T[0:SSD_HEADS, :]


def _in_proj(x2d, norm_w, mod3, w_in):
    tm = INPROJ_TM
    steps_per_batch = SEQ // tm
    w_dt = jnp.pad(w_in[:, IN_WIDTH - SSD_HEADS:IN_WIDTH].astype(F32), ((0, 0), (0, LANES - SSD_HEADS)))
    return pl.pallas_call(
        _inproj_kernel,
        grid=(TOKENS // tm,),
        in_specs=[pl.BlockSpec((tm, D_MODEL), lambda i: (i, 0)),
                  pl.BlockSpec((1, D_MODEL), lambda i: (0, 0)),
                  pl.BlockSpec((1, 1, D_MODEL), lambda i: ((i // steps_per_batch) * 6 + 1, 0, 0)),
                  pl.BlockSpec((1, 1, D_MODEL), lambda i: ((i // steps_per_batch) * 6 + 0, 0, 0)),
                  pl.BlockSpec((D_MODEL, IN_WIDTH), lambda i: (0, 0), pipeline_mode=pl.Buffered(1)),
                  pl.BlockSpec((D_MODEL, LANES), lambda i: (0, 0))],
        out_specs=[pl.BlockSpec((tm, QKV_WIDTH), lambda i: (i, 0)),
                   pl.BlockSpec((tm, SSD_WIDTH), lambda i: (i, 0)),
                   pl.BlockSpec((tm, XBC_WIDTH), lambda i: (i, 0)),
                   pl.BlockSpec((tm, LANES), lambda i: (i, 0)),
                   pl.BlockSpec((SSD_HEADS, tm), lambda i: (0, i))],
        out_shape=[jax.ShapeDtypeStruct((TOKENS, QKV_WIDTH), BF16),
                   jax.ShapeDtypeStruct((TOKENS, SSD_WIDTH), BF16),
                   jax.ShapeDtypeStruct((TOKENS, XBC_WIDTH), BF16),
                   jax.ShapeDtypeStruct((TOKENS, LANES), F32),
                   jax.ShapeDtypeStruct((SSD_HEADS, TOKENS), F32)],
        scratch_shapes=[pltpu.VMEM((D_MODEL, IN_PAD), BF16)],
        compiler_params=_cparams(("arbitrary",)),
        name="in_proj",
    )(x2d, norm_w.reshape(1, D_MODEL), mod3, mod3, w_in.astype(F32), w_dt)


ATT_SUB = 8


ROPE_TM = 2048
_ROPE_HALF = ROPE_DIM // 2
_TOK_PER_ROW = LANES // _ROPE_HALF


def _exact_dot(x, onehot_b):
    hi, lo = _split_bf16(x)
    return (jnp.dot(hi, onehot_b, preferred_element_type=F32)
            + jnp.dot(lo, onehot_b, preferred_element_type=F32))


def _rope_kernel(pos_ref, freq_ref, sel_ref, own_ref, gcos_ref, gs1_ref, gs2_ref, ident_ref,
                 cos_ref, s1_ref, s2_ref):
    ang = pos_ref[...].astype(F32) * freq_ref[...]
    cos_p, sin_p = jnp.cos(ang), jnp.sin(ang)
    hi_c, lo_c = _split_bf16(cos_p)
    hi_s, lo_s = _split_bf16(sin_p)
    sel = sel_ref[...]
    rows_c = jnp.dot(sel, hi_c, preferred_element_type=F32) + jnp.dot(sel, lo_c, preferred_element_type=F32)
    rows_s = jnp.dot(sel, hi_s, preferred_element_type=F32) + jnp.dot(sel, lo_s, preferred_element_type=F32)
    own = own_ref[...]
    cos_ref[...] = _exact_dot(rows_c * own, gcos_ref[...]) + ident_ref[...]
    s1_ref[...] = _exact_dot(rows_s * own, gs1_ref[...])
    s2_ref[...] = _exact_dot(rows_s * own, gs2_ref[...])


def _rope_tables(positions):
    half, per_row = _ROPE_HALF, _TOK_PER_ROW
    rows = ROPE_TM // per_row
    pos_rep = jnp.repeat(positions.reshape(TOKENS).astype(I32), half).reshape(TOKENS // per_row, LANES)
    inv_freq = jnp.power(ROPE_THETA, -jnp.arange(half, dtype=F32) * 2.0 / ROPE_DIM)
    freq = jnp.tile(inv_freq, per_row).reshape(1, LANES)
    tok = jnp.arange(ROPE_TM)
    lane = jnp.arange(LANES)
    sel = (tok[:, None] // per_row == jnp.arange(rows)[None, :]).astype(BF16)
    own = (lane[None, :] // half == tok[:, None] % per_row).astype(F32)
    d = lane % HEAD_DIM
    src_f = lane % half
    hits = lambda lo, hi: ((src_f[:, None] == d[None, :] % half) & (d[None, :] >= lo) & (d[None, :] < hi))
    gcos = hits(0, ROPE_DIM).astype(BF16)
    gs1 = -hits(0, half).astype(BF16)
    gs2 = hits(half, ROPE_DIM).astype(BF16)
    ident = (d >= ROPE_DIM).astype(F32).reshape(1, LANES)
    const = lambda shape: pl.BlockSpec(shape, lambda i: (0, 0))
    out_spec = pl.BlockSpec((ROPE_TM, LANES), lambda i: (i, 0))
    out = jax.ShapeDtypeStruct((TOKENS, LANES), F32)
    return pl.pallas_call(
        _rope_kernel,
        grid=(TOKENS // ROPE_TM,),
        in_specs=[pl.BlockSpec((rows, LANES), lambda i: (i, 0)), const((1, LANES)),
                  const((ROPE_TM, rows)), const((ROPE_TM, LANES)),
                  const((LANES, LANES)), const((LANES, LANES)), const((LANES, LANES)), const((1, LANES))],
        out_specs=[out_spec, out_spec, out_spec],
        out_shape=[out, out, out],
        compiler_params=_cparams(("arbitrary",)),
        name="rope_tables",
    )(pos_rep, freq, sel, own, gcos, gs1, gs2, ident)


def _seg_meansq(xf, ones128):
    rows, width = xf.shape
    nt = width // LANES
    parts = _split_bf16(xf * xf)
    stacked = jnp.concatenate([p[:, t * LANES:(t + 1) * LANES] for p in parts for t in range(nt)], axis=0)
    tot = jnp.dot(stacked, ones128, preferred_element_type=F32)
    tiles = [tot[t * rows:(t + 1) * rows] + tot[(nt + t) * rows:(nt + t + 1) * rows] for t in range(nt)]
    return jnp.concatenate(tiles, axis=1) * (1.0 / HEAD_DIM)


def _norm_rope(x_bf, w_row, ones_bd, cosf, s1, s2):
    xf = x_bf.astype(F32)
    width = xf.shape[1]
    xn = xf * lax.rsqrt(_seg_meansq(xf, ones_bd) + EPS) * w_row
    half = ROPE_DIM // 2
    up = pltpu.roll(xn, width - half, axis=1)
    down = pltpu.roll(xn, half, axis=1)
    return xn * cosf + up * s1 + down * s2


def _attn_kernel(sink_ref, q_ref, kv_ref, cos_ref, s1_ref, s2_ref, qw_ref, kw_ref,
                 ones_ref, o_ref, kprev_ref, vprev_ref):
    j = pl.program_id(1)
    blk = ATT_BLOCK

    @pl.when(j == 0)
    def _():
        kprev_ref[...] = jnp.zeros_like(kprev_ref)
        vprev_ref[...] = jnp.zeros_like(vprev_ref)

    cos1 = cos_ref[...]
    s1_1 = s1_ref[...]
    s2_1 = s2_ref[...]
    reps = ATT_WIDTH // LANES
    cosq = jnp.concatenate([cos1] * reps, axis=1)
    s1q = jnp.concatenate([s1_1] * reps, axis=1)
    s2q = jnp.concatenate([s2_1] * reps, axis=1)

    q = _norm_rope(q_ref[...], qw_ref[...], ones_ref[...], cosq, s1q, s2q)
    qf = q * (HEAD_DIM ** -0.5)
    kv = kv_ref[...]
    kn = _norm_rope(kv[:, 0:KV_WIDTH], kw_ref[...], ones_ref[...], cos1, s1_1, s2_1)
    vn = kv[:, KV_WIDTH:2 * KV_WIDTH].astype(F32)

    kall = jnp.concatenate([kprev_ref[...], kn], axis=0)
    vall = jnp.concatenate([vprev_ref[...], vn], axis=0)
    kprev_ref[...] = kn[(ATT_SUB - 1) * blk:ATT_SUB * blk]
    vprev_ref[...] = vn[(ATT_SUB - 1) * blk:ATT_SUB * blk]

    lo_all = lax.broadcasted_iota(I32, kall.shape, 1) < HEAD_DIM
    ones_all = jnp.ones(kall.shape, BF16)

    row = lax.broadcasted_iota(I32, (2 * blk, blk), 0)
    col = lax.broadcasted_iota(I32, (2 * blk, blk), 1)
    from_prev = col > (row & (blk - 1))
    second_tile = lax.broadcasted_iota(I32, (2 * blk, 1), 0) >= blk
    zero_p = jnp.zeros((2 * blk, blk), F32)

    k_par, v_par = [], []
    for g in range(ATT_KV_HEADS):
        keep = lo_all if g == 0 else ~lo_all
        k_own = jnp.where(keep, kall, 0.0)
        v_own = jnp.where(keep, vall, 0.0)
        k_oth = pltpu.roll(k_own, HEAD_DIM, axis=1)
        v_oth = pltpu.roll(v_own, HEAD_DIM, axis=1)
        k_lo, k_hi = (k_own, k_oth) if g == 0 else (k_oth, k_own)
        v_lo, v_hi = (v_own, v_oth) if g == 0 else (v_oth, v_own)
        k_par.append((k_lo.astype(BF16), k_hi.astype(BF16)))
        v_par.append((jnp.concatenate([v_lo.astype(BF16), ones_all], axis=1),
                      jnp.concatenate([v_hi.astype(BF16), ones_all], axis=1)))

    problems = [(g, sub) for g in range(ATT_KV_HEADS) for sub in range(ATT_SUB)]
    scores = []
    for g, sub in problems:
        r0, c0 = sub * blk, g * 2 * LANES
        qcat = jnp.concatenate([qf[r0:r0 + blk, c0:c0 + LANES],
                                qf[r0:r0 + blk, c0 + LANES:c0 + 2 * LANES]], axis=0).astype(BF16)
        kw = jnp.concatenate([k_par[g][0][r0:r0 + 2 * blk], k_par[g][1][r0:r0 + 2 * blk]], axis=0)
        scores.append(lax.dot_general(qcat, kw, (((1,), (1,)), ((), ())),
                                      preferred_element_type=F32))

    weights, rescale = [], []
    for (g, sub), s_all in zip(problems, scores):
        for par in range(2):
            s = s_all[:, par * 2 * blk:(par + 1) * 2 * blk]
            s_prev = s[:, 0:blk]
            if sub == 0:
                s_prev = s_prev + jnp.where(j > 0, 0.0, NEG_BIG)
            s = jnp.where(from_prev, s_prev, s[:, blk:2 * blk])
            h_first = ATT_HEADS // ATT_KV_HEADS * g + par
            sink = jnp.where(second_tile, sink_ref[h_first + 2], sink_ref[h_first])
            m = jnp.maximum(jnp.max(s, axis=-1, keepdims=True), sink)
            p = jnp.exp(s - m)
            weights.append(jnp.concatenate([jnp.where(from_prev, p, zero_p), jnp.where(from_prev, zero_p, p)],
                                           axis=1).astype(BF16))
            rescale.append(jnp.exp(sink - m))

    outs = []
    for idx, (g, sub) in enumerate(problems):
        for par in range(2):
            outs.append(jnp.dot(weights[2 * idx + par], v_par[g][par][sub * blk:(sub + 2) * blk],
                                preferred_element_type=F32))

    for idx, (g, sub) in enumerate(problems):
        r0, c0 = sub * blk, g * 2 * LANES
        pair = None
        for par in range(2):
            o = outs[2 * idx + par]
            part = o[:, 0:LANES] * (1.0 / (o[:, LANES:2 * LANES] + rescale[2 * idx + par]))
            pair = part if pair is None else pair + part
        o_ref[r0:r0 + blk, c0:c0 + LANES] = pair[0:blk].astype(BF16)
        o_ref[r0:r0 + blk, c0 + LANES:c0 + 2 * LANES] = pair[blk:2 * blk].astype(BF16)


def _attention(qkv, positions, q_norm_w, k_norm_w, sinks):
    cosf, s1, s2 = _rope_tables(positions)
    qw = jnp.tile(q_norm_w.astype(F32), ATT_HEADS).reshape(1, ATT_WIDTH)
    kw = jnp.tile(k_norm_w.astype(F32), ATT_KV_HEADS).reshape(1, KV_WIDTH)
    seg = jnp.arange(LANES) // HEAD_DIM
    ones128 = (seg[:, None] == seg[None, :]).astype(BF16)
    const = lambda shape: pl.BlockSpec(shape, lambda b, j, s: (0, 0))
    rows = ATT_SUB * ATT_BLOCK
    nb = SEQ // rows
    tok = lambda width, cb: pl.BlockSpec((rows, width), lambda b, j, s: (b * nb + j, cb))
    grid_spec = pltpu.PrefetchScalarGridSpec(
        num_scalar_prefetch=1,
        grid=(BATCH, nb),
        in_specs=[tok(ATT_WIDTH, 0), tok(2 * KV_WIDTH, 2), tok(LANES, 0), tok(LANES, 0), tok(LANES, 0),
                  const((1, ATT_WIDTH)), const((1, KV_WIDTH)), const((LANES, LANES))],
        out_specs=tok(ATT_WIDTH, 0),
        scratch_shapes=[pltpu.VMEM((ATT_BLOCK, KV_WIDTH), F32),
                        pltpu.VMEM((ATT_BLOCK, KV_WIDTH), F32)],
    )
    return pl.pallas_call(
        _attn_kernel,
        grid_spec=grid_spec,
        out_shape=jax.ShapeDtypeStruct((TOKENS, ATT_WIDTH), BF16),
        compiler_params=_cparams(("arbitrary", "arbitrary")),
        name="attention",
    )(sinks.astype(F32), qkv, qkv, cosf, s1, s2, qw, kw, ones128)


SSD_SUB = 4


def _softplus(x):
    return jnp.maximum(x, 0.0) + jnp.log1p(jnp.exp(-jnp.abs(x)))


def _silu(x):
    h = 0.5 * x
    return h + h * jnp.tanh(h)


def _ssd_kernel(xbc_ref, z_ref, dt_ref, dtt_ref, cw_ref, cb_ref, dtb_row_ref, dtb_col_ref,
                alog_row_ref, alog_col_ref, dskip_ref, nw_ref, tril_ref, triu_ref,
                o_ref, conv_ref, state_ref):
    c = pl.program_id(1)
    L = CHUNK
    tail = 8

    @pl.when(c == 0)
    def _():
        conv_ref[0:tail, :] = jnp.zeros((tail, XBC_WIDTH), F32)
        state_ref[...] = jnp.zeros_like(state_ref)

    row = lax.broadcasted_iota(I32, (L, L), 0)
    col = lax.broadcasted_iota(I32, (L, L), 1)
    causal = col <= row
    lane = lax.broadcasted_iota(I32, (L, LANES), 1)
    lo_half = lane < SSD_HEAD_DIM

    prepared = [_ssd_prepare(s * L, xbc_ref, dt_ref, dtt_ref, cw_ref, cb_ref, dtb_row_ref, dtb_col_ref,
                             alog_row_ref, alog_col_ref, tril_ref, triu_ref, conv_ref)
                for s in range(SSD_SUB)]
    for s in range(SSD_SUB):
        _ssd_chunk(s * L, prepared[s], causal, lo_half, z_ref, dskip_ref, nw_ref, o_ref, state_ref)


def _ssd_prepare(r0, xbc_ref, dt_ref, dtt_ref, cw_ref, cb_ref, dtb_row_ref, dtb_col_ref,
                 alog_row_ref, alog_col_ref, tril_ref, triu_ref, conv_ref):
    L = CHUNK
    tail = 8
    xb = xbc_ref[r0:r0 + L, :].astype(F32)
    conv_ref[tail:tail + L, :] = xb
    acc = cb_ref[...] + cw_ref[CONV_K - 1:CONV_K, :] * xb
    for k in range(CONV_K - 1):
        off = tail - (CONV_K - 1) + k
        acc = acc + cw_ref[k:k + 1, :] * conv_ref[off:off + L, :]
    conv_ref[0:tail, :] = xb[L - tail:L, :]
    u = _silu(acc)
    xs = u[:, 0:SSD_WIDTH]
    bmat = u[:, SSD_WIDTH:SSD_WIDTH + SSD_GROUPS * SSD_STATE]
    cmat = u[:, SSD_WIDTH + SSD_GROUPS * SSD_STATE:XBC_WIDTH]

    dt = _softplus(dt_ref[r0:r0 + L, :] + dtb_row_ref[...])
    a = dt * (-jnp.exp(alog_row_ref[...]))
    a_hi, a_lo = _split_bf16(a)
    a_cum = (jnp.dot(tril_ref[...], a_hi, preferred_element_type=F32)
             + jnp.dot(tril_ref[...], a_lo, preferred_element_type=F32))
    dt_t = _softplus(dtt_ref[:, r0:r0 + L] + dtb_col_ref[...])
    a_t = dt_t * (-jnp.exp(alog_col_ref[...]))
    at_hi, at_lo = _split_bf16(a_t)
    a_cum_t = (jnp.dot(at_hi, triu_ref[...], preferred_element_type=F32)
               + jnp.dot(at_lo, triu_ref[...], preferred_element_type=F32))
    a_end_t = a_cum_t[:, L - 1:L]
    return dict(
        xs=xs, bmat=bmat, cmat=cmat, a_cum=a_cum, exp_a_cum=jnp.exp(a_cum),
        shifted_t=a_cum_t - jnp.log(dt_t),
        wst_t=jnp.exp(a_end_t - a_cum_t) * dt_t,
        cdec_t=jnp.exp(a_end_t))


def _ssd_chunk(r0, p, causal, lo_half, z_ref, dskip_ref, nw_ref, o_ref, state_ref):
    L = CHUNK
    xs, bmat, cmat, a_cum, exp_a_cum = p["xs"], p["bmat"], p["cmat"], p["a_cum"], p["exp_a_cum"]
    shifted_t, wst_t, cdec_t = p["shifted_t"], p["wst_t"], p["cdec_t"]
    xs_b = xs.astype(BF16)
    heads_per_group = SSD_HEADS // SSD_GROUPS
    gated = []
    for g in range(SSD_GROUPS):
        b_g = bmat[:, g * SSD_STATE:(g + 1) * SSD_STATE]
        c_g = cmat[:, g * SSD_STATE:(g + 1) * SSD_STATE]
        cb = lax.dot_general(c_g.astype(BF16), b_g.astype(BF16), (((1,), (1,)), ((), ())),
                             preferred_element_type=F32)
        b_gt = b_g.T
        for t in range(heads_per_group // 2):
            tile = g * (heads_per_group // 2) + t
            c0 = tile * LANES
            xs_tile = xs_b[:, c0:c0 + LANES]
            st_tile = state_ref[:, c0:c0 + LANES]
            st_b = st_tile.astype(BF16)
            y_tile = jnp.zeros((L, LANES), F32)
            new_tile = jnp.zeros((SSD_STATE, LANES), F32)
            for e in range(2):
                h = 2 * tile + e
                keep = lo_half if e == 0 else ~lo_half
                colb = jnp.broadcast_to(a_cum[:, h:h + 1], (L, L))
                rowb = shifted_t[h:h + 1, :]
                w_in = cb * jnp.exp(jnp.where(causal, colb - rowb, NEG_BIG))
                w_off = c_g * jnp.broadcast_to(exp_a_cum[:, h:h + 1], (L, L))
                lhs = jnp.concatenate([w_in, w_off], axis=1).astype(BF16)
                rhs = jnp.concatenate([jnp.where(keep, xs_tile, jnp.zeros_like(xs_tile)),
                                       jnp.where(keep, st_b, jnp.zeros_like(st_b))], axis=0)
                y_tile = y_tile + jnp.dot(lhs, rhs, preferred_element_type=F32)
                m_h = (b_gt * wst_t[h:h + 1, :]).astype(BF16)
                new_tile = new_tile + jnp.dot(m_h, jnp.where(keep, xs_tile, jnp.zeros_like(xs_tile)),
                                              preferred_element_type=F32)
            cd = jnp.where(lo_half[0:1, :], cdec_t[2 * tile:2 * tile + 1, :],
                           cdec_t[2 * tile + 1:2 * tile + 2, :])
            state_ref[:, c0:c0 + LANES] = st_tile * cd + new_tile
            y_full = y_tile + dskip_ref[:, c0:c0 + LANES] * xs[:, c0:c0 + LANES]
            gated.append(y_full * _silu(z_ref[r0:r0 + L, c0:c0 + LANES].astype(F32)))

    gw = SSD_WIDTH // SSD_GROUPS
    tiles_per_group = gw // LANES
    for g in range(SSD_GROUPS):
        yg = jnp.concatenate(gated[g * tiles_per_group:(g + 1) * tiles_per_group], axis=1)
        ms = jnp.mean(yg * yg, axis=-1, keepdims=True)
        o_ref[r0:r0 + L, g * gw:(g + 1) * gw] = (
            (yg * lax.rsqrt(ms + EPS)) * nw_ref[:, g * gw:(g + 1) * gw]).astype(o_ref.dtype)


def _ssd(xbc, z, dt, dt_t, conv_w, conv_b, dt_bias, a_log, d_skip, ssd_norm_w):
    L = SSD_SUB * CHUNK
    nc = SEQ // L
    pad_row = lambda v: jnp.pad(v.astype(F32), (0, LANES - SSD_HEADS)).reshape(1, LANES)
    col8 = lambda v: v.astype(F32).reshape(SSD_HEADS, 1)
    idx = jnp.arange(CHUNK)
    tril = (idx[None, :] <= idx[:, None]).astype(BF16)
    triu = (idx[:, None] <= idx[None, :]).astype(BF16)
    dskip = jnp.repeat(d_skip.astype(F32), SSD_HEAD_DIM).reshape(1, SSD_WIDTH)
    const = lambda shape: pl.BlockSpec(shape, lambda b, c: (0, 0))
    tok = lambda width: pl.BlockSpec((L, width), lambda b, c: (b * nc + c, 0))
    return pl.pallas_call(
        _ssd_kernel,
        grid=(BATCH, nc),
        in_specs=[tok(XBC_WIDTH), tok(SSD_WIDTH), tok(LANES),
                  pl.BlockSpec((SSD_HEADS, L), lambda b, c: (0, b * nc + c)),
                  const((CONV_K, XBC_WIDTH)), const((1, XBC_WIDTH)),
                  const((1, LANES)), const((SSD_HEADS, 1)), const((1, LANES)), const((SSD_HEADS, 1)),
                  const((1, SSD_WIDTH)), const((1, SSD_WIDTH)), const((CHUNK, CHUNK)), const((CHUNK, CHUNK))],
        out_specs=tok(SSD_WIDTH),
        out_shape=jax.ShapeDtypeStruct((TOKENS, SSD_WIDTH), BF16),
        scratch_shapes=[pltpu.VMEM((8 + CHUNK, XBC_WIDTH), F32),
                        pltpu.VMEM((SSD_STATE, SSD_WIDTH), F32)],
        compiler_params=_cparams(("arbitrary", "arbitrary")),
        name="ssd",
    )(xbc, z, dt, dt_t, conv_w.astype(F32), conv_b.astype(F32).reshape(1, XBC_WIDTH),
      pad_row(dt_bias), col8(dt_bias), pad_row(a_log), col8(a_log), dskip,
      ssd_norm_w.astype(F32).reshape(1, SSD_WIDTH), tril, triu)


OUT_TM = 512
ROUTE_W = 8
ROUTER_COLS = N_GROUPS + N_EXPERTS
RUN_ALIGN = 16
RUN_SHIFT = 4
LOCAL_ROWS = 1536
assert RUN_ALIGN == 1 << RUN_SHIFT and LOCAL_ROWS >= TOP_K * OUT_TM + N_EXPERTS * (RUN_ALIGN - 1)


def _lane_pick(values, lane, index):
    return jnp.sum(jnp.where(lane == index, values, 0.0), axis=-1, keepdims=True)


def _first_argmax(vals, lane):
    m = jnp.max(vals, axis=-1, keepdims=True)
    idx = jnp.min(jnp.where(vals == m, lane, float(LANES)), axis=-1, keepdims=True)
    return m, idx


def _out_router_kernel(att_ref, y_ref, x_ref, g1_ref, wof_ref, nw_ref, sc_ref, sh_ref, wr_ref, br_ref,
                       ltri_ref, sut_ref, x1_ref, h2_ref, route_ref, routet_ref, tcnt_ref,
                       wr_split_ref, logits_ref, wo_ref):
    i = pl.program_id(0)

    @pl.when(i == 0)
    def _():
        hi, lo = _split_bf16(wr_ref[...])
        wr_split_ref[:, 0:LANES] = hi
        wr_split_ref[:, LANES:2 * LANES] = lo
        logits_ref[...] = jnp.zeros_like(logits_ref)
        for r0 in range(0, D_MODEL, 256):
            wo_ref[r0:r0 + 256, :] = wof_ref[r0:r0 + 256, :].astype(BF16)

    logits = logits_ref[...]

    mixer = (jnp.dot(att_ref[...], wo_ref[0:ATT_WIDTH, :], preferred_element_type=F32)
             + jnp.dot(y_ref[...], wo_ref[ATT_WIDTH:ATT_WIDTH + SSD_WIDTH, :], preferred_element_type=F32))
    x1 = x_ref[...] + g1_ref[0] * mixer
    x1_ref[...] = x1
    yn = x1 * lax.rsqrt(jnp.mean(x1 * x1, axis=-1, keepdims=True) + EPS)
    h2 = (yn * nw_ref[...]) * (1.0 + sc_ref[0]) + sh_ref[0]
    h2_ref[...] = h2.astype(BF16)

    h_hi, h_lo = _split_bf16(h2)
    both = jnp.dot(h_hi, wr_split_ref[...], preferred_element_type=F32)
    logits_ref[...] = (both[:, 0:LANES] + both[:, LANES:2 * LANES]
                       + jnp.dot(h_lo, wr_split_ref[:, 0:LANES], preferred_element_type=F32)) + br_ref[...]

    tm = logits.shape[0]
    lane = lax.broadcasted_iota(I32, (tm, LANES), 1).astype(F32)

    gl = jnp.where(lane < N_GROUPS, logits, NEG_BIG)
    gmax, gidx = _first_argmax(gl, lane)
    g_p = 1.0 / jnp.sum(jnp.exp(gl - gmax), axis=-1, keepdims=True)

    lo_lane = N_GROUPS + EXPERTS_PER_GROUP * gidx
    el = jnp.where((lane >= lo_lane) & (lane < lo_lane + EXPERTS_PER_GROUP), logits, NEG_BIG)
    m1, i1 = _first_argmax(el, lane)
    m2, i2 = _first_argmax(jnp.where(lane == i1, NEG_BIG, el), lane)
    r = jnp.exp(m2 - m1)
    p1 = 1.0 / (1.0 + r)
    p2 = r / (1.0 + r)
    e0 = i1 - N_GROUPS
    e1 = i2 - N_GROUPS

    onehot = ((lane == e0) | (lane == e1)).astype(F32)
    tile_cnt = jnp.sum(onehot, axis=0, keepdims=True)
    run_len = jnp.floor((tile_cnt + (RUN_ALIGN - 1)) * (1.0 / RUN_ALIGN)) * RUN_ALIGN
    run_start = jnp.dot(jnp.broadcast_to(run_len, (8, LANES)).astype(BF16), sut_ref[...],
                        preferred_element_type=F32)[0:1, :]
    before = jnp.dot(ltri_ref[...], onehot.astype(BF16), preferred_element_type=F32) + run_start
    slot0 = _lane_pick(before, lane, e0)
    slot1 = _lane_pick(before, lane, e1)
    tcnt_ref[0] = tile_cnt

    rec = jnp.zeros((tm, LANES), F32)
    for k, v in enumerate([slot0, slot1, g_p * p1, g_p * p2, e0, e1]):
        rec = jnp.where(lane == k, v, rec)
    route_ref[...] = rec[:, 0:ROUTE_W]
    routet_ref[...] = rec.T[0:ROUTE_W, :]


def _out_router(att, y, x2d, mod3, w_out_b, norm_w, w_router, b_router):
    tm = OUT_TM
    n_steps = TOKENS // tm
    steps_per_batch = SEQ // tm
    idx = jnp.arange(tm)
    ltri = (idx[None, :] < idx[:, None]).astype(BF16)
    lidx = jnp.arange(LANES)
    sut = (lidx[:, None] < lidx[None, :]).astype(BF16)
    const = lambda shape: pl.BlockSpec(shape, lambda i: (0, 0))
    cur = lambda i: jnp.minimum(i, n_steps - 1)
    prev = lambda i: jnp.maximum(i - 1, 0)
    tok = lambda width: pl.BlockSpec((tm, width), lambda i: (cur(i), 0))
    modspec = lambda k: pl.BlockSpec((1, 1, D_MODEL), lambda i: ((cur(i) // steps_per_batch) * 6 + k, 0, 0))
    return pl.pallas_call(
        _out_router_kernel,
        grid=(n_steps + 1,),
        in_specs=[tok(ATT_WIDTH), tok(SSD_WIDTH), tok(D_MODEL), modspec(2),
                  const((D_MODEL, D_MODEL)), const((1, D_MODEL)), modspec(4), modspec(3),
                  const((D_MODEL, LANES)), const((1, LANES)), const((tm, tm)), const((LANES, LANES))],
        out_specs=[tok(D_MODEL), tok(D_MODEL),
                   pl.BlockSpec((tm, ROUTE_W), lambda i: (prev(i), 0)),
                   pl.BlockSpec((ROUTE_W, tm), lambda i: (prev(i), 0)),
                   pl.BlockSpec((1, 1, LANES), lambda i: (prev(i), 0, 0))],
        out_shape=[jax.ShapeDtypeStruct((TOKENS, D_MODEL), F32),
                   jax.ShapeDtypeStruct((TOKENS, D_MODEL), BF16),
                   jax.ShapeDtypeStruct((TOKENS, ROUTE_W), F32),
                   jax.ShapeDtypeStruct((n_steps * ROUTE_W, tm), F32),
                   jax.ShapeDtypeStruct((n_steps, 1, LANES), F32)],
        scratch_shapes=[pltpu.VMEM((D_MODEL, 2 * LANES), BF16), pltpu.VMEM((tm, LANES), F32),
                        pltpu.VMEM((D_MODEL, D_MODEL), BF16)],
        compiler_params=_cparams(("arbitrary",)),
        name="out_router",
    )(att, y, x2d, mod3, w_out_b, norm_w.reshape(1, D_MODEL), mod3, mod3, w_router, b_router, ltri, sut)


MOE_TM = 512
ZERO_ROWS = 256
N_TOKEN_TILES = TOKENS // OUT_TM
MAX_SORTED_ROWS = TOKENS * TOP_K + N_TOKEN_TILES * N_EXPERTS * (RUN_ALIGN - 1)
N_TILES = MAX_SORTED_ROWS // MOE_TM + N_EXPERTS
N_ROWS = N_TILES * MOE_TM
assert MOE_TM % ZERO_ROWS == 0


BIG_PIECE = 2 * RUN_ALIGN
PIECE_SLOTS = LOCAL_ROWS // BIG_PIECE
TABLE_W = 4 * PIECE_SLOTS
COMBINE_K = 256
assert PIECE_SLOTS >= N_EXPERTS and LOCAL_ROWS % COMBINE_K == 0


def _run_copies(table_ref, n_big, n_small, make_copy, action):
    def big(q, carry):
        action(make_copy(table_ref[0, 0, q], table_ref[0, 0, PIECE_SLOTS + q], BIG_PIECE))
        return carry

    def small(q, carry):
        action(make_copy(table_ref[0, 0, 2 * PIECE_SLOTS + q], table_ref[0, 0, 3 * PIECE_SLOTS + q], RUN_ALIGN))
        return carry

    lax.fori_loop(0, n_big, big, 0)
    lax.fori_loop(0, n_small, small, 0)


def _dispatch_kernel(seg_end_ref, used_end_ref, nb_ref, ns_ref, tab_ref, routet_ref, h2_ref, xs_ref,
                     sbuf_ref, zero_ref, sems, zsem):
    i = pl.program_id(0)
    last = pl.num_programs(0) - 1
    buf = lax.rem(i, 2)

    def zero_fills(action):
        def tail_copy(row):
            return pltpu.make_async_copy(zero_ref.at[pl.ds(0, RUN_ALIGN)],
                                         xs_ref.at[pl.ds(pl.multiple_of(row, RUN_ALIGN), RUN_ALIGN)], zsem)

        def block_copy(block):
            start = pl.multiple_of(block * ZERO_ROWS, ZERO_ROWS)
            return pltpu.make_async_copy(zero_ref, xs_ref.at[pl.ds(start, ZERO_ROWS)], zsem)

        def tails(e, carry):
            def body(r, c):
                action(tail_copy(r * RUN_ALIGN))
                return c

            lax.fori_loop(used_end_ref[e] // RUN_ALIGN, seg_end_ref[e] // RUN_ALIGN, body, 0)
            return carry

        def blocks(block, carry):
            action(block_copy(block))
            return carry

        lax.fori_loop(0, N_EXPERTS, tails, 0)
        lax.fori_loop(seg_end_ref[N_EXPERTS - 1] // ZERO_ROWS, N_ROWS // ZERO_ROWS, blocks, 0)

    @pl.when(i == 0)
    def _():
        zero_ref[...] = jnp.zeros_like(zero_ref)
        zero_fills(lambda cp: cp.start())

    slot = lax.broadcasted_iota(I32, (LOCAL_ROWS, OUT_TM), 0).astype(F32)
    perm = jnp.where((slot == routet_ref[0:1, :]) | (slot == routet_ref[1:2, :]), 1.0, 0.0).astype(BF16)
    sbuf_ref[buf] = jnp.dot(perm, h2_ref[...], preferred_element_type=F32).astype(BF16)

    def piece(b):
        def make(local, sorted_row, rows):
            return pltpu.make_async_copy(
                sbuf_ref.at[b, pl.ds(pl.multiple_of(local, RUN_ALIGN), rows)],
                xs_ref.at[pl.ds(pl.multiple_of(sorted_row, RUN_ALIGN), rows)], sems.at[b])
        return make

    _run_copies(tab_ref, nb_ref[i], ns_ref[i], piece(buf), lambda cp: cp.start())
    prev = jnp.maximum(i - 1, 0)

    @pl.when(i > 0)
    def _():
        _run_copies(tab_ref, nb_ref[prev], ns_ref[prev], lambda lo, so, rows: piece(1 - buf)(0, 0, rows),
                    lambda cp: cp.wait())

    @pl.when(i == last)
    def _():
        _run_copies(tab_ref, nb_ref[i], ns_ref[i], lambda lo, so, rows: piece(buf)(0, 0, rows),
                    lambda cp: cp.wait())
        zero_fills(lambda cp: cp.wait())


def _piece_spec(index_map):
    return pl.BlockSpec((1, 1, TABLE_W), index_map, memory_space=pltpu.SMEM)


def _dispatch(seg_end, used_end, n_big, n_small, piece_table, route_t, h2):
    grid_spec = pltpu.PrefetchScalarGridSpec(
        num_scalar_prefetch=4,
        grid=(N_TOKEN_TILES,),
        in_specs=[_piece_spec(lambda i, se, ue, nb, ns: (i, 0, 0)),
                  pl.BlockSpec((ROUTE_W, OUT_TM), lambda i, se, ue, nb, ns: (i, 0)),
                  pl.BlockSpec((OUT_TM, D_MODEL), lambda i, se, ue, nb, ns: (i, 0))],
        out_specs=pl.BlockSpec(memory_space=pl.ANY),
        scratch_shapes=[pltpu.VMEM((2, LOCAL_ROWS, D_MODEL), BF16),
                        pltpu.VMEM((ZERO_ROWS, D_MODEL), BF16),
                        pltpu.SemaphoreType.DMA((2,)), pltpu.SemaphoreType.DMA],
    )
    return pl.pallas_call(
        _dispatch_kernel,
        grid_spec=grid_spec,
        out_shape=jax.ShapeDtypeStruct((N_ROWS, D_MODEL), BF16),
        compiler_params=_cparams(("arbitrary",)),
        name="dispatch",
    )(seg_end, used_end, n_big, n_small, piece_table, route_t, h2)


def _experts_kernel(te_ref, seg_ref, nxt_ref, nu_ref, xs_ref, wg_hbm, wu_hbm, wd_hbm, ys_ref,
                    wg_buf, wu_buf, wd_buf, wgu_b_ref, wd_b_ref, wsem):
    i = pl.program_id(0)
    used = i < nu_ref[0]
    slot = lax.rem(seg_ref[i], 2)

    def weight_copies(expert, s):
        return [pltpu.make_async_copy(wg_hbm.at[expert], wg_buf.at[s], wsem.at[s]),
                pltpu.make_async_copy(wu_hbm.at[expert], wu_buf.at[s], wsem.at[s]),
                pltpu.make_async_copy(wd_hbm.at[expert], wd_buf.at[s], wsem.at[s])]

    @pl.when(i == 0)
    def _():
        for cp in weight_copies(te_ref[0], 0):
            cp.start()

    @pl.when(used & ((i == 0) | (te_ref[i] != te_ref[jnp.maximum(i - 1, 0)])))
    def _():
        for cp in weight_copies(te_ref[i], slot):
            cp.wait()

        @pl.when(nxt_ref[i] >= 0)
        def _():
            for cp in weight_copies(nxt_ref[i], 1 - slot):
                cp.start()

        wgu_b_ref[:, 0:D_EXPERT] = wg_buf[slot].astype(BF16)
        wgu_b_ref[:, D_EXPERT:2 * D_EXPERT] = wu_buf[slot].astype(BF16)
        wd_b_ref[...] = wd_buf[slot].astype(BF16)

    @pl.when(used)
    def _():
        h = jnp.dot(xs_ref[...], wgu_b_ref[...], preferred_element_type=F32)
        act = (_silu(h[:, 0:D_EXPERT]) * h[:, D_EXPERT:2 * D_EXPERT]).astype(BF16)
        ys_ref[...] = jnp.dot(act, wd_b_ref[...], preferred_element_type=F32).astype(BF16)


def _experts(tile_expert, tile_segment, next_expert, n_used, xs, w_gate, w_up, w_down):
    row_tile = lambda i, te, sg, nx, nu: (jnp.minimum(i, nu[0] - 1), 0)
    n_prefetch = 4
    grid_spec = pltpu.PrefetchScalarGridSpec(
        num_scalar_prefetch=n_prefetch,
        grid=(N_TILES,),
        in_specs=[pl.BlockSpec((MOE_TM, D_MODEL), row_tile),
                  pl.BlockSpec(memory_space=pl.ANY), pl.BlockSpec(memory_space=pl.ANY),
                  pl.BlockSpec(memory_space=pl.ANY)],
        out_specs=pl.BlockSpec((MOE_TM, D_MODEL), row_tile),
        scratch_shapes=[pltpu.VMEM((2, D_MODEL, D_EXPERT), F32), pltpu.VMEM((2, D_MODEL, D_EXPERT), F32),
                        pltpu.VMEM((2, D_EXPERT, D_MODEL), F32),
                        pltpu.VMEM((D_MODEL, 2 * D_EXPERT), BF16), pltpu.VMEM((D_EXPERT, D_MODEL), BF16),
                        pltpu.SemaphoreType.DMA((2,))],
    )
    return pl.pallas_call(
        _experts_kernel,
        grid_spec=grid_spec,
        out_shape=jax.ShapeDtypeStruct((N_ROWS, D_MODEL), BF16),
        input_output_aliases={n_prefetch: 0},
        compiler_params=_cparams(("arbitrary",)),
        name="experts",
    )(tile_expert, tile_segment, next_expert, n_used, xs, w_gate, w_up, w_down)


def _combine_kernel(nb_ref, ns_ref, tab_ref, tab_next_ref, route_ref, x1_ref, g2_ref, ys_ref, o_ref,
                    gbuf_ref, sems):
    i = pl.program_id(0)
    last = pl.num_programs(0) - 1
    buf = lax.rem(i, 2)

    def piece(b):
        def make(local, sorted_row, rows):
            return pltpu.make_async_copy(
                ys_ref.at[pl.ds(pl.multiple_of(sorted_row, RUN_ALIGN), rows)],
                gbuf_ref.at[b, pl.ds(pl.multiple_of(local, RUN_ALIGN), rows)], sems.at[b])
        return make

    @pl.when(i == 0)
    def _():
        gbuf_ref[...] = jnp.zeros_like(gbuf_ref)
        _run_copies(tab_ref, nb_ref[0], ns_ref[0], piece(0), lambda cp: cp.start())

    nxt = jnp.minimum(i + 1, last)

    @pl.when(i < last)
    def _():
        _run_copies(tab_next_ref, nb_ref[nxt], ns_ref[nxt], piece(1 - buf), lambda cp: cp.start())

    rec = route_ref[...]
    slot0 = lax.broadcasted_iota(I32, (OUT_TM, COMBINE_K), 1).astype(F32)
    _run_copies(tab_ref, nb_ref[i], ns_ref[i], lambda lo, so, rows: piece(buf)(0, 0, rows), lambda cp: cp.wait())
    moe = jnp.zeros((OUT_TM, D_MODEL), F32)
    for k0 in range(0, LOCAL_ROWS, COMBINE_K):
        s0, s1 = rec[:, 0:1] - float(k0), rec[:, 1:2] - float(k0)
        weights = (jnp.where(slot0 == s0, rec[:, 2:3], 0.0)
                   + jnp.where(slot0 == s1, rec[:, 3:4], 0.0)).astype(BF16)
        moe = moe + jnp.dot(weights, gbuf_ref[buf, k0:k0 + COMBINE_K, :], preferred_element_type=F32)
    o_ref[...] = x1_ref[...] + g2_ref[0] * moe


def _combine(n_big, n_small, piece_table, route, x1, mod3, ys):
    tm = OUT_TM
    steps_per_batch = SEQ // tm
    grid_spec = pltpu.PrefetchScalarGridSpec(
        num_scalar_prefetch=2,
        grid=(N_TOKEN_TILES,),
        in_specs=[_piece_spec(lambda i, nb, ns: (i, 0, 0)),
                  _piece_spec(lambda i, nb, ns: (jnp.minimum(i + 1, N_TOKEN_TILES - 1), 0, 0)),
                  pl.BlockSpec((tm, ROUTE_W), lambda i, nb, ns: (i, 0)),
                  pl.BlockSpec((tm, D_MODEL), lambda i, nb, ns: (i, 0)),
                  pl.BlockSpec((1, 1, D_MODEL), lambda i, nb, ns: ((i // steps_per_batch) * 6 + 5, 0, 0)),
                  pl.BlockSpec(memory_space=pl.ANY)],
        out_specs=pl.BlockSpec((tm, D_MODEL), lambda i, nb, ns: (i, 0)),
        scratch_shapes=[pltpu.VMEM((2, LOCAL_ROWS, D_MODEL), BF16), pltpu.SemaphoreType.DMA((2,))],
    )
    return pl.pallas_call(
        _combine_kernel,
        grid_spec=grid_spec,
        out_shape=jax.ShapeDtypeStruct((TOKENS, D_MODEL), F32),
        compiler_params=_cparams(("arbitrary",)),
        name="combine",
    )(n_big, n_small, piece_table, piece_table, route, x1, mod3, ys)


def kernel(x, c, positions, norm1_w, norm2_w, w_ada, b_ada, w_in, conv_w, conv_b, dt_bias, a_log,
           d_skip, ssd_norm_w, q_norm_w, k_norm_w, sinks, w_out, w_group, b_group, w_expert, b_expert,
           w_gate, w_up, w_down):
    assert x.shape == (BATCH, SEQ, D_MODEL) and w_in.shape == (D_MODEL, IN_WIDTH)
    x2d = x.reshape(TOKENS, D_MODEL)
    mod = _ada_mod(c, w_ada, b_ada)
    mod3 = mod.reshape(BATCH * 6, 1, D_MODEL)

    qkv, z, xbc, dt, dt_t = _in_proj(x2d, norm1_w, mod3, w_in)
    att = _attention(qkv, positions, q_norm_w, k_norm_w, sinks)
    y = _ssd(xbc, z, dt, dt_t, conv_w, conv_b, dt_bias, a_log, d_skip, ssd_norm_w)

    w_router = jnp.pad(jnp.concatenate([w_group, w_expert], axis=1).astype(F32),
                       ((0, 0), (0, LANES - ROUTER_COLS)))
    b_router = jnp.pad(jnp.concatenate([b_group, b_expert]).astype(F32),
                       (0, LANES - ROUTER_COLS)).reshape(1, LANES)
    x1, h2, route, route_t, tcnt = _out_router(att, y, x2d, mod3, w_out.astype(F32), norm2_w,
                                                w_router, b_router)

    tc = tcnt[:, 0, 0:N_EXPERTS].astype(I32)
    run_rows = ((tc + RUN_ALIGN - 1) // RUN_ALIGN) * RUN_ALIGN
    counts = jnp.sum(run_rows, axis=0)
    padded = ((counts + MOE_TM - 1) // MOE_TM) * MOE_TM
    seg_end = jnp.cumsum(padded)
    seg_start = seg_end - padded
    run_dst = seg_start[None, :] + jnp.cumsum(run_rows, axis=0) - run_rows
    n_used = (seg_end[-1] // MOE_TM).reshape(1)
    last_row = jnp.minimum(jnp.arange(N_TILES, dtype=I32) * MOE_TM, seg_end[-1] - 1)
    tile_expert = jnp.sum((seg_end[None, :] <= last_row[:, None]).astype(I32), axis=1)

    run_local = jnp.cumsum(run_rows, axis=1) - run_rows
    n_big_run = run_rows // BIG_PIECE
    n_small_run = (run_rows // RUN_ALIGN) % 2
    q = jnp.arange(PIECE_SLOTS, dtype=I32)
    experts = jnp.arange(N_EXPERTS, dtype=I32)

    def flat(per_run, local0, dst0, stride):
        end = jnp.cumsum(per_run, axis=1)
        run_of = jnp.sum((end[:, None, :] <= q[None, :, None]).astype(I32), axis=2)
        pick = (run_of[:, :, None] == experts[None, None, :]).astype(I32)
        k = q[None, :] - jnp.sum(pick * (end - per_run)[:, None, :], axis=2)
        local = jnp.sum(pick * local0[:, None, :], axis=2) + stride * k
        dst = jnp.sum(pick * dst0[:, None, :], axis=2) + stride * k
        return end[:, -1], local, dst

    n_big, big_local, big_dst = flat(n_big_run, run_local, run_dst, BIG_PIECE)
    n_small, small_local, small_dst = flat(n_small_run, run_local + BIG_PIECE * n_big_run,
                                           run_dst + BIG_PIECE * n_big_run, 0)
    piece_table = jnp.concatenate([big_local, big_dst, small_local, small_dst], axis=1)
    piece_table = piece_table.astype(I32).reshape(N_TOKEN_TILES, 1, TABLE_W)

    nonempty = padded > 0
    seg_rank = jnp.cumsum(nonempty.astype(I32)) - 1
    later = nonempty[None, :] & (experts[None, :] > experts[:, None])
    next_of = jnp.min(jnp.where(later, experts[None, :], N_EXPERTS), axis=1)
    next_of = jnp.where(next_of == N_EXPERTS, -1, next_of)
    tile_is = (tile_expert[:, None] == experts[None, :]).astype(I32)
    tile_segment = jnp.sum(tile_is * seg_rank[None, :], axis=1)
    next_expert = jnp.sum(tile_is * next_of[None, :], axis=1)

    n_big, n_small = n_big.astype(I32), n_small.astype(I32)
    xs = _dispatch(seg_end.astype(I32), (seg_start + counts).astype(I32), n_big, n_small, piece_table,
                   route_t, h2)
    ys = _experts(tile_expert, tile_segment.astype(I32), next_expert.astype(I32), n_used.astype(I32),
                  xs, w_gate, w_up, w_down)
    out = _combine(n_big, n_small, piece_table, route, x1, mod3, ys)
    return out.reshape(BATCH, SEQ, D_MODEL)
```

```python
import jax
import jax.numpy as jnp
from jax import lax
from jax.experimental import pallas as pl
from jax.experimental.pallas import tpu as pltpu

F32 = jnp.float32
BF16 = jnp.bfloat16
I32 = jnp.int32

D_MODEL = 1024
BATCH = 2
SEQ = 8192
TOKENS = BATCH * SEQ
ATT_HEADS = 8
ATT_KV_HEADS = 2
HEAD_DIM = 64
ATT_WIDTH = ATT_HEADS * HEAD_DIM
KV_WIDTH = ATT_KV_HEADS * HEAD_DIM
ATT_BLOCK = 128
ROPE_DIM = HEAD_DIM // 4
ROPE_THETA = 500000.0
SSD_HEADS = 8
SSD_HEAD_DIM = 64
SSD_WIDTH = SSD_HEADS * SSD_HEAD_DIM
SSD_GROUPS = 2
SSD_STATE = 128
CONV_K = 4
CHUNK = 128
XBC_WIDTH = SSD_WIDTH + 2 * SSD_GROUPS * SSD_STATE
IN_WIDTH = ATT_WIDTH + 2 * KV_WIDTH + SSD_WIDTH + XBC_WIDTH + SSD_HEADS
N_GROUPS = 4
EXPERTS_PER_GROUP = 8
N_EXPERTS = N_GROUPS * EXPERTS_PER_GROUP
TOP_K = 2
D_EXPERT = 256
EPS = 1e-6

LANES = 128
QKV_WIDTH = ATT_WIDTH + 2 * KV_WIDTH
IN_PAD = QKV_WIDTH + SSD_WIDTH + XBC_WIDTH + LANES
NEG_BIG = -1e30

VMEM_LIMIT = 48 * 1024 * 1024


def _cparams(sem):
    return pltpu.CompilerParams(dimension_semantics=sem, vmem_limit_bytes=VMEM_LIMIT)


def _split_bf16(x):
    hi = x.astype(BF16)
    lo = (x - hi.astype(F32)).astype(BF16)
    return hi, lo


ADA_TN = 768


def _ada_kernel(ct_ref, w_ref, b_ref, o_ref):
    ct = ct_ref[...]
    s = ct * jax.nn.sigmoid(ct)
    w = w_ref[...]
    rows = [jnp.sum(s[:, b:b + 1] * w, axis=0, keepdims=True) for b in range(BATCH)]
    o_ref[...] = jnp.concatenate(rows, axis=0) + b_ref[...]


def _ada_mod(c, w_ada, b_ada):
    n = w_ada.shape[1]
    return pl.pallas_call(
        _ada_kernel,
        grid=(n // ADA_TN,),
        in_specs=[pl.BlockSpec((D_MODEL, BATCH), lambda j: (0, 0)),
                  pl.BlockSpec((D_MODEL, ADA_TN), lambda j: (0, j)),
                  pl.BlockSpec((1, ADA_TN), lambda j: (0, j))],
        out_specs=pl.BlockSpec((BATCH, ADA_TN), lambda j: (0, j)),
        out_shape=jax.ShapeDtypeStruct((BATCH, n), F32),
        compiler_params=_cparams(("arbitrary",)),
        name="ada_mod",
    )(c.T, w_ada, b_ada.reshape(1, n))


INPROJ_TM = 512
_INPROJ_CHUNK = 256


def _inproj_kernel(x_ref, nw_ref, sc_ref, sh_ref, wf_ref, wdt_ref, qkv_ref, z_ref, xbc_ref, dt_ref, dtt_ref,
                   w_ref):
    @pl.when(pl.program_id(0) == 0)
    def _():
        for c0 in range(0, IN_PAD - LANES, _INPROJ_CHUNK):
            w_ref[:, c0:c0 + _INPROJ_CHUNK] = wf_ref[:, c0:c0 + _INPROJ_CHUNK].astype(BF16)
        w_ref[:, IN_PAD - LANES:IN_PAD] = wdt_ref[...].astype(BF16)

    x = x_ref[...]
    y = x * lax.rsqrt(jnp.mean(x * x, axis=-1, keepdims=True) + EPS)
    h = (y * nw_ref[...]) * (1.0 + sc_ref[0]) + sh_ref[0]
    hb = h.astype(BF16)

    def proj(c0, c1):
        return jnp.dot(hb, w_ref[:, c0:c1], preferred_element_type=F32)

    for c0 in range(0, QKV_WIDTH, _INPROJ_CHUNK):
        qkv_ref[:, c0:c0 + _INPROJ_CHUNK] = proj(c0, c0 + _INPROJ_CHUNK).astype(BF16)
    base = QKV_WIDTH
    for c0 in range(0, SSD_WIDTH, _INPROJ_CHUNK):
        z_ref[:, c0:c0 + _INPROJ_CHUNK] = proj(base + c0, base + c0 + _INPROJ_CHUNK).astype(BF16)
    base += SSD_WIDTH
    for c0 in range(0, XBC_WIDTH, _INPROJ_CHUNK):
        xbc_ref[:, c0:c0 + _INPROJ_CHUNK] = proj(base + c0, base + c0 + _INPROJ_CHUNK).astype(BF16)
    base += XBC_WIDTH
    dt = proj(base, base + LANES)
    dt_ref[...] = dt
    dtt_ref[...] = dt.T[0:SSD_HEADS, :]


def _in_proj(x2d, norm_w, mod3, w_in):
    tm = INPROJ_TM
    steps_per_batch = SEQ // tm
    w_dt = jnp.pad(w_in[:, IN_WIDTH - SSD_HEADS:IN_WIDTH].astype(F32), ((0, 0), (0, LANES - SSD_HEADS)))
    return pl.pallas_call(
        _inproj_kernel,
        grid=(TOKENS // tm,),
        in_specs=[pl.BlockSpec((tm, D_MODEL), lambda i: (i, 0)),
                  pl.BlockSpec((1, D_MODEL), lambda i: (0, 0)),
                  pl.BlockSpec((1, 1, D_MODEL), lambda i: ((i // steps_per_batch) * 6 + 1, 0, 0)),
                  pl.BlockSpec((1, 1, D_MODEL), lambda i: ((i // steps_per_batch) * 6 + 0, 0, 0)),
                  pl.BlockSpec((D_MODEL, IN_WIDTH), lambda i: (0, 0), pipeline_mode=pl.Buffered(1)),
                  pl.BlockSpec((D_MODEL, LANES), lambda i: (0, 0))],
        out_specs=[pl.BlockSpec((tm, QKV_WIDTH), lambda i: (i, 0)),
                   pl.BlockSpec((tm, SSD_WIDTH), lambda i: (i, 0)),
                   pl.BlockSpec((tm, XBC_WIDTH), lambda i: (i, 0)),
                   pl.BlockSpec((tm, LANES), lambda i: (i, 0)),
                   pl.BlockSpec((SSD_HEADS, tm), lambda i: (0, i))],
        out_shape=[jax.ShapeDtypeStruct((TOKENS, QKV_WIDTH), BF16),
                   jax.ShapeDtypeStruct((TOKENS, SSD_WIDTH), BF16),
                   jax.ShapeDtypeStruct((TOKENS, XBC_WIDTH), BF16),
                   jax.ShapeDtypeStruct((TOKENS, LANES), F32),
                   jax.ShapeDtypeStruct((SSD_HEADS, TOKENS), F32)],
        scratch_shapes=[pltpu.VMEM((D_MODEL, IN_PAD), BF16)],
        compiler_params=_cparams(("arbitrary",)),
        name="in_proj",
    )(x2d, norm_w.reshape(1, D_MODEL), mod3, mod3, w_in.astype(F32), w_dt)


ATT_SUB = 8


ROPE_TM = 2048
_ROPE_HALF = ROPE_DIM // 2
_TOK_PER_ROW = LANES // _ROPE_HALF


def _exact_dot(x, onehot_b):
    hi, lo = _split_bf16(x)
    return (jnp.dot(hi, onehot_b, preferred_element_type=F32)
            + jnp.dot(lo, onehot_b, preferred_element_type=F32))


def _rope_kernel(pos_ref, freq_ref, sel_ref, own_ref, gcos_ref, gs1_ref, gs2_ref, ident_ref,
                 cos_ref, s1_ref, s2_ref):
    ang = pos_ref[...].astype(F32) * freq_ref[...]
    cos_p, sin_p = jnp.cos(ang), jnp.sin(ang)
    hi_c, lo_c = _split_bf16(cos_p)
    hi_s, lo_s = _split_bf16(sin_p)
    sel = sel_ref[...]
    rows_c = jnp.dot(sel, hi_c, preferred_element_type=F32) + jnp.dot(sel, lo_c, preferred_element_type=F32)
    rows_s = jnp.dot(sel, hi_s, preferred_element_type=F32) + jnp.dot(sel, lo_s, preferred_element_type=F32)
    own = own_ref[...]
    cos_ref[...] = _exact_dot(rows_c * own, gcos_ref[...]) + ident_ref[...]
    s1_ref[...] = _exact_dot(rows_s * own, gs1_ref[...])
    s2_ref[...] = _exact_dot(rows_s * own, gs2_ref[...])


def _rope_tables(positions):
    half, per_row = _ROPE_HALF, _TOK_PER_ROW
    rows = ROPE_TM // per_row
    pos_rep = jnp.repeat(positions.reshape(TOKENS).astype(I32), half).reshape(TOKENS // per_row, LANES)
    inv_freq = jnp.power(ROPE_THETA, -jnp.arange(half, dtype=F32) * 2.0 / ROPE_DIM)
    freq = jnp.tile(inv_freq, per_row).reshape(1, LANES)
    tok = jnp.arange(ROPE_TM)
    lane = jnp.arange(LANES)
    sel = (tok[:, None] // per_row == jnp.arange(rows)[None, :]).astype(BF16)
    own = (lane[None, :] // half == tok[:, None] % per_row).astype(F32)
    d = lane % HEAD_DIM
    src_f = lane % half
    hits = lambda lo, hi: ((src_f[:, None] == d[None, :] % half) & (d[None, :] >= lo) & (d[None, :] < hi))
    gcos = hits(0, ROPE_DIM).astype(BF16)
    gs1 = -hits(0, half).astype(BF16)
    gs2 = hits(half, ROPE_DIM).astype(BF16)
    ident = (d >= ROPE_DIM).astype(F32).reshape(1, LANES)
    const = lambda shape: pl.BlockSpec(shape, lambda i: (0, 0))
    out_spec = pl.BlockSpec((ROPE_TM, LANES), lambda i: (i, 0))
    out = jax.ShapeDtypeStruct((TOKENS, LANES), F32)
    return pl.pallas_call(
        _rope_kernel,
        grid=(TOKENS // ROPE_TM,),
        in_specs=[pl.BlockSpec((rows, LANES), lambda i: (i, 0)), const((1, LANES)),
                  const((ROPE_TM, rows)), const((ROPE_TM, LANES)),
                  const((LANES, LANES)), const((LANES, LANES)), const((LANES, LANES)), const((1, LANES))],
        out_specs=[out_spec, out_spec, out_spec],
        out_shape=[out, out, out],
        compiler_params=_cparams(("arbitrary",)),
        name="rope_tables",
    )(pos_rep, freq, sel, own, gcos, gs1, gs2, ident)


def _seg_meansq(xf, ones128):
    rows, width = xf.shape
    nt = width // LANES
    parts = _split_bf16(xf * xf)
    stacked = jnp.concatenate([p[:, t * LANES:(t + 1) * LANES] for p in parts for t in range(nt)], axis=0)
    tot = jnp.dot(stacked, ones128, preferred_element_type=F32)
    tiles = [tot[t * rows:(t + 1) * rows] + tot[(nt + t) * rows:(nt + t + 1) * rows] for t in range(nt)]
    return jnp.concatenate(tiles, axis=1) * (1.0 / HEAD_DIM)


def _norm_rope(x_bf, w_row, ones_bd, cosf, s1, s2):
    xf = x_bf.astype(F32)
    width = xf.shape[1]
    xn = xf * lax.rsqrt(_seg_meansq(xf, ones_bd) + EPS) * w_row
    half = ROPE_DIM // 2
    up = pltpu.roll(xn, width - half, axis=1)
    down = pltpu.roll(xn, half, axis=1)
    return xn * cosf + up * s1 + down * s2


def _attn_kernel(sink_ref, q_ref, kv_ref, cos_ref, s1_ref, s2_ref, qw_ref, kw_ref,
                 ones_ref, o_ref, kprev_ref, vprev_ref):
    j = pl.program_id(1)
    blk = ATT_BLOCK

    @pl.when(j == 0)
    def _():
        kprev_ref[...] = jnp.zeros_like(kprev_ref)
        vprev_ref[...] = jnp.zeros_like(vprev_ref)

    cos1 = cos_ref[...]
    s1_1 = s1_ref[...]
    s2_1 = s2_ref[...]
    reps = ATT_WIDTH // LANES
    cosq = jnp.concatenate([cos1] * reps, axis=1)
    s1q = jnp.concatenate([s1_1] * reps, axis=1)
    s2q = jnp.concatenate([s2_1] * reps, axis=1)

    q = _norm_rope(q_ref[...], qw_ref[...], ones_ref[...], cosq, s1q, s2q)
    qf = q * (HEAD_DIM ** -0.5)
    kv = kv_ref[...]
    kn = _norm_rope(kv[:, 0:KV_WIDTH], kw_ref[...], ones_ref[...], cos1, s1_1, s2_1)
    vn = kv[:, KV_WIDTH:2 * KV_WIDTH].astype(F32)

    kall = jnp.concatenate([kprev_ref[...], kn], axis=0)
    vall = jnp.concatenate([vprev_ref[...], vn], axis=0)
    kprev_ref[...] = kn[(ATT_SUB - 1) * blk:ATT_SUB * blk]
    vprev_ref[...] = vn[(ATT_SUB - 1) * blk:ATT_SUB * blk]

    lo_all = lax.broadcasted_iota(I32, kall.shape, 1) < HEAD_DIM
    ones_all = jnp.ones(kall.shape, BF16)

    row = lax.broadcasted_iota(I32, (2 * blk, blk), 0)
    col = lax.broadcasted_iota(I32, (2 * blk, blk), 1)
    from_prev = col > (row & (blk - 1))
    second_tile = lax.broadcasted_iota(I32, (2 * blk, 1), 0) >= blk
    zero_p = jnp.zeros((2 * blk, blk), F32)

    k_par, v_par = [], []
    for g in range(ATT_KV_HEADS):
        keep = lo_all if g == 0 else ~lo_all
        k_own = jnp.where(keep, kall, 0.0)
        v_own = jnp.where(keep, vall, 0.0)
        k_oth = pltpu.roll(k_own, HEAD_DIM, axis=1)
        v_oth = pltpu.roll(v_own, HEAD_DIM, axis=1)
        k_lo, k_hi = (k_own, k_oth) if g == 0 else (k_oth, k_own)
        v_lo, v_hi = (v_own, v_oth) if g == 0 else (v_oth, v_own)
        k_par.append((k_lo.astype(BF16), k_hi.astype(BF16)))
        v_par.append((jnp.concatenate([v_lo.astype(BF16), ones_all], axis=1),
                      jnp.concatenate([v_hi.astype(BF16), ones_all], axis=1)))

    problems = [(g, sub) for g in range(ATT_KV_HEADS) for sub in range(ATT_SUB)]
    scores = []
    for g, sub in problems:
        r0, c0 = sub * blk, g * 2 * LANES
        qcat = jnp.concatenate([qf[r0:r0 + blk, c0:c0 + LANES],
                                qf[r0:r0 + blk, c0 + LANES:c0 + 2 * LANES]], axis=0).astype(BF16)
        kw = jnp.concatenate([k_par[g][0][r0:r0 + 2 * blk], k_par[g][1][r0:r0 + 2 * blk]], axis=0)
        scores.append(lax.dot_general(qcat, kw, (((1,), (1,)), ((), ())),
                                      preferred_element_type=F32))

    weights, rescale = [], []
    for (g, sub), s_all in zip(problems, scores):
        for par in range(2):
            s = s_all[:, par * 2 * blk:(par + 1) * 2 * blk]
            s_prev = s[:, 0:blk]
            if sub == 0:
                s_prev = s_prev + jnp.where(j > 0, 0.0, NEG_BIG)
            s = jnp.where(from_prev, s_prev, s[:, blk:2 * blk])
            h_first = ATT_HEADS // ATT_KV_HEADS * g + par
            sink = jnp.where(second_tile, sink_ref[h_first + 2], sink_ref[h_first])
            m = jnp.maximum(jnp.max(s, axis=-1, keepdims=True), sink)
            p = jnp.exp(s - m)
            weights.append(jnp.concatenate([jnp.where(from_prev, p, zero_p), jnp.where(from_prev, zero_p, p)],
                                           axis=1).astype(BF16))
            rescale.append(jnp.exp(sink - m))

    outs = []
    for idx, (g, sub) in enumerate(problems):
        for par in range(2):
            outs.append(jnp.dot(weights[2 * idx + par], v_par[g][par][sub * blk:(sub + 2) * blk],
                                preferred_element_type=F32))

    for idx, (g, sub) in enumerate(problems):
        r0, c0 = sub * blk, g * 2 * LANES
        pair = None
        for par in range(2):
            o = outs[2 * idx + par]
            part = o[:, 0:LANES] * (1.0 / (o[:, LANES:2 * LANES] + rescale[2 * idx + par]))
            pair = part if pair is None else pair + part
        o_ref[r0:r0 + blk, c0:c0 + LANES] = pair[0:blk].astype(BF16)
        o_ref[r0:r0 + blk, c0 + LANES:c0 + 2 * LANES] = pair[blk:2 * blk].astype(BF16)


def _attention(qkv, positions, q_norm_w, k_norm_w, sinks):
    cosf, s1, s2 = _rope_tables(positions)
    qw = jnp.tile(q_norm_w.astype(F32), ATT_HEADS).reshape(1, ATT_WIDTH)
    kw = jnp.tile(k_norm_w.astype(F32), ATT_KV_HEADS).reshape(1, KV_WIDTH)
    seg = jnp.arange(LANES) // HEAD_DIM
    ones128 = (seg[:, None] == seg[None, :]).astype(BF16)
    const = lambda shape: pl.BlockSpec(shape, lambda b, j, s: (0, 0))
    rows = ATT_SUB * ATT_BLOCK
    nb = SEQ // rows
    tok = lambda width, cb: pl.BlockSpec((rows, width), lambda b, j, s: (b * nb + j, cb))
    grid_spec = pltpu.PrefetchScalarGridSpec(
        num_scalar_prefetch=1,
        grid=(BATCH, nb),
        in_specs=[tok(ATT_WIDTH, 0), tok(2 * KV_WIDTH, 2), tok(LANES, 0), tok(LANES, 0), tok(LANES, 0),
                  const((1, ATT_WIDTH)), const((1, KV_WIDTH)), const((LANES, LANES))],
        out_specs=tok(ATT_WIDTH, 0),
        scratch_shapes=[pltpu.VMEM((ATT_BLOCK, KV_WIDTH), F32),
                        pltpu.VMEM((ATT_BLOCK, KV_WIDTH), F32)],
    )
    return pl.pallas_call(
        _attn_kernel,
        grid_spec=grid_spec,
        out_shape=jax.ShapeDtypeStruct((TOKENS, ATT_WIDTH), BF16),
        compiler_params=_cparams(("arbitrary", "arbitrary")),
        name="attention",
    )(sinks.astype(F32), qkv, qkv, cosf, s1, s2, qw, kw, ones128)


SSD_SUB = 4


def _softplus(x):
    return jnp.maximum(x, 0.0) + jnp.log1p(jnp.exp(-jnp.abs(x)))


def _silu(x):
    h = 0.5 * x
    return h + h * jnp.tanh(h)


def _ssd_kernel(xbc_ref, z_ref, dt_ref, dtt_ref, cw_ref, cb_ref, dtb_row_ref, dtb_col_ref,
                alog_row_ref, alog_col_ref, dskip_ref, nw_ref, tril_ref, triu_ref,
                o_ref, conv_ref, state_ref):
    c = pl.program_id(1)
    L = CHUNK
    tail = 8

    @pl.when(c == 0)
    def _():
        conv_ref[0:tail, :] = jnp.zeros((tail, XBC_WIDTH), F32)
        state_ref[...] = jnp.zeros_like(state_ref)

    row = lax.broadcasted_iota(I32, (L, L), 0)
    col = lax.broadcasted_iota(I32, (L, L), 1)
    causal = col <= row
    lane = lax.broadcasted_iota(I32, (L, LANES), 1)
    lo_half = lane < SSD_HEAD_DIM

    prepared = [_ssd_prepare(s * L, xbc_ref, dt_ref, dtt_ref, cw_ref, cb_ref, dtb_row_ref, dtb_col_ref,
                             alog_row_ref, alog_col_ref, tril_ref, triu_ref, conv_ref)
                for s in range(SSD_SUB)]
    for s in range(SSD_SUB):
        _ssd_chunk(s * L, prepared[s], causal, lo_half, z_ref, dskip_ref, nw_ref, o_ref, state_ref)


def _ssd_prepare(r0, xbc_ref, dt_ref, dtt_ref, cw_ref, cb_ref, dtb_row_ref, dtb_col_ref,
                 alog_row_ref, alog_col_ref, tril_ref, triu_ref, conv_ref):
    L = CHUNK
    tail = 8
    xb = xbc_ref[r0:r0 + L, :].astype(F32)
    conv_ref[tail:tail + L, :] = xb
    acc = cb_ref[...] + cw_ref[CONV_K - 1:CONV_K, :] * xb
    for k in range(CONV_K - 1):
        off = tail - (CONV_K - 1) + k
        acc = acc + cw_ref[k:k + 1, :] * conv_ref[off:off + L, :]
    conv_ref[0:tail, :] = xb[L - tail:L, :]
    u = _silu(acc)
    xs = u[:, 0:SSD_WIDTH]
    bmat = u[:, SSD_WIDTH:SSD_WIDTH + SSD_GROUPS * SSD_STATE]
    cmat = u[:, SSD_WIDTH + SSD_GROUPS * SSD_STATE:XBC_WIDTH]

    dt = _softplus(dt_ref[r0:r0 + L, :] + dtb_row_ref[...])
    a = dt * (-jnp.exp(alog_row_ref[...]))
    a_hi, a_lo = _split_bf16(a)
    a_cum = (jnp.dot(tril_ref[...], a_hi, preferred_element_type=F32)
             + jnp.dot(tril_ref[...], a_lo, preferred_element_type=F32))
    dt_t = _softplus(dtt_ref[:, r0:r0 + L] + dtb_col_ref[...])
    a_t = dt_t * (-jnp.exp(alog_col_ref[...]))
    at_hi, at_lo = _split_bf16(a_t)
    a_cum_t = (jnp.dot(at_hi, triu_ref[...], preferred_element_type=F32)
               + jnp.dot(at_lo, triu_ref[...], preferred_element_type=F32))
    a_end_t = a_cum_t[:, L - 1:L]
    return dict(
        xs=xs, bmat=bmat, cmat=cmat, a_cum=a_cum, exp_a_cum=jnp.exp(a_cum),
        shifted_t=a_cum_t - jnp.log(dt_t),
        wst_t=jnp.exp(a_end_t - a_cum_t) * dt_t,
        cdec_t=jnp.exp(a_end_t))


def _ssd_chunk(r0, p, causal, lo_half, z_ref, dskip_ref, nw_ref, o_ref, state_ref):
    L = CHUNK
    xs, bmat, cmat, a_cum, exp_a_cum = p["xs"], p["bmat"], p["cmat"], p["a_cum"], p["exp_a_cum"]
    shifted_t, wst_t, cdec_t = p["shifted_t"], p["wst_t"], p["cdec_t"]
    xs_b = xs.astype(BF16)
    heads_per_group = SSD_HEADS // SSD_GROUPS
    gated = []
    for g in range(SSD_GROUPS):
        b_g = bmat[:, g * SSD_STATE:(g + 1) * SSD_STATE]
        c_g = cmat[:, g * SSD_STATE:(g + 1) * SSD_STATE]
        cb = lax.dot_general(c_g.astype(BF16), b_g.astype(BF16), (((1,), (1,)), ((), ())),
                             preferred_element_type=F32)
        b_gt = b_g.T
        for t in range(heads_per_group // 2):
            tile = g * (heads_per_group // 2) + t
            c0 = tile * LANES
            xs_tile = xs_b[:, c0:c0 + LANES]
            st_tile = state_ref[:, c0:c0 + LANES]
            st_b = st_tile.astype(BF16)
            y_tile = jnp.zeros((L, LANES), F32)
            new_tile = jnp.zeros((SSD_STATE, LANES), F32)
            for e in range(2):
                h = 2 * tile + e
                keep = lo_half if e == 0 else ~lo_half
                colb = jnp.broadcast_to(a_cum[:, h:h + 1], (L, L))
                rowb = shifted_t[h:h + 1, :]
                w_in = cb * jnp.exp(jnp.where(causal, colb - rowb, NEG_BIG))
                w_off = c_g * jnp.broadcast_to(exp_a_cum[:, h:h + 1], (L, L))
                lhs = jnp.concatenate([w_in, w_off], axis=1).astype(BF16)
                rhs = jnp.concatenate([jnp.where(keep, xs_tile, jnp.zeros_like(xs_tile)),
                                       jnp.where(keep, st_b, jnp.zeros_like(st_b))], axis=0)
                y_tile = y_tile + jnp.dot(lhs, rhs, preferred_element_type=F32)
                m_h = (b_gt * wst_t[h:h + 1, :]).astype(BF16)
                new_tile = new_tile + jnp.dot(m_h, jnp.where(keep, xs_tile, jnp.zeros_like(xs_tile)),
                                              preferred_element_type=F32)
            cd = jnp.where(lo_half[0:1, :], cdec_t[2 * tile:2 * tile + 1, :],
                           cdec_t[2 * tile + 1:2 * tile + 2, :])
            state_ref[:, c0:c0 + LANES] = st_tile * cd + new_tile
            y_full = y_tile + dskip_ref[:, c0:c0 + LANES] * xs[:, c0:c0 + LANES]
            gated.append(y_full * _silu(z_ref[r0:r0 + L, c0:c0 + LANES].astype(F32)))

    gw = SSD_WIDTH // SSD_GROUPS
    tiles_per_group = gw // LANES
    for g in range(SSD_GROUPS):
        yg = jnp.concatenate(gated[g * tiles_per_group:(g + 1) * tiles_per_group], axis=1)
        ms = jnp.mean(yg * yg, axis=-1, keepdims=True)
        o_ref[r0:r0 + L, g * gw:(g + 1) * gw] = (
            (yg * lax.rsqrt(ms + EPS)) * nw_ref[:, g * gw:(g + 1) * gw]).astype(o_ref.dtype)


def _ssd(xbc, z, dt, dt_t, conv_w, conv_b, dt_bias, a_log, d_skip, ssd_norm_w):
    L = SSD_SUB * CHUNK
    nc = SEQ // L
    pad_row = lambda v: jnp.pad(v.astype(F32), (0, LANES - SSD_HEADS)).reshape(1, LANES)
    col8 = lambda v: v.astype(F32).reshape(SSD_HEADS, 1)
    idx = jnp.arange(CHUNK)
    tril = (idx[None, :] <= idx[:, None]).astype(BF16)
    triu = (idx[:, None] <= idx[None, :]).astype(BF16)
    dskip = jnp.repeat(d_skip.astype(F32), SSD_HEAD_DIM).reshape(1, SSD_WIDTH)
    const = lambda shape: pl.BlockSpec(shape, lambda b, c: (0, 0))
    tok = lambda width: pl.BlockSpec((L, width), lambda b, c: (b * nc + c, 0))
    return pl.pallas_call(
        _ssd_kernel,
        grid=(BATCH, nc),
        in_specs=[tok(XBC_WIDTH), tok(SSD_WIDTH), tok(LANES),
                  pl.BlockSpec((SSD_HEADS, L), lambda b, c: (0, b * nc + c)),
                  const((CONV_K, XBC_WIDTH)), const((1, XBC_WIDTH)),
                  const((1, LANES)), const((SSD_HEADS, 1)), const((1, LANES)), const((SSD_HEADS, 1)),
                  const((1, SSD_WIDTH)), const((1, SSD_WIDTH)), const((CHUNK, CHUNK)), const((CHUNK, CHUNK))],
        out_specs=tok(SSD_WIDTH),
        out_shape=jax.ShapeDtypeStruct((TOKENS, SSD_WIDTH), BF16),
        scratch_shapes=[pltpu.VMEM((8 + CHUNK, XBC_WIDTH), F32),
                        pltpu.VMEM((SSD_STATE, SSD_WIDTH), F32)],
        compiler_params=_cparams(("arbitrary", "arbitrary")),
        name="ssd",
    )(xbc, z, dt, dt_t, conv_w.astype(F32), conv_b.astype(F32).reshape(1, XBC_WIDTH),
      pad_row(dt_bias), col8(dt_bias), pad_row(a_log), col8(a_log), dskip,
      ssd_norm_w.astype(F32).reshape(1, SSD_WIDTH), tril, triu)


OUT_TM = 512
ROUTE_W = 8
ROUTER_COLS = N_GROUPS + N_EXPERTS
RUN_ALIGN = 16
RUN_SHIFT = 4
LOCAL_ROWS = 1536
assert RUN_ALIGN == 1 << RUN_SHIFT and LOCAL_ROWS >= TOP_K * OUT_TM + N_EXPERTS * (RUN_ALIGN - 1)


def _lane_pick(values, lane, index):
    return jnp.sum(jnp.where(lane == index, values, 0.0), axis=-1, keepdims=True)


def _first_argmax(vals, lane):
    m = jnp.max(vals, axis=-1, keepdims=True)
    idx = jnp.min(jnp.where(vals == m, lane, float(LANES)), axis=-1, keepdims=True)
    return m, idx


def _out_router_kernel(att_ref, y_ref, x_ref, g1_ref, wof_ref, nw_ref, sc_ref, sh_ref, wr_ref, br_ref,
                       ltri_ref, sut_ref, x1_ref, h2_ref, route_ref, routet_ref, tcnt_ref,
                       wr_split_ref, logits_ref, wo_ref):
    i = pl.program_id(0)

    @pl.when(i == 0)
    def _():
        hi, lo = _split_bf16(wr_ref[...])
        wr_split_ref[:, 0:LANES] = hi
        wr_split_ref[:, LANES:2 * LANES] = lo
        logits_ref[...] = jnp.zeros_like(logits_ref)
        for r0 in range(0, D_MODEL, 256):
            wo_ref[r0:r0 + 256, :] = wof_ref[r0:r0 + 256, :].astype(BF16)

    logits = logits_ref[...]

    mixer = (jnp.dot(att_ref[...], wo_ref[0:ATT_WIDTH, :], preferred_element_type=F32)
             + jnp.dot(y_ref[...], wo_ref[ATT_WIDTH:ATT_WIDTH + SSD_WIDTH, :], preferred_element_type=F32))
    x1 = x_ref[...] + g1_ref[0] * mixer
    x1_ref[...] = x1
    yn = x1 * lax.rsqrt(jnp.mean(x1 * x1, axis=-1, keepdims=True) + EPS)
    h2 = (yn * nw_ref[...]) * (1.0 + sc_ref[0]) + sh_ref[0]
    h2_ref[...] = h2.astype(BF16)

    h_hi, h_lo = _split_bf16(h2)
    both = jnp.dot(h_hi, wr_split_ref[...], preferred_element_type=F32)
    logits_ref[...] = (both[:, 0:LANES] + both[:, LANES:2 * LANES]
                       + jnp.dot(h_lo, wr_split_ref[:, 0:LANES], preferred_element_type=F32)) + br_ref[...]

    tm = logits.shape[0]
    lane = lax.broadcasted_iota(I32, (tm, LANES), 1).astype(F32)

    gl = jnp.where(lane < N_GROUPS, logits, NEG_BIG)
    gmax, gidx = _first_argmax(gl, lane)
    g_p = 1.0 / jnp.sum(jnp.exp(gl - gmax), axis=-1, keepdims=True)

    lo_lane = N_GROUPS + EXPERTS_PER_GROUP * gidx
    el = jnp.where((lane >= lo_lane) & (lane < lo_lane + EXPERTS_PER_GROUP), logits, NEG_BIG)
    m1, i1 = _first_argmax(el, lane)
    m2, i2 = _first_argmax(jnp.where(lane == i1, NEG_BIG, el), lane)
    r = jnp.exp(m2 - m1)
    p1 = 1.0 / (1.0 + r)
    p2 = r / (1.0 + r)
    e0 = i1 - N_GROUPS
    e1 = i2 - N_GROUPS

    onehot = ((lane == e0) | (lane == e1)).astype(F32)
    tile_cnt = jnp.sum(onehot, axis=0, keepdims=True)
    run_len = jnp.floor((tile_cnt + (RUN_ALIGN - 1)) * (1.0 / RUN_ALIGN)) * RUN_ALIGN
    run_start = jnp.dot(jnp.broadcast_to(run_len, (8, LANES)).astype(BF16), sut_ref[...],
                        preferred_element_type=F32)[0:1, :]
    before = jnp.dot(ltri_ref[...], onehot.astype(BF16), preferred_element_type=F32) + run_start
    slot0 = _lane_pick(before, lane, e0)
    slot1 = _lane_pick(before, lane, e1)
    tcnt_ref[0] = tile_cnt

    rec = jnp.zeros((tm, LANES), F32)
    for k, v in enumerate([slot0, slot1, g_p * p1, g_p * p2, e0, e1]):
        rec = jnp.where(lane == k, v, rec)
    route_ref[...] = rec[:, 0:ROUTE_W]
    routet_ref[...] = rec.T[0:ROUTE_W, :]


def _out_router(att, y, x2d, mod3, w_out_b, norm_w, w_router, b_router):
    tm = OUT_TM
    n_steps = TOKENS // tm
    steps_per_batch = SEQ // tm
    idx = jnp.arange(tm)
    ltri = (idx[None, :] < idx[:, None]).astype(BF16)
    lidx = jnp.arange(LANES)
    sut = (lidx[:, None] < lidx[None, :]).astype(BF16)
    const = lambda shape: pl.BlockSpec(shape, lambda i: (0, 0))
    cur = lambda i: jnp.minimum(i, n_steps - 1)
    prev = lambda i: jnp.maximum(i - 1, 0)
    tok = lambda width: pl.BlockSpec((tm, width), lambda i: (cur(i), 0))
    modspec = lambda k: pl.BlockSpec((1, 1, D_MODEL), lambda i: ((cur(i) // steps_per_batch) * 6 + k, 0, 0))
    return pl.pallas_call(
        _out_router_kernel,
        grid=(n_steps + 1,),
        in_specs=[tok(ATT_WIDTH), tok(SSD_WIDTH), tok(D_MODEL), modspec(2),
                  const((D_MODEL, D_MODEL)), const((1, D_MODEL)), modspec(4), modspec(3),
                  const((D_MODEL, LANES)), const((1, LANES)), const((tm, tm)), const((LANES, LANES))],
        out_specs=[tok(D_MODEL), tok(D_MODEL),
                   pl.BlockSpec((tm, ROUTE_W), lambda i: (prev(i), 0)),
                   pl.BlockSpec((ROUTE_W, tm), lambda i: (prev(i), 0)),
                   pl.BlockSpec((1, 1, LANES), lambda i: (prev(i), 0, 0))],
        out_shape=[jax.ShapeDtypeStruct((TOKENS, D_MODEL), F32),
                   jax.ShapeDtypeStruct((TOKENS, D_MODEL), BF16),
                   jax.ShapeDtypeStruct((TOKENS, ROUTE_W), F32),
                   jax.ShapeDtypeStruct((n_steps * ROUTE_W, tm), F32),
                   jax.ShapeDtypeStruct((n_steps, 1, LANES), F32)],
        scratch_shapes=[pltpu.VMEM((D_MODEL, 2 * LANES), BF16), pltpu.VMEM((tm, LANES), F32),
                        pltpu.VMEM((D_MODEL, D_MODEL), BF16)],
        compiler_params=_cparams(("arbitrary",)),
        name="out_router",
    )(att, y, x2d, mod3, w_out_b, norm_w.reshape(1, D_MODEL), mod3, mod3, w_router, b_router, ltri, sut)


MOE_TM = 512
ZERO_ROWS = 256
N_TOKEN_TILES = TOKENS // OUT_TM
MAX_SORTED_ROWS = TOKENS * TOP_K + N_TOKEN_TILES * N_EXPERTS * (RUN_ALIGN - 1)
N_TILES = MAX_SORTED_ROWS // MOE_TM + N_EXPERTS
N_ROWS = N_TILES * MOE_TM
assert MOE_TM % ZERO_ROWS == 0


BIG_PIECE = 2 * RUN_ALIGN
PIECE_SLOTS = LOCAL_ROWS // BIG_PIECE
TABLE_W = 4 * PIECE_SLOTS
COMBINE_K = 256
assert PIECE_SLOTS >= N_EXPERTS and LOCAL_ROWS % COMBINE_K == 0


def _run_copies(table_ref, n_big, n_small, make_copy, action):
    def big(q, carry):
        action(make_copy(table_ref[0, 0, q], table_ref[0, 0, PIECE_SLOTS + q], BIG_PIECE))
        return carry

    def small(q, carry):
        action(make_copy(table_ref[0, 0, 2 * PIECE_SLOTS + q], table_ref[0, 0, 3 * PIECE_SLOTS + q], RUN_ALIGN))
        return carry

    lax.fori_loop(0, n_big, big, 0)
    lax.fori_loop(0, n_small, small, 0)


def _dispatch_kernel(seg_end_ref, used_end_ref, nb_ref, ns_ref, tab_ref, routet_ref, h2_ref, xs_ref,
                     sbuf_ref, zero_ref, sems, zsem):
    i = pl.program_id(0)
    last = pl.num_programs(0) - 1
    buf = lax.rem(i, 2)

    def zero_fills(action):
        def tail_copy(row):
            return pltpu.make_async_copy(zero_ref.at[pl.ds(0, RUN_ALIGN)],
                                         xs_ref.at[pl.ds(pl.multiple_of(row, RUN_ALIGN), RUN_ALIGN)], zsem)

        def block_copy(block):
            start = pl.multiple_of(block * ZERO_ROWS, ZERO_ROWS)
            return pltpu.make_async_copy(zero_ref, xs_ref.at[pl.ds(start, ZERO_ROWS)], zsem)

        def tails(e, carry):
            def body(r, c):
                action(tail_copy(r * RUN_ALIGN))
                return c

            lax.fori_loop(used_end_ref[e] // RUN_ALIGN, seg_end_ref[e] // RUN_ALIGN, body, 0)
            return carry

        def blocks(block, carry):
            action(block_copy(block))
            return carry

        lax.fori_loop(0, N_EXPERTS, tails, 0)
        lax.fori_loop(seg_end_ref[N_EXPERTS - 1] // ZERO_ROWS, N_ROWS // ZERO_ROWS, blocks, 0)

    @pl.when(i == 0)
    def _():
        zero_ref[...] = jnp.zeros_like(zero_ref)
        zero_fills(lambda cp: cp.start())

    slot = lax.broadcasted_iota(I32, (LOCAL_ROWS, OUT_TM), 0).astype(F32)
    perm = jnp.where((slot == routet_ref[0:1, :]) | (slot == routet_ref[1:2, :]), 1.0, 0.0).astype(BF16)
    sbuf_ref[buf] = jnp.dot(perm, h2_ref[...], preferred_element_type=F32).astype(BF16)

    def piece(b):
        def make(local, sorted_row, rows):
            return pltpu.make_async_copy(
                sbuf_ref.at[b, pl.ds(pl.multiple_of(local, RUN_ALIGN), rows)],
                xs_ref.at[pl.ds(pl.multiple_of(sorted_row, RUN_ALIGN), rows)], sems.at[b])
        return make

    _run_copies(tab_ref, nb_ref[i], ns_ref[i], piece(buf), lambda cp: cp.start())
    prev = jnp.maximum(i - 1, 0)

    @pl.when(i > 0)
    def _():
        _run_copies(tab_ref, nb_ref[prev], ns_ref[prev], lambda lo, so, rows: piece(1 - buf)(0, 0, rows),
                    lambda cp: cp.wait())

    @pl.when(i == last)
    def _():
        _run_copies(tab_ref, nb_ref[i], ns_ref[i], lambda lo, so, rows: piece(buf)(0, 0, rows),
                    lambda cp: cp.wait())
        zero_fills(lambda cp: cp.wait())


def _piece_spec(index_map):
    return pl.BlockSpec((1, 1, TABLE_W), index_map, memory_space=pltpu.SMEM)


def _dispatch(seg_end, used_end, n_big, n_small, piece_table, route_t, h2):
    grid_spec = pltpu.PrefetchScalarGridSpec(
        num_scalar_prefetch=4,
        grid=(N_TOKEN_TILES,),
        in_specs=[_piece_spec(lambda i, se, ue, nb, ns: (i, 0, 0)),
                  pl.BlockSpec((ROUTE_W, OUT_TM), lambda i, se, ue, nb, ns: (i, 0)),
                  pl.BlockSpec((OUT_TM, D_MODEL), lambda i, se, ue, nb, ns: (i, 0))],
        out_specs=pl.BlockSpec(memory_space=pl.ANY),
        scratch_shapes=[pltpu.VMEM((2, LOCAL_ROWS, D_MODEL), BF16),
                        pltpu.VMEM((ZERO_ROWS, D_MODEL), BF16),
                        pltpu.SemaphoreType.DMA((2,)), pltpu.SemaphoreType.DMA],
    )
    return pl.pallas_call(
        _dispatch_kernel,
        grid_spec=grid_spec,
        out_shape=jax.ShapeDtypeStruct((N_ROWS, D_MODEL), BF16),
        compiler_params=_cparams(("arbitrary",)),
        name="dispatch",
    )(seg_end, used_end, n_big, n_small, piece_table, route_t, h2)


X_BUFS = 3


def _experts_kernel(te_ref, seg_ref, nxt_ref, nu_ref, xs_hbm, wg_hbm, wu_hbm, wd_hbm, ys_hbm,
                    xbuf, ybuf, wg_buf, wu_buf, wd_buf, wgu_b_ref, wd_b_ref, xsem, ysem, wsem):
    n = nu_ref[0]

    def rows(t):
        return pl.ds(pl.multiple_of(t * MOE_TM, MOE_TM), MOE_TM)

    def x_copy(t, s):
        return pltpu.make_async_copy(xs_hbm.at[rows(t)], xbuf.at[s], xsem.at[s])

    def y_copy(t, s):
        return pltpu.make_async_copy(ybuf.at[s], ys_hbm.at[rows(t)], ysem.at[s])

    def weight_copies(expert, s):
        return [pltpu.make_async_copy(wg_hbm.at[expert], wg_buf.at[s], wsem.at[s]),
                pltpu.make_async_copy(wu_hbm.at[expert], wu_buf.at[s], wsem.at[s]),
                pltpu.make_async_copy(wd_hbm.at[expert], wd_buf.at[s], wsem.at[s])]

    for cp in weight_copies(te_ref[0], 0):
        cp.start()
    for t in range(X_BUFS - 1):
        @pl.when(t < n)
        def _():
            x_copy(t, t).start()

    def tile(i, carry):
        xs_slot = lax.rem(i, X_BUFS)
        ys_slot = lax.rem(i, 2)
        w_slot = lax.rem(seg_ref[i], 2)
        x_copy(i, xs_slot).wait()
        ahead = i + (X_BUFS - 1)

        @pl.when(ahead < n)
        def _():
            x_copy(ahead, lax.rem(ahead, X_BUFS)).start()

        @pl.when((i == 0) | (te_ref[i] != te_ref[jnp.maximum(i - 1, 0)]))
        def _():
            for cp in weight_copies(te_ref[i], w_slot):
                cp.wait()

            @pl.when(nxt_ref[i] >= 0)
            def _():
                for cp in weight_copies(nxt_ref[i], 1 - w_slot):
                    cp.start()

            wgu_b_ref[:, 0:D_EXPERT] = wg_buf[w_slot].astype(BF16)
            wgu_b_ref[:, D_EXPERT:2 * D_EXPERT] = wu_buf[w_slot].astype(BF16)
            wd_b_ref[...] = wd_buf[w_slot].astype(BF16)

        @pl.when(i >= 2)
        def _():
            y_copy(i - 2, ys_slot).wait()

        h = jnp.dot(xbuf[xs_slot], wgu_b_ref[...], preferred_element_type=F32)
        act = (_silu(h[:, 0:D_EXPERT]) * h[:, D_EXPERT:2 * D_EXPERT]).astype(BF16)
        ybuf[ys_slot] = jnp.dot(act, wd_b_ref[...], preferred_element_type=F32).astype(BF16)
        y_copy(i, ys_slot).start()
        return carry

    lax.fori_loop(0, n, tile, 0)

    @pl.when(n >= 2)
    def _():
        y_copy(n - 2, lax.rem(n - 2, 2)).wait()

    y_copy(n - 1, lax.rem(n - 1, 2)).wait()


def _experts(tile_expert, tile_segment, next_expert, n_used, xs, w_gate, w_up, w_down):
    n_prefetch = 4
    anywhere = pl.BlockSpec(memory_space=pl.ANY)
    grid_spec = pltpu.PrefetchScalarGridSpec(
        num_scalar_prefetch=n_prefetch,
        grid=(1,),
        in_specs=[anywhere, anywhere, anywhere, anywhere],
        out_specs=anywhere,
        scratch_shapes=[pltpu.VMEM((X_BUFS, MOE_TM, D_MODEL), BF16), pltpu.VMEM((2, MOE_TM, D_MODEL), BF16),
                        pltpu.VMEM((2, D_MODEL, D_EXPERT), F32), pltpu.VMEM((2, D_MODEL, D_EXPERT), F32),
                        pltpu.VMEM((2, D_EXPERT, D_MODEL), F32),
                        pltpu.VMEM((D_MODEL, 2 * D_EXPERT), BF16), pltpu.VMEM((D_EXPERT, D_MODEL), BF16),
                        pltpu.SemaphoreType.DMA((X_BUFS,)), pltpu.SemaphoreType.DMA((2,)),
                        pltpu.SemaphoreType.DMA((2,))],
    )
    return pl.pallas_call(
        _experts_kernel,
        grid_spec=grid_spec,
        out_shape=jax.ShapeDtypeStruct((N_ROWS, D_MODEL), BF16),
        input_output_aliases={n_prefetch: 0},
        compiler_params=_cparams(("arbitrary",)),
        name="experts",
    )(tile_expert, tile_segment, next_expert, n_used, xs, w_gate, w_up, w_down)


def _combine_kernel(nb_ref, ns_ref, tab_ref, tab_next_ref, route_ref, x1_ref, g2_ref, ys_ref, o_ref,
                    gbuf_ref, sems):
    i = pl.program_id(0)
    last = pl.num_programs(0) - 1
    buf = lax.rem(i, 2)

    def piece(b):
        def make(local, sorted_row, rows):
            return pltpu.make_async_copy(
                ys_ref.at[pl.ds(pl.multiple_of(sorted_row, RUN_ALIGN), rows)],
                gbuf_ref.at[b, pl.ds(pl.multiple_of(local, RUN_ALIGN), rows)], sems.at[b])
        return make

    @pl.when(i == 0)
    def _():
        gbuf_ref[...] = jnp.zeros_like(gbuf_ref)
        _run_copies(tab_ref, nb_ref[0], ns_ref[0], piece(0), lambda cp: cp.start())

    nxt = jnp.minimum(i + 1, last)

    @pl.when(i < last)
    def _():
        _run_copies(tab_next_ref, nb_ref[nxt], ns_ref[nxt], piece(1 - buf), lambda cp: cp.start())

    rec = route_ref[...]
    slot0 = lax.broadcasted_iota(I32, (OUT_TM, COMBINE_K), 1).astype(F32)
    _run_copies(tab_ref, nb_ref[i], ns_ref[i], lambda lo, so, rows: piece(buf)(0, 0, rows), lambda cp: cp.wait())
    moe = jnp.zeros((OUT_TM, D_MODEL), F32)
    for k0 in range(0, LOCAL_ROWS, COMBINE_K):
        s0, s1 = rec[:, 0:1] - float(k0), rec[:, 1:2] - float(k0)
        weights = (jnp.where(slot0 == s0, rec[:, 2:3], 0.0)
                   + jnp.where(slot0 == s1, rec[:, 3:4], 0.0)).astype(BF16)
        moe = moe + jnp.dot(weights, gbuf_ref[buf, k0:k0 + COMBINE_K, :], preferred_element_type=F32)
    o_ref[...] = x1_ref[...] + g2_ref[0] * moe


def _combine(n_big, n_small, piece_table, route, x1, mod3, ys):
    tm = OUT_TM
    steps_per_batch = SEQ // tm
    grid_spec = pltpu.PrefetchScalarGridSpec(
        num_scalar_prefetch=2,
        grid=(N_TOKEN_TILES,),
        in_specs=[_piece_spec(lambda i, nb, ns: (i, 0, 0)),
                  _piece_spec(lambda i, nb, ns: (jnp.minimum(i + 1, N_TOKEN_TILES - 1), 0, 0)),
                  pl.BlockSpec((tm, ROUTE_W), lambda i, nb, ns: (i, 0)),
                  pl.BlockSpec((tm, D_MODEL), lambda i, nb, ns: (i, 0)),
                  pl.BlockSpec((1, 1, D_MODEL), lambda i, nb, ns: ((i // steps_per_batch) * 6 + 5, 0, 0)),
                  pl.BlockSpec(memory_space=pl.ANY)],
        out_specs=pl.BlockSpec((tm, D_MODEL), lambda i, nb, ns: (i, 0)),
        scratch_shapes=[pltpu.VMEM((2, LOCAL_ROWS, D_MODEL), BF16), pltpu.SemaphoreType.DMA((2,))],
    )
    return pl.pallas_call(
        _combine_kernel,
        grid_spec=grid_spec,
        out_shape=jax.ShapeDtypeStruct((TOKENS, D_MODEL), F32),
        compiler_params=_cparams(("arbitrary",)),
        name="combine",
    )(n_big, n_small, piece_table, piece_table, route, x1, mod3, ys)


def kernel(x, c, positions, norm1_w, norm2_w, w_ada, b_ada, w_in, conv_w, conv_b, dt_bias, a_log,
           d_skip, ssd_norm_w, q_norm_w, k_norm_w, sinks, w_out, w_group, b_group, w_expert, b_expert,
           w_gate, w_up, w_down):
    assert x.shape == (BATCH, SEQ, D_MODEL) and w_in.shape == (D_MODEL, IN_WIDTH)
    x2d = x.reshape(TOKENS, D_MODEL)
    mod = _ada_mod(c, w_ada, b_ada)
    mod3 = mod.reshape(BATCH * 6, 1, D_MODEL)

    qkv, z, xbc, dt, dt_t = _in_proj(x2d, norm1_w, mod3, w_in)
    att = _attention(qkv, positions, q_norm_w, k_norm_w, sinks)
    y = _ssd(xbc, z, dt, dt_t, conv_w, conv_b, dt_bias, a_log, d_skip, ssd_norm_w)

    w_router = jnp.pad(jnp.concatenate([w_group, w_expert], axis=1).astype(F32),
                       ((0, 0), (0, LANES - ROUTER_COLS)))
    b_router = jnp.pad(jnp.concatenate([b_group, b_expert]).astype(F32),
                       (0, LANES - ROUTER_COLS)).reshape(1, LANES)
    x1, h2, route, route_t, tcnt = _out_router(att, y, x2d, mod3, w_out.astype(F32), norm2_w,
                                                w_router, b_router)

    tc = tcnt[:, 0, 0:N_EXPERTS].astype(I32)
    run_rows = ((tc + RUN_ALIGN - 1) // RUN_ALIGN) * RUN_ALIGN
    counts = jnp.sum(run_rows, axis=0)
    padded = ((counts + MOE_TM - 1) // MOE_TM) * MOE_TM
    seg_end = jnp.cumsum(padded)
    seg_start = seg_end - padded
    run_dst = seg_start[None, :] + jnp.cumsum(run_rows, axis=0) - run_rows
    n_used = (seg_end[-1] // MOE_TM).reshape(1)
    last_row = jnp.minimum(jnp.arange(N_TILES, dtype=I32) * MOE_TM, seg_end[-1] - 1)
    tile_expert = jnp.sum((seg_end[None, :] <= last_row[:, None]).astype(I32), axis=1)

    run_local = jnp.cumsum(run_rows, axis=1) - run_rows
    n_big_run = run_rows // BIG_PIECE
    n_small_run = (run_rows // RUN_ALIGN) % 2
    q = jnp.arange(PIECE_SLOTS, dtype=I32)
    experts = jnp.arange(N_EXPERTS, dtype=I32)

    def flat(per_run, local0, dst0, stride):
        end = jnp.cumsum(per_run, axis=1)
        run_of = jnp.sum((end[:, None, :] <= q[None, :, None]).astype(I32), axis=2)
        pick = (run_of[:, :, None] == experts[None, None, :]).astype(I32)
        k = q[None, :] - jnp.sum(pick * (end - per_run)[:, None, :], axis=2)
        local = jnp.sum(pick * local0[:, None, :], axis=2) + stride * k
        dst = jnp.sum(pick * dst0[:, None, :], axis=2) + stride * k
        return end[:, -1], local, dst

    n_big, big_local, big_dst = flat(n_big_run, run_local, run_dst, BIG_PIECE)
    n_small, small_local, small_dst = flat(n_small_run, run_local + BIG_PIECE * n_big_run,
                                           run_dst + BIG_PIECE * n_big_run, 0)
    piece_table = jnp.concatenate([big_local, big_dst, small_local, small_dst], axis=1)
    piece_table = piece_table.astype(I32).reshape(N_TOKEN_TILES, 1, TABLE_W)

    nonempty = padded > 0
    seg_rank = jnp.cumsum(nonempty.astype(I32)) - 1
    later = nonempty[None, :] & (experts[None, :] > experts[:, None])
    next_of = jnp.min(jnp.where(later, experts[None, :], N_EXPERTS), axis=1)
    next_of = jnp.where(next_of == N_EXPERTS, -1, next_of)
    tile_is = (tile_expert[:, None] == experts[None, :]).astype(I32)
    tile_segment = jnp.sum(tile_is * seg_rank[None, :], axis=1)
    next_expert = jnp.sum(tile_is * next_of[None, :], axis=1)

    n_big, n_small = n_big.astype(I32), n_small.astype(I32)
    xs = _dispatch(seg_end.astype(I32), (seg_start + counts).astype(I32), n_big, n_small, piece_table,
                   route_t, h2)
    ys = _experts(tile_expert, tile_segment.astype(I32), next_expert.astype(I32), n_used.astype(I32),
                  xs, w_gate, w_up, w_down)
    out = _combine(n_big, n_small, piece_table, route, x1, mod3, ys)
    return out.reshape(BATCH, SEQ, D_MODEL)
```

```python
import jax
import jax.numpy as jnp
from jax import lax
from jax.experimental import pallas as pl
from jax.experimental.pallas import tpu as pltpu

F32 = jnp.float32
BF16 = jnp.bfloat16
I32 = jnp.int32

D_MODEL = 1024
BATCH = 2
SEQ = 8192
TOKENS = BATCH * SEQ
ATT_HEADS = 8
ATT_KV_HEADS = 2
HEAD_DIM = 64
ATT_WIDTH = ATT_HEADS * HEAD_DIM
KV_WIDTH = ATT_KV_HEADS * HEAD_DIM
ATT_BLOCK = 128
ROPE_DIM = HEAD_DIM // 4
ROPE_THETA = 500000.0
SSD_HEADS = 8
SSD_HEAD_DIM = 64
SSD_WIDTH = SSD_HEADS * SSD_HEAD_DIM
SSD_GROUPS = 2
SSD_STATE = 128
CONV_K = 4
CHUNK = 128
XBC_WIDTH = SSD_WIDTH + 2 * SSD_GROUPS * SSD_STATE
IN_WIDTH = ATT_WIDTH + 2 * KV_WIDTH + SSD_WIDTH + XBC_WIDTH + SSD_HEADS
N_GROUPS = 4
EXPERTS_PER_GROUP = 8
N_EXPERTS = N_GROUPS * EXPERTS_PER_GROUP
TOP_K = 2
D_EXPERT = 256
EPS = 1e-6

LANES = 128
QKV_WIDTH = ATT_WIDTH + 2 * KV_WIDTH
IN_PAD = QKV_WIDTH + SSD_WIDTH + XBC_WIDTH + LANES
NEG_BIG = -1e30

VMEM_LIMIT = 48 * 1024 * 1024


def _cparams(sem):
    return pltpu.CompilerParams(dimension_semantics=sem, vmem_limit_bytes=VMEM_LIMIT)


def _split_bf16(x):
    hi = x.astype(BF16)
    lo = (x - hi.astype(F32)).astype(BF16)
    return hi, lo


ADA_TN = 768


def _ada_kernel(ct_ref, w_ref, b_ref, o_ref):
    ct = ct_ref[...]
    s = ct * jax.nn.sigmoid(ct)
    w = w_ref[...]
    rows = [jnp.sum(s[:, b:b + 1] * w, axis=0, keepdims=True) for b in range(BATCH)]
    o_ref[...] = jnp.concatenate(rows, axis=0) + b_ref[...]


INPROJ_TM = 512
_INPROJ_CHUNK = 256


def _inproj_kernel(x_ref, nw_ref, sc_ref, sh_ref, wf_ref, wdt_ref, qkv_ref, z_ref, xbc_ref, dt_ref, dtt_ref,
                   w_ref):
    @pl.when(pl.program_id(0) == 0)
    def _():
        for c0 in range(0, IN_PAD - LANES, _INPROJ_CHUNK):
            w_ref[:, c0:c0 + _INPROJ_CHUNK] = wf_ref[:, c0:c0 + _INPROJ_CHUNK].astype(BF16)
        w_ref[:, IN_PAD - LANES:IN_PAD] = wdt_ref[...].astype(BF16)

    x = x_ref[...]
    y = x * lax.rsqrt(jnp.mean(x * x, axis=-1, keepdims=True) + EPS)
    h = (y * nw_ref[...]) * (1.0 + sc_ref[0]) + sh_ref[0]
    hb = h.astype(BF16)

    def proj(c0, c1):
        return jnp.dot(hb, w_ref[:, c0:c1], preferred_element_type=F32)

    for c0 in range(0, QKV_WIDTH, _INPROJ_CHUNK):
        qkv_ref[:, c0:c0 + _INPROJ_CHUNK] = proj(c0, c0 + _INPROJ_CHUNK).astype(BF16)
    base = QKV_WIDTH
    for c0 in range(0, SSD_WIDTH, _INPROJ_CHUNK):
        z_ref[:, c0:c0 + _INPROJ_CHUNK] = proj(base + c0, base + c0 + _INPROJ_CHUNK).astype(BF16)
    base += SSD_WIDTH
    for c0 in range(0, XBC_WIDTH, _INPROJ_CHUNK):
        xbc_ref[:, c0:c0 + _INPROJ_CHUNK] = proj(base + c0, base + c0 + _INPROJ_CHUNK).astype(BF16)
    base += XBC_WIDTH
    dt = proj(base, base + LANES)
    dt_ref[...] = dt
    dtt_ref[...] = dt.T[0:SSD_HEADS, :]


def _in_proj(x2d, norm_w, mod3, w_in):
    tm = INPROJ_TM
    steps_per_batch = SEQ // tm
    w_dt = jnp.pad(w_in[:, IN_WIDTH - SSD_HEADS:IN_WIDTH].astype(F32), ((0, 0), (0, LANES - SSD_HEADS)))
    return pl.pallas_call(
        _inproj_kernel,
        grid=(TOKENS // tm,),
        in_specs=[pl.BlockSpec((tm, D_MODEL), lambda i: (i, 0)),
                  pl.BlockSpec((1, D_MODEL), lambda i: (0, 0)),
                  pl.BlockSpec((1, 1, D_MODEL), lambda i: ((i // steps_per_batch) * 6 + 1, 0, 0)),
                  pl.BlockSpec((1, 1, D_MODEL), lambda i: ((i // steps_per_batch) * 6 + 0, 0, 0)),
                  pl.BlockSpec((D_MODEL, IN_WIDTH), lambda i: (0, 0), pipeline_mode=pl.Buffered(1)),
                  pl.BlockSpec((D_MODEL, LANES), lambda i: (0, 0))],
        out_specs=[pl.BlockSpec((tm, QKV_WIDTH), lambda i: (i, 0)),
                   pl.BlockSpec((tm, SSD_WIDTH), lambda i: (i, 0)),
                   pl.BlockSpec((tm, XBC_WIDTH), lambda i: (i, 0)),
                   pl.BlockSpec((tm, LANES), lambda i: (i, 0)),
                   pl.BlockSpec((SSD_HEADS, tm), lambda i: (0, i))],
        out_shape=[jax.ShapeDtypeStruct((TOKENS, QKV_WIDTH), BF16),
                   jax.ShapeDtypeStruct((TOKENS, SSD_WIDTH), BF16),
                   jax.ShapeDtypeStruct((TOKENS, XBC_WIDTH), BF16),
                   jax.ShapeDtypeStruct((TOKENS, LANES), F32),
                   jax.ShapeDtypeStruct((SSD_HEADS, TOKENS), F32)],
        scratch_shapes=[pltpu.VMEM((D_MODEL, IN_PAD), BF16)],
        compiler_params=_cparams(("arbitrary",)),
        name="in_proj",
    )(x2d, norm_w.reshape(1, D_MODEL), mod3, mod3, w_in.astype(F32), w_dt)


ATT_SUB = 8


ROPE_TM = 2048
_ROPE_HALF = ROPE_DIM // 2
_TOK_PER_ROW = LANES // _ROPE_HALF


def _exact_dot(x, onehot_b):
    hi, lo = _split_bf16(x)
    return (jnp.dot(hi, onehot_b, preferred_element_type=F32)
            + jnp.dot(lo, onehot_b, preferred_element_type=F32))


def _rope_kernel(pos_ref, freq_ref, sel_ref, own_ref, gcos_ref, gs1_ref, gs2_ref, ident_ref,
                 cos_ref, s1_ref, s2_ref):
    ang = pos_ref[...].astype(F32) * freq_ref[...]
    cos_p, sin_p = jnp.cos(ang), jnp.sin(ang)
    hi_c, lo_c = _split_bf16(cos_p)
    hi_s, lo_s = _split_bf16(sin_p)
    sel = sel_ref[...]
    rows_c = jnp.dot(sel, hi_c, preferred_element_type=F32) + jnp.dot(sel, lo_c, preferred_element_type=F32)
    rows_s = jnp.dot(sel, hi_s, preferred_element_type=F32) + jnp.dot(sel, lo_s, preferred_element_type=F32)
    own = own_ref[...]
    cos_ref[...] = _exact_dot(rows_c * own, gcos_ref[...]) + ident_ref[...]
    s1_ref[...] = _exact_dot(rows_s * own, gs1_ref[...])
    s2_ref[...] = _exact_dot(rows_s * own, gs2_ref[...])


def _tables_kernel(pos_ref, freq_ref, sel_ref, own_ref, gcos_ref, gs1_ref, gs2_ref, ident_ref,
                   ct_ref, w_ref, b_ref, cos_ref, s1_ref, s2_ref, mod_ref):
    _rope_kernel(pos_ref, freq_ref, sel_ref, own_ref, gcos_ref, gs1_ref, gs2_ref, ident_ref,
                 cos_ref, s1_ref, s2_ref)
    _ada_kernel(ct_ref, w_ref, b_ref, mod_ref)


def _rope_tables_and_mod(positions, c, w_ada, b_ada):
    n_mod = w_ada.shape[1]
    assert TOKENS // ROPE_TM == n_mod // ADA_TN
    half, per_row = _ROPE_HALF, _TOK_PER_ROW
    rows = ROPE_TM // per_row
    pos_rep = jnp.repeat(positions.reshape(TOKENS).astype(I32), half).reshape(TOKENS // per_row, LANES)
    inv_freq = jnp.power(ROPE_THETA, -jnp.arange(half, dtype=F32) * 2.0 / ROPE_DIM)
    freq = jnp.tile(inv_freq, per_row).reshape(1, LANES)
    tok = jnp.arange(ROPE_TM)
    lane = jnp.arange(LANES)
    sel = (tok[:, None] // per_row == jnp.arange(rows)[None, :]).astype(BF16)
    own = (lane[None, :] // half == tok[:, None] % per_row).astype(F32)
    d = lane % HEAD_DIM
    src_f = lane % half
    hits = lambda lo, hi: ((src_f[:, None] == d[None, :] % half) & (d[None, :] >= lo) & (d[None, :] < hi))
    gcos = hits(0, ROPE_DIM).astype(BF16)
    gs1 = -hits(0, half).astype(BF16)
    gs2 = hits(half, ROPE_DIM).astype(BF16)
    ident = (d >= ROPE_DIM).astype(F32).reshape(1, LANES)
    const = lambda shape: pl.BlockSpec(shape, lambda i: (0, 0))
    out_spec = pl.BlockSpec((ROPE_TM, LANES), lambda i: (i, 0))
    out = jax.ShapeDtypeStruct((TOKENS, LANES), F32)
    return pl.pallas_call(
        _tables_kernel,
        grid=(TOKENS // ROPE_TM,),
        in_specs=[pl.BlockSpec((rows, LANES), lambda i: (i, 0)), const((1, LANES)),
                  const((ROPE_TM, rows)), const((ROPE_TM, LANES)),
                  const((LANES, LANES)), const((LANES, LANES)), const((LANES, LANES)), const((1, LANES)),
                  const((D_MODEL, BATCH)),
                  pl.BlockSpec((D_MODEL, ADA_TN), lambda i: (0, i)),
                  pl.BlockSpec((1, ADA_TN), lambda i: (0, i))],
        out_specs=[out_spec, out_spec, out_spec, pl.BlockSpec((BATCH, ADA_TN), lambda i: (0, i))],
        out_shape=[out, out, out, jax.ShapeDtypeStruct((BATCH, n_mod), F32)],
        compiler_params=_cparams(("arbitrary",)),
        name="rope_tables_ada_mod",
    )(pos_rep, freq, sel, own, gcos, gs1, gs2, ident, c.T, w_ada, b_ada.reshape(1, n_mod))


def _seg_meansq(xf, ones128):
    rows, width = xf.shape
    nt = width // LANES
    parts = _split_bf16(xf * xf)
    stacked = jnp.concatenate([p[:, t * LANES:(t + 1) * LANES] for p in parts for t in range(nt)], axis=0)
    tot = jnp.dot(stacked, ones128, preferred_element_type=F32)
    tiles = [tot[t * rows:(t + 1) * rows] + tot[(nt + t) * rows:(nt + t + 1) * rows] for t in range(nt)]
    return jnp.concatenate(tiles, axis=1) * (1.0 / HEAD_DIM)


def _norm_rope(x_bf, w_row, ones_bd, cosf, s1, s2):
    xf = x_bf.astype(F32)
    width = xf.shape[1]
    xn = xf * lax.rsqrt(_seg_meansq(xf, ones_bd) + EPS) * w_row
    half = ROPE_DIM // 2
    up = pltpu.roll(xn, width - half, axis=1)
    down = pltpu.roll(xn, half, axis=1)
    return xn * cosf + up * s1 + down * s2


def _attn_kernel(sink_ref, q_ref, kv_ref, cos_ref, s1_ref, s2_ref, qw_ref, kw_ref,
                 ones_ref, o_ref, kprev_ref, vprev_ref):
    j = pl.program_id(1)
    blk = ATT_BLOCK

    @pl.when(j == 0)
    def _():
        kprev_ref[...] = jnp.zeros_like(kprev_ref)
        vprev_ref[...] = jnp.zeros_like(vprev_ref)

    cos1 = cos_ref[...]
    s1_1 = s1_ref[...]
    s2_1 = s2_ref[...]
    reps = ATT_WIDTH // LANES
    cosq = jnp.concatenate([cos1] * reps, axis=1)
    s1q = jnp.concatenate([s1_1] * reps, axis=1)
    s2q = jnp.concatenate([s2_1] * reps, axis=1)

    q = _norm_rope(q_ref[...], qw_ref[...], ones_ref[...], cosq, s1q, s2q)
    qf = q * (HEAD_DIM ** -0.5)
    kv = kv_ref[...]
    kn = _norm_rope(kv[:, 0:KV_WIDTH], kw_ref[...], ones_ref[...], cos1, s1_1, s2_1)
    vn = kv[:, KV_WIDTH:2 * KV_WIDTH].astype(F32)

    kall = jnp.concatenate([kprev_ref[...], kn], axis=0)
    vall = jnp.concatenate([vprev_ref[...], vn], axis=0)
    kprev_ref[...] = kn[(ATT_SUB - 1) * blk:ATT_SUB * blk]
    vprev_ref[...] = vn[(ATT_SUB - 1) * blk:ATT_SUB * blk]

    lo_all = lax.broadcasted_iota(I32, kall.shape, 1) < HEAD_DIM
    ones_all = jnp.ones(kall.shape, BF16)

    row = lax.broadcasted_iota(I32, (2 * blk, blk), 0)
    col = lax.broadcasted_iota(I32, (2 * blk, blk), 1)
    from_prev = col > (row & (blk - 1))
    second_tile = lax.broadcasted_iota(I32, (2 * blk, 1), 0) >= blk
    zero_p = jnp.zeros((2 * blk, blk), F32)

    k_par, v_par = [], []
    for g in range(ATT_KV_HEADS):
        keep = lo_all if g == 0 else ~lo_all
        k_own = jnp.where(keep, kall, 0.0)
        v_own = jnp.where(keep, vall, 0.0)
        k_oth = pltpu.roll(k_own, HEAD_DIM, axis=1)
        v_oth = pltpu.roll(v_own, HEAD_DIM, axis=1)
        k_lo, k_hi = (k_own, k_oth) if g == 0 else (k_oth, k_own)
        v_lo, v_hi = (v_own, v_oth) if g == 0 else (v_oth, v_own)
        k_par.append((k_lo.astype(BF16), k_hi.astype(BF16)))
        v_par.append((jnp.concatenate([v_lo.astype(BF16), ones_all], axis=1),
                      jnp.concatenate([v_hi.astype(BF16), ones_all], axis=1)))

    problems = [(g, sub) for g in range(ATT_KV_HEADS) for sub in range(ATT_SUB)]
    scores = []
    for g, sub in problems:
        r0, c0 = sub * blk, g * 2 * LANES
        qcat = jnp.concatenate([qf[r0:r0 + blk, c0:c0 + LANES],
                                qf[r0:r0 + blk, c0 + LANES:c0 + 2 * LANES]], axis=0).astype(BF16)
        kw = jnp.concatenate([k_par[g][0][r0:r0 + 2 * blk], k_par[g][1][r0:r0 + 2 * blk]], axis=0)
        scores.append(lax.dot_general(qcat, kw, (((1,), (1,)), ((), ())),
                                      preferred_element_type=F32))

    weights, rescale = [], []
    for (g, sub), s_all in zip(problems, scores):
        for par in range(2):
            s = s_all[:, par * 2 * blk:(par + 1) * 2 * blk]
            s_prev = s[:, 0:blk]
            if sub == 0:
                s_prev = s_prev + jnp.where(j > 0, 0.0, NEG_BIG)
            s = jnp.where(from_prev, s_prev, s[:, blk:2 * blk])
            h_first = ATT_HEADS // ATT_KV_HEADS * g + par
            sink = jnp.where(second_tile, sink_ref[h_first + 2], sink_ref[h_first])
            m = jnp.maximum(jnp.max(s, axis=-1, keepdims=True), sink)
            p = jnp.exp(s - m)
            weights.append(jnp.concatenate([jnp.where(from_prev, p, zero_p), jnp.where(from_prev, zero_p, p)],
                                           axis=1).astype(BF16))
            rescale.append(jnp.exp(sink - m))

    outs = []
    for idx, (g, sub) in enumerate(problems):
        for par in range(2):
            outs.append(jnp.dot(weights[2 * idx + par], v_par[g][par][sub * blk:(sub + 2) * blk],
                                preferred_element_type=F32))

    for idx, (g, sub) in enumerate(problems):
        r0, c0 = sub * blk, g * 2 * LANES
        pair = None
        for par in range(2):
            o = outs[2 * idx + par]
            part = o[:, 0:LANES] * (1.0 / (o[:, LANES:2 * LANES] + rescale[2 * idx + par]))
            pair = part if pair is None else pair + part
        o_ref[r0:r0 + blk, c0:c0 + LANES] = pair[0:blk].astype(BF16)
        o_ref[r0:r0 + blk, c0 + LANES:c0 + 2 * LANES] = pair[blk:2 * blk].astype(BF16)


def _attention(qkv, rope_tables, q_norm_w, k_norm_w, sinks):
    cosf, s1, s2 = rope_tables
    qw = jnp.tile(q_norm_w.astype(F32), ATT_HEADS).reshape(1, ATT_WIDTH)
    kw = jnp.tile(k_norm_w.astype(F32), ATT_KV_HEADS).reshape(1, KV_WIDTH)
    seg = jnp.arange(LANES) // HEAD_DIM
    ones128 = (seg[:, None] == seg[None, :]).astype(BF16)
    const = lambda shape: pl.BlockSpec(shape, lambda b, j, s: (0, 0))
    rows = ATT_SUB * ATT_BLOCK
    nb = SEQ // rows
    tok = lambda width, cb: pl.BlockSpec((rows, width), lambda b, j, s: (b * nb + j, cb))
    grid_spec = pltpu.PrefetchScalarGridSpec(
        num_scalar_prefetch=1,
        grid=(BATCH, nb),
        in_specs=[tok(ATT_WIDTH, 0), tok(2 * KV_WIDTH, 2), tok(LANES, 0), tok(LANES, 0), tok(LANES, 0),
                  const((1, ATT_WIDTH)), const((1, KV_WIDTH)), const((LANES, LANES))],
        out_specs=tok(ATT_WIDTH, 0),
        scratch_shapes=[pltpu.VMEM((ATT_BLOCK, KV_WIDTH), F32),
                        pltpu.VMEM((ATT_BLOCK, KV_WIDTH), F32)],
    )
    return pl.pallas_call(
        _attn_kernel,
        grid_spec=grid_spec,
        out_shape=jax.ShapeDtypeStruct((TOKENS, ATT_WIDTH), BF16),
        compiler_params=_cparams(("arbitrary", "arbitrary")),
        name="attention",
    )(sinks.astype(F32), qkv, qkv, cosf, s1, s2, qw, kw, ones128)


SSD_SUB = 4


def _softplus(x):
    return jnp.maximum(x, 0.0) + jnp.log1p(jnp.exp(-jnp.abs(x)))


def _silu(x):
    h = 0.5 * x
    return h + h * jnp.tanh(h)


def _ssd_kernel(xbc_ref, z_ref, dt_ref, dtt_ref, cw_ref, cb_ref, dtb_row_ref, dtb_col_ref,
                alog_row_ref, alog_col_ref, dskip_ref, nw_ref, tril_ref, triu_ref,
                o_ref, conv_ref, state_ref):
    c = pl.program_id(1)
    L = CHUNK
    tail = 8

    @pl.when(c == 0)
    def _():
        conv_ref[0:tail, :] = jnp.zeros((tail, XBC_WIDTH), F32)
        state_ref[...] = jnp.zeros_like(state_ref)

    row = lax.broadcasted_iota(I32, (L, L), 0)
    col = lax.broadcasted_iota(I32, (L, L), 1)
    causal = col <= row
    lane = lax.broadcasted_iota(I32, (L, LANES), 1)
    lo_half = lane < SSD_HEAD_DIM

    prepared = [_ssd_prepare(s * L, xbc_ref, dt_ref, dtt_ref, cw_ref, cb_ref, dtb_row_ref, dtb_col_ref,
                             alog_row_ref, alog_col_ref, tril_ref, triu_ref, conv_ref)
                for s in range(SSD_SUB)]
    for s in range(SSD_SUB):
        _ssd_chunk(s * L, prepared[s], causal, lo_half, z_ref, dskip_ref, nw_ref, o_ref, state_ref)


def _ssd_prepare(r0, xbc_ref, dt_ref, dtt_ref, cw_ref, cb_ref, dtb_row_ref, dtb_col_ref,
                 alog_row_ref, alog_col_ref, tril_ref, triu_ref, conv_ref):
    L = CHUNK
    tail = 8
    xb = xbc_ref[r0:r0 + L, :].astype(F32)
    conv_ref[tail:tail + L, :] = xb
    acc = cb_ref[...] + cw_ref[CONV_K - 1:CONV_K, :] * xb
    for k in range(CONV_K - 1):
        off = tail - (CONV_K - 1) + k
        acc = acc + cw_ref[k:k + 1, :] * conv_ref[off:off + L, :]
    conv_ref[0:tail, :] = xb[L - tail:L, :]
    u = _silu(acc)
    xs = u[:, 0:SSD_WIDTH]
    bmat = u[:, SSD_WIDTH:SSD_WIDTH + SSD_GROUPS * SSD_STATE]
    cmat = u[:, SSD_WIDTH + SSD_GROUPS * SSD_STATE:XBC_WIDTH]

    dt = _softplus(dt_ref[r0:r0 + L, :] + dtb_row_ref[...])
    a = dt * (-jnp.exp(alog_row_ref[...]))
    a_hi, a_lo = _split_bf16(a)
    a_cum = (jnp.dot(tril_ref[...], a_hi, preferred_element_type=F32)
             + jnp.dot(tril_ref[...], a_lo, preferred_element_type=F32))
    dt_t = _softplus(dtt_ref[:, r0:r0 + L] + dtb_col_ref[...])
    a_t = dt_t * (-jnp.exp(alog_col_ref[...]))
    at_hi, at_lo = _split_bf16(a_t)
    a_cum_t = (jnp.dot(at_hi, triu_ref[...], preferred_element_type=F32)
               + jnp.dot(at_lo, triu_ref[...], preferred_element_type=F32))
    a_end_t = a_cum_t[:, L - 1:L]
    return dict(
        xs=xs, bmat=bmat, cmat=cmat, a_cum=a_cum, exp_a_cum=jnp.exp(a_cum),
        shifted_t=a_cum_t - jnp.log(dt_t),
        wst_t=jnp.exp(a_end_t - a_cum_t) * dt_t,
        cdec_t=jnp.exp(a_end_t))


def _ssd_chunk(r0, p, causal, lo_half, z_ref, dskip_ref, nw_ref, o_ref, state_ref):
    L = CHUNK
    xs, bmat, cmat, a_cum, exp_a_cum = p["xs"], p["bmat"], p["cmat"], p["a_cum"], p["exp_a_cum"]
    shifted_t, wst_t, cdec_t = p["shifted_t"], p["wst_t"], p["cdec_t"]
    xs_b = xs.astype(BF16)
    heads_per_group = SSD_HEADS // SSD_GROUPS
    gated = []
    for g in range(SSD_GROUPS):
        b_g = bmat[:, g * SSD_STATE:(g + 1) * SSD_STATE]
        c_g = cmat[:, g * SSD_STATE:(g + 1) * SSD_STATE]
        cb = lax.dot_general(c_g.astype(BF16), b_g.astype(BF16), (((1,), (1,)), ((), ())),
                             preferred_element_type=F32)
        b_gt = b_g.T
        for t in range(heads_per_group // 2):
            tile = g * (heads_per_group // 2) + t
            c0 = tile * LANES
            xs_tile = xs_b[:, c0:c0 + LANES]
            st_tile = state_ref[:, c0:c0 + LANES]
            st_b = st_tile.astype(BF16)
            y_tile = jnp.zeros((L, LANES), F32)
            new_tile = jnp.zeros((SSD_STATE, LANES), F32)
            for e in range(2):
                h = 2 * tile + e
                keep = lo_half if e == 0 else ~lo_half
                colb = jnp.broadcast_to(a_cum[:, h:h + 1], (L, L))
                rowb = shifted_t[h:h + 1, :]
                w_in = cb * jnp.exp(jnp.where(causal, colb - rowb, NEG_BIG))
                w_off = c_g * jnp.broadcast_to(exp_a_cum[:, h:h + 1], (L, L))
                lhs = jnp.concatenate([w_in, w_off], axis=1).astype(BF16)
                rhs = jnp.concatenate([jnp.where(keep, xs_tile, jnp.zeros_like(xs_tile)),
                                       jnp.where(keep, st_b, jnp.zeros_like(st_b))], axis=0)
                y_tile = y_tile + jnp.dot(lhs, rhs, preferred_element_type=F32)
                m_h = (b_gt * wst_t[h:h + 1, :]).astype(BF16)
                new_tile = new_tile + jnp.dot(m_h, jnp.where(keep, xs_tile, jnp.zeros_like(xs_tile)),
                                              preferred_element_type=F32)
            cd = jnp.where(lo_half[0:1, :], cdec_t[2 * tile:2 * tile + 1, :],
                           cdec_t[2 * tile + 1:2 * tile + 2, :])
            state_ref[:, c0:c0 + LANES] = st_tile * cd + new_tile
            y_full = y_tile + dskip_ref[:, c0:c0 + LANES] * xs[:, c0:c0 + LANES]
            gated.append(y_full * _silu(z_ref[r0:r0 + L, c0:c0 + LANES].astype(F32)))

    gw = SSD_WIDTH // SSD_GROUPS
    tiles_per_group = gw // LANES
    for g in range(SSD_GROUPS):
        yg = jnp.concatenate(gated[g * tiles_per_group:(g + 1) * tiles_per_group], axis=1)
        ms = jnp.mean(yg * yg, axis=-1, keepdims=True)
        o_ref[r0:r0 + L, g * gw:(g + 1) * gw] = (
            (yg * lax.rsqrt(ms + EPS)) * nw_ref[:, g * gw:(g + 1) * gw]).astype(o_ref.dtype)


def _ssd(xbc, z, dt, dt_t, conv_w, conv_b, dt_bias, a_log, d_skip, ssd_norm_w):
    L = SSD_SUB * CHUNK
    nc = SEQ // L
    pad_row = lambda v: jnp.pad(v.astype(F32), (0, LANES - SSD_HEADS)).reshape(1, LANES)
    col8 = lambda v: v.astype(F32).reshape(SSD_HEADS, 1)
    idx = jnp.arange(CHUNK)
    tril = (idx[None, :] <= idx[:, None]).astype(BF16)
    triu = (idx[:, None] <= idx[None, :]).astype(BF16)
    dskip = jnp.repeat(d_skip.astype(F32), SSD_HEAD_DIM).reshape(1, SSD_WIDTH)
    const = lambda shape: pl.BlockSpec(shape, lambda b, c: (0, 0))
    tok = lambda width: pl.BlockSpec((L, width), lambda b, c: (b * nc + c, 0))
    return pl.pallas_call(
        _ssd_kernel,
        grid=(BATCH, nc),
        in_specs=[tok(XBC_WIDTH), tok(SSD_WIDTH), tok(LANES),
                  pl.BlockSpec((SSD_HEADS, L), lambda b, c: (0, b * nc + c)),
                  const((CONV_K, XBC_WIDTH)), const((1, XBC_WIDTH)),
                  const((1, LANES)), const((SSD_HEADS, 1)), const((1, LANES)), const((SSD_HEADS, 1)),
                  const((1, SSD_WIDTH)), const((1, SSD_WIDTH)), const((CHUNK, CHUNK)), const((CHUNK, CHUNK))],
        out_specs=tok(SSD_WIDTH),
        out_shape=jax.ShapeDtypeStruct((TOKENS, SSD_WIDTH), BF16),
        scratch_shapes=[pltpu.VMEM((8 + CHUNK, XBC_WIDTH), F32),
                        pltpu.VMEM((SSD_STATE, SSD_WIDTH), F32)],
        compiler_params=_cparams(("arbitrary", "arbitrary")),
        name="ssd",
    )(xbc, z, dt, dt_t, conv_w.astype(F32), conv_b.astype(F32).reshape(1, XBC_WIDTH),
      pad_row(dt_bias), col8(dt_bias), pad_row(a_log), col8(a_log), dskip,
      ssd_norm_w.astype(F32).reshape(1, SSD_WIDTH), tril, triu)


OUT_TM = 512
ROUTE_W = 8
ROUTER_COLS = N_GROUPS + N_EXPERTS
RUN_ALIGN = 16
RUN_SHIFT = 4
LOCAL_ROWS = 1536
assert RUN_ALIGN == 1 << RUN_SHIFT and LOCAL_ROWS >= TOP_K * OUT_TM + N_EXPERTS * (RUN_ALIGN - 1)


def _lane_pick(values, lane, index):
    return jnp.sum(jnp.where(lane == index, values, 0.0), axis=-1, keepdims=True)


def _first_argmax(vals, lane):
    m = jnp.max(vals, axis=-1, keepdims=True)
    idx = jnp.min(jnp.where(vals == m, lane, float(LANES)), axis=-1, keepdims=True)
    return m, idx


def _out_router_kernel(att_ref, y_ref, x_ref, g1_ref, wof_ref, nw_ref, sc_ref, sh_ref, wr_ref, br_ref,
                       ltri_ref, sut_ref, x1_ref, h2_ref, route_ref, routet_ref, tcnt_ref,
                       wr_split_ref, logits_ref, wo_ref):
    i = pl.program_id(0)

    @pl.when(i == 0)
    def _():
        hi, lo = _split_bf16(wr_ref[...])
        wr_split_ref[:, 0:LANES] = hi
        wr_split_ref[:, LANES:2 * LANES] = lo
        logits_ref[...] = jnp.zeros_like(logits_ref)
        for r0 in range(0, D_MODEL, 256):
            wo_ref[r0:r0 + 256, :] = wof_ref[r0:r0 + 256, :].astype(BF16)

    logits = logits_ref[...]

    mixer = (jnp.dot(att_ref[...], wo_ref[0:ATT_WIDTH, :], preferred_element_type=F32)
             + jnp.dot(y_ref[...], wo_ref[ATT_WIDTH:ATT_WIDTH + SSD_WIDTH, :], preferred_element_type=F32))
    x1 = x_ref[...] + g1_ref[0] * mixer
    x1_ref[...] = x1
    yn = x1 * lax.rsqrt(jnp.mean(x1 * x1, axis=-1, keepdims=True) + EPS)
    h2 = (yn * nw_ref[...]) * (1.0 + sc_ref[0]) + sh_ref[0]
    h2_ref[...] = h2.astype(BF16)

    h_hi, h_lo = _split_bf16(h2)
    both = jnp.dot(h_hi, wr_split_ref[...], preferred_element_type=F32)
    logits_ref[...] = (both[:, 0:LANES] + both[:, LANES:2 * LANES]
                       + jnp.dot(h_lo, wr_split_ref[:, 0:LANES], preferred_element_type=F32)) + br_ref[...]

    tm = logits.shape[0]
    lane = lax.broadcasted_iota(I32, (tm, LANES), 1).astype(F32)

    gl = jnp.where(lane < N_GROUPS, logits, NEG_BIG)
    gmax, gidx = _first_argmax(gl, lane)
    g_p = 1.0 / jnp.sum(jnp.exp(gl - gmax), axis=-1, keepdims=True)

    lo_lane = N_GROUPS + EXPERTS_PER_GROUP * gidx
    el = jnp.where((lane >= lo_lane) & (lane < lo_lane + EXPERTS_PER_GROUP), logits, NEG_BIG)
    m1, i1 = _first_argmax(el, lane)
    m2, i2 = _first_argmax(jnp.where(lane == i1, NEG_BIG, el), lane)
    r = jnp.exp(m2 - m1)
    p1 = 1.0 / (1.0 + r)
    p2 = r / (1.0 + r)
    e0 = i1 - N_GROUPS
    e1 = i2 - N_GROUPS

    onehot = ((lane == e0) | (lane == e1)).astype(F32)
    tile_cnt = jnp.sum(onehot, axis=0, keepdims=True)
    run_len = jnp.floor((tile_cnt + (RUN_ALIGN - 1)) * (1.0 / RUN_ALIGN)) * RUN_ALIGN
    run_start = jnp.dot(jnp.broadcast_to(run_len, (8, LANES)).astype(BF16), sut_ref[...],
                        preferred_element_type=F32)[0:1, :]
    before = jnp.dot(ltri_ref[...], onehot.astype(BF16), preferred_element_type=F32) + run_start
    slot0 = _lane_pick(before, lane, e0)
    slot1 = _lane_pick(before, lane, e1)
    tcnt_ref[0] = tile_cnt

    rec = jnp.zeros((tm, LANES), F32)
    for k, v in enumerate([slot0, slot1, g_p * p1, g_p * p2, e0, e1]):
        rec = jnp.where(lane == k, v, rec)
    route_ref[...] = rec[:, 0:ROUTE_W]
    routet_ref[...] = rec.T[0:ROUTE_W, :]


def _out_router(att, y, x2d, mod3, w_out_b, norm_w, w_router, b_router):
    tm = OUT_TM
    n_steps = TOKENS // tm
    steps_per_batch = SEQ // tm
    idx = jnp.arange(tm)
    ltri = (idx[None, :] < idx[:, None]).astype(BF16)
    lidx = jnp.arange(LANES)
    sut = (lidx[:, None] < lidx[None, :]).astype(BF16)
    const = lambda shape: pl.BlockSpec(shape, lambda i: (0, 0))
    cur = lambda i: jnp.minimum(i, n_steps - 1)
    prev = lambda i: jnp.maximum(i - 1, 0)
    tok = lambda width: pl.BlockSpec((tm, width), lambda i: (cur(i), 0))
    modspec = lambda k: pl.BlockSpec((1, 1, D_MODEL), lambda i: ((cur(i) // steps_per_batch) * 6 + k, 0, 0))
    return pl.pallas_call(
        _out_router_kernel,
        grid=(n_steps + 1,),
        in_specs=[tok(ATT_WIDTH), tok(SSD_WIDTH), tok(D_MODEL), modspec(2),
                  const((D_MODEL, D_MODEL)), const((1, D_MODEL)), modspec(4), modspec(3),
                  const((D_MODEL, LANES)), const((1, LANES)), const((tm, tm)), const((LANES, LANES))],
        out_specs=[tok(D_MODEL), tok(D_MODEL),
                   pl.BlockSpec((tm, ROUTE_W), lambda i: (prev(i), 0)),
                   pl.BlockSpec((ROUTE_W, tm), lambda i: (prev(i), 0)),
                   pl.BlockSpec((1, 1, LANES), lambda i: (prev(i), 0, 0))],
        out_shape=[jax.ShapeDtypeStruct((TOKENS, D_MODEL), F32),
                   jax.ShapeDtypeStruct((TOKENS, D_MODEL), BF16),
                   jax.ShapeDtypeStruct((TOKENS, ROUTE_W), F32),
                   jax.ShapeDtypeStruct((n_steps * ROUTE_W, tm), F32),
                   jax.ShapeDtypeStruct((n_steps, 1, LANES), F32)],
        scratch_shapes=[pltpu.VMEM((D_MODEL, 2 * LANES), BF16), pltpu.VMEM((tm, LANES), F32),
                        pltpu.VMEM((D_MODEL, D_MODEL), BF16)],
        compiler_params=_cparams(("arbitrary",)),
        name="out_router",
    )(att, y, x2d, mod3, w_out_b, norm_w.reshape(1, D_MODEL), mod3, mod3, w_router, b_router, ltri, sut)


MOE_TM = 512
ZERO_ROWS = 256
N_TOKEN_TILES = TOKENS // OUT_TM
MAX_SORTED_ROWS = TOKENS * TOP_K + N_TOKEN_TILES * N_EXPERTS * (RUN_ALIGN - 1)
N_TILES = MAX_SORTED_ROWS // MOE_TM + N_EXPERTS
N_ROWS = N_TILES * MOE_TM
assert MOE_TM % ZERO_ROWS == 0


BIG_PIECE = 2 * RUN_ALIGN
PIECE_SLOTS = LOCAL_ROWS // BIG_PIECE
TABLE_W = 4 * PIECE_SLOTS
COMBINE_K = 256
assert PIECE_SLOTS >= N_EXPERTS and LOCAL_ROWS % COMBINE_K == 0


def _run_copies(table_ref, n_big, n_small, make_copy, action):
    def big(q, carry):
        action(make_copy(table_ref[0, 0, q], table_ref[0, 0, PIECE_SLOTS + q], BIG_PIECE))
        return carry

    def small(q, carry):
        action(make_copy(table_ref[0, 0, 2 * PIECE_SLOTS + q], table_ref[0, 0, 3 * PIECE_SLOTS + q], RUN_ALIGN))
        return carry

    lax.fori_loop(0, n_big, big, 0)
    lax.fori_loop(0, n_small, small, 0)


def _dispatch_kernel(seg_end_ref, used_end_ref, nb_ref, ns_ref, tab_ref, routet_ref, h2_ref, xs_ref,
                     sbuf_ref, zero_ref, sems, zsem):
    i = pl.program_id(0)
    last = pl.num_programs(0) - 1
    buf = lax.rem(i, 2)

    def zero_fills(action):
        def tail_copy(row):
            return pltpu.make_async_copy(zero_ref.at[pl.ds(0, RUN_ALIGN)],
                                         xs_ref.at[pl.ds(pl.multiple_of(row, RUN_ALIGN), RUN_ALIGN)], zsem)

        def block_copy(block):
            start = pl.multiple_of(block * ZERO_ROWS, ZERO_ROWS)
            return pltpu.make_async_copy(zero_ref, xs_ref.at[pl.ds(start, ZERO_ROWS)], zsem)

        def tails(e, carry):
            def body(r, c):
                action(tail_copy(r * RUN_ALIGN))
                return c

            lax.fori_loop(used_end_ref[e] // RUN_ALIGN, seg_end_ref[e] // RUN_ALIGN, body, 0)
            return carry

        def blocks(block, carry):
            action(block_copy(block))
            return carry

        lax.fori_loop(0, N_EXPERTS, tails, 0)
        lax.fori_loop(seg_end_ref[N_EXPERTS - 1] // ZERO_ROWS, N_ROWS // ZERO_ROWS, blocks, 0)

    @pl.when(i == 0)
    def _():
        zero_ref[...] = jnp.zeros_like(zero_ref)
        zero_fills(lambda cp: cp.start())

    slot = lax.broadcasted_iota(I32, (LOCAL_ROWS, OUT_TM), 0).astype(F32)
    perm = jnp.where((slot == routet_ref[0:1, :]) | (slot == routet_ref[1:2, :]), 1.0, 0.0).astype(BF16)
    sbuf_ref[buf] = jnp.dot(perm, h2_ref[...], preferred_element_type=F32).astype(BF16)

    def piece(b):
        def make(local, sorted_row, rows):
            return pltpu.make_async_copy(
                sbuf_ref.at[b, pl.ds(pl.multiple_of(local, RUN_ALIGN), rows)],
                xs_ref.at[pl.ds(pl.multiple_of(sorted_row, RUN_ALIGN), rows)], sems.at[b])
        return make

    _run_copies(tab_ref, nb_ref[i], ns_ref[i], piece(buf), lambda cp: cp.start())
    prev = jnp.maximum(i - 1, 0)

    @pl.when(i > 0)
    def _():
        _run_copies(tab_ref, nb_ref[prev], ns_ref[prev], lambda lo, so, rows: piece(1 - buf)(0, 0, rows),
                    lambda cp: cp.wait())

    @pl.when(i == last)
    def _():
        _run_copies(tab_ref, nb_ref[i], ns_ref[i], lambda lo, so, rows: piece(buf)(0, 0, rows),
                    lambda cp: cp.wait())
        zero_fills(lambda cp: cp.wait())


def _piece_spec(index_map):
    return pl.BlockSpec((1, 1, TABLE_W), index_map, memory_space=pltpu.SMEM)


def _dispatch(seg_end, used_end, n_big, n_small, piece_table, route_t, h2):
    grid_spec = pltpu.PrefetchScalarGridSpec(
        num_scalar_prefetch=4,
        grid=(N_TOKEN_TILES,),
        in_specs=[_piece_spec(lambda i, se, ue, nb, ns: (i, 0, 0)),
                  pl.BlockSpec((ROUTE_W, OUT_TM), lambda i, se, ue, nb, ns: (i, 0)),
                  pl.BlockSpec((OUT_TM, D_MODEL), lambda i, se, ue, nb, ns: (i, 0))],
        out_specs=pl.BlockSpec(memory_space=pl.ANY),
        scratch_shapes=[pltpu.VMEM((2, LOCAL_ROWS, D_MODEL), BF16),
                        pltpu.VMEM((ZERO_ROWS, D_MODEL), BF16),
                        pltpu.SemaphoreType.DMA((2,)), pltpu.SemaphoreType.DMA],
    )
    return pl.pallas_call(
        _dispatch_kernel,
        grid_spec=grid_spec,
        out_shape=jax.ShapeDtypeStruct((N_ROWS, D_MODEL), BF16),
        compiler_params=_cparams(("arbitrary",)),
        name="dispatch",
    )(seg_end, used_end, n_big, n_small, piece_table, route_t, h2)


X_BUFS = 3


def _experts_kernel(te_ref, seg_ref, nxt_ref, nu_ref, xs_hbm, wg_hbm, wu_hbm, wd_hbm, ys_hbm,
                    xbuf, ybuf, wg_buf, wu_buf, wd_buf, wgu_b_ref, wd_b_ref, xsem, ysem, wsem):
    n = nu_ref[0]

    def rows(t):
        return pl.ds(pl.multiple_of(t * MOE_TM, MOE_TM), MOE_TM)

    def x_copy(t, s):
        return pltpu.make_async_copy(xs_hbm.at[rows(t)], xbuf.at[s], xsem.at[s])

    def y_copy(t, s):
        return pltpu.make_async_copy(ybuf.at[s], ys_hbm.at[rows(t)], ysem.at[s])

    def weight_copies(expert, s):
        return [pltpu.make_async_copy(wg_hbm.at[expert], wg_buf.at[s], wsem.at[s]),
                pltpu.make_async_copy(wu_hbm.at[expert], wu_buf.at[s], wsem.at[s]),
                pltpu.make_async_copy(wd_hbm.at[expert], wd_buf.at[s], wsem.at[s])]

    for cp in weight_copies(te_ref[0], 0):
        cp.start()
    for t in range(X_BUFS - 1):
        @pl.when(t < n)
        def _():
            x_copy(t, t).start()

    def tile(i, carry):
        xs_slot = lax.rem(i, X_BUFS)
        ys_slot = lax.rem(i, 2)
        w_slot = lax.rem(seg_ref[i], 2)
        x_copy(i, xs_slot).wait()
        ahead = i + (X_BUFS - 1)

        @pl.when(ahead < n)
        def _():
            x_copy(ahead, lax.rem(ahead, X_BUFS)).start()

        @pl.when((i == 0) | (te_ref[i] != te_ref[jnp.maximum(i - 1, 0)]))
        def _():
            for cp in weight_copies(te_ref[i], w_slot):
                cp.wait()

            @pl.when(nxt_ref[i] >= 0)
            def _():
                for cp in weight_copies(nxt_ref[i], 1 - w_slot):
                    cp.start()

            wgu_b_ref[:, 0:D_EXPERT] = wg_buf[w_slot].astype(BF16)
            wgu_b_ref[:, D_EXPERT:2 * D_EXPERT] = wu_buf[w_slot].astype(BF16)
            wd_b_ref[...] = wd_buf[w_slot].astype(BF16)

        @pl.when(i >= 2)
        def _():
            y_copy(i - 2, ys_slot).wait()

        h = jnp.dot(xbuf[xs_slot], wgu_b_ref[...], preferred_element_type=F32)
        act = (_silu(h[:, 0:D_EXPERT]) * h[:, D_EXPERT:2 * D_EXPERT]).astype(BF16)
        ybuf[ys_slot] = jnp.dot(act, wd_b_ref[...], preferred_element_type=F32).astype(BF16)
        y_copy(i, ys_slot).start()
        return carry

    lax.fori_loop(0, n, tile, 0)

    @pl.when(n >= 2)
    def _():
        y_copy(n - 2, lax.rem(n - 2, 2)).wait()

    y_copy(n - 1, lax.rem(n - 1, 2)).wait()


def _experts(tile_expert, tile_segment, next_expert, n_used, xs, w_gate, w_up, w_down):
    n_prefetch = 4
    anywhere = pl.BlockSpec(memory_space=pl.ANY)
    grid_spec = pltpu.PrefetchScalarGridSpec(
        num_scalar_prefetch=n_prefetch,
        grid=(1,),
        in_specs=[anywhere, anywhere, anywhere, anywhere],
        out_specs=anywhere,
        scratch_shapes=[pltpu.VMEM((X_BUFS, MOE_TM, D_MODEL), BF16), pltpu.VMEM((2, MOE_TM, D_MODEL), BF16),
                        pltpu.VMEM((2, D_MODEL, D_EXPERT), F32), pltpu.VMEM((2, D_MODEL, D_EXPERT), F32),
                        pltpu.VMEM((2, D_EXPERT, D_MODEL), F32),
                        pltpu.VMEM((D_MODEL, 2 * D_EXPERT), BF16), pltpu.VMEM((D_EXPERT, D_MODEL), BF16),
                        pltpu.SemaphoreType.DMA((X_BUFS,)), pltpu.SemaphoreType.DMA((2,)),
                        pltpu.SemaphoreType.DMA((2,))],
    )
    return pl.pallas_call(
        _experts_kernel,
        grid_spec=grid_spec,
        out_shape=jax.ShapeDtypeStruct((N_ROWS, D_MODEL), BF16),
        input_output_aliases={n_prefetch: 0},
        compiler_params=_cparams(("arbitrary",)),
        name="experts",
    )(tile_expert, tile_segment, next_expert, n_used, xs, w_gate, w_up, w_down)


def _combine_kernel(nb_ref, ns_ref, tab_ref, tab_next_ref, route_ref, x1_ref, g2_ref, ys_ref, o_ref,
                    gbuf_ref, sems):
    i = pl.program_id(0)
    last = pl.num_programs(0) - 1
    buf = lax.rem(i, 2)

    def piece(b):
        def make(local, sorted_row, rows):
            return pltpu.make_async_copy(
                ys_ref.at[pl.ds(pl.multiple_of(sorted_row, RUN_ALIGN), rows)],
                gbuf_ref.at[b, pl.ds(pl.multiple_of(local, RUN_ALIGN), rows)], sems.at[b])
        return make

    @pl.when(i == 0)
    def _():
        gbuf_ref[...] = jnp.zeros_like(gbuf_ref)
        _run_copies(tab_ref, nb_ref[0], ns_ref[0], piece(0), lambda cp: cp.start())

    nxt = jnp.minimum(i + 1, last)

    @pl.when(i < last)
    def _():
        _run_copies(tab_next_ref, nb_ref[nxt], ns_ref[nxt], piece(1 - buf), lambda cp: cp.start())

    rec = route_ref[...]
    slot0 = lax.broadcasted_iota(I32, (OUT_TM, COMBINE_K), 1).astype(F32)
    _run_copies(tab_ref, nb_ref[i], ns_ref[i], lambda lo, so, rows: piece(buf)(0, 0, rows), lambda cp: cp.wait())
    moe = jnp.zeros((OUT_TM, D_MODEL), F32)
    for k0 in range(0, LOCAL_ROWS, COMBINE_K):
        s0, s1 = rec[:, 0:1] - float(k0), rec[:, 1:2] - float(k0)
        weights = (jnp.where(slot0 == s0, rec[:, 2:3], 0.0)
                   + jnp.where(slot0 == s1, rec[:, 3:4], 0.0)).astype(BF16)
        moe = moe + jnp.dot(weights, gbuf_ref[buf, k0:k0 + COMBINE_K, :], preferred_element_type=F32)
    o_ref[...] = x1_ref[...] + g2_ref[0] * moe


def _combine(n_big, n_small, piece_table, route, x1, mod3, ys):
    tm = OUT_TM
    steps_per_batch = SEQ // tm
    grid_spec = pltpu.PrefetchScalarGridSpec(
        num_scalar_prefetch=2,
        grid=(N_TOKEN_TILES,),
        in_specs=[_piece_spec(lambda i, nb, ns: (i, 0, 0)),
                  _piece_spec(lambda i, nb, ns: (jnp.minimum(i + 1, N_TOKEN_TILES - 1), 0, 0)),
                  pl.BlockSpec((tm, ROUTE_W), lambda i, nb, ns: (i, 0)),
                  pl.BlockSpec((tm, D_MODEL), lambda i, nb, ns: (i, 0)),
                  pl.BlockSpec((1, 1, D_MODEL), lambda i, nb, ns: ((i // steps_per_batch) * 6 + 5, 0, 0)),
                  pl.BlockSpec(memory_space=pl.ANY)],
        out_specs=pl.BlockSpec((tm, D_MODEL), lambda i, nb, ns: (i, 0)),
        scratch_shapes=[pltpu.VMEM((2, LOCAL_ROWS, D_MODEL), BF16), pltpu.SemaphoreType.DMA((2,))],
    )
    return pl.pallas_call(
        _combine_kernel,
        grid_spec=grid_spec,
        out_shape=jax.ShapeDtypeStruct((TOKENS, D_MODEL), F32),
        compiler_params=_cparams(("arbitrary",)),
        name="combine",
    )(n_big, n_small, piece_table, piece_table, route, x1, mod3, ys)


def kernel(x, c, positions, norm1_w, norm2_w, w_ada, b_ada, w_in, conv_w, conv_b, dt_bias, a_log,
           d_skip, ssd_norm_w, q_norm_w, k_norm_w, sinks, w_out, w_group, b_group, w_expert, b_expert,
           w_gate, w_up, w_down):
    assert x.shape == (BATCH, SEQ, D_MODEL) and w_in.shape == (D_MODEL, IN_WIDTH)
    x2d = x.reshape(TOKENS, D_MODEL)
    cosf, s1, s2, mod = _rope_tables_and_mod(positions, c, w_ada, b_ada)
    mod3 = mod.reshape(BATCH * 6, 1, D_MODEL)

    qkv, z, xbc, dt, dt_t = _in_proj(x2d, norm1_w, mod3, w_in)
    att = _attention(qkv, (cosf, s1, s2), q_norm_w, k_norm_w, sinks)
    y = _ssd(xbc, z, dt, dt_t, conv_w, conv_b, dt_bias, a_log, d_skip, ssd_norm_w)

    w_router = jnp.pad(jnp.concatenate([w_group, w_expert], axis=1).astype(F32),
                       ((0, 0), (0, LANES - ROUTER_COLS)))
    b_router = jnp.pad(jnp.concatenate([b_group, b_expert]).astype(F32),
                       (0, LANES - ROUTER_COLS)).reshape(1, LANES)
    x1, h2, route, route_t, tcnt = _out_router(att, y, x2d, mod3, w_out.astype(F32), norm2_w,
                                                w_router, b_router)

    tc = tcnt[:, 0, 0:N_EXPERTS].astype(I32)
    run_rows = ((tc + RUN_ALIGN - 1) // RUN_ALIGN) * RUN_ALIGN
    counts = jnp.sum(run_rows, axis=0)
    padded = ((counts + MOE_TM - 1) // MOE_TM) * MOE_TM
    seg_end = jnp.cumsum(padded)
    seg_start = seg_end - padded
    run_dst = seg_start[None, :] + jnp.cumsum(run_rows, axis=0) - run_rows
    n_used = (seg_end[-1] // MOE_TM).reshape(1)
    last_row = jnp.minimum(jnp.arange(N_TILES, dtype=I32) * MOE_TM, seg_end[-1] - 1)
    tile_expert = jnp.sum((seg_end[None, :] <= last_row[:, None]).astype(I32), axis=1)

    run_local = jnp.cumsum(run_rows, axis=1) - run_rows
    n_big_run = run_rows // BIG_PIECE
    n_small_run = (run_rows // RUN_ALIGN) % 2
    q = jnp.arange(PIECE_SLOTS, dtype=I32)
    experts = jnp.arange(N_EXPERTS, dtype=I32)

    def flat(per_run, local0, dst0, stride):
        end = jnp.cumsum(per_run, axis=1)
        run_of = jnp.sum((end[:, None, :] <= q[None, :, None]).astype(I32), axis=2)
        pick = (run_of[:, :, None] == experts[None, None, :]).astype(I32)
        k = q[None, :] - jnp.sum(pick * (end - per_run)[:, None, :], axis=2)
        local = jnp.sum(pick * local0[:, None, :], axis=2) + stride * k
        dst = jnp.sum(pick * dst0[:, None, :], axis=2) + stride * k
        return end[:, -1], local, dst

    n_big, big_local, big_dst = flat(n_big_run, run_local, run_dst, BIG_PIECE)
    n_small, small_local, small_dst = flat(n_small_run, run_local + BIG_PIECE * n_big_run,
                                           run_dst + BIG_PIECE * n_big_run, 0)
    piece_table = jnp.concatenate([big_local, big_dst, small_local, small_dst], axis=1)
    piece_table = piece_table.astype(I32).reshape(N_TOKEN_TILES, 1, TABLE_W)

    nonempty = padded > 0
    seg_rank = jnp.cumsum(nonempty.astype(I32)) - 1
    later = nonempty[None, :] & (experts[None, :] > experts[:, None])
    next_of = jnp.min(jnp.where(later, experts[None, :], N_EXPERTS), axis=1)
    next_of = jnp.where(next_of == N_EXPERTS, -1, next_of)
    tile_is = (tile_expert[:, None] == experts[None, :]).astype(I32)
    tile_segment = jnp.sum(tile_is * seg_rank[None, :], axis=1)
    next_expert = jnp.sum(tile_is * next_of[None, :], axis=1)

    n_big, n_small = n_big.astype(I32), n_small.astype(I32)
    xs = _dispatch(seg_end.astype(I32), (seg_start + counts).astype(I32), n_big, n_small, piece_table,
                   route_t, h2)
    ys = _experts(tile_expert, tile_segment.astype(I32), next_expert.astype(I32), n_used.astype(I32),
                  xs, w_gate, w_up, w_down)
    out = _combine(n_big, n_small, piece_table, route, x1, mod3, ys)
    return out.reshape(BATCH, SEQ, D_MODEL)
```

```python
import jax
import jax.numpy as jnp
from jax import lax
from jax.experimental import pallas as pl
from jax.experimental.pallas import tpu as pltpu

F32 = jnp.float32
BF16 = jnp.bfloat16
I32 = jnp.int32

D_MODEL = 1024
BATCH = 2
SEQ = 8192
TOKENS = BATCH * SEQ
ATT_HEADS = 8
ATT_KV_HEADS = 2
HEAD_DIM = 64
ATT_WIDTH = ATT_HEADS * HEAD_DIM
KV_WIDTH = ATT_KV_HEADS * HEAD_DIM
ATT_BLOCK = 128
ROPE_DIM = HEAD_DIM // 4
ROPE_THETA = 500000.0
SSD_HEADS = 8
SSD_HEAD_DIM = 64
SSD_WIDTH = SSD_HEADS * SSD_HEAD_DIM
SSD_GROUPS = 2
SSD_STATE = 128
CONV_K = 4
CHUNK = 128
XBC_WIDTH = SSD_WIDTH + 2 * SSD_GROUPS * SSD_STATE
IN_WIDTH = ATT_WIDTH + 2 * KV_WIDTH + SSD_WIDTH + XBC_WIDTH + SSD_HEADS
N_GROUPS = 4
EXPERTS_PER_GROUP = 8
N_EXPERTS = N_GROUPS * EXPERTS_PER_GROUP
TOP_K = 2
D_EXPERT = 256
EPS = 1e-6

LANES = 128
QKV_WIDTH = ATT_WIDTH + 2 * KV_WIDTH
IN_PAD = QKV_WIDTH + SSD_WIDTH + XBC_WIDTH + LANES
NEG_BIG = -1e30

VMEM_LIMIT = 48 * 1024 * 1024


def _cparams(sem):
    return pltpu.CompilerParams(dimension_semantics=sem, vmem_limit_bytes=VMEM_LIMIT)


def _split_bf16(x):
    hi = x.astype(BF16)
    lo = (x - hi.astype(F32)).astype(BF16)
    return hi, lo


ADA_TN = 768


def _ada_kernel(ct_ref, w_ref, b_ref, o_ref):
    ct = ct_ref[...]
    s = ct * jax.nn.sigmoid(ct)
    w = w_ref[...]
    rows = [jnp.sum(s[:, b:b + 1] * w, axis=0, keepdims=True) for b in range(BATCH)]
    o_ref[...] = jnp.concatenate(rows, axis=0) + b_ref[...]


INPROJ_TM = 512
_INPROJ_CHUNK = 256


def _inproj_kernel(x_ref, nw_ref, sc_ref, sh_ref, wf_hbm, wdt_ref, qkv_ref, z_ref, xbc_ref, dt_ref, dtt_ref,
                   w_ref, stage_ref, wsem):
    @pl.when(pl.program_id(0) == 0)
    def _():
        chunks = list(range(0, IN_PAD - LANES, _INPROJ_CHUNK))

        def fetch(k):
            return pltpu.make_async_copy(wf_hbm.at[:, pl.ds(chunks[k], _INPROJ_CHUNK)], stage_ref.at[k % 2],
                                         wsem.at[k % 2])

        fetch(0).start()
        for k, c0 in enumerate(chunks):
            if k + 1 < len(chunks):
                fetch(k + 1).start()
            fetch(k).wait()
            w_ref[:, c0:c0 + _INPROJ_CHUNK] = stage_ref[k % 2].astype(BF16)
        w_ref[:, IN_PAD - LANES:IN_PAD] = wdt_ref[...].astype(BF16)

    x = x_ref[...]
    y = x * lax.rsqrt(jnp.mean(x * x, axis=-1, keepdims=True) + EPS)
    h = (y * nw_ref[...]) * (1.0 + sc_ref[0]) + sh_ref[0]
    hb = h.astype(BF16)

    def proj(c0, c1):
        return jnp.dot(hb, w_ref[:, c0:c1], preferred_element_type=F32)

    for c0 in range(0, QKV_WIDTH, _INPROJ_CHUNK):
        qkv_ref[:, c0:c0 + _INPROJ_CHUNK] = proj(c0, c0 + _INPROJ_CHUNK).astype(BF16)
    base = QKV_WIDTH
    for c0 in range(0, SSD_WIDTH, _INPROJ_CHUNK):
        z_ref[:, c0:c0 + _INPROJ_CHUNK] = proj(base + c0, base + c0 + _INPROJ_CHUNK).astype(BF16)
    base += SSD_WIDTH
    for c0 in range(0, XBC_WIDTH, _INPROJ_CHUNK):
        xbc_ref[:, c0:c0 + _INPROJ_CHUNK] = proj(base + c0, base + c0 + _INPROJ_CHUNK).astype(BF16)
    base += XBC_WIDTH
    dt = proj(base, base + LANES)
    dt_ref[...] = dt
    dtt_ref[...] = dt.T[0:SSD_HEADS, :]


def _in_proj(x2d, norm_w, mod3, w_in):
    tm = INPROJ_TM
    steps_per_batch = SEQ // tm
    w_dt = jnp.pad(w_in[:, IN_WIDTH - SSD_HEADS:IN_WIDTH].astype(F32), ((0, 0), (0, LANES - SSD_HEADS)))
    return pl.pallas_call(
        _inproj_kernel,
        grid=(TOKENS // tm,),
        in_specs=[pl.BlockSpec((tm, D_MODEL), lambda i: (i, 0)),
                  pl.BlockSpec((1, D_MODEL), lambda i: (0, 0)),
                  pl.BlockSpec((1, 1, D_MODEL), lambda i: ((i // steps_per_batch) * 6 + 1, 0, 0)),
                  pl.BlockSpec((1, 1, D_MODEL), lambda i: ((i // steps_per_batch) * 6 + 0, 0, 0)),
                  pl.BlockSpec(memory_space=pl.ANY),
                  pl.BlockSpec((D_MODEL, LANES), lambda i: (0, 0))],
        out_specs=[pl.BlockSpec((tm, QKV_WIDTH), lambda i: (i, 0)),
                   pl.BlockSpec((tm, SSD_WIDTH), lambda i: (i, 0)),
                   pl.BlockSpec((tm, XBC_WIDTH), lambda i: (i, 0)),
                   pl.BlockSpec((tm, LANES), lambda i: (i, 0)),
                   pl.BlockSpec((SSD_HEADS, tm), lambda i: (0, i))],
        out_shape=[jax.ShapeDtypeStruct((TOKENS, QKV_WIDTH), BF16),
                   jax.ShapeDtypeStruct((TOKENS, SSD_WIDTH), BF16),
                   jax.ShapeDtypeStruct((TOKENS, XBC_WIDTH), BF16),
                   jax.ShapeDtypeStruct((TOKENS, LANES), F32),
                   jax.ShapeDtypeStruct((SSD_HEADS, TOKENS), F32)],
        scratch_shapes=[pltpu.VMEM((D_MODEL, IN_PAD), BF16), pltpu.VMEM((2, D_MODEL, _INPROJ_CHUNK), F32),
                        pltpu.SemaphoreType.DMA((2,))],
        compiler_params=_cparams(("arbitrary",)),
        name="in_proj",
    )(x2d, norm_w.reshape(1, D_MODEL), mod3, mod3, w_in.astype(F32), w_dt)


ATT_SUB = 8


ROPE_TM = 2048
_ROPE_HALF = ROPE_DIM // 2
_TOK_PER_ROW = LANES // _ROPE_HALF


def _exact_dot(x, onehot_b):
    hi, lo = _split_bf16(x)
    return (jnp.dot(hi, onehot_b, preferred_element_type=F32)
            + jnp.dot(lo, onehot_b, preferred_element_type=F32))


def _rope_kernel(pos_ref, freq_ref, sel_ref, own_ref, gcos_ref, gs1_ref, gs2_ref, ident_ref,
                 cos_ref, s1_ref, s2_ref):
    ang = pos_ref[...].astype(F32) * freq_ref[...]
    cos_p, sin_p = jnp.cos(ang), jnp.sin(ang)
    hi_c, lo_c = _split_bf16(cos_p)
    hi_s, lo_s = _split_bf16(sin_p)
    sel = sel_ref[...]
    rows_c = jnp.dot(sel, hi_c, preferred_element_type=F32) + jnp.dot(sel, lo_c, preferred_element_type=F32)
    rows_s = jnp.dot(sel, hi_s, preferred_element_type=F32) + jnp.dot(sel, lo_s, preferred_element_type=F32)
    own = own_ref[...]
    cos_ref[...] = _exact_dot(rows_c * own, gcos_ref[...]) + ident_ref[...]
    s1_ref[...] = _exact_dot(rows_s * own, gs1_ref[...])
    s2_ref[...] = _exact_dot(rows_s * own, gs2_ref[...])


def _tables_kernel(pos_ref, freq_ref, sel_ref, own_ref, gcos_ref, gs1_ref, gs2_ref, ident_ref,
                   ct_ref, w_ref, b_ref, cos_ref, s1_ref, s2_ref, mod_ref):
    _rope_kernel(pos_ref, freq_ref, sel_ref, own_ref, gcos_ref, gs1_ref, gs2_ref, ident_ref,
                 cos_ref, s1_ref, s2_ref)
    _ada_kernel(ct_ref, w_ref, b_ref, mod_ref)


def _rope_tables_and_mod(positions, c, w_ada, b_ada):
    n_mod = w_ada.shape[1]
    assert TOKENS // ROPE_TM == n_mod // ADA_TN
    half, per_row = _ROPE_HALF, _TOK_PER_ROW
    rows = ROPE_TM // per_row
    pos_rep = jnp.repeat(positions.reshape(TOKENS).astype(I32), half).reshape(TOKENS // per_row, LANES)
    inv_freq = jnp.power(ROPE_THETA, -jnp.arange(half, dtype=F32) * 2.0 / ROPE_DIM)
    freq = jnp.tile(inv_freq, per_row).reshape(1, LANES)
    tok = jnp.arange(ROPE_TM)
    lane = jnp.arange(LANES)
    sel = (tok[:, None] // per_row == jnp.arange(rows)[None, :]).astype(BF16)
    own = (lane[None, :] // half == tok[:, None] % per_row).astype(F32)
    d = lane % HEAD_DIM
    src_f = lane % half
    hits = lambda lo, hi: ((src_f[:, None] == d[None, :] % half) & (d[None, :] >= lo) & (d[None, :] < hi))
    gcos = hits(0, ROPE_DIM).astype(BF16)
    gs1 = -hits(0, half).astype(BF16)
    gs2 = hits(half, ROPE_DIM).astype(BF16)
    ident = (d >= ROPE_DIM).astype(F32).reshape(1, LANES)
    const = lambda shape: pl.BlockSpec(shape, lambda i: (0, 0))
    out_spec = pl.BlockSpec((ROPE_TM, LANES), lambda i: (i, 0))
    out = jax.ShapeDtypeStruct((TOKENS, LANES), F32)
    return pl.pallas_call(
        _tables_kernel,
        grid=(TOKENS // ROPE_TM,),
        in_specs=[pl.BlockSpec((rows, LANES), lambda i: (i, 0)), const((1, LANES)),
                  const((ROPE_TM, rows)), const((ROPE_TM, LANES)),
                  const((LANES, LANES)), const((LANES, LANES)), const((LANES, LANES)), const((1, LANES)),
                  const((D_MODEL, BATCH)),
                  pl.BlockSpec((D_MODEL, ADA_TN), lambda i: (0, i)),
                  pl.BlockSpec((1, ADA_TN), lambda i: (0, i))],
        out_specs=[out_spec, out_spec, out_spec, pl.BlockSpec((BATCH, ADA_TN), lambda i: (0, i))],
        out_shape=[out, out, out, jax.ShapeDtypeStruct((BATCH, n_mod), F32)],
        compiler_params=_cparams(("arbitrary",)),
        name="rope_tables_ada_mod",
    )(pos_rep, freq, sel, own, gcos, gs1, gs2, ident, c.T, w_ada, b_ada.reshape(1, n_mod))


def _seg_meansq(xf, ones128):
    rows, width = xf.shape
    nt = width // LANES
    parts = _split_bf16(xf * xf)
    stacked = jnp.concatenate([p[:, t * LANES:(t + 1) * LANES] for p in parts for t in range(nt)], axis=0)
    tot = jnp.dot(stacked, ones128, preferred_element_type=F32)
    tiles = [tot[t * rows:(t + 1) * rows] + tot[(nt + t) * rows:(nt + t + 1) * rows] for t in range(nt)]
    return jnp.concatenate(tiles, axis=1) * (1.0 / HEAD_DIM)


def _norm_rope(x_bf, w_row, ones_bd, cosf, s1, s2):
    xf = x_bf.astype(F32)
    width = xf.shape[1]
    xn = xf * lax.rsqrt(_seg_meansq(xf, ones_bd) + EPS) * w_row
    half = ROPE_DIM // 2
    up = pltpu.roll(xn, width - half, axis=1)
    down = pltpu.roll(xn, half, axis=1)
    return xn * cosf + up * s1 + down * s2


def _attn_kernel(sink_ref, q_ref, kv_ref, cos_ref, s1_ref, s2_ref, qw_ref, kw_ref,
                 ones_ref, o_ref, kprev_ref, vprev_ref):
    j = pl.program_id(1)
    blk = ATT_BLOCK

    @pl.when(j == 0)
    def _():
        kprev_ref[...] = jnp.zeros_like(kprev_ref)
        vprev_ref[...] = jnp.zeros_like(vprev_ref)

    cos1 = cos_ref[...]
    s1_1 = s1_ref[...]
    s2_1 = s2_ref[...]
    reps = ATT_WIDTH // LANES
    cosq = jnp.concatenate([cos1] * reps, axis=1)
    s1q = jnp.concatenate([s1_1] * reps, axis=1)
    s2q = jnp.concatenate([s2_1] * reps, axis=1)

    q = _norm_rope(q_ref[...], qw_ref[...], ones_ref[...], cosq, s1q, s2q)
    qf = q * (HEAD_DIM ** -0.5)
    kv = kv_ref[...]
    kn = _norm_rope(kv[:, 0:KV_WIDTH], kw_ref[...], ones_ref[...], cos1, s1_1, s2_1)
    vn = kv[:, KV_WIDTH:2 * KV_WIDTH].astype(F32)

    kall = jnp.concatenate([kprev_ref[...], kn], axis=0)
    vall = jnp.concatenate([vprev_ref[...], vn], axis=0)
    kprev_ref[...] = kn[(ATT_SUB - 1) * blk:ATT_SUB * blk]
    vprev_ref[...] = vn[(ATT_SUB - 1) * blk:ATT_SUB * blk]

    lo_all = lax.broadcasted_iota(I32, kall.shape, 1) < HEAD_DIM
    ones_all = jnp.ones(kall.shape, BF16)

    row = lax.broadcasted_iota(I32, (2 * blk, blk), 0)
    col = lax.broadcasted_iota(I32, (2 * blk, blk), 1)
    from_prev = col > (row & (blk - 1))
    second_tile = lax.broadcasted_iota(I32, (2 * blk, 1), 0) >= blk
    zero_p = jnp.zeros((2 * blk, blk), F32)

    k_par, v_par = [], []
    for g in range(ATT_KV_HEADS):
        keep = lo_all if g == 0 else ~lo_all
        k_own = jnp.where(keep, kall, 0.0)
        v_own = jnp.where(keep, vall, 0.0)
        k_oth = pltpu.roll(k_own, HEAD_DIM, axis=1)
        v_oth = pltpu.roll(v_own, HEAD_DIM, axis=1)
        k_lo, k_hi = (k_own, k_oth) if g == 0 else (k_oth, k_own)
        v_lo, v_hi = (v_own, v_oth) if g == 0 else (v_oth, v_own)
        k_par.append((k_lo.astype(BF16), k_hi.astype(BF16)))
        v_par.append((jnp.concatenate([v_lo.astype(BF16), ones_all], axis=1),
                      jnp.concatenate([v_hi.astype(BF16), ones_all], axis=1)))

    problems = [(g, sub) for g in range(ATT_KV_HEADS) for sub in range(ATT_SUB)]
    scores = []
    for g, sub in problems:
        r0, c0 = sub * blk, g * 2 * LANES
        qcat = jnp.concatenate([qf[r0:r0 + blk, c0:c0 + LANES],
                                qf[r0:r0 + blk, c0 + LANES:c0 + 2 * LANES]], axis=0).astype(BF16)
        kw = jnp.concatenate([k_par[g][0][r0:r0 + 2 * blk], k_par[g][1][r0:r0 + 2 * blk]], axis=0)
        scores.append(lax.dot_general(qcat, kw, (((1,), (1,)), ((), ())),
                                      preferred_element_type=F32))

    weights, rescale = [], []
    for (g, sub), s_all in zip(problems, scores):
        for par in range(2):
            s = s_all[:, par * 2 * blk:(par + 1) * 2 * blk]
            s_prev = s[:, 0:blk]
            if sub == 0:
                s_prev = s_prev + jnp.where(j > 0, 0.0, NEG_BIG)
            s = jnp.where(from_prev, s_prev, s[:, blk:2 * blk])
            h_first = ATT_HEADS // ATT_KV_HEADS * g + par
            sink = jnp.where(second_tile, sink_ref[h_first + 2], sink_ref[h_first])
            m = jnp.maximum(jnp.max(s, axis=-1, keepdims=True), sink)
            p = jnp.exp(s - m)
            weights.append(jnp.concatenate([jnp.where(from_prev, p, zero_p), jnp.where(from_prev, zero_p, p)],
                                           axis=1).astype(BF16))
            rescale.append(jnp.exp(sink - m))

    outs = []
    for idx, (g, sub) in enumerate(problems):
        for par in range(2):
            outs.append(jnp.dot(weights[2 * idx + par], v_par[g][par][sub * blk:(sub + 2) * blk],
                                preferred_element_type=F32))

    for idx, (g, sub) in enumerate(problems):
        r0, c0 = sub * blk, g * 2 * LANES
        pair = None
        for par in range(2):
            o = outs[2 * idx + par]
            part = o[:, 0:LANES] * (1.0 / (o[:, LANES:2 * LANES] + rescale[2 * idx + par]))
            pair = part if pair is None else pair + part
        o_ref[r0:r0 + blk, c0:c0 + LANES] = pair[0:blk].astype(BF16)
        o_ref[r0:r0 + blk, c0 + LANES:c0 + 2 * LANES] = pair[blk:2 * blk].astype(BF16)


def _attention(qkv, rope_tables, q_norm_w, k_norm_w, sinks):
    cosf, s1, s2 = rope_tables
    qw = jnp.tile(q_norm_w.astype(F32), ATT_HEADS).reshape(1, ATT_WIDTH)
    kw = jnp.tile(k_norm_w.astype(F32), ATT_KV_HEADS).reshape(1, KV_WIDTH)
    seg = jnp.arange(LANES) // HEAD_DIM
    ones128 = (seg[:, None] == seg[None, :]).astype(BF16)
    const = lambda shape: pl.BlockSpec(shape, lambda b, j, s: (0, 0))
    rows = ATT_SUB * ATT_BLOCK
    nb = SEQ // rows
    tok = lambda width, cb: pl.BlockSpec((rows, width), lambda b, j, s: (b * nb + j, cb))
    grid_spec = pltpu.PrefetchScalarGridSpec(
        num_scalar_prefetch=1,
        grid=(BATCH, nb),
        in_specs=[tok(ATT_WIDTH, 0), tok(2 * KV_WIDTH, 2), tok(LANES, 0), tok(LANES, 0), tok(LANES, 0),
                  const((1, ATT_WIDTH)), const((1, KV_WIDTH)), const((LANES, LANES))],
        out_specs=tok(ATT_WIDTH, 0),
        scratch_shapes=[pltpu.VMEM((ATT_BLOCK, KV_WIDTH), F32),
                        pltpu.VMEM((ATT_BLOCK, KV_WIDTH), F32)],
    )
    return pl.pallas_call(
        _attn_kernel,
        grid_spec=grid_spec,
        out_shape=jax.ShapeDtypeStruct((TOKENS, ATT_WIDTH), BF16),
        compiler_params=_cparams(("arbitrary", "arbitrary")),
        name="attention",
    )(sinks.astype(F32), qkv, qkv, cosf, s1, s2, qw, kw, ones128)


SSD_SUB = 4


def _softplus(x):
    return jnp.maximum(x, 0.0) + jnp.log1p(jnp.exp(-jnp.abs(x)))


def _silu(x):
    h = 0.5 * x
    return h + h * jnp.tanh(h)


def _ssd_kernel(xbc_ref, z_ref, dt_ref, dtt_ref, cw_ref, cb_ref, dtb_row_ref, dtb_col_ref,
                alog_row_ref, alog_col_ref, dskip_ref, nw_ref, tril_ref, triu_ref,
                o_ref, conv_ref, state_ref):
    c = pl.program_id(1)
    L = CHUNK
    tail = 8

    @pl.when(c == 0)
    def _():
        conv_ref[0:tail, :] = jnp.zeros((tail, XBC_WIDTH), F32)
        state_ref[...] = jnp.zeros_like(state_ref)

    row = lax.broadcasted_iota(I32, (L, L), 0)
    col = lax.broadcasted_iota(I32, (L, L), 1)
    causal = col <= row
    lane = lax.broadcasted_iota(I32, (L, LANES), 1)
    lo_half = lane < SSD_HEAD_DIM

    prepared = [_ssd_prepare(s * L, xbc_ref, dt_ref, dtt_ref, cw_ref, cb_ref, dtb_row_ref, dtb_col_ref,
                             alog_row_ref, alog_col_ref, tril_ref, triu_ref, conv_ref)
                for s in range(SSD_SUB)]
    for s in range(SSD_SUB):
        _ssd_chunk(s * L, prepared[s], causal, lo_half, z_ref, dskip_ref, nw_ref, o_ref, state_ref)


def _ssd_prepare(r0, xbc_ref, dt_ref, dtt_ref, cw_ref, cb_ref, dtb_row_ref, dtb_col_ref,
                 alog_row_ref, alog_col_ref, tril_ref, triu_ref, conv_ref):
    L = CHUNK
    tail = 8
    xb = xbc_ref[r0:r0 + L, :].astype(F32)
    conv_ref[tail:tail + L, :] = xb
    acc = cb_ref[...] + cw_ref[CONV_K - 1:CONV_K, :] * xb
    for k in range(CONV_K - 1):
        off = tail - (CONV_K - 1) + k
        acc = acc + cw_ref[k:k + 1, :] * conv_ref[off:off + L, :]
    conv_ref[0:tail, :] = xb[L - tail:L, :]
    u = _silu(acc)
    xs = u[:, 0:SSD_WIDTH]
    bmat = u[:, SSD_WIDTH:SSD_WIDTH + SSD_GROUPS * SSD_STATE]
    cmat = u[:, SSD_WIDTH + SSD_GROUPS * SSD_STATE:XBC_WIDTH]

    dt = _softplus(dt_ref[r0:r0 + L, :] + dtb_row_ref[...])
    a = dt * (-jnp.exp(alog_row_ref[...]))
    a_hi, a_lo = _split_bf16(a)
    a_cum = (jnp.dot(tril_ref[...], a_hi, preferred_element_type=F32)
             + jnp.dot(tril_ref[...], a_lo, preferred_element_type=F32))
    dt_t = _softplus(dtt_ref[:, r0:r0 + L] + dtb_col_ref[...])
    a_t = dt_t * (-jnp.exp(alog_col_ref[...]))
    at_hi, at_lo = _split_bf16(a_t)
    a_cum_t = (jnp.dot(at_hi, triu_ref[...], preferred_element_type=F32)
               + jnp.dot(at_lo, triu_ref[...], preferred_element_type=F32))
    a_end_t = a_cum_t[:, L - 1:L]
    return dict(
        xs=xs, bmat=bmat, cmat=cmat, a_cum=a_cum, exp_a_cum=jnp.exp(a_cum),
        shifted_t=a_cum_t - jnp.log(dt_t),
        wst_t=jnp.exp(a_end_t - a_cum_t) * dt_t,
        cdec_t=jnp.exp(a_end_t))


def _ssd_chunk(r0, p, causal, lo_half, z_ref, dskip_ref, nw_ref, o_ref, state_ref):
    L = CHUNK
    xs, bmat, cmat, a_cum, exp_a_cum = p["xs"], p["bmat"], p["cmat"], p["a_cum"], p["exp_a_cum"]
    shifted_t, wst_t, cdec_t = p["shifted_t"], p["wst_t"], p["cdec_t"]
    xs_b = xs.astype(BF16)
    heads_per_group = SSD_HEADS // SSD_GROUPS
    gated = []
    for g in range(SSD_GROUPS):
        b_g = bmat[:, g * SSD_STATE:(g + 1) * SSD_STATE]
        c_g = cmat[:, g * SSD_STATE:(g + 1) * SSD_STATE]
        cb = lax.dot_general(c_g.astype(BF16), b_g.astype(BF16), (((1,), (1,)), ((), ())),
                             preferred_element_type=F32)
        b_gt = b_g.T
        for t in range(heads_per_group // 2):
            tile = g * (heads_per_group // 2) + t
            c0 = tile * LANES
            xs_tile = xs_b[:, c0:c0 + LANES]
            st_tile = state_ref[:, c0:c0 + LANES]
            st_b = st_tile.astype(BF16)
            y_tile = jnp.zeros((L, LANES), F32)
            new_tile = jnp.zeros((SSD_STATE, LANES), F32)
            for e in range(2):
                h = 2 * tile + e
                keep = lo_half if e == 0 else ~lo_half
                colb = jnp.broadcast_to(a_cum[:, h:h + 1], (L, L))
                rowb = shifted_t[h:h + 1, :]
                w_in = cb * jnp.exp(jnp.where(causal, colb - rowb, NEG_BIG))
                w_off = c_g * jnp.broadcast_to(exp_a_cum[:, h:h + 1], (L, L))
                lhs = jnp.concatenate([w_in, w_off], axis=1).astype(BF16)
                rhs = jnp.concatenate([jnp.where(keep, xs_tile, jnp.zeros_like(xs_tile)),
                                       jnp.where(keep, st_b, jnp.zeros_like(st_b))], axis=0)
                y_tile = y_tile + jnp.dot(lhs, rhs, preferred_element_type=F32)
                m_h = (b_gt * wst_t[h:h + 1, :]).astype(BF16)
                new_tile = new_tile + jnp.dot(m_h, jnp.where(keep, xs_tile, jnp.zeros_like(xs_tile)),
                                              preferred_element_type=F32)
            cd = jnp.where(lo_half[0:1, :], cdec_t[2 * tile:2 * tile + 1, :],
                           cdec_t[2 * tile + 1:2 * tile + 2, :])
            state_ref[:, c0:c0 + LANES] = st_tile * cd + new_tile
            y_full = y_tile + dskip_ref[:, c0:c0 + LANES] * xs[:, c0:c0 + LANES]
            gated.append(y_full * _silu(z_ref[r0:r0 + L, c0:c0 + LANES].astype(F32)))

    gw = SSD_WIDTH // SSD_GROUPS
    tiles_per_group = gw // LANES
    for g in range(SSD_GROUPS):
        yg = jnp.concatenate(gated[g * tiles_per_group:(g + 1) * tiles_per_group], axis=1)
        ms = jnp.mean(yg * yg, axis=-1, keepdims=True)
        o_ref[r0:r0 + L, g * gw:(g + 1) * gw] = (
            (yg * lax.rsqrt(ms + EPS)) * nw_ref[:, g * gw:(g + 1) * gw]).astype(o_ref.dtype)


def _ssd(xbc, z, dt, dt_t, conv_w, conv_b, dt_bias, a_log, d_skip, ssd_norm_w):
    L = SSD_SUB * CHUNK
    nc = SEQ // L
    pad_row = lambda v: jnp.pad(v.astype(F32), (0, LANES - SSD_HEADS)).reshape(1, LANES)
    col8 = lambda v: v.astype(F32).reshape(SSD_HEADS, 1)
    idx = jnp.arange(CHUNK)
    tril = (idx[None, :] <= idx[:, None]).astype(BF16)
    triu = (idx[:, None] <= idx[None, :]).astype(BF16)
    dskip = jnp.repeat(d_skip.astype(F32), SSD_HEAD_DIM).reshape(1, SSD_WIDTH)
    const = lambda shape: pl.BlockSpec(shape, lambda b, c: (0, 0))
    tok = lambda width: pl.BlockSpec((L, width), lambda b, c: (b * nc + c, 0))
    return pl.pallas_call(
        _ssd_kernel,
        grid=(BATCH, nc),
        in_specs=[tok(XBC_WIDTH), tok(SSD_WIDTH), tok(LANES),
                  pl.BlockSpec((SSD_HEADS, L), lambda b, c: (0, b * nc + c)),
                  const((CONV_K, XBC_WIDTH)), const((1, XBC_WIDTH)),
                  const((1, LANES)), const((SSD_HEADS, 1)), const((1, LANES)), const((SSD_HEADS, 1)),
                  const((1, SSD_WIDTH)), const((1, SSD_WIDTH)), const((CHUNK, CHUNK)), const((CHUNK, CHUNK))],
        out_specs=tok(SSD_WIDTH),
        out_shape=jax.ShapeDtypeStruct((TOKENS, SSD_WIDTH), BF16),
        scratch_shapes=[pltpu.VMEM((8 + CHUNK, XBC_WIDTH), F32),
                        pltpu.VMEM((SSD_STATE, SSD_WIDTH), F32)],
        compiler_params=_cparams(("arbitrary", "arbitrary")),
        name="ssd",
    )(xbc, z, dt, dt_t, conv_w.astype(F32), conv_b.astype(F32).reshape(1, XBC_WIDTH),
      pad_row(dt_bias), col8(dt_bias), pad_row(a_log), col8(a_log), dskip,
      ssd_norm_w.astype(F32).reshape(1, SSD_WIDTH), tril, triu)


OUT_TM = 512
ROUTE_W = 8
ROUTER_COLS = N_GROUPS + N_EXPERTS
RUN_ALIGN = 16
RUN_SHIFT = 4
LOCAL_ROWS = 1536
assert RUN_ALIGN == 1 << RUN_SHIFT and LOCAL_ROWS >= TOP_K * OUT_TM + N_EXPERTS * (RUN_ALIGN - 1)


def _lane_pick(values, lane, index):
    return jnp.sum(jnp.where(lane == index, values, 0.0), axis=-1, keepdims=True)


def _first_argmax(vals, lane):
    m = jnp.max(vals, axis=-1, keepdims=True)
    idx = jnp.min(jnp.where(vals == m, lane, float(LANES)), axis=-1, keepdims=True)
    return m, idx


def _out_router_kernel(att_ref, y_ref, x_ref, g1_ref, wof_ref, nw_ref, sc_ref, sh_ref, wr_ref, br_ref,
                       ltri_ref, sut_ref, x1_ref, h2_ref, route_ref, routet_ref, tcnt_ref,
                       wr_split_ref, logits_ref, wo_ref):
    i = pl.program_id(0)

    @pl.when(i == 0)
    def _():
        hi, lo = _split_bf16(wr_ref[...])
        wr_split_ref[:, 0:LANES] = hi
        wr_split_ref[:, LANES:2 * LANES] = lo
        logits_ref[...] = jnp.zeros_like(logits_ref)
        for r0 in range(0, D_MODEL, 256):
            wo_ref[r0:r0 + 256, :] = wof_ref[r0:r0 + 256, :].astype(BF16)

    logits = logits_ref[...]

    mixer = (jnp.dot(att_ref[...], wo_ref[0:ATT_WIDTH, :], preferred_element_type=F32)
             + jnp.dot(y_ref[...], wo_ref[ATT_WIDTH:ATT_WIDTH + SSD_WIDTH, :], preferred_element_type=F32))
    x1 = x_ref[...] + g1_ref[0] * mixer
    x1_ref[...] = x1
    yn = x1 * lax.rsqrt(jnp.mean(x1 * x1, axis=-1, keepdims=True) + EPS)
    h2 = (yn * nw_ref[...]) * (1.0 + sc_ref[0]) + sh_ref[0]
    h2_ref[...] = h2.astype(BF16)

    h_hi, h_lo = _split_bf16(h2)
    both = jnp.dot(h_hi, wr_split_ref[...], preferred_element_type=F32)
    logits_ref[...] = (both[:, 0:LANES] + both[:, LANES:2 * LANES]
                       + jnp.dot(h_lo, wr_split_ref[:, 0:LANES], preferred_element_type=F32)) + br_ref[...]

    tm = logits.shape[0]
    lane = lax.broadcasted_iota(I32, (tm, LANES), 1).astype(F32)

    gl = jnp.where(lane < N_GROUPS, logits, NEG_BIG)
    gmax, gidx = _first_argmax(gl, lane)
    g_p = 1.0 / jnp.sum(jnp.exp(gl - gmax), axis=-1, keepdims=True)

    lo_lane = N_GROUPS + EXPERTS_PER_GROUP * gidx
    el = jnp.where((lane >= lo_lane) & (lane < lo_lane + EXPERTS_PER_GROUP), logits, NEG_BIG)
    m1, i1 = _first_argmax(el, lane)
    m2, i2 = _first_argmax(jnp.where(lane == i1, NEG_BIG, el), lane)
    r = jnp.exp(m2 - m1)
    p1 = 1.0 / (1.0 + r)
    p2 = r / (1.0 + r)
    e0 = i1 - N_GROUPS
    e1 = i2 - N_GROUPS

    onehot = ((lane == e0) | (lane == e1)).astype(F32)
    tile_cnt = jnp.sum(onehot, axis=0, keepdims=True)
    run_len = jnp.floor((tile_cnt + (RUN_ALIGN - 1)) * (1.0 / RUN_ALIGN)) * RUN_ALIGN
    run_start = jnp.dot(jnp.broadcast_to(run_len, (8, LANES)).astype(BF16), sut_ref[...],
                        preferred_element_type=F32)[0:1, :]
    before = jnp.dot(ltri_ref[...], onehot.astype(BF16), preferred_element_type=F32) + run_start
    slot0 = _lane_pick(before, lane, e0)
    slot1 = _lane_pick(before, lane, e1)
    tcnt_ref[0] = tile_cnt

    rec = jnp.zeros((tm, LANES), F32)
    for k, v in enumerate([slot0, slot1, g_p * p1, g_p * p2, e0, e1]):
        rec = jnp.where(lane == k, v, rec)
    route_ref[...] = rec[:, 0:ROUTE_W]
    routet_ref[...] = rec.T[0:ROUTE_W, :]


def _out_router(att, y, x2d, mod3, w_out_b, norm_w, w_router, b_router):
    tm = OUT_TM
    n_steps = TOKENS // tm
    steps_per_batch = SEQ // tm
    idx = jnp.arange(tm)
    ltri = (idx[None, :] < idx[:, None]).astype(BF16)
    lidx = jnp.arange(LANES)
    sut = (lidx[:, None] < lidx[None, :]).astype(BF16)
    const = lambda shape: pl.BlockSpec(shape, lambda i: (0, 0))
    cur = lambda i: jnp.minimum(i, n_steps - 1)
    prev = lambda i: jnp.maximum(i - 1, 0)
    tok = lambda width: pl.BlockSpec((tm, width), lambda i: (cur(i), 0))
    modspec = lambda k: pl.BlockSpec((1, 1, D_MODEL), lambda i: ((cur(i) // steps_per_batch) * 6 + k, 0, 0))
    return pl.pallas_call(
        _out_router_kernel,
        grid=(n_steps + 1,),
        in_specs=[tok(ATT_WIDTH), tok(SSD_WIDTH), tok(D_MODEL), modspec(2),
                  const((D_MODEL, D_MODEL)), const((1, D_MODEL)), modspec(4), modspec(3),
                  const((D_MODEL, LANES)), const((1, LANES)), const((tm, tm)), const((LANES, LANES))],
        out_specs=[tok(D_MODEL), tok(D_MODEL),
                   pl.BlockSpec((tm, ROUTE_W), lambda i: (prev(i), 0)),
                   pl.BlockSpec((ROUTE_W, tm), lambda i: (prev(i), 0)),
                   pl.BlockSpec((1, 1, LANES), lambda i: (prev(i), 0, 0))],
        out_shape=[jax.ShapeDtypeStruct((TOKENS, D_MODEL), F32),
                   jax.ShapeDtypeStruct((TOKENS, D_MODEL), BF16),
                   jax.ShapeDtypeStruct((TOKENS, ROUTE_W), F32),
                   jax.ShapeDtypeStruct((n_steps * ROUTE_W, tm), F32),
                   jax.ShapeDtypeStruct((n_steps, 1, LANES), F32)],
        scratch_shapes=[pltpu.VMEM((D_MODEL, 2 * LANES), BF16), pltpu.VMEM((tm, LANES), F32),
                        pltpu.VMEM((D_MODEL, D_MODEL), BF16)],
        compiler_params=_cparams(("arbitrary",)),
        name="out_router",
    )(att, y, x2d, mod3, w_out_b, norm_w.reshape(1, D_MODEL), mod3, mod3, w_router, b_router, ltri, sut)


MOE_TM = 512
ZERO_ROWS = 256
TAIL_BLOCK = 128
N_TOKEN_TILES = TOKENS // OUT_TM
MAX_SORTED_ROWS = TOKENS * TOP_K + N_TOKEN_TILES * N_EXPERTS * (RUN_ALIGN - 1)
N_TILES = MAX_SORTED_ROWS // MOE_TM + N_EXPERTS
N_ROWS = N_TILES * MOE_TM
assert MOE_TM % ZERO_ROWS == 0 and ZERO_ROWS % TAIL_BLOCK == 0 and TAIL_BLOCK % RUN_ALIGN == 0


BIG_PIECE = 2 * RUN_ALIGN
PIECE_SLOTS = LOCAL_ROWS // BIG_PIECE
TABLE_W = 4 * PIECE_SLOTS
COMBINE_K = 256
assert PIECE_SLOTS >= N_EXPERTS and LOCAL_ROWS % COMBINE_K == 0


def _run_copies(table_ref, n_big, n_small, make_copy, action):
    def big(q, carry):
        action(make_copy(table_ref[0, 0, q], table_ref[0, 0, PIECE_SLOTS + q], BIG_PIECE))
        return carry

    def small(q, carry):
        action(make_copy(table_ref[0, 0, 2 * PIECE_SLOTS + q], table_ref[0, 0, 3 * PIECE_SLOTS + q], RUN_ALIGN))
        return carry

    lax.fori_loop(0, n_big, big, 0)
    lax.fori_loop(0, n_small, small, 0)


def _dispatch_kernel(seg_end_ref, used_end_ref, nb_ref, ns_ref, tab_ref, routet_ref, h2_ref, xs_ref,
                     sbuf_ref, zero_ref, sems, zsem):
    i = pl.program_id(0)
    last = pl.num_programs(0) - 1
    buf = lax.rem(i, 2)

    def zero_fills(action):
        def tail_copy(row, rows):
            return pltpu.make_async_copy(zero_ref.at[pl.ds(0, rows)],
                                         xs_ref.at[pl.ds(pl.multiple_of(row, RUN_ALIGN), rows)], zsem)

        def block_copy(block):
            start = pl.multiple_of(block * ZERO_ROWS, ZERO_ROWS)
            return pltpu.make_async_copy(zero_ref, xs_ref.at[pl.ds(start, ZERO_ROWS)], zsem)

        def tails(e, carry):
            start, end = used_end_ref[e], seg_end_ref[e]
            mid = jnp.minimum(((start + TAIL_BLOCK - 1) // TAIL_BLOCK) * TAIL_BLOCK, end)

            def small(r, c):
                action(tail_copy(r * RUN_ALIGN, RUN_ALIGN))
                return c

            def big(r, c):
                action(tail_copy(r * TAIL_BLOCK, TAIL_BLOCK))
                return c

            lax.fori_loop(start // RUN_ALIGN, mid // RUN_ALIGN, small, 0)
            lax.fori_loop(mid // TAIL_BLOCK, end // TAIL_BLOCK, big, 0)
            return carry

        def blocks(block, carry):
            action(block_copy(block))
            return carry

        lax.fori_loop(0, N_EXPERTS, tails, 0)
        lax.fori_loop(seg_end_ref[N_EXPERTS - 1] // ZERO_ROWS, N_ROWS // ZERO_ROWS, blocks, 0)

    @pl.when(i == 0)
    def _():
        zero_ref[...] = jnp.zeros_like(zero_ref)
        zero_fills(lambda cp: cp.start())

    slot = lax.broadcasted_iota(I32, (LOCAL_ROWS, OUT_TM), 0).astype(F32)
    perm = jnp.where((slot == routet_ref[0:1, :]) | (slot == routet_ref[1:2, :]), 1.0, 0.0).astype(BF16)
    sbuf_ref[buf] = jnp.dot(perm, h2_ref[...], preferred_element_type=F32).astype(BF16)

    def piece(b):
        def make(local, sorted_row, rows):
            return pltpu.make_async_copy(
                sbuf_ref.at[b, pl.ds(pl.multiple_of(local, RUN_ALIGN), rows)],
                xs_ref.at[pl.ds(pl.multiple_of(sorted_row, RUN_ALIGN), rows)], sems.at[b])
        return make

    _run_copies(tab_ref, nb_ref[i], ns_ref[i], piece(buf), lambda cp: cp.start())
    prev = jnp.maximum(i - 1, 0)

    @pl.when(i > 0)
    def _():
        _run_copies(tab_ref, nb_ref[prev], ns_ref[prev], lambda lo, so, rows: piece(1 - buf)(0, 0, rows),
                    lambda cp: cp.wait())

    @pl.when(i == last)
    def _():
        _run_copies(tab_ref, nb_ref[i], ns_ref[i], lambda lo, so, rows: piece(buf)(0, 0, rows),
                    lambda cp: cp.wait())
        zero_fills(lambda cp: cp.wait())


def _piece_spec(index_map):
    return pl.BlockSpec((1, 1, TABLE_W), index_map, memory_space=pltpu.SMEM)


def _dispatch(seg_end, used_end, n_big, n_small, piece_table, route_t, h2):
    grid_spec = pltpu.PrefetchScalarGridSpec(
        num_scalar_prefetch=4,
        grid=(N_TOKEN_TILES,),
        in_specs=[_piece_spec(lambda i, se, ue, nb, ns: (i, 0, 0)),
                  pl.BlockSpec((ROUTE_W, OUT_TM), lambda i, se, ue, nb, ns: (i, 0)),
                  pl.BlockSpec((OUT_TM, D_MODEL), lambda i, se, ue, nb, ns: (i, 0))],
        out_specs=pl.BlockSpec(memory_space=pl.ANY),
        scratch_shapes=[pltpu.VMEM((2, LOCAL_ROWS, D_MODEL), BF16),
                        pltpu.VMEM((ZERO_ROWS, D_MODEL), BF16),
                        pltpu.SemaphoreType.DMA((2,)), pltpu.SemaphoreType.DMA],
    )
    return pl.pallas_call(
        _dispatch_kernel,
        grid_spec=grid_spec,
        out_shape=jax.ShapeDtypeStruct((N_ROWS, D_MODEL), BF16),
        compiler_params=_cparams(("arbitrary",)),
        name="dispatch",
    )(seg_end, used_end, n_big, n_small, piece_table, route_t, h2)


X_BUFS = 3


def _experts_kernel(te_ref, seg_ref, nxt_ref, nu_ref, xs_hbm, wg_hbm, wu_hbm, wd_hbm, ys_hbm,
                    xbuf, ybuf, wg_buf, wu_buf, wd_buf, wgu_b_ref, wd_b_ref, xsem, ysem, wsem):
    n = nu_ref[0]

    def rows(t):
        return pl.ds(pl.multiple_of(t * MOE_TM, MOE_TM), MOE_TM)

    def x_copy(t, s):
        return pltpu.make_async_copy(xs_hbm.at[rows(t)], xbuf.at[s], xsem.at[s])

    def y_copy(t, s):
        return pltpu.make_async_copy(ybuf.at[s], ys_hbm.at[rows(t)], ysem.at[s])

    def weight_copies(expert, s):
        return [pltpu.make_async_copy(wg_hbm.at[expert], wg_buf.at[s], wsem.at[s]),
                pltpu.make_async_copy(wu_hbm.at[expert], wu_buf.at[s], wsem.at[s]),
                pltpu.make_async_copy(wd_hbm.at[expert], wd_buf.at[s], wsem.at[s])]

    for cp in weight_copies(te_ref[0], 0):
        cp.start()
    for t in range(X_BUFS - 1):
        @pl.when(t < n)
        def _():
            x_copy(t, t).start()

    def tile(i, carry):
        xs_slot = lax.rem(i, X_BUFS)
        ys_slot = lax.rem(i, 2)
        w_slot = lax.rem(seg_ref[i], 2)
        x_copy(i, xs_slot).wait()
        ahead = i + (X_BUFS - 1)

        @pl.when(ahead < n)
        def _():
            x_copy(ahead, lax.rem(ahead, X_BUFS)).start()

        @pl.when((i == 0) | (te_ref[i] != te_ref[jnp.maximum(i - 1, 0)]))
        def _():
            for cp in weight_copies(te_ref[i], w_slot):
                cp.wait()

            @pl.when(nxt_ref[i] >= 0)
            def _():
                for cp in weight_copies(nxt_ref[i], 1 - w_slot):
                    cp.start()

            wgu_b_ref[:, 0:D_EXPERT] = wg_buf[w_slot].astype(BF16)
            wgu_b_ref[:, D_EXPERT:2 * D_EXPERT] = wu_buf[w_slot].astype(BF16)
            wd_b_ref[...] = wd_buf[w_slot].astype(BF16)

        @pl.when(i >= 2)
        def _():
            y_copy(i - 2, ys_slot).wait()

        h = jnp.dot(xbuf[xs_slot], wgu_b_ref[...], preferred_element_type=F32)
        act = (_silu(h[:, 0:D_EXPERT]) * h[:, D_EXPERT:2 * D_EXPERT]).astype(BF16)
        ybuf[ys_slot] = jnp.dot(act, wd_b_ref[...], preferred_element_type=F32).astype(BF16)
        y_copy(i, ys_slot).start()
        return carry

    lax.fori_loop(0, n, tile, 0)

    @pl.when(n >= 2)
    def _():
        y_copy(n - 2, lax.rem(n - 2, 2)).wait()

    y_copy(n - 1, lax.rem(n - 1, 2)).wait()


def _experts(tile_expert, tile_segment, next_expert, n_used, xs, w_gate, w_up, w_down):
    n_prefetch = 4
    anywhere = pl.BlockSpec(memory_space=pl.ANY)
    grid_spec = pltpu.PrefetchScalarGridSpec(
        num_scalar_prefetch=n_prefetch,
        grid=(1,),
        in_specs=[anywhere, anywhere, anywhere, anywhere],
        out_specs=anywhere,
        scratch_shapes=[pltpu.VMEM((X_BUFS, MOE_TM, D_MODEL), BF16), pltpu.VMEM((2, MOE_TM, D_MODEL), BF16),
                        pltpu.VMEM((2, D_MODEL, D_EXPERT), F32), pltpu.VMEM((2, D_MODEL, D_EXPERT), F32),
                        pltpu.VMEM((2, D_EXPERT, D_MODEL), F32),
                        pltpu.VMEM((D_MODEL, 2 * D_EXPERT), BF16), pltpu.VMEM((D_EXPERT, D_MODEL), BF16),
                        pltpu.SemaphoreType.DMA((X_BUFS,)), pltpu.SemaphoreType.DMA((2,)),
                        pltpu.SemaphoreType.DMA((2,))],
    )
    return pl.pallas_call(
        _experts_kernel,
        grid_spec=grid_spec,
        out_shape=jax.ShapeDtypeStruct((N_ROWS, D_MODEL), BF16),
        input_output_aliases={n_prefetch: 0},
        compiler_params=_cparams(("arbitrary",)),
        name="experts",
    )(tile_expert, tile_segment, next_expert, n_used, xs, w_gate, w_up, w_down)


def _combine_kernel(nb_ref, ns_ref, tab_ref, tab_next_ref, route_ref, x1_ref, g2_ref, ys_ref, o_ref,
                    gbuf_ref, sems):
    i = pl.program_id(0)
    last = pl.num_programs(0) - 1
    buf = lax.rem(i, 2)

    def piece(b):
        def make(local, sorted_row, rows):
            return pltpu.make_async_copy(
                ys_ref.at[pl.ds(pl.multiple_of(sorted_row, RUN_ALIGN), rows)],
                gbuf_ref.at[b, pl.ds(pl.multiple_of(local, RUN_ALIGN), rows)], sems.at[b])
        return make

    @pl.when(i == 0)
    def _():
        gbuf_ref[...] = jnp.zeros_like(gbuf_ref)
        _run_copies(tab_ref, nb_ref[0], ns_ref[0], piece(0), lambda cp: cp.start())

    nxt = jnp.minimum(i + 1, last)

    @pl.when(i < last)
    def _():
        _run_copies(tab_next_ref, nb_ref[nxt], ns_ref[nxt], piece(1 - buf), lambda cp: cp.start())

    rec = route_ref[...]
    slot0 = lax.broadcasted_iota(I32, (OUT_TM, COMBINE_K), 1).astype(F32)
    _run_copies(tab_ref, nb_ref[i], ns_ref[i], lambda lo, so, rows: piece(buf)(0, 0, rows), lambda cp: cp.wait())
    moe = jnp.zeros((OUT_TM, D_MODEL), F32)
    for k0 in range(0, LOCAL_ROWS, COMBINE_K):
        s0, s1 = rec[:, 0:1] - float(k0), rec[:, 1:2] - float(k0)
        weights = (jnp.where(slot0 == s0, rec[:, 2:3], 0.0)
                   + jnp.where(slot0 == s1, rec[:, 3:4], 0.0)).astype(BF16)
        moe = moe + jnp.dot(weights, gbuf_ref[buf, k0:k0 + COMBINE_K, :], preferred_element_type=F32)
    o_ref[...] = x1_ref[...] + g2_ref[0] * moe


def _combine(n_big, n_small, piece_table, route, x1, mod3, ys):
    tm = OUT_TM
    steps_per_batch = SEQ // tm
    grid_spec = pltpu.PrefetchScalarGridSpec(
        num_scalar_prefetch=2,
        grid=(N_TOKEN_TILES,),
        in_specs=[_piece_spec(lambda i, nb, ns: (i, 0, 0)),
                  _piece_spec(lambda i, nb, ns: (jnp.minimum(i + 1, N_TOKEN_TILES - 1), 0, 0)),
                  pl.BlockSpec((tm, ROUTE_W), lambda i, nb, ns: (i, 0)),
                  pl.BlockSpec((tm, D_MODEL), lambda i, nb, ns: (i, 0)),
                  pl.BlockSpec((1, 1, D_MODEL), lambda i, nb, ns: ((i // steps_per_batch) * 6 + 5, 0, 0)),
                  pl.BlockSpec(memory_space=pl.ANY)],
        out_specs=pl.BlockSpec((tm, D_MODEL), lambda i, nb, ns: (i, 0)),
        scratch_shapes=[pltpu.VMEM((2, LOCAL_ROWS, D_MODEL), BF16), pltpu.SemaphoreType.DMA((2,))],
    )
    return pl.pallas_call(
        _combine_kernel,
        grid_spec=grid_spec,
        out_shape=jax.ShapeDtypeStruct((TOKENS, D_MODEL), F32),
        compiler_params=_cparams(("arbitrary",)),
        name="combine",
    )(n_big, n_small, piece_table, piece_table, route, x1, mod3, ys)


def kernel(x, c, positions, norm1_w, norm2_w, w_ada, b_ada, w_in, conv_w, conv_b, dt_bias, a_log,
           d_skip, ssd_norm_w, q_norm_w, k_norm_w, sinks, w_out, w_group, b_group, w_expert, b_expert,
           w_gate, w_up, w_down):
    assert x.shape == (BATCH, SEQ, D_MODEL) and w_in.shape == (D_MODEL, IN_WIDTH)
    x2d = x.reshape(TOKENS, D_MODEL)
    cosf, s1, s2, mod = _rope_tables_and_mod(positions, c, w_ada, b_ada)
    mod3 = mod.reshape(BATCH * 6, 1, D_MODEL)

    qkv, z, xbc, dt, dt_t = _in_proj(x2d, norm1_w, mod3, w_in)
    att = _attention(qkv, (cosf, s1, s2), q_norm_w, k_norm_w, sinks)
    y = _ssd(xbc, z, dt, dt_t, conv_w, conv_b, dt_bias, a_log, d_skip, ssd_norm_w)

    w_router = jnp.pad(jnp.concatenate([w_group, w_expert], axis=1).astype(F32),
                       ((0, 0), (0, LANES - ROUTER_COLS)))
    b_router = jnp.pad(jnp.concatenate([b_group, b_expert]).astype(F32),
                       (0, LANES - ROUTER_COLS)).reshape(1, LANES)
    x1, h2, route, route_t, tcnt = _out_router(att, y, x2d, mod3, w_out.astype(F32), norm2_w,
                                                w_router, b_router)

    tc = tcnt[:, 0, 0:N_EXPERTS].astype(I32)
    run_rows = ((tc + RUN_ALIGN - 1) // RUN_ALIGN) * RUN_ALIGN
    counts = jnp.sum(run_rows, axis=0)
    padded = ((counts + MOE_TM - 1) // MOE_TM) * MOE_TM
    seg_end = jnp.cumsum(padded)
    seg_start = seg_end - padded
    run_dst = seg_start[None, :] + jnp.cumsum(run_rows, axis=0) - run_rows
    n_used = (seg_end[-1] // MOE_TM).reshape(1)
    last_row = jnp.minimum(jnp.arange(N_TILES, dtype=I32) * MOE_TM, seg_end[-1] - 1)
    tile_expert = jnp.sum((seg_end[None, :] <= last_row[:, None]).astype(I32), axis=1)

    run_local = jnp.cumsum(run_rows, axis=1) - run_rows
    n_big_run = run_rows // BIG_PIECE
    n_small_run = (run_rows // RUN_ALIGN) % 2
    q = jnp.arange(PIECE_SLOTS, dtype=I32)
    experts = jnp.arange(N_EXPERTS, dtype=I32)

    def flat(per_run, local0, dst0, stride):
        end = jnp.cumsum(per_run, axis=1)
        run_of = jnp.sum((end[:, None, :] <= q[None, :, None]).astype(I32), axis=2)
        pick = (run_of[:, :, None] == experts[None, None, :]).astype(I32)
        k = q[None, :] - jnp.sum(pick * (end - per_run)[:, None, :], axis=2)
        local = jnp.sum(pick * local0[:, None, :], axis=2) + stride * k
        dst = jnp.sum(pick * dst0[:, None, :], axis=2) + stride * k
        return end[:, -1], local, dst

    n_big, big_local, big_dst = flat(n_big_run, run_local, run_dst, BIG_PIECE)
    n_small, small_local, small_dst = flat(n_small_run, run_local + BIG_PIECE * n_big_run,
                                           run_dst + BIG_PIECE * n_big_run, 0)
    piece_table = jnp.concatenate([big_local, big_dst, small_local, small_dst], axis=1)
    piece_table = piece_table.astype(I32).reshape(N_TOKEN_TILES, 1, TABLE_W)

    nonempty = padded > 0
    seg_rank = jnp.cumsum(nonempty.astype(I32)) - 1
    later = nonempty[None, :] & (experts[None, :] > experts[:, None])
    next_of = jnp.min(jnp.where(later, experts[None, :], N_EXPERTS), axis=1)
    next_of = jnp.where(next_of == N_EXPERTS, -1, next_of)
    tile_is = (tile_expert[:, None] == experts[None, :]).astype(I32)
    tile_segment = jnp.sum(tile_is * seg_rank[None, :], axis=1)
    next_expert = jnp.sum(tile_is * next_of[None, :], axis=1)

    n_big, n_small = n_big.astype(I32), n_small.astype(I32)
    xs = _dispatch(seg_end.astype(I32), (seg_start + counts).astype(I32), n_big, n_small, piece_table,
                   route_t, h2)
    ys = _experts(tile_expert, tile_segment.astype(I32), next_expert.astype(I32), n_used.astype(I32),
                  xs, w_gate, w_up, w_down)
    out = _combine(n_big, n_small, piece_table, route, x1, mod3, ys)
    return out.reshape(BATCH, SEQ, D_MODEL)
```

```python
import jax
import jax.numpy as jnp
from jax import lax
from jax.experimental import pallas as pl
from jax.experimental.pallas import tpu as pltpu

F32 = jnp.float32
BF16 = jnp.bfloat16
I32 = jnp.int32

D_MODEL = 1024
BATCH = 2
SEQ = 8192
TOKENS = BATCH * SEQ
ATT_HEADS = 8
ATT_KV_HEADS = 2
HEAD_DIM = 64
ATT_WIDTH = ATT_HEADS * HEAD_DIM
KV_WIDTH = ATT_KV_HEADS * HEAD_DIM
ATT_BLOCK = 128
ROPE_DIM = HEAD_DIM // 4
ROPE_THETA = 500000.0
SSD_HEADS = 8
SSD_HEAD_DIM = 64
SSD_WIDTH = SSD_HEADS * SSD_HEAD_DIM
SSD_GROUPS = 2
SSD_STATE = 128
CONV_K = 4
CHUNK = 128
XBC_WIDTH = SSD_WIDTH + 2 * SSD_GROUPS * SSD_STATE
IN_WIDTH = ATT_WIDTH + 2 * KV_WIDTH + SSD_WIDTH + XBC_WIDTH + SSD_HEADS
N_GROUPS = 4
EXPERTS_PER_GROUP = 8
N_EXPERTS = N_GROUPS * EXPERTS_PER_GROUP
TOP_K = 2
D_EXPERT = 256
EPS = 1e-6

LANES = 128
QKV_WIDTH = ATT_WIDTH + 2 * KV_WIDTH
IN_PAD = QKV_WIDTH + SSD_WIDTH + XBC_WIDTH + LANES
NEG_BIG = -1e30

VMEM_LIMIT = 48 * 1024 * 1024


def _cparams(sem):
    return pltpu.CompilerParams(dimension_semantics=sem, vmem_limit_bytes=VMEM_LIMIT)


def _split_bf16(x):
    hi = x.astype(BF16)
    lo = (x - hi.astype(F32)).astype(BF16)
    return hi, lo


ADA_TN = 768


def _ada_kernel(ct_ref, w_ref, b_ref, o_ref):
    ct = ct_ref[...]
    s = ct * jax.nn.sigmoid(ct)
    w = w_ref[...]
    rows = [jnp.sum(s[:, b:b + 1] * w, axis=0, keepdims=True) for b in range(BATCH)]
    o_ref[...] = jnp.concatenate(rows, axis=0) + b_ref[...]


INPROJ_TM = 1024
_INPROJ_CHUNK = 256


def _inproj_kernel(x_ref, nw_ref, sc_ref, sh_ref, wf_ref, wdt_ref, qkv_ref, z_ref, xbc_ref, dt_ref, dtt_ref,
                   w_ref):
    @pl.when(pl.program_id(0) == 0)
    def _():
        for c0 in range(0, IN_PAD - LANES, _INPROJ_CHUNK):
            w_ref[:, c0:c0 + _INPROJ_CHUNK] = wf_ref[:, c0:c0 + _INPROJ_CHUNK].astype(BF16)
        w_ref[:, IN_PAD - LANES:IN_PAD] = wdt_ref[...].astype(BF16)

    x = x_ref[...]
    y = x * lax.rsqrt(jnp.mean(x * x, axis=-1, keepdims=True) + EPS)
    h = (y * nw_ref[...]) * (1.0 + sc_ref[0]) + sh_ref[0]
    hb = h.astype(BF16)

    def proj(c0, c1):
        return jnp.dot(hb, w_ref[:, c0:c1], preferred_element_type=F32)

    for c0 in range(0, QKV_WIDTH, _INPROJ_CHUNK):
        qkv_ref[:, c0:c0 + _INPROJ_CHUNK] = proj(c0, c0 + _INPROJ_CHUNK).astype(BF16)
    base = QKV_WIDTH
    for c0 in range(0, SSD_WIDTH, _INPROJ_CHUNK):
        z_ref[:, c0:c0 + _INPROJ_CHUNK] = proj(base + c0, base + c0 + _INPROJ_CHUNK).astype(BF16)
    base += SSD_WIDTH
    for c0 in range(0, XBC_WIDTH, _INPROJ_CHUNK):
        xbc_ref[:, c0:c0 + _INPROJ_CHUNK] = proj(base + c0, base + c0 + _INPROJ_CHUNK).astype(BF16)
    base += XBC_WIDTH
    dt = proj(base, base + LANES)
    dt_ref[...] = dt
    dtt_ref[...] = dt.T[0:SSD_HEADS, :]


def _in_proj(x2d, norm_w, mod3, w_in):
    tm = INPROJ_TM
    steps_per_batch = SEQ // tm
    w_dt = jnp.pad(w_in[:, IN_WIDTH - SSD_HEADS:IN_WIDTH].astype(F32), ((0, 0), (0, LANES - SSD_HEADS)))
    return pl.pallas_call(
        _inproj_kernel,
        grid=(TOKENS // tm,),
        in_specs=[pl.BlockSpec((tm, D_MODEL), lambda i: (i, 0)),
                  pl.BlockSpec((1, D_MODEL), lambda i: (0, 0)),
                  pl.BlockSpec((1, 1, D_MODEL), lambda i: ((i // steps_per_batch) * 6 + 1, 0, 0)),
                  pl.BlockSpec((1, 1, D_MODEL), lambda i: ((i // steps_per_batch) * 6 + 0, 0, 0)),
                  pl.BlockSpec((D_MODEL, IN_WIDTH), lambda i: (0, 0), pipeline_mode=pl.Buffered(1)),
                  pl.BlockSpec((D_MODEL, LANES), lambda i: (0, 0))],
        out_specs=[pl.BlockSpec((tm, QKV_WIDTH), lambda i: (i, 0)),
                   pl.BlockSpec((tm, SSD_WIDTH), lambda i: (i, 0)),
                   pl.BlockSpec((tm, XBC_WIDTH), lambda i: (i, 0)),
                   pl.BlockSpec((tm, LANES), lambda i: (i, 0)),
                   pl.BlockSpec((SSD_HEADS, tm), lambda i: (0, i))],
        out_shape=[jax.ShapeDtypeStruct((TOKENS, QKV_WIDTH), BF16),
                   jax.ShapeDtypeStruct((TOKENS, SSD_WIDTH), BF16),
                   jax.ShapeDtypeStruct((TOKENS, XBC_WIDTH), BF16),
                   jax.ShapeDtypeStruct((TOKENS, LANES), F32),
                   jax.ShapeDtypeStruct((SSD_HEADS, TOKENS), F32)],
        scratch_shapes=[pltpu.VMEM((D_MODEL, IN_PAD), BF16)],
        compiler_params=_cparams(("arbitrary",)),
        name="in_proj",
    )(x2d, norm_w.reshape(1, D_MODEL), mod3, mod3, w_in.astype(F32), w_dt)


ATT_SUB = 8


ROPE_TM = 2048
_ROPE_HALF = ROPE_DIM // 2
_TOK_PER_ROW = LANES // _ROPE_HALF


def _exact_dot(x, onehot_b):
    hi, lo = _split_bf16(x)
    return (jnp.dot(hi, onehot_b, preferred_element_type=F32)
            + jnp.dot(lo, onehot_b, preferred_element_type=F32))


def _rope_kernel(pos_ref, freq_ref, sel_ref, own_ref, gcos_ref, gs1_ref, gs2_ref, ident_ref,
                 cos_ref, s1_ref, s2_ref):
    ang = pos_ref[...].astype(F32) * freq_ref[...]
    cos_p, sin_p = jnp.cos(ang), jnp.sin(ang)
    hi_c, lo_c = _split_bf16(cos_p)
    hi_s, lo_s = _split_bf16(sin_p)
    sel = sel_ref[...]
    rows_c = jnp.dot(sel, hi_c, preferred_element_type=F32) + jnp.dot(sel, lo_c, preferred_element_type=F32)
    rows_s = jnp.dot(sel, hi_s, preferred_element_type=F32) + jnp.dot(sel, lo_s, preferred_element_type=F32)
    own = own_ref[...]
    cos_ref[...] = _exact_dot(rows_c * own, gcos_ref[...]) + ident_ref[...]
    s1_ref[...] = _exact_dot(rows_s * own, gs1_ref[...])
    s2_ref[...] = _exact_dot(rows_s * own, gs2_ref[...])


def _tables_kernel(pos_ref, freq_ref, sel_ref, own_ref, gcos_ref, gs1_ref, gs2_ref, ident_ref,
                   ct_ref, w_ref, b_ref, cos_ref, s1_ref, s2_ref, mod_ref):
    _rope_kernel(pos_ref, freq_ref, sel_ref, own_ref, gcos_ref, gs1_ref, gs2_ref, ident_ref,
                 cos_ref, s1_ref, s2_ref)
    _ada_kernel(ct_ref, w_ref, b_ref, mod_ref)


def _rope_tables_and_mod(positions, c, w_ada, b_ada):
    n_mod = w_ada.shape[1]
    assert TOKENS // ROPE_TM == n_mod // ADA_TN
    half, per_row = _ROPE_HALF, _TOK_PER_ROW
    rows = ROPE_TM // per_row
    pos_rep = jnp.repeat(positions.reshape(TOKENS).astype(I32), half).reshape(TOKENS // per_row, LANES)
    inv_freq = jnp.power(ROPE_THETA, -jnp.arange(half, dtype=F32) * 2.0 / ROPE_DIM)
    freq = jnp.tile(inv_freq, per_row).reshape(1, LANES)
    tok = jnp.arange(ROPE_TM)
    lane = jnp.arange(LANES)
    sel = (tok[:, None] // per_row == jnp.arange(rows)[None, :]).astype(BF16)
    own = (lane[None, :] // half == tok[:, None] % per_row).astype(F32)
    d = lane % HEAD_DIM
    src_f = lane % half
    hits = lambda lo, hi: ((src_f[:, None] == d[None, :] % half) & (d[None, :] >= lo) & (d[None, :] < hi))
    gcos = hits(0, ROPE_DIM).astype(BF16)
    gs1 = -hits(0, half).astype(BF16)
    gs2 = hits(half, ROPE_DIM).astype(BF16)
    ident = (d >= ROPE_DIM).astype(F32).reshape(1, LANES)
    const = lambda shape: pl.BlockSpec(shape, lambda i: (0, 0))
    out_spec = pl.BlockSpec((ROPE_TM, LANES), lambda i: (i, 0))
    out = jax.ShapeDtypeStruct((TOKENS, LANES), F32)
    return pl.pallas_call(
        _tables_kernel,
        grid=(TOKENS // ROPE_TM,),
        in_specs=[pl.BlockSpec((rows, LANES), lambda i: (i, 0)), const((1, LANES)),
                  const((ROPE_TM, rows)), const((ROPE_TM, LANES)),
                  const((LANES, LANES)), const((LANES, LANES)), const((LANES, LANES)), const((1, LANES)),
                  const((D_MODEL, BATCH)),
                  pl.BlockSpec((D_MODEL, ADA_TN), lambda i: (0, i)),
                  pl.BlockSpec((1, ADA_TN), lambda i: (0, i))],
        out_specs=[out_spec, out_spec, out_spec, pl.BlockSpec((BATCH, ADA_TN), lambda i: (0, i))],
        out_shape=[out, out, out, jax.ShapeDtypeStruct((BATCH, n_mod), F32)],
        compiler_params=_cparams(("arbitrary",)),
        name="rope_tables_ada_mod",
    )(pos_rep, freq, sel, own, gcos, gs1, gs2, ident, c.T, w_ada, b_ada.reshape(1, n_mod))


def _seg_meansq(xf, ones128):
    rows, width = xf.shape
    nt = width // LANES
    parts = _split_bf16(xf * xf)
    stacked = jnp.concatenate([p[:, t * LANES:(t + 1) * LANES] for p in parts for t in range(nt)], axis=0)
    tot = jnp.dot(stacked, ones128, preferred_element_type=F32)
    tiles = [tot[t * rows:(t + 1) * rows] + tot[(nt + t) * rows:(nt + t + 1) * rows] for t in range(nt)]
    return jnp.concatenate(tiles, axis=1) * (1.0 / HEAD_DIM)


def _norm_rope(x_bf, w_row, ones_bd, cosf, s1, s2):
    xf = x_bf.astype(F32)
    width = xf.shape[1]
    xn = xf * lax.rsqrt(_seg_meansq(xf, ones_bd) + EPS) * w_row
    half = ROPE_DIM // 2
    up = pltpu.roll(xn, width - half, axis=1)
    down = pltpu.roll(xn, half, axis=1)
    return xn * cosf + up * s1 + down * s2


def _attn_kernel(sink_ref, q_ref, kv_ref, cos_ref, s1_ref, s2_ref, qw_ref, kw_ref,
                 ones_ref, o_ref, kprev_ref, vprev_ref):
    j = pl.program_id(1)
    blk = ATT_BLOCK

    @pl.when(j == 0)
    def _():
        kprev_ref[...] = jnp.zeros_like(kprev_ref)
        vprev_ref[...] = jnp.zeros_like(vprev_ref)

    cos1 = cos_ref[...]
    s1_1 = s1_ref[...]
    s2_1 = s2_ref[...]
    reps = ATT_WIDTH // LANES
    cosq = jnp.concatenate([cos1] * reps, axis=1)
    s1q = jnp.concatenate([s1_1] * reps, axis=1)
    s2q = jnp.concatenate([s2_1] * reps, axis=1)

    q = _norm_rope(q_ref[...], qw_ref[...], ones_ref[...], cosq, s1q, s2q)
    qf = q * (HEAD_DIM ** -0.5)
    kv = kv_ref[...]
    kn = _norm_rope(kv[:, 0:KV_WIDTH], kw_ref[...], ones_ref[...], cos1, s1_1, s2_1)
    vn = kv[:, KV_WIDTH:2 * KV_WIDTH].astype(F32)

    kall = jnp.concatenate([kprev_ref[...], kn], axis=0)
    vall = jnp.concatenate([vprev_ref[...], vn], axis=0)
    kprev_ref[...] = kn[(ATT_SUB - 1) * blk:ATT_SUB * blk]
    vprev_ref[...] = vn[(ATT_SUB - 1) * blk:ATT_SUB * blk]

    lo_all = lax.broadcasted_iota(I32, kall.shape, 1) < HEAD_DIM
    ones_all = jnp.ones(kall.shape, BF16)

    row = lax.broadcasted_iota(I32, (2 * blk, blk), 0)
    col = lax.broadcasted_iota(I32, (2 * blk, blk), 1)
    from_prev = col > (row & (blk - 1))
    second_tile = lax.broadcasted_iota(I32, (2 * blk, 1), 0) >= blk
    zero_p = jnp.zeros((2 * blk, blk), F32)

    k_par, v_par = [], []
    for g in range(ATT_KV_HEADS):
        keep = lo_all if g == 0 else ~lo_all
        k_own = jnp.where(keep, kall, 0.0)
        v_own = jnp.where(keep, vall, 0.0)
        k_oth = pltpu.roll(k_own, HEAD_DIM, axis=1)
        v_oth = pltpu.roll(v_own, HEAD_DIM, axis=1)
        k_lo, k_hi = (k_own, k_oth) if g == 0 else (k_oth, k_own)
        v_lo, v_hi = (v_own, v_oth) if g == 0 else (v_oth, v_own)
        k_par.append((k_lo.astype(BF16), k_hi.astype(BF16)))
        v_par.append((jnp.concatenate([v_lo.astype(BF16), ones_all], axis=1),
                      jnp.concatenate([v_hi.astype(BF16), ones_all], axis=1)))

    problems = [(g, sub) for g in range(ATT_KV_HEADS) for sub in range(ATT_SUB)]
    scores = []
    for g, sub in problems:
        r0, c0 = sub * blk, g * 2 * LANES
        qcat = jnp.concatenate([qf[r0:r0 + blk, c0:c0 + LANES],
                                qf[r0:r0 + blk, c0 + LANES:c0 + 2 * LANES]], axis=0).astype(BF16)
        kw = jnp.concatenate([k_par[g][0][r0:r0 + 2 * blk], k_par[g][1][r0:r0 + 2 * blk]], axis=0)
        scores.append(lax.dot_general(qcat, kw, (((1,), (1,)), ((), ())),
                                      preferred_element_type=F32))

    weights, rescale = [], []
    for (g, sub), s_all in zip(problems, scores):
        for par in range(2):
            s = s_all[:, par * 2 * blk:(par + 1) * 2 * blk]
            s_prev = s[:, 0:blk]
            if sub == 0:
                s_prev = s_prev + jnp.where(j > 0, 0.0, NEG_BIG)
            s = jnp.where(from_prev, s_prev, s[:, blk:2 * blk])
            h_first = ATT_HEADS // ATT_KV_HEADS * g + par
            sink = jnp.where(second_tile, sink_ref[h_first + 2], sink_ref[h_first])
            m = jnp.maximum(jnp.max(s, axis=-1, keepdims=True), sink)
            p = jnp.exp(s - m)
            weights.append(jnp.concatenate([jnp.where(from_prev, p, zero_p), jnp.where(from_prev, zero_p, p)],
                                           axis=1).astype(BF16))
            rescale.append(jnp.exp(sink - m))

    outs = []
    for idx, (g, sub) in enumerate(problems):
        for par in range(2):
            outs.append(jnp.dot(weights[2 * idx + par], v_par[g][par][sub * blk:(sub + 2) * blk],
                                preferred_element_type=F32))

    for idx, (g, sub) in enumerate(problems):
        r0, c0 = sub * blk, g * 2 * LANES
        pair = None
        for par in range(2):
            o = outs[2 * idx + par]
            part = o[:, 0:LANES] * (1.0 / (o[:, LANES:2 * LANES] + rescale[2 * idx + par]))
            pair = part if pair is None else pair + part
        o_ref[r0:r0 + blk, c0:c0 + LANES] = pair[0:blk].astype(BF16)
        o_ref[r0:r0 + blk, c0 + LANES:c0 + 2 * LANES] = pair[blk:2 * blk].astype(BF16)


def _attention(qkv, rope_tables, q_norm_w, k_norm_w, sinks):
    cosf, s1, s2 = rope_tables
    qw = jnp.tile(q_norm_w.astype(F32), ATT_HEADS).reshape(1, ATT_WIDTH)
    kw = jnp.tile(k_norm_w.astype(F32), ATT_KV_HEADS).reshape(1, KV_WIDTH)
    seg = jnp.arange(LANES) // HEAD_DIM
    ones128 = (seg[:, None] == seg[None, :]).astype(BF16)
    const = lambda shape: pl.BlockSpec(shape, lambda b, j, s: (0, 0))
    rows = ATT_SUB * ATT_BLOCK
    nb = SEQ // rows
    tok = lambda width, cb: pl.BlockSpec((rows, width), lambda b, j, s: (b * nb + j, cb))
    grid_spec = pltpu.PrefetchScalarGridSpec(
        num_scalar_prefetch=1,
        grid=(BATCH, nb),
        in_specs=[tok(ATT_WIDTH, 0), tok(2 * KV_WIDTH, 2), tok(LANES, 0), tok(LANES, 0), tok(LANES, 0),
                  const((1, ATT_WIDTH)), const((1, KV_WIDTH)), const((LANES, LANES))],
        out_specs=tok(ATT_WIDTH, 0),
        scratch_shapes=[pltpu.VMEM((ATT_BLOCK, KV_WIDTH), F32),
                        pltpu.VMEM((ATT_BLOCK, KV_WIDTH), F32)],
    )
    return pl.pallas_call(
        _attn_kernel,
        grid_spec=grid_spec,
        out_shape=jax.ShapeDtypeStruct((TOKENS, ATT_WIDTH), BF16),
        compiler_params=_cparams(("arbitrary", "arbitrary")),
        name="attention",
    )(sinks.astype(F32), qkv, qkv, cosf, s1, s2, qw, kw, ones128)


SSD_SUB = 4


def _softplus(x):
    return jnp.maximum(x, 0.0) + jnp.log1p(jnp.exp(-jnp.abs(x)))


def _silu(x):
    h = 0.5 * x
    return h + h * jnp.tanh(h)


def _ssd_kernel(xbc_ref, z_ref, dt_ref, dtt_ref, cw_ref, cb_ref, dtb_row_ref, dtb_col_ref,
                alog_row_ref, alog_col_ref, dskip_ref, nw_ref, tril_ref, triu_ref,
                o_ref, conv_ref, state_ref):
    c = pl.program_id(1)
    L = CHUNK
    tail = 8

    @pl.when(c == 0)
    def _():
        conv_ref[0:tail, :] = jnp.zeros((tail, XBC_WIDTH), F32)
        state_ref[...] = jnp.zeros_like(state_ref)

    row = lax.broadcasted_iota(I32, (L, L), 0)
    col = lax.broadcasted_iota(I32, (L, L), 1)
    causal = col <= row
    lane = lax.broadcasted_iota(I32, (L, LANES), 1)
    lo_half = lane < SSD_HEAD_DIM

    prepared = [_ssd_prepare(s * L, xbc_ref, dt_ref, dtt_ref, cw_ref, cb_ref, dtb_row_ref, dtb_col_ref,
                             alog_row_ref, alog_col_ref, tril_ref, triu_ref, conv_ref)
                for s in range(SSD_SUB)]
    for s in range(SSD_SUB):
        _ssd_chunk(s * L, prepared[s], causal, lo_half, z_ref, dskip_ref, nw_ref, o_ref, state_ref)


def _ssd_prepare(r0, xbc_ref, dt_ref, dtt_ref, cw_ref, cb_ref, dtb_row_ref, dtb_col_ref,
                 alog_row_ref, alog_col_ref, tril_ref, triu_ref, conv_ref):
    L = CHUNK
    tail = 8
    xb = xbc_ref[r0:r0 + L, :].astype(F32)
    conv_ref[tail:tail + L, :] = xb
    acc = cb_ref[...] + cw_ref[CONV_K - 1:CONV_K, :] * xb
    for k in range(CONV_K - 1):
        off = tail - (CONV_K - 1) + k
        acc = acc + cw_ref[k:k + 1, :] * conv_ref[off:off + L, :]
    conv_ref[0:tail, :] = xb[L - tail:L, :]
    u = _silu(acc)
    xs = u[:, 0:SSD_WIDTH]
    bmat = u[:, SSD_WIDTH:SSD_WIDTH + SSD_GROUPS * SSD_STATE]
    cmat = u[:, SSD_WIDTH + SSD_GROUPS * SSD_STATE:XBC_WIDTH]

    dt = _softplus(dt_ref[r0:r0 + L, :] + dtb_row_ref[...])
    a = dt * (-jnp.exp(alog_row_ref[...]))
    a_hi, a_lo = _split_bf16(a)
    a_cum = (jnp.dot(tril_ref[...], a_hi, preferred_element_type=F32)
             + jnp.dot(tril_ref[...], a_lo, preferred_element_type=F32))
    dt_t = _softplus(dtt_ref[:, r0:r0 + L] + dtb_col_ref[...])
    a_t = dt_t * (-jnp.exp(alog_col_ref[...]))
    at_hi, at_lo = _split_bf16(a_t)
    a_cum_t = (jnp.dot(at_hi, triu_ref[...], preferred_element_type=F32)
               + jnp.dot(at_lo, triu_ref[...], preferred_element_type=F32))
    a_end_t = a_cum_t[:, L - 1:L]
    return dict(
        xs=xs, bmat=bmat, cmat=cmat, a_cum=a_cum, exp_a_cum=jnp.exp(a_cum),
        shifted_t=a_cum_t - jnp.log(dt_t),
        wst_t=jnp.exp(a_end_t - a_cum_t) * dt_t,
        cdec_t=jnp.exp(a_end_t))


def _ssd_chunk(r0, p, causal, lo_half, z_ref, dskip_ref, nw_ref, o_ref, state_ref):
    L = CHUNK
    xs, bmat, cmat, a_cum, exp_a_cum = p["xs"], p["bmat"], p["cmat"], p["a_cum"], p["exp_a_cum"]
    shifted_t, wst_t, cdec_t = p["shifted_t"], p["wst_t"], p["cdec_t"]
    xs_b = xs.astype(BF16)
    heads_per_group = SSD_HEADS // SSD_GROUPS
    gated = []
    for g in range(SSD_GROUPS):
        b_g = bmat[:, g * SSD_STATE:(g + 1) * SSD_STATE]
        c_g = cmat[:, g * SSD_STATE:(g + 1) * SSD_STATE]
        cb = lax.dot_general(c_g.astype(BF16), b_g.astype(BF16), (((1,), (1,)), ((), ())),
                             preferred_element_type=F32)
        b_gt = b_g.T
        for t in range(heads_per_group // 2):
            tile = g * (heads_per_group // 2) + t
            c0 = tile * LANES
            xs_tile = xs_b[:, c0:c0 + LANES]
            st_tile = state_ref[:, c0:c0 + LANES]
            st_b = st_tile.astype(BF16)
            y_tile = jnp.zeros((L, LANES), F32)
            new_tile = jnp.zeros((SSD_STATE, LANES), F32)
            for e in range(2):
                h = 2 * tile + e
                keep = lo_half if e == 0 else ~lo_half
                colb = jnp.broadcast_to(a_cum[:, h:h + 1], (L, L))
                rowb = shifted_t[h:h + 1, :]
                w_in = cb * jnp.exp(jnp.where(causal, colb - rowb, NEG_BIG))
                w_off = c_g * jnp.broadcast_to(exp_a_cum[:, h:h + 1], (L, L))
                lhs = jnp.concatenate([w_in, w_off], axis=1).astype(BF16)
                rhs = jnp.concatenate([jnp.where(keep, xs_tile, jnp.zeros_like(xs_tile)),
                                       jnp.where(keep, st_b, jnp.zeros_like(st_b))], axis=0)
                y_tile = y_tile + jnp.dot(lhs, rhs, preferred_element_type=F32)
                m_h = (b_gt * wst_t[h:h + 1, :]).astype(BF16)
                new_tile = new_tile + jnp.dot(m_h, jnp.where(keep, xs_tile, jnp.zeros_like(xs_tile)),
                                              preferred_element_type=F32)
            cd = jnp.where(lo_half[0:1, :], cdec_t[2 * tile:2 * tile + 1, :],
                           cdec_t[2 * tile + 1:2 * tile + 2, :])
            state_ref[:, c0:c0 + LANES] = st_tile * cd + new_tile
            y_full = y_tile + dskip_ref[:, c0:c0 + LANES] * xs[:, c0:c0 + LANES]
            gated.append(y_full * _silu(z_ref[r0:r0 + L, c0:c0 + LANES].astype(F32)))

    gw = SSD_WIDTH // SSD_GROUPS
    tiles_per_group = gw // LANES
    for g in range(SSD_GROUPS):
        yg = jnp.concatenate(gated[g * tiles_per_group:(g + 1) * tiles_per_group], axis=1)
        ms = jnp.mean(yg * yg, axis=-1, keepdims=True)
        o_ref[r0:r0 + L, g * gw:(g + 1) * gw] = (
            (yg * lax.rsqrt(ms + EPS)) * nw_ref[:, g * gw:(g + 1) * gw]).astype(o_ref.dtype)


def _ssd(xbc, z, dt, dt_t, conv_w, conv_b, dt_bias, a_log, d_skip, ssd_norm_w):
    L = SSD_SUB * CHUNK
    nc = SEQ // L
    pad_row = lambda v: jnp.pad(v.astype(F32), (0, LANES - SSD_HEADS)).reshape(1, LANES)
    col8 = lambda v: v.astype(F32).reshape(SSD_HEADS, 1)
    idx = jnp.arange(CHUNK)
    tril = (idx[None, :] <= idx[:, None]).astype(BF16)
    triu = (idx[:, None] <= idx[None, :]).astype(BF16)
    dskip = jnp.repeat(d_skip.astype(F32), SSD_HEAD_DIM).reshape(1, SSD_WIDTH)
    const = lambda shape: pl.BlockSpec(shape, lambda b, c: (0, 0))
    tok = lambda width: pl.BlockSpec((L, width), lambda b, c: (b * nc + c, 0))
    return pl.pallas_call(
        _ssd_kernel,
        grid=(BATCH, nc),
        in_specs=[tok(XBC_WIDTH), tok(SSD_WIDTH), tok(LANES),
                  pl.BlockSpec((SSD_HEADS, L), lambda b, c: (0, b * nc + c)),
                  const((CONV_K, XBC_WIDTH)), const((1, XBC_WIDTH)),
                  const((1, LANES)), const((SSD_HEADS, 1)), const((1, LANES)), const((SSD_HEADS, 1)),
                  const((1, SSD_WIDTH)), const((1, SSD_WIDTH)), const((CHUNK, CHUNK)), const((CHUNK, CHUNK))],
        out_specs=tok(SSD_WIDTH),
        out_shape=jax.ShapeDtypeStruct((TOKENS, SSD_WIDTH), BF16),
        scratch_shapes=[pltpu.VMEM((8 + CHUNK, XBC_WIDTH), F32),
                        pltpu.VMEM((SSD_STATE, SSD_WIDTH), F32)],
        compiler_params=_cparams(("arbitrary", "arbitrary")),
        name="ssd",
    )(xbc, z, dt, dt_t, conv_w.astype(F32), conv_b.astype(F32).reshape(1, XBC_WIDTH),
      pad_row(dt_bias), col8(dt_bias), pad_row(a_log), col8(a_log), dskip,
      ssd_norm_w.astype(F32).reshape(1, SSD_WIDTH), tril, triu)


OUT_TM = 512
OR_SUB = 2
ROUTE_W = 8
ROUTER_COLS = N_GROUPS + N_EXPERTS
RUN_ALIGN = 16
RUN_SHIFT = 4
LOCAL_ROWS = 1536
assert RUN_ALIGN == 1 << RUN_SHIFT and LOCAL_ROWS >= TOP_K * OUT_TM + N_EXPERTS * (RUN_ALIGN - 1)


def _lane_pick(values, lane, index):
    return jnp.sum(jnp.where(lane == index, values, 0.0), axis=-1, keepdims=True)


def _first_argmax(vals, lane):
    m = jnp.max(vals, axis=-1, keepdims=True)
    idx = jnp.min(jnp.where(vals == m, lane, float(LANES)), axis=-1, keepdims=True)
    return m, idx


def _out_router_kernel(att_ref, y_ref, x_ref, g1_ref, wof_ref, nw_ref, sc_ref, sh_ref, wr_ref, br_ref,
                       ltri_ref, sut_ref, x1_ref, h2_ref, route_ref, routet_ref, tcnt_ref,
                       wr_split_ref, logits_ref, wo_ref):
    i = pl.program_id(0)
    tm = OUT_TM

    @pl.when(i == 0)
    def _():
        hi, lo = _split_bf16(wr_ref[...])
        wr_split_ref[:, 0:LANES] = hi
        wr_split_ref[:, LANES:2 * LANES] = lo
        logits_ref[...] = jnp.zeros_like(logits_ref)
        for r0 in range(0, D_MODEL, _INPROJ_CHUNK):
            wo_ref[r0:r0 + _INPROJ_CHUNK, :] = wof_ref[r0:r0 + _INPROJ_CHUNK, :].astype(BF16)

    previous = [logits_ref[s] for s in range(OR_SUB)]

    for s in range(OR_SUB):
        rows = slice(s * tm, (s + 1) * tm)
        mixer = (jnp.dot(att_ref[rows, :], wo_ref[0:ATT_WIDTH, :], preferred_element_type=F32)
                 + jnp.dot(y_ref[rows, :], wo_ref[ATT_WIDTH:ATT_WIDTH + SSD_WIDTH, :], preferred_element_type=F32))
        x1 = x_ref[rows, :] + g1_ref[0] * mixer
        x1_ref[rows, :] = x1
        yn = x1 * lax.rsqrt(jnp.mean(x1 * x1, axis=-1, keepdims=True) + EPS)
        h2 = (yn * nw_ref[...]) * (1.0 + sc_ref[0]) + sh_ref[0]
        h2_ref[rows, :] = h2.astype(BF16)

        h_hi, h_lo = _split_bf16(h2)
        both = jnp.dot(h_hi, wr_split_ref[...], preferred_element_type=F32)
        logits_ref[s] = (both[:, 0:LANES] + both[:, LANES:2 * LANES]
                         + jnp.dot(h_lo, wr_split_ref[:, 0:LANES], preferred_element_type=F32)) + br_ref[...]

    for s in range(OR_SUB):
        _route_tile(previous[s], s, ltri_ref, sut_ref, route_ref, routet_ref, tcnt_ref)


def _route_tile(logits, s, ltri_ref, sut_ref, route_ref, routet_ref, tcnt_ref):
    tm = logits.shape[0]
    lane = lax.broadcasted_iota(I32, (tm, LANES), 1).astype(F32)

    gl = jnp.where(lane < N_GROUPS, logits, NEG_BIG)
    gmax, gidx = _first_argmax(gl, lane)
    g_p = 1.0 / jnp.sum(jnp.exp(gl - gmax), axis=-1, keepdims=True)

    lo_lane = N_GROUPS + EXPERTS_PER_GROUP * gidx
    el = jnp.where((lane >= lo_lane) & (lane < lo_lane + EXPERTS_PER_GROUP), logits, NEG_BIG)
    m1, i1 = _first_argmax(el, lane)
    m2, i2 = _first_argmax(jnp.where(lane == i1, NEG_BIG, el), lane)
    r = jnp.exp(m2 - m1)
    p1 = 1.0 / (1.0 + r)
    p2 = r / (1.0 + r)
    e0 = i1 - N_GROUPS
    e1 = i2 - N_GROUPS

    onehot = ((lane == e0) | (lane == e1)).astype(F32)
    tile_cnt = jnp.sum(onehot, axis=0, keepdims=True)
    run_len = jnp.floor((tile_cnt + (RUN_ALIGN - 1)) * (1.0 / RUN_ALIGN)) * RUN_ALIGN
    run_start = jnp.dot(jnp.broadcast_to(run_len, (8, LANES)).astype(BF16), sut_ref[...],
                        preferred_element_type=F32)[0:1, :]
    before = jnp.dot(ltri_ref[...], onehot.astype(BF16), preferred_element_type=F32) + run_start
    slot0 = _lane_pick(before, lane, e0)
    slot1 = _lane_pick(before, lane, e1)
    tcnt_ref[s] = tile_cnt

    rec = jnp.zeros((tm, LANES), F32)
    for k, v in enumerate([slot0, slot1, g_p * p1, g_p * p2, e0, e1]):
        rec = jnp.where(lane == k, v, rec)
    route_ref[s * tm:(s + 1) * tm, :] = rec[:, 0:ROUTE_W]
    routet_ref[s * ROUTE_W:(s + 1) * ROUTE_W, :] = rec.T[0:ROUTE_W, :]


def _out_router(att, y, x2d, mod3, w_out_b, norm_w, w_router, b_router):
    tm = OUT_TM
    rows = OR_SUB * tm
    n_steps = TOKENS // rows
    steps_per_batch = SEQ // rows
    idx = jnp.arange(tm)
    ltri = (idx[None, :] < idx[:, None]).astype(BF16)
    lidx = jnp.arange(LANES)
    sut = (lidx[:, None] < lidx[None, :]).astype(BF16)
    const = lambda shape: pl.BlockSpec(shape, lambda i: (0, 0))
    cur = lambda i: jnp.minimum(i, n_steps - 1)
    prev = lambda i: jnp.maximum(i - 1, 0)
    tok = lambda width: pl.BlockSpec((rows, width), lambda i: (cur(i), 0))
    modspec = lambda k: pl.BlockSpec((1, 1, D_MODEL), lambda i: ((cur(i) // steps_per_batch) * 6 + k, 0, 0))
    return pl.pallas_call(
        _out_router_kernel,
        grid=(n_steps + 1,),
        in_specs=[tok(ATT_WIDTH), tok(SSD_WIDTH), tok(D_MODEL), modspec(2),
                  const((D_MODEL, D_MODEL)), const((1, D_MODEL)), modspec(4), modspec(3),
                  const((D_MODEL, LANES)), const((1, LANES)), const((tm, tm)), const((LANES, LANES))],
        out_specs=[tok(D_MODEL), tok(D_MODEL),
                   pl.BlockSpec((rows, ROUTE_W), lambda i: (prev(i), 0)),
                   pl.BlockSpec((OR_SUB * ROUTE_W, tm), lambda i: (prev(i), 0)),
                   pl.BlockSpec((OR_SUB, 1, LANES), lambda i: (prev(i), 0, 0))],
        out_shape=[jax.ShapeDtypeStruct((TOKENS, D_MODEL), F32),
                   jax.ShapeDtypeStruct((TOKENS, D_MODEL), BF16),
                   jax.ShapeDtypeStruct((TOKENS, ROUTE_W), F32),
                   jax.ShapeDtypeStruct((N_TOKEN_TILES * ROUTE_W, tm), F32),
                   jax.ShapeDtypeStruct((N_TOKEN_TILES, 1, LANES), F32)],
        scratch_shapes=[pltpu.VMEM((D_MODEL, 2 * LANES), BF16), pltpu.VMEM((OR_SUB, tm, LANES), F32),
                        pltpu.VMEM((D_MODEL, D_MODEL), BF16)],
        compiler_params=_cparams(("arbitrary",)),
        name="out_router",
    )(att, y, x2d, mod3, w_out_b, norm_w.reshape(1, D_MODEL), mod3, mod3, w_router, b_router, ltri, sut)


MOE_TM = 512
ZERO_ROWS = 256
N_TOKEN_TILES = TOKENS // OUT_TM
MAX_SORTED_ROWS = TOKENS * TOP_K + N_TOKEN_TILES * N_EXPERTS * (RUN_ALIGN - 1)
N_TILES = MAX_SORTED_ROWS // MOE_TM + N_EXPERTS
N_ROWS = N_TILES * MOE_TM
assert MOE_TM % ZERO_ROWS == 0


BIG_PIECE = 2 * RUN_ALIGN
PIECE_SLOTS = LOCAL_ROWS // BIG_PIECE
TABLE_W = 4 * PIECE_SLOTS
COMBINE_K = 256
assert PIECE_SLOTS >= N_EXPERTS and LOCAL_ROWS % COMBINE_K == 0


def _run_copies(table_ref, n_big, n_small, make_copy, action):
    def big(q, carry):
        action(make_copy(table_ref[0, 0, q], table_ref[0, 0, PIECE_SLOTS + q], BIG_PIECE))
        return carry

    def small(q, carry):
        action(make_copy(table_ref[0, 0, 2 * PIECE_SLOTS + q], table_ref[0, 0, 3 * PIECE_SLOTS + q], RUN_ALIGN))
        return carry

    lax.fori_loop(0, n_big, big, 0)
    lax.fori_loop(0, n_small, small, 0)


def _dispatch_kernel(seg_end_ref, used_end_ref, nb_ref, ns_ref, tab_ref, routet_ref, h2_ref, xs_ref,
                     sbuf_ref, zero_ref, sems, zsem):
    i = pl.program_id(0)
    last = pl.num_programs(0) - 1
    buf = lax.rem(i, 2)

    def zero_fills(action):
        def tail_copy(row):
            return pltpu.make_async_copy(zero_ref.at[pl.ds(0, RUN_ALIGN)],
                                         xs_ref.at[pl.ds(pl.multiple_of(row, RUN_ALIGN), RUN_ALIGN)], zsem)

        def block_copy(block):
            start = pl.multiple_of(block * ZERO_ROWS, ZERO_ROWS)
            return pltpu.make_async_copy(zero_ref, xs_ref.at[pl.ds(start, ZERO_ROWS)], zsem)

        def tails(e, carry):
            def body(r, c):
                action(tail_copy(r * RUN_ALIGN))
                return c

            lax.fori_loop(used_end_ref[e] // RUN_ALIGN, seg_end_ref[e] // RUN_ALIGN, body, 0)
            return carry

        def blocks(block, carry):
            action(block_copy(block))
            return carry

        lax.fori_loop(0, N_EXPERTS, tails, 0)
        lax.fori_loop(seg_end_ref[N_EXPERTS - 1] // ZERO_ROWS, N_ROWS // ZERO_ROWS, blocks, 0)

    @pl.when(i == 0)
    def _():
        zero_ref[...] = jnp.zeros_like(zero_ref)
        zero_fills(lambda cp: cp.start())

    slot = lax.broadcasted_iota(I32, (LOCAL_ROWS, OUT_TM), 0).astype(F32)
    perm = jnp.where((slot == routet_ref[0:1, :]) | (slot == routet_ref[1:2, :]), 1.0, 0.0).astype(BF16)
    sbuf_ref[buf] = jnp.dot(perm, h2_ref[...], preferred_element_type=F32).astype(BF16)

    def piece(b):
        def make(local, sorted_row, rows):
            return pltpu.make_async_copy(
                sbuf_ref.at[b, pl.ds(pl.multiple_of(local, RUN_ALIGN), rows)],
                xs_ref.at[pl.ds(pl.multiple_of(sorted_row, RUN_ALIGN), rows)], sems.at[b])
        return make

    _run_copies(tab_ref, nb_ref[i], ns_ref[i], piece(buf), lambda cp: cp.start())
    prev = jnp.maximum(i - 1, 0)

    @pl.when(i > 0)
    def _():
        _run_copies(tab_ref, nb_ref[prev], ns_ref[prev], lambda lo, so, rows: piece(1 - buf)(0, 0, rows),
                    lambda cp: cp.wait())

    @pl.when(i == last)
    def _():
        _run_copies(tab_ref, nb_ref[i], ns_ref[i], lambda lo, so, rows: piece(buf)(0, 0, rows),
                    lambda cp: cp.wait())
        zero_fills(lambda cp: cp.wait())


def _piece_spec(index_map):
    return pl.BlockSpec((1, 1, TABLE_W), index_map, memory_space=pltpu.SMEM)


def _dispatch(seg_end, used_end, n_big, n_small, piece_table, route_t, h2):
    grid_spec = pltpu.PrefetchScalarGridSpec(
        num_scalar_prefetch=4,
        grid=(N_TOKEN_TILES,),
        in_specs=[_piece_spec(lambda i, se, ue, nb, ns: (i, 0, 0)),
                  pl.BlockSpec((ROUTE_W, OUT_TM), lambda i, se, ue, nb, ns: (i, 0)),
                  pl.BlockSpec((OUT_TM, D_MODEL), lambda i, se, ue, nb, ns: (i, 0))],
        out_specs=pl.BlockSpec(memory_space=pl.ANY),
        scratch_shapes=[pltpu.VMEM((2, LOCAL_ROWS, D_MODEL), BF16),
                        pltpu.VMEM((ZERO_ROWS, D_MODEL), BF16),
                        pltpu.SemaphoreType.DMA((2,)), pltpu.SemaphoreType.DMA],
    )
    return pl.pallas_call(
        _dispatch_kernel,
        grid_spec=grid_spec,
        out_shape=jax.ShapeDtypeStruct((N_ROWS, D_MODEL), BF16),
        compiler_params=_cparams(("arbitrary",)),
        name="dispatch",
    )(seg_end, used_end, n_big, n_small, piece_table, route_t, h2)


X_BUFS = 3


def _experts_kernel(te_ref, seg_ref, nxt_ref, nu_ref, xs_hbm, wg_hbm, wu_hbm, wd_hbm, ys_hbm,
                    xbuf, ybuf, wg_buf, wu_buf, wd_buf, wgu_b_ref, wd_b_ref, xsem, ysem, wsem):
    n = nu_ref[0]

    def rows(t):
        return pl.ds(pl.multiple_of(t * MOE_TM, MOE_TM), MOE_TM)

    def x_copy(t, s):
        return pltpu.make_async_copy(xs_hbm.at[rows(t)], xbuf.at[s], xsem.at[s])

    def y_copy(t, s):
        return pltpu.make_async_copy(ybuf.at[s], ys_hbm.at[rows(t)], ysem.at[s])

    def weight_copies(expert, s):
        return [pltpu.make_async_copy(wg_hbm.at[expert], wg_buf.at[s], wsem.at[s]),
                pltpu.make_async_copy(wu_hbm.at[expert], wu_buf.at[s], wsem.at[s]),
                pltpu.make_async_copy(wd_hbm.at[expert], wd_buf.at[s], wsem.at[s])]

    for cp in weight_copies(te_ref[0], 0):
        cp.start()
    for t in range(X_BUFS - 1):
        @pl.when(t < n)
        def _():
            x_copy(t, t).start()

    def tile(i, carry):
        xs_slot = lax.rem(i, X_BUFS)
        ys_slot = lax.rem(i, 2)
        w_slot = lax.rem(seg_ref[i], 2)
        x_copy(i, xs_slot).wait()
        ahead = i + (X_BUFS - 1)

        @pl.when(ahead < n)
        def _():
            x_copy(ahead, lax.rem(ahead, X_BUFS)).start()

        @pl.when((i == 0) | (te_ref[i] != te_ref[jnp.maximum(i - 1, 0)]))
        def _():
            for cp in weight_copies(te_ref[i], w_slot):
                cp.wait()

            @pl.when(nxt_ref[i] >= 0)
            def _():
                for cp in weight_copies(nxt_ref[i], 1 - w_slot):
                    cp.start()

            wgu_b_ref[:, 0:D_EXPERT] = wg_buf[w_slot].astype(BF16)
            wgu_b_ref[:, D_EXPERT:2 * D_EXPERT] = wu_buf[w_slot].astype(BF16)
            wd_b_ref[...] = wd_buf[w_slot].astype(BF16)

        @pl.when(i >= 2)
        def _():
            y_copy(i - 2, ys_slot).wait()

        h = jnp.dot(xbuf[xs_slot], wgu_b_ref[...], preferred_element_type=F32)
        act = (_silu(h[:, 0:D_EXPERT]) * h[:, D_EXPERT:2 * D_EXPERT]).astype(BF16)
        ybuf[ys_slot] = jnp.dot(act, wd_b_ref[...], preferred_element_type=F32).astype(BF16)
        y_copy(i, ys_slot).start()
        return carry

    lax.fori_loop(0, n, tile, 0)

    @pl.when(n >= 2)
    def _():
        y_copy(n - 2, lax.rem(n - 2, 2)).wait()

    y_copy(n - 1, lax.rem(n - 1, 2)).wait()


def _experts(tile_expert, tile_segment, next_expert, n_used, xs, w_gate, w_up, w_down):
    n_prefetch = 4
    anywhere = pl.BlockSpec(memory_space=pl.ANY)
    grid_spec = pltpu.PrefetchScalarGridSpec(
        num_scalar_prefetch=n_prefetch,
        grid=(1,),
        in_specs=[anywhere, anywhere, anywhere, anywhere],
        out_specs=anywhere,
        scratch_shapes=[pltpu.VMEM((X_BUFS, MOE_TM, D_MODEL), BF16), pltpu.VMEM((2, MOE_TM, D_MODEL), BF16),
                        pltpu.VMEM((2, D_MODEL, D_EXPERT), F32), pltpu.VMEM((2, D_MODEL, D_EXPERT), F32),
                        pltpu.VMEM((2, D_EXPERT, D_MODEL), F32),
                        pltpu.VMEM((D_MODEL, 2 * D_EXPERT), BF16), pltpu.VMEM((D_EXPERT, D_MODEL), BF16),
                        pltpu.SemaphoreType.DMA((X_BUFS,)), pltpu.SemaphoreType.DMA((2,)),
                        pltpu.SemaphoreType.DMA((2,))],
    )
    return pl.pallas_call(
        _experts_kernel,
        grid_spec=grid_spec,
        out_shape=jax.ShapeDtypeStruct((N_ROWS, D_MODEL), BF16),
        input_output_aliases={n_prefetch: 0},
        compiler_params=_cparams(("arbitrary",)),
        name="experts",
    )(tile_expert, tile_segment, next_expert, n_used, xs, w_gate, w_up, w_down)


def _combine_kernel(nb_ref, ns_ref, tab_ref, tab_next_ref, route_ref, x1_ref, g2_ref, ys_ref, o_ref,
                    gbuf_ref, sems):
    i = pl.program_id(0)
    last = pl.num_programs(0) - 1
    buf = lax.rem(i, 2)

    def piece(b):
        def make(local, sorted_row, rows):
            return pltpu.make_async_copy(
                ys_ref.at[pl.ds(pl.multiple_of(sorted_row, RUN_ALIGN), rows)],
                gbuf_ref.at[b, pl.ds(pl.multiple_of(local, RUN_ALIGN), rows)], sems.at[b])
        return make

    @pl.when(i == 0)
    def _():
        gbuf_ref[...] = jnp.zeros_like(gbuf_ref)
        _run_copies(tab_ref, nb_ref[0], ns_ref[0], piece(0), lambda cp: cp.start())

    nxt = jnp.minimum(i + 1, last)

    @pl.when(i < last)
    def _():
        _run_copies(tab_next_ref, nb_ref[nxt], ns_ref[nxt], piece(1 - buf), lambda cp: cp.start())

    rec = route_ref[...]
    slot0 = lax.broadcasted_iota(I32, (OUT_TM, COMBINE_K), 1).astype(F32)
    _run_copies(tab_ref, nb_ref[i], ns_ref[i], lambda lo, so, rows: piece(buf)(0, 0, rows), lambda cp: cp.wait())
    moe = jnp.zeros((OUT_TM, D_MODEL), F32)
    for k0 in range(0, LOCAL_ROWS, COMBINE_K):
        s0, s1 = rec[:, 0:1] - float(k0), rec[:, 1:2] - float(k0)
        weights = (jnp.where(slot0 == s0, rec[:, 2:3], 0.0)
                   + jnp.where(slot0 == s1, rec[:, 3:4], 0.0)).astype(BF16)
        moe = moe + jnp.dot(weights, gbuf_ref[buf, k0:k0 + COMBINE_K, :], preferred_element_type=F32)
    o_ref[...] = x1_ref[...] + g2_ref[0] * moe


def _combine(n_big, n_small, piece_table, route, x1, mod3, ys):
    tm = OUT_TM
    steps_per_batch = SEQ // tm
    grid_spec = pltpu.PrefetchScalarGridSpec(
        num_scalar_prefetch=2,
        grid=(N_TOKEN_TILES,),
        in_specs=[_piece_spec(lambda i, nb, ns: (i, 0, 0)),
                  _piece_spec(lambda i, nb, ns: (jnp.minimum(i + 1, N_TOKEN_TILES - 1), 0, 0)),
                  pl.BlockSpec((tm, ROUTE_W), lambda i, nb, ns: (i, 0)),
                  pl.BlockSpec((tm, D_MODEL), lambda i, nb, ns: (i, 0)),
                  pl.BlockSpec((1, 1, D_MODEL), lambda i, nb, ns: ((i // steps_per_batch) * 6 + 5, 0, 0)),
                  pl.BlockSpec(memory_space=pl.ANY)],
        out_specs=pl.BlockSpec((tm, D_MODEL), lambda i, nb, ns: (i, 0)),
        scratch_shapes=[pltpu.VMEM((2, LOCAL_ROWS, D_MODEL), BF16), pltpu.SemaphoreType.DMA((2,))],
    )
    return pl.pallas_call(
        _combine_kernel,
        grid_spec=grid_spec,
        out_shape=jax.ShapeDtypeStruct((TOKENS, D_MODEL), F32),
        compiler_params=_cparams(("arbitrary",)),
        name="combine",
    )(n_big, n_small, piece_table, piece_table, route, x1, mod3, ys)


def kernel(x, c, positions, norm1_w, norm2_w, w_ada, b_ada, w_in, conv_w, conv_b, dt_bias, a_log,
           d_skip, ssd_norm_w, q_norm_w, k_norm_w, sinks, w_out, w_group, b_group, w_expert, b_expert,
           w_gate, w_up, w_down):
    assert x.shape == (BATCH, SEQ, D_MODEL) and w_in.shape == (D_MODEL, IN_WIDTH)
    x2d = x.reshape(TOKENS, D_MODEL)
    cosf, s1, s2, mod = _rope_tables_and_mod(positions, c, w_ada, b_ada)
    mod3 = mod.reshape(BATCH * 6, 1, D_MODEL)

    qkv, z, xbc, dt, dt_t = _in_proj(x2d, norm1_w, mod3, w_in)
    att = _attention(qkv, (cosf, s1, s2), q_norm_w, k_norm_w, sinks)
    y = _ssd(xbc, z, dt, dt_t, conv_w, conv_b, dt_bias, a_log, d_skip, ssd_norm_w)

    w_router = jnp.pad(jnp.concatenate([w_group, w_expert], axis=1).astype(F32),
                       ((0, 0), (0, LANES - ROUTER_COLS)))
    b_router = jnp.pad(jnp.concatenate([b_group, b_expert]).astype(F32),
                       (0, LANES - ROUTER_COLS)).reshape(1, LANES)
    x1, h2, route, route_t, tcnt = _out_router(att, y, x2d, mod3, w_out.astype(F32), norm2_w,
                                                w_router, b_router)

    tc = tcnt[:, 0, 0:N_EXPERTS].astype(I32)
    run_rows = ((tc + RUN_ALIGN - 1) // RUN_ALIGN) * RUN_ALIGN
    counts = jnp.sum(run_rows, axis=0)
    padded = ((counts + MOE_TM - 1) // MOE_TM) * MOE_TM
    seg_end = jnp.cumsum(padded)
    seg_start = seg_end - padded
    run_dst = seg_start[None, :] + jnp.cumsum(run_rows, axis=0) - run_rows
    n_used = (seg_end[-1] // MOE_TM).reshape(1)
    last_row = jnp.minimum(jnp.arange(N_TILES, dtype=I32) * MOE_TM, seg_end[-1] - 1)
    tile_expert = jnp.sum((seg_end[None, :] <= last_row[:, None]).astype(I32), axis=1)

    run_local = jnp.cumsum(run_rows, axis=1) - run_rows
    n_big_run = run_rows // BIG_PIECE
    n_small_run = (run_rows // RUN_ALIGN) % 2
    q = jnp.arange(PIECE_SLOTS, dtype=I32)
    experts = jnp.arange(N_EXPERTS, dtype=I32)

    def flat(per_run, local0, dst0, stride):
        end = jnp.cumsum(per_run, axis=1)
        run_of = jnp.sum((end[:, None, :] <= q[None, :, None]).astype(I32), axis=2)
        pick = (run_of[:, :, None] == experts[None, None, :]).astype(I32)
        k = q[None, :] - jnp.sum(pick * (end - per_run)[:, None, :], axis=2)
        local = jnp.sum(pick * local0[:, None, :], axis=2) + stride * k
        dst = jnp.sum(pick * dst0[:, None, :], axis=2) + stride * k
        return end[:, -1], local, dst

    n_big, big_local, big_dst = flat(n_big_run, run_local, run_dst, BIG_PIECE)
    n_small, small_local, small_dst = flat(n_small_run, run_local + BIG_PIECE * n_big_run,
                                           run_dst + BIG_PIECE * n_big_run, 0)
    piece_table = jnp.concatenate([big_local, big_dst, small_local, small_dst], axis=1)
    piece_table = piece_table.astype(I32).reshape(N_TOKEN_TILES, 1, TABLE_W)

    nonempty = padded > 0
    seg_rank = jnp.cumsum(nonempty.astype(I32)) - 1
    later = nonempty[None, :] & (experts[None, :] > experts[:, None])
    next_of = jnp.min(jnp.where(later, experts[None, :], N_EXPERTS), axis=1)
    next_of = jnp.where(next_of == N_EXPERTS, -1, next_of)
    tile_is = (tile_expert[:, None] == experts[None, :]).astype(I32)
    tile_segment = jnp.sum(tile_is * seg_rank[None, :], axis=1)
    next_expert = jnp.sum(tile_is * next_of[None, :], axis=1)

    n_big, n_small = n_big.astype(I32), n_small.astype(I32)
    xs = _dispatch(seg_end.astype(I32), (seg_start + counts).astype(I32), n_big, n_small, piece_table,
                   route_t, h2)
    ys = _experts(tile_expert, tile_segment.astype(I32), next_expert.astype(I32), n_used.astype(I32),
                  xs, w_gate, w_up, w_down)
    out = _combine(n_big, n_small, piece_table, route, x1, mod3, ys)
    return out.reshape(BATCH, SEQ, D_MODEL)
```

```python
import jax
import jax.numpy as jnp
from jax import lax
from jax.experimental import pallas as pl
from jax.experimental.pallas import tpu as pltpu

F32 = jnp.float32
BF16 = jnp.bfloat16
I32 = jnp.int32

D_MODEL = 1024
BATCH = 2
SEQ = 8192
TOKENS = BATCH * SEQ
ATT_HEADS = 8
ATT_KV_HEADS = 2
HEAD_DIM = 64
ATT_WIDTH = ATT_HEADS * HEAD_DIM
KV_WIDTH = ATT_KV_HEADS * HEAD_DIM
ATT_BLOCK = 128
ROPE_DIM = HEAD_DIM // 4
ROPE_THETA = 500000.0
SSD_HEADS = 8
SSD_HEAD_DIM = 64
SSD_WIDTH = SSD_HEADS * SSD_HEAD_DIM
SSD_GROUPS = 2
SSD_STATE = 128
CONV_K = 4
CHUNK = 128
XBC_WIDTH = SSD_WIDTH + 2 * SSD_GROUPS * SSD_STATE
IN_WIDTH = ATT_WIDTH + 2 * KV_WIDTH + SSD_WIDTH + XBC_WIDTH + SSD_HEADS
N_GROUPS = 4
EXPERTS_PER_GROUP = 8
N_EXPERTS = N_GROUPS * EXPERTS_PER_GROUP
TOP_K = 2
D_EXPERT = 256
EPS = 1e-6

LANES = 128
QKV_WIDTH = ATT_WIDTH + 2 * KV_WIDTH
IN_PAD = QKV_WIDTH + SSD_WIDTH + XBC_WIDTH + LANES
NEG_BIG = -1e30

VMEM_LIMIT = 48 * 1024 * 1024


def _cparams(sem):
    return pltpu.CompilerParams(dimension_semantics=sem, vmem_limit_bytes=VMEM_LIMIT)


def _split_bf16(x):
    hi = x.astype(BF16)
    lo = (x - hi.astype(F32)).astype(BF16)
    return hi, lo


ADA_TN = 768


def _ada_kernel(ct_ref, w_ref, b_ref, o_ref):
    ct = ct_ref[...]
    s = ct * jax.nn.sigmoid(ct)
    w = w_ref[...]
    rows = [jnp.sum(s[:, b:b + 1] * w, axis=0, keepdims=True) for b in range(BATCH)]
    o_ref[...] = jnp.concatenate(rows, axis=0) + b_ref[...]


INPROJ_TM = 1024
_INPROJ_CHUNK = 256


def _inproj_kernel(x_ref, nw_ref, sc_ref, sh_ref, wf_ref, wdt_ref, qkv_ref, z_ref, xbc_ref, dt_ref, dtt_ref,
                   w_ref):
    @pl.when(pl.program_id(0) == 0)
    def _():
        for c0 in range(0, IN_PAD - LANES, _INPROJ_CHUNK):
            w_ref[:, c0:c0 + _INPROJ_CHUNK] = wf_ref[:, c0:c0 + _INPROJ_CHUNK].astype(BF16)
        w_ref[:, IN_PAD - LANES:IN_PAD] = wdt_ref[...].astype(BF16)

    x = x_ref[...]
    y = x * lax.rsqrt(jnp.mean(x * x, axis=-1, keepdims=True) + EPS)
    h = (y * nw_ref[...]) * (1.0 + sc_ref[0]) + sh_ref[0]
    hb = h.astype(BF16)

    def proj(c0, c1):
        return jnp.dot(hb, w_ref[:, c0:c1], preferred_element_type=F32)

    for c0 in range(0, QKV_WIDTH, _INPROJ_CHUNK):
        qkv_ref[:, c0:c0 + _INPROJ_CHUNK] = proj(c0, c0 + _INPROJ_CHUNK).astype(BF16)
    base = QKV_WIDTH
    for c0 in range(0, SSD_WIDTH, _INPROJ_CHUNK):
        z_ref[:, c0:c0 + _INPROJ_CHUNK] = proj(base + c0, base + c0 + _INPROJ_CHUNK).astype(BF16)
    base += SSD_WIDTH
    for c0 in range(0, XBC_WIDTH, _INPROJ_CHUNK):
        xbc_ref[:, c0:c0 + _INPROJ_CHUNK] = proj(base + c0, base + c0 + _INPROJ_CHUNK).astype(BF16)
    base += XBC_WIDTH
    dt = proj(base, base + LANES)
    dt_ref[...] = dt
    dtt_ref[...] = dt.T[0:SSD_HEADS, :]


def _in_proj(x2d, norm_w, mod3, w_in):
    tm = INPROJ_TM
    steps_per_batch = SEQ // tm
    w_dt = jnp.pad(w_in[:, IN_WIDTH - SSD_HEADS:IN_WIDTH].astype(F32), ((0, 0), (0, LANES - SSD_HEADS)))
    return pl.pallas_call(
        _inproj_kernel,
        grid=(TOKENS // tm,),
        in_specs=[pl.BlockSpec((tm, D_MODEL), lambda i: (i, 0)),
                  pl.BlockSpec((1, D_MODEL), lambda i: (0, 0)),
                  pl.BlockSpec((1, 1, D_MODEL), lambda i: ((i // steps_per_batch) * 6 + 1, 0, 0)),
                  pl.BlockSpec((1, 1, D_MODEL), lambda i: ((i // steps_per_batch) * 6 + 0, 0, 0)),
                  pl.BlockSpec((D_MODEL, IN_WIDTH), lambda i: (0, 0), pipeline_mode=pl.Buffered(1)),
                  pl.BlockSpec((D_MODEL, LANES), lambda i: (0, 0))],
        out_specs=[pl.BlockSpec((tm, QKV_WIDTH), lambda i: (i, 0)),
                   pl.BlockSpec((tm, SSD_WIDTH), lambda i: (i, 0)),
                   pl.BlockSpec((tm, XBC_WIDTH), lambda i: (i, 0)),
                   pl.BlockSpec((tm, LANES), lambda i: (i, 0)),
                   pl.BlockSpec((SSD_HEADS, tm), lambda i: (0, i))],
        out_shape=[jax.ShapeDtypeStruct((TOKENS, QKV_WIDTH), BF16),
                   jax.ShapeDtypeStruct((TOKENS, SSD_WIDTH), BF16),
                   jax.ShapeDtypeStruct((TOKENS, XBC_WIDTH), BF16),
                   jax.ShapeDtypeStruct((TOKENS, LANES), F32),
                   jax.ShapeDtypeStruct((SSD_HEADS, TOKENS), F32)],
        scratch_shapes=[pltpu.VMEM((D_MODEL, IN_PAD), BF16)],
        compiler_params=_cparams(("arbitrary",)),
        name="in_proj",
    )(x2d, norm_w.reshape(1, D_MODEL), mod3, mod3, w_in.astype(F32), w_dt)


ATT_SUB = 8


ROPE_TM = 2048
_ROPE_HALF = ROPE_DIM // 2
_TOK_PER_ROW = LANES // _ROPE_HALF


def _exact_dot(x, onehot_b):
    hi, lo = _split_bf16(x)
    return (jnp.dot(hi, onehot_b, preferred_element_type=F32)
            + jnp.dot(lo, onehot_b, preferred_element_type=F32))


def _rope_kernel(pos_ref, freq_ref, sel_ref, own_ref, gcos_ref, gs1_ref, gs2_ref, ident_ref,
                 cos_ref, s1_ref, s2_ref):
    ang = pos_ref[...].astype(F32) * freq_ref[...]
    cos_p, sin_p = jnp.cos(ang), jnp.sin(ang)
    hi_c, lo_c = _split_bf16(cos_p)
    hi_s, lo_s = _split_bf16(sin_p)
    sel = sel_ref[...]
    rows_c = jnp.dot(sel, hi_c, preferred_element_type=F32) + jnp.dot(sel, lo_c, preferred_element_type=F32)
    rows_s = jnp.dot(sel, hi_s, preferred_element_type=F32) + jnp.dot(sel, lo_s, preferred_element_type=F32)
    own = own_ref[...]
    cos_ref[...] = _exact_dot(rows_c * own, gcos_ref[...]) + ident_ref[...]
    s1_ref[...] = _exact_dot(rows_s * own, gs1_ref[...])
    s2_ref[...] = _exact_dot(rows_s * own, gs2_ref[...])


def _tables_kernel(pos_ref, freq_ref, sel_ref, own_ref, gcos_ref, gs1_ref, gs2_ref, ident_ref,
                   ct_ref, w_ref, b_ref, cos_ref, s1_ref, s2_ref, mod_ref):
    _rope_kernel(pos_ref, freq_ref, sel_ref, own_ref, gcos_ref, gs1_ref, gs2_ref, ident_ref,
                 cos_ref, s1_ref, s2_ref)
    _ada_kernel(ct_ref, w_ref, b_ref, mod_ref)


def _rope_tables_and_mod(positions, c, w_ada, b_ada):
    n_mod = w_ada.shape[1]
    assert TOKENS // ROPE_TM == n_mod // ADA_TN
    half, per_row = _ROPE_HALF, _TOK_PER_ROW
    rows = ROPE_TM // per_row
    pos_rep = jnp.repeat(positions.reshape(TOKENS).astype(I32), half).reshape(TOKENS // per_row, LANES)
    inv_freq = jnp.power(ROPE_THETA, -jnp.arange(half, dtype=F32) * 2.0 / ROPE_DIM)
    freq = jnp.tile(inv_freq, per_row).reshape(1, LANES)
    tok = jnp.arange(ROPE_TM)
    lane = jnp.arange(LANES)
    sel = (tok[:, None] // per_row == jnp.arange(rows)[None, :]).astype(BF16)
    own = (lane[None, :] // half == tok[:, None] % per_row).astype(F32)
    d = lane % HEAD_DIM
    src_f = lane % half
    hits = lambda lo, hi: ((src_f[:, None] == d[None, :] % half) & (d[None, :] >= lo) & (d[None, :] < hi))
    gcos = hits(0, ROPE_DIM).astype(BF16)
    gs1 = -hits(0, half).astype(BF16)
    gs2 = hits(half, ROPE_DIM).astype(BF16)
    ident = (d >= ROPE_DIM).astype(F32).reshape(1, LANES)
    const = lambda shape: pl.BlockSpec(shape, lambda i: (0, 0))
    out_spec = pl.BlockSpec((ROPE_TM, LANES), lambda i: (i, 0))
    out = jax.ShapeDtypeStruct((TOKENS, LANES), F32)
    return pl.pallas_call(
        _tables_kernel,
        grid=(TOKENS // ROPE_TM,),
        in_specs=[pl.BlockSpec((rows, LANES), lambda i: (i, 0)), const((1, LANES)),
                  const((ROPE_TM, rows)), const((ROPE_TM, LANES)),
                  const((LANES, LANES)), const((LANES, LANES)), const((LANES, LANES)), const((1, LANES)),
                  const((D_MODEL, BATCH)),
                  pl.BlockSpec((D_MODEL, ADA_TN), lambda i: (0, i)),
                  pl.BlockSpec((1, ADA_TN), lambda i: (0, i))],
        out_specs=[out_spec, out_spec, out_spec, pl.BlockSpec((BATCH, ADA_TN), lambda i: (0, i))],
        out_shape=[out, out, out, jax.ShapeDtypeStruct((BATCH, n_mod), F32)],
        compiler_params=_cparams(("arbitrary",)),
        name="rope_tables_ada_mod",
    )(pos_rep, freq, sel, own, gcos, gs1, gs2, ident, c.T, w_ada, b_ada.reshape(1, n_mod))


def _seg_meansq(xf, ones128):
    rows, width = xf.shape
    nt = width // LANES
    parts = _split_bf16(xf * xf)
    stacked = jnp.concatenate([p[:, t * LANES:(t + 1) * LANES] for p in parts for t in range(nt)], axis=0)
    tot = jnp.dot(stacked, ones128, preferred_element_type=F32)
    tiles = [tot[t * rows:(t + 1) * rows] + tot[(nt + t) * rows:(nt + t + 1) * rows] for t in range(nt)]
    return jnp.concatenate(tiles, axis=1) * (1.0 / HEAD_DIM)


def _norm_rope(x_bf, w_row, ones_bd, cosf, s1, s2):
    xf = x_bf.astype(F32)
    width = xf.shape[1]
    xn = xf * lax.rsqrt(_seg_meansq(xf, ones_bd) + EPS) * w_row
    half = ROPE_DIM // 2
    up = pltpu.roll(xn, width - half, axis=1)
    down = pltpu.roll(xn, half, axis=1)
    return xn * cosf + up * s1 + down * s2


def _attn_kernel(sink_ref, q_ref, kv_ref, cos_ref, s1_ref, s2_ref, qw_ref, kw_ref,
                 ones_ref, o_ref, kprev_ref, vprev_ref):
    j = pl.program_id(1)
    blk = ATT_BLOCK

    @pl.when(j == 0)
    def _():
        kprev_ref[...] = jnp.zeros_like(kprev_ref)
        vprev_ref[...] = jnp.zeros_like(vprev_ref)

    cos1 = cos_ref[...]
    s1_1 = s1_ref[...]
    s2_1 = s2_ref[...]
    reps = ATT_WIDTH // LANES
    cosq = jnp.concatenate([cos1] * reps, axis=1)
    s1q = jnp.concatenate([s1_1] * reps, axis=1)
    s2q = jnp.concatenate([s2_1] * reps, axis=1)

    q = _norm_rope(q_ref[...], qw_ref[...], ones_ref[...], cosq, s1q, s2q)
    qf = q * (HEAD_DIM ** -0.5)
    kv = kv_ref[...]
    kn = _norm_rope(kv[:, 0:KV_WIDTH], kw_ref[...], ones_ref[...], cos1, s1_1, s2_1)
    vn = kv[:, KV_WIDTH:2 * KV_WIDTH].astype(F32)

    kall = jnp.concatenate([kprev_ref[...], kn], axis=0)
    vall = jnp.concatenate([vprev_ref[...], vn], axis=0)
    kprev_ref[...] = kn[(ATT_SUB - 1) * blk:ATT_SUB * blk]
    vprev_ref[...] = vn[(ATT_SUB - 1) * blk:ATT_SUB * blk]

    lo_all = lax.broadcasted_iota(I32, kall.shape, 1) < HEAD_DIM
    ones_all = jnp.ones(kall.shape, BF16)

    row = lax.broadcasted_iota(I32, (2 * blk, blk), 0)
    col = lax.broadcasted_iota(I32, (2 * blk, blk), 1)
    from_prev = col > (row & (blk - 1))
    second_tile = lax.broadcasted_iota(I32, (2 * blk, 1), 0) >= blk
    zero_p = jnp.zeros((2 * blk, blk), F32)

    k_par, v_par = [], []
    for g in range(ATT_KV_HEADS):
        keep = lo_all if g == 0 else ~lo_all
        k_own = jnp.where(keep, kall, 0.0)
        v_own = jnp.where(keep, vall, 0.0)
        k_oth = pltpu.roll(k_own, HEAD_DIM, axis=1)
        v_oth = pltpu.roll(v_own, HEAD_DIM, axis=1)
        k_lo, k_hi = (k_own, k_oth) if g == 0 else (k_oth, k_own)
        v_lo, v_hi = (v_own, v_oth) if g == 0 else (v_oth, v_own)
        k_par.append((k_lo.astype(BF16), k_hi.astype(BF16)))
        v_par.append((jnp.concatenate([v_lo.astype(BF16), ones_all], axis=1),
                      jnp.concatenate([v_hi.astype(BF16), ones_all], axis=1)))

    problems = [(g, sub) for g in range(ATT_KV_HEADS) for sub in range(ATT_SUB)]
    scores = []
    for g, sub in problems:
        r0, c0 = sub * blk, g * 2 * LANES
        qcat = jnp.concatenate([qf[r0:r0 + blk, c0:c0 + LANES],
                                qf[r0:r0 + blk, c0 + LANES:c0 + 2 * LANES]], axis=0).astype(BF16)
        kw = jnp.concatenate([k_par[g][0][r0:r0 + 2 * blk], k_par[g][1][r0:r0 + 2 * blk]], axis=0)
        scores.append(lax.dot_general(qcat, kw, (((1,), (1,)), ((), ())),
                                      preferred_element_type=F32))

    weights, rescale = [], []
    for (g, sub), s_all in zip(problems, scores):
        for par in range(2):
            s = s_all[:, par * 2 * blk:(par + 1) * 2 * blk]
            s_prev = s[:, 0:blk]
            if sub == 0:
                s_prev = s_prev + jnp.where(j > 0, 0.0, NEG_BIG)
            s = jnp.where(from_prev, s_prev, s[:, blk:2 * blk])
            h_first = ATT_HEADS // ATT_KV_HEADS * g + par
            sink = jnp.where(second_tile, sink_ref[h_first + 2], sink_ref[h_first])
            m = jnp.maximum(jnp.max(s, axis=-1, keepdims=True), sink)
            p = jnp.exp(s - m)
            weights.append(jnp.concatenate([jnp.where(from_prev, p, zero_p), jnp.where(from_prev, zero_p, p)],
                                           axis=1).astype(BF16))
            rescale.append(jnp.exp(sink - m))

    outs = []
    for idx, (g, sub) in enumerate(problems):
        for par in range(2):
            outs.append(jnp.dot(weights[2 * idx + par], v_par[g][par][sub * blk:(sub + 2) * blk],
                                preferred_element_type=F32))

    for idx, (g, sub) in enumerate(problems):
        r0, c0 = sub * blk, g * 2 * LANES
        pair = None
        for par in range(2):
            o = outs[2 * idx + par]
            part = o[:, 0:LANES] * (1.0 / (o[:, LANES:2 * LANES] + rescale[2 * idx + par]))
            pair = part if pair is None else pair + part
        o_ref[r0:r0 + blk, c0:c0 + LANES] = pair[0:blk].astype(BF16)
        o_ref[r0:r0 + blk, c0 + LANES:c0 + 2 * LANES] = pair[blk:2 * blk].astype(BF16)


def _attention(qkv, rope_tables, q_norm_w, k_norm_w, sinks):
    cosf, s1, s2 = rope_tables
    qw = jnp.tile(q_norm_w.astype(F32), ATT_HEADS).reshape(1, ATT_WIDTH)
    kw = jnp.tile(k_norm_w.astype(F32), ATT_KV_HEADS).reshape(1, KV_WIDTH)
    seg = jnp.arange(LANES) // HEAD_DIM
    ones128 = (seg[:, None] == seg[None, :]).astype(BF16)
    const = lambda shape: pl.BlockSpec(shape, lambda b, j, s: (0, 0))
    rows = ATT_SUB * ATT_BLOCK
    nb = SEQ // rows
    tok = lambda width, cb: pl.BlockSpec((rows, width), lambda b, j, s: (b * nb + j, cb))
    grid_spec = pltpu.PrefetchScalarGridSpec(
        num_scalar_prefetch=1,
        grid=(BATCH, nb),
        in_specs=[tok(ATT_WIDTH, 0), tok(2 * KV_WIDTH, 2), tok(LANES, 0), tok(LANES, 0), tok(LANES, 0),
                  const((1, ATT_WIDTH)), const((1, KV_WIDTH)), const((LANES, LANES))],
        out_specs=tok(ATT_WIDTH, 0),
        scratch_shapes=[pltpu.VMEM((ATT_BLOCK, KV_WIDTH), F32),
                        pltpu.VMEM((ATT_BLOCK, KV_WIDTH), F32)],
    )
    return pl.pallas_call(
        _attn_kernel,
        grid_spec=grid_spec,
        out_shape=jax.ShapeDtypeStruct((TOKENS, ATT_WIDTH), BF16),
        compiler_params=_cparams(("arbitrary", "arbitrary")),
        name="attention",
    )(sinks.astype(F32), qkv, qkv, cosf, s1, s2, qw, kw, ones128)


SSD_SUB = 4


def _softplus(x):
    return jnp.maximum(x, 0.0) + jnp.log1p(jnp.exp(-jnp.abs(x)))


def _silu(x):
    h = 0.5 * x
    return h + h * jnp.tanh(h)


def _ssd_kernel(xbc_ref, z_ref, dt_ref, dtt_ref, cw_ref, cb_ref, dtb_row_ref, dtb_col_ref,
                alog_row_ref, alog_col_ref, dskip_ref, nw_ref, tril_ref, triu_ref,
                o_ref, conv_ref, state_ref):
    c = pl.program_id(1)
    L = CHUNK
    tail = 8

    @pl.when(c == 0)
    def _():
        conv_ref[0:tail, :] = jnp.zeros((tail, XBC_WIDTH), F32)
        state_ref[...] = jnp.zeros_like(state_ref)

    row = lax.broadcasted_iota(I32, (L, L), 0)
    col = lax.broadcasted_iota(I32, (L, L), 1)
    causal = col <= row
    lane = lax.broadcasted_iota(I32, (L, LANES), 1)
    lo_half = lane < SSD_HEAD_DIM

    prepared = [_ssd_prepare(s * L, xbc_ref, dt_ref, dtt_ref, cw_ref, cb_ref, dtb_row_ref, dtb_col_ref,
                             alog_row_ref, alog_col_ref, tril_ref, triu_ref, conv_ref)
                for s in range(SSD_SUB)]
    for s in range(SSD_SUB):
        _ssd_chunk(s * L, prepared[s], causal, lo_half, z_ref, dskip_ref, nw_ref, o_ref, state_ref)


def _ssd_prepare(r0, xbc_ref, dt_ref, dtt_ref, cw_ref, cb_ref, dtb_row_ref, dtb_col_ref,
                 alog_row_ref, alog_col_ref, tril_ref, triu_ref, conv_ref):
    L = CHUNK
    tail = 8
    xb = xbc_ref[r0:r0 + L, :].astype(F32)
    conv_ref[tail:tail + L, :] = xb
    acc = cb_ref[...] + cw_ref[CONV_K - 1:CONV_K, :] * xb
    for k in range(CONV_K - 1):
        off = tail - (CONV_K - 1) + k
        acc = acc + cw_ref[k:k + 1, :] * conv_ref[off:off + L, :]
    conv_ref[0:tail, :] = xb[L - tail:L, :]
    u = _silu(acc)
    xs = u[:, 0:SSD_WIDTH]
    bmat = u[:, SSD_WIDTH:SSD_WIDTH + SSD_GROUPS * SSD_STATE]
    cmat = u[:, SSD_WIDTH + SSD_GROUPS * SSD_STATE:XBC_WIDTH]

    dt = _softplus(dt_ref[r0:r0 + L, :] + dtb_row_ref[...])
    a = dt * (-jnp.exp(alog_row_ref[...]))
    a_hi, a_lo = _split_bf16(a)
    a_cum = (jnp.dot(tril_ref[...], a_hi, preferred_element_type=F32)
             + jnp.dot(tril_ref[...], a_lo, preferred_element_type=F32))
    dt_t = _softplus(dtt_ref[:, r0:r0 + L] + dtb_col_ref[...])
    a_t = dt_t * (-jnp.exp(alog_col_ref[...]))
    at_hi, at_lo = _split_bf16(a_t)
    a_cum_t = (jnp.dot(at_hi, triu_ref[...], preferred_element_type=F32)
               + jnp.dot(at_lo, triu_ref[...], preferred_element_type=F32))
    a_end_t = a_cum_t[:, L - 1:L]
    return dict(
        xs=xs, bmat=bmat, cmat=cmat, a_cum=a_cum, exp_a_cum=jnp.exp(a_cum),
        shifted_t=a_cum_t - jnp.log(dt_t),
        wst_t=jnp.exp(a_end_t - a_cum_t) * dt_t,
        cdec_t=jnp.exp(a_end_t))


def _ssd_chunk(r0, p, causal, lo_half, z_ref, dskip_ref, nw_ref, o_ref, state_ref):
    L = CHUNK
    xs, bmat, cmat, a_cum, exp_a_cum = p["xs"], p["bmat"], p["cmat"], p["a_cum"], p["exp_a_cum"]
    shifted_t, wst_t, cdec_t = p["shifted_t"], p["wst_t"], p["cdec_t"]
    xs_b = xs.astype(BF16)
    heads_per_group = SSD_HEADS // SSD_GROUPS
    gated = []
    for g in range(SSD_GROUPS):
        b_g = bmat[:, g * SSD_STATE:(g + 1) * SSD_STATE]
        c_g = cmat[:, g * SSD_STATE:(g + 1) * SSD_STATE]
        cb = lax.dot_general(c_g.astype(BF16), b_g.astype(BF16), (((1,), (1,)), ((), ())),
                             preferred_element_type=F32)
        b_gt = b_g.T
        for t in range(heads_per_group // 2):
            tile = g * (heads_per_group // 2) + t
            c0 = tile * LANES
            xs_tile = xs_b[:, c0:c0 + LANES]
            st_tile = state_ref[:, c0:c0 + LANES]
            st_b = st_tile.astype(BF16)
            y_tile = jnp.zeros((L, LANES), F32)
            new_tile = jnp.zeros((SSD_STATE, LANES), F32)
            for e in range(2):
                h = 2 * tile + e
                keep = lo_half if e == 0 else ~lo_half
                colb = jnp.broadcast_to(a_cum[:, h:h + 1], (L, L))
                rowb = shifted_t[h:h + 1, :]
                w_in = cb * jnp.exp(jnp.where(causal, colb - rowb, NEG_BIG))
                w_off = c_g * jnp.broadcast_to(exp_a_cum[:, h:h + 1], (L, L))
                lhs = jnp.concatenate([w_in, w_off], axis=1).astype(BF16)
                rhs = jnp.concatenate([jnp.where(keep, xs_tile, jnp.zeros_like(xs_tile)),
                                       jnp.where(keep, st_b, jnp.zeros_like(st_b))], axis=0)
                y_tile = y_tile + jnp.dot(lhs, rhs, preferred_element_type=F32)
                m_h = (b_gt * wst_t[h:h + 1, :]).astype(BF16)
                new_tile = new_tile + jnp.dot(m_h, jnp.where(keep, xs_tile, jnp.zeros_like(xs_tile)),
                                              preferred_element_type=F32)
            cd = jnp.where(lo_half[0:1, :], cdec_t[2 * tile:2 * tile + 1, :],
                           cdec_t[2 * tile + 1:2 * tile + 2, :])
            state_ref[:, c0:c0 + LANES] = st_tile * cd + new_tile
            y_full = y_tile + dskip_ref[:, c0:c0 + LANES] * xs[:, c0:c0 + LANES]
            gated.append(y_full * _silu(z_ref[r0:r0 + L, c0:c0 + LANES].astype(F32)))

    gw = SSD_WIDTH // SSD_GROUPS
    tiles_per_group = gw // LANES
    for g in range(SSD_GROUPS):
        yg = jnp.concatenate(gated[g * tiles_per_group:(g + 1) * tiles_per_group], axis=1)
        ms = jnp.mean(yg * yg, axis=-1, keepdims=True)
        o_ref[r0:r0 + L, g * gw:(g + 1) * gw] = (
            (yg * lax.rsqrt(ms + EPS)) * nw_ref[:, g * gw:(g + 1) * gw]).astype(o_ref.dtype)


def _ssd(xbc, z, dt, dt_t, conv_w, conv_b, dt_bias, a_log, d_skip, ssd_norm_w):
    L = SSD_SUB * CHUNK
    nc = SEQ // L
    pad_row = lambda v: jnp.pad(v.astype(F32), (0, LANES - SSD_HEADS)).reshape(1, LANES)
    col8 = lambda v: v.astype(F32).reshape(SSD_HEADS, 1)
    idx = jnp.arange(CHUNK)
    tril = (idx[None, :] <= idx[:, None]).astype(BF16)
    triu = (idx[:, None] <= idx[None, :]).astype(BF16)
    dskip = jnp.repeat(d_skip.astype(F32), SSD_HEAD_DIM).reshape(1, SSD_WIDTH)
    const = lambda shape: pl.BlockSpec(shape, lambda b, c: (0, 0))
    tok = lambda width: pl.BlockSpec((L, width), lambda b, c: (b * nc + c, 0))
    return pl.pallas_call(
        _ssd_kernel,
        grid=(BATCH, nc),
        in_specs=[tok(XBC_WIDTH), tok(SSD_WIDTH), tok(LANES),
                  pl.BlockSpec((SSD_HEADS, L), lambda b, c: (0, b * nc + c)),
                  const((CONV_K, XBC_WIDTH)), const((1, XBC_WIDTH)),
                  const((1, LANES)), const((SSD_HEADS, 1)), const((1, LANES)), const((SSD_HEADS, 1)),
                  const((1, SSD_WIDTH)), const((1, SSD_WIDTH)), const((CHUNK, CHUNK)), const((CHUNK, CHUNK))],
        out_specs=tok(SSD_WIDTH),
        out_shape=jax.ShapeDtypeStruct((TOKENS, SSD_WIDTH), BF16),
        scratch_shapes=[pltpu.VMEM((8 + CHUNK, XBC_WIDTH), F32),
                        pltpu.VMEM((SSD_STATE, SSD_WIDTH), F32)],
        compiler_params=_cparams(("arbitrary", "arbitrary")),
        name="ssd",
    )(xbc, z, dt, dt_t, conv_w.astype(F32), conv_b.astype(F32).reshape(1, XBC_WIDTH),
      pad_row(dt_bias), col8(dt_bias), pad_row(a_log), col8(a_log), dskip,
      ssd_norm_w.astype(F32).reshape(1, SSD_WIDTH), tril, triu)


OUT_TM = 512
OR_SUB = 2
ROUTE_W = 8
ROUTER_COLS = N_GROUPS + N_EXPERTS
RUN_ALIGN = 16
RUN_SHIFT = 4
LOCAL_ROWS = 1536
assert RUN_ALIGN == 1 << RUN_SHIFT and LOCAL_ROWS >= TOP_K * OUT_TM + N_EXPERTS * (RUN_ALIGN - 1)


def _lane_pick(values, lane, index):
    return jnp.sum(jnp.where(lane == index, values, 0.0), axis=-1, keepdims=True)


def _first_argmax(vals, lane):
    m = jnp.max(vals, axis=-1, keepdims=True)
    idx = jnp.min(jnp.where(vals == m, lane, float(LANES)), axis=-1, keepdims=True)
    return m, idx


def _out_router_kernel(att_ref, y_ref, x_ref, g1_ref, wof_ref, nw_ref, sc_ref, sh_ref, wr_ref, br_ref,
                       ltri_ref, sut_ref, x1_ref, h2_ref, route_ref, routet_ref, tcnt_ref,
                       wr_split_ref, logits_ref, wo_ref):
    i = pl.program_id(0)
    tm = OUT_TM

    @pl.when(i == 0)
    def _():
        hi, lo = _split_bf16(wr_ref[...])
        wr_split_ref[:, 0:LANES] = hi
        wr_split_ref[:, LANES:2 * LANES] = lo
        logits_ref[...] = jnp.zeros_like(logits_ref)
        for r0 in range(0, D_MODEL, _INPROJ_CHUNK):
            wo_ref[r0:r0 + _INPROJ_CHUNK, :] = wof_ref[r0:r0 + _INPROJ_CHUNK, :].astype(BF16)

    previous = [logits_ref[s] for s in range(OR_SUB)]

    for s in range(OR_SUB):
        rows = slice(s * tm, (s + 1) * tm)
        mixer = (jnp.dot(att_ref[rows, :], wo_ref[0:ATT_WIDTH, :], preferred_element_type=F32)
                 + jnp.dot(y_ref[rows, :], wo_ref[ATT_WIDTH:ATT_WIDTH + SSD_WIDTH, :], preferred_element_type=F32))
        x1 = x_ref[rows, :] + g1_ref[0] * mixer
        x1_ref[rows, :] = x1
        yn = x1 * lax.rsqrt(jnp.mean(x1 * x1, axis=-1, keepdims=True) + EPS)
        h2 = (yn * nw_ref[...]) * (1.0 + sc_ref[0]) + sh_ref[0]
        h2_ref[rows, :] = h2.astype(BF16)

        h_hi, h_lo = _split_bf16(h2)
        both = jnp.dot(h_hi, wr_split_ref[...], preferred_element_type=F32)
        logits_ref[s] = (both[:, 0:LANES] + both[:, LANES:2 * LANES]
                         + jnp.dot(h_lo, wr_split_ref[:, 0:LANES], preferred_element_type=F32)) + br_ref[...]

    for s in range(OR_SUB):
        _route_tile(previous[s], s, ltri_ref, sut_ref, route_ref, routet_ref, tcnt_ref)


def _route_tile(logits, s, ltri_ref, sut_ref, route_ref, routet_ref, tcnt_ref):
    tm = logits.shape[0]
    lane = lax.broadcasted_iota(I32, (tm, LANES), 1).astype(F32)

    gl = jnp.where(lane < N_GROUPS, logits, NEG_BIG)
    gmax, gidx = _first_argmax(gl, lane)
    g_p = 1.0 / jnp.sum(jnp.exp(gl - gmax), axis=-1, keepdims=True)

    lo_lane = N_GROUPS + EXPERTS_PER_GROUP * gidx
    el = jnp.where((lane >= lo_lane) & (lane < lo_lane + EXPERTS_PER_GROUP), logits, NEG_BIG)
    m1, i1 = _first_argmax(el, lane)
    m2, i2 = _first_argmax(jnp.where(lane == i1, NEG_BIG, el), lane)
    r = jnp.exp(m2 - m1)
    p1 = 1.0 / (1.0 + r)
    p2 = r / (1.0 + r)
    e0 = i1 - N_GROUPS
    e1 = i2 - N_GROUPS

    onehot = ((lane == e0) | (lane == e1)).astype(F32)
    tile_cnt = jnp.sum(onehot, axis=0, keepdims=True)
    run_len = jnp.floor((tile_cnt + (RUN_ALIGN - 1)) * (1.0 / RUN_ALIGN)) * RUN_ALIGN
    run_start = jnp.dot(jnp.broadcast_to(run_len, (8, LANES)).astype(BF16), sut_ref[...],
                        preferred_element_type=F32)[0:1, :]
    before = jnp.dot(ltri_ref[...], onehot.astype(BF16), preferred_element_type=F32) + run_start
    slot0 = _lane_pick(before, lane, e0)
    slot1 = _lane_pick(before, lane, e1)
    tcnt_ref[s] = tile_cnt

    rec = jnp.zeros((tm, LANES), F32)
    for k, v in enumerate([slot0, slot1, g_p * p1, g_p * p2, e0, e1]):
        rec = jnp.where(lane == k, v, rec)
    route_ref[s * tm:(s + 1) * tm, :] = rec[:, 0:ROUTE_W]
    routet_ref[s * ROUTE_W:(s + 1) * ROUTE_W, :] = rec.T[0:ROUTE_W, :]


def _out_router(att, y, x2d, mod3, w_out_b, norm_w, w_router, b_router):
    tm = OUT_TM
    rows = OR_SUB * tm
    n_steps = TOKENS // rows
    steps_per_batch = SEQ // rows
    idx = jnp.arange(tm)
    ltri = (idx[None, :] < idx[:, None]).astype(BF16)
    lidx = jnp.arange(LANES)
    sut = (lidx[:, None] < lidx[None, :]).astype(BF16)
    const = lambda shape: pl.BlockSpec(shape, lambda i: (0, 0))
    cur = lambda i: jnp.minimum(i, n_steps - 1)
    prev = lambda i: jnp.maximum(i - 1, 0)
    tok = lambda width: pl.BlockSpec((rows, width), lambda i: (cur(i), 0))
    modspec = lambda k: pl.BlockSpec((1, 1, D_MODEL), lambda i: ((cur(i) // steps_per_batch) * 6 + k, 0, 0))
    return pl.pallas_call(
        _out_router_kernel,
        grid=(n_steps + 1,),
        in_specs=[tok(ATT_WIDTH), tok(SSD_WIDTH), tok(D_MODEL), modspec(2),
                  const((D_MODEL, D_MODEL)), const((1, D_MODEL)), modspec(4), modspec(3),
                  const((D_MODEL, LANES)), const((1, LANES)), const((tm, tm)), const((LANES, LANES))],
        out_specs=[tok(D_MODEL), tok(D_MODEL),
                   pl.BlockSpec((rows, ROUTE_W), lambda i: (prev(i), 0)),
                   pl.BlockSpec((OR_SUB * ROUTE_W, tm), lambda i: (prev(i), 0)),
                   pl.BlockSpec((OR_SUB, 1, LANES), lambda i: (prev(i), 0, 0))],
        out_shape=[jax.ShapeDtypeStruct((TOKENS, D_MODEL), F32),
                   jax.ShapeDtypeStruct((TOKENS, D_MODEL), BF16),
                   jax.ShapeDtypeStruct((TOKENS, ROUTE_W), F32),
                   jax.ShapeDtypeStruct((N_TOKEN_TILES * ROUTE_W, tm), F32),
                   jax.ShapeDtypeStruct((N_TOKEN_TILES, 1, LANES), F32)],
        scratch_shapes=[pltpu.VMEM((D_MODEL, 2 * LANES), BF16), pltpu.VMEM((OR_SUB, tm, LANES), F32),
                        pltpu.VMEM((D_MODEL, D_MODEL), BF16)],
        compiler_params=_cparams(("arbitrary",)),
        name="out_router",
    )(att, y, x2d, mod3, w_out_b, norm_w.reshape(1, D_MODEL), mod3, mod3, w_router, b_router, ltri, sut)


MOE_TM = 512
ZERO_ROWS = 256
N_TOKEN_TILES = TOKENS // OUT_TM
MAX_SORTED_ROWS = TOKENS * TOP_K + N_TOKEN_TILES * N_EXPERTS * (RUN_ALIGN - 1)
N_TILES = MAX_SORTED_ROWS // MOE_TM + N_EXPERTS
N_ROWS = N_TILES * MOE_TM
assert MOE_TM % ZERO_ROWS == 0


BIG_PIECE = 2 * RUN_ALIGN
PIECE_SLOTS = LOCAL_ROWS // BIG_PIECE
TABLE_W = 4 * PIECE_SLOTS
COMBINE_K = 256
LOCAL_TAIL = COMBINE_K
LOCAL_BODY = LOCAL_ROWS - LOCAL_TAIL
assert PIECE_SLOTS >= N_EXPERTS and LOCAL_ROWS % COMBINE_K == 0


def _local_rows_used(n_big, n_small):
    return n_big * BIG_PIECE + n_small * RUN_ALIGN


def _run_copies(table_ref, n_big, n_small, make_copy, action):
    def big(q, carry):
        action(make_copy(table_ref[0, 0, q], table_ref[0, 0, PIECE_SLOTS + q], BIG_PIECE))
        return carry

    def small(q, carry):
        action(make_copy(table_ref[0, 0, 2 * PIECE_SLOTS + q], table_ref[0, 0, 3 * PIECE_SLOTS + q], RUN_ALIGN))
        return carry

    lax.fori_loop(0, n_big, big, 0)
    lax.fori_loop(0, n_small, small, 0)


def _dispatch_kernel(seg_end_ref, used_end_ref, nb_ref, ns_ref, tab_ref, routet_ref, h2_ref, xs_ref,
                     sbuf_ref, zero_ref, sems, zsem):
    i = pl.program_id(0)
    last = pl.num_programs(0) - 1
    buf = lax.rem(i, 2)

    def zero_fills(action):
        def tail_copy(row):
            return pltpu.make_async_copy(zero_ref.at[pl.ds(0, RUN_ALIGN)],
                                         xs_ref.at[pl.ds(pl.multiple_of(row, RUN_ALIGN), RUN_ALIGN)], zsem)

        def block_copy(block):
            start = pl.multiple_of(block * ZERO_ROWS, ZERO_ROWS)
            return pltpu.make_async_copy(zero_ref, xs_ref.at[pl.ds(start, ZERO_ROWS)], zsem)

        def tails(e, carry):
            def body(r, c):
                action(tail_copy(r * RUN_ALIGN))
                return c

            lax.fori_loop(used_end_ref[e] // RUN_ALIGN, seg_end_ref[e] // RUN_ALIGN, body, 0)
            return carry

        def blocks(block, carry):
            action(block_copy(block))
            return carry

        lax.fori_loop(0, N_EXPERTS, tails, 0)
        lax.fori_loop(seg_end_ref[N_EXPERTS - 1] // ZERO_ROWS, N_ROWS // ZERO_ROWS, blocks, 0)

    @pl.when(i == 0)
    def _():
        zero_ref[...] = jnp.zeros_like(zero_ref)
        zero_fills(lambda cp: cp.start())

    def sort_rows(r0, rows):
        slot = (lax.broadcasted_iota(I32, (rows, OUT_TM), 0) + r0).astype(F32)
        perm = jnp.where((slot == routet_ref[0:1, :]) | (slot == routet_ref[1:2, :]), 1.0, 0.0).astype(BF16)
        sbuf_ref[buf, r0:r0 + rows, :] = jnp.dot(perm, h2_ref[...], preferred_element_type=F32).astype(BF16)

    sort_rows(0, LOCAL_BODY)

    @pl.when(_local_rows_used(nb_ref[i], ns_ref[i]) > LOCAL_BODY)
    def _():
        sort_rows(LOCAL_BODY, LOCAL_TAIL)

    def piece(b):
        def make(local, sorted_row, rows):
            return pltpu.make_async_copy(
                sbuf_ref.at[b, pl.ds(pl.multiple_of(local, RUN_ALIGN), rows)],
                xs_ref.at[pl.ds(pl.multiple_of(sorted_row, RUN_ALIGN), rows)], sems.at[b])
        return make

    _run_copies(tab_ref, nb_ref[i], ns_ref[i], piece(buf), lambda cp: cp.start())
    prev = jnp.maximum(i - 1, 0)

    @pl.when(i > 0)
    def _():
        _run_copies(tab_ref, nb_ref[prev], ns_ref[prev], lambda lo, so, rows: piece(1 - buf)(0, 0, rows),
                    lambda cp: cp.wait())

    @pl.when(i == last)
    def _():
        _run_copies(tab_ref, nb_ref[i], ns_ref[i], lambda lo, so, rows: piece(buf)(0, 0, rows),
                    lambda cp: cp.wait())
        zero_fills(lambda cp: cp.wait())


def _piece_spec(index_map):
    return pl.BlockSpec((1, 1, TABLE_W), index_map, memory_space=pltpu.SMEM)


def _dispatch(seg_end, used_end, n_big, n_small, piece_table, route_t, h2):
    grid_spec = pltpu.PrefetchScalarGridSpec(
        num_scalar_prefetch=4,
        grid=(N_TOKEN_TILES,),
        in_specs=[_piece_spec(lambda i, se, ue, nb, ns: (i, 0, 0)),
                  pl.BlockSpec((ROUTE_W, OUT_TM), lambda i, se, ue, nb, ns: (i, 0)),
                  pl.BlockSpec((OUT_TM, D_MODEL), lambda i, se, ue, nb, ns: (i, 0))],
        out_specs=pl.BlockSpec(memory_space=pl.ANY),
        scratch_shapes=[pltpu.VMEM((2, LOCAL_ROWS, D_MODEL), BF16),
                        pltpu.VMEM((ZERO_ROWS, D_MODEL), BF16),
                        pltpu.SemaphoreType.DMA((2,)), pltpu.SemaphoreType.DMA],
    )
    return pl.pallas_call(
        _dispatch_kernel,
        grid_spec=grid_spec,
        out_shape=jax.ShapeDtypeStruct((N_ROWS, D_MODEL), BF16),
        compiler_params=_cparams(("arbitrary",)),
        name="dispatch",
    )(seg_end, used_end, n_big, n_small, piece_table, route_t, h2)


X_BUFS = 3


def _experts_kernel(te_ref, seg_ref, nxt_ref, nu_ref, xs_hbm, wg_hbm, wu_hbm, wd_hbm, ys_hbm,
                    xbuf, ybuf, wg_buf, wu_buf, wd_buf, wgu_b_ref, wd_b_ref, xsem, ysem, wsem):
    n = nu_ref[0]

    def rows(t):
        return pl.ds(pl.multiple_of(t * MOE_TM, MOE_TM), MOE_TM)

    def x_copy(t, s):
        return pltpu.make_async_copy(xs_hbm.at[rows(t)], xbuf.at[s], xsem.at[s])

    def y_copy(t, s):
        return pltpu.make_async_copy(ybuf.at[s], ys_hbm.at[rows(t)], ysem.at[s])

    def weight_copies(expert, s):
        return [pltpu.make_async_copy(wg_hbm.at[expert], wg_buf.at[s], wsem.at[s]),
                pltpu.make_async_copy(wu_hbm.at[expert], wu_buf.at[s], wsem.at[s]),
                pltpu.make_async_copy(wd_hbm.at[expert], wd_buf.at[s], wsem.at[s])]

    for cp in weight_copies(te_ref[0], 0):
        cp.start()
    for t in range(X_BUFS - 1):
        @pl.when(t < n)
        def _():
            x_copy(t, t).start()

    def tile(i, carry):
        xs_slot = lax.rem(i, X_BUFS)
        ys_slot = lax.rem(i, 2)
        w_slot = lax.rem(seg_ref[i], 2)
        x_copy(i, xs_slot).wait()
        ahead = i + (X_BUFS - 1)

        @pl.when(ahead < n)
        def _():
            x_copy(ahead, lax.rem(ahead, X_BUFS)).start()

        @pl.when((i == 0) | (te_ref[i] != te_ref[jnp.maximum(i - 1, 0)]))
        def _():
            for cp in weight_copies(te_ref[i], w_slot):
                cp.wait()

            @pl.when(nxt_ref[i] >= 0)
            def _():
                for cp in weight_copies(nxt_ref[i], 1 - w_slot):
                    cp.start()

            wgu_b_ref[:, 0:D_EXPERT] = wg_buf[w_slot].astype(BF16)
            wgu_b_ref[:, D_EXPERT:2 * D_EXPERT] = wu_buf[w_slot].astype(BF16)
            wd_b_ref[...] = wd_buf[w_slot].astype(BF16)

        @pl.when(i >= 2)
        def _():
            y_copy(i - 2, ys_slot).wait()

        h = jnp.dot(xbuf[xs_slot], wgu_b_ref[...], preferred_element_type=F32)
        act = (_silu(h[:, 0:D_EXPERT]) * h[:, D_EXPERT:2 * D_EXPERT]).astype(BF16)
        ybuf[ys_slot] = jnp.dot(act, wd_b_ref[...], preferred_element_type=F32).astype(BF16)
        y_copy(i, ys_slot).start()
        return carry

    lax.fori_loop(0, n, tile, 0)

    @pl.when(n >= 2)
    def _():
        y_copy(n - 2, lax.rem(n - 2, 2)).wait()

    y_copy(n - 1, lax.rem(n - 1, 2)).wait()


def _experts(tile_expert, tile_segment, next_expert, n_used, xs, w_gate, w_up, w_down):
    n_prefetch = 4
    anywhere = pl.BlockSpec(memory_space=pl.ANY)
    grid_spec = pltpu.PrefetchScalarGridSpec(
        num_scalar_prefetch=n_prefetch,
        grid=(1,),
        in_specs=[anywhere, anywhere, anywhere, anywhere],
        out_specs=anywhere,
        scratch_shapes=[pltpu.VMEM((X_BUFS, MOE_TM, D_MODEL), BF16), pltpu.VMEM((2, MOE_TM, D_MODEL), BF16),
                        pltpu.VMEM((2, D_MODEL, D_EXPERT), F32), pltpu.VMEM((2, D_MODEL, D_EXPERT), F32),
                        pltpu.VMEM((2, D_EXPERT, D_MODEL), F32),
                        pltpu.VMEM((D_MODEL, 2 * D_EXPERT), BF16), pltpu.VMEM((D_EXPERT, D_MODEL), BF16),
                        pltpu.SemaphoreType.DMA((X_BUFS,)), pltpu.SemaphoreType.DMA((2,)),
                        pltpu.SemaphoreType.DMA((2,))],
    )
    return pl.pallas_call(
        _experts_kernel,
        grid_spec=grid_spec,
        out_shape=jax.ShapeDtypeStruct((N_ROWS, D_MODEL), BF16),
        input_output_aliases={n_prefetch: 0},
        compiler_params=_cparams(("arbitrary",)),
        name="experts",
    )(tile_expert, tile_segment, next_expert, n_used, xs, w_gate, w_up, w_down)


def _combine_kernel(nb_ref, ns_ref, tab_ref, tab_next_ref, route_ref, x1_ref, g2_ref, ys_ref, o_ref,
                    gbuf_ref, sems):
    i = pl.program_id(0)
    last = pl.num_programs(0) - 1
    buf = lax.rem(i, 2)

    def piece(b):
        def make(local, sorted_row, rows):
            return pltpu.make_async_copy(
                ys_ref.at[pl.ds(pl.multiple_of(sorted_row, RUN_ALIGN), rows)],
                gbuf_ref.at[b, pl.ds(pl.multiple_of(local, RUN_ALIGN), rows)], sems.at[b])
        return make

    @pl.when(i == 0)
    def _():
        gbuf_ref[...] = jnp.zeros_like(gbuf_ref)
        _run_copies(tab_ref, nb_ref[0], ns_ref[0], piece(0), lambda cp: cp.start())

    nxt = jnp.minimum(i + 1, last)

    @pl.when(i < last)
    def _():
        _run_copies(tab_next_ref, nb_ref[nxt], ns_ref[nxt], piece(1 - buf), lambda cp: cp.start())

    rec = route_ref[...]
    slot0 = lax.broadcasted_iota(I32, (OUT_TM, COMBINE_K), 1).astype(F32)
    _run_copies(tab_ref, nb_ref[i], ns_ref[i], lambda lo, so, rows: piece(buf)(0, 0, rows), lambda cp: cp.wait())

    def slice_sum(k0):
        s0, s1 = rec[:, 0:1] - float(k0), rec[:, 1:2] - float(k0)
        weights = (jnp.where(slot0 == s0, rec[:, 2:3], 0.0)
                   + jnp.where(slot0 == s1, rec[:, 3:4], 0.0)).astype(BF16)
        return jnp.dot(weights, gbuf_ref[buf, k0:k0 + COMBINE_K, :], preferred_element_type=F32)

    moe = jnp.zeros((OUT_TM, D_MODEL), F32)
    for k0 in range(0, LOCAL_BODY, COMBINE_K):
        moe = moe + slice_sum(k0)
    o_ref[...] = x1_ref[...] + g2_ref[0] * moe

    @pl.when(_local_rows_used(nb_ref[i], ns_ref[i]) > LOCAL_BODY)
    def _():
        o_ref[...] += g2_ref[0] * slice_sum(LOCAL_BODY)


def _combine(n_big, n_small, piece_table, route, x1, mod3, ys):
    tm = OUT_TM
    steps_per_batch = SEQ // tm
    grid_spec = pltpu.PrefetchScalarGridSpec(
        num_scalar_prefetch=2,
        grid=(N_TOKEN_TILES,),
        in_specs=[_piece_spec(lambda i, nb, ns: (i, 0, 0)),
                  _piece_spec(lambda i, nb, ns: (jnp.minimum(i + 1, N_TOKEN_TILES - 1), 0, 0)),
                  pl.BlockSpec((tm, ROUTE_W), lambda i, nb, ns: (i, 0)),
                  pl.BlockSpec((tm, D_MODEL), lambda i, nb, ns: (i, 0)),
                  pl.BlockSpec((1, 1, D_MODEL), lambda i, nb, ns: ((i // steps_per_batch) * 6 + 5, 0, 0)),
                  pl.BlockSpec(memory_space=pl.ANY)],
        out_specs=pl.BlockSpec((tm, D_MODEL), lambda i, nb, ns: (i, 0)),
        scratch_shapes=[pltpu.VMEM((2, LOCAL_ROWS, D_MODEL), BF16), pltpu.SemaphoreType.DMA((2,))],
    )
    return pl.pallas_call(
        _combine_kernel,
        grid_spec=grid_spec,
        out_shape=jax.ShapeDtypeStruct((TOKENS, D_MODEL), F32),
        compiler_params=_cparams(("arbitrary",)),
        name="combine",
    )(n_big, n_small, piece_table, piece_table, route, x1, mod3, ys)


def kernel(x, c, positions, norm1_w, norm2_w, w_ada, b_ada, w_in, conv_w, conv_b, dt_bias, a_log,
           d_skip, ssd_norm_w, q_norm_w, k_norm_w, sinks, w_out, w_group, b_group, w_expert, b_expert,
           w_gate, w_up, w_down):
    assert x.shape == (BATCH, SEQ, D_MODEL) and w_in.shape == (D_MODEL, IN_WIDTH)
    x2d = x.reshape(TOKENS, D_MODEL)
    cosf, s1, s2, mod = _rope_tables_and_mod(positions, c, w_ada, b_ada)
    mod3 = mod.reshape(BATCH * 6, 1, D_MODEL)

    qkv, z, xbc, dt, dt_t = _in_proj(x2d, norm1_w, mod3, w_in)
    att = _attention(qkv, (cosf, s1, s2), q_norm_w, k_norm_w, sinks)
    y = _ssd(xbc, z, dt, dt_t, conv_w, conv_b, dt_bias, a_log, d_skip, ssd_norm_w)

    w_router = jnp.pad(jnp.concatenate([w_group, w_expert], axis=1).astype(F32),
                       ((0, 0), (0, LANES - ROUTER_COLS)))
    b_router = jnp.pad(jnp.concatenate([b_group, b_expert]).astype(F32),
                       (0, LANES - ROUTER_COLS)).reshape(1, LANES)
    x1, h2, route, route_t, tcnt = _out_router(att, y, x2d, mod3, w_out.astype(F32), norm2_w,
                                                w_router, b_router)

    tc = tcnt[:, 0, 0:N_EXPERTS].astype(I32)
    run_rows = ((tc + RUN_ALIGN - 1) // RUN_ALIGN) * RUN_ALIGN
    counts = jnp.sum(run_rows, axis=0)
    padded = ((counts + MOE_TM - 1) // MOE_TM) * MOE_TM
    seg_end = jnp.cumsum(padded)
    seg_start = seg_end - padded
    run_dst = seg_start[None, :] + jnp.cumsum(run_rows, axis=0) - run_rows
    n_used = (seg_end[-1] // MOE_TM).reshape(1)
    last_row = jnp.minimum(jnp.arange(N_TILES, dtype=I32) * MOE_TM, seg_end[-1] - 1)
    tile_expert = jnp.sum((seg_end[None, :] <= last_row[:, None]).astype(I32), axis=1)

    run_local = jnp.cumsum(run_rows, axis=1) - run_rows
    n_big_run = run_rows // BIG_PIECE
    n_small_run = (run_rows // RUN_ALIGN) % 2
    q = jnp.arange(PIECE_SLOTS, dtype=I32)
    experts = jnp.arange(N_EXPERTS, dtype=I32)

    def flat(per_run, local0, dst0, stride):
        end = jnp.cumsum(per_run, axis=1)
        run_of = jnp.sum((end[:, None, :] <= q[None, :, None]).astype(I32), axis=2)
        pick = (run_of[:, :, None] == experts[None, None, :]).astype(I32)
        k = q[None, :] - jnp.sum(pick * (end - per_run)[:, None, :], axis=2)
        local = jnp.sum(pick * local0[:, None, :], axis=2) + stride * k
        dst = jnp.sum(pick * dst0[:, None, :], axis=2) + stride * k
        return end[:, -1], local, dst

    n_big, big_local, big_dst = flat(n_big_run, run_local, run_dst, BIG_PIECE)
    n_small, small_local, small_dst = flat(n_small_run, run_local + BIG_PIECE * n_big_run,
                                           run_dst + BIG_PIECE * n_big_run, 0)
    piece_table = jnp.concatenate([big_local, big_dst, small_local, small_dst], axis=1)
    piece_table = piece_table.astype(I32).reshape(N_TOKEN_TILES, 1, TABLE_W)

    nonempty = padded > 0
    seg_rank = jnp.cumsum(nonempty.astype(I32)) - 1
    later = nonempty[None, :] & (experts[None, :] > experts[:, None])
    next_of = jnp.min(jnp.where(later, experts[None, :], N_EXPERTS), axis=1)
    next_of = jnp.where(next_of == N_EXPERTS, -1, next_of)
    tile_is = (tile_expert[:, None] == experts[None, :]).astype(I32)
    tile_segment = jnp.sum(tile_is * seg_rank[None, :], axis=1)
    next_expert = jnp.sum(tile_is * next_of[None, :], axis=1)

    n_big, n_small = n_big.astype(I32), n_small.astype(I32)
    xs = _dispatch(seg_end.astype(I32), (seg_start + counts).astype(I32), n_big, n_small, piece_table,
                   route_t, h2)
    ys = _experts(tile_expert, tile_segment.astype(I32), next_expert.astype(I32), n_used.astype(I32),
                  xs, w_gate, w_up, w_down)
    out = _combine(n_big, n_small, piece_table, route, x1, mod3, ys)
    return out.reshape(BATCH, SEQ, D_MODEL)
```

```python
import jax
import jax.numpy as jnp
from jax import lax
from jax.experimental import pallas as pl
from jax.experimental.pallas import tpu as pltpu

F32 = jnp.float32
BF16 = jnp.bfloat16
I32 = jnp.int32

D_MODEL = 1024
BATCH = 2
SEQ = 8192
TOKENS = BATCH * SEQ
ATT_HEADS = 8
ATT_KV_HEADS = 2
HEAD_DIM = 64
ATT_WIDTH = ATT_HEADS * HEAD_DIM
KV_WIDTH = ATT_KV_HEADS * HEAD_DIM
ATT_BLOCK = 128
ROPE_DIM = HEAD_DIM // 4
ROPE_THETA = 500000.0
SSD_HEADS = 8
SSD_HEAD_DIM = 64
SSD_WIDTH = SSD_HEADS * SSD_HEAD_DIM
SSD_GROUPS = 2
SSD_STATE = 128
CONV_K = 4
CHUNK = 128
XBC_WIDTH = SSD_WIDTH + 2 * SSD_GROUPS * SSD_STATE
IN_WIDTH = ATT_WIDTH + 2 * KV_WIDTH + SSD_WIDTH + XBC_WIDTH + SSD_HEADS
N_GROUPS = 4
EXPERTS_PER_GROUP = 8
N_EXPERTS = N_GROUPS * EXPERTS_PER_GROUP
TOP_K = 2
D_EXPERT = 256
EPS = 1e-6

LANES = 128
QKV_WIDTH = ATT_WIDTH + 2 * KV_WIDTH
IN_PAD = QKV_WIDTH + SSD_WIDTH + XBC_WIDTH + LANES
NEG_BIG = -1e30

VMEM_LIMIT = 48 * 1024 * 1024


def _cparams(sem):
    return pltpu.CompilerParams(dimension_semantics=sem, vmem_limit_bytes=VMEM_LIMIT)


def _split_bf16(x):
    hi = x.astype(BF16)
    lo = (x - hi.astype(F32)).astype(BF16)
    return hi, lo


ADA_TN = 768


def _ada_kernel(ct_ref, w_ref, b_ref, o_ref):
    ct = ct_ref[...]
    s = ct * jax.nn.sigmoid(ct)
    w = w_ref[...]
    rows = [jnp.sum(s[:, b:b + 1] * w, axis=0, keepdims=True) for b in range(BATCH)]
    o_ref[...] = jnp.concatenate(rows, axis=0) + b_ref[...]


INPROJ_TM = 1024
_INPROJ_CHUNK = 256


def _inproj_kernel(x_ref, nw_ref, sc_ref, sh_ref, wf_ref, wdt_ref, qkv_ref, z_ref, xbc_ref, dt_ref, dtt_ref,
                   w_ref):
    @pl.when(pl.program_id(0) == 0)
    def _():
        for c0 in range(0, IN_PAD - LANES, _INPROJ_CHUNK):
            w_ref[:, c0:c0 + _INPROJ_CHUNK] = wf_ref[:, c0:c0 + _INPROJ_CHUNK].astype(BF16)
        w_ref[:, IN_PAD - LANES:IN_PAD] = wdt_ref[...].astype(BF16)

    x = x_ref[...]
    y = x * lax.rsqrt(jnp.mean(x * x, axis=-1, keepdims=True) + EPS)
    h = (y * nw_ref[...]) * (1.0 + sc_ref[0]) + sh_ref[0]
    hb = h.astype(BF16)

    def proj(c0, c1):
        return jnp.dot(hb, w_ref[:, c0:c1], preferred_element_type=F32)

    for c0 in range(0, QKV_WIDTH, _INPROJ_CHUNK):
        qkv_ref[:, c0:c0 + _INPROJ_CHUNK] = proj(c0, c0 + _INPROJ_CHUNK).astype(BF16)
    base = QKV_WIDTH
    for c0 in range(0, SSD_WIDTH, _INPROJ_CHUNK):
        z_ref[:, c0:c0 + _INPROJ_CHUNK] = proj(base + c0, base + c0 + _INPROJ_CHUNK).astype(BF16)
    base += SSD_WIDTH
    for c0 in range(0, XBC_WIDTH, _INPROJ_CHUNK):
        xbc_ref[:, c0:c0 + _INPROJ_CHUNK] = proj(base + c0, base + c0 + _INPROJ_CHUNK).astype(BF16)
    base += XBC_WIDTH
    dt = proj(base, base + LANES)
    dt_ref[...] = dt
    dtt_ref[...] = dt.T[0:SSD_HEADS, :]


def _in_proj(x2d, norm_w, mod3, w_in):
    tm = INPROJ_TM
    steps_per_batch = SEQ // tm
    w_dt = jnp.pad(w_in[:, IN_WIDTH - SSD_HEADS:IN_WIDTH].astype(F32), ((0, 0), (0, LANES - SSD_HEADS)))
    return pl.pallas_call(
        _inproj_kernel,
        grid=(TOKENS // tm,),
        in_specs=[pl.BlockSpec((tm, D_MODEL), lambda i: (i, 0)),
                  pl.BlockSpec((1, D_MODEL), lambda i: (0, 0)),
                  pl.BlockSpec((1, 1, D_MODEL), lambda i: ((i // steps_per_batch) * 6 + 1, 0, 0)),
                  pl.BlockSpec((1, 1, D_MODEL), lambda i: ((i // steps_per_batch) * 6 + 0, 0, 0)),
                  pl.BlockSpec((D_MODEL, IN_WIDTH), lambda i: (0, 0), pipeline_mode=pl.Buffered(1)),
                  pl.BlockSpec((D_MODEL, LANES), lambda i: (0, 0))],
        out_specs=[pl.BlockSpec((tm, QKV_WIDTH), lambda i: (i, 0)),
                   pl.BlockSpec((tm, SSD_WIDTH), lambda i: (i, 0)),
                   pl.BlockSpec((tm, XBC_WIDTH), lambda i: (i, 0)),
                   pl.BlockSpec((tm, LANES), lambda i: (i, 0)),
                   pl.BlockSpec((SSD_HEADS, tm), lambda i: (0, i))],
        out_shape=[jax.ShapeDtypeStruct((TOKENS, QKV_WIDTH), BF16),
                   jax.ShapeDtypeStruct((TOKENS, SSD_WIDTH), BF16),
                   jax.ShapeDtypeStruct((TOKENS, XBC_WIDTH), BF16),
                   jax.ShapeDtypeStruct((TOKENS, LANES), F32),
                   jax.ShapeDtypeStruct((SSD_HEADS, TOKENS), F32)],
        scratch_shapes=[pltpu.VMEM((D_MODEL, IN_PAD), BF16)],
        compiler_params=_cparams(("arbitrary",)),
        name="in_proj",
    )(x2d, norm_w.reshape(1, D_MODEL), mod3, mod3, w_in.astype(F32), w_dt)


ATT_SUB = 8


ROPE_TM = 2048
_ROPE_HALF = ROPE_DIM // 2
_TOK_PER_ROW = LANES // _ROPE_HALF


def _exact_dot(x, onehot_b):
    hi, lo = _split_bf16(x)
    return (jnp.dot(hi, onehot_b, preferred_element_type=F32)
            + jnp.dot(lo, onehot_b, preferred_element_type=F32))


def _rope_kernel(pos_ref, freq_ref, sel_ref, own_ref, gcos_ref, gs1_ref, gs2_ref, ident_ref,
                 cos_ref, s1_ref, s2_ref):
    ang = pos_ref[...].astype(F32) * freq_ref[...]
    cos_p, sin_p = jnp.cos(ang), jnp.sin(ang)
    hi_c, lo_c = _split_bf16(cos_p)
    hi_s, lo_s = _split_bf16(sin_p)
    sel = sel_ref[...]
    rows_c = jnp.dot(sel, hi_c, preferred_element_type=F32) + jnp.dot(sel, lo_c, preferred_element_type=F32)
    rows_s = jnp.dot(sel, hi_s, preferred_element_type=F32) + jnp.dot(sel, lo_s, preferred_element_type=F32)
    own = own_ref[...]
    cos_ref[...] = _exact_dot(rows_c * own, gcos_ref[...]) + ident_ref[...]
    s1_ref[...] = _exact_dot(rows_s * own, gs1_ref[...])
    s2_ref[...] = _exact_dot(rows_s * own, gs2_ref[...])


def _tables_kernel(pos_ref, freq_ref, sel_ref, own_ref, gcos_ref, gs1_ref, gs2_ref, ident_ref,
                   ct_ref, w_ref, b_ref, cos_ref, s1_ref, s2_ref, mod_ref):
    _rope_kernel(pos_ref, freq_ref, sel_ref, own_ref, gcos_ref, gs1_ref, gs2_ref, ident_ref,
                 cos_ref, s1_ref, s2_ref)
    _ada_kernel(ct_ref, w_ref, b_ref, mod_ref)


def _rope_tables_and_mod(positions, c, w_ada, b_ada):
    n_mod = w_ada.shape[1]
    assert TOKENS // ROPE_TM == n_mod // ADA_TN
    half, per_row = _ROPE_HALF, _TOK_PER_ROW
    rows = ROPE_TM // per_row
    pos_rep = jnp.repeat(positions.reshape(TOKENS).astype(I32), half).reshape(TOKENS // per_row, LANES)
    inv_freq = jnp.power(ROPE_THETA, -jnp.arange(half, dtype=F32) * 2.0 / ROPE_DIM)
    freq = jnp.tile(inv_freq, per_row).reshape(1, LANES)
    tok = jnp.arange(ROPE_TM)
    lane = jnp.arange(LANES)
    sel = (tok[:, None] // per_row == jnp.arange(rows)[None, :]).astype(BF16)
    own = (lane[None, :] // half == tok[:, None] % per_row).astype(F32)
    d = lane % HEAD_DIM
    src_f = lane % half
    hits = lambda lo, hi: ((src_f[:, None] == d[None, :] % half) & (d[None, :] >= lo) & (d[None, :] < hi))
    gcos = hits(0, ROPE_DIM).astype(BF16)
    gs1 = -hits(0, half).astype(BF16)
    gs2 = hits(half, ROPE_DIM).astype(BF16)
    ident = (d >= ROPE_DIM).astype(F32).reshape(1, LANES)
    const = lambda shape: pl.BlockSpec(shape, lambda i: (0, 0))
    out_spec = pl.BlockSpec((ROPE_TM, LANES), lambda i: (i, 0))
    out = jax.ShapeDtypeStruct((TOKENS, LANES), F32)
    return pl.pallas_call(
        _tables_kernel,
        grid=(TOKENS // ROPE_TM,),
        in_specs=[pl.BlockSpec((rows, LANES), lambda i: (i, 0)), const((1, LANES)),
                  const((ROPE_TM, rows)), const((ROPE_TM, LANES)),
                  const((LANES, LANES)), const((LANES, LANES)), const((LANES, LANES)), const((1, LANES)),
                  const((D_MODEL, BATCH)),
                  pl.BlockSpec((D_MODEL, ADA_TN), lambda i: (0, i)),
                  pl.BlockSpec((1, ADA_TN), lambda i: (0, i))],
        out_specs=[out_spec, out_spec, out_spec, pl.BlockSpec((BATCH, ADA_TN), lambda i: (0, i))],
        out_shape=[out, out, out, jax.ShapeDtypeStruct((BATCH, n_mod), F32)],
        compiler_params=_cparams(("arbitrary",)),
        name="rope_tables_ada_mod",
    )(pos_rep, freq, sel, own, gcos, gs1, gs2, ident, c.T, w_ada, b_ada.reshape(1, n_mod))


def _seg_meansq(xf, ones128):
    rows, width = xf.shape
    nt = width // LANES
    parts = _split_bf16(xf * xf)
    stacked = jnp.concatenate([p[:, t * LANES:(t + 1) * LANES] for p in parts for t in range(nt)], axis=0)
    tot = jnp.dot(stacked, ones128, preferred_element_type=F32)
    tiles = [tot[t * rows:(t + 1) * rows] + tot[(nt + t) * rows:(nt + t + 1) * rows] for t in range(nt)]
    return jnp.concatenate(tiles, axis=1) * (1.0 / HEAD_DIM)


def _norm_rope(x_bf, w_row, ones_bd, cosf, s1, s2):
    xf = x_bf.astype(F32)
    width = xf.shape[1]
    xn = xf * lax.rsqrt(_seg_meansq(xf, ones_bd) + EPS) * w_row
    half = ROPE_DIM // 2
    up = pltpu.roll(xn, width - half, axis=1)
    down = pltpu.roll(xn, half, axis=1)
    return xn * cosf + up * s1 + down * s2


def _attn_kernel(sink_ref, q_ref, kv_ref, cos_ref, s1_ref, s2_ref, qw_ref, kw_ref,
                 ones_ref, o_ref, kprev_ref, vprev_ref):
    j = pl.program_id(1)
    blk = ATT_BLOCK

    @pl.when(j == 0)
    def _():
        kprev_ref[...] = jnp.zeros_like(kprev_ref)
        vprev_ref[...] = jnp.zeros_like(vprev_ref)

    cos1 = cos_ref[...]
    s1_1 = s1_ref[...]
    s2_1 = s2_ref[...]
    reps = ATT_WIDTH // LANES
    cosq = jnp.concatenate([cos1] * reps, axis=1)
    s1q = jnp.concatenate([s1_1] * reps, axis=1)
    s2q = jnp.concatenate([s2_1] * reps, axis=1)

    q = _norm_rope(q_ref[...], qw_ref[...], ones_ref[...], cosq, s1q, s2q)
    qf = q * (HEAD_DIM ** -0.5)
    kv = kv_ref[...]
    kn = _norm_rope(kv[:, 0:KV_WIDTH], kw_ref[...], ones_ref[...], cos1, s1_1, s2_1)
    vn = kv[:, KV_WIDTH:2 * KV_WIDTH].astype(F32)

    kall = jnp.concatenate([kprev_ref[...], kn], axis=0)
    vall = jnp.concatenate([vprev_ref[...], vn], axis=0)
    kprev_ref[...] = kn[(ATT_SUB - 1) * blk:ATT_SUB * blk]
    vprev_ref[...] = vn[(ATT_SUB - 1) * blk:ATT_SUB * blk]

    lo_all = lax.broadcasted_iota(I32, kall.shape, 1) < HEAD_DIM
    ones_all = jnp.ones(kall.shape, BF16)

    row = lax.broadcasted_iota(I32, (2 * blk, blk), 0)
    col = lax.broadcasted_iota(I32, (2 * blk, blk), 1)
    from_prev = col > (row & (blk - 1))
    second_tile = lax.broadcasted_iota(I32, (2 * blk, 1), 0) >= blk
    zero_p = jnp.zeros((2 * blk, blk), F32)

    k_par, v_par = [], []
    for g in range(ATT_KV_HEADS):
        keep = lo_all if g == 0 else ~lo_all
        k_own = jnp.where(keep, kall, 0.0)
        v_own = jnp.where(keep, vall, 0.0)
        k_oth = pltpu.roll(k_own, HEAD_DIM, axis=1)
        v_oth = pltpu.roll(v_own, HEAD_DIM, axis=1)
        k_lo, k_hi = (k_own, k_oth) if g == 0 else (k_oth, k_own)
        v_lo, v_hi = (v_own, v_oth) if g == 0 else (v_oth, v_own)
        k_par.append((k_lo.astype(BF16), k_hi.astype(BF16)))
        v_par.append((jnp.concatenate([v_lo.astype(BF16), ones_all], axis=1),
                      jnp.concatenate([v_hi.astype(BF16), ones_all], axis=1)))

    problems = [(g, sub) for g in range(ATT_KV_HEADS) for sub in range(ATT_SUB)]
    scores = []
    for g, sub in problems:
        r0, c0 = sub * blk, g * 2 * LANES
        qcat = jnp.concatenate([qf[r0:r0 + blk, c0:c0 + LANES],
                                qf[r0:r0 + blk, c0 + LANES:c0 + 2 * LANES]], axis=0).astype(BF16)
        kw = jnp.concatenate([k_par[g][0][r0:r0 + 2 * blk], k_par[g][1][r0:r0 + 2 * blk]], axis=0)
        scores.append(lax.dot_general(qcat, kw, (((1,), (1,)), ((), ())),
                                      preferred_element_type=F32))

    weights, rescale = [], []
    for (g, sub), s_all in zip(problems, scores):
        for par in range(2):
            s = s_all[:, par * 2 * blk:(par + 1) * 2 * blk]
            s_prev = s[:, 0:blk]
            if sub == 0:
                s_prev = s_prev + jnp.where(j > 0, 0.0, NEG_BIG)
            s = jnp.where(from_prev, s_prev, s[:, blk:2 * blk])
            h_first = ATT_HEADS // ATT_KV_HEADS * g + par
            sink = jnp.where(second_tile, sink_ref[h_first + 2], sink_ref[h_first])
            m = jnp.maximum(jnp.max(s, axis=-1, keepdims=True), sink)
            p = jnp.exp(s - m)
            weights.append(jnp.concatenate([jnp.where(from_prev, p, zero_p), jnp.where(from_prev, zero_p, p)],
                                           axis=1).astype(BF16))
            rescale.append(jnp.exp(sink - m))

    outs = []
    for idx, (g, sub) in enumerate(problems):
        for par in range(2):
            outs.append(jnp.dot(weights[2 * idx + par], v_par[g][par][sub * blk:(sub + 2) * blk],
                                preferred_element_type=F32))

    for idx, (g, sub) in enumerate(problems):
        r0, c0 = sub * blk, g * 2 * LANES
        pair = None
        for par in range(2):
            o = outs[2 * idx + par]
            part = o[:, 0:LANES] * (1.0 / (o[:, LANES:2 * LANES] + rescale[2 * idx + par]))
            pair = part if pair is None else pair + part
        o_ref[r0:r0 + blk, c0:c0 + LANES] = pair[0:blk].astype(BF16)
        o_ref[r0:r0 + blk, c0 + LANES:c0 + 2 * LANES] = pair[blk:2 * blk].astype(BF16)


def _attention(qkv, rope_tables, q_norm_w, k_norm_w, sinks):
    cosf, s1, s2 = rope_tables
    qw = jnp.tile(q_norm_w.astype(F32), ATT_HEADS).reshape(1, ATT_WIDTH)
    kw = jnp.tile(k_norm_w.astype(F32), ATT_KV_HEADS).reshape(1, KV_WIDTH)
    seg = jnp.arange(LANES) // HEAD_DIM
    ones128 = (seg[:, None] == seg[None, :]).astype(BF16)
    const = lambda shape: pl.BlockSpec(shape, lambda b, j, s: (0, 0))
    rows = ATT_SUB * ATT_BLOCK
    nb = SEQ // rows
    tok = lambda width, cb: pl.BlockSpec((rows, width), lambda b, j, s: (b * nb + j, cb))
    grid_spec = pltpu.PrefetchScalarGridSpec(
        num_scalar_prefetch=1,
        grid=(BATCH, nb),
        in_specs=[tok(ATT_WIDTH, 0), tok(2 * KV_WIDTH, 2), tok(LANES, 0), tok(LANES, 0), tok(LANES, 0),
                  const((1, ATT_WIDTH)), const((1, KV_WIDTH)), const((LANES, LANES))],
        out_specs=tok(ATT_WIDTH, 0),
        scratch_shapes=[pltpu.VMEM((ATT_BLOCK, KV_WIDTH), F32),
                        pltpu.VMEM((ATT_BLOCK, KV_WIDTH), F32)],
    )
    return pl.pallas_call(
        _attn_kernel,
        grid_spec=grid_spec,
        out_shape=jax.ShapeDtypeStruct((TOKENS, ATT_WIDTH), BF16),
        compiler_params=_cparams(("arbitrary", "arbitrary")),
        name="attention",
    )(sinks.astype(F32), qkv, qkv, cosf, s1, s2, qw, kw, ones128)


SSD_SUB = 4


def _softplus(x):
    return jnp.maximum(x, 0.0) + jnp.log1p(jnp.exp(-jnp.abs(x)))


def _silu(x):
    h = 0.5 * x
    return h + h * jnp.tanh(h)


def _ssd_kernel(xbc_ref, z_ref, dt_ref, dtt_ref, cw_ref, cb_ref, dtb_row_ref, dtb_col_ref,
                alog_row_ref, alog_col_ref, dskip_ref, nw_ref, tril_ref, triu_ref,
                o_ref, conv_ref, state_ref):
    c = pl.program_id(1)
    L = CHUNK
    tail = 8

    @pl.when(c == 0)
    def _():
        conv_ref[0:tail, :] = jnp.zeros((tail, XBC_WIDTH), F32)
        state_ref[...] = jnp.zeros_like(state_ref)

    row = lax.broadcasted_iota(I32, (L, L), 0)
    col = lax.broadcasted_iota(I32, (L, L), 1)
    causal = col <= row
    lane = lax.broadcasted_iota(I32, (L, LANES), 1)
    lo_half = lane < SSD_HEAD_DIM

    prepared = [_ssd_prepare(s * L, xbc_ref, dt_ref, dtt_ref, cw_ref, cb_ref, dtb_row_ref, dtb_col_ref,
                             alog_row_ref, alog_col_ref, tril_ref, triu_ref, conv_ref)
                for s in range(SSD_SUB)]
    for s in range(SSD_SUB):
        _ssd_chunk(s * L, prepared[s], causal, lo_half, z_ref, dskip_ref, nw_ref, o_ref, state_ref)


def _ssd_prepare(r0, xbc_ref, dt_ref, dtt_ref, cw_ref, cb_ref, dtb_row_ref, dtb_col_ref,
                 alog_row_ref, alog_col_ref, tril_ref, triu_ref, conv_ref):
    L = CHUNK
    tail = 8
    xb = xbc_ref[r0:r0 + L, :].astype(F32)
    conv_ref[tail:tail + L, :] = xb
    acc = cb_ref[...] + cw_ref[CONV_K - 1:CONV_K, :] * xb
    for k in range(CONV_K - 1):
        off = tail - (CONV_K - 1) + k
        acc = acc + cw_ref[k:k + 1, :] * conv_ref[off:off + L, :]
    conv_ref[0:tail, :] = xb[L - tail:L, :]
    u = _silu(acc)
    xs = u[:, 0:SSD_WIDTH]
    bmat = u[:, SSD_WIDTH:SSD_WIDTH + SSD_GROUPS * SSD_STATE]
    cmat = u[:, SSD_WIDTH + SSD_GROUPS * SSD_STATE:XBC_WIDTH]

    dt = _softplus(dt_ref[r0:r0 + L, :] + dtb_row_ref[...])
    a = dt * (-jnp.exp(alog_row_ref[...]))
    a_hi, a_lo = _split_bf16(a)
    a_cum = (jnp.dot(tril_ref[...], a_hi, preferred_element_type=F32)
             + jnp.dot(tril_ref[...], a_lo, preferred_element_type=F32))
    dt_t = _softplus(dtt_ref[:, r0:r0 + L] + dtb_col_ref[...])
    a_t = dt_t * (-jnp.exp(alog_col_ref[...]))
    at_hi, at_lo = _split_bf16(a_t)
    a_cum_t = (jnp.dot(at_hi, triu_ref[...], preferred_element_type=F32)
               + jnp.dot(at_lo, triu_ref[...], preferred_element_type=F32))
    a_end_t = a_cum_t[:, L - 1:L]
    return dict(
        xs=xs, bmat=bmat, cmat=cmat, a_cum=a_cum, exp_a_cum=jnp.exp(a_cum),
        shifted_t=a_cum_t - jnp.log(dt_t),
        wst_t=jnp.exp(a_end_t - a_cum_t) * dt_t,
        cdec_t=jnp.exp(a_end_t))


def _ssd_chunk(r0, p, causal, lo_half, z_ref, dskip_ref, nw_ref, o_ref, state_ref):
    L = CHUNK
    xs, bmat, cmat, a_cum, exp_a_cum = p["xs"], p["bmat"], p["cmat"], p["a_cum"], p["exp_a_cum"]
    shifted_t, wst_t, cdec_t = p["shifted_t"], p["wst_t"], p["cdec_t"]
    xs_b = xs.astype(BF16)
    heads_per_group = SSD_HEADS // SSD_GROUPS
    gated = []
    for g in range(SSD_GROUPS):
        b_g = bmat[:, g * SSD_STATE:(g + 1) * SSD_STATE]
        c_g = cmat[:, g * SSD_STATE:(g + 1) * SSD_STATE]
        cb = lax.dot_general(c_g.astype(BF16), b_g.astype(BF16), (((1,), (1,)), ((), ())),
                             preferred_element_type=F32)
        b_gt = b_g.T
        for t in range(heads_per_group // 2):
            tile = g * (heads_per_group // 2) + t
            c0 = tile * LANES
            xs_tile = xs_b[:, c0:c0 + LANES]
            st_tile = state_ref[:, c0:c0 + LANES]
            st_b = st_tile.astype(BF16)
            y_tile = jnp.zeros((L, LANES), F32)
            new_tile = jnp.zeros((SSD_STATE, LANES), F32)
            for e in range(2):
                h = 2 * tile + e
                keep = lo_half if e == 0 else ~lo_half
                colb = jnp.broadcast_to(a_cum[:, h:h + 1], (L, L))
                rowb = shifted_t[h:h + 1, :]
                w_in = cb * jnp.exp(jnp.where(causal, colb - rowb, NEG_BIG))
                w_off = c_g * jnp.broadcast_to(exp_a_cum[:, h:h + 1], (L, L))
                lhs = jnp.concatenate([w_in, w_off], axis=1).astype(BF16)
                rhs = jnp.concatenate([jnp.where(keep, xs_tile, jnp.zeros_like(xs_tile)),
                                       jnp.where(keep, st_b, jnp.zeros_like(st_b))], axis=0)
                y_tile = y_tile + jnp.dot(lhs, rhs, preferred_element_type=F32)
                m_h = (b_gt * wst_t[h:h + 1, :]).astype(BF16)
                new_tile = new_tile + jnp.dot(m_h, jnp.where(keep, xs_tile, jnp.zeros_like(xs_tile)),
                                              preferred_element_type=F32)
            cd = jnp.where(lo_half[0:1, :], cdec_t[2 * tile:2 * tile + 1, :],
                           cdec_t[2 * tile + 1:2 * tile + 2, :])
            state_ref[:, c0:c0 + LANES] = st_tile * cd + new_tile
            y_full = y_tile + dskip_ref[:, c0:c0 + LANES] * xs[:, c0:c0 + LANES]
            gated.append(y_full * _silu(z_ref[r0:r0 + L, c0:c0 + LANES].astype(F32)))

    gw = SSD_WIDTH // SSD_GROUPS
    tiles_per_group = gw // LANES
    for g in range(SSD_GROUPS):
        yg = jnp.concatenate(gated[g * tiles_per_group:(g + 1) * tiles_per_group], axis=1)
        ms = jnp.mean(yg * yg, axis=-1, keepdims=True)
        o_ref[r0:r0 + L, g * gw:(g + 1) * gw] = (
            (yg * lax.rsqrt(ms + EPS)) * nw_ref[:, g * gw:(g + 1) * gw]).astype(o_ref.dtype)


def _ssd(xbc, z, dt, dt_t, conv_w, conv_b, dt_bias, a_log, d_skip, ssd_norm_w):
    L = SSD_SUB * CHUNK
    nc = SEQ // L
    pad_row = lambda v: jnp.pad(v.astype(F32), (0, LANES - SSD_HEADS)).reshape(1, LANES)
    col8 = lambda v: v.astype(F32).reshape(SSD_HEADS, 1)
    idx = jnp.arange(CHUNK)
    tril = (idx[None, :] <= idx[:, None]).astype(BF16)
    triu = (idx[:, None] <= idx[None, :]).astype(BF16)
    dskip = jnp.repeat(d_skip.astype(F32), SSD_HEAD_DIM).reshape(1, SSD_WIDTH)
    const = lambda shape: pl.BlockSpec(shape, lambda b, c: (0, 0))
    tok = lambda width: pl.BlockSpec((L, width), lambda b, c: (b * nc + c, 0))
    return pl.pallas_call(
        _ssd_kernel,
        grid=(BATCH, nc),
        in_specs=[tok(XBC_WIDTH), tok(SSD_WIDTH), tok(LANES),
                  pl.BlockSpec((SSD_HEADS, L), lambda b, c: (0, b * nc + c)),
                  const((CONV_K, XBC_WIDTH)), const((1, XBC_WIDTH)),
                  const((1, LANES)), const((SSD_HEADS, 1)), const((1, LANES)), const((SSD_HEADS, 1)),
                  const((1, SSD_WIDTH)), const((1, SSD_WIDTH)), const((CHUNK, CHUNK)), const((CHUNK, CHUNK))],
        out_specs=tok(SSD_WIDTH),
        out_shape=jax.ShapeDtypeStruct((TOKENS, SSD_WIDTH), BF16),
        scratch_shapes=[pltpu.VMEM((8 + CHUNK, XBC_WIDTH), F32),
                        pltpu.VMEM((SSD_STATE, SSD_WIDTH), F32)],
        compiler_params=_cparams(("arbitrary", "arbitrary")),
        name="ssd",
    )(xbc, z, dt, dt_t, conv_w.astype(F32), conv_b.astype(F32).reshape(1, XBC_WIDTH),
      pad_row(dt_bias), col8(dt_bias), pad_row(a_log), col8(a_log), dskip,
      ssd_norm_w.astype(F32).reshape(1, SSD_WIDTH), tril, triu)


OUT_TM = 512
OR_SUB = 2
ROUTE_W = 8
ROUTER_COLS = N_GROUPS + N_EXPERTS
RUN_ALIGN = 16
RUN_SHIFT = 4
LOCAL_ROWS = 1536
assert RUN_ALIGN == 1 << RUN_SHIFT and LOCAL_ROWS >= TOP_K * OUT_TM + N_EXPERTS * (RUN_ALIGN - 1)


def _lane_pick(values, lane, index):
    return jnp.sum(jnp.where(lane == index, values, 0.0), axis=-1, keepdims=True)


def _first_argmax(vals, lane):
    m = jnp.max(vals, axis=-1, keepdims=True)
    idx = jnp.min(jnp.where(vals == m, lane, float(LANES)), axis=-1, keepdims=True)
    return m, idx


def _out_router_kernel(att_ref, y_ref, x_ref, g1_ref, wof_ref, nw_ref, sc_ref, sh_ref, wr_ref, br_ref,
                       ltri_ref, sut_ref, x1_ref, h2_ref, route_ref, routet_ref, tcnt_ref,
                       wr_split_ref, logits_ref, wo_ref):
    i = pl.program_id(0)
    tm = OUT_TM

    @pl.when(i == 0)
    def _():
        hi, lo = _split_bf16(wr_ref[...])
        wr_split_ref[:, 0:LANES] = hi
        wr_split_ref[:, LANES:2 * LANES] = lo
        logits_ref[...] = jnp.zeros_like(logits_ref)
        for r0 in range(0, D_MODEL, _INPROJ_CHUNK):
            wo_ref[r0:r0 + _INPROJ_CHUNK, :] = wof_ref[r0:r0 + _INPROJ_CHUNK, :].astype(BF16)

    previous = [logits_ref[s] for s in range(OR_SUB)]

    for s in range(OR_SUB):
        rows = slice(s * tm, (s + 1) * tm)
        mixer = (jnp.dot(att_ref[rows, :], wo_ref[0:ATT_WIDTH, :], preferred_element_type=F32)
                 + jnp.dot(y_ref[rows, :], wo_ref[ATT_WIDTH:ATT_WIDTH + SSD_WIDTH, :], preferred_element_type=F32))
        x1 = x_ref[rows, :] + g1_ref[0] * mixer
        x1_ref[rows, :] = x1
        yn = x1 * lax.rsqrt(jnp.mean(x1 * x1, axis=-1, keepdims=True) + EPS)
        h2 = (yn * nw_ref[...]) * (1.0 + sc_ref[0]) + sh_ref[0]
        h2_ref[rows, :] = h2.astype(BF16)

        h_hi, h_lo = _split_bf16(h2)
        both = jnp.dot(h_hi, wr_split_ref[...], preferred_element_type=F32)
        logits_ref[s] = (both[:, 0:LANES] + both[:, LANES:2 * LANES]
                         + jnp.dot(h_lo, wr_split_ref[:, 0:LANES], preferred_element_type=F32)) + br_ref[...]

    for s in range(OR_SUB):
        _route_tile(previous[s], s, ltri_ref, sut_ref, route_ref, routet_ref, tcnt_ref)


def _route_tile(logits, s, ltri_ref, sut_ref, route_ref, routet_ref, tcnt_ref):
    tm = logits.shape[0]
    lane = lax.broadcasted_iota(I32, (tm, LANES), 1).astype(F32)

    gl = jnp.where(lane < N_GROUPS, logits, NEG_BIG)
    gmax, gidx = _first_argmax(gl, lane)
    g_p = 1.0 / jnp.sum(jnp.exp(gl - gmax), axis=-1, keepdims=True)

    lo_lane = N_GROUPS + EXPERTS_PER_GROUP * gidx
    el = jnp.where((lane >= lo_lane) & (lane < lo_lane + EXPERTS_PER_GROUP), logits, NEG_BIG)
    m1, i1 = _first_argmax(el, lane)
    m2, i2 = _first_argmax(jnp.where(lane == i1, NEG_BIG, el), lane)
    r = jnp.exp(m2 - m1)
    p1 = 1.0 / (1.0 + r)
    p2 = r / (1.0 + r)
    e0 = i1 - N_GROUPS
    e1 = i2 - N_GROUPS

    onehot = ((lane == e0) | (lane == e1)).astype(F32)
    tile_cnt = jnp.sum(onehot, axis=0, keepdims=True)
    run_len = jnp.floor((tile_cnt + (RUN_ALIGN - 1)) * (1.0 / RUN_ALIGN)) * RUN_ALIGN
    run_start = jnp.dot(jnp.broadcast_to(run_len, (8, LANES)).astype(BF16), sut_ref[...],
                        preferred_element_type=F32)[0:1, :]
    before = jnp.dot(ltri_ref[...], onehot.astype(BF16), preferred_element_type=F32) + run_start
    slot0 = _lane_pick(before, lane, e0)
    slot1 = _lane_pick(before, lane, e1)
    tcnt_ref[s] = tile_cnt

    rec = jnp.zeros((tm, LANES), F32)
    for k, v in enumerate([slot0, slot1, g_p * p1, g_p * p2, e0, e1]):
        rec = jnp.where(lane == k, v, rec)
    route_ref[s * tm:(s + 1) * tm, :] = rec[:, 0:ROUTE_W]
    routet_ref[s * ROUTE_W:(s + 1) * ROUTE_W, :] = rec.T[0:ROUTE_W, :]


def _out_router(att, y, x2d, mod3, w_out_b, norm_w, w_router, b_router):
    tm = OUT_TM
    rows = OR_SUB * tm
    n_steps = TOKENS // rows
    steps_per_batch = SEQ // rows
    idx = jnp.arange(tm)
    ltri = (idx[None, :] < idx[:, None]).astype(BF16)
    lidx = jnp.arange(LANES)
    sut = (lidx[:, None] < lidx[None, :]).astype(BF16)
    const = lambda shape: pl.BlockSpec(shape, lambda i: (0, 0))
    cur = lambda i: jnp.minimum(i, n_steps - 1)
    prev = lambda i: jnp.maximum(i - 1, 0)
    tok = lambda width: pl.BlockSpec((rows, width), lambda i: (cur(i), 0))
    modspec = lambda k: pl.BlockSpec((1, 1, D_MODEL), lambda i: ((cur(i) // steps_per_batch) * 6 + k, 0, 0))
    return pl.pallas_call(
        _out_router_kernel,
        grid=(n_steps + 1,),
        in_specs=[tok(ATT_WIDTH), tok(SSD_WIDTH), tok(D_MODEL), modspec(2),
                  const((D_MODEL, D_MODEL)), const((1, D_MODEL)), modspec(4), modspec(3),
                  const((D_MODEL, LANES)), const((1, LANES)), const((tm, tm)), const((LANES, LANES))],
        out_specs=[tok(D_MODEL), tok(D_MODEL),
                   pl.BlockSpec((rows, ROUTE_W), lambda i: (prev(i), 0)),
                   pl.BlockSpec((OR_SUB * ROUTE_W, tm), lambda i: (prev(i), 0)),
                   pl.BlockSpec((OR_SUB, 1, LANES), lambda i: (prev(i), 0, 0))],
        out_shape=[jax.ShapeDtypeStruct((TOKENS, D_MODEL), F32),
                   jax.ShapeDtypeStruct((TOKENS, D_MODEL), BF16),
                   jax.ShapeDtypeStruct((TOKENS, ROUTE_W), F32),
                   jax.ShapeDtypeStruct((N_TOKEN_TILES * ROUTE_W, tm), F32),
                   jax.ShapeDtypeStruct((N_TOKEN_TILES, 1, LANES), F32)],
        scratch_shapes=[pltpu.VMEM((D_MODEL, 2 * LANES), BF16), pltpu.VMEM((OR_SUB, tm, LANES), F32),
                        pltpu.VMEM((D_MODEL, D_MODEL), BF16)],
        compiler_params=_cparams(("arbitrary",)),
        name="out_router",
    )(att, y, x2d, mod3, w_out_b, norm_w.reshape(1, D_MODEL), mod3, mod3, w_router, b_router, ltri, sut)


MOE_TM = 512
ZERO_ROWS = 256
N_TOKEN_TILES = TOKENS // OUT_TM
MAX_SORTED_ROWS = TOKENS * TOP_K + N_TOKEN_TILES * N_EXPERTS * (RUN_ALIGN - 1)
N_TILES = MAX_SORTED_ROWS // MOE_TM + N_EXPERTS
N_ROWS = N_TILES * MOE_TM
assert MOE_TM % ZERO_ROWS == 0


BIG_PIECE = 2 * RUN_ALIGN
PIECE_SLOTS = LOCAL_ROWS // BIG_PIECE
TABLE_W = 4 * PIECE_SLOTS
COMBINE_K = 256
LOCAL_TAIL = COMBINE_K
LOCAL_BODY = LOCAL_ROWS - LOCAL_TAIL
assert PIECE_SLOTS >= N_EXPERTS and LOCAL_ROWS % COMBINE_K == 0


def _local_rows_used(n_big, n_small):
    return n_big * BIG_PIECE + n_small * RUN_ALIGN


def _run_copies(table_ref, n_big, n_small, make_copy, action):
    def big(q, carry):
        action(make_copy(table_ref[0, 0, q], table_ref[0, 0, PIECE_SLOTS + q], BIG_PIECE))
        return carry

    def small(q, carry):
        action(make_copy(table_ref[0, 0, 2 * PIECE_SLOTS + q], table_ref[0, 0, 3 * PIECE_SLOTS + q], RUN_ALIGN))
        return carry

    lax.fori_loop(0, n_big, big, 0)
    lax.fori_loop(0, n_small, small, 0)


def _dispatch_kernel(seg_end_ref, used_end_ref, nb_ref, ns_ref, tab_ref, routet_ref, h2_ref, xs_ref,
                     sbuf_ref, zero_ref, sems, zsem):
    i = pl.program_id(0)
    last = pl.num_programs(0) - 1
    buf = lax.rem(i, 2)

    def zero_fills(action):
        def tail_copy(row):
            return pltpu.make_async_copy(zero_ref.at[pl.ds(0, RUN_ALIGN)],
                                         xs_ref.at[pl.ds(pl.multiple_of(row, RUN_ALIGN), RUN_ALIGN)], zsem)

        def block_copy(block):
            start = pl.multiple_of(block * ZERO_ROWS, ZERO_ROWS)
            return pltpu.make_async_copy(zero_ref, xs_ref.at[pl.ds(start, ZERO_ROWS)], zsem)

        def tails(e, carry):
            def body(r, c):
                action(tail_copy(r * RUN_ALIGN))
                return c

            lax.fori_loop(used_end_ref[e] // RUN_ALIGN, seg_end_ref[e] // RUN_ALIGN, body, 0)
            return carry

        def blocks(block, carry):
            action(block_copy(block))
            return carry

        lax.fori_loop(0, N_EXPERTS, tails, 0)
        lax.fori_loop(seg_end_ref[N_EXPERTS - 1] // ZERO_ROWS, N_ROWS // ZERO_ROWS, blocks, 0)

    @pl.when(i == 0)
    def _():
        zero_ref[...] = jnp.zeros_like(zero_ref)
        zero_fills(lambda cp: cp.start())

    def sort_rows(r0, rows):
        slot = (lax.broadcasted_iota(I32, (rows, OUT_TM), 0) + r0).astype(F32)
        perm = jnp.where((slot == routet_ref[0:1, :]) | (slot == routet_ref[1:2, :]), 1.0, 0.0).astype(BF16)
        sbuf_ref[buf, r0:r0 + rows, :] = jnp.dot(perm, h2_ref[...], preferred_element_type=F32).astype(BF16)

    sort_rows(0, LOCAL_BODY)

    @pl.when(_local_rows_used(nb_ref[i], ns_ref[i]) > LOCAL_BODY)
    def _():
        sort_rows(LOCAL_BODY, LOCAL_TAIL)

    def piece(b):
        def make(local, sorted_row, rows):
            return pltpu.make_async_copy(
                sbuf_ref.at[b, pl.ds(pl.multiple_of(local, RUN_ALIGN), rows)],
                xs_ref.at[pl.ds(pl.multiple_of(sorted_row, RUN_ALIGN), rows)], sems.at[b])
        return make

    _run_copies(tab_ref, nb_ref[i], ns_ref[i], piece(buf), lambda cp: cp.start())
    prev = jnp.maximum(i - 1, 0)

    @pl.when(i > 0)
    def _():
        _run_copies(tab_ref, nb_ref[prev], ns_ref[prev], lambda lo, so, rows: piece(1 - buf)(0, 0, rows),
                    lambda cp: cp.wait())

    @pl.when(i == last)
    def _():
        _run_copies(tab_ref, nb_ref[i], ns_ref[i], lambda lo, so, rows: piece(buf)(0, 0, rows),
                    lambda cp: cp.wait())
        zero_fills(lambda cp: cp.wait())


def _piece_spec(index_map):
    return pl.BlockSpec((1, 1, TABLE_W), index_map, memory_space=pltpu.SMEM)


def _dispatch(seg_end, used_end, n_big, n_small, piece_table, route_t, h2):
    grid_spec = pltpu.PrefetchScalarGridSpec(
        num_scalar_prefetch=4,
        grid=(N_TOKEN_TILES,),
        in_specs=[_piece_spec(lambda i, se, ue, nb, ns: (i, 0, 0)),
                  pl.BlockSpec((ROUTE_W, OUT_TM), lambda i, se, ue, nb, ns: (i, 0)),
                  pl.BlockSpec((OUT_TM, D_MODEL), lambda i, se, ue, nb, ns: (i, 0))],
        out_specs=pl.BlockSpec(memory_space=pl.ANY),
        scratch_shapes=[pltpu.VMEM((2, LOCAL_ROWS, D_MODEL), BF16),
                        pltpu.VMEM((ZERO_ROWS, D_MODEL), BF16),
                        pltpu.SemaphoreType.DMA((2,)), pltpu.SemaphoreType.DMA],
    )
    return pl.pallas_call(
        _dispatch_kernel,
        grid_spec=grid_spec,
        out_shape=jax.ShapeDtypeStruct((N_ROWS, D_MODEL), BF16),
        compiler_params=_cparams(("arbitrary",)),
        name="dispatch",
    )(seg_end, used_end, n_big, n_small, piece_table, route_t, h2)


X_BUFS = 3


def _experts_kernel(te_ref, seg_ref, nxt_ref, rows_ref, nu_ref, xs_hbm, wg_hbm, wu_hbm, wd_hbm, ys_hbm,
                    xbuf, ybuf, wg_buf, wu_buf, wd_buf, wgu_b_ref, wd_b_ref, xsem, ysem, wsem):
    n = nu_ref[0]

    def rows(t):
        return pl.ds(pl.multiple_of(t * MOE_TM, MOE_TM), MOE_TM)

    def x_copy(t, s):
        return pltpu.make_async_copy(xs_hbm.at[rows(t)], xbuf.at[s], xsem.at[s])

    def y_copy(t, s):
        return pltpu.make_async_copy(ybuf.at[s], ys_hbm.at[rows(t)], ysem.at[s])

    def weight_copies(expert, s):
        return [pltpu.make_async_copy(wg_hbm.at[expert], wg_buf.at[s], wsem.at[s]),
                pltpu.make_async_copy(wu_hbm.at[expert], wu_buf.at[s], wsem.at[s]),
                pltpu.make_async_copy(wd_hbm.at[expert], wd_buf.at[s], wsem.at[s])]

    for cp in weight_copies(te_ref[0], 0):
        cp.start()
    for t in range(X_BUFS - 1):
        @pl.when(t < n)
        def _():
            x_copy(t, t).start()

    def tile(i, carry):
        xs_slot = lax.rem(i, X_BUFS)
        ys_slot = lax.rem(i, 2)
        w_slot = lax.rem(seg_ref[i], 2)
        x_copy(i, xs_slot).wait()
        ahead = i + (X_BUFS - 1)

        @pl.when(ahead < n)
        def _():
            x_copy(ahead, lax.rem(ahead, X_BUFS)).start()

        @pl.when((i == 0) | (te_ref[i] != te_ref[jnp.maximum(i - 1, 0)]))
        def _():
            for cp in weight_copies(te_ref[i], w_slot):
                cp.wait()

            @pl.when(nxt_ref[i] >= 0)
            def _():
                for cp in weight_copies(nxt_ref[i], 1 - w_slot):
                    cp.start()

            wgu_b_ref[:, 0:D_EXPERT] = wg_buf[w_slot].astype(BF16)
            wgu_b_ref[:, D_EXPERT:2 * D_EXPERT] = wu_buf[w_slot].astype(BF16)
            wd_b_ref[...] = wd_buf[w_slot].astype(BF16)

        @pl.when(i >= 2)
        def _():
            y_copy(i - 2, ys_slot).wait()

        def mlp(x):
            h = jnp.dot(x, wgu_b_ref[...], preferred_element_type=F32)
            act = (_silu(h[:, 0:D_EXPERT]) * h[:, D_EXPERT:2 * D_EXPERT]).astype(BF16)
            return jnp.dot(act, wd_b_ref[...], preferred_element_type=F32).astype(BF16)

        half = MOE_TM // 2

        @pl.when(rows_ref[i] > half)
        def _():
            ybuf[ys_slot] = mlp(xbuf[xs_slot])

        @pl.when(rows_ref[i] <= half)
        def _():
            ybuf[ys_slot, 0:half, :] = mlp(xbuf[xs_slot, 0:half, :])
            ybuf[ys_slot, half:MOE_TM, :] = jnp.zeros((MOE_TM - half, D_MODEL), BF16)

        y_copy(i, ys_slot).start()
        return carry

    lax.fori_loop(0, n, tile, 0)

    @pl.when(n >= 2)
    def _():
        y_copy(n - 2, lax.rem(n - 2, 2)).wait()

    y_copy(n - 1, lax.rem(n - 1, 2)).wait()


def _experts(tile_expert, tile_segment, next_expert, tile_rows, n_used, xs, w_gate, w_up, w_down):
    n_prefetch = 5
    anywhere = pl.BlockSpec(memory_space=pl.ANY)
    grid_spec = pltpu.PrefetchScalarGridSpec(
        num_scalar_prefetch=n_prefetch,
        grid=(1,),
        in_specs=[anywhere, anywhere, anywhere, anywhere],
        out_specs=anywhere,
        scratch_shapes=[pltpu.VMEM((X_BUFS, MOE_TM, D_MODEL), BF16), pltpu.VMEM((2, MOE_TM, D_MODEL), BF16),
                        pltpu.VMEM((2, D_MODEL, D_EXPERT), F32), pltpu.VMEM((2, D_MODEL, D_EXPERT), F32),
                        pltpu.VMEM((2, D_EXPERT, D_MODEL), F32),
                        pltpu.VMEM((D_MODEL, 2 * D_EXPERT), BF16), pltpu.VMEM((D_EXPERT, D_MODEL), BF16),
                        pltpu.SemaphoreType.DMA((X_BUFS,)), pltpu.SemaphoreType.DMA((2,)),
                        pltpu.SemaphoreType.DMA((2,))],
    )
    return pl.pallas_call(
        _experts_kernel,
        grid_spec=grid_spec,
        out_shape=jax.ShapeDtypeStruct((N_ROWS, D_MODEL), BF16),
        input_output_aliases={n_prefetch: 0},
        compiler_params=_cparams(("arbitrary",)),
        name="experts",
    )(tile_expert, tile_segment, next_expert, tile_rows, n_used, xs, w_gate, w_up, w_down)


def _combine_kernel(nb_ref, ns_ref, tab_ref, tab_next_ref, route_ref, x1_ref, g2_ref, ys_ref, o_ref,
                    gbuf_ref, sems):
    i = pl.program_id(0)
    last = pl.num_programs(0) - 1
    buf = lax.rem(i, 2)

    def piece(b):
        def make(local, sorted_row, rows):
            return pltpu.make_async_copy(
                ys_ref.at[pl.ds(pl.multiple_of(sorted_row, RUN_ALIGN), rows)],
                gbuf_ref.at[b, pl.ds(pl.multiple_of(local, RUN_ALIGN), rows)], sems.at[b])
        return make

    @pl.when(i == 0)
    def _():
        gbuf_ref[...] = jnp.zeros_like(gbuf_ref)
        _run_copies(tab_ref, nb_ref[0], ns_ref[0], piece(0), lambda cp: cp.start())

    nxt = jnp.minimum(i + 1, last)

    @pl.when(i < last)
    def _():
        _run_copies(tab_next_ref, nb_ref[nxt], ns_ref[nxt], piece(1 - buf), lambda cp: cp.start())

    rec = route_ref[...]
    slot0 = lax.broadcasted_iota(I32, (OUT_TM, COMBINE_K), 1).astype(F32)
    _run_copies(tab_ref, nb_ref[i], ns_ref[i], lambda lo, so, rows: piece(buf)(0, 0, rows), lambda cp: cp.wait())

    def slice_sum(k0):
        s0, s1 = rec[:, 0:1] - float(k0), rec[:, 1:2] - float(k0)
        weights = (jnp.where(slot0 == s0, rec[:, 2:3], 0.0)
                   + jnp.where(slot0 == s1, rec[:, 3:4], 0.0)).astype(BF16)
        return jnp.dot(weights, gbuf_ref[buf, k0:k0 + COMBINE_K, :], preferred_element_type=F32)

    moe = jnp.zeros((OUT_TM, D_MODEL), F32)
    for k0 in range(0, LOCAL_BODY, COMBINE_K):
        moe = moe + slice_sum(k0)
    o_ref[...] = x1_ref[...] + g2_ref[0] * moe

    @pl.when(_local_rows_used(nb_ref[i], ns_ref[i]) > LOCAL_BODY)
    def _():
        o_ref[...] += g2_ref[0] * slice_sum(LOCAL_BODY)


def _combine(n_big, n_small, piece_table, route, x1, mod3, ys):
    tm = OUT_TM
    steps_per_batch = SEQ // tm
    grid_spec = pltpu.PrefetchScalarGridSpec(
        num_scalar_prefetch=2,
        grid=(N_TOKEN_TILES,),
        in_specs=[_piece_spec(lambda i, nb, ns: (i, 0, 0)),
                  _piece_spec(lambda i, nb, ns: (jnp.minimum(i + 1, N_TOKEN_TILES - 1), 0, 0)),
                  pl.BlockSpec((tm, ROUTE_W), lambda i, nb, ns: (i, 0)),
                  pl.BlockSpec((tm, D_MODEL), lambda i, nb, ns: (i, 0)),
                  pl.BlockSpec((1, 1, D_MODEL), lambda i, nb, ns: ((i // steps_per_batch) * 6 + 5, 0, 0)),
                  pl.BlockSpec(memory_space=pl.ANY)],
        out_specs=pl.BlockSpec((tm, D_MODEL), lambda i, nb, ns: (i, 0)),
        scratch_shapes=[pltpu.VMEM((2, LOCAL_ROWS, D_MODEL), BF16), pltpu.SemaphoreType.DMA((2,))],
    )
    return pl.pallas_call(
        _combine_kernel,
        grid_spec=grid_spec,
        out_shape=jax.ShapeDtypeStruct((TOKENS, D_MODEL), F32),
        compiler_params=_cparams(("arbitrary",)),
        name="combine",
    )(n_big, n_small, piece_table, piece_table, route, x1, mod3, ys)


def kernel(x, c, positions, norm1_w, norm2_w, w_ada, b_ada, w_in, conv_w, conv_b, dt_bias, a_log,
           d_skip, ssd_norm_w, q_norm_w, k_norm_w, sinks, w_out, w_group, b_group, w_expert, b_expert,
           w_gate, w_up, w_down):
    assert x.shape == (BATCH, SEQ, D_MODEL) and w_in.shape == (D_MODEL, IN_WIDTH)
    x2d = x.reshape(TOKENS, D_MODEL)
    cosf, s1, s2, mod = _rope_tables_and_mod(positions, c, w_ada, b_ada)
    mod3 = mod.reshape(BATCH * 6, 1, D_MODEL)

    qkv, z, xbc, dt, dt_t = _in_proj(x2d, norm1_w, mod3, w_in)
    att = _attention(qkv, (cosf, s1, s2), q_norm_w, k_norm_w, sinks)
    y = _ssd(xbc, z, dt, dt_t, conv_w, conv_b, dt_bias, a_log, d_skip, ssd_norm_w)

    w_router = jnp.pad(jnp.concatenate([w_group, w_expert], axis=1).astype(F32),
                       ((0, 0), (0, LANES - ROUTER_COLS)))
    b_router = jnp.pad(jnp.concatenate([b_group, b_expert]).astype(F32),
                       (0, LANES - ROUTER_COLS)).reshape(1, LANES)
    x1, h2, route, route_t, tcnt = _out_router(att, y, x2d, mod3, w_out.astype(F32), norm2_w,
                                                w_router, b_router)

    tc = tcnt[:, 0, 0:N_EXPERTS].astype(I32)
    run_rows = ((tc + RUN_ALIGN - 1) // RUN_ALIGN) * RUN_ALIGN
    counts = jnp.sum(run_rows, axis=0)
    padded = ((counts + MOE_TM - 1) // MOE_TM) * MOE_TM
    seg_end = jnp.cumsum(padded)
    seg_start = seg_end - padded
    run_dst = seg_start[None, :] + jnp.cumsum(run_rows, axis=0) - run_rows
    n_used = (seg_end[-1] // MOE_TM).reshape(1)
    last_row = jnp.minimum(jnp.arange(N_TILES, dtype=I32) * MOE_TM, seg_end[-1] - 1)
    tile_expert = jnp.sum((seg_end[None, :] <= last_row[:, None]).astype(I32), axis=1)

    run_local = jnp.cumsum(run_rows, axis=1) - run_rows
    n_big_run = run_rows // BIG_PIECE
    n_small_run = (run_rows // RUN_ALIGN) % 2
    q = jnp.arange(PIECE_SLOTS, dtype=I32)
    experts = jnp.arange(N_EXPERTS, dtype=I32)

    def flat(per_run, local0, dst0, stride):
        end = jnp.cumsum(per_run, axis=1)
        run_of = jnp.sum((end[:, None, :] <= q[None, :, None]).astype(I32), axis=2)
        pick = (run_of[:, :, None] == experts[None, None, :]).astype(I32)
        k = q[None, :] - jnp.sum(pick * (end - per_run)[:, None, :], axis=2)
        local = jnp.sum(pick * local0[:, None, :], axis=2) + stride * k
        dst = jnp.sum(pick * dst0[:, None, :], axis=2) + stride * k
        return end[:, -1], local, dst

    n_big, big_local, big_dst = flat(n_big_run, run_local, run_dst, BIG_PIECE)
    n_small, small_local, small_dst = flat(n_small_run, run_local + BIG_PIECE * n_big_run,
                                           run_dst + BIG_PIECE * n_big_run, 0)
    piece_table = jnp.concatenate([big_local, big_dst, small_local, small_dst], axis=1)
    piece_table = piece_table.astype(I32).reshape(N_TOKEN_TILES, 1, TABLE_W)

    nonempty = padded > 0
    seg_rank = jnp.cumsum(nonempty.astype(I32)) - 1
    later = nonempty[None, :] & (experts[None, :] > experts[:, None])
    next_of = jnp.min(jnp.where(later, experts[None, :], N_EXPERTS), axis=1)
    next_of = jnp.where(next_of == N_EXPERTS, -1, next_of)
    tile_is = (tile_expert[:, None] == experts[None, :]).astype(I32)
    tile_segment = jnp.sum(tile_is * seg_rank[None, :], axis=1)
    next_expert = jnp.sum(tile_is * next_of[None, :], axis=1)
    used_end = seg_start + counts
    tile_rows = jnp.clip(jnp.sum(tile_is * used_end[None, :], axis=1) - jnp.arange(N_TILES, dtype=I32) * MOE_TM,
                         0, MOE_TM)

    n_big, n_small = n_big.astype(I32), n_small.astype(I32)
    xs = _dispatch(seg_end.astype(I32), used_end.astype(I32), n_big, n_small, piece_table, route_t, h2)
    ys = _experts(tile_expert, tile_segment.astype(I32), next_expert.astype(I32), tile_rows.astype(I32),
                  n_used.astype(I32), xs, w_gate, w_up, w_down)
    out = _combine(n_big, n_small, piece_table, route, x1, mod3, ys)
    return out.reshape(BATCH, SEQ, D_MODEL)
```

```python
import jax
import jax.numpy as jnp
from jax import lax
from jax.experimental import pallas as pl
from jax.experimental.pallas import tpu as pltpu

F32 = jnp.float32
BF16 = jnp.bfloat16
I32 = jnp.int32

D_MODEL = 1024
BATCH = 2
SEQ = 8192
TOKENS = BATCH * SEQ
ATT_HEADS = 8
ATT_KV_HEADS = 2
HEAD_DIM = 64
ATT_WIDTH = ATT_HEADS * HEAD_DIM
KV_WIDTH = ATT_KV_HEADS * HEAD_DIM
ATT_BLOCK = 128
ROPE_DIM = HEAD_DIM // 4
ROPE_THETA = 500000.0
SSD_HEADS = 8
SSD_HEAD_DIM = 64
SSD_WIDTH = SSD_HEADS * SSD_HEAD_DIM
SSD_GROUPS = 2
SSD_STATE = 128
CONV_K = 4
CHUNK = 128
XBC_WIDTH = SSD_WIDTH + 2 * SSD_GROUPS * SSD_STATE
IN_WIDTH = ATT_WIDTH + 2 * KV_WIDTH + SSD_WIDTH + XBC_WIDTH + SSD_HEADS
N_GROUPS = 4
EXPERTS_PER_GROUP = 8
N_EXPERTS = N_GROUPS * EXPERTS_PER_GROUP
TOP_K = 2
D_EXPERT = 256
EPS = 1e-6

LANES = 128
QKV_WIDTH = ATT_WIDTH + 2 * KV_WIDTH
IN_PAD = QKV_WIDTH + SSD_WIDTH + XBC_WIDTH + LANES
NEG_BIG = -1e30

VMEM_LIMIT = 48 * 1024 * 1024


def _cparams(sem):
    return pltpu.CompilerParams(dimension_semantics=sem, vmem_limit_bytes=VMEM_LIMIT)


def _split_bf16(x):
    hi = x.astype(BF16)
    lo = (x - hi.astype(F32)).astype(BF16)
    return hi, lo


ADA_TN = 768


def _ada_kernel(ct_ref, w_ref, b_ref, o_ref):
    ct = ct_ref[...]
    s = ct * jax.nn.sigmoid(ct)
    w = w_ref[...]
    rows = [jnp.sum(s[:, b:b + 1] * w, axis=0, keepdims=True) for b in range(BATCH)]
    o_ref[...] = jnp.concatenate(rows, axis=0) + b_ref[...]


INPROJ_TM = 1024
_INPROJ_CHUNK = 256


def _inproj_kernel(x_ref, nw_ref, sc_ref, sh_ref, wf_ref, wdt_ref, qkv_ref, z_ref, xbc_ref, dt_ref, dtt_ref,
                   w_ref):
    @pl.when(pl.program_id(0) == 0)
    def _():
        for c0 in range(0, IN_PAD - LANES, _INPROJ_CHUNK):
            w_ref[:, c0:c0 + _INPROJ_CHUNK] = wf_ref[:, c0:c0 + _INPROJ_CHUNK].astype(BF16)
        w_ref[:, IN_PAD - LANES:IN_PAD] = wdt_ref[...].astype(BF16)

    x = x_ref[...]
    y = x * lax.rsqrt(jnp.mean(x * x, axis=-1, keepdims=True) + EPS)
    h = (y * nw_ref[...]) * (1.0 + sc_ref[0]) + sh_ref[0]
    hb = h.astype(BF16)

    def proj(c0, c1):
        return jnp.dot(hb, w_ref[:, c0:c1], preferred_element_type=F32)

    for c0 in range(0, QKV_WIDTH, _INPROJ_CHUNK):
        qkv_ref[:, c0:c0 + _INPROJ_CHUNK] = proj(c0, c0 + _INPROJ_CHUNK).astype(BF16)
    base = QKV_WIDTH
    for c0 in range(0, SSD_WIDTH, _INPROJ_CHUNK):
        z_ref[:, c0:c0 + _INPROJ_CHUNK] = proj(base + c0, base + c0 + _INPROJ_CHUNK).astype(BF16)
    base += SSD_WIDTH
    for c0 in range(0, XBC_WIDTH, _INPROJ_CHUNK):
        xbc_ref[:, c0:c0 + _INPROJ_CHUNK] = proj(base + c0, base + c0 + _INPROJ_CHUNK).astype(BF16)
    base += XBC_WIDTH
    dt = proj(base, base + LANES)
    dt_ref[...] = dt
    dtt_ref[...] = dt.T[0:SSD_HEADS, :]


def _in_proj(x2d, norm_w, mod3, w_in):
    tm = INPROJ_TM
    steps_per_batch = SEQ // tm
    w_dt = jnp.pad(w_in[:, IN_WIDTH - SSD_HEADS:IN_WIDTH].astype(F32), ((0, 0), (0, LANES - SSD_HEADS)))
    return pl.pallas_call(
        _inproj_kernel,
        grid=(TOKENS // tm,),
        in_specs=[pl.BlockSpec((tm, D_MODEL), lambda i: (i, 0)),
                  pl.BlockSpec((1, D_MODEL), lambda i: (0, 0)),
                  pl.BlockSpec((1, 1, D_MODEL), lambda i: ((i // steps_per_batch) * 6 + 1, 0, 0)),
                  pl.BlockSpec((1, 1, D_MODEL), lambda i: ((i // steps_per_batch) * 6 + 0, 0, 0)),
                  pl.BlockSpec((D_MODEL, IN_WIDTH), lambda i: (0, 0), pipeline_mode=pl.Buffered(1)),
                  pl.BlockSpec((D_MODEL, LANES), lambda i: (0, 0))],
        out_specs=[pl.BlockSpec((tm, QKV_WIDTH), lambda i: (i, 0)),
                   pl.BlockSpec((tm, SSD_WIDTH), lambda i: (i, 0)),
                   pl.BlockSpec((tm, XBC_WIDTH), lambda i: (i, 0)),
                   pl.BlockSpec((tm, LANES), lambda i: (i, 0)),
                   pl.BlockSpec((SSD_HEADS, tm), lambda i: (0, i))],
        out_shape=[jax.ShapeDtypeStruct((TOKENS, QKV_WIDTH), BF16),
                   jax.ShapeDtypeStruct((TOKENS, SSD_WIDTH), BF16),
                   jax.ShapeDtypeStruct((TOKENS, XBC_WIDTH), BF16),
                   jax.ShapeDtypeStruct((TOKENS, LANES), F32),
                   jax.ShapeDtypeStruct((SSD_HEADS, TOKENS), F32)],
        scratch_shapes=[pltpu.VMEM((D_MODEL, IN_PAD), BF16)],
        compiler_params=_cparams(("arbitrary",)),
        name="in_proj",
    )(x2d, norm_w.reshape(1, D_MODEL), mod3, mod3, w_in.astype(F32), w_dt)


ATT_SUB = 8


ROPE_TM = 2048
_ROPE_HALF = ROPE_DIM // 2
_TOK_PER_ROW = LANES // _ROPE_HALF


def _exact_dot(x, onehot_b):
    hi, lo = _split_bf16(x)
    return (jnp.dot(hi, onehot_b, preferred_element_type=F32)
            + jnp.dot(lo, onehot_b, preferred_element_type=F32))


def _rope_kernel(pos_ref, freq_ref, sel_ref, own_ref, gcos_ref, gs1_ref, gs2_ref, ident_ref,
                 cos_ref, s1_ref, s2_ref):
    ang = pos_ref[...].astype(F32) * freq_ref[...]
    cos_p, sin_p = jnp.cos(ang), jnp.sin(ang)
    hi_c, lo_c = _split_bf16(cos_p)
    hi_s, lo_s = _split_bf16(sin_p)
    sel = sel_ref[...]
    rows_c = jnp.dot(sel, hi_c, preferred_element_type=F32) + jnp.dot(sel, lo_c, preferred_element_type=F32)
    rows_s = jnp.dot(sel, hi_s, preferred_element_type=F32) + jnp.dot(sel, lo_s, preferred_element_type=F32)
    own = own_ref[...]
    cos_ref[...] = _exact_dot(rows_c * own, gcos_ref[...]) + ident_ref[...]
    s1_ref[...] = _exact_dot(rows_s * own, gs1_ref[...])
    s2_ref[...] = _exact_dot(rows_s * own, gs2_ref[...])


def _tables_kernel(pos_ref, freq_ref, sel_ref, own_ref, gcos_ref, gs1_ref, gs2_ref, ident_ref,
                   ct_ref, w_ref, b_ref, cos_ref, s1_ref, s2_ref, mod_ref):
    _rope_kernel(pos_ref, freq_ref, sel_ref, own_ref, gcos_ref, gs1_ref, gs2_ref, ident_ref,
                 cos_ref, s1_ref, s2_ref)
    _ada_kernel(ct_ref, w_ref, b_ref, mod_ref)


def _rope_tables_and_mod(positions, c, w_ada, b_ada):
    n_mod = w_ada.shape[1]
    assert TOKENS // ROPE_TM == n_mod // ADA_TN
    half, per_row = _ROPE_HALF, _TOK_PER_ROW
    rows = ROPE_TM // per_row
    pos_rep = jnp.repeat(positions.reshape(TOKENS).astype(I32), half).reshape(TOKENS // per_row, LANES)
    inv_freq = jnp.power(ROPE_THETA, -jnp.arange(half, dtype=F32) * 2.0 / ROPE_DIM)
    freq = jnp.tile(inv_freq, per_row).reshape(1, LANES)
    tok = jnp.arange(ROPE_TM)
    lane = jnp.arange(LANES)
    sel = (tok[:, None] // per_row == jnp.arange(rows)[None, :]).astype(BF16)
    own = (lane[None, :] // half == tok[:, None] % per_row).astype(F32)
    d = lane % HEAD_DIM
    src_f = lane % half
    hits = lambda lo, hi: ((src_f[:, None] == d[None, :] % half) & (d[None, :] >= lo) & (d[None, :] < hi))
    gcos = hits(0, ROPE_DIM).astype(BF16)
    gs1 = -hits(0, half).astype(BF16)
    gs2 = hits(half, ROPE_DIM).astype(BF16)
    ident = (d >= ROPE_DIM).astype(F32).reshape(1, LANES)
    const = lambda shape: pl.BlockSpec(shape, lambda i: (0, 0))
    out_spec = pl.BlockSpec((ROPE_TM, LANES), lambda i: (i, 0))
    out = jax.ShapeDtypeStruct((TOKENS, LANES), F32)
    in_specs = [pl.BlockSpec((rows, LANES), lambda i: (i, 0)), const((1, LANES)),
                const((ROPE_TM, rows)), const((ROPE_TM, LANES)),
                const((LANES, LANES)), const((LANES, LANES)), const((LANES, LANES)), const((1, LANES)),
                const((D_MODEL, BATCH)),
                pl.BlockSpec((D_MODEL, ADA_TN), lambda i: (0, i), pipeline_mode=pl.Buffered(3)),
                pl.BlockSpec((1, ADA_TN), lambda i: (0, i))]
    out_specs = [out_spec, out_spec, out_spec, pl.BlockSpec((BATCH, ADA_TN), lambda i: (0, i))]

    def pipelined(*refs):
        pltpu.emit_pipeline(_tables_kernel, grid=(TOKENS // ROPE_TM,), in_specs=in_specs,
                            out_specs=out_specs)(*refs)

    anywhere = pl.BlockSpec(memory_space=pl.ANY)
    return pl.pallas_call(
        pipelined,
        in_specs=[anywhere] * len(in_specs),
        out_specs=[anywhere] * len(out_specs),
        out_shape=[out, out, out, jax.ShapeDtypeStruct((BATCH, n_mod), F32)],
        compiler_params=pltpu.CompilerParams(vmem_limit_bytes=VMEM_LIMIT),
        name="rope_tables_ada_mod",
    )(pos_rep, freq, sel, own, gcos, gs1, gs2, ident, c.T, w_ada, b_ada.reshape(1, n_mod))


def _seg_meansq(xf, ones128):
    rows, width = xf.shape
    nt = width // LANES
    parts = _split_bf16(xf * xf)
    stacked = jnp.concatenate([p[:, t * LANES:(t + 1) * LANES] for p in parts for t in range(nt)], axis=0)
    tot = jnp.dot(stacked, ones128, preferred_element_type=F32)
    tiles = [tot[t * rows:(t + 1) * rows] + tot[(nt + t) * rows:(nt + t + 1) * rows] for t in range(nt)]
    return jnp.concatenate(tiles, axis=1) * (1.0 / HEAD_DIM)


def _norm_rope(x_bf, w_row, ones_bd, cosf, s1, s2):
    xf = x_bf.astype(F32)
    width = xf.shape[1]
    xn = xf * lax.rsqrt(_seg_meansq(xf, ones_bd) + EPS) * w_row
    half = ROPE_DIM // 2
    up = pltpu.roll(xn, width - half, axis=1)
    down = pltpu.roll(xn, half, axis=1)
    return xn * cosf + up * s1 + down * s2


def _attn_kernel(sink_ref, q_ref, kv_ref, cos_ref, s1_ref, s2_ref, qw_ref, kw_ref,
                 ones_ref, o_ref, kprev_ref, vprev_ref):
    j = pl.program_id(1)
    blk = ATT_BLOCK

    @pl.when(j == 0)
    def _():
        kprev_ref[...] = jnp.zeros_like(kprev_ref)
        vprev_ref[...] = jnp.zeros_like(vprev_ref)

    cos1 = cos_ref[...]
    s1_1 = s1_ref[...]
    s2_1 = s2_ref[...]
    reps = ATT_WIDTH // LANES
    cosq = jnp.concatenate([cos1] * reps, axis=1)
    s1q = jnp.concatenate([s1_1] * reps, axis=1)
    s2q = jnp.concatenate([s2_1] * reps, axis=1)

    q = _norm_rope(q_ref[...], qw_ref[...], ones_ref[...], cosq, s1q, s2q)
    qf = q * (HEAD_DIM ** -0.5)
    kv = kv_ref[...]
    kn = _norm_rope(kv[:, 0:KV_WIDTH], kw_ref[...], ones_ref[...], cos1, s1_1, s2_1)
    vn = kv[:, KV_WIDTH:2 * KV_WIDTH].astype(F32)

    kall = jnp.concatenate([kprev_ref[...], kn], axis=0)
    vall = jnp.concatenate([vprev_ref[...], vn], axis=0)
    kprev_ref[...] = kn[(ATT_SUB - 1) * blk:ATT_SUB * blk]
    vprev_ref[...] = vn[(ATT_SUB - 1) * blk:ATT_SUB * blk]

    lo_all = lax.broadcasted_iota(I32, kall.shape, 1) < HEAD_DIM
    ones_all = jnp.ones(kall.shape, BF16)

    row = lax.broadcasted_iota(I32, (2 * blk, blk), 0)
    col = lax.broadcasted_iota(I32, (2 * blk, blk), 1)
    from_prev = col > (row & (blk - 1))
    second_tile = lax.broadcasted_iota(I32, (2 * blk, 1), 0) >= blk
    zero_p = jnp.zeros((2 * blk, blk), F32)

    k_par, v_par = [], []
    for g in range(ATT_KV_HEADS):
        keep = lo_all if g == 0 else ~lo_all
        k_own = jnp.where(keep, kall, 0.0)
        v_own = jnp.where(keep, vall, 0.0)
        k_oth = pltpu.roll(k_own, HEAD_DIM, axis=1)
        v_oth = pltpu.roll(v_own, HEAD_DIM, axis=1)
        k_lo, k_hi = (k_own, k_oth) if g == 0 else (k_oth, k_own)
        v_lo, v_hi = (v_own, v_oth) if g == 0 else (v_oth, v_own)
        k_par.append((k_lo.astype(BF16), k_hi.astype(BF16)))
        v_par.append((jnp.concatenate([v_lo.astype(BF16), ones_all], axis=1),
                      jnp.concatenate([v_hi.astype(BF16), ones_all], axis=1)))

    problems = [(g, sub) for g in range(ATT_KV_HEADS) for sub in range(ATT_SUB)]
    scores = []
    for g, sub in problems:
        r0, c0 = sub * blk, g * 2 * LANES
        qcat = jnp.concatenate([qf[r0:r0 + blk, c0:c0 + LANES],
                                qf[r0:r0 + blk, c0 + LANES:c0 + 2 * LANES]], axis=0).astype(BF16)
        kw = jnp.concatenate([k_par[g][0][r0:r0 + 2 * blk], k_par[g][1][r0:r0 + 2 * blk]], axis=0)
        scores.append(lax.dot_general(qcat, kw, (((1,), (1,)), ((), ())),
                                      preferred_element_type=F32))

    weights, rescale = [], []
    for (g, sub), s_all in zip(problems, scores):
        for par in range(2):
            s = s_all[:, par * 2 * blk:(par + 1) * 2 * blk]
            s_prev = s[:, 0:blk]
            if sub == 0:
                s_prev = s_prev + jnp.where(j > 0, 0.0, NEG_BIG)
            s = jnp.where(from_prev, s_prev, s[:, blk:2 * blk])
            h_first = ATT_HEADS // ATT_KV_HEADS * g + par
            sink = jnp.where(second_tile, sink_ref[h_first + 2], sink_ref[h_first])
            m = jnp.maximum(jnp.max(s, axis=-1, keepdims=True), sink)
            p = jnp.exp(s - m)
            weights.append(jnp.concatenate([jnp.where(from_prev, p, zero_p), jnp.where(from_prev, zero_p, p)],
                                           axis=1).astype(BF16))
            rescale.append(jnp.exp(sink - m))

    outs = []
    for idx, (g, sub) in enumerate(problems):
        for par in range(2):
            outs.append(jnp.dot(weights[2 * idx + par], v_par[g][par][sub * blk:(sub + 2) * blk],
                                preferred_element_type=F32))

    for idx, (g, sub) in enumerate(problems):
        r0, c0 = sub * blk, g * 2 * LANES
        pair = None
        for par in range(2):
            o = outs[2 * idx + par]
            part = o[:, 0:LANES] * (1.0 / (o[:, LANES:2 * LANES] + rescale[2 * idx + par]))
            pair = part if pair is None else pair + part
        o_ref[r0:r0 + blk, c0:c0 + LANES] = pair[0:blk].astype(BF16)
        o_ref[r0:r0 + blk, c0 + LANES:c0 + 2 * LANES] = pair[blk:2 * blk].astype(BF16)


def _attention(qkv, rope_tables, q_norm_w, k_norm_w, sinks):
    cosf, s1, s2 = rope_tables
    qw = jnp.tile(q_norm_w.astype(F32), ATT_HEADS).reshape(1, ATT_WIDTH)
    kw = jnp.tile(k_norm_w.astype(F32), ATT_KV_HEADS).reshape(1, KV_WIDTH)
    seg = jnp.arange(LANES) // HEAD_DIM
    ones128 = (seg[:, None] == seg[None, :]).astype(BF16)
    const = lambda shape: pl.BlockSpec(shape, lambda b, j, s: (0, 0))
    rows = ATT_SUB * ATT_BLOCK
    nb = SEQ // rows
    tok = lambda width, cb: pl.BlockSpec((rows, width), lambda b, j, s: (b * nb + j, cb))
    grid_spec = pltpu.PrefetchScalarGridSpec(
        num_scalar_prefetch=1,
        grid=(BATCH, nb),
        in_specs=[tok(ATT_WIDTH, 0), tok(2 * KV_WIDTH, 2), tok(LANES, 0), tok(LANES, 0), tok(LANES, 0),
                  const((1, ATT_WIDTH)), const((1, KV_WIDTH)), const((LANES, LANES))],
        out_specs=tok(ATT_WIDTH, 0),
        scratch_shapes=[pltpu.VMEM((ATT_BLOCK, KV_WIDTH), F32),
                        pltpu.VMEM((ATT_BLOCK, KV_WIDTH), F32)],
    )
    return pl.pallas_call(
        _attn_kernel,
        grid_spec=grid_spec,
        out_shape=jax.ShapeDtypeStruct((TOKENS, ATT_WIDTH), BF16),
        compiler_params=_cparams(("arbitrary", "arbitrary")),
        name="attention",
    )(sinks.astype(F32), qkv, qkv, cosf, s1, s2, qw, kw, ones128)


SSD_SUB = 4


def _softplus(x):
    return jnp.maximum(x, 0.0) + jnp.log1p(jnp.exp(-jnp.abs(x)))


def _silu(x):
    h = 0.5 * x
    return h + h * jnp.tanh(h)


def _ssd_kernel(xbc_ref, z_ref, dt_ref, dtt_ref, cw_ref, cb_ref, dtb_row_ref, dtb_col_ref,
                alog_row_ref, alog_col_ref, dskip_ref, nw_ref, tril_ref, triu_ref,
                o_ref, conv_ref, state_ref):
    c = pl.program_id(1)
    L = CHUNK
    tail = 8

    @pl.when(c == 0)
    def _():
        conv_ref[0:tail, :] = jnp.zeros((tail, XBC_WIDTH), F32)
        state_ref[...] = jnp.zeros_like(state_ref)

    row = lax.broadcasted_iota(I32, (L, L), 0)
    col = lax.broadcasted_iota(I32, (L, L), 1)
    causal = col <= row
    lane = lax.broadcasted_iota(I32, (L, LANES), 1)
    lo_half = lane < SSD_HEAD_DIM

    prepared = [_ssd_prepare(s * L, xbc_ref, dt_ref, dtt_ref, cw_ref, cb_ref, dtb_row_ref, dtb_col_ref,
                             alog_row_ref, alog_col_ref, tril_ref, triu_ref, conv_ref)
                for s in range(SSD_SUB)]
    for s in range(SSD_SUB):
        _ssd_chunk(s * L, prepared[s], causal, lo_half, z_ref, dskip_ref, nw_ref, o_ref, state_ref)


def _ssd_prepare(r0, xbc_ref, dt_ref, dtt_ref, cw_ref, cb_ref, dtb_row_ref, dtb_col_ref,
                 alog_row_ref, alog_col_ref, tril_ref, triu_ref, conv_ref):
    L = CHUNK
    tail = 8
    xb = xbc_ref[r0:r0 + L, :].astype(F32)
    conv_ref[tail:tail + L, :] = xb
    acc = cb_ref[...] + cw_ref[CONV_K - 1:CONV_K, :] * xb
    for k in range(CONV_K - 1):
        off = tail - (CONV_K - 1) + k
        acc = acc + cw_ref[k:k + 1, :] * conv_ref[off:off + L, :]
    conv_ref[0:tail, :] = xb[L - tail:L, :]
    u = _silu(acc)
    xs = u[:, 0:SSD_WIDTH]
    bmat = u[:, SSD_WIDTH:SSD_WIDTH + SSD_GROUPS * SSD_STATE]
    cmat = u[:, SSD_WIDTH + SSD_GROUPS * SSD_STATE:XBC_WIDTH]

    dt = _softplus(dt_ref[r0:r0 + L, :] + dtb_row_ref[...])
    a = dt * (-jnp.exp(alog_row_ref[...]))
    a_hi, a_lo = _split_bf16(a)
    a_cum = (jnp.dot(tril_ref[...], a_hi, preferred_element_type=F32)
             + jnp.dot(tril_ref[...], a_lo, preferred_element_type=F32))
    dt_t = _softplus(dtt_ref[:, r0:r0 + L] + dtb_col_ref[...])
    a_t = dt_t * (-jnp.exp(alog_col_ref[...]))
    at_hi, at_lo = _split_bf16(a_t)
    a_cum_t = (jnp.dot(at_hi, triu_ref[...], preferred_element_type=F32)
               + jnp.dot(at_lo, triu_ref[...], preferred_element_type=F32))
    a_end_t = a_cum_t[:, L - 1:L]
    return dict(
        xs=xs, bmat=bmat, cmat=cmat, a_cum=a_cum, exp_a_cum=jnp.exp(a_cum),
        shifted_t=a_cum_t - jnp.log(dt_t),
        wst_t=jnp.exp(a_end_t - a_cum_t) * dt_t,
        cdec_t=jnp.exp(a_end_t))


def _ssd_chunk(r0, p, causal, lo_half, z_ref, dskip_ref, nw_ref, o_ref, state_ref):
    L = CHUNK
    xs, bmat, cmat, a_cum, exp_a_cum = p["xs"], p["bmat"], p["cmat"], p["a_cum"], p["exp_a_cum"]
    shifted_t, wst_t, cdec_t = p["shifted_t"], p["wst_t"], p["cdec_t"]
    xs_b = xs.astype(BF16)
    heads_per_group = SSD_HEADS // SSD_GROUPS
    gated = []
    for g in range(SSD_GROUPS):
        b_g = bmat[:, g * SSD_STATE:(g + 1) * SSD_STATE]
        c_g = cmat[:, g * SSD_STATE:(g + 1) * SSD_STATE]
        cb = lax.dot_general(c_g.astype(BF16), b_g.astype(BF16), (((1,), (1,)), ((), ())),
                             preferred_element_type=F32)
        b_gt = b_g.T
        for t in range(heads_per_group // 2):
            tile = g * (heads_per_group // 2) + t
            c0 = tile * LANES
            xs_tile = xs_b[:, c0:c0 + LANES]
            st_tile = state_ref[:, c0:c0 + LANES]
            st_b = st_tile.astype(BF16)
            y_tile = jnp.zeros((L, LANES), F32)
            new_tile = jnp.zeros((SSD_STATE, LANES), F32)
            for e in range(2):
                h = 2 * tile + e
                keep = lo_half if e == 0 else ~lo_half
                colb = jnp.broadcast_to(a_cum[:, h:h + 1], (L, L))
                rowb = shifted_t[h:h + 1, :]
                w_in = cb * jnp.exp(jnp.where(causal, colb - rowb, NEG_BIG))
                w_off = c_g * jnp.broadcast_to(exp_a_cum[:, h:h + 1], (L, L))
                lhs = jnp.concatenate([w_in, w_off], axis=1).astype(BF16)
                rhs = jnp.concatenate([jnp.where(keep, xs_tile, jnp.zeros_like(xs_tile)),
                                       jnp.where(keep, st_b, jnp.zeros_like(st_b))], axis=0)
                y_tile = y_tile + jnp.dot(lhs, rhs, preferred_element_type=F32)
                m_h = (b_gt * wst_t[h:h + 1, :]).astype(BF16)
                new_tile = new_tile + jnp.dot(m_h, jnp.where(keep, xs_tile, jnp.zeros_like(xs_tile)),
                                              preferred_element_type=F32)
            cd = jnp.where(lo_half[0:1, :], cdec_t[2 * tile:2 * tile + 1, :],
                           cdec_t[2 * tile + 1:2 * tile + 2, :])
            state_ref[:, c0:c0 + LANES] = st_tile * cd + new_tile
            y_full = y_tile + dskip_ref[:, c0:c0 + LANES] * xs[:, c0:c0 + LANES]
            gated.append(y_full * _silu(z_ref[r0:r0 + L, c0:c0 + LANES].astype(F32)))

    gw = SSD_WIDTH // SSD_GROUPS
    tiles_per_group = gw // LANES
    for g in range(SSD_GROUPS):
        yg = jnp.concatenate(gated[g * tiles_per_group:(g + 1) * tiles_per_group], axis=1)
        ms = jnp.mean(yg * yg, axis=-1, keepdims=True)
        o_ref[r0:r0 + L, g * gw:(g + 1) * gw] = (
            (yg * lax.rsqrt(ms + EPS)) * nw_ref[:, g * gw:(g + 1) * gw]).astype(o_ref.dtype)


def _ssd(xbc, z, dt, dt_t, conv_w, conv_b, dt_bias, a_log, d_skip, ssd_norm_w):
    L = SSD_SUB * CHUNK
    nc = SEQ // L
    pad_row = lambda v: jnp.pad(v.astype(F32), (0, LANES - SSD_HEADS)).reshape(1, LANES)
    col8 = lambda v: v.astype(F32).reshape(SSD_HEADS, 1)
    idx = jnp.arange(CHUNK)
    tril = (idx[None, :] <= idx[:, None]).astype(BF16)
    triu = (idx[:, None] <= idx[None, :]).astype(BF16)
    dskip = jnp.repeat(d_skip.astype(F32), SSD_HEAD_DIM).reshape(1, SSD_WIDTH)
    const = lambda shape: pl.BlockSpec(shape, lambda b, c: (0, 0))
    tok = lambda width: pl.BlockSpec((L, width), lambda b, c: (b * nc + c, 0))
    return pl.pallas_call(
        _ssd_kernel,
        grid=(BATCH, nc),
        in_specs=[tok(XBC_WIDTH), tok(SSD_WIDTH), tok(LANES),
                  pl.BlockSpec((SSD_HEADS, L), lambda b, c: (0, b * nc + c)),
                  const((CONV_K, XBC_WIDTH)), const((1, XBC_WIDTH)),
                  const((1, LANES)), const((SSD_HEADS, 1)), const((1, LANES)), const((SSD_HEADS, 1)),
                  const((1, SSD_WIDTH)), const((1, SSD_WIDTH)), const((CHUNK, CHUNK)), const((CHUNK, CHUNK))],
        out_specs=tok(SSD_WIDTH),
        out_shape=jax.ShapeDtypeStruct((TOKENS, SSD_WIDTH), BF16),
        scratch_shapes=[pltpu.VMEM((8 + CHUNK, XBC_WIDTH), F32),
                        pltpu.VMEM((SSD_STATE, SSD_WIDTH), F32)],
        compiler_params=_cparams(("arbitrary", "arbitrary")),
        name="ssd",
    )(xbc, z, dt, dt_t, conv_w.astype(F32), conv_b.astype(F32).reshape(1, XBC_WIDTH),
      pad_row(dt_bias), col8(dt_bias), pad_row(a_log), col8(a_log), dskip,
      ssd_norm_w.astype(F32).reshape(1, SSD_WIDTH), tril, triu)


OUT_TM = 512
OR_SUB = 2
ROUTE_W = 8
ROUTER_COLS = N_GROUPS + N_EXPERTS
RUN_ALIGN = 16
RUN_SHIFT = 4
LOCAL_ROWS = 1536
assert RUN_ALIGN == 1 << RUN_SHIFT and LOCAL_ROWS >= TOP_K * OUT_TM + N_EXPERTS * (RUN_ALIGN - 1)


def _lane_pick(values, lane, index):
    return jnp.sum(jnp.where(lane == index, values, 0.0), axis=-1, keepdims=True)


def _first_argmax(vals, lane):
    m = jnp.max(vals, axis=-1, keepdims=True)
    idx = jnp.min(jnp.where(vals == m, lane, float(LANES)), axis=-1, keepdims=True)
    return m, idx


def _out_router_kernel(att_ref, y_ref, x_ref, g1_ref, wof_ref, nw_ref, sc_ref, sh_ref, wr_ref, br_ref,
                       ltri_ref, sut_ref, x1_ref, h2_ref, route_ref, routet_ref, tcnt_ref,
                       wr_split_ref, logits_ref, wo_ref):
    i = pl.program_id(0)

    @pl.when(i == 0)
    def _():
        hi, lo = _split_bf16(wr_ref[...])
        wr_split_ref[:, 0:LANES] = hi
        wr_split_ref[:, LANES:2 * LANES] = lo
        logits_ref[...] = jnp.zeros_like(logits_ref)
        for r0 in range(0, D_MODEL, _INPROJ_CHUNK):
            wo_ref[r0:r0 + _INPROJ_CHUNK, :] = wof_ref[r0:r0 + _INPROJ_CHUNK, :].astype(BF16)

    def route(previous):
        for s in range(OR_SUB):
            _route_tile(previous[s], s, ltri_ref, sut_ref, route_ref, routet_ref, tcnt_ref)

    @pl.when(i == pl.num_programs(0) - 1)
    def _():
        route([logits_ref[s] for s in range(OR_SUB)])

    @pl.when(i < pl.num_programs(0) - 1)
    def _():
        previous = [logits_ref[s] for s in range(OR_SUB)]
        _mix_norm_logits(att_ref, y_ref, x_ref, g1_ref, nw_ref, sc_ref, sh_ref, br_ref,
                         x1_ref, h2_ref, wr_split_ref, logits_ref, wo_ref)
        route(previous)


def _mix_norm_logits(att_ref, y_ref, x_ref, g1_ref, nw_ref, sc_ref, sh_ref, br_ref,
                     x1_ref, h2_ref, wr_split_ref, logits_ref, wo_ref):
    tm = OUT_TM
    for s in range(OR_SUB):
        rows = slice(s * tm, (s + 1) * tm)
        mixer = (jnp.dot(att_ref[rows, :], wo_ref[0:ATT_WIDTH, :], preferred_element_type=F32)
                 + jnp.dot(y_ref[rows, :], wo_ref[ATT_WIDTH:ATT_WIDTH + SSD_WIDTH, :], preferred_element_type=F32))
        x1 = x_ref[rows, :] + g1_ref[0] * mixer
        x1_ref[rows, :] = x1
        yn = x1 * lax.rsqrt(jnp.mean(x1 * x1, axis=-1, keepdims=True) + EPS)
        h2 = (yn * nw_ref[...]) * (1.0 + sc_ref[0]) + sh_ref[0]
        h2_ref[rows, :] = h2.astype(BF16)

        h_hi, h_lo = _split_bf16(h2)
        both = jnp.dot(h_hi, wr_split_ref[...], preferred_element_type=F32)
        logits_ref[s] = (both[:, 0:LANES] + both[:, LANES:2 * LANES]
                         + jnp.dot(h_lo, wr_split_ref[:, 0:LANES], preferred_element_type=F32)) + br_ref[...]


def _route_tile(logits, s, ltri_ref, sut_ref, route_ref, routet_ref, tcnt_ref):
    tm = logits.shape[0]
    lane = lax.broadcasted_iota(I32, (tm, LANES), 1).astype(F32)

    gl = jnp.where(lane < N_GROUPS, logits, NEG_BIG)
    gmax, gidx = _first_argmax(gl, lane)
    g_p = 1.0 / jnp.sum(jnp.exp(gl - gmax), axis=-1, keepdims=True)

    lo_lane = N_GROUPS + EXPERTS_PER_GROUP * gidx
    el = jnp.where((lane >= lo_lane) & (lane < lo_lane + EXPERTS_PER_GROUP), logits, NEG_BIG)
    m1, i1 = _first_argmax(el, lane)
    m2, i2 = _first_argmax(jnp.where(lane == i1, NEG_BIG, el), lane)
    r = jnp.exp(m2 - m1)
    p1 = 1.0 / (1.0 + r)
    p2 = r / (1.0 + r)
    e0 = i1 - N_GROUPS
    e1 = i2 - N_GROUPS

    onehot = ((lane == e0) | (lane == e1)).astype(F32)
    tile_cnt = jnp.sum(onehot, axis=0, keepdims=True)
    run_len = jnp.floor((tile_cnt + (RUN_ALIGN - 1)) * (1.0 / RUN_ALIGN)) * RUN_ALIGN
    run_start = jnp.dot(jnp.broadcast_to(run_len, (8, LANES)).astype(BF16), sut_ref[...],
                        preferred_element_type=F32)[0:1, :]
    before = jnp.dot(ltri_ref[...], onehot.astype(BF16), preferred_element_type=F32) + run_start
    slot0 = _lane_pick(before, lane, e0)
    slot1 = _lane_pick(before, lane, e1)
    tcnt_ref[s] = tile_cnt

    rec = jnp.zeros((tm, LANES), F32)
    for k, v in enumerate([slot0, slot1, g_p * p1, g_p * p2, e0, e1]):
        rec = jnp.where(lane == k, v, rec)
    route_ref[s * tm:(s + 1) * tm, :] = rec[:, 0:ROUTE_W]
    routet_ref[s * ROUTE_W:(s + 1) * ROUTE_W, :] = rec.T[0:ROUTE_W, :]


def _out_router(att, y, x2d, mod3, w_out_b, norm_w, w_router, b_router):
    tm = OUT_TM
    rows = OR_SUB * tm
    n_steps = TOKENS // rows
    steps_per_batch = SEQ // rows
    idx = jnp.arange(tm)
    ltri = (idx[None, :] < idx[:, None]).astype(BF16)
    lidx = jnp.arange(LANES)
    sut = (lidx[:, None] < lidx[None, :]).astype(BF16)
    const = lambda shape: pl.BlockSpec(shape, lambda i: (0, 0))
    cur = lambda i: jnp.minimum(i, n_steps - 1)
    prev = lambda i: jnp.maximum(i - 1, 0)
    tok = lambda width: pl.BlockSpec((rows, width), lambda i: (cur(i), 0))
    modspec = lambda k: pl.BlockSpec((1, 1, D_MODEL), lambda i: ((cur(i) // steps_per_batch) * 6 + k, 0, 0))
    return pl.pallas_call(
        _out_router_kernel,
        grid=(n_steps + 1,),
        in_specs=[tok(ATT_WIDTH), tok(SSD_WIDTH), tok(D_MODEL), modspec(2),
                  const((D_MODEL, D_MODEL)), const((1, D_MODEL)), modspec(4), modspec(3),
                  const((D_MODEL, LANES)), const((1, LANES)), const((tm, tm)), const((LANES, LANES))],
        out_specs=[tok(D_MODEL), tok(D_MODEL),
                   pl.BlockSpec((rows, ROUTE_W), lambda i: (prev(i), 0)),
                   pl.BlockSpec((OR_SUB * ROUTE_W, tm), lambda i: (prev(i), 0)),
                   pl.BlockSpec((OR_SUB, 1, LANES), lambda i: (prev(i), 0, 0))],
        out_shape=[jax.ShapeDtypeStruct((TOKENS, D_MODEL), F32),
                   jax.ShapeDtypeStruct((TOKENS, D_MODEL), BF16),
                   jax.ShapeDtypeStruct((TOKENS, ROUTE_W), F32),
                   jax.ShapeDtypeStruct((N_TOKEN_TILES * ROUTE_W, tm), F32),
                   jax.ShapeDtypeStruct((N_TOKEN_TILES, 1, LANES), F32)],
        scratch_shapes=[pltpu.VMEM((D_MODEL, 2 * LANES), BF16), pltpu.VMEM((OR_SUB, tm, LANES), F32),
                        pltpu.VMEM((D_MODEL, D_MODEL), BF16)],
        compiler_params=_cparams(("arbitrary",)),
        name="out_router",
    )(att, y, x2d, mod3, w_out_b, norm_w.reshape(1, D_MODEL), mod3, mod3, w_router, b_router, ltri, sut)


MOE_TM = 512
ZERO_ROWS = 256
N_TOKEN_TILES = TOKENS // OUT_TM
MAX_SORTED_ROWS = TOKENS * TOP_K + N_TOKEN_TILES * N_EXPERTS * (RUN_ALIGN - 1)
N_TILES = MAX_SORTED_ROWS // MOE_TM + N_EXPERTS
N_ROWS = N_TILES * MOE_TM
assert MOE_TM % ZERO_ROWS == 0


BIG_PIECE = 2 * RUN_ALIGN
PIECE_SLOTS = LOCAL_ROWS // BIG_PIECE
TABLE_W = 4 * PIECE_SLOTS
COMBINE_K = 256
LOCAL_TAIL = COMBINE_K
LOCAL_BODY = LOCAL_ROWS - LOCAL_TAIL
assert PIECE_SLOTS >= N_EXPERTS and LOCAL_ROWS % COMBINE_K == 0


def _local_rows_used(n_big, n_small):
    return n_big * BIG_PIECE + n_small * RUN_ALIGN


def _run_copies(table_ref, n_big, n_small, make_copy, action):
    def big(q, carry):
        action(make_copy(table_ref[0, 0, q], table_ref[0, 0, PIECE_SLOTS + q], BIG_PIECE))
        return carry

    def small(q, carry):
        action(make_copy(table_ref[0, 0, 2 * PIECE_SLOTS + q], table_ref[0, 0, 3 * PIECE_SLOTS + q], RUN_ALIGN))
        return carry

    lax.fori_loop(0, n_big, big, 0)
    lax.fori_loop(0, n_small, small, 0)


def _dispatch_kernel(seg_end_ref, used_end_ref, nb_ref, ns_ref, tab_ref, routet_ref, h2_ref, xs_ref,
                     sbuf_ref, zero_ref, sems, zsem):
    i = pl.program_id(0)
    last = pl.num_programs(0) - 1
    buf = lax.rem(i, 2)

    def zero_fills(action):
        def tail_copy(row):
            return pltpu.make_async_copy(zero_ref.at[pl.ds(0, RUN_ALIGN)],
                                         xs_ref.at[pl.ds(pl.multiple_of(row, RUN_ALIGN), RUN_ALIGN)], zsem)

        def block_copy(block):
            start = pl.multiple_of(block * ZERO_ROWS, ZERO_ROWS)
            return pltpu.make_async_copy(zero_ref, xs_ref.at[pl.ds(start, ZERO_ROWS)], zsem)

        def tails(e, carry):
            def body(r, c):
                action(tail_copy(r * RUN_ALIGN))
                return c

            lax.fori_loop(used_end_ref[e] // RUN_ALIGN, seg_end_ref[e] // RUN_ALIGN, body, 0)
            return carry

        def blocks(block, carry):
            action(block_copy(block))
            return carry

        lax.fori_loop(0, N_EXPERTS, tails, 0)
        lax.fori_loop(seg_end_ref[N_EXPERTS - 1] // ZERO_ROWS, N_ROWS // ZERO_ROWS, blocks, 0)

    @pl.when(i == 0)
    def _():
        zero_ref[...] = jnp.zeros_like(zero_ref)
        zero_fills(lambda cp: cp.start())

    def sort_rows(r0, rows):
        slot = (lax.broadcasted_iota(I32, (rows, OUT_TM), 0) + r0).astype(F32)
        perm = jnp.where((slot == routet_ref[0:1, :]) | (slot == routet_ref[1:2, :]), 1.0, 0.0).astype(BF16)
        sbuf_ref[buf, r0:r0 + rows, :] = jnp.dot(perm, h2_ref[...], preferred_element_type=F32).astype(BF16)

    sort_rows(0, LOCAL_BODY)

    @pl.when(_local_rows_used(nb_ref[i], ns_ref[i]) > LOCAL_BODY)
    def _():
        sort_rows(LOCAL_BODY, LOCAL_TAIL)

    def piece(b):
        def make(local, sorted_row, rows):
            return pltpu.make_async_copy(
                sbuf_ref.at[b, pl.ds(pl.multiple_of(local, RUN_ALIGN), rows)],
                xs_ref.at[pl.ds(pl.multiple_of(sorted_row, RUN_ALIGN), rows)], sems.at[b])
        return make

    _run_copies(tab_ref, nb_ref[i], ns_ref[i], piece(buf), lambda cp: cp.start())
    prev = jnp.maximum(i - 1, 0)

    @pl.when(i > 0)
    def _():
        _run_copies(tab_ref, nb_ref[prev], ns_ref[prev], lambda lo, so, rows: piece(1 - buf)(0, 0, rows),
                    lambda cp: cp.wait())

    @pl.when(i == last)
    def _():
        _run_copies(tab_ref, nb_ref[i], ns_ref[i], lambda lo, so, rows: piece(buf)(0, 0, rows),
                    lambda cp: cp.wait())
        zero_fills(lambda cp: cp.wait())


def _piece_spec(index_map):
    return pl.BlockSpec((1, 1, TABLE_W), index_map, memory_space=pltpu.SMEM)


def _dispatch(seg_end, used_end, n_big, n_small, piece_table, route_t, h2):
    grid_spec = pltpu.PrefetchScalarGridSpec(
        num_scalar_prefetch=4,
        grid=(N_TOKEN_TILES,),
        in_specs=[_piece_spec(lambda i, se, ue, nb, ns: (i, 0, 0)),
                  pl.BlockSpec((ROUTE_W, OUT_TM), lambda i, se, ue, nb, ns: (i, 0)),
                  pl.BlockSpec((OUT_TM, D_MODEL), lambda i, se, ue, nb, ns: (i, 0))],
        out_specs=pl.BlockSpec(memory_space=pl.ANY),
        scratch_shapes=[pltpu.VMEM((2, LOCAL_ROWS, D_MODEL), BF16),
                        pltpu.VMEM((ZERO_ROWS, D_MODEL), BF16),
                        pltpu.SemaphoreType.DMA((2,)), pltpu.SemaphoreType.DMA],
    )
    return pl.pallas_call(
        _dispatch_kernel,
        grid_spec=grid_spec,
        out_shape=jax.ShapeDtypeStruct((N_ROWS, D_MODEL), BF16),
        compiler_params=_cparams(("arbitrary",)),
        name="dispatch",
    )(seg_end, used_end, n_big, n_small, piece_table, route_t, h2)


X_BUFS = 3


def _experts_kernel(te_ref, seg_ref, nxt_ref, nu_ref, xs_hbm, wg_hbm, wu_hbm, wd_hbm, ys_hbm,
                    xbuf, ybuf, wg_buf, wu_buf, wd_buf, wgu_b_ref, wd_b_ref, xsem, ysem, wsem):
    n = nu_ref[0]

    def rows(t):
        return pl.ds(pl.multiple_of(t * MOE_TM, MOE_TM), MOE_TM)

    def x_copy(t, s):
        return pltpu.make_async_copy(xs_hbm.at[rows(t)], xbuf.at[s], xsem.at[s])

    def y_copy(t, s):
        return pltpu.make_async_copy(ybuf.at[s], ys_hbm.at[rows(t)], ysem.at[s])

    def weight_copies(expert, s):
        return [pltpu.make_async_copy(wg_hbm.at[expert], wg_buf.at[s], wsem.at[s]),
                pltpu.make_async_copy(wu_hbm.at[expert], wu_buf.at[s], wsem.at[s]),
                pltpu.make_async_copy(wd_hbm.at[expert], wd_buf.at[s], wsem.at[s])]

    for cp in weight_copies(te_ref[0], 0):
        cp.start()
    for t in range(X_BUFS - 1):
        @pl.when(t < n)
        def _():
            x_copy(t, t).start()

    def tile(i, carry):
        xs_slot = lax.rem(i, X_BUFS)
        ys_slot = lax.rem(i, 2)
        w_slot = lax.rem(seg_ref[i], 2)
        x_copy(i, xs_slot).wait()
        ahead = i + (X_BUFS - 1)

        @pl.when(ahead < n)
        def _():
            x_copy(ahead, lax.rem(ahead, X_BUFS)).start()

        @pl.when((i == 0) | (te_ref[i] != te_ref[jnp.maximum(i - 1, 0)]))
        def _():
            for cp in weight_copies(te_ref[i], w_slot):
                cp.wait()

            @pl.when(nxt_ref[i] >= 0)
            def _():
                for cp in weight_copies(nxt_ref[i], 1 - w_slot):
                    cp.start()

            wgu_b_ref[:, 0:D_EXPERT] = wg_buf[w_slot].astype(BF16)
            wgu_b_ref[:, D_EXPERT:2 * D_EXPERT] = wu_buf[w_slot].astype(BF16)
            wd_b_ref[...] = wd_buf[w_slot].astype(BF16)

        @pl.when(i >= 2)
        def _():
            y_copy(i - 2, ys_slot).wait()

        h = jnp.dot(xbuf[xs_slot], wgu_b_ref[...], preferred_element_type=F32)
        act = (_silu(h[:, 0:D_EXPERT]) * h[:, D_EXPERT:2 * D_EXPERT]).astype(BF16)
        ybuf[ys_slot] = jnp.dot(act, wd_b_ref[...], preferred_element_type=F32).astype(BF16)
        y_copy(i, ys_slot).start()
        return carry

    lax.fori_loop(0, n, tile, 0)

    @pl.when(n >= 2)
    def _():
        y_copy(n - 2, lax.rem(n - 2, 2)).wait()

    y_copy(n - 1, lax.rem(n - 1, 2)).wait()


def _experts(tile_expert, tile_segment, next_expert, n_used, xs, w_gate, w_up, w_down):
    n_prefetch = 4
    anywhere = pl.BlockSpec(memory_space=pl.ANY)
    grid_spec = pltpu.PrefetchScalarGridSpec(
        num_scalar_prefetch=n_prefetch,
        grid=(1,),
        in_specs=[anywhere, anywhere, anywhere, anywhere],
        out_specs=anywhere,
        scratch_shapes=[pltpu.VMEM((X_BUFS, MOE_TM, D_MODEL), BF16), pltpu.VMEM((2, MOE_TM, D_MODEL), BF16),
                        pltpu.VMEM((2, D_MODEL, D_EXPERT), F32), pltpu.VMEM((2, D_MODEL, D_EXPERT), F32),
                        pltpu.VMEM((2, D_EXPERT, D_MODEL), F32),
                        pltpu.VMEM((D_MODEL, 2 * D_EXPERT), BF16), pltpu.VMEM((D_EXPERT, D_MODEL), BF16),
                        pltpu.SemaphoreType.DMA((X_BUFS,)), pltpu.SemaphoreType.DMA((2,)),
                        pltpu.SemaphoreType.DMA((2,))],
    )
    return pl.pallas_call(
        _experts_kernel,
        grid_spec=grid_spec,
        out_shape=jax.ShapeDtypeStruct((N_ROWS, D_MODEL), BF16),
        input_output_aliases={n_prefetch: 0},
        compiler_params=_cparams(("arbitrary",)),
        name="experts",
    )(tile_expert, tile_segment, next_expert, n_used, xs, w_gate, w_up, w_down)


def _combine_kernel(nb_ref, ns_ref, tab_ref, tab_next_ref, route_ref, x1_ref, g2_ref, ys_ref, o_ref,
                    gbuf_ref, sems):
    i = pl.program_id(0)
    last = pl.num_programs(0) - 1
    buf = lax.rem(i, 2)

    def piece(b):
        def make(local, sorted_row, rows):
            return pltpu.make_async_copy(
                ys_ref.at[pl.ds(pl.multiple_of(sorted_row, RUN_ALIGN), rows)],
                gbuf_ref.at[b, pl.ds(pl.multiple_of(local, RUN_ALIGN), rows)], sems.at[b])
        return make

    @pl.when(i == 0)
    def _():
        gbuf_ref[...] = jnp.zeros_like(gbuf_ref)
        _run_copies(tab_ref, nb_ref[0], ns_ref[0], piece(0), lambda cp: cp.start())

    nxt = jnp.minimum(i + 1, last)

    @pl.when(i < last)
    def _():
        _run_copies(tab_next_ref, nb_ref[nxt], ns_ref[nxt], piece(1 - buf), lambda cp: cp.start())

    rec = route_ref[...]
    slot0 = lax.broadcasted_iota(I32, (OUT_TM, COMBINE_K), 1).astype(F32)
    _run_copies(tab_ref, nb_ref[i], ns_ref[i], lambda lo, so, rows: piece(buf)(0, 0, rows), lambda cp: cp.wait())

    def slice_sum(k0):
        s0, s1 = rec[:, 0:1] - float(k0), rec[:, 1:2] - float(k0)
        weights = (jnp.where(slot0 == s0, rec[:, 2:3], 0.0)
                   + jnp.where(slot0 == s1, rec[:, 3:4], 0.0)).astype(BF16)
        return jnp.dot(weights, gbuf_ref[buf, k0:k0 + COMBINE_K, :], preferred_element_type=F32)

    moe = jnp.zeros((OUT_TM, D_MODEL), F32)
    for k0 in range(0, LOCAL_BODY, COMBINE_K):
        moe = moe + slice_sum(k0)
    o_ref[...] = x1_ref[...] + g2_ref[0] * moe

    @pl.when(_local_rows_used(nb_ref[i], ns_ref[i]) > LOCAL_BODY)
    def _():
        o_ref[...] += g2_ref[0] * slice_sum(LOCAL_BODY)


def _combine(n_big, n_small, piece_table, route, x1, mod3, ys):
    tm = OUT_TM
    steps_per_batch = SEQ // tm
    grid_spec = pltpu.PrefetchScalarGridSpec(
        num_scalar_prefetch=2,
        grid=(N_TOKEN_TILES,),
        in_specs=[_piece_spec(lambda i, nb, ns: (i, 0, 0)),
                  _piece_spec(lambda i, nb, ns: (jnp.minimum(i + 1, N_TOKEN_TILES - 1), 0, 0)),
                  pl.BlockSpec((tm, ROUTE_W), lambda i, nb, ns: (i, 0)),
                  pl.BlockSpec((tm, D_MODEL), lambda i, nb, ns: (i, 0)),
                  pl.BlockSpec((1, 1, D_MODEL), lambda i, nb, ns: ((i // steps_per_batch) * 6 + 5, 0, 0)),
                  pl.BlockSpec(memory_space=pl.ANY)],
        out_specs=pl.BlockSpec((tm, D_MODEL), lambda i, nb, ns: (i, 0)),
        scratch_shapes=[pltpu.VMEM((2, LOCAL_ROWS, D_MODEL), BF16), pltpu.SemaphoreType.DMA((2,))],
    )
    return pl.pallas_call(
        _combine_kernel,
        grid_spec=grid_spec,
        out_shape=jax.ShapeDtypeStruct((TOKENS, D_MODEL), F32),
        compiler_params=_cparams(("arbitrary",)),
        name="combine",
    )(n_big, n_small, piece_table, piece_table, route, x1, mod3, ys)


def kernel(x, c, positions, norm1_w, norm2_w, w_ada, b_ada, w_in, conv_w, conv_b, dt_bias, a_log,
           d_skip, ssd_norm_w, q_norm_w, k_norm_w, sinks, w_out, w_group, b_group, w_expert, b_expert,
           w_gate, w_up, w_down):
    assert x.shape == (BATCH, SEQ, D_MODEL) and w_in.shape == (D_MODEL, IN_WIDTH)
    x2d = x.reshape(TOKENS, D_MODEL)
    cosf, s1, s2, mod = _rope_tables_and_mod(positions, c, w_ada, b_ada)
    mod3 = mod.reshape(BATCH * 6, 1, D_MODEL)

    qkv, z, xbc, dt, dt_t = _in_proj(x2d, norm1_w, mod3, w_in)
    att = _attention(qkv, (cosf, s1, s2), q_norm_w, k_norm_w, sinks)
    y = _ssd(xbc, z, dt, dt_t, conv_w, conv_b, dt_bias, a_log, d_skip, ssd_norm_w)

    w_router = jnp.pad(jnp.concatenate([w_group, w_expert], axis=1).astype(F32),
                       ((0, 0), (0, LANES - ROUTER_COLS)))
    b_router = jnp.pad(jnp.concatenate([b_group, b_expert]).astype(F32),
                       (0, LANES - ROUTER_COLS)).reshape(1, LANES)
    x1, h2, route, route_t, tcnt = _out_router(att, y, x2d, mod3, w_out.astype(F32), norm2_w,
                                                w_router, b_router)

    tc = tcnt[:, 0, 0:N_EXPERTS].astype(I32)
    run_rows = ((tc + RUN_ALIGN - 1) // RUN_ALIGN) * RUN_ALIGN
    counts = jnp.sum(run_rows, axis=0)
    padded = ((counts + MOE_TM - 1) // MOE_TM) * MOE_TM
    seg_end = jnp.cumsum(padded)
    seg_start = seg_end - padded
    run_dst = seg_start[None, :] + jnp.cumsum(run_rows, axis=0) - run_rows
    n_used = (seg_end[-1] // MOE_TM).reshape(1)
    last_row = jnp.minimum(jnp.arange(N_TILES, dtype=I32) * MOE_TM, seg_end[-1] - 1)
    tile_expert = jnp.sum((seg_end[None, :] <= last_row[:, None]).astype(I32), axis=1)

    run_local = jnp.cumsum(run_rows, axis=1) - run_rows
    n_big_run = run_rows // BIG_PIECE
    n_small_run = (run_rows // RUN_ALIGN) % 2
    q = jnp.arange(PIECE_SLOTS, dtype=I32)
    experts = jnp.arange(N_EXPERTS, dtype=I32)

    def flat(per_run, local0, dst0, stride):
        end = jnp.cumsum(per_run, axis=1)
        run_of = jnp.sum((end[:, None, :] <= q[None, :, None]).astype(I32), axis=2)
        pick = (run_of[:, :, None] == experts[None, None, :]).astype(I32)
        k = q[None, :] - jnp.sum(pick * (end - per_run)[:, None, :], axis=2)
        local = jnp.sum(pick * local0[:, None, :], axis=2) + stride * k
        dst = jnp.sum(pick * dst0[:, None, :], axis=2) + stride * k
        return end[:, -1], local, dst

    n_big, big_local, big_dst = flat(n_big_run, run_local, run_dst, BIG_PIECE)
    n_small, small_local, small_dst = flat(n_small_run, run_local + BIG_PIECE * n_big_run,
                                           run_dst + BIG_PIECE * n_big_run, 0)
    piece_table = jnp.concatenate([big_local, big_dst, small_local, small_dst], axis=1)
    piece_table = piece_table.astype(I32).reshape(N_TOKEN_TILES, 1, TABLE_W)

    nonempty = padded > 0
    seg_rank = jnp.cumsum(nonempty.astype(I32)) - 1
    later = nonempty[None, :] & (experts[None, :] > experts[:, None])
    next_of = jnp.min(jnp.where(later, experts[None, :], N_EXPERTS), axis=1)
    next_of = jnp.where(next_of == N_EXPERTS, -1, next_of)
    tile_is = (tile_expert[:, None] == experts[None, :]).astype(I32)
    tile_segment = jnp.sum(tile_is * seg_rank[None, :], axis=1)
    next_expert = jnp.sum(tile_is * next_of[None, :], axis=1)

    n_big, n_small = n_big.astype(I32), n_small.astype(I32)
    xs = _dispatch(seg_end.astype(I32), (seg_start + counts).astype(I32), n_big, n_small, piece_table,
                   route_t, h2)
    ys = _experts(tile_expert, tile_segment.astype(I32), next_expert.astype(I32), n_used.astype(I32),
                  xs, w_gate, w_up, w_down)
    out = _combine(n_big, n_small, piece_table, route, x1, mod3, ys)
    return out.reshape(BATCH, SEQ, D_MODEL)
```

```python
import jax
import jax.numpy as jnp
from jax import lax
from jax.experimental import pallas as pl
from jax.experimental.pallas import tpu as pltpu

F32 = jnp.float32
BF16 = jnp.bfloat16
I32 = jnp.int32

D_MODEL = 1024
BATCH = 2
SEQ = 8192
TOKENS = BATCH * SEQ
ATT_HEADS = 8
ATT_KV_HEADS = 2
HEAD_DIM = 64
ATT_WIDTH = ATT_HEADS * HEAD_DIM
KV_WIDTH = ATT_KV_HEADS * HEAD_DIM
ATT_BLOCK = 128
ROPE_DIM = HEAD_DIM // 4
ROPE_THETA = 500000.0
SSD_HEADS = 8
SSD_HEAD_DIM = 64
SSD_WIDTH = SSD_HEADS * SSD_HEAD_DIM
SSD_GROUPS = 2
SSD_STATE = 128
CONV_K = 4
CHUNK = 128
XBC_WIDTH = SSD_WIDTH + 2 * SSD_GROUPS * SSD_STATE
IN_WIDTH = ATT_WIDTH + 2 * KV_WIDTH + SSD_WIDTH + XBC_WIDTH + SSD_HEADS
N_GROUPS = 4
EXPERTS_PER_GROUP = 8
N_EXPERTS = N_GROUPS * EXPERTS_PER_GROUP
TOP_K = 2
D_EXPERT = 256
EPS = 1e-6

LANES = 128
QKV_WIDTH = ATT_WIDTH + 2 * KV_WIDTH
IN_PAD = QKV_WIDTH + SSD_WIDTH + XBC_WIDTH + LANES
NEG_BIG = -1e30

VMEM_LIMIT = 48 * 1024 * 1024


def _cparams(sem):
    return pltpu.CompilerParams(dimension_semantics=sem, vmem_limit_bytes=VMEM_LIMIT)


def _split_bf16(x):
    hi = x.astype(BF16)
    lo = (x - hi.astype(F32)).astype(BF16)
    return hi, lo


ADA_TN = 768


def _ada_kernel(ct_ref, w_ref, b_ref, o_ref):
    ct = ct_ref[...]
    s = ct * jax.nn.sigmoid(ct)
    w = w_ref[...]
    rows = [jnp.sum(s[:, b:b + 1] * w, axis=0, keepdims=True) for b in range(BATCH)]
    o_ref[...] = jnp.concatenate(rows, axis=0) + b_ref[...]


INPROJ_TM = 1024
_INPROJ_CHUNK = 256


def _inproj_kernel(x_ref, nw_ref, sc_ref, sh_ref, wf_ref, wdt_ref, qkv_ref, z_ref, xbc_ref, dt_ref, dtt_ref,
                   w_ref):
    @pl.when(pl.program_id(0) == 0)
    def _():
        for c0 in range(0, IN_PAD - LANES, _INPROJ_CHUNK):
            w_ref[:, c0:c0 + _INPROJ_CHUNK] = wf_ref[:, c0:c0 + _INPROJ_CHUNK].astype(BF16)
        w_ref[:, IN_PAD - LANES:IN_PAD] = wdt_ref[...].astype(BF16)

    x = x_ref[...]
    y = x * lax.rsqrt(jnp.mean(x * x, axis=-1, keepdims=True) + EPS)
    h = (y * nw_ref[...]) * (1.0 + sc_ref[0]) + sh_ref[0]
    hb = h.astype(BF16)

    def proj(c0, c1):
        return jnp.dot(hb, w_ref[:, c0:c1], preferred_element_type=F32)

    for c0 in range(0, QKV_WIDTH, _INPROJ_CHUNK):
        qkv_ref[:, c0:c0 + _INPROJ_CHUNK] = proj(c0, c0 + _INPROJ_CHUNK).astype(BF16)
    base = QKV_WIDTH
    for c0 in range(0, SSD_WIDTH, _INPROJ_CHUNK):
        z_ref[:, c0:c0 + _INPROJ_CHUNK] = proj(base + c0, base + c0 + _INPROJ_CHUNK).astype(BF16)
    base += SSD_WIDTH
    for c0 in range(0, XBC_WIDTH, _INPROJ_CHUNK):
        xbc_ref[:, c0:c0 + _INPROJ_CHUNK] = proj(base + c0, base + c0 + _INPROJ_CHUNK).astype(BF16)
    base += XBC_WIDTH
    dt = proj(base, base + LANES)
    dt_ref[...] = dt
    dtt_ref[...] = dt.T[0:SSD_HEADS, :]


def _in_proj(x2d, norm_w, mod3, w_in):
    tm = INPROJ_TM
    steps_per_batch = SEQ // tm
    w_dt = jnp.pad(w_in[:, IN_WIDTH - SSD_HEADS:IN_WIDTH].astype(F32), ((0, 0), (0, LANES - SSD_HEADS)))
    return pl.pallas_call(
        _inproj_kernel,
        grid=(TOKENS // tm,),
        in_specs=[pl.BlockSpec((tm, D_MODEL), lambda i: (i, 0)),
                  pl.BlockSpec((1, D_MODEL), lambda i: (0, 0)),
                  pl.BlockSpec((1, 1, D_MODEL), lambda i: ((i // steps_per_batch) * 6 + 1, 0, 0)),
                  pl.BlockSpec((1, 1, D_MODEL), lambda i: ((i // steps_per_batch) * 6 + 0, 0, 0)),
                  pl.BlockSpec((D_MODEL, IN_WIDTH), lambda i: (0, 0), pipeline_mode=pl.Buffered(1)),
                  pl.BlockSpec((D_MODEL, LANES), lambda i: (0, 0))],
        out_specs=[pl.BlockSpec((tm, QKV_WIDTH), lambda i: (i, 0)),
                   pl.BlockSpec((tm, SSD_WIDTH), lambda i: (i, 0)),
                   pl.BlockSpec((tm, XBC_WIDTH), lambda i: (i, 0)),
                   pl.BlockSpec((tm, LANES), lambda i: (i, 0)),
                   pl.BlockSpec((SSD_HEADS, tm), lambda i: (0, i))],
        out_shape=[jax.ShapeDtypeStruct((TOKENS, QKV_WIDTH), BF16),
                   jax.ShapeDtypeStruct((TOKENS, SSD_WIDTH), BF16),
                   jax.ShapeDtypeStruct((TOKENS, XBC_WIDTH), BF16),
                   jax.ShapeDtypeStruct((TOKENS, LANES), F32),
                   jax.ShapeDtypeStruct((SSD_HEADS, TOKENS), F32)],
        scratch_shapes=[pltpu.VMEM((D_MODEL, IN_PAD), BF16)],
        compiler_params=_cparams(("arbitrary",)),
        name="in_proj",
    )(x2d, norm_w.reshape(1, D_MODEL), mod3, mod3, w_in.astype(F32), w_dt)


ATT_SUB = 8


ROPE_TM = 2048
_ROPE_HALF = ROPE_DIM // 2
_TOK_PER_ROW = LANES // _ROPE_HALF


def _exact_dot(x, onehot_b):
    hi, lo = _split_bf16(x)
    return (jnp.dot(hi, onehot_b, preferred_element_type=F32)
            + jnp.dot(lo, onehot_b, preferred_element_type=F32))


def _rope_kernel(pos_ref, freq_ref, sel_ref, own_ref, gcos_ref, gs1_ref, gs2_ref, ident_ref,
                 cos_ref, s1_ref, s2_ref):
    ang = pos_ref[...].astype(F32) * freq_ref[...]
    cos_p, sin_p = jnp.cos(ang), jnp.sin(ang)
    hi_c, lo_c = _split_bf16(cos_p)
    hi_s, lo_s = _split_bf16(sin_p)
    sel = sel_ref[...]
    rows_c = jnp.dot(sel, hi_c, preferred_element_type=F32) + jnp.dot(sel, lo_c, preferred_element_type=F32)
    rows_s = jnp.dot(sel, hi_s, preferred_element_type=F32) + jnp.dot(sel, lo_s, preferred_element_type=F32)
    own = own_ref[...]
    cos_ref[...] = _exact_dot(rows_c * own, gcos_ref[...]) + ident_ref[...]
    s1_ref[...] = _exact_dot(rows_s * own, gs1_ref[...])
    s2_ref[...] = _exact_dot(rows_s * own, gs2_ref[...])


def _tables_kernel(pos_ref, freq_ref, sel_ref, own_ref, gcos_ref, gs1_ref, gs2_ref, ident_ref,
                   ct_ref, w_ref, b_ref, cos_ref, s1_ref, s2_ref, mod_ref):
    _rope_kernel(pos_ref, freq_ref, sel_ref, own_ref, gcos_ref, gs1_ref, gs2_ref, ident_ref,
                 cos_ref, s1_ref, s2_ref)
    _ada_kernel(ct_ref, w_ref, b_ref, mod_ref)


def _rope_tables_and_mod(positions, c, w_ada, b_ada):
    n_mod = w_ada.shape[1]
    assert TOKENS // ROPE_TM == n_mod // ADA_TN
    half, per_row = _ROPE_HALF, _TOK_PER_ROW
    rows = ROPE_TM // per_row
    pos_rep = jnp.repeat(positions.reshape(TOKENS).astype(I32), half).reshape(TOKENS // per_row, LANES)
    inv_freq = jnp.power(ROPE_THETA, -jnp.arange(half, dtype=F32) * 2.0 / ROPE_DIM)
    freq = jnp.tile(inv_freq, per_row).reshape(1, LANES)
    tok = jnp.arange(ROPE_TM)
    lane = jnp.arange(LANES)
    sel = (tok[:, None] // per_row == jnp.arange(rows)[None, :]).astype(BF16)
    own = (lane[None, :] // half == tok[:, None] % per_row).astype(F32)
    d = lane % HEAD_DIM
    src_f = lane % half
    hits = lambda lo, hi: ((src_f[:, None] == d[None, :] % half) & (d[None, :] >= lo) & (d[None, :] < hi))
    gcos = hits(0, ROPE_DIM).astype(BF16)
    gs1 = -hits(0, half).astype(BF16)
    gs2 = hits(half, ROPE_DIM).astype(BF16)
    ident = (d >= ROPE_DIM).astype(F32).reshape(1, LANES)
    const = lambda shape: pl.BlockSpec(shape, lambda i: (0, 0))
    out_spec = pl.BlockSpec((ROPE_TM, LANES), lambda i: (i, 0))
    out = jax.ShapeDtypeStruct((TOKENS, LANES), F32)
    in_specs = [pl.BlockSpec((rows, LANES), lambda i: (i, 0)), const((1, LANES)),
                const((ROPE_TM, rows)), const((ROPE_TM, LANES)),
                const((LANES, LANES)), const((LANES, LANES)), const((LANES, LANES)), const((1, LANES)),
                const((D_MODEL, BATCH)),
                pl.BlockSpec((D_MODEL, ADA_TN), lambda i: (0, i), pipeline_mode=pl.Buffered(3)),
                pl.BlockSpec((1, ADA_TN), lambda i: (0, i))]
    out_specs = [out_spec, out_spec, out_spec, pl.BlockSpec((BATCH, ADA_TN), lambda i: (0, i))]

    def pipelined(*refs):
        pltpu.emit_pipeline(_tables_kernel, grid=(TOKENS // ROPE_TM,), in_specs=in_specs,
                            out_specs=out_specs)(*refs)

    anywhere = pl.BlockSpec(memory_space=pl.ANY)
    return pl.pallas_call(
        pipelined,
        in_specs=[anywhere] * len(in_specs),
        out_specs=[anywhere] * len(out_specs),
        out_shape=[out, out, out, jax.ShapeDtypeStruct((BATCH, n_mod), F32)],
        compiler_params=pltpu.CompilerParams(vmem_limit_bytes=VMEM_LIMIT),
        name="rope_tables_ada_mod",
    )(pos_rep, freq, sel, own, gcos, gs1, gs2, ident, c.T, w_ada, b_ada.reshape(1, n_mod))


def _seg_meansq(xf, ones128):
    rows, width = xf.shape
    nt = width // LANES
    parts = _split_bf16(xf * xf)
    stacked = jnp.concatenate([p[:, t * LANES:(t + 1) * LANES] for p in parts for t in range(nt)], axis=0)
    tot = jnp.dot(stacked, ones128, preferred_element_type=F32)
    tiles = [tot[t * rows:(t + 1) * rows] + tot[(nt + t) * rows:(nt + t + 1) * rows] for t in range(nt)]
    return jnp.concatenate(tiles, axis=1) * (1.0 / HEAD_DIM)


def _norm_rope(x_bf, w_row, ones_bd, cosf, s1, s2):
    xf = x_bf.astype(F32)
    width = xf.shape[1]
    xn = xf * lax.rsqrt(_seg_meansq(xf, ones_bd) + EPS) * w_row
    half = ROPE_DIM // 2
    up = pltpu.roll(xn, width - half, axis=1)
    down = pltpu.roll(xn, half, axis=1)
    return xn * cosf + up * s1 + down * s2


def _attn_kernel(sink_ref, q_ref, kv_ref, cos_ref, s1_ref, s2_ref, qw_ref, kw_ref,
                 ones_ref, o_ref, kprev_ref, vprev_ref):
    j = pl.program_id(1)
    blk = ATT_BLOCK

    @pl.when(j == 0)
    def _():
        kprev_ref[...] = jnp.zeros_like(kprev_ref)
        vprev_ref[...] = jnp.zeros_like(vprev_ref)

    cos1 = cos_ref[...]
    s1_1 = s1_ref[...]
    s2_1 = s2_ref[...]
    reps = ATT_WIDTH // LANES
    cosq = jnp.concatenate([cos1] * reps, axis=1)
    s1q = jnp.concatenate([s1_1] * reps, axis=1)
    s2q = jnp.concatenate([s2_1] * reps, axis=1)

    q = _norm_rope(q_ref[...], qw_ref[...], ones_ref[...], cosq, s1q, s2q)
    qf = q * (HEAD_DIM ** -0.5)
    kv = kv_ref[...]
    kn = _norm_rope(kv[:, 0:KV_WIDTH], kw_ref[...], ones_ref[...], cos1, s1_1, s2_1)
    vn = kv[:, KV_WIDTH:2 * KV_WIDTH].astype(F32)

    kall = jnp.concatenate([kprev_ref[...], kn], axis=0)
    vall = jnp.concatenate([vprev_ref[...], vn], axis=0)
    kprev_ref[...] = kn[(ATT_SUB - 1) * blk:ATT_SUB * blk]
    vprev_ref[...] = vn[(ATT_SUB - 1) * blk:ATT_SUB * blk]

    lo_all = lax.broadcasted_iota(I32, kall.shape, 1) < HEAD_DIM
    ones_all = jnp.ones(kall.shape, BF16)

    row = lax.broadcasted_iota(I32, (2 * blk, blk), 0)
    col = lax.broadcasted_iota(I32, (2 * blk, blk), 1)
    from_prev = col > (row & (blk - 1))
    second_tile = lax.broadcasted_iota(I32, (2 * blk, 1), 0) >= blk
    zero_p = jnp.zeros((2 * blk, blk), F32)

    k_par, v_par = [], []
    for g in range(ATT_KV_HEADS):
        keep = lo_all if g == 0 else ~lo_all
        k_own = jnp.where(keep, kall, 0.0)
        v_own = jnp.where(keep, vall, 0.0)
        k_oth = pltpu.roll(k_own, HEAD_DIM, axis=1)
        v_oth = pltpu.roll(v_own, HEAD_DIM, axis=1)
        k_lo, k_hi = (k_own, k_oth) if g == 0 else (k_oth, k_own)
        v_lo, v_hi = (v_own, v_oth) if g == 0 else (v_oth, v_own)
        k_par.append((k_lo.astype(BF16), k_hi.astype(BF16)))
        v_par.append((jnp.concatenate([v_lo.astype(BF16), ones_all], axis=1),
                      jnp.concatenate([v_hi.astype(BF16), ones_all], axis=1)))

    problems = [(g, sub) for g in range(ATT_KV_HEADS) for sub in range(ATT_SUB)]
    scores = []
    for g, sub in problems:
        r0, c0 = sub * blk, g * 2 * LANES
        qcat = jnp.concatenate([qf[r0:r0 + blk, c0:c0 + LANES],
                                qf[r0:r0 + blk, c0 + LANES:c0 + 2 * LANES]], axis=0).astype(BF16)
        kw = jnp.concatenate([k_par[g][0][r0:r0 + 2 * blk], k_par[g][1][r0:r0 + 2 * blk]], axis=0)
        scores.append(lax.dot_general(qcat, kw, (((1,), (1,)), ((), ())),
                                      preferred_element_type=F32))

    weights, rescale = [], []
    for (g, sub), s_all in zip(problems, scores):
        for par in range(2):
            s = s_all[:, par * 2 * blk:(par + 1) * 2 * blk]
            s_prev = s[:, 0:blk]
            if sub == 0:
                s_prev = s_prev + jnp.where(j > 0, 0.0, NEG_BIG)
            s = jnp.where(from_prev, s_prev, s[:, blk:2 * blk])
            h_first = ATT_HEADS // ATT_KV_HEADS * g + par
            sink = jnp.where(second_tile, sink_ref[h_first + 2], sink_ref[h_first])
            m = jnp.maximum(jnp.max(s, axis=-1, keepdims=True), sink)
            p = jnp.exp(s - m)
            weights.append(jnp.concatenate([jnp.where(from_prev, p, zero_p), jnp.where(from_prev, zero_p, p)],
                                           axis=1).astype(BF16))
            rescale.append(jnp.exp(sink - m))

    outs = []
    for idx, (g, sub) in enumerate(problems):
        for par in range(2):
            outs.append(jnp.dot(weights[2 * idx + par], v_par[g][par][sub * blk:(sub + 2) * blk],
                                preferred_element_type=F32))

    for idx, (g, sub) in enumerate(problems):
        r0, c0 = sub * blk, g * 2 * LANES
        pair = None
        for par in range(2):
            o = outs[2 * idx + par]
            part = o[:, 0:LANES] * (1.0 / (o[:, LANES:2 * LANES] + rescale[2 * idx + par]))
            pair = part if pair is None else pair + part
        o_ref[r0:r0 + blk, c0:c0 + LANES] = pair[0:blk].astype(BF16)
        o_ref[r0:r0 + blk, c0 + LANES:c0 + 2 * LANES] = pair[blk:2 * blk].astype(BF16)


def _attention(qkv, rope_tables, q_norm_w, k_norm_w, sinks):
    cosf, s1, s2 = rope_tables
    qw = jnp.tile(q_norm_w.astype(F32), ATT_HEADS).reshape(1, ATT_WIDTH)
    kw = jnp.tile(k_norm_w.astype(F32), ATT_KV_HEADS).reshape(1, KV_WIDTH)
    seg = jnp.arange(LANES) // HEAD_DIM
    ones128 = (seg[:, None] == seg[None, :]).astype(BF16)
    const = lambda shape: pl.BlockSpec(shape, lambda b, j, s: (0, 0))
    rows = ATT_SUB * ATT_BLOCK
    nb = SEQ // rows
    tok = lambda width, cb: pl.BlockSpec((rows, width), lambda b, j, s: (b * nb + j, cb))
    grid_spec = pltpu.PrefetchScalarGridSpec(
        num_scalar_prefetch=1,
        grid=(BATCH, nb),
        in_specs=[tok(ATT_WIDTH, 0), tok(2 * KV_WIDTH, 2), tok(LANES, 0), tok(LANES, 0), tok(LANES, 0),
                  const((1, ATT_WIDTH)), const((1, KV_WIDTH)), const((LANES, LANES))],
        out_specs=tok(ATT_WIDTH, 0),
        scratch_shapes=[pltpu.VMEM((ATT_BLOCK, KV_WIDTH), F32),
                        pltpu.VMEM((ATT_BLOCK, KV_WIDTH), F32)],
    )
    return pl.pallas_call(
        _attn_kernel,
        grid_spec=grid_spec,
        out_shape=jax.ShapeDtypeStruct((TOKENS, ATT_WIDTH), BF16),
        compiler_params=_cparams(("arbitrary", "arbitrary")),
        name="attention",
    )(sinks.astype(F32), qkv, qkv, cosf, s1, s2, qw, kw, ones128)


SSD_SUB = 4


def _softplus(x):
    return jnp.maximum(x, 0.0) + jnp.log1p(jnp.exp(-jnp.abs(x)))


def _silu(x):
    h = 0.5 * x
    return h + h * jnp.tanh(h)


def _ssd_kernel(xbc_ref, z_ref, dt_ref, dtt_ref, cw_ref, cb_ref, dtb_row_ref, dtb_col_ref,
                alog_row_ref, alog_col_ref, dskip_ref, nw_ref, tril_ref, triu_ref,
                o_ref, conv_ref, state_ref):
    c = pl.program_id(1)
    L = CHUNK
    tail = 8

    @pl.when(c == 0)
    def _():
        conv_ref[0:tail, :] = jnp.zeros((tail, XBC_WIDTH), F32)
        state_ref[...] = jnp.zeros_like(state_ref)

    row = lax.broadcasted_iota(I32, (L, L), 0)
    col = lax.broadcasted_iota(I32, (L, L), 1)
    causal = col <= row
    lane = lax.broadcasted_iota(I32, (L, LANES), 1)
    lo_half = lane < SSD_HEAD_DIM

    prepared = [_ssd_prepare(s * L, xbc_ref, dt_ref, dtt_ref, cw_ref, cb_ref, dtb_row_ref, dtb_col_ref,
                             alog_row_ref, alog_col_ref, tril_ref, triu_ref, conv_ref)
                for s in range(SSD_SUB)]
    for s in range(SSD_SUB):
        _ssd_chunk(s * L, prepared[s], causal, lo_half, z_ref, dskip_ref, nw_ref, o_ref, state_ref)


def _ssd_prepare(r0, xbc_ref, dt_ref, dtt_ref, cw_ref, cb_ref, dtb_row_ref, dtb_col_ref,
                 alog_row_ref, alog_col_ref, tril_ref, triu_ref, conv_ref):
    L = CHUNK
    tail = 8
    xb = xbc_ref[r0:r0 + L, :].astype(F32)
    conv_ref[tail:tail + L, :] = xb
    acc = cb_ref[...] + cw_ref[CONV_K - 1:CONV_K, :] * xb
    for k in range(CONV_K - 1):
        off = tail - (CONV_K - 1) + k
        acc = acc + cw_ref[k:k + 1, :] * conv_ref[off:off + L, :]
    conv_ref[0:tail, :] = xb[L - tail:L, :]
    u = _silu(acc)
    xs = u[:, 0:SSD_WIDTH]
    bmat = u[:, SSD_WIDTH:SSD_WIDTH + SSD_GROUPS * SSD_STATE]
    cmat = u[:, SSD_WIDTH + SSD_GROUPS * SSD_STATE:XBC_WIDTH]

    dt = _softplus(dt_ref[r0:r0 + L, :] + dtb_row_ref[...])
    a = dt * (-jnp.exp(alog_row_ref[...]))
    a_hi, a_lo = _split_bf16(a)
    a_cum = (jnp.dot(tril_ref[...], a_hi, preferred_element_type=F32)
             + jnp.dot(tril_ref[...], a_lo, preferred_element_type=F32))
    dt_t = _softplus(dtt_ref[:, r0:r0 + L] + dtb_col_ref[...])
    a_t = dt_t * (-jnp.exp(alog_col_ref[...]))
    at_hi, at_lo = _split_bf16(a_t)
    a_cum_t = (jnp.dot(at_hi, triu_ref[...], preferred_element_type=F32)
               + jnp.dot(at_lo, triu_ref[...], preferred_element_type=F32))
    a_end_t = a_cum_t[:, L - 1:L]
    return dict(
        xs=xs, bmat=bmat, cmat=cmat, a_cum=a_cum, exp_a_cum=jnp.exp(a_cum),
        shifted_t=a_cum_t - jnp.log(dt_t),
        wst_t=jnp.exp(a_end_t - a_cum_t) * dt_t,
        cdec_t=jnp.exp(a_end_t))


def _ssd_chunk(r0, p, causal, lo_half, z_ref, dskip_ref, nw_ref, o_ref, state_ref):
    L = CHUNK
    xs, bmat, cmat, a_cum, exp_a_cum = p["xs"], p["bmat"], p["cmat"], p["a_cum"], p["exp_a_cum"]
    shifted_t, wst_t, cdec_t = p["shifted_t"], p["wst_t"], p["cdec_t"]
    xs_b = xs.astype(BF16)
    heads_per_group = SSD_HEADS // SSD_GROUPS
    gated = []
    for g in range(SSD_GROUPS):
        b_g = bmat[:, g * SSD_STATE:(g + 1) * SSD_STATE]
        c_g = cmat[:, g * SSD_STATE:(g + 1) * SSD_STATE]
        cb = lax.dot_general(c_g.astype(BF16), b_g.astype(BF16), (((1,), (1,)), ((), ())),
                             preferred_element_type=F32)
        b_gt = b_g.T
        for t in range(heads_per_group // 2):
            tile = g * (heads_per_group // 2) + t
            c0 = tile * LANES
            xs_tile = xs_b[:, c0:c0 + LANES]
            st_tile = state_ref[:, c0:c0 + LANES]
            st_b = st_tile.astype(BF16)
            y_tile = jnp.zeros((L, LANES), F32)
            new_tile = jnp.zeros((SSD_STATE, LANES), F32)
            for e in range(2):
                h = 2 * tile + e
                keep = lo_half if e == 0 else ~lo_half
                colb = jnp.broadcast_to(a_cum[:, h:h + 1], (L, L))
                rowb = shifted_t[h:h + 1, :]
                w_in = cb * jnp.exp(jnp.where(causal, colb - rowb, NEG_BIG))
                w_off = c_g * jnp.broadcast_to(exp_a_cum[:, h:h + 1], (L, L))
                lhs = jnp.concatenate([w_in, w_off], axis=1).astype(BF16)
                rhs = jnp.concatenate([jnp.where(keep, xs_tile, jnp.zeros_like(xs_tile)),
                                       jnp.where(keep, st_b, jnp.zeros_like(st_b))], axis=0)
                y_tile = y_tile + jnp.dot(lhs, rhs, preferred_element_type=F32)
                m_h = (b_gt * wst_t[h:h + 1, :]).astype(BF16)
                new_tile = new_tile + jnp.dot(m_h, jnp.where(keep, xs_tile, jnp.zeros_like(xs_tile)),
                                              preferred_element_type=F32)
            cd = jnp.where(lo_half[0:1, :], cdec_t[2 * tile:2 * tile + 1, :],
                           cdec_t[2 * tile + 1:2 * tile + 2, :])
            state_ref[:, c0:c0 + LANES] = st_tile * cd + new_tile
            y_full = y_tile + dskip_ref[:, c0:c0 + LANES] * xs[:, c0:c0 + LANES]
            gated.append(y_full * _silu(z_ref[r0:r0 + L, c0:c0 + LANES].astype(F32)))

    gw = SSD_WIDTH // SSD_GROUPS
    tiles_per_group = gw // LANES
    for g in range(SSD_GROUPS):
        yg = jnp.concatenate(gated[g * tiles_per_group:(g + 1) * tiles_per_group], axis=1)
        ms = jnp.mean(yg * yg, axis=-1, keepdims=True)
        o_ref[r0:r0 + L, g * gw:(g + 1) * gw] = (
            (yg * lax.rsqrt(ms + EPS)) * nw_ref[:, g * gw:(g + 1) * gw]).astype(o_ref.dtype)


def _ssd(xbc, z, dt, dt_t, conv_w, conv_b, dt_bias, a_log, d_skip, ssd_norm_w):
    L = SSD_SUB * CHUNK
    nc = SEQ // L
    pad_row = lambda v: jnp.pad(v.astype(F32), (0, LANES - SSD_HEADS)).reshape(1, LANES)
    col8 = lambda v: v.astype(F32).reshape(SSD_HEADS, 1)
    idx = jnp.arange(CHUNK)
    tril = (idx[None, :] <= idx[:, None]).astype(BF16)
    triu = (idx[:, None] <= idx[None, :]).astype(BF16)
    dskip = jnp.repeat(d_skip.astype(F32), SSD_HEAD_DIM).reshape(1, SSD_WIDTH)
    const = lambda shape: pl.BlockSpec(shape, lambda b, c: (0, 0))
    tok = lambda width: pl.BlockSpec((L, width), lambda b, c: (b * nc + c, 0))
    return pl.pallas_call(
        _ssd_kernel,
        grid=(BATCH, nc),
        in_specs=[tok(XBC_WIDTH), tok(SSD_WIDTH), tok(LANES),
                  pl.BlockSpec((SSD_HEADS, L), lambda b, c: (0, b * nc + c)),
                  const((CONV_K, XBC_WIDTH)), const((1, XBC_WIDTH)),
                  const((1, LANES)), const((SSD_HEADS, 1)), const((1, LANES)), const((SSD_HEADS, 1)),
                  const((1, SSD_WIDTH)), const((1, SSD_WIDTH)), const((CHUNK, CHUNK)), const((CHUNK, CHUNK))],
        out_specs=tok(SSD_WIDTH),
        out_shape=jax.ShapeDtypeStruct((TOKENS, SSD_WIDTH), BF16),
        scratch_shapes=[pltpu.VMEM((8 + CHUNK, XBC_WIDTH), F32),
                        pltpu.VMEM((SSD_STATE, SSD_WIDTH), F32)],
        compiler_params=_cparams(("arbitrary", "arbitrary")),
        name="ssd",
    )(xbc, z, dt, dt_t, conv_w.astype(F32), conv_b.astype(F32).reshape(1, XBC_WIDTH),
      pad_row(dt_bias), col8(dt_bias), pad_row(a_log), col8(a_log), dskip,
      ssd_norm_w.astype(F32).reshape(1, SSD_WIDTH), tril, triu)


OUT_TM = 512
OR_SUB = 2
ROUTE_W = 8
ROUTER_COLS = N_GROUPS + N_EXPERTS
RUN_ALIGN = 16
RUN_SHIFT = 4
LOCAL_ROWS = 1536
assert RUN_ALIGN == 1 << RUN_SHIFT and LOCAL_ROWS >= TOP_K * OUT_TM + N_EXPERTS * (RUN_ALIGN - 1)


def _lane_pick(values, lane, index):
    return jnp.sum(jnp.where(lane == index, values, 0.0), axis=-1, keepdims=True)


def _first_argmax(vals, lane):
    m = jnp.max(vals, axis=-1, keepdims=True)
    idx = jnp.min(jnp.where(vals == m, lane, float(LANES)), axis=-1, keepdims=True)
    return m, idx


def _out_router_kernel(i, last, att_ref, y_ref, x_ref, g1_ref, wof_ref, nw_ref, sc_ref, sh_ref, wr_ref, br_ref,
                       ltri_ref, sut_ref, x1_ref, h2_ref, route_ref, routet_ref, tcnt_ref,
                       wr_split_ref, logits_ref, wo_ref):
    @pl.when(i == 0)
    def _():
        hi, lo = _split_bf16(wr_ref[...])
        wr_split_ref[:, 0:LANES] = hi
        wr_split_ref[:, LANES:2 * LANES] = lo
        logits_ref[...] = jnp.zeros_like(logits_ref)
        for r0 in range(0, D_MODEL, _INPROJ_CHUNK):
            wo_ref[r0:r0 + _INPROJ_CHUNK, :] = wof_ref[r0:r0 + _INPROJ_CHUNK, :].astype(BF16)

    def route(previous):
        for s in range(OR_SUB):
            _route_tile(previous[s], s, ltri_ref, sut_ref, route_ref, routet_ref, tcnt_ref)

    @pl.when(i == last)
    def _():
        route([logits_ref[s] for s in range(OR_SUB)])

    @pl.when(i < last)
    def _():
        previous = [logits_ref[s] for s in range(OR_SUB)]
        _mix_norm_logits(att_ref, y_ref, x_ref, g1_ref, nw_ref, sc_ref, sh_ref, br_ref,
                         x1_ref, h2_ref, wr_split_ref, logits_ref, wo_ref)
        route(previous)


def _mix_norm_logits(att_ref, y_ref, x_ref, g1_ref, nw_ref, sc_ref, sh_ref, br_ref,
                     x1_ref, h2_ref, wr_split_ref, logits_ref, wo_ref):
    tm = OUT_TM
    for s in range(OR_SUB):
        rows = slice(s * tm, (s + 1) * tm)
        mixer = (jnp.dot(att_ref[rows, :], wo_ref[0:ATT_WIDTH, :], preferred_element_type=F32)
                 + jnp.dot(y_ref[rows, :], wo_ref[ATT_WIDTH:ATT_WIDTH + SSD_WIDTH, :], preferred_element_type=F32))
        x1 = x_ref[rows, :] + g1_ref[0] * mixer
        x1_ref[rows, :] = x1
        yn = x1 * lax.rsqrt(jnp.mean(x1 * x1, axis=-1, keepdims=True) + EPS)
        h2 = (yn * nw_ref[...]) * (1.0 + sc_ref[0]) + sh_ref[0]
        h2_ref[rows, :] = h2.astype(BF16)

        h_hi, h_lo = _split_bf16(h2)
        both = jnp.dot(h_hi, wr_split_ref[...], preferred_element_type=F32)
        logits_ref[s] = (both[:, 0:LANES] + both[:, LANES:2 * LANES]
                         + jnp.dot(h_lo, wr_split_ref[:, 0:LANES], preferred_element_type=F32)) + br_ref[...]


def _route_tile(logits, s, ltri_ref, sut_ref, route_ref, routet_ref, tcnt_ref):
    tm = logits.shape[0]
    lane = lax.broadcasted_iota(I32, (tm, LANES), 1).astype(F32)

    gl = jnp.where(lane < N_GROUPS, logits, NEG_BIG)
    gmax, gidx = _first_argmax(gl, lane)
    g_p = 1.0 / jnp.sum(jnp.exp(gl - gmax), axis=-1, keepdims=True)

    lo_lane = N_GROUPS + EXPERTS_PER_GROUP * gidx
    el = jnp.where((lane >= lo_lane) & (lane < lo_lane + EXPERTS_PER_GROUP), logits, NEG_BIG)
    m1, i1 = _first_argmax(el, lane)
    m2, i2 = _first_argmax(jnp.where(lane == i1, NEG_BIG, el), lane)
    r = jnp.exp(m2 - m1)
    p1 = 1.0 / (1.0 + r)
    p2 = r / (1.0 + r)
    e0 = i1 - N_GROUPS
    e1 = i2 - N_GROUPS

    onehot = ((lane == e0) | (lane == e1)).astype(F32)
    tile_cnt = jnp.sum(onehot, axis=0, keepdims=True)
    run_len = jnp.floor((tile_cnt + (RUN_ALIGN - 1)) * (1.0 / RUN_ALIGN)) * RUN_ALIGN
    run_start = jnp.dot(jnp.broadcast_to(run_len, (8, LANES)).astype(BF16), sut_ref[...],
                        preferred_element_type=F32)[0:1, :]
    before = jnp.dot(ltri_ref[...], onehot.astype(BF16), preferred_element_type=F32) + run_start
    slot0 = _lane_pick(before, lane, e0)
    slot1 = _lane_pick(before, lane, e1)
    tcnt_ref[s] = tile_cnt

    rec = jnp.zeros((tm, LANES), F32)
    for k, v in enumerate([slot0, slot1, g_p * p1, g_p * p2, e0, e1]):
        rec = jnp.where(lane == k, v, rec)
    route_ref[s * tm:(s + 1) * tm, :] = rec[:, 0:ROUTE_W]
    routet_ref[s * ROUTE_W:(s + 1) * ROUTE_W, :] = rec.T[0:ROUTE_W, :]


def _out_router(att, y, x2d, mod3, w_out_b, norm_w, w_router, b_router):
    tm = OUT_TM
    rows = OR_SUB * tm
    n_steps = TOKENS // rows
    steps_per_batch = SEQ // rows
    idx = jnp.arange(tm)
    ltri = (idx[None, :] < idx[:, None]).astype(BF16)
    lidx = jnp.arange(LANES)
    sut = (lidx[:, None] < lidx[None, :]).astype(BF16)
    const = lambda shape: pl.BlockSpec(shape, lambda i: (0, 0))
    cur = lambda i: jnp.minimum(i, n_steps - 1)
    prev = lambda i: jnp.maximum(i - 1, 0)
    tok = lambda width: pl.BlockSpec((rows, width), lambda i: (cur(i), 0))
    modspec = lambda k: pl.BlockSpec((1, 1, D_MODEL), lambda i: ((cur(i) // steps_per_batch) * 6 + k, 0, 0))
    x_spec = pl.BlockSpec((rows, D_MODEL), lambda i: (cur(i), 0), pipeline_mode=pl.Buffered(3))
    in_specs = [tok(ATT_WIDTH), tok(SSD_WIDTH), x_spec, modspec(2),
                const((D_MODEL, D_MODEL)), const((1, D_MODEL)), modspec(4), modspec(3),
                const((D_MODEL, LANES)), const((1, LANES)), const((tm, tm)), const((LANES, LANES))]
    out_specs = [tok(D_MODEL), tok(D_MODEL),
                 pl.BlockSpec((rows, ROUTE_W), lambda i: (prev(i), 0)),
                 pl.BlockSpec((OR_SUB * ROUTE_W, tm), lambda i: (prev(i), 0)),
                 pl.BlockSpec((OR_SUB, 1, LANES), lambda i: (prev(i), 0, 0))]
    n_hbm = len(in_specs) + len(out_specs)

    def pipelined(*refs):
        hbm, scratch, step_ref = refs[:n_hbm], refs[n_hbm:-1], refs[-1]
        step_ref[0] = 0

        def step(*blocks):
            i = step_ref[0]
            _out_router_kernel(i, n_steps, *blocks, *scratch)
            step_ref[0] = i + 1

        pltpu.emit_pipeline(step, grid=(n_steps + 1,), in_specs=in_specs, out_specs=out_specs)(*hbm)

    anywhere = pl.BlockSpec(memory_space=pl.ANY)
    return pl.pallas_call(
        pipelined,
        in_specs=[anywhere] * len(in_specs),
        out_specs=[anywhere] * len(out_specs),
        out_shape=[jax.ShapeDtypeStruct((TOKENS, D_MODEL), F32),
                   jax.ShapeDtypeStruct((TOKENS, D_MODEL), BF16),
                   jax.ShapeDtypeStruct((TOKENS, ROUTE_W), F32),
                   jax.ShapeDtypeStruct((N_TOKEN_TILES * ROUTE_W, tm), F32),
                   jax.ShapeDtypeStruct((N_TOKEN_TILES, 1, LANES), F32)],
        scratch_shapes=[pltpu.VMEM((D_MODEL, 2 * LANES), BF16), pltpu.VMEM((OR_SUB, tm, LANES), F32),
                        pltpu.VMEM((D_MODEL, D_MODEL), BF16), pltpu.SMEM((1,), I32)],
        compiler_params=pltpu.CompilerParams(vmem_limit_bytes=VMEM_LIMIT),
        name="out_router",
    )(att, y, x2d, mod3, w_out_b, norm_w.reshape(1, D_MODEL), mod3, mod3, w_router, b_router, ltri, sut)


MOE_TM = 512
ZERO_ROWS = 256
N_TOKEN_TILES = TOKENS // OUT_TM
MAX_SORTED_ROWS = TOKENS * TOP_K + N_TOKEN_TILES * N_EXPERTS * (RUN_ALIGN - 1)
N_TILES = MAX_SORTED_ROWS // MOE_TM + N_EXPERTS
N_ROWS = N_TILES * MOE_TM
assert MOE_TM % ZERO_ROWS == 0


BIG_PIECE = 2 * RUN_ALIGN
PIECE_SLOTS = LOCAL_ROWS // BIG_PIECE
TABLE_W = 4 * PIECE_SLOTS
COMBINE_K = 256
LOCAL_TAIL = COMBINE_K
LOCAL_BODY = LOCAL_ROWS - LOCAL_TAIL
assert PIECE_SLOTS >= N_EXPERTS and LOCAL_ROWS % COMBINE_K == 0


def _local_rows_used(n_big, n_small):
    return n_big * BIG_PIECE + n_small * RUN_ALIGN


def _run_copies(table_ref, n_big, n_small, make_copy, action):
    def big(q, carry):
        action(make_copy(table_ref[0, 0, q], table_ref[0, 0, PIECE_SLOTS + q], BIG_PIECE))
        return carry

    def small(q, carry):
        action(make_copy(table_ref[0, 0, 2 * PIECE_SLOTS + q], table_ref[0, 0, 3 * PIECE_SLOTS + q], RUN_ALIGN))
        return carry

    lax.fori_loop(0, n_big, big, 0)
    lax.fori_loop(0, n_small, small, 0)


def _dispatch_kernel(seg_end_ref, used_end_ref, nb_ref, ns_ref, tab_ref, routet_ref, h2_ref, xs_ref,
                     sbuf_ref, zero_ref, sems, zsem):
    i = pl.program_id(0)
    last = pl.num_programs(0) - 1
    buf = lax.rem(i, 2)

    def zero_fills(action):
        def tail_copy(row):
            return pltpu.make_async_copy(zero_ref.at[pl.ds(0, RUN_ALIGN)],
                                         xs_ref.at[pl.ds(pl.multiple_of(row, RUN_ALIGN), RUN_ALIGN)], zsem)

        def block_copy(block):
            start = pl.multiple_of(block * ZERO_ROWS, ZERO_ROWS)
            return pltpu.make_async_copy(zero_ref, xs_ref.at[pl.ds(start, ZERO_ROWS)], zsem)

        def tails(e, carry):
            def body(r, c):
                action(tail_copy(r * RUN_ALIGN))
                return c

            lax.fori_loop(used_end_ref[e] // RUN_ALIGN, seg_end_ref[e] // RUN_ALIGN, body, 0)
            return carry

        def blocks(block, carry):
            action(block_copy(block))
            return carry

        lax.fori_loop(0, N_EXPERTS, tails, 0)
        lax.fori_loop(seg_end_ref[N_EXPERTS - 1] // ZERO_ROWS, N_ROWS // ZERO_ROWS, blocks, 0)

    @pl.when(i == 0)
    def _():
        zero_ref[...] = jnp.zeros_like(zero_ref)
        zero_fills(lambda cp: cp.start())

    def sort_rows(r0, rows):
        slot = (lax.broadcasted_iota(I32, (rows, OUT_TM), 0) + r0).astype(F32)
        perm = jnp.where((slot == routet_ref[0:1, :]) | (slot == routet_ref[1:2, :]), 1.0, 0.0).astype(BF16)
        sbuf_ref[buf, r0:r0 + rows, :] = jnp.dot(perm, h2_ref[...], preferred_element_type=F32).astype(BF16)

    sort_rows(0, LOCAL_BODY)

    @pl.when(_local_rows_used(nb_ref[i], ns_ref[i]) > LOCAL_BODY)
    def _():
        sort_rows(LOCAL_BODY, LOCAL_TAIL)

    def piece(b):
        def make(local, sorted_row, rows):
            return pltpu.make_async_copy(
                sbuf_ref.at[b, pl.ds(pl.multiple_of(local, RUN_ALIGN), rows)],
                xs_ref.at[pl.ds(pl.multiple_of(sorted_row, RUN_ALIGN), rows)], sems.at[b])
        return make

    _run_copies(tab_ref, nb_ref[i], ns_ref[i], piece(buf), lambda cp: cp.start())
    prev = jnp.maximum(i - 1, 0)

    @pl.when(i > 0)
    def _():
        _run_copies(tab_ref, nb_ref[prev], ns_ref[prev], lambda lo, so, rows: piece(1 - buf)(0, 0, rows),
                    lambda cp: cp.wait())

    @pl.when(i == last)
    def _():
        _run_copies(tab_ref, nb_ref[i], ns_ref[i], lambda lo, so, rows: piece(buf)(0, 0, rows),
                    lambda cp: cp.wait())
        zero_fills(lambda cp: cp.wait())


def _piece_spec(index_map):
    return pl.BlockSpec((1, 1, TABLE_W), index_map, memory_space=pltpu.SMEM)


def _dispatch(seg_end, used_end, n_big, n_small, piece_table, route_t, h2):
    grid_spec = pltpu.PrefetchScalarGridSpec(
        num_scalar_prefetch=4,
        grid=(N_TOKEN_TILES,),
        in_specs=[_piece_spec(lambda i, se, ue, nb, ns: (i, 0, 0)),
                  pl.BlockSpec((ROUTE_W, OUT_TM), lambda i, se, ue, nb, ns: (i, 0)),
                  pl.BlockSpec((OUT_TM, D_MODEL), lambda i, se, ue, nb, ns: (i, 0))],
        out_specs=pl.BlockSpec(memory_space=pl.ANY),
        scratch_shapes=[pltpu.VMEM((2, LOCAL_ROWS, D_MODEL), BF16),
                        pltpu.VMEM((ZERO_ROWS, D_MODEL), BF16),
                        pltpu.SemaphoreType.DMA((2,)), pltpu.SemaphoreType.DMA],
    )
    return pl.pallas_call(
        _dispatch_kernel,
        grid_spec=grid_spec,
        out_shape=jax.ShapeDtypeStruct((N_ROWS, D_MODEL), BF16),
        compiler_params=_cparams(("arbitrary",)),
        name="dispatch",
    )(seg_end, used_end, n_big, n_small, piece_table, route_t, h2)


X_BUFS = 3


def _experts_kernel(te_ref, seg_ref, nxt_ref, nu_ref, xs_hbm, wg_hbm, wu_hbm, wd_hbm, ys_hbm,
                    xbuf, ybuf, wg_buf, wu_buf, wd_buf, wgu_b_ref, wd_b_ref, xsem, ysem, wsem):
    n = nu_ref[0]

    def rows(t):
        return pl.ds(pl.multiple_of(t * MOE_TM, MOE_TM), MOE_TM)

    def x_copy(t, s):
        return pltpu.make_async_copy(xs_hbm.at[rows(t)], xbuf.at[s], xsem.at[s])

    def y_copy(t, s):
        return pltpu.make_async_copy(ybuf.at[s], ys_hbm.at[rows(t)], ysem.at[s])

    def weight_copies(expert, s):
        return [pltpu.make_async_copy(wg_hbm.at[expert], wg_buf.at[s], wsem.at[s]),
                pltpu.make_async_copy(wu_hbm.at[expert], wu_buf.at[s], wsem.at[s]),
                pltpu.make_async_copy(wd_hbm.at[expert], wd_buf.at[s], wsem.at[s])]

    for cp in weight_copies(te_ref[0], 0):
        cp.start()
    for t in range(X_BUFS - 1):
        @pl.when(t < n)
        def _():
            x_copy(t, t).start()

    def tile(i, carry):
        xs_slot = lax.rem(i, X_BUFS)
        ys_slot = lax.rem(i, 2)
        w_slot = lax.rem(seg_ref[i], 2)
        x_copy(i, xs_slot).wait()
        ahead = i + (X_BUFS - 1)

        @pl.when(ahead < n)
        def _():
            x_copy(ahead, lax.rem(ahead, X_BUFS)).start()

        @pl.when((i == 0) | (te_ref[i] != te_ref[jnp.maximum(i - 1, 0)]))
        def _():
            for cp in weight_copies(te_ref[i], w_slot):
                cp.wait()

            @pl.when(nxt_ref[i] >= 0)
            def _():
                for cp in weight_copies(nxt_ref[i], 1 - w_slot):
                    cp.start()

            wgu_b_ref[:, 0:D_EXPERT] = wg_buf[w_slot].astype(BF16)
            wgu_b_ref[:, D_EXPERT:2 * D_EXPERT] = wu_buf[w_slot].astype(BF16)
            wd_b_ref[...] = wd_buf[w_slot].astype(BF16)

        @pl.when(i >= 2)
        def _():
            y_copy(i - 2, ys_slot).wait()

        h = jnp.dot(xbuf[xs_slot], wgu_b_ref[...], preferred_element_type=F32)
        act = (_silu(h[:, 0:D_EXPERT]) * h[:, D_EXPERT:2 * D_EXPERT]).astype(BF16)
        ybuf[ys_slot] = jnp.dot(act, wd_b_ref[...], preferred_element_type=F32).astype(BF16)
        y_copy(i, ys_slot).start()
        return carry

    lax.fori_loop(0, n, tile, 0)

    @pl.when(n >= 2)
    def _():
        y_copy(n - 2, lax.rem(n - 2, 2)).wait()

    y_copy(n - 1, lax.rem(n - 1, 2)).wait()


def _experts(tile_expert, tile_segment, next_expert, n_used, xs, w_gate, w_up, w_down):
    n_prefetch = 4
    anywhere = pl.BlockSpec(memory_space=pl.ANY)
    grid_spec = pltpu.PrefetchScalarGridSpec(
        num_scalar_prefetch=n_prefetch,
        grid=(1,),
        in_specs=[anywhere, anywhere, anywhere, anywhere],
        out_specs=anywhere,
        scratch_shapes=[pltpu.VMEM((X_BUFS, MOE_TM, D_MODEL), BF16), pltpu.VMEM((2, MOE_TM, D_MODEL), BF16),
                        pltpu.VMEM((2, D_MODEL, D_EXPERT), F32), pltpu.VMEM((2, D_MODEL, D_EXPERT), F32),
                        pltpu.VMEM((2, D_EXPERT, D_MODEL), F32),
                        pltpu.VMEM((D_MODEL, 2 * D_EXPERT), BF16), pltpu.VMEM((D_EXPERT, D_MODEL), BF16),
                        pltpu.SemaphoreType.DMA((X_BUFS,)), pltpu.SemaphoreType.DMA((2,)),
                        pltpu.SemaphoreType.DMA((2,))],
    )
    return pl.pallas_call(
        _experts_kernel,
        grid_spec=grid_spec,
        out_shape=jax.ShapeDtypeStruct((N_ROWS, D_MODEL), BF16),
        input_output_aliases={n_prefetch: 0},
        compiler_params=_cparams(("arbitrary",)),
        name="experts",
    )(tile_expert, tile_segment, next_expert, n_used, xs, w_gate, w_up, w_down)


def _combine_kernel(nb_ref, ns_ref, tab_ref, tab_next_ref, route_ref, x1_ref, g2_ref, ys_ref, o_ref,
                    gbuf_ref, sems):
    i = pl.program_id(0)
    last = pl.num_programs(0) - 1
    buf = lax.rem(i, 2)

    def piece(b):
        def make(local, sorted_row, rows):
            return pltpu.make_async_copy(
                ys_ref.at[pl.ds(pl.multiple_of(sorted_row, RUN_ALIGN), rows)],
                gbuf_ref.at[b, pl.ds(pl.multiple_of(local, RUN_ALIGN), rows)], sems.at[b])
        return make

    @pl.when(i == 0)
    def _():
        gbuf_ref[...] = jnp.zeros_like(gbuf_ref)
        _run_copies(tab_ref, nb_ref[0], ns_ref[0], piece(0), lambda cp: cp.start())

    nxt = jnp.minimum(i + 1, last)

    @pl.when(i < last)
    def _():
        _run_copies(tab_next_ref, nb_ref[nxt], ns_ref[nxt], piece(1 - buf), lambda cp: cp.start())

    rec = route_ref[...]
    slot0 = lax.broadcasted_iota(I32, (OUT_TM, COMBINE_K), 1).astype(F32)
    _run_copies(tab_ref, nb_ref[i], ns_ref[i], lambda lo, so, rows: piece(buf)(0, 0, rows), lambda cp: cp.wait())

    def slice_sum(k0):
        s0, s1 = rec[:, 0:1] - float(k0), rec[:, 1:2] - float(k0)
        weights = (jnp.where(slot0 == s0, rec[:, 2:3], 0.0)
                   + jnp.where(slot0 == s1, rec[:, 3:4], 0.0)).astype(BF16)
        return jnp.dot(weights, gbuf_ref[buf, k0:k0 + COMBINE_K, :], preferred_element_type=F32)

    moe = jnp.zeros((OUT_TM, D_MODEL), F32)
    for k0 in range(0, LOCAL_BODY, COMBINE_K):
        moe = moe + slice_sum(k0)
    o_ref[...] = x1_ref[...] + g2_ref[0] * moe

    @pl.when(_local_rows_used(nb_ref[i], ns_ref[i]) > LOCAL_BODY)
    def _():
        o_ref[...] += g2_ref[0] * slice_sum(LOCAL_BODY)


def _combine(n_big, n_small, piece_table, route, x1, mod3, ys):
    tm = OUT_TM
    steps_per_batch = SEQ // tm
    grid_spec = pltpu.PrefetchScalarGridSpec(
        num_scalar_prefetch=2,
        grid=(N_TOKEN_TILES,),
        in_specs=[_piece_spec(lambda i, nb, ns: (i, 0, 0)),
                  _piece_spec(lambda i, nb, ns: (jnp.minimum(i + 1, N_TOKEN_TILES - 1), 0, 0)),
                  pl.BlockSpec((tm, ROUTE_W), lambda i, nb, ns: (i, 0)),
                  pl.BlockSpec((tm, D_MODEL), lambda i, nb, ns: (i, 0)),
                  pl.BlockSpec((1, 1, D_MODEL), lambda i, nb, ns: ((i // steps_per_batch) * 6 + 5, 0, 0)),
                  pl.BlockSpec(memory_space=pl.ANY)],
        out_specs=pl.BlockSpec((tm, D_MODEL), lambda i, nb, ns: (i, 0)),
        scratch_shapes=[pltpu.VMEM((2, LOCAL_ROWS, D_MODEL), BF16), pltpu.SemaphoreType.DMA((2,))],
    )
    return pl.pallas_call(
        _combine_kernel,
        grid_spec=grid_spec,
        out_shape=jax.ShapeDtypeStruct((TOKENS, D_MODEL), F32),
        compiler_params=_cparams(("arbitrary",)),
        name="combine",
    )(n_big, n_small, piece_table, piece_table, route, x1, mod3, ys)


def kernel(x, c, positions, norm1_w, norm2_w, w_ada, b_ada, w_in, conv_w, conv_b, dt_bias, a_log,
           d_skip, ssd_norm_w, q_norm_w, k_norm_w, sinks, w_out, w_group, b_group, w_expert, b_expert,
           w_gate, w_up, w_down):
    assert x.shape == (BATCH, SEQ, D_MODEL) and w_in.shape == (D_MODEL, IN_WIDTH)
    x2d = x.reshape(TOKENS, D_MODEL)
    cosf, s1, s2, mod = _rope_tables_and_mod(positions, c, w_ada, b_ada)
    mod3 = mod.reshape(BATCH * 6, 1, D_MODEL)

    qkv, z, xbc, dt, dt_t = _in_proj(x2d, norm1_w, mod3, w_in)
    att = _attention(qkv, (cosf, s1, s2), q_norm_w, k_norm_w, sinks)
    y = _ssd(xbc, z, dt, dt_t, conv_w, conv_b, dt_bias, a_log, d_skip, ssd_norm_w)

    w_router = jnp.pad(jnp.concatenate([w_group, w_expert], axis=1).astype(F32),
                       ((0, 0), (0, LANES - ROUTER_COLS)))
    b_router = jnp.pad(jnp.concatenate([b_group, b_expert]).astype(F32),
                       (0, LANES - ROUTER_COLS)).reshape(1, LANES)
    x1, h2, route, route_t, tcnt = _out_router(att, y, x2d, mod3, w_out.astype(F32), norm2_w,
                                                w_router, b_router)

    tc = tcnt[:, 0, 0:N_EXPERTS].astype(I32)
    run_rows = ((tc + RUN_ALIGN - 1) // RUN_ALIGN) * RUN_ALIGN
    counts = jnp.sum(run_rows, axis=0)
    padded = ((counts + MOE_TM - 1) // MOE_TM) * MOE_TM
    seg_end = jnp.cumsum(padded)
    seg_start = seg_end - padded
    run_dst = seg_start[None, :] + jnp.cumsum(run_rows, axis=0) - run_rows
    n_used = (seg_end[-1] // MOE_TM).reshape(1)
    last_row = jnp.minimum(jnp.arange(N_TILES, dtype=I32) * MOE_TM, seg_end[-1] - 1)
    tile_expert = jnp.sum((seg_end[None, :] <= last_row[:, None]).astype(I32), axis=1)

    run_local = jnp.cumsum(run_rows, axis=1) - run_rows
    n_big_run = run_rows // BIG_PIECE
    n_small_run = (run_rows // RUN_ALIGN) % 2
    q = jnp.arange(PIECE_SLOTS, dtype=I32)
    experts = jnp.arange(N_EXPERTS, dtype=I32)

    def flat(per_run, local0, dst0, stride):
        end = jnp.cumsum(per_run, axis=1)
        run_of = jnp.sum((end[:, None, :] <= q[None, :, None]).astype(I32), axis=2)
        pick = (run_of[:, :, None] == experts[None, None, :]).astype(I32)
        k = q[None, :] - jnp.sum(pick * (end - per_run)[:, None, :], axis=2)
        local = jnp.sum(pick * local0[:, None, :], axis=2) + stride * k
        dst = jnp.sum(pick * dst0[:, None, :], axis=2) + stride * k
        return end[:, -1], local, dst

    n_big, big_local, big_dst = flat(n_big_run, run_local, run_dst, BIG_PIECE)
    n_small, small_local, small_dst = flat(n_small_run, run_local + BIG_PIECE * n_big_run,
                                           run_dst + BIG_PIECE * n_big_run, 0)
    piece_table = jnp.concatenate([big_local, big_dst, small_local, small_dst], axis=1)
    piece_table = piece_table.astype(I32).reshape(N_TOKEN_TILES, 1, TABLE_W)

    nonempty = padded > 0
    seg_rank = jnp.cumsum(nonempty.astype(I32)) - 1
    later = nonempty[None, :] & (experts[None, :] > experts[:, None])
    next_of = jnp.min(jnp.where(later, experts[None, :], N_EXPERTS), axis=1)
    next_of = jnp.where(next_of == N_EXPERTS, -1, next_of)
    tile_is = (tile_expert[:, None] == experts[None, :]).astype(I32)
    tile_segment = jnp.sum(tile_is * seg_rank[None, :], axis=1)
    next_expert = jnp.sum(tile_is * next_of[None, :], axis=1)

    n_big, n_small = n_big.astype(I32), n_small.astype(I32)
    xs = _dispatch(seg_end.astype(I32), (seg_start + counts).astype(I32), n_big, n_small, piece_table,
                   route_t, h2)
    ys = _experts(tile_expert, tile_segment.astype(I32), next_expert.astype(I32), n_used.astype(I32),
                  xs, w_gate, w_up, w_down)
    out = _combine(n_big, n_small, piece_table, route, x1, mod3, ys)
    return out.reshape(BATCH, SEQ, D_MODEL)
```

```python
import jax
import jax.numpy as jnp
from jax import lax
from jax.experimental import pallas as pl
from jax.experimental.pallas import tpu as pltpu

F32 = jnp.float32
BF16 = jnp.bfloat16
I32 = jnp.int32

D_MODEL = 1024
BATCH = 2
SEQ = 8192
TOKENS = BATCH * SEQ
ATT_HEADS = 8
ATT_KV_HEADS = 2
HEAD_DIM = 64
ATT_WIDTH = ATT_HEADS * HEAD_DIM
KV_WIDTH = ATT_KV_HEADS * HEAD_DIM
ATT_BLOCK = 128
ROPE_DIM = HEAD_DIM // 4
ROPE_THETA = 500000.0
SSD_HEADS = 8
SSD_HEAD_DIM = 64
SSD_WIDTH = SSD_HEADS * SSD_HEAD_DIM
SSD_GROUPS = 2
SSD_STATE = 128
CONV_K = 4
CHUNK = 128
XBC_WIDTH = SSD_WIDTH + 2 * SSD_GROUPS * SSD_STATE
IN_WIDTH = ATT_WIDTH + 2 * KV_WIDTH + SSD_WIDTH + XBC_WIDTH + SSD_HEADS
N_GROUPS = 4
EXPERTS_PER_GROUP = 8
N_EXPERTS = N_GROUPS * EXPERTS_PER_GROUP
TOP_K = 2
D_EXPERT = 256
EPS = 1e-6

LANES = 128
QKV_WIDTH = ATT_WIDTH + 2 * KV_WIDTH
IN_PAD = QKV_WIDTH + SSD_WIDTH + XBC_WIDTH + LANES
NEG_BIG = -1e30

VMEM_LIMIT = 48 * 1024 * 1024


def _cparams(sem):
    return pltpu.CompilerParams(dimension_semantics=sem, vmem_limit_bytes=VMEM_LIMIT)


def _split_bf16(x):
    hi = x.astype(BF16)
    lo = (x - hi.astype(F32)).astype(BF16)
    return hi, lo


ADA_TN = 768


def _ada_kernel(ct_ref, w_ref, b_ref, o_ref):
    ct = ct_ref[...]
    s = ct * jax.nn.sigmoid(ct)
    w = w_ref[...]
    rows = [jnp.sum(s[:, b:b + 1] * w, axis=0, keepdims=True) for b in range(BATCH)]
    o_ref[...] = jnp.concatenate(rows, axis=0) + b_ref[...]


INPROJ_TM = 1024
_INPROJ_CHUNK = 256


def _inproj_kernel(x_ref, nw_ref, sc_ref, sh_ref, wf_ref, wdt_ref, qkv_ref, z_ref, xbc_ref, dt_ref, dtt_ref,
                   w_ref):
    @pl.when(pl.program_id(0) == 0)
    def _():
        for c0 in range(0, IN_PAD - LANES, _INPROJ_CHUNK):
            w_ref[:, c0:c0 + _INPROJ_CHUNK] = wf_ref[:, c0:c0 + _INPROJ_CHUNK].astype(BF16)
        w_ref[:, IN_PAD - LANES:IN_PAD] = wdt_ref[...].astype(BF16)

    x = x_ref[...]
    y = x * lax.rsqrt(jnp.mean(x * x, axis=-1, keepdims=True) + EPS)
    h = (y * nw_ref[...]) * (1.0 + sc_ref[0]) + sh_ref[0]
    hb = h.astype(BF16)

    def proj(c0, c1):
        return jnp.dot(hb, w_ref[:, c0:c1], preferred_element_type=F32)

    for c0 in range(0, QKV_WIDTH, _INPROJ_CHUNK):
        qkv_ref[:, c0:c0 + _INPROJ_CHUNK] = proj(c0, c0 + _INPROJ_CHUNK).astype(BF16)
    base = QKV_WIDTH
    for c0 in range(0, SSD_WIDTH, _INPROJ_CHUNK):
        z_ref[:, c0:c0 + _INPROJ_CHUNK] = proj(base + c0, base + c0 + _INPROJ_CHUNK).astype(BF16)
    base += SSD_WIDTH
    for c0 in range(0, XBC_WIDTH, _INPROJ_CHUNK):
        xbc_ref[:, c0:c0 + _INPROJ_CHUNK] = proj(base + c0, base + c0 + _INPROJ_CHUNK).astype(BF16)
    base += XBC_WIDTH
    dt = proj(base, base + LANES)
    dt_ref[...] = dt
    dtt_ref[...] = dt.T[0:SSD_HEADS, :]


def _in_proj(x2d, norm_w, mod3, w_in):
    tm = INPROJ_TM
    steps_per_batch = SEQ // tm
    w_dt = jnp.pad(w_in[:, IN_WIDTH - SSD_HEADS:IN_WIDTH].astype(F32), ((0, 0), (0, LANES - SSD_HEADS)))
    return pl.pallas_call(
        _inproj_kernel,
        grid=(TOKENS // tm,),
        in_specs=[pl.BlockSpec((tm, D_MODEL), lambda i: (i, 0)),
                  pl.BlockSpec((1, D_MODEL), lambda i: (0, 0)),
                  pl.BlockSpec((1, 1, D_MODEL), lambda i: ((i // steps_per_batch) * 6 + 1, 0, 0)),
                  pl.BlockSpec((1, 1, D_MODEL), lambda i: ((i // steps_per_batch) * 6 + 0, 0, 0)),
                  pl.BlockSpec((D_MODEL, IN_WIDTH), lambda i: (0, 0), pipeline_mode=pl.Buffered(1)),
                  pl.BlockSpec((D_MODEL, LANES), lambda i: (0, 0))],
        out_specs=[pl.BlockSpec((tm, QKV_WIDTH), lambda i: (i, 0)),
                   pl.BlockSpec((tm, SSD_WIDTH), lambda i: (i, 0)),
                   pl.BlockSpec((tm, XBC_WIDTH), lambda i: (i, 0)),
                   pl.BlockSpec((tm, LANES), lambda i: (i, 0)),
                   pl.BlockSpec((SSD_HEADS, tm), lambda i: (0, i))],
        out_shape=[jax.ShapeDtypeStruct((TOKENS, QKV_WIDTH), BF16),
                   jax.ShapeDtypeStruct((TOKENS, SSD_WIDTH), BF16),
                   jax.ShapeDtypeStruct((TOKENS, XBC_WIDTH), BF16),
                   jax.ShapeDtypeStruct((TOKENS, LANES), F32),
                   jax.ShapeDtypeStruct((SSD_HEADS, TOKENS), F32)],
        scratch_shapes=[pltpu.VMEM((D_MODEL, IN_PAD), BF16)],
        compiler_params=_cparams(("arbitrary",)),
        name="in_proj",
    )(x2d, norm_w.reshape(1, D_MODEL), mod3, mod3, w_in.astype(F32), w_dt)


ATT_SUB = 8


ROPE_TM = 2048
_ROPE_HALF = ROPE_DIM // 2
_TOK_PER_ROW = LANES // _ROPE_HALF


def _exact_dot(x, onehot_b):
    hi, lo = _split_bf16(x)
    return (jnp.dot(hi, onehot_b, preferred_element_type=F32)
            + jnp.dot(lo, onehot_b, preferred_element_type=F32))


def _rope_kernel(pos_ref, freq_ref, sel_ref, own_ref, gcos_ref, gsn_ref, ident_ref,
                 cos_ref, sn_ref):
    ang = pos_ref[...].astype(F32) * freq_ref[...]
    cos_p, sin_p = jnp.cos(ang), jnp.sin(ang)
    hi_c, lo_c = _split_bf16(cos_p)
    hi_s, lo_s = _split_bf16(sin_p)
    sel = sel_ref[...]
    rows_c = jnp.dot(sel, hi_c, preferred_element_type=F32) + jnp.dot(sel, lo_c, preferred_element_type=F32)
    rows_s = jnp.dot(sel, hi_s, preferred_element_type=F32) + jnp.dot(sel, lo_s, preferred_element_type=F32)
    own = own_ref[...]
    cos_ref[...] = _exact_dot(rows_c * own, gcos_ref[...]) + ident_ref[...]
    sn_ref[...] = _exact_dot(rows_s * own, gsn_ref[...])


def _tables_kernel(pos_ref, freq_ref, sel_ref, own_ref, gcos_ref, gsn_ref, ident_ref,
                   ct_ref, w_ref, b_ref, cos_ref, sn_ref, mod_ref):
    _rope_kernel(pos_ref, freq_ref, sel_ref, own_ref, gcos_ref, gsn_ref, ident_ref, cos_ref, sn_ref)
    _ada_kernel(ct_ref, w_ref, b_ref, mod_ref)


def _rope_tables_and_mod(positions, c, w_ada, b_ada):
    n_mod = w_ada.shape[1]
    assert TOKENS // ROPE_TM == n_mod // ADA_TN
    half, per_row = _ROPE_HALF, _TOK_PER_ROW
    rows = ROPE_TM // per_row
    pos_rep = jnp.repeat(positions.reshape(TOKENS).astype(I32), half).reshape(TOKENS // per_row, LANES)
    inv_freq = jnp.power(ROPE_THETA, -jnp.arange(half, dtype=F32) * 2.0 / ROPE_DIM)
    freq = jnp.tile(inv_freq, per_row).reshape(1, LANES)
    tok = jnp.arange(ROPE_TM)
    lane = jnp.arange(LANES)
    sel = (tok[:, None] // per_row == jnp.arange(rows)[None, :]).astype(BF16)
    own = (lane[None, :] // half == tok[:, None] % per_row).astype(F32)
    d = lane % HEAD_DIM
    src_f = lane % half
    hits = lambda lo, hi: ((src_f[:, None] == d[None, :] % half) & (d[None, :] >= lo) & (d[None, :] < hi))
    gcos = hits(0, ROPE_DIM).astype(BF16)
    gsn = hits(half, ROPE_DIM).astype(BF16) - hits(0, half).astype(BF16)
    ident = (d >= ROPE_DIM).astype(F32).reshape(1, LANES)
    const = lambda shape: pl.BlockSpec(shape, lambda i: (0, 0))
    out_spec = pl.BlockSpec((ROPE_TM, LANES), lambda i: (i, 0))
    out = jax.ShapeDtypeStruct((TOKENS, LANES), F32)
    in_specs = [pl.BlockSpec((rows, LANES), lambda i: (i, 0)), const((1, LANES)),
                const((ROPE_TM, rows)), const((ROPE_TM, LANES)),
                const((LANES, LANES)), const((LANES, LANES)), const((1, LANES)),
                const((D_MODEL, BATCH)),
                pl.BlockSpec((D_MODEL, ADA_TN), lambda i: (0, i), pipeline_mode=pl.Buffered(3)),
                pl.BlockSpec((1, ADA_TN), lambda i: (0, i))]
    out_specs = [out_spec, out_spec, pl.BlockSpec((BATCH, ADA_TN), lambda i: (0, i))]

    def pipelined(*refs):
        pltpu.emit_pipeline(_tables_kernel, grid=(TOKENS // ROPE_TM,), in_specs=in_specs,
                            out_specs=out_specs)(*refs)

    anywhere = pl.BlockSpec(memory_space=pl.ANY)
    return pl.pallas_call(
        pipelined,
        in_specs=[anywhere] * len(in_specs),
        out_specs=[anywhere] * len(out_specs),
        out_shape=[out, out, jax.ShapeDtypeStruct((BATCH, n_mod), F32)],
        compiler_params=pltpu.CompilerParams(vmem_limit_bytes=VMEM_LIMIT),
        name="rope_tables_ada_mod",
    )(pos_rep, freq, sel, own, gcos, gsn, ident, c.T, w_ada, b_ada.reshape(1, n_mod))


def _seg_meansq(xf, ones128):
    rows, width = xf.shape
    nt = width // LANES
    parts = _split_bf16(xf * xf)
    stacked = jnp.concatenate([p[:, t * LANES:(t + 1) * LANES] for p in parts for t in range(nt)], axis=0)
    tot = jnp.dot(stacked, ones128, preferred_element_type=F32)
    tiles = [tot[t * rows:(t + 1) * rows] + tot[(nt + t) * rows:(nt + t + 1) * rows] for t in range(nt)]
    return jnp.concatenate(tiles, axis=1) * (1.0 / HEAD_DIM)


def _norm_rope(x_bf, w_row, ones_bd, cosf, sn):
    xf = x_bf.astype(F32)
    width = xf.shape[1]
    xn = xf * lax.rsqrt(_seg_meansq(xf, ones_bd) + EPS) * w_row
    half = ROPE_DIM // 2
    up = pltpu.roll(xn, width - half, axis=1)
    down = pltpu.roll(xn, half, axis=1)
    first = lax.broadcasted_iota(I32, xn.shape, 1) % HEAD_DIM < half
    return xn * cosf + jnp.where(first, up, down) * sn


def _attn_kernel(sink_ref, q_ref, kv_ref, cos_ref, sn_ref, qw_ref, kw_ref,
                 ones_ref, o_ref, kprev_ref, vprev_ref):
    j = pl.program_id(1)
    blk = ATT_BLOCK

    @pl.when(j == 0)
    def _():
        kprev_ref[...] = jnp.zeros_like(kprev_ref)
        vprev_ref[...] = jnp.zeros_like(vprev_ref)

    cos1 = cos_ref[...]
    sn1 = sn_ref[...]
    reps = ATT_WIDTH // LANES
    cosq = jnp.concatenate([cos1] * reps, axis=1)
    snq = jnp.concatenate([sn1] * reps, axis=1)

    q = _norm_rope(q_ref[...], qw_ref[...], ones_ref[...], cosq, snq)
    qf = q * (HEAD_DIM ** -0.5)
    kv = kv_ref[...]
    kn = _norm_rope(kv[:, 0:KV_WIDTH], kw_ref[...], ones_ref[...], cos1, sn1)
    vn = kv[:, KV_WIDTH:2 * KV_WIDTH].astype(F32)

    kall = jnp.concatenate([kprev_ref[...], kn], axis=0)
    vall = jnp.concatenate([vprev_ref[...], vn], axis=0)
    kprev_ref[...] = kn[(ATT_SUB - 1) * blk:ATT_SUB * blk]
    vprev_ref[...] = vn[(ATT_SUB - 1) * blk:ATT_SUB * blk]

    lo_all = lax.broadcasted_iota(I32, kall.shape, 1) < HEAD_DIM
    ones_all = jnp.ones(kall.shape, BF16)

    row = lax.broadcasted_iota(I32, (2 * blk, blk), 0)
    col = lax.broadcasted_iota(I32, (2 * blk, blk), 1)
    from_prev = col > (row & (blk - 1))
    second_tile = lax.broadcasted_iota(I32, (2 * blk, 1), 0) >= blk
    zero_p = jnp.zeros((2 * blk, blk), F32)

    k_par, v_par = [], []
    for g in range(ATT_KV_HEADS):
        keep = lo_all if g == 0 else ~lo_all
        k_own = jnp.where(keep, kall, 0.0)
        v_own = jnp.where(keep, vall, 0.0)
        k_oth = pltpu.roll(k_own, HEAD_DIM, axis=1)
        v_oth = pltpu.roll(v_own, HEAD_DIM, axis=1)
        k_lo, k_hi = (k_own, k_oth) if g == 0 else (k_oth, k_own)
        v_lo, v_hi = (v_own, v_oth) if g == 0 else (v_oth, v_own)
        k_par.append((k_lo.astype(BF16), k_hi.astype(BF16)))
        v_par.append((jnp.concatenate([v_lo.astype(BF16), ones_all], axis=1),
                      jnp.concatenate([v_hi.astype(BF16), ones_all], axis=1)))

    problems = [(g, sub) for g in range(ATT_KV_HEADS) for sub in range(ATT_SUB)]
    scores = []
    for g, sub in problems:
        r0, c0 = sub * blk, g * 2 * LANES
        qcat = jnp.concatenate([qf[r0:r0 + blk, c0:c0 + LANES],
                                qf[r0:r0 + blk, c0 + LANES:c0 + 2 * LANES]], axis=0).astype(BF16)
        kw = jnp.concatenate([k_par[g][0][r0:r0 + 2 * blk], k_par[g][1][r0:r0 + 2 * blk]], axis=0)
        scores.append(lax.dot_general(qcat, kw, (((1,), (1,)), ((), ())),
                                      preferred_element_type=F32))

    weights, rescale = [], []
    for (g, sub), s_all in zip(problems, scores):
        for par in range(2):
            s = s_all[:, par * 2 * blk:(par + 1) * 2 * blk]
            s_prev = s[:, 0:blk]
            if sub == 0:
                s_prev = s_prev + jnp.where(j > 0, 0.0, NEG_BIG)
            s = jnp.where(from_prev, s_prev, s[:, blk:2 * blk])
            h_first = ATT_HEADS // ATT_KV_HEADS * g + par
            sink = jnp.where(second_tile, sink_ref[h_first + 2], sink_ref[h_first])
            m = jnp.maximum(jnp.max(s, axis=-1, keepdims=True), sink)
            p = jnp.exp(s - m)
            weights.append(jnp.concatenate([jnp.where(from_prev, p, zero_p), jnp.where(from_prev, zero_p, p)],
                                           axis=1).astype(BF16))
            rescale.append(jnp.exp(sink - m))

    outs = []
    for idx, (g, sub) in enumerate(problems):
        for par in range(2):
            outs.append(jnp.dot(weights[2 * idx + par], v_par[g][par][sub * blk:(sub + 2) * blk],
                                preferred_element_type=F32))

    for idx, (g, sub) in enumerate(problems):
        r0, c0 = sub * blk, g * 2 * LANES
        pair = None
        for par in range(2):
            o = outs[2 * idx + par]
            part = o[:, 0:LANES] * (1.0 / (o[:, LANES:2 * LANES] + rescale[2 * idx + par]))
            pair = part if pair is None else pair + part
        o_ref[r0:r0 + blk, c0:c0 + LANES] = pair[0:blk].astype(BF16)
        o_ref[r0:r0 + blk, c0 + LANES:c0 + 2 * LANES] = pair[blk:2 * blk].astype(BF16)


def _attention(qkv, rope_tables, q_norm_w, k_norm_w, sinks):
    cosf, sn = rope_tables
    qw =jnp.tile(q_norm_w.astype(F32), ATT_HEADS).reshape(1, ATT_WIDTH)
    kw = jnp.tile(k_norm_w.astype(F32), ATT_KV_HEADS).reshape(1, KV_WIDTH)
    seg = jnp.arange(LANES) // HEAD_DIM
    ones128 = (seg[:, None] == seg[None, :]).astype(BF16)
    const = lambda shape: pl.BlockSpec(shape, lambda b, j, s: (0, 0))
    rows = ATT_SUB * ATT_BLOCK
    nb = SEQ // rows
    tok = lambda width, cb: pl.BlockSpec((rows, width), lambda b, j, s: (b * nb + j, cb))
    grid_spec = pltpu.PrefetchScalarGridSpec(
        num_scalar_prefetch=1,
        grid=(BATCH, nb),
        in_specs=[tok(ATT_WIDTH, 0), tok(2 * KV_WIDTH, 2), tok(LANES, 0), tok(LANES, 0),
                  const((1, ATT_WIDTH)), const((1, KV_WIDTH)), const((LANES, LANES))],
        out_specs=tok(ATT_WIDTH, 0),
        scratch_shapes=[pltpu.VMEM((ATT_BLOCK, KV_WIDTH), F32),
                        pltpu.VMEM((ATT_BLOCK, KV_WIDTH), F32)],
    )
    return pl.pallas_call(
        _attn_kernel,
        grid_spec=grid_spec,
        out_shape=jax.ShapeDtypeStruct((TOKENS, ATT_WIDTH), BF16),
        compiler_params=_cparams(("arbitrary", "arbitrary")),
        name="attention",
    )(sinks.astype(F32), qkv, qkv, cosf, sn, qw, kw, ones128)


SSD_SUB = 4


def _softplus(x):
    return jnp.maximum(x, 0.0) + jnp.log1p(jnp.exp(-jnp.abs(x)))


def _silu(x):
    h = 0.5 * x
    return h + h * jnp.tanh(h)


def _ssd_kernel(xbc_ref, z_ref, dt_ref, dtt_ref, cw_ref, cb_ref, dtb_row_ref, dtb_col_ref,
                alog_row_ref, alog_col_ref, dskip_ref, nw_ref, tril_ref, triu_ref,
                o_ref, conv_ref, state_ref):
    c = pl.program_id(1)
    L = CHUNK
    tail = 8

    @pl.when(c == 0)
    def _():
        conv_ref[0:tail, :] = jnp.zeros((tail, XBC_WIDTH), F32)
        state_ref[...] = jnp.zeros_like(state_ref)

    row = lax.broadcasted_iota(I32, (L, L), 0)
    col = lax.broadcasted_iota(I32, (L, L), 1)
    causal = col <= row
    lane = lax.broadcasted_iota(I32, (L, LANES), 1)
    lo_half = lane < SSD_HEAD_DIM

    prepared = [_ssd_prepare(s * L, xbc_ref, dt_ref, dtt_ref, cw_ref, cb_ref, dtb_row_ref, dtb_col_ref,
                             alog_row_ref, alog_col_ref, tril_ref, triu_ref, conv_ref)
                for s in range(SSD_SUB)]
    for s in range(SSD_SUB):
        _ssd_chunk(s * L, prepared[s], causal, lo_half, z_ref, dskip_ref, nw_ref, o_ref, state_ref)


def _ssd_prepare(r0, xbc_ref, dt_ref, dtt_ref, cw_ref, cb_ref, dtb_row_ref, dtb_col_ref,
                 alog_row_ref, alog_col_ref, tril_ref, triu_ref, conv_ref):
    L = CHUNK
    tail = 8
    xb = xbc_ref[r0:r0 + L, :].astype(F32)
    conv_ref[tail:tail + L, :] = xb
    acc = cb_ref[...] + cw_ref[CONV_K - 1:CONV_K, :] * xb
    for k in range(CONV_K - 1):
        off = tail - (CONV_K - 1) + k
        acc = acc + cw_ref[k:k + 1, :] * conv_ref[off:off + L, :]
    conv_ref[0:tail, :] = xb[L - tail:L, :]
    u = _silu(acc)
    xs = u[:, 0:SSD_WIDTH]
    bmat = u[:, SSD_WIDTH:SSD_WIDTH + SSD_GROUPS * SSD_STATE]
    cmat = u[:, SSD_WIDTH + SSD_GROUPS * SSD_STATE:XBC_WIDTH]

    dt = _softplus(dt_ref[r0:r0 + L, :] + dtb_row_ref[...])
    a = dt * (-jnp.exp(alog_row_ref[...]))
    a_hi, a_lo = _split_bf16(a)
    a_cum = (jnp.dot(tril_ref[...], a_hi, preferred_element_type=F32)
             + jnp.dot(tril_ref[...], a_lo, preferred_element_type=F32))
    dt_t = _softplus(dtt_ref[:, r0:r0 + L] + dtb_col_ref[...])
    a_t = dt_t * (-jnp.exp(alog_col_ref[...]))
    at_hi, at_lo = _split_bf16(a_t)
    a_cum_t = (jnp.dot(at_hi, triu_ref[...], preferred_element_type=F32)
               + jnp.dot(at_lo, triu_ref[...], preferred_element_type=F32))
    a_end_t = a_cum_t[:, L - 1:L]
    return dict(
        xs=xs, bmat=bmat, cmat=cmat, a_cum=a_cum, exp_a_cum=jnp.exp(a_cum),
        shifted_t=a_cum_t - jnp.log(dt_t),
        wst_t=jnp.exp(a_end_t - a_cum_t) * dt_t,
        cdec_t=jnp.exp(a_end_t))


def _ssd_chunk(r0, p, causal, lo_half, z_ref, dskip_ref, nw_ref, o_ref, state_ref):
    L = CHUNK
    xs, bmat, cmat, a_cum, exp_a_cum = p["xs"], p["bmat"], p["cmat"], p["a_cum"], p["exp_a_cum"]
    shifted_t, wst_t, cdec_t = p["shifted_t"], p["wst_t"], p["cdec_t"]
    xs_b = xs.astype(BF16)
    heads_per_group = SSD_HEADS // SSD_GROUPS
    gated = []
    for g in range(SSD_GROUPS):
        b_g = bmat[:, g * SSD_STATE:(g + 1) * SSD_STATE]
        c_g = cmat[:, g * SSD_STATE:(g + 1) * SSD_STATE]
        cb = lax.dot_general(c_g.astype(BF16), b_g.astype(BF16), (((1,), (1,)), ((), ())),
                             preferred_element_type=F32)
        b_gt = b_g.T
        for t in range(heads_per_group // 2):
            tile = g * (heads_per_group // 2) + t
            c0 = tile * LANES
            xs_tile = xs_b[:, c0:c0 + LANES]
            st_tile = state_ref[:, c0:c0 + LANES]
            st_b = st_tile.astype(BF16)
            y_tile = jnp.zeros((L, LANES), F32)
            new_tile = jnp.zeros((SSD_STATE, LANES), F32)
            for e in range(2):
                h = 2 * tile + e
                keep = lo_half if e == 0 else ~lo_half
                colb = jnp.broadcast_to(a_cum[:, h:h + 1], (L, L))
                rowb = shifted_t[h:h + 1, :]
                w_in = cb * jnp.exp(jnp.where(causal, colb - rowb, NEG_BIG))
                w_off = c_g * jnp.broadcast_to(exp_a_cum[:, h:h + 1], (L, L))
                lhs = jnp.concatenate([w_in, w_off], axis=1).astype(BF16)
                rhs = jnp.concatenate([jnp.where(keep, xs_tile, jnp.zeros_like(xs_tile)),
                                       jnp.where(keep, st_b, jnp.zeros_like(st_b))], axis=0)
                y_tile = y_tile + jnp.dot(lhs, rhs, preferred_element_type=F32)
                m_h = (b_gt * wst_t[h:h + 1, :]).astype(BF16)
                new_tile = new_tile + jnp.dot(m_h, jnp.where(keep, xs_tile, jnp.zeros_like(xs_tile)),
                                              preferred_element_type=F32)
            cd = jnp.where(lo_half[0:1, :], cdec_t[2 * tile:2 * tile + 1, :],
                           cdec_t[2 * tile + 1:2 * tile + 2, :])
            state_ref[:, c0:c0 + LANES] = st_tile * cd + new_tile
            y_full = y_tile + dskip_ref[:, c0:c0 + LANES] * xs[:, c0:c0 + LANES]
            gated.append(y_full * _silu(z_ref[r0:r0 + L, c0:c0 + LANES].astype(F32)))

    gw = SSD_WIDTH // SSD_GROUPS
    tiles_per_group = gw // LANES
    for g in range(SSD_GROUPS):
        yg = jnp.concatenate(gated[g * tiles_per_group:(g + 1) * tiles_per_group], axis=1)
        ms = jnp.mean(yg * yg, axis=-1, keepdims=True)
        o_ref[r0:r0 + L, g * gw:(g + 1) * gw] = (
            (yg * lax.rsqrt(ms + EPS)) * nw_ref[:, g * gw:(g + 1) * gw]).astype(o_ref.dtype)


def _ssd(xbc, z, dt, dt_t, conv_w, conv_b, dt_bias, a_log, d_skip, ssd_norm_w):
    L = SSD_SUB * CHUNK
    nc = SEQ // L
    pad_row = lambda v: jnp.pad(v.astype(F32), (0, LANES - SSD_HEADS)).reshape(1, LANES)
    col8 = lambda v: v.astype(F32).reshape(SSD_HEADS, 1)
    idx = jnp.arange(CHUNK)
    tril = (idx[None, :] <= idx[:, None]).astype(BF16)
    triu = (idx[:, None] <= idx[None, :]).astype(BF16)
    dskip = jnp.repeat(d_skip.astype(F32), SSD_HEAD_DIM).reshape(1, SSD_WIDTH)
    const = lambda shape: pl.BlockSpec(shape, lambda b, c: (0, 0))
    tok = lambda width: pl.BlockSpec((L, width), lambda b, c: (b * nc + c, 0))
    return pl.pallas_call(
        _ssd_kernel,
        grid=(BATCH, nc),
        in_specs=[tok(XBC_WIDTH), tok(SSD_WIDTH), tok(LANES),
                  pl.BlockSpec((SSD_HEADS, L), lambda b, c: (0, b * nc + c)),
                  const((CONV_K, XBC_WIDTH)), const((1, XBC_WIDTH)),
                  const((1, LANES)), const((SSD_HEADS, 1)), const((1, LANES)), const((SSD_HEADS, 1)),
                  const((1, SSD_WIDTH)), const((1, SSD_WIDTH)), const((CHUNK, CHUNK)), const((CHUNK, CHUNK))],
        out_specs=tok(SSD_WIDTH),
        out_shape=jax.ShapeDtypeStruct((TOKENS, SSD_WIDTH), BF16),
        scratch_shapes=[pltpu.VMEM((8 + CHUNK, XBC_WIDTH), F32),
                        pltpu.VMEM((SSD_STATE, SSD_WIDTH), F32)],
        compiler_params=_cparams(("arbitrary", "arbitrary")),
        name="ssd",
    )(xbc, z, dt, dt_t, conv_w.astype(F32), conv_b.astype(F32).reshape(1, XBC_WIDTH),
      pad_row(dt_bias), col8(dt_bias), pad_row(a_log), col8(a_log), dskip,
      ssd_norm_w.astype(F32).reshape(1, SSD_WIDTH), tril, triu)


OUT_TM = 512
OR_SUB = 2
ROUTE_W = 8
ROUTER_COLS = N_GROUPS + N_EXPERTS
RUN_ALIGN = 16
RUN_SHIFT = 4
LOCAL_ROWS = 1536
assert RUN_ALIGN == 1 << RUN_SHIFT and LOCAL_ROWS >= TOP_K * OUT_TM + N_EXPERTS * (RUN_ALIGN - 1)


def _lane_pick(values, lane, index):
    return jnp.sum(jnp.where(lane == index, values, 0.0), axis=-1, keepdims=True)


def _first_argmax(vals, lane):
    m = jnp.max(vals, axis=-1, keepdims=True)
    idx = jnp.min(jnp.where(vals == m, lane, float(LANES)), axis=-1, keepdims=True)
    return m, idx


def _out_router_kernel(i, last, att_ref, y_ref, x_ref, g1_ref, wof_ref, nw_ref, sc_ref, sh_ref, wr_ref, br_ref,
                       ltri_ref, sut_ref, x1_ref, h2_ref, route_ref, routet_ref, tcnt_ref,
                       wr_split_ref, logits_ref, wo_ref):
    @pl.when(i == 0)
    def _():
        hi, lo = _split_bf16(wr_ref[...])
        wr_split_ref[:, 0:LANES] = hi
        wr_split_ref[:, LANES:2 * LANES] = lo
        logits_ref[...] = jnp.zeros_like(logits_ref)
        for r0 in range(0, D_MODEL, _INPROJ_CHUNK):
            wo_ref[r0:r0 + _INPROJ_CHUNK, :] = wof_ref[r0:r0 + _INPROJ_CHUNK, :].astype(BF16)

    def route(previous):
        for s in range(OR_SUB):
            _route_tile(previous[s], s, ltri_ref, sut_ref, route_ref, routet_ref, tcnt_ref)

    @pl.when(i == last)
    def _():
        route([logits_ref[s] for s in range(OR_SUB)])

    @pl.when(i < last)
    def _():
        previous = [logits_ref[s] for s in range(OR_SUB)]
        _mix_norm_logits(att_ref, y_ref, x_ref, g1_ref, nw_ref, sc_ref, sh_ref, br_ref,
                         x1_ref, h2_ref, wr_split_ref, logits_ref, wo_ref)
        route(previous)


def _mix_norm_logits(att_ref, y_ref, x_ref, g1_ref, nw_ref, sc_ref, sh_ref, br_ref,
                     x1_ref, h2_ref, wr_split_ref, logits_ref, wo_ref):
    tm = OUT_TM
    for s in range(OR_SUB):
        rows = slice(s * tm, (s + 1) * tm)
        mixer = (jnp.dot(att_ref[rows, :], wo_ref[0:ATT_WIDTH, :], preferred_element_type=F32)
                 + jnp.dot(y_ref[rows, :], wo_ref[ATT_WIDTH:ATT_WIDTH + SSD_WIDTH, :], preferred_element_type=F32))
        x1 = x_ref[rows, :] + g1_ref[0] * mixer
        x1_ref[rows, :] = x1
        yn = x1 * lax.rsqrt(jnp.mean(x1 * x1, axis=-1, keepdims=True) + EPS)
        h2 = (yn * nw_ref[...]) * (1.0 + sc_ref[0]) + sh_ref[0]
        h2_ref[rows, :] = h2.astype(BF16)

        h_hi, h_lo = _split_bf16(h2)
        both = jnp.dot(h_hi, wr_split_ref[...], preferred_element_type=F32)
        logits_ref[s] = (both[:, 0:LANES] + both[:, LANES:2 * LANES]
                         + jnp.dot(h_lo, wr_split_ref[:, 0:LANES], preferred_element_type=F32)) + br_ref[...]


def _route_tile(logits, s, ltri_ref, sut_ref, route_ref, routet_ref, tcnt_ref):
    tm = logits.shape[0]
    lane = lax.broadcasted_iota(I32, (tm, LANES), 1).astype(F32)

    gl = jnp.where(lane < N_GROUPS, logits, NEG_BIG)
    gmax, gidx = _first_argmax(gl, lane)
    g_p = 1.0 / jnp.sum(jnp.exp(gl - gmax), axis=-1, keepdims=True)

    lo_lane = N_GROUPS + EXPERTS_PER_GROUP * gidx
    el = jnp.where((lane >= lo_lane) & (lane < lo_lane + EXPERTS_PER_GROUP), logits, NEG_BIG)
    m1, i1 = _first_argmax(el, lane)
    m2, i2 = _first_argmax(jnp.where(lane == i1, NEG_BIG, el), lane)
    r = jnp.exp(m2 - m1)
    p1 = 1.0 / (1.0 + r)
    p2 = r / (1.0 + r)
    e0 = i1 - N_GROUPS
    e1 = i2 - N_GROUPS

    onehot = ((lane == e0) | (lane == e1)).astype(F32)
    tile_cnt = jnp.sum(onehot, axis=0, keepdims=True)
    run_len = jnp.floor((tile_cnt + (RUN_ALIGN - 1)) * (1.0 / RUN_ALIGN)) * RUN_ALIGN
    run_start = jnp.dot(jnp.broadcast_to(run_len, (8, LANES)).astype(BF16), sut_ref[...],
                        preferred_element_type=F32)[0:1, :]
    before = jnp.dot(ltri_ref[...], onehot.astype(BF16), preferred_element_type=F32) + run_start
    slot0 = _lane_pick(before, lane, e0)
    slot1 = _lane_pick(before, lane, e1)
    tcnt_ref[s] = tile_cnt

    rec = jnp.zeros((tm, LANES), F32)
    for k, v in enumerate([slot0, slot1, g_p * p1, g_p * p2, e0, e1]):
        rec = jnp.where(lane == k, v, rec)
    route_ref[s * tm:(s + 1) * tm, :] = rec[:, 0:ROUTE_W]
    routet_ref[s * ROUTE_W:(s + 1) * ROUTE_W, :] = rec.T[0:ROUTE_W, :]


def _out_router(att, y, x2d, mod3, w_out_b, norm_w, w_router, b_router):
    tm = OUT_TM
    rows = OR_SUB * tm
    n_steps = TOKENS // rows
    steps_per_batch = SEQ // rows
    idx = jnp.arange(tm)
    ltri = (idx[None, :] < idx[:, None]).astype(BF16)
    lidx = jnp.arange(LANES)
    sut = (lidx[:, None] < lidx[None, :]).astype(BF16)
    const = lambda shape: pl.BlockSpec(shape, lambda i: (0, 0))
    cur = lambda i: jnp.minimum(i, n_steps - 1)
    prev = lambda i: jnp.maximum(i - 1, 0)
    tok = lambda width: pl.BlockSpec((rows, width), lambda i: (cur(i), 0))
    modspec = lambda k: pl.BlockSpec((1, 1, D_MODEL), lambda i: ((cur(i) // steps_per_batch) * 6 + k, 0, 0))
    x_spec = pl.BlockSpec((rows, D_MODEL), lambda i: (cur(i), 0), pipeline_mode=pl.Buffered(3))
    in_specs = [tok(ATT_WIDTH), tok(SSD_WIDTH), x_spec, modspec(2),
                const((D_MODEL, D_MODEL)), const((1, D_MODEL)), modspec(4), modspec(3),
                const((D_MODEL, LANES)), const((1, LANES)), const((tm, tm)), const((LANES, LANES))]
    out_specs = [tok(D_MODEL), tok(D_MODEL),
                 pl.BlockSpec((rows, ROUTE_W), lambda i: (prev(i), 0)),
                 pl.BlockSpec((OR_SUB * ROUTE_W, tm), lambda i: (prev(i), 0)),
                 pl.BlockSpec((OR_SUB, 1, LANES), lambda i: (prev(i), 0, 0))]
    n_hbm = len(in_specs) + len(out_specs)

    def pipelined(*refs):
        hbm, scratch, step_ref = refs[:n_hbm], refs[n_hbm:-1], refs[-1]
        step_ref[0] = 0

        def step(*blocks):
            i = step_ref[0]
            _out_router_kernel(i, n_steps, *blocks, *scratch)
            step_ref[0] = i + 1

        pltpu.emit_pipeline(step, grid=(n_steps + 1,), in_specs=in_specs, out_specs=out_specs)(*hbm)

    anywhere = pl.BlockSpec(memory_space=pl.ANY)
    return pl.pallas_call(
        pipelined,
        in_specs=[anywhere] * len(in_specs),
        out_specs=[anywhere] * len(out_specs),
        out_shape=[jax.ShapeDtypeStruct((TOKENS, D_MODEL), F32),
                   jax.ShapeDtypeStruct((TOKENS, D_MODEL), BF16),
                   jax.ShapeDtypeStruct((TOKENS, ROUTE_W), F32),
                   jax.ShapeDtypeStruct((N_TOKEN_TILES * ROUTE_W, tm), F32),
                   jax.ShapeDtypeStruct((N_TOKEN_TILES, 1, LANES), F32)],
        scratch_shapes=[pltpu.VMEM((D_MODEL, 2 * LANES), BF16), pltpu.VMEM((OR_SUB, tm, LANES), F32),
                        pltpu.VMEM((D_MODEL, D_MODEL), BF16), pltpu.SMEM((1,), I32)],
        compiler_params=pltpu.CompilerParams(vmem_limit_bytes=VMEM_LIMIT),
        name="out_router",
    )(att, y, x2d, mod3, w_out_b, norm_w.reshape(1, D_MODEL), mod3, mod3, w_router, b_router, ltri, sut)


MOE_TM = 512
ZERO_ROWS = 256
N_TOKEN_TILES = TOKENS // OUT_TM
MAX_SORTED_ROWS = TOKENS * TOP_K + N_TOKEN_TILES * N_EXPERTS * (RUN_ALIGN - 1)
N_TILES = MAX_SORTED_ROWS // MOE_TM + N_EXPERTS
N_ROWS = N_TILES * MOE_TM
assert MOE_TM % ZERO_ROWS == 0


BIG_PIECE = 2 * RUN_ALIGN
PIECE_SLOTS = LOCAL_ROWS // BIG_PIECE
TABLE_W = 4 * PIECE_SLOTS
COMBINE_K = 256
LOCAL_TAIL = COMBINE_K
LOCAL_BODY = LOCAL_ROWS - LOCAL_TAIL
assert PIECE_SLOTS >= N_EXPERTS and LOCAL_ROWS % COMBINE_K == 0


def _local_rows_used(n_big, n_small):
    return n_big * BIG_PIECE + n_small * RUN_ALIGN


def _run_copies(table_ref, n_big, n_small, make_copy, action):
    def big(q, carry):
        action(make_copy(table_ref[0, 0, q], table_ref[0, 0, PIECE_SLOTS + q], BIG_PIECE))
        return carry

    def small(q, carry):
        action(make_copy(table_ref[0, 0, 2 * PIECE_SLOTS + q], table_ref[0, 0, 3 * PIECE_SLOTS + q], RUN_ALIGN))
        return carry

    lax.fori_loop(0, n_big, big, 0)
    lax.fori_loop(0, n_small, small, 0)


def _dispatch_kernel(seg_end_ref, used_end_ref, nb_ref, ns_ref, tab_ref, routet_ref, h2_ref, xs_ref,
                     sbuf_ref, zero_ref, sems, zsem):
    i = pl.program_id(0)
    last = pl.num_programs(0) - 1
    buf = lax.rem(i, 2)

    def zero_fills(action):
        def tail_copy(row):
            return pltpu.make_async_copy(zero_ref.at[pl.ds(0, RUN_ALIGN)],
                                         xs_ref.at[pl.ds(pl.multiple_of(row, RUN_ALIGN), RUN_ALIGN)], zsem)

        def block_copy(block):
            start = pl.multiple_of(block * ZERO_ROWS, ZERO_ROWS)
            return pltpu.make_async_copy(zero_ref, xs_ref.at[pl.ds(start, ZERO_ROWS)], zsem)

        def tails(e, carry):
            def body(r, c):
                action(tail_copy(r * RUN_ALIGN))
                return c

            lax.fori_loop(used_end_ref[e] // RUN_ALIGN, seg_end_ref[e] // RUN_ALIGN, body, 0)
            return carry

        def blocks(block, carry):
            action(block_copy(block))
            return carry

        lax.fori_loop(0, N_EXPERTS, tails, 0)
        lax.fori_loop(seg_end_ref[N_EXPERTS - 1] // ZERO_ROWS, N_ROWS // ZERO_ROWS, blocks, 0)

    @pl.when(i == 0)
    def _():
        zero_ref[...] = jnp.zeros_like(zero_ref)
        zero_fills(lambda cp: cp.start())

    def sort_rows(r0, rows):
        slot = (lax.broadcasted_iota(I32, (rows, OUT_TM), 0) + r0).astype(F32)
        perm = jnp.where((slot == routet_ref[0:1, :]) | (slot == routet_ref[1:2, :]), 1.0, 0.0).astype(BF16)
        sbuf_ref[buf, r0:r0 + rows, :] = jnp.dot(perm, h2_ref[...], preferred_element_type=F32).astype(BF16)

    sort_rows(0, LOCAL_BODY)

    @pl.when(_local_rows_used(nb_ref[i], ns_ref[i]) > LOCAL_BODY)
    def _():
        sort_rows(LOCAL_BODY, LOCAL_TAIL)

    def piece(b):
        def make(local, sorted_row, rows):
            return pltpu.make_async_copy(
                sbuf_ref.at[b, pl.ds(pl.multiple_of(local, RUN_ALIGN), rows)],
                xs_ref.at[pl.ds(pl.multiple_of(sorted_row, RUN_ALIGN), rows)], sems.at[b])
        return make

    _run_copies(tab_ref, nb_ref[i], ns_ref[i], piece(buf), lambda cp: cp.start())
    prev = jnp.maximum(i - 1, 0)

    @pl.when(i > 0)
    def _():
        _run_copies(tab_ref, nb_ref[prev], ns_ref[prev], lambda lo, so, rows: piece(1 - buf)(0, 0, rows),
                    lambda cp: cp.wait())

    @pl.when(i == last)
    def _():
        _run_copies(tab_ref, nb_ref[i], ns_ref[i], lambda lo, so, rows: piece(buf)(0, 0, rows),
                    lambda cp: cp.wait())
        zero_fills(lambda cp: cp.wait())


def _piece_spec(index_map):
    return pl.BlockSpec((1, 1, TABLE_W), index_map, memory_space=pltpu.SMEM)


def _dispatch(seg_end, used_end, n_big, n_small, piece_table, route_t, h2):
    grid_spec = pltpu.PrefetchScalarGridSpec(
        num_scalar_prefetch=4,
        grid=(N_TOKEN_TILES,),
        in_specs=[_piece_spec(lambda i, se, ue, nb, ns: (i, 0, 0)),
                  pl.BlockSpec((ROUTE_W, OUT_TM), lambda i, se, ue, nb, ns: (i, 0)),
                  pl.BlockSpec((OUT_TM, D_MODEL), lambda i, se, ue, nb, ns: (i, 0))],
        out_specs=pl.BlockSpec(memory_space=pl.ANY),
        scratch_shapes=[pltpu.VMEM((2, LOCAL_ROWS, D_MODEL), BF16),
                        pltpu.VMEM((ZERO_ROWS, D_MODEL), BF16),
                        pltpu.SemaphoreType.DMA((2,)), pltpu.SemaphoreType.DMA],
    )
    return pl.pallas_call(
        _dispatch_kernel,
        grid_spec=grid_spec,
        out_shape=jax.ShapeDtypeStruct((N_ROWS, D_MODEL), BF16),
        compiler_params=_cparams(("arbitrary",)),
        name="dispatch",
    )(seg_end, used_end, n_big, n_small, piece_table, route_t, h2)


X_BUFS = 3


def _experts_kernel(te_ref, seg_ref, nxt_ref, nu_ref, xs_hbm, wg_hbm, wu_hbm, wd_hbm, ys_hbm,
                    xbuf, ybuf, wg_buf, wu_buf, wd_buf, wgu_b_ref, wd_b_ref, xsem, ysem, wsem):
    n = nu_ref[0]

    def rows(t):
        return pl.ds(pl.multiple_of(t * MOE_TM, MOE_TM), MOE_TM)

    def x_copy(t, s):
        return pltpu.make_async_copy(xs_hbm.at[rows(t)], xbuf.at[s], xsem.at[s])

    def y_copy(t, s):
        return pltpu.make_async_copy(ybuf.at[s], ys_hbm.at[rows(t)], ysem.at[s])

    def weight_copies(expert, s):
        return [pltpu.make_async_copy(wg_hbm.at[expert], wg_buf.at[s], wsem.at[s]),
                pltpu.make_async_copy(wu_hbm.at[expert], wu_buf.at[s], wsem.at[s]),
                pltpu.make_async_copy(wd_hbm.at[expert], wd_buf.at[s], wsem.at[s])]

    for cp in weight_copies(te_ref[0], 0):
        cp.start()
    for t in range(X_BUFS - 1):
        @pl.when(t < n)
        def _():
            x_copy(t, t).start()

    def tile(i, carry):
        xs_slot = lax.rem(i, X_BUFS)
        ys_slot = lax.rem(i, 2)
        w_slot = lax.rem(seg_ref[i], 2)
        x_copy(i, xs_slot).wait()
        ahead = i + (X_BUFS - 1)

        @pl.when(ahead < n)
        def _():
            x_copy(ahead, lax.rem(ahead, X_BUFS)).start()

        @pl.when((i == 0) | (te_ref[i] != te_ref[jnp.maximum(i - 1, 0)]))
        def _():
            for cp in weight_copies(te_ref[i], w_slot):
                cp.wait()

            @pl.when(nxt_ref[i] >= 0)
            def _():
                for cp in weight_copies(nxt_ref[i], 1 - w_slot):
                    cp.start()

            wgu_b_ref[:, 0:D_EXPERT] = wg_buf[w_slot].astype(BF16)
            wgu_b_ref[:, D_EXPERT:2 * D_EXPERT] = wu_buf[w_slot].astype(BF16)
            wd_b_ref[...] = wd_buf[w_slot].astype(BF16)

        @pl.when(i >= 2)
        def _():
            y_copy(i - 2, ys_slot).wait()

        h = jnp.dot(xbuf[xs_slot], wgu_b_ref[...], preferred_element_type=F32)
        act = (_silu(h[:, 0:D_EXPERT]) * h[:, D_EXPERT:2 * D_EXPERT]).astype(BF16)
        ybuf[ys_slot] = jnp.dot(act, wd_b_ref[...], preferred_element_type=F32).astype(BF16)
        y_copy(i, ys_slot).start()
        return carry

    lax.fori_loop(0, n, tile, 0)

    @pl.when(n >= 2)
    def _():
        y_copy(n - 2, lax.rem(n - 2, 2)).wait()

    y_copy(n - 1, lax.rem(n - 1, 2)).wait()


def _experts(tile_expert, tile_segment, next_expert, n_used, xs, w_gate, w_up, w_down):
    n_prefetch = 4
    anywhere = pl.BlockSpec(memory_space=pl.ANY)
    grid_spec = pltpu.PrefetchScalarGridSpec(
        num_scalar_prefetch=n_prefetch,
        grid=(1,),
        in_specs=[anywhere, anywhere, anywhere, anywhere],
        out_specs=anywhere,
        scratch_shapes=[pltpu.VMEM((X_BUFS, MOE_TM, D_MODEL), BF16), pltpu.VMEM((2, MOE_TM, D_MODEL), BF16),
                        pltpu.VMEM((2, D_MODEL, D_EXPERT), F32), pltpu.VMEM((2, D_MODEL, D_EXPERT), F32),
                        pltpu.VMEM((2, D_EXPERT, D_MODEL), F32),
                        pltpu.VMEM((D_MODEL, 2 * D_EXPERT), BF16), pltpu.VMEM((D_EXPERT, D_MODEL), BF16),
                        pltpu.SemaphoreType.DMA((X_BUFS,)), pltpu.SemaphoreType.DMA((2,)),
                        pltpu.SemaphoreType.DMA((2,))],
    )
    return pl.pallas_call(
        _experts_kernel,
        grid_spec=grid_spec,
        out_shape=jax.ShapeDtypeStruct((N_ROWS, D_MODEL), BF16),
        input_output_aliases={n_prefetch: 0},
        compiler_params=_cparams(("arbitrary",)),
        name="experts",
    )(tile_expert, tile_segment, next_expert, n_used, xs, w_gate, w_up, w_down)


def _combine_kernel(nb_ref, ns_ref, tab_ref, tab_next_ref, route_ref, x1_ref, g2_ref, ys_ref, o_ref,
                    gbuf_ref, sems):
    i = pl.program_id(0)
    last = pl.num_programs(0) - 1
    buf = lax.rem(i, 2)

    def piece(b):
        def make(local, sorted_row, rows):
            return pltpu.make_async_copy(
                ys_ref.at[pl.ds(pl.multiple_of(sorted_row, RUN_ALIGN), rows)],
                gbuf_ref.at[b, pl.ds(pl.multiple_of(local, RUN_ALIGN), rows)], sems.at[b])
        return make

    @pl.when(i == 0)
    def _():
        gbuf_ref[...] = jnp.zeros_like(gbuf_ref)
        _run_copies(tab_ref, nb_ref[0], ns_ref[0], piece(0), lambda cp: cp.start())

    nxt = jnp.minimum(i + 1, last)

    @pl.when(i < last)
    def _():
        _run_copies(tab_next_ref, nb_ref[nxt], ns_ref[nxt], piece(1 - buf), lambda cp: cp.start())

    rec = route_ref[...]
    slot0 = lax.broadcasted_iota(I32, (OUT_TM, COMBINE_K), 1).astype(F32)
    _run_copies(tab_ref, nb_ref[i], ns_ref[i], lambda lo, so, rows: piece(buf)(0, 0, rows), lambda cp: cp.wait())

    def slice_sum(k0):
        s0, s1 = rec[:, 0:1] - float(k0), rec[:, 1:2] - float(k0)
        weights = (jnp.where(slot0 == s0, rec[:, 2:3], 0.0)
                   + jnp.where(slot0 == s1, rec[:, 3:4], 0.0)).astype(BF16)
        return jnp.dot(weights, gbuf_ref[buf, k0:k0 + COMBINE_K, :], preferred_element_type=F32)

    moe = jnp.zeros((OUT_TM, D_MODEL), F32)
    for k0 in range(0, LOCAL_BODY, COMBINE_K):
        moe = moe + slice_sum(k0)
    o_ref[...] = x1_ref[...] + g2_ref[0] * moe

    @pl.when(_local_rows_used(nb_ref[i], ns_ref[i]) > LOCAL_BODY)
    def _():
        o_ref[...] += g2_ref[0] * slice_sum(LOCAL_BODY)


def _combine(n_big, n_small, piece_table, route, x1, mod3, ys):
    tm = OUT_TM
    steps_per_batch = SEQ // tm
    grid_spec = pltpu.PrefetchScalarGridSpec(
        num_scalar_prefetch=2,
        grid=(N_TOKEN_TILES,),
        in_specs=[_piece_spec(lambda i, nb, ns: (i, 0, 0)),
                  _piece_spec(lambda i, nb, ns: (jnp.minimum(i + 1, N_TOKEN_TILES - 1), 0, 0)),
                  pl.BlockSpec((tm, ROUTE_W), lambda i, nb, ns: (i, 0)),
                  pl.BlockSpec((tm, D_MODEL), lambda i, nb, ns: (i, 0)),
                  pl.BlockSpec((1, 1, D_MODEL), lambda i, nb, ns: ((i // steps_per_batch) * 6 + 5, 0, 0)),
                  pl.BlockSpec(memory_space=pl.ANY)],
        out_specs=pl.BlockSpec((tm, D_MODEL), lambda i, nb, ns: (i, 0)),
        scratch_shapes=[pltpu.VMEM((2, LOCAL_ROWS, D_MODEL), BF16), pltpu.SemaphoreType.DMA((2,))],
    )
    return pl.pallas_call(
        _combine_kernel,
        grid_spec=grid_spec,
        out_shape=jax.ShapeDtypeStruct((TOKENS, D_MODEL), F32),
        compiler_params=_cparams(("arbitrary",)),
        name="combine",
    )(n_big, n_small, piece_table, piece_table, route, x1, mod3, ys)


def kernel(x, c, positions, norm1_w, norm2_w, w_ada, b_ada, w_in, conv_w, conv_b, dt_bias, a_log,
           d_skip, ssd_norm_w, q_norm_w, k_norm_w, sinks, w_out, w_group, b_group, w_expert, b_expert,
           w_gate, w_up, w_down):
    assert x.shape == (BATCH, SEQ, D_MODEL) and w_in.shape == (D_MODEL, IN_WIDTH)
    x2d = x.reshape(TOKENS, D_MODEL)
    cosf, sn, mod = _rope_tables_and_mod(positions, c, w_ada, b_ada)
    mod3 = mod.reshape(BATCH * 6, 1, D_MODEL)

    qkv, z, xbc, dt, dt_t = _in_proj(x2d, norm1_w, mod3, w_in)
    att = _attention(qkv, (cosf, sn), q_norm_w, k_norm_w, sinks)
    y = _ssd(xbc, z, dt, dt_t, conv_w, conv_b, dt_bias, a_log, d_skip, ssd_norm_w)

    w_router = jnp.pad(jnp.concatenate([w_group, w_expert], axis=1).astype(F32),
                       ((0, 0), (0, LANES - ROUTER_COLS)))
    b_router = jnp.pad(jnp.concatenate([b_group, b_expert]).astype(F32),
                       (0, LANES - ROUTER_COLS)).reshape(1, LANES)
    x1, h2, route, route_t, tcnt = _out_router(att, y, x2d, mod3, w_out.astype(F32), norm2_w,
                                                w_router, b_router)

    tc = tcnt[:, 0, 0:N_EXPERTS].astype(I32)
    run_rows = ((tc + RUN_ALIGN - 1) // RUN_ALIGN) * RUN_ALIGN
    counts = jnp.sum(run_rows, axis=0)
    padded = ((counts + MOE_TM - 1) // MOE_TM) * MOE_TM
    seg_end = jnp.cumsum(padded)
    seg_start = seg_end - padded
    run_dst = seg_start[None, :] + jnp.cumsum(run_rows, axis=0) - run_rows
    n_used = (seg_end[-1] // MOE_TM).reshape(1)
    last_row = jnp.minimum(jnp.arange(N_TILES, dtype=I32) * MOE_TM, seg_end[-1] - 1)
    tile_expert = jnp.sum((seg_end[None, :] <= last_row[:, None]).astype(I32), axis=1)

    run_local = jnp.cumsum(run_rows, axis=1) - run_rows
    n_big_run = run_rows // BIG_PIECE
    n_small_run = (run_rows // RUN_ALIGN) % 2
    q = jnp.arange(PIECE_SLOTS, dtype=I32)
    experts = jnp.arange(N_EXPERTS, dtype=I32)

    def flat(per_run, local0, dst0, stride):
        end = jnp.cumsum(per_run, axis=1)
        run_of = jnp.sum((end[:, None, :] <= q[None, :, None]).astype(I32), axis=2)
        pick = (run_of[:, :, None] == experts[None, None, :]).astype(I32)
        k = q[None, :] - jnp.sum(pick * (end - per_run)[:, None, :], axis=2)
        local = jnp.sum(pick * local0[:, None, :], axis=2) + stride * k
        dst = jnp.sum(pick * dst0[:, None, :], axis=2) + stride * k
        return end[:, -1], local, dst

    n_big, big_local, big_dst = flat(n_big_run, run_local, run_dst, BIG_PIECE)
    n_small, small_local, small_dst = flat(n_small_run, run_local + BIG_PIECE * n_big_run,
                                           run_dst + BIG_PIECE * n_big_run, 0)
    piece_table = jnp.concatenate([big_local, big_dst, small_local, small_dst], axis=1)
    piece_table = piece_table.astype(I32).reshape(N_TOKEN_TILES, 1, TABLE_W)

    nonempty = padded > 0
    seg_rank = jnp.cumsum(nonempty.astype(I32)) - 1
    later = nonempty[None, :] & (experts[None, :] > experts[:, None])
    next_of = jnp.min(jnp.where(later, experts[None, :], N_EXPERTS), axis=1)
    next_of = jnp.where(next_of == N_EXPERTS, -1, next_of)
    tile_is = (tile_expert[:, None] == experts[None, :]).astype(I32)
    tile_segment = jnp.sum(tile_is * seg_rank[None, :], axis=1)
    next_expert = jnp.sum(tile_is * next_of[None, :], axis=1)

    n_big, n_small = n_big.astype(I32), n_small.astype(I32)
    xs = _dispatch(seg_end.astype(I32), (seg_start + counts).astype(I32), n_big, n_small, piece_table,
                   route_t, h2)
    ys = _experts(tile_expert, tile_segment.astype(I32), next_expert.astype(I32), n_used.astype(I32),
                  xs, w_gate, w_up, w_down)
    out = _combine(n_big, n_small, piece_table, route, x1, mod3, ys)
    return out.reshape(BATCH, SEQ, D_MODEL)
```
